```python
import jax, jax.numpy as jnp
from jax import lax
import numpy as np

D_MODEL = 1024
BATCH = 8
SEQ = 16384
DEPTH = 4

N_MIXERS = 2
N_HEADS = 16
HEAD_DIM = D_MODEL // N_HEADS
Q_BLOCK = 128
SGU_CHUNK = 128
SGU_WIDTH = 2 * D_MODEL
SGU_GROUPS = 16
SGU_GROUP_DIM = SGU_WIDTH // SGU_GROUPS
FFN_HIDDEN = ((8 * D_MODEL // 3 + 255) // 256) * 256
N_ATTN_LAYERS = (DEPTH + 1) // 2
N_SGU_LAYERS = DEPTH // 2
NORM_EPS = 1e-6
LN_EPS = 1e-5

kernel_name = "fox_gmlp_interleaved_hybrid"


def rms_norm(x, w):
    xf = x.astype(jnp.float32)
    y = xf * lax.rsqrt(jnp.mean(xf * xf, axis=-1, keepdims=True) + NORM_EPS)
    return (y * w.astype(jnp.float32)).astype(x.dtype)


def forgetting_attention(h, w_in, b_f, w_out):
    B, S, _ = h.shape
    proj = h @ w_in
    q, k, v, f_logit = jnp.split(proj, [D_MODEL, 2 * D_MODEL, 3 * D_MODEL], axis=-1)
    q = q.reshape(B, S, N_HEADS, HEAD_DIM).transpose(0, 2, 1, 3)
    k = k.reshape(B, S, N_HEADS, HEAD_DIM).transpose(0, 2, 1, 3)
    v = v.reshape(B, S, N_HEADS, HEAD_DIM).transpose(0, 2, 1, 3)
    log_f = jax.nn.log_sigmoid((f_logit + b_f).astype(jnp.float32))
    c = jnp.cumsum(log_f, axis=1).transpose(0, 2, 1)
    n_blk = S // Q_BLOCK
    q_blk = q.reshape(B, N_HEADS, n_blk, Q_BLOCK, HEAD_DIM).transpose(2, 0, 1, 3, 4)
    c_blk = c.reshape(B, N_HEADS, n_blk, Q_BLOCK).transpose(2, 0, 1, 3)
    k_pos = jnp.arange(S)
    scale = HEAD_DIM ** -0.5

    def attend(args):
        qb, cb, start = args
        s = jnp.einsum('bhqd,bhkd->bhqk', qb, k, preferred_element_type=jnp.float32) * scale
        s = s + cb[..., :, None] - c[:, :, None, :]
        q_pos = start + jnp.arange(Q_BLOCK)
        s = jnp.where(k_pos[None, :] <= q_pos[:, None], s, -jnp.inf)
        p = jax.nn.softmax(s, axis=-1).astype(v.dtype)
        return jnp.einsum('bhqk,bhkd->bhqd', p, v)

    o = lax.map(attend, (q_blk, c_blk, jnp.arange(n_blk) * Q_BLOCK))
    o = o.transpose(1, 0, 3, 2, 4).reshape(B, S, D_MODEL)
    return o @ w_out


def spatial_gating_mixer(h, w_in, ln_g, ln_b, w_s, b_s, w_out):
    B, S, _ = h.shape
    z = jax.nn.gelu(h @ w_in, approximate=False)
    u, v = jnp.split(z, 2, axis=-1)
    vf = v.astype(jnp.float32)
    mu = jnp.mean(vf, axis=-1, keepdims=True)
    var = jnp.mean(jnp.square(vf - mu), axis=-1, keepdims=True)
    vn = ((vf - mu) * lax.rsqrt(var + LN_EPS) * ln_g.astype(jnp.float32)
          + ln_b.astype(jnp.float32)).astype(v.dtype)
    vc = vn.reshape(B, S // SGU_CHUNK, SGU_CHUNK, SGU_GROUPS, SGU_GROUP_DIM)
    w_causal = jnp.tril(w_s)
    mixed = jnp.einsum('gts,bcsgd->bctgd', w_causal, vc) + b_s.T[:, :, None]
    gated = u * mixed.reshape(B, S, SGU_WIDTH)
    return gated @ w_out


def swiglu_ffn(h, w_in, w_out):
    g, u = jnp.split(h @ w_in, 2, axis=-1)
    return (jax.nn.silu(g) * u) @ w_out


def _fwd_setup_inputs(seed: int = 0) -> dict:
    key = jax.random.key(seed)
    ks = jax.random.split(key, 20)
    f32 = jnp.float32
    D = D_MODEL

    def nrm(k, shape, fan_in):
        return jax.random.normal(k, shape, f32) * (fan_in ** -0.5)

    def gain(k, shape):
        return 1.0 + 0.05 * jax.random.normal(k, shape, f32)

    x = jax.random.normal(ks[0], (BATCH, SEQ, D), f32)
    mixer_norm_w = gain(ks[1], (DEPTH, D))
    attn_w_in = nrm(ks[2], (N_ATTN_LAYERS, D, 3 * D + N_HEADS), D)
    attn_b_f = jax.random.uniform(ks[3], (N_ATTN_LAYERS, N_HEADS), f32, 1.0, 6.0)
    attn_w_out = nrm(ks[4], (N_ATTN_LAYERS, D, D), D)
    sgu_w_in = nrm(ks[5], (N_SGU_LAYERS, D, 2 * SGU_WIDTH), D)
    sgu_ln_g = gain(ks[6], (N_SGU_LAYERS, SGU_WIDTH))
    sgu_ln_b = 0.02 * jax.random.normal(ks[7], (N_SGU_LAYERS, SGU_WIDTH), f32)
    sgu_w_s = nrm(ks[8], (N_SGU_LAYERS, SGU_GROUPS, SGU_CHUNK, SGU_CHUNK), SGU_CHUNK)
    sgu_b_s = 1.0 + 0.1 * jax.random.normal(ks[9], (N_SGU_LAYERS, SGU_GROUPS, SGU_CHUNK), f32)
    sgu_w_out = nrm(ks[10], (N_SGU_LAYERS, SGU_WIDTH, D), SGU_WIDTH)
    ffn_norm_w = gain(ks[11], (DEPTH, D))
    ffn_w_in = nrm(ks[12], (DEPTH, D, 2 * FFN_HIDDEN), D)
    ffn_w_out = nrm(ks[13], (DEPTH, FFN_HIDDEN, D), FFN_HIDDEN)
    final_norm_w = gain(ks[14], (D,))
    return {"x": x, "mixer_norm_w": mixer_norm_w, "attn_w_in": attn_w_in,
            "attn_b_f": attn_b_f, "attn_w_out": attn_w_out, "sgu_w_in": sgu_w_in,
            "sgu_ln_g": sgu_ln_g, "sgu_ln_b": sgu_ln_b, "sgu_w_s": sgu_w_s,
            "sgu_b_s": sgu_b_s, "sgu_w_out": sgu_w_out, "ffn_norm_w": ffn_norm_w,
            "ffn_w_in": ffn_w_in, "ffn_w_out": ffn_w_out, "final_norm_w": final_norm_w}


def _fwd_reference(x, mixer_norm_w, attn_w_in, attn_b_f, attn_w_out, sgu_w_in, sgu_ln_g,
              sgu_ln_b, sgu_w_s, sgu_b_s, sgu_w_out, ffn_norm_w, ffn_w_in, ffn_w_out,
              final_norm_w):
    for i in range(DEPTH):
        h = rms_norm(x, mixer_norm_w[i])
        j = i // N_MIXERS
        if i % N_MIXERS == 0:
            x = x + forgetting_attention(h, attn_w_in[j], attn_b_f[j], attn_w_out[j])
        else:
            x = x + spatial_gating_mixer(h, sgu_w_in[j], sgu_ln_g[j], sgu_ln_b[j],
                                         sgu_w_s[j], sgu_b_s[j], sgu_w_out[j])
        x = x + swiglu_ffn(rms_norm(x, ffn_norm_w[i]), ffn_w_in[i], ffn_w_out[i])
    return rms_norm(x, final_norm_w)


import jax as _jax
import jax.numpy as _jnp

TWIN_FORMAT = 'train_step'
FWD_PARAMS = ['x', 'mixer_norm_w', 'attn_w_in', 'attn_b_f', 'attn_w_out', 'sgu_w_in', 'sgu_ln_g', 'sgu_ln_b', 'sgu_w_s', 'sgu_b_s', 'sgu_w_out', 'ffn_norm_w', 'ffn_w_in', 'ffn_w_out', 'final_norm_w']
TWIN_WEIGHTS = ['mixer_norm_w', 'attn_w_in', 'attn_b_f', 'attn_w_out', 'sgu_w_in', 'sgu_ln_g', 'sgu_ln_b', 'sgu_w_s', 'sgu_b_s', 'sgu_w_out', 'ffn_norm_w', 'ffn_w_in', 'ffn_w_out', 'final_norm_w']
TWIN_DIFF_INPUT = 'x'
TWIN_INPUTS = ['x', 'mixer_norm_w', 'attn_w_in', 'attn_b_f', 'attn_w_out', 'sgu_w_in', 'sgu_ln_g', 'sgu_ln_b', 'sgu_w_s', 'sgu_b_s', 'sgu_w_out', 'ffn_norm_w', 'ffn_w_in', 'ffn_w_out', 'final_norm_w', 'loss_target', 'm_mixer_norm_w', 'm_attn_w_in', 'm_attn_b_f', 'm_attn_w_out', 'm_sgu_w_in', 'm_sgu_ln_g', 'm_sgu_ln_b', 'm_sgu_w_s', 'm_sgu_b_s', 'm_sgu_w_out', 'm_ffn_norm_w', 'm_ffn_w_in', 'm_ffn_w_out', 'm_final_norm_w', 'v_mixer_norm_w', 'v_attn_w_in', 'v_attn_b_f', 'v_attn_w_out', 'v_sgu_w_in', 'v_sgu_ln_g', 'v_sgu_ln_b', 'v_sgu_w_s', 'v_sgu_b_s', 'v_sgu_w_out', 'v_ffn_norm_w', 'v_ffn_w_in', 'v_ffn_w_out', 'v_final_norm_w']
TWIN_OUTPUTS = ['loss', 'grad_x', 'grad_mixer_norm_w', 'grad_attn_w_in', 'grad_attn_b_f', 'grad_attn_w_out', 'grad_sgu_w_in', 'grad_sgu_ln_g', 'grad_sgu_ln_b', 'grad_sgu_w_s', 'grad_sgu_b_s', 'grad_sgu_w_out', 'grad_ffn_norm_w', 'grad_ffn_w_in', 'grad_ffn_w_out', 'grad_final_norm_w', 'delta_mixer_norm_w', 'delta_attn_w_in', 'delta_attn_b_f', 'delta_attn_w_out', 'delta_sgu_w_in', 'delta_sgu_ln_g', 'delta_sgu_ln_b', 'delta_sgu_w_s', 'delta_sgu_b_s', 'delta_sgu_w_out', 'delta_ffn_norm_w', 'delta_ffn_w_in', 'delta_ffn_w_out', 'delta_final_norm_w', 'new_m_mixer_norm_w', 'new_m_attn_w_in', 'new_m_attn_b_f', 'new_m_attn_w_out', 'new_m_sgu_w_in', 'new_m_sgu_ln_g', 'new_m_sgu_ln_b', 'new_m_sgu_w_s', 'new_m_sgu_b_s', 'new_m_sgu_w_out', 'new_m_ffn_norm_w', 'new_m_ffn_w_in', 'new_m_ffn_w_out', 'new_m_final_norm_w', 'new_v_mixer_norm_w', 'new_v_attn_w_in', 'new_v_attn_b_f', 'new_v_attn_w_out', 'new_v_sgu_w_in', 'new_v_sgu_ln_g', 'new_v_sgu_ln_b', 'new_v_sgu_w_s', 'new_v_sgu_b_s', 'new_v_sgu_w_out', 'new_v_ffn_norm_w', 'new_v_ffn_w_in', 'new_v_ffn_w_out', 'new_v_final_norm_w']
TWIN_LEAF_KINDS = {'loss': 'loss', 'grad_x': 'grad_x', 'grad_mixer_norm_w': 'grad_w', 'grad_attn_w_in': 'grad_w', 'grad_attn_b_f': 'grad_w', 'grad_attn_w_out': 'grad_w', 'grad_sgu_w_in': 'grad_w', 'grad_sgu_ln_g': 'grad_w', 'grad_sgu_ln_b': 'grad_w', 'grad_sgu_w_s': 'grad_w', 'grad_sgu_b_s': 'grad_w', 'grad_sgu_w_out': 'grad_w', 'grad_ffn_norm_w': 'grad_w', 'grad_ffn_w_in': 'grad_w', 'grad_ffn_w_out': 'grad_w', 'grad_final_norm_w': 'grad_w', 'delta_mixer_norm_w': 'delta_w', 'delta_attn_w_in': 'delta_w', 'delta_attn_b_f': 'delta_w', 'delta_attn_w_out': 'delta_w', 'delta_sgu_w_in': 'delta_w', 'delta_sgu_ln_g': 'delta_w', 'delta_sgu_ln_b': 'delta_w', 'delta_sgu_w_s': 'delta_w', 'delta_sgu_b_s': 'delta_w', 'delta_sgu_w_out': 'delta_w', 'delta_ffn_norm_w': 'delta_w', 'delta_ffn_w_in': 'delta_w', 'delta_ffn_w_out': 'delta_w', 'delta_final_norm_w': 'delta_w', 'new_m_mixer_norm_w': 'new_m', 'new_m_attn_w_in': 'new_m', 'new_m_attn_b_f': 'new_m', 'new_m_attn_w_out': 'new_m', 'new_m_sgu_w_in': 'new_m', 'new_m_sgu_ln_g': 'new_m', 'new_m_sgu_ln_b': 'new_m', 'new_m_sgu_w_s': 'new_m', 'new_m_sgu_b_s': 'new_m', 'new_m_sgu_w_out': 'new_m', 'new_m_ffn_norm_w': 'new_m', 'new_m_ffn_w_in': 'new_m', 'new_m_ffn_w_out': 'new_m', 'new_m_final_norm_w': 'new_m', 'new_v_mixer_norm_w': 'new_v', 'new_v_attn_w_in': 'new_v', 'new_v_attn_b_f': 'new_v', 'new_v_attn_w_out': 'new_v', 'new_v_sgu_w_in': 'new_v', 'new_v_sgu_ln_g': 'new_v', 'new_v_sgu_ln_b': 'new_v', 'new_v_sgu_w_s': 'new_v', 'new_v_sgu_b_s': 'new_v', 'new_v_sgu_w_out': 'new_v', 'new_v_ffn_norm_w': 'new_v', 'new_v_ffn_w_in': 'new_v', 'new_v_ffn_w_out': 'new_v', 'new_v_final_norm_w': 'new_v'}


def _forward(args):
    return _fwd_reference(*[args[k] for k in FWD_PARAMS])


def _output_shape():
    def fwd():
        inp = _fwd_setup_inputs(0)
        return _fwd_reference(*[inp[k] for k in FWD_PARAMS])
    out = _jax.eval_shape(fwd)
    return out.shape, out.dtype

N_MICROBATCH = 1
ADAM_LR = 0.001
ADAM_B1 = 0.9
ADAM_B2 = 0.999
ADAM_EPS = 1e-08
ADAM_WD = 0.01
ADAM_STEP = 10
PER_EXAMPLE_BATCH_AXIS = {'x': 0, 'loss_target': 0}
SHARED_INPUTS = []
_WEIGHT_DTYPES = {'mixer_norm_w': _jnp.float32, 'attn_w_in': _jnp.float32, 'attn_b_f': _jnp.float32, 'attn_w_out': _jnp.float32, 'sgu_w_in': _jnp.float32, 'sgu_ln_g': _jnp.float32, 'sgu_ln_b': _jnp.float32, 'sgu_w_s': _jnp.float32, 'sgu_b_s': _jnp.float32, 'sgu_w_out': _jnp.float32, 'ffn_norm_w': _jnp.float32, 'ffn_w_in': _jnp.float32, 'ffn_w_out': _jnp.float32, 'final_norm_w': _jnp.float32}
MOMENT_SCALE = {'mixer_norm_w': 2.297073e-01, 'attn_w_in': 1.209924e-01, 'attn_b_f': 1.327152e+00, 'attn_w_out': 1.474780e-01, 'sgu_w_in': 1.316287e-01, 'sgu_ln_g': 8.670975e-02, 'sgu_ln_b': 8.759904e-02, 'sgu_w_s': 8.402935e-02, 'sgu_b_s': 1.270097e-01, 'sgu_w_out': 3.079239e-01, 'ffn_norm_w': 2.512439e-01, 'ffn_w_in': 1.063438e-01, 'ffn_w_out': 1.745893e-01, 'final_norm_w': 1.285774e+02}


def _to_microbatches(a, axis):
    t = _jnp.moveaxis(a, axis, 0)
    t = t.reshape((N_MICROBATCH, t.shape[0] // N_MICROBATCH) + t.shape[1:])
    return _jnp.moveaxis(t, 1, axis + 1)


def setup_inputs(seed: int = 0) -> dict:
    inp = _fwd_setup_inputs(seed)
    key = _jax.random.fold_in(_jax.random.key(seed), 7919)
    shape, _ = _output_shape()
    out = dict(inp)
    out["loss_target"] = _jax.random.normal(_jax.random.fold_in(key, 0), shape, _jnp.float32)
    for i, name in enumerate(TWIN_WEIGHTS):
        w = inp[name].astype(_jnp.float32)
        if MOMENT_SCALE is None:
            s = _jnp.sqrt(_jnp.mean(_jnp.square(w)) + 1e-30)
        else:
            s = MOMENT_SCALE[name]
        km, kv = _jax.random.split(_jax.random.fold_in(key, i + 1))
        out[name] = w
        out["m_" + name] = s * _jax.random.normal(km, w.shape, _jnp.float32)
        out["v_" + name] = (s * s) * _jax.random.uniform(kv, w.shape, _jnp.float32, 0.5, 1.5)
    if N_MICROBATCH > 1:
        for name, axis in PER_EXAMPLE_BATCH_AXIS.items():
            out[name] = _to_microbatches(out[name], axis)
    return {'x': out['x'], 'mixer_norm_w': out['mixer_norm_w'], 'attn_w_in': out['attn_w_in'], 'attn_b_f': out['attn_b_f'], 'attn_w_out': out['attn_w_out'], 'sgu_w_in': out['sgu_w_in'], 'sgu_ln_g': out['sgu_ln_g'], 'sgu_ln_b': out['sgu_ln_b'], 'sgu_w_s': out['sgu_w_s'], 'sgu_b_s': out['sgu_b_s'], 'sgu_w_out': out['sgu_w_out'], 'ffn_norm_w': out['ffn_norm_w'], 'ffn_w_in': out['ffn_w_in'], 'ffn_w_out': out['ffn_w_out'], 'final_norm_w': out['final_norm_w'], 'loss_target': out['loss_target'], 'm_mixer_norm_w': out['m_mixer_norm_w'], 'm_attn_w_in': out['m_attn_w_in'], 'm_attn_b_f': out['m_attn_b_f'], 'm_attn_w_out': out['m_attn_w_out'], 'm_sgu_w_in': out['m_sgu_w_in'], 'm_sgu_ln_g': out['m_sgu_ln_g'], 'm_sgu_ln_b': out['m_sgu_ln_b'], 'm_sgu_w_s': out['m_sgu_w_s'], 'm_sgu_b_s': out['m_sgu_b_s'], 'm_sgu_w_out': out['m_sgu_w_out'], 'm_ffn_norm_w': out['m_ffn_norm_w'], 'm_ffn_w_in': out['m_ffn_w_in'], 'm_ffn_w_out': out['m_ffn_w_out'], 'm_final_norm_w': out['m_final_norm_w'], 'v_mixer_norm_w': out['v_mixer_norm_w'], 'v_attn_w_in': out['v_attn_w_in'], 'v_attn_b_f': out['v_attn_b_f'], 'v_attn_w_out': out['v_attn_w_out'], 'v_sgu_w_in': out['v_sgu_w_in'], 'v_sgu_ln_g': out['v_sgu_ln_g'], 'v_sgu_ln_b': out['v_sgu_ln_b'], 'v_sgu_w_s': out['v_sgu_w_s'], 'v_sgu_b_s': out['v_sgu_b_s'], 'v_sgu_w_out': out['v_sgu_w_out'], 'v_ffn_norm_w': out['v_ffn_norm_w'], 'v_ffn_w_in': out['v_ffn_w_in'], 'v_ffn_w_out': out['v_ffn_w_out'], 'v_final_norm_w': out['v_final_norm_w']}


def _loss(weights, diff, rest, loss_target):
    with _jax.named_scope("forward"):
        args = {**rest, TWIN_DIFF_INPUT: diff, **{k: w.astype(_WEIGHT_DTYPES[k]) for k, w in weights.items()}}
        y = _forward(args)
    with _jax.named_scope("loss_head"):
        err = _jnp.square(y.astype(_jnp.float32) - loss_target)
        return 0.5 * _jnp.sum(_jnp.mean(err, axis=-1)) if err.ndim else 0.5 * err


def _adamw(w, g, m, v):
    m = ADAM_B1 * m + (1.0 - ADAM_B1) * g
    v = ADAM_B2 * v + (1.0 - ADAM_B2) * _jnp.square(g)
    m_hat = m / (1.0 - ADAM_B1 ** ADAM_STEP)
    v_hat = v / (1.0 - ADAM_B2 ** ADAM_STEP)
    delta = -ADAM_LR * (m_hat / (_jnp.sqrt(v_hat) + ADAM_EPS) + ADAM_WD * w)
    return delta, m, v


def reference(x, mixer_norm_w, attn_w_in, attn_b_f, attn_w_out, sgu_w_in, sgu_ln_g, sgu_ln_b, sgu_w_s, sgu_b_s, sgu_w_out, ffn_norm_w, ffn_w_in, ffn_w_out, final_norm_w, loss_target, m_mixer_norm_w, m_attn_w_in, m_attn_b_f, m_attn_w_out, m_sgu_w_in, m_sgu_ln_g, m_sgu_ln_b, m_sgu_w_s, m_sgu_b_s, m_sgu_w_out, m_ffn_norm_w, m_ffn_w_in, m_ffn_w_out, m_final_norm_w, v_mixer_norm_w, v_attn_w_in, v_attn_b_f, v_attn_w_out, v_sgu_w_in, v_sgu_ln_g, v_sgu_ln_b, v_sgu_w_s, v_sgu_b_s, v_sgu_w_out, v_ffn_norm_w, v_ffn_w_in, v_ffn_w_out, v_final_norm_w):
    given = dict(x=x, mixer_norm_w=mixer_norm_w, attn_w_in=attn_w_in, attn_b_f=attn_b_f, attn_w_out=attn_w_out, sgu_w_in=sgu_w_in, sgu_ln_g=sgu_ln_g, sgu_ln_b=sgu_ln_b, sgu_w_s=sgu_w_s, sgu_b_s=sgu_b_s, sgu_w_out=sgu_w_out, ffn_norm_w=ffn_norm_w, ffn_w_in=ffn_w_in, ffn_w_out=ffn_w_out, final_norm_w=final_norm_w, loss_target=loss_target, m_mixer_norm_w=m_mixer_norm_w, m_attn_w_in=m_attn_w_in, m_attn_b_f=m_attn_b_f, m_attn_w_out=m_attn_w_out, m_sgu_w_in=m_sgu_w_in, m_sgu_ln_g=m_sgu_ln_g, m_sgu_ln_b=m_sgu_ln_b, m_sgu_w_s=m_sgu_w_s, m_sgu_b_s=m_sgu_b_s, m_sgu_w_out=m_sgu_w_out, m_ffn_norm_w=m_ffn_norm_w, m_ffn_w_in=m_ffn_w_in, m_ffn_w_out=m_ffn_w_out, m_final_norm_w=m_final_norm_w, v_mixer_norm_w=v_mixer_norm_w, v_attn_w_in=v_attn_w_in, v_attn_b_f=v_attn_b_f, v_attn_w_out=v_attn_w_out, v_sgu_w_in=v_sgu_w_in, v_sgu_ln_g=v_sgu_ln_g, v_sgu_ln_b=v_sgu_ln_b, v_sgu_w_s=v_sgu_w_s, v_sgu_b_s=v_sgu_b_s, v_sgu_w_out=v_sgu_w_out, v_ffn_norm_w=v_ffn_norm_w, v_ffn_w_in=v_ffn_w_in, v_ffn_w_out=v_ffn_w_out, v_final_norm_w=v_final_norm_w)
    weights = {n: given[n] for n in TWIN_WEIGHTS}
    shared = {n: given[n] for n in SHARED_INPUTS}
    per_example = {n: given[n] for n in ['x']}
    grad_fn = _jax.value_and_grad(_loss, argnums=(0, 1))

    def one_microbatch(ex, loss_target):
        ex = dict(ex)
        diff = ex.pop(TWIN_DIFF_INPUT)
        return grad_fn(weights, diff, {**shared, **ex}, loss_target)

    if N_MICROBATCH == 1:
        loss, (grad_w, grad_x) = one_microbatch(per_example, given["loss_target"])
    else:
        def body(carry, xs):
            loss_sum, grad_sum = carry
            l_k, (gw_k, gx_k) = one_microbatch(xs[0], xs[1])
            with _jax.named_scope("update"):
                return (loss_sum + l_k, _jax.tree.map(_jnp.add, grad_sum, gw_k)), gx_k

        init = (_jnp.zeros((), _jnp.float32), _jax.tree.map(_jnp.zeros_like, weights))
        (loss, grad_w), grad_x = _jax.lax.scan(body, init, (per_example, given["loss_target"]))
    with _jax.named_scope("update"):
        delta_w, new_m, new_v = {}, {}, {}
        for n in TWIN_WEIGHTS:
            delta_w[n], new_m[n], new_v[n] = _adamw(weights[n], grad_w[n], given["m_" + n], given["v_" + n])
    return (loss, grad_x, *[grad_w[n] for n in TWIN_WEIGHTS], *[delta_w[n] for n in TWIN_WEIGHTS],
            *[new_m[n] for n in TWIN_WEIGHTS], *[new_v[n] for n in TWIN_WEIGHTS])
```

```python
import functools
import math

import jax
import jax.numpy as jnp
from jax import lax
from jax.experimental import pallas as pl
from jax.experimental.pallas import tpu as pltpu

F32 = jnp.float32
BF16 = jnp.bfloat16
NORM_EPS = 1e-6
LN_EPS = 1e-5
ADAM_LR = 0.001
ADAM_B1 = 0.9
ADAM_B2 = 0.999
ADAM_EPS = 1e-08
ADAM_WD = 0.01
ADAM_STEP = 10

LANES = 128
SUBLANES = 8
PACK_COLS = 1024
VMEM_LIMIT = 56 * 1024 * 1024
NEG_BIG = -1e30
MESH = pl.DeviceIdType.MESH


def _cp():
    return pltpu.CompilerParams(vmem_limit_bytes=VMEM_LIMIT)


def _tile(n, cap, mult):
    best = None
    d = mult
    while d <= min(n, cap):
        if n % d == 0:
            best = d
        d += mult
    return n if best is None else best


def _hbm():
    return pl.BlockSpec(memory_space=pltpu.HBM)


def _rmsnorm_fwd(x, w, name):
    T, D = x.shape
    tm = _tile(T, 512, SUBLANES)

    def body(x_ref, w_ref, h_ref):
        xf = x_ref[...]
        r = lax.rsqrt(jnp.mean(xf * xf, axis=-1, keepdims=True) + NORM_EPS)
        h_ref[...] = (xf * r * w_ref[...]).astype(BF16)

    return pl.pallas_call(
        body, grid=(T // tm,),
        in_specs=[pl.BlockSpec((tm, D), lambda i: (i, 0)), pl.BlockSpec((1, D), lambda i: (0, 0))],
        out_specs=pl.BlockSpec((tm, D), lambda i: (i, 0)),
        out_shape=jax.ShapeDtypeStruct((T, D), BF16), name=name, compiler_params=_cp(),
    )(x, w.reshape(1, D))


def _rmsnorm_bwd(dh, x, w, dres, name):
    T, D = x.shape
    tm = _tile(T, 512, SUBLANES)

    def body(dh_ref, x_ref, w_ref, dres_ref, dx_ref, dw_ref):
        @pl.when(pl.program_id(0) == 0)
        def _():
            dw_ref[...] = jnp.zeros_like(dw_ref)

        xf = x_ref[...]
        r = lax.rsqrt(jnp.mean(xf * xf, axis=-1, keepdims=True) + NORM_EPS)
        xhat = xf * r
        dhv = dh_ref[...]
        dxhat = dhv * w_ref[...]
        dx_ref[...] = dres_ref[...] + r * (dxhat - xhat * jnp.mean(dxhat * xhat, axis=-1, keepdims=True))
        dw_ref[...] += jnp.sum(dhv * xhat, axis=0, keepdims=True)

    row = pl.BlockSpec((tm, D), lambda i: (i, 0))
    return pl.pallas_call(
        body, grid=(T // tm,),
        in_specs=[row, row, pl.BlockSpec((1, D), lambda i: (0, 0)), row],
        out_specs=[row, pl.BlockSpec((SUBLANES, D), lambda i: (0, 0))],
        out_shape=[jax.ShapeDtypeStruct((T, D), F32), jax.ShapeDtypeStruct((SUBLANES, D), F32)],
        name=name, compiler_params=_cp(),
    )(dh, x, w.reshape(1, D), dres)


def _mm(a, b, mode, out_dtype, name, res=None):
    if mode == "tn":
        kt, M = a.shape
        N = b.shape[1]
        tm = _tile(M, 1408, LANES)
        tn = _tile(N, 1408, LANES)
        tk = _tile(kt, 512, 16)

        def body(a_ref, b_ref, o_ref):
            @pl.when(pl.program_id(2) == 0)
            def _():
                o_ref[...] = jnp.zeros_like(o_ref)

            o_ref[...] += lax.dot_general(
                a_ref[...].astype(BF16), b_ref[...].astype(BF16), (((0,), (0,)), ((), ())),
                preferred_element_type=F32)

        return pl.pallas_call(
            body, grid=(M // tm, N // tn, kt // tk),
            in_specs=[pl.BlockSpec((tk, tm), lambda i, j, k: (k, i)),
                      pl.BlockSpec((tk, tn), lambda i, j, k: (k, j))],
            out_specs=pl.BlockSpec((tm, tn), lambda i, j, k: (i, j)),
            out_shape=jax.ShapeDtypeStruct((M, N), F32), name=name, compiler_params=_cp(),
        )(a, b)

    M, K = a.shape
    N = b.shape[1] if mode == "nn" else b.shape[0]
    tm = _tile(M, 512, 16)
    cap = max(LANES, min(1408, ((6 << 20) // (2 * K)) // LANES * LANES))
    tn = _tile(N, cap, LANES)
    dims = (((1,), (0,)), ((), ())) if mode == "nn" else (((1,), (1,)), ((), ()))

    def body(*refs):
        if res is None:
            a_ref, b_ref, o_ref = refs
        else:
            a_ref, b_ref, r_ref, o_ref = refs
        acc = lax.dot_general(a_ref[...].astype(BF16), b_ref[...].astype(BF16), dims,
                              preferred_element_type=F32)
        if res is not None:
            acc = acc + r_ref[...]
        o_ref[...] = acc.astype(out_dtype)

    b_spec = (pl.BlockSpec((K, tn), lambda i, j: (0, j)) if mode == "nn"
              else pl.BlockSpec((tn, K), lambda i, j: (j, 0)))
    in_specs = [pl.BlockSpec((tm, K), lambda i, j: (i, 0)), b_spec]
    args = [a, b]
    if res is not None:
        in_specs.append(pl.BlockSpec((tm, tn), lambda i, j: (i, j)))
        args.append(res)
    return pl.pallas_call(
        body, grid=(M // tm, N // tn), in_specs=in_specs,
        out_specs=pl.BlockSpec((tm, tn), lambda i, j: (i, j)),
        out_shape=jax.ShapeDtypeStruct((M, N), out_dtype), name=name, compiler_params=_cp(),
    )(*args)


def _silu_mul(a, name):
    T, F2 = a.shape
    F = F2 // 2
    tm = _tile(T, 256, 16)

    def body(a_ref, s_ref):
        g = a_ref[:, :F].astype(F32)
        u = a_ref[:, F:].astype(F32)
        s_ref[...] = (g * jax.nn.sigmoid(g) * u).astype(BF16)

    return pl.pallas_call(
        body, grid=(T // tm,),
        in_specs=[pl.BlockSpec((tm, F2), lambda i: (i, 0))],
        out_specs=pl.BlockSpec((tm, F), lambda i: (i, 0)),
        out_shape=jax.ShapeDtypeStruct((T, F), BF16), name=name, compiler_params=_cp(),
    )(a)


def _silu_mul_bwd(ds, a, name):
    T, F2 = a.shape
    F = F2 // 2
    tm = _tile(T, 256, 16)

    def body(ds_ref, a_ref, da_ref):
        g = a_ref[:, :F].astype(F32)
        u = a_ref[:, F:].astype(F32)
        dsv = ds_ref[...]
        sg = jax.nn.sigmoid(g)
        da_ref[:, :F] = (dsv * u * (sg * (1.0 + g * (1.0 - sg)))).astype(BF16)
        da_ref[:, F:] = (dsv * (g * sg)).astype(BF16)

    return pl.pallas_call(
        body, grid=(T // tm,),
        in_specs=[pl.BlockSpec((tm, F), lambda i: (i, 0)), pl.BlockSpec((tm, F2), lambda i: (i, 0))],
        out_specs=pl.BlockSpec((tm, F2), lambda i: (i, 0)),
        out_shape=jax.ShapeDtypeStruct((T, F2), BF16), name=name, compiler_params=_cp(),
    )(ds, a)


def _loss_head(x, w, tgt, name):
    T, D = x.shape
    tm = _tile(T, 512, SUBLANES)

    def body(x_ref, w_ref, t_ref, dx_ref, loss_ref, dw_ref):
        @pl.when(pl.program_id(0) == 0)
        def _():
            loss_ref[...] = jnp.zeros_like(loss_ref)
            dw_ref[...] = jnp.zeros_like(dw_ref)

        xf = x_ref[...]
        wv = w_ref[...]
        r = lax.rsqrt(jnp.mean(xf * xf, axis=-1, keepdims=True) + NORM_EPS)
        xhat = xf * r
        err = xhat * wv - t_ref[...]
        per_tok = jnp.mean(err * err, axis=-1, keepdims=True)
        loss_ref[...] += 0.5 * jnp.sum(per_tok, axis=0, keepdims=True)
        dy = err * (1.0 / D)
        dxhat = dy * wv
        dx_ref[...] = r * (dxhat - xhat * jnp.mean(dxhat * xhat, axis=-1, keepdims=True))
        dw_ref[...] += jnp.sum(dy * xhat, axis=0, keepdims=True)

    row = pl.BlockSpec((tm, D), lambda i: (i, 0))
    return pl.pallas_call(
        body, grid=(T // tm,),
        in_specs=[row, pl.BlockSpec((1, D), lambda i: (0, 0)), row],
        out_specs=[row, pl.BlockSpec((SUBLANES, LANES), lambda i: (0, 0)),
                   pl.BlockSpec((SUBLANES, D), lambda i: (0, 0))],
        out_shape=[jax.ShapeDtypeStruct((T, D), F32), jax.ShapeDtypeStruct((SUBLANES, LANES), F32),
                   jax.ShapeDtypeStruct((SUBLANES, D), F32)],
        name=name, compiler_params=_cp(),
    )(x, w.reshape(1, D), tgt)


def _split3(v):
    hi = v.astype(BF16)
    r1 = v - hi.astype(F32)
    mid = r1.astype(BF16)
    lo = (r1 - mid.astype(F32)).astype(BF16)
    return hi, mid, lo


def _tri_dot(tri, v):
    out = None
    for piece in _split3(v):
        t = jnp.dot(tri, piece, preferred_element_type=F32)
        out = t if out is None else out + t
    return out


def _gate_fwd(f, b_f, P, name):
    T = f.shape[0]
    tb = _tile(T, 256, LANES)

    def body(f_ref, b_ref, ct_ref, cc_ref, carry):
        @pl.when(pl.program_id(0) == 0)
        def _():
            carry[...] = jnp.zeros_like(carry)

        z = f_ref[...] + b_ref[...]
        logf = jnp.minimum(z, 0.0) - jnp.log(1.0 + jnp.exp(-jnp.abs(z)))
        row = lax.broadcasted_iota(jnp.int32, (tb, tb), 0)
        col = lax.broadcasted_iota(jnp.int32, (tb, tb), 1)
        tri = (col <= row).astype(BF16)
        c = _tri_dot(tri, logf) + carry[0:1, :]
        carry[...] = jnp.broadcast_to(c[tb - 1:tb, :], carry.shape)
        for p in range(P):
            shifted = c if p == 0 else pltpu.roll(c, LANES - 2 * p, 1)
            cc_ref[p] = shifted
            ct_ref[p] = shifted.T[0:SUBLANES, :]

    return pl.pallas_call(
        body, grid=(T // tb,),
        in_specs=[pl.BlockSpec((tb, LANES), lambda i: (i, 0)), pl.BlockSpec((1, LANES), lambda i: (0, 0))],
        out_specs=[pl.BlockSpec((P, SUBLANES, tb), lambda i: (0, 0, i)),
                   pl.BlockSpec((P, tb, LANES), lambda i: (0, i, 0))],
        out_shape=[jax.ShapeDtypeStruct((P, SUBLANES, T), F32), jax.ShapeDtypeStruct((P, T, LANES), F32)],
        scratch_shapes=[pltpu.VMEM((SUBLANES, LANES), F32)],
        name=name, compiler_params=_cp(),
    )(f, b_f)


def _gate_bwd(dc_cols, drowT, f, b_f, P, name):
    T = f.shape[0]
    tb = _tile(T, 256, LANES)
    nb = T // tb

    def body(dc_ref, dr_ref, f_ref, b_ref, df_ref, db_ref, carry):
        @pl.when(pl.program_id(0) == 0)
        def _():
            carry[...] = jnp.zeros_like(carry)
            db_ref[...] = jnp.zeros_like(db_ref)

        lane = lax.broadcasted_iota(jnp.int32, (tb, LANES), 1)
        dc = jnp.zeros((tb, LANES), F32)
        for p in range(P):
            rows = jnp.concatenate([dr_ref[p], jnp.zeros((LANES - SUBLANES, tb), F32)], axis=0)
            part = jnp.where(lane < 2, dc_ref[p] + rows.T, 0.0)
            dc = dc + (part if p == 0 else pltpu.roll(part, 2 * p, 1))
        row = lax.broadcasted_iota(jnp.int32, (tb, tb), 0)
        col = lax.broadcasted_iota(jnp.int32, (tb, tb), 1)
        tri = (col >= row).astype(BF16)
        dlogf = _tri_dot(tri, dc) + carry[0:1, :]
        carry[...] = jnp.broadcast_to(dlogf[0:1, :], carry.shape)
        z = f_ref[...] + b_ref[...]
        df = jnp.where(lane < 2 * P, dlogf * jax.nn.sigmoid(-z), 0.0)
        df_ref[...] = df
        db_ref[...] += jnp.sum(df, axis=0, keepdims=True)

    return pl.pallas_call(
        body, grid=(nb,),
        in_specs=[pl.BlockSpec((P, tb, LANES), lambda i: (0, nb - 1 - i, 0)),
                  pl.BlockSpec((P, SUBLANES, tb), lambda i: (0, 0, nb - 1 - i)),
                  pl.BlockSpec((tb, LANES), lambda i: (nb - 1 - i, 0)),
                  pl.BlockSpec((1, LANES), lambda i: (0, 0))],
        out_specs=[pl.BlockSpec((tb, LANES), lambda i: (nb - 1 - i, 0)),
                   pl.BlockSpec((SUBLANES, LANES), lambda i: (0, 0))],
        out_shape=[jax.ShapeDtypeStruct((T, LANES), F32), jax.ShapeDtypeStruct((SUBLANES, LANES), F32)],
        scratch_shapes=[pltpu.VMEM((SUBLANES, LANES), F32)],
        name=name, compiler_params=_cp(),
    )(dc_cols, drowT, f, b_f)


def _nt(a, b):
    return lax.dot_general(a, b, (((1,), (1,)), ((), ())), preferred_element_type=F32)


def _attn_fwd(qkv, cT, P, scale, name):
    T = qkv.shape[0]
    tq = _tile(T, 256, LANES)
    nq = T // tq

    def body(q_ref, k_ref, v_ref, c_ref, o_ref, lse_ref):
        i = pl.program_id(1)
        lane = lax.broadcasted_iota(jnp.int32, (1, LANES), 1)
        q = (q_ref[...].astype(F32) * scale).astype(BF16)
        q_heads = (jnp.where(lane < 64, q, jnp.zeros_like(q)), jnp.where(lane >= 64, q, jnp.zeros_like(q)))
        c0 = c_ref[0, :, pl.ds(pl.multiple_of(i * tq, tq), LANES)][:, 0:1]

        def block(j, carry, masked):
            start = pl.multiple_of(j * tq, tq)
            k = k_ref[pl.ds(start, tq), :]
            v = v_ref[pl.ds(start, tq), :]
            bias = c0 - c_ref[0, :, pl.ds(start, tq)]
            new = []
            for a in range(2):
                m, l, acc = carry[a]
                s = _nt(q_heads[a], k) + bias[a:a + 1, :]
                if masked:
                    row = lax.broadcasted_iota(jnp.int32, (tq, tq), 0)
                    col = lax.broadcasted_iota(jnp.int32, (tq, tq), 1)
                    s = jnp.where(col <= row, s, NEG_BIG)
                m_new = jnp.maximum(m, jnp.max(s, axis=1, keepdims=True))
                alpha = jnp.exp(m - m_new)
                p = jnp.exp(s - m_new)
                l = alpha * l + jnp.sum(p, axis=1, keepdims=True)
                acc = alpha * acc + jnp.dot(p.astype(BF16), v, preferred_element_type=F32)
                new.append((m_new, l, acc))
            return tuple(new)

        init = tuple((jnp.full((tq, 1), NEG_BIG, F32), jnp.zeros((tq, 1), F32), jnp.zeros((tq, LANES), F32))
                     for _ in range(2))
        carry = lax.fori_loop(0, i, lambda j, cr: block(j, cr, False), init)
        (m0, l0, a0), (m1, l1, a1) = block(i, carry, True)
        o_ref[...] = jnp.where(lane < 64, a0 / l0, a1 / l1).astype(BF16)
        lse = jnp.where(lane == 0, m0 + jnp.log(l0), jnp.where(lane == 1, m1 + jnp.log(l1), 0.0))
        lse_ref[0] = lse.T[0:SUBLANES, :]

    return pl.pallas_call(
        body, grid=(P, nq),
        in_specs=[pl.BlockSpec((tq, LANES), lambda p, i: (i, p)),
                  pl.BlockSpec((T, LANES), lambda p, i: (0, P + p)),
                  pl.BlockSpec((T, LANES), lambda p, i: (0, 2 * P + p)),
                  pl.BlockSpec((1, SUBLANES, T), lambda p, i: (p, 0, 0))],
        out_specs=[pl.BlockSpec((tq, LANES), lambda p, i: (i, p)),
                   pl.BlockSpec((1, SUBLANES, tq), lambda p, i: (p, 0, i))],
        out_shape=[jax.ShapeDtypeStruct((T, LANES * P), BF16), jax.ShapeDtypeStruct((P, SUBLANES, T), F32)],
        name=name, compiler_params=_cp(),
    )(qkv, qkv, qkv, cT)


def _attn_delta(do, o, P, name):
    T, D = o.shape
    tb = _tile(T, 256, LANES)

    def body(do_ref, o_ref, d_ref):
        lane = lax.broadcasted_iota(jnp.int32, (1, LANES), 1)
        for p in range(P):
            cols = slice(p * LANES, (p + 1) * LANES)
            prod = do_ref[:, cols].astype(F32) * o_ref[:, cols].astype(F32)
            d0 = jnp.sum(jnp.where(lane < 64, prod, 0.0), axis=1, keepdims=True)
            d1 = jnp.sum(jnp.where(lane >= 64, prod, 0.0), axis=1, keepdims=True)
            both = jnp.where(lane == 0, d0, jnp.where(lane == 1, d1, 0.0))
            d_ref[p] = both.T[0:SUBLANES, :]

    return pl.pallas_call(
        body, grid=(T // tb,),
        in_specs=[pl.BlockSpec((tb, D), lambda i: (i, 0)), pl.BlockSpec((tb, D), lambda i: (i, 0))],
        out_specs=pl.BlockSpec((P, SUBLANES, tb), lambda i: (0, 0, i)),
        out_shape=jax.ShapeDtypeStruct((P, SUBLANES, T), F32), name=name, compiler_params=_cp(),
    )(do, o)


def _attn_bwd(qkv, do, lseT, dT, cT, c_cols, P, scale, name):
    T = qkv.shape[0]
    tq = _tile(T, 256, LANES)
    nq = T // tq

    def body(q_ref, do_ref, k_ref, v_ref, lse_ref, d_ref, c_ref, cc_ref,
             dq_ref, dk_ref, dv_ref, dc_ref, drow_ref, dq_acc):
        j = pl.program_id(1)

        @pl.when(j == 0)
        def _():
            dq_acc[...] = jnp.zeros_like(dq_acc)
            drow_ref[...] = jnp.zeros_like(drow_ref)

        lane = lax.broadcasted_iota(jnp.int32, (1, LANES), 1)
        k = k_ref[...]
        v = v_ref[...]
        zero = jnp.zeros_like(k)
        k_heads = (jnp.where(lane < 64, k, zero), jnp.where(lane >= 64, k, zero))
        v_heads = (jnp.where(lane < 64, v, zero), jnp.where(lane >= 64, v, zero))
        cc = cc_ref[0]
        c_keys = (cc[:, 0:1], cc[:, 1:2])

        def block(i, carry, masked):
            start = pl.multiple_of(i * tq, tq)
            q = (q_ref[pl.ds(start, tq), :].astype(F32) * scale).astype(BF16)
            dov = do_ref[pl.ds(start, tq), :]
            lse = lse_ref[0, :, pl.ds(start, tq)]
            dlt = d_ref[0, :, pl.ds(start, tq)]
            c0 = c_ref[0, :, pl.ds(start, LANES)][:, 0:1]
            rowv = c0 - lse
            dq_blk = jnp.zeros((tq, LANES), F32)
            new = []
            for a in range(2):
                dk_a, dv_a, dsum = carry[a]
                st = _nt(k_heads[a], q)
                pt = jnp.exp(st - c_keys[a] + rowv[a:a + 1, :])
                if masked:
                    row = lax.broadcasted_iota(jnp.int32, (tq, tq), 0)
                    col = lax.broadcasted_iota(jnp.int32, (tq, tq), 1)
                    pt = jnp.where(col >= row, pt, 0.0)
                dpt = _nt(v_heads[a], dov)
                dst = pt * (dpt - dlt[a:a + 1, :])
                dst_b = dst.astype(BF16)
                dv_a = dv_a + jnp.dot(pt.astype(BF16), dov, preferred_element_type=F32)
                dk_a = dk_a + jnp.dot(dst_b, q, preferred_element_type=F32)
                dq_blk = dq_blk + lax.dot_general(dst_b, k_heads[a], (((0,), (0,)), ((), ())),
                                                  preferred_element_type=F32)
                drow_ref[0, a:a + 1, pl.ds(start, tq)] += jnp.sum(dst, axis=0, keepdims=True)
                new.append((dk_a, dv_a, dsum + dst))
            dq_acc[pl.ds(start, tq), :] += dq_blk
            return tuple(new)

        init = tuple((jnp.zeros((tq, LANES), F32), jnp.zeros((tq, LANES), F32), jnp.zeros((tq, tq), F32))
                     for _ in range(2))
        carry = block(j, init, True)
        (dk0, dv0, ds0), (dk1, dv1, ds1) = lax.fori_loop(j + 1, nq, lambda i, cr: block(i, cr, False), carry)
        dk_ref[...] = jnp.where(lane < 64, dk0, dk1).astype(BF16)
        dv_ref[...] = jnp.where(lane < 64, dv0, dv1).astype(BF16)
        dc0 = -jnp.sum(ds0, axis=1, keepdims=True)
        dc1 = -jnp.sum(ds1, axis=1, keepdims=True)
        dc_ref[0] = jnp.where(lane == 0, dc0, jnp.where(lane == 1, dc1, 0.0))

        @pl.when(j == nq - 1)
        def _():
            dq_ref[...] = (dq_acc[...] * scale).astype(BF16)

    full = lambda col: pl.BlockSpec((T, LANES), lambda p, j: (0, col(p)))
    blk = lambda col: pl.BlockSpec((tq, LANES), lambda p, j: (j, col(p)))
    rows = pl.BlockSpec((1, SUBLANES, T), lambda p, j: (p, 0, 0))
    cols = pl.BlockSpec((1, tq, LANES), lambda p, j: (p, j, 0))
    D = LANES * P
    return pl.pallas_call(
        body, grid=(P, nq),
        in_specs=[full(lambda p: p), full(lambda p: p), blk(lambda p: P + p), blk(lambda p: 2 * P + p),
                  rows, rows, rows, cols],
        out_specs=[full(lambda p: p), blk(lambda p: p), blk(lambda p: p), cols, rows],
        out_shape=[jax.ShapeDtypeStruct((T, D), BF16), jax.ShapeDtypeStruct((T, D), BF16),
                   jax.ShapeDtypeStruct((T, D), BF16), jax.ShapeDtypeStruct((P, T, LANES), F32),
                   jax.ShapeDtypeStruct((P, SUBLANES, T), F32)],
        scratch_shapes=[pltpu.VMEM((T, LANES), F32)],
        name=name, compiler_params=_cp(),
    )(qkv, do, qkv, qkv, lseT, dT, cT, c_cols)


_SQRT_HALF = 0.7071067811865476
_INV_SQRT_2PI = 0.3989422804014327


def _gelu(v):
    return 0.5 * v * (1.0 + lax.erf(v * _SQRT_HALF))


def _gelu_grad(v):
    return 0.5 * (1.0 + lax.erf(v * _SQRT_HALF)) + v * (_INV_SQRT_2PI * jnp.exp(-0.5 * v * v))


def _sgu_fwd(a, ln_g, ln_b, w_tril, bias, name):
    T, W2 = a.shape
    W = W2 // 2
    G = w_tril.shape[0]
    tb = _tile(T, 256, LANES)

    def body(a_ref, g_ref, b_ref, w_ref, bias_ref, out_ref):
        zu = _gelu(a_ref[:, :W].astype(F32))
        zv = _gelu(a_ref[:, W:].astype(F32))
        mu = jnp.mean(zv, axis=-1, keepdims=True)
        d = zv - mu
        rstd = lax.rsqrt(jnp.mean(d * d, axis=-1, keepdims=True) + LN_EPS)
        vn = (d * rstd * g_ref[...] + b_ref[...]).astype(BF16)
        for c in range(tb // LANES):
            rs = slice(c * LANES, (c + 1) * LANES)
            for g in range(G):
                cs = slice(g * LANES, (g + 1) * LANES)
                mixed = jnp.dot(w_ref[g], vn[rs, cs], preferred_element_type=F32) + bias_ref[:, cs]
                out_ref[rs, cs] = (zu[rs, cs] * mixed).astype(BF16)

    return pl.pallas_call(
        body, grid=(T // tb,),
        in_specs=[pl.BlockSpec((tb, W2), lambda i: (i, 0)), pl.BlockSpec((1, W), lambda i: (0, 0)),
                  pl.BlockSpec((1, W), lambda i: (0, 0)), pl.BlockSpec((G, LANES, LANES), lambda i: (0, 0, 0)),
                  pl.BlockSpec((LANES, W), lambda i: (0, 0))],
        out_specs=pl.BlockSpec((tb, W), lambda i: (i, 0)),
        out_shape=jax.ShapeDtypeStruct((T, W), BF16), name=name, compiler_params=_cp(),
    )(a, ln_g.reshape(1, W), ln_b.reshape(1, W), w_tril, bias)


def _sgu_bwd(a, dgated, ln_g, ln_b, w_tril, w_tril_t, bias, name):
    T, W2 = a.shape
    W = W2 // 2
    G = w_tril.shape[0]
    tb = _tile(T, 256, LANES)

    def body(a_ref, dg_ref, g_ref, b_ref, w_ref, wt_ref, bias_ref,
             da_ref, dws_ref, dbias_ref, dlng_ref, dlnb_ref, dvn_ref):
        @pl.when(pl.program_id(0) == 0)
        def _():
            dws_ref[...] = jnp.zeros_like(dws_ref)
            dbias_ref[...] = jnp.zeros_like(dbias_ref)
            dlng_ref[...] = jnp.zeros_like(dlng_ref)
            dlnb_ref[...] = jnp.zeros_like(dlnb_ref)

        up = a_ref[:, :W].astype(F32)
        vp = a_ref[:, W:].astype(F32)
        zu = _gelu(up)
        zv = _gelu(vp)
        mu = jnp.mean(zv, axis=-1, keepdims=True)
        d = zv - mu
        rstd = lax.rsqrt(jnp.mean(d * d, axis=-1, keepdims=True) + LN_EPS)
        vhat = d * rstd
        gam = g_ref[...]
        vn = (vhat * gam + b_ref[...]).astype(BF16)
        dgated = dg_ref[...]
        for c in range(tb // LANES):
            rs = slice(c * LANES, (c + 1) * LANES)
            for g in range(G):
                cs = slice(g * LANES, (g + 1) * LANES)
                vb = vn[rs, cs]
                mixed = jnp.dot(w_ref[g], vb, preferred_element_type=F32) + bias_ref[:, cs]
                dgt = dgated[rs, cs]
                da_ref[rs, cs] = (dgt * mixed * _gelu_grad(up[rs, cs])).astype(BF16)
                dmx = dgt * zu[rs, cs]
                dbias_ref[:, cs] += dmx
                dmb = dmx.astype(BF16)
                dws_ref[g] += _nt(dmb, vb)
                dvn_ref[rs, cs] = jnp.dot(wt_ref[g], dmb, preferred_element_type=F32)
        dvn = dvn_ref[...]
        dlng_ref[...] += jnp.sum(dvn * vhat, axis=0, keepdims=True)
        dlnb_ref[...] += jnp.sum(dvn, axis=0, keepdims=True)
        dvh = dvn * gam
        dzv = rstd * (dvh - jnp.mean(dvh, axis=-1, keepdims=True)
                      - vhat * jnp.mean(dvh * vhat, axis=-1, keepdims=True))
        da_ref[:, W:] = (dzv * _gelu_grad(vp)).astype(BF16)

    const2 = lambda shape: pl.BlockSpec(shape, lambda i: (0, 0))
    const3 = pl.BlockSpec((G, LANES, LANES), lambda i: (0, 0, 0))
    return pl.pallas_call(
        body, grid=(T // tb,),
        in_specs=[pl.BlockSpec((tb, W2), lambda i: (i, 0)), pl.BlockSpec((tb, W), lambda i: (i, 0)),
                  const2((1, W)), const2((1, W)), const3, const3, const2((LANES, W))],
        out_specs=[pl.BlockSpec((tb, W2), lambda i: (i, 0)), const3, const2((LANES, W)),
                   const2((SUBLANES, W)), const2((SUBLANES, W))],
        out_shape=[jax.ShapeDtypeStruct((T, W2), BF16), jax.ShapeDtypeStruct((G, LANES, LANES), F32),
                   jax.ShapeDtypeStruct((LANES, W), F32), jax.ShapeDtypeStruct((SUBLANES, W), F32),
                   jax.ShapeDtypeStruct((SUBLANES, W), F32)],
        scratch_shapes=[pltpu.VMEM((tb, W), F32)],
        name=name, compiler_params=_cp(),
    )(a, dgated, ln_g.reshape(1, W), ln_b.reshape(1, W), w_tril, w_tril_t, bias)


def _adam_math(w, g, m, v):
    m = ADAM_B1 * m + (1.0 - ADAM_B1) * g
    v = ADAM_B2 * v + (1.0 - ADAM_B2) * (g * g)
    m_hat = m / (1.0 - ADAM_B1 ** ADAM_STEP)
    v_hat = v / (1.0 - ADAM_B2 ** ADAM_STEP)
    delta = -ADAM_LR * (m_hat / (jnp.sqrt(v_hat) + ADAM_EPS) + ADAM_WD * w)
    return delta, m, v


def _adamw(g, w, m, v, name):
    R, C = w.shape
    tb = _tile(R, 512, SUBLANES)

    def body(g_ref, w_ref, m_ref, v_ref, d_ref, mo_ref, vo_ref):
        d, mm, vv = _adam_math(w_ref[...], g_ref[...], m_ref[...], v_ref[...])
        d_ref[...] = d
        mo_ref[...] = mm
        vo_ref[...] = vv

    row = pl.BlockSpec((tb, C), lambda i: (i, 0))
    sds = jax.ShapeDtypeStruct((R, C), F32)
    return pl.pallas_call(body, grid=(R // tb,), in_specs=[row] * 4, out_specs=[row] * 3,
                          out_shape=[sds] * 3, name=name, compiler_params=_cp())(g, w, m, v)


def _adamw_sum(parts, w, m, v, name):
    K, R, C = parts.shape
    tb = _tile(R, 128, SUBLANES)

    def body(p_ref, w_ref, m_ref, v_ref, g_ref, d_ref, mo_ref, vo_ref):
        g = p_ref[0]
        for k in range(1, K):
            g = g + p_ref[k]
        d, mm, vv = _adam_math(w_ref[...], g, m_ref[...], v_ref[...])
        g_ref[...] = g
        d_ref[...] = d
        mo_ref[...] = mm
        vo_ref[...] = vv

    row = pl.BlockSpec((tb, C), lambda i: (i, 0))
    sds = jax.ShapeDtypeStruct((R, C), F32)
    return pl.pallas_call(
        body, grid=(R // tb,),
        in_specs=[pl.BlockSpec((K, tb, C), lambda i: (0, i, 0)), row, row, row],
        out_specs=[row] * 4, out_shape=[sds] * 4, name=name, compiler_params=_cp())(parts, w, m, v)


def _pair_sum(g_all, recv, c_idx, name):
    K, R, C = g_all.shape
    rh = R // 2
    tb = _tile(rh, 512, SUBLANES)
    nb = rh // tb

    def body(c_ref, a_ref, b_ref, o_ref):
        o_ref[...] = a_ref[...] + b_ref[...]

    return pl.pallas_call(
        body,
        grid_spec=pltpu.PrefetchScalarGridSpec(
            num_scalar_prefetch=1, grid=(K, nb),
            in_specs=[pl.BlockSpec((1, tb, C), lambda k, i, c: (k, c[0] * nb + i, 0)),
                      pl.BlockSpec((1, tb, C), lambda k, i, c: (k, i, 0))],
            out_specs=pl.BlockSpec((1, tb, C), lambda k, i, c: (k, i, 0))),
        out_shape=jax.ShapeDtypeStruct((K, rh, C), F32), name=name, compiler_params=_cp(),
    )(c_idx, g_all, recv)


def _sum_parts(parts, name):
    K, R, C = parts.shape
    tb = _tile(R, 512, SUBLANES)

    def body(p_ref, o_ref):
        g = p_ref[0]
        for k in range(1, K):
            g = g + p_ref[k]
        o_ref[...] = g

    return pl.pallas_call(
        body, grid=(R // tb,), in_specs=[pl.BlockSpec((K, tb, C), lambda i: (0, i, 0))],
        out_specs=pl.BlockSpec((tb, C), lambda i: (i, 0)),
        out_shape=jax.ShapeDtypeStruct((R, C), F32), name=name, compiler_params=_cp())(parts)


_CHIP_RELATIONS = ((1, 0), (0, 1), (1, 1))


def _position():
    return lax.axis_index("x"), lax.axis_index("y"), lax.axis_index("c")


def _flip(v, bit):
    return 1 - v if bit else v


def _gather_weights(w_pack, ln_pack, name):
    R, C = w_pack.shape

    def body(w_ref, ln_ref, ow_ref, oln_ref, local_sems, send_sems, recv_sems):
        x, y, c = _position()
        me = 2 * x + y
        own_w = pltpu.make_async_copy(w_ref, ow_ref.at[me], local_sems.at[0])
        own_ln = pltpu.make_async_copy(ln_ref, oln_ref.at[me], local_sems.at[1])
        own_w.start()
        own_ln.start()

        def copies(r, slot):
            dx, dy = _CHIP_RELATIONS[r]
            peer = (_flip(x, dx), _flip(y, dy), c)
            cw = pltpu.make_async_remote_copy(
                src_ref=w_ref, dst_ref=ow_ref.at[slot], send_sem=send_sems.at[2 * r],
                recv_sem=recv_sems.at[2 * r], device_id=peer, device_id_type=MESH)
            cl = pltpu.make_async_remote_copy(
                src_ref=ln_ref, dst_ref=oln_ref.at[slot], send_sem=send_sems.at[2 * r + 1],
                recv_sem=recv_sems.at[2 * r + 1], device_id=peer, device_id_type=MESH)
            return cw, cl

        sent = [copies(r, me) for r in range(3)]
        for cw, cl in sent:
            cw.start()
            cl.start()
        for r in range(3):
            dx, dy = _CHIP_RELATIONS[r]
            cw, cl = copies(r, 2 * _flip(x, dx) + _flip(y, dy))
            cw.wait_recv()
            cl.wait_recv()
        for cw, cl in sent:
            cw.wait_send()
            cl.wait_send()
        own_w.wait()
        own_ln.wait()

    return pl.pallas_call(
        body, in_specs=[_hbm(), _hbm()], out_specs=[_hbm(), _hbm()],
        out_shape=[jax.ShapeDtypeStruct((4, R, C), w_pack.dtype), jax.ShapeDtypeStruct((4, SUBLANES, C), F32)],
        scratch_shapes=[pltpu.SemaphoreType.DMA((2,)), pltpu.SemaphoreType.DMA((6,)),
                        pltpu.SemaphoreType.DMA((6,))],
        name=name, compiler_params=_cp(),
    )(w_pack, ln_pack)


def _sibling_halves(g_all, name):
    K, R, C = g_all.shape
    rh = R // 2

    def body(g_ref, o_ref, send_sem, recv_sem):
        x, y, c = _position()
        start = pl.multiple_of((1 - c) * rh, SUBLANES)
        cp = pltpu.make_async_remote_copy(
            src_ref=g_ref.at[:, pl.ds(start, rh), :], dst_ref=o_ref, send_sem=send_sem, recv_sem=recv_sem,
            device_id=(x, y, 1 - c), device_id_type=MESH)
        cp.start()
        cp.wait_recv()
        cp.wait_send()

    return pl.pallas_call(
        body, in_specs=[_hbm()], out_specs=_hbm(),
        out_shape=jax.ShapeDtypeStruct((K, rh, C), F32),
        scratch_shapes=[pltpu.SemaphoreType.DMA(()), pltpu.SemaphoreType.DMA(())],
        name=name, compiler_params=_cp(),
    )(g_all)


def _chip_exchange(parts, name):
    K, R, C = parts.shape

    def body(p_ref, o_ref, local_sem, send_sems, recv_sems):
        x, y, c = _position()
        me = 2 * x + y
        own = pltpu.make_async_copy(p_ref.at[me], o_ref.at[me], local_sem)
        own.start()

        def copy(r, src_slot, dst_slot):
            dx, dy = _CHIP_RELATIONS[r]
            return pltpu.make_async_remote_copy(
                src_ref=p_ref.at[src_slot], dst_ref=o_ref.at[dst_slot], send_sem=send_sems.at[r],
                recv_sem=recv_sems.at[r], device_id=(_flip(x, dx), _flip(y, dy), c), device_id_type=MESH)

        def chip(r):
            dx, dy = _CHIP_RELATIONS[r]
            return 2 * _flip(x, dx) + _flip(y, dy)

        sent = [copy(r, chip(r), me) for r in range(3)]
        for cp in sent:
            cp.start()
        for r in range(3):
            copy(r, me, chip(r)).wait_recv()
        for cp in sent:
            cp.wait_send()
        own.wait()

    return pl.pallas_call(
        body, in_specs=[_hbm()], out_specs=_hbm(),
        out_shape=jax.ShapeDtypeStruct((K, R, C), F32),
        scratch_shapes=[pltpu.SemaphoreType.DMA(()), pltpu.SemaphoreType.DMA((3,)),
                        pltpu.SemaphoreType.DMA((3,))],
        name=name, compiler_params=_cp(),
    )(parts)


def _join_halves(half, name):
    rh, C = half.shape

    def body(h_ref, o_ref, local_sem, send_sem, recv_sem):
        x, y, c = _position()
        mine = pl.ds(pl.multiple_of(c * rh, SUBLANES), rh)
        theirs = pl.ds(pl.multiple_of((1 - c) * rh, SUBLANES), rh)
        own = pltpu.make_async_copy(h_ref, o_ref.at[mine, :], local_sem)
        own.start()
        cp = pltpu.make_async_remote_copy(
            src_ref=h_ref, dst_ref=o_ref.at[mine, :], send_sem=send_sem, recv_sem=recv_sem,
            device_id=(x, y, 1 - c), device_id_type=MESH)
        cp.start()
        pltpu.make_async_remote_copy(
            src_ref=h_ref, dst_ref=o_ref.at[theirs, :], send_sem=send_sem, recv_sem=recv_sem,
            device_id=(x, y, 1 - c), device_id_type=MESH).wait_recv()
        cp.wait_send()
        own.wait()

    return pl.pallas_call(
        body, in_specs=[_hbm()], out_specs=_hbm(),
        out_shape=jax.ShapeDtypeStruct((2 * rh, C), F32),
        scratch_shapes=[pltpu.SemaphoreType.DMA(()), pltpu.SemaphoreType.DMA(()), pltpu.SemaphoreType.DMA(())],
        name=name, compiler_params=_cp(),
    )(half)


def _gather_all(part, name):
    R, C = part.shape
    masks = [(b >> 2 & 1, b >> 1 & 1, b & 1) for b in range(1, 8)]

    def body(p_ref, o_ref, local_sem, send_sems, recv_sems):
        x, y, c = _position()
        me = 4 * x + 2 * y + c
        own = pltpu.make_async_copy(p_ref, o_ref.at[me], local_sem)
        own.start()

        def copy(r, slot):
            dx, dy, dc = masks[r]
            return pltpu.make_async_remote_copy(
                src_ref=p_ref, dst_ref=o_ref.at[slot], send_sem=send_sems.at[r], recv_sem=recv_sems.at[r],
                device_id=(_flip(x, dx), _flip(y, dy), _flip(c, dc)), device_id_type=MESH)

        sent = [copy(r, me) for r in range(7)]
        for cp in sent:
            cp.start()
        for r in range(7):
            dx, dy, dc = masks[r]
            copy(r, 4 * _flip(x, dx) + 2 * _flip(y, dy) + _flip(c, dc)).wait_recv()
        for cp in sent:
            cp.wait_send()
        own.wait()

    return pl.pallas_call(
        body, in_specs=[_hbm()], out_specs=_hbm(),
        out_shape=jax.ShapeDtypeStruct((8, R, C), F32),
        scratch_shapes=[pltpu.SemaphoreType.DMA(()), pltpu.SemaphoreType.DMA((7,)),
                        pltpu.SemaphoreType.DMA((7,))],
        name=name, compiler_params=_cp(),
    )(part)


def _pack(arrs, row_mult):
    flat = jnp.concatenate([a.reshape(-1).astype(F32) for a in arrs])
    rows = -(-flat.shape[0] // PACK_COLS)
    rows = -(-rows // row_mult) * row_mult
    flat = jnp.pad(flat, (0, rows * PACK_COLS - flat.shape[0]))
    return flat.reshape(rows, PACK_COLS)


def _unpack(buf, shapes):
    lead = buf.shape[:-2]
    flat = buf.reshape(lead + (-1,))
    out, off = [], 0
    for shp in shapes:
        n = math.prod(shp)
        out.append(flat[..., off:off + n].reshape(lead + tuple(shp)))
        off += n
    return out


def _cols_from_chips(g):
    k, L, A, n = g.shape
    return jnp.transpose(g, (1, 2, 0, 3)).reshape(L, A, k * n)


def _rows_from_chips(g):
    k, L, n, B = g.shape
    return jnp.transpose(g, (1, 0, 2, 3)).reshape(L, k * n, B)


def _cols_to_chips(full, k=4):
    L, A, N = full.shape
    return jnp.transpose(full.reshape(L, A, k, N // k), (2, 0, 1, 3))


def _rows_to_chips(full, k=4):
    L, N, B = full.shape
    return jnp.transpose(full.reshape(L, k, N // k, B), (1, 0, 2, 3))


def kernel(x, mixer_norm_w, attn_w_in, attn_b_f, attn_w_out, sgu_w_in, sgu_ln_g, sgu_ln_b, sgu_w_s, sgu_b_s, sgu_w_out, ffn_norm_w, ffn_w_in, ffn_w_out, final_norm_w, loss_target, m_mixer_norm_w, m_attn_w_in, m_attn_b_f, m_attn_w_out, m_sgu_w_in, m_sgu_ln_g, m_sgu_ln_b, m_sgu_w_s, m_sgu_b_s, m_sgu_w_out, m_ffn_norm_w, m_ffn_w_in, m_ffn_w_out, m_final_norm_w, v_mixer_norm_w, v_attn_w_in, v_attn_b_f, v_attn_w_out, v_sgu_w_in, v_sgu_ln_g, v_sgu_ln_b, v_sgu_w_s, v_sgu_b_s, v_sgu_w_out, v_ffn_norm_w, v_ffn_w_in, v_ffn_w_out, v_final_norm_w):
    T, D = x.shape[1], x.shape[2]
    depth = mixer_norm_w.shape[0]
    H = attn_b_f.shape[1]
    P = D // LANES
    assert D % LANES == 0 and D // H == 64 and 2 * P == H and 2 * P <= LANES
    G = sgu_w_s.shape[1]
    W = sgu_w_out.shape[1] * 4
    assert sgu_w_s.shape[2] == LANES and W == G * LANES
    scale = float(D // H) ** -0.5
    f_pad = LANES
    c_idx = lax.axis_index("c").astype(jnp.int32).reshape(1)

    sharded = [attn_w_in, attn_w_out, sgu_w_in, sgu_w_out, ffn_w_in, ffn_w_out, sgu_ln_g, sgu_ln_b]
    sharded_m = [m_attn_w_in, m_attn_w_out, m_sgu_w_in, m_sgu_w_out, m_ffn_w_in, m_ffn_w_out, m_sgu_ln_g, m_sgu_ln_b]
    sharded_v = [v_attn_w_in, v_attn_w_out, v_sgu_w_in, v_sgu_w_out, v_ffn_w_in, v_ffn_w_out, v_sgu_ln_g, v_sgu_ln_b]
    shard_shapes = [a.shape for a in sharded]
    w_pack = _pack(sharded, 512)
    ln_pack = _pack([sgu_ln_g, sgu_ln_b], SUBLANES)
    gat_w, gat_ln = _gather_weights(w_pack.astype(BF16), ln_pack, "gather_weights")
    g_ai, g_ao, g_si, g_so, g_fi, g_fo, _, _ = _unpack(gat_w, shard_shapes)
    g_lng, g_lnb = _unpack(gat_ln, [sgu_ln_g.shape, sgu_ln_b.shape])
    w_ai = _cols_from_chips(g_ai)
    w_ai = jnp.pad(w_ai, ((0, 0), (0, 0), (0, 3 * D + f_pad - w_ai.shape[2])))
    w_ao = _rows_from_chips(g_ao)
    w_si = _cols_from_chips(g_si)
    w_so = _rows_from_chips(g_so)
    w_fi = _cols_from_chips(g_fi)
    w_fo = _rows_from_chips(g_fo)
    ln_g = jnp.transpose(g_lng, (1, 0, 2)).reshape(sgu_ln_g.shape[0], W)
    ln_b = jnp.transpose(g_lnb, (1, 0, 2)).reshape(sgu_ln_b.shape[0], W)
    w_tril = jnp.tril(sgu_w_s)
    w_tril_b = w_tril.astype(BF16)
    w_tril_tb = jnp.swapaxes(w_tril, 2, 3).astype(BF16)
    sgu_bias = jnp.repeat(jnp.swapaxes(sgu_b_s, 1, 2), LANES, axis=2)
    b_f_pad = jnp.pad(attn_b_f, ((0, 0), (0, LANES - H)))

    xs = x.reshape(T, D)
    saved = []
    for i in range(depth):
        j = i // 2
        h = _rmsnorm_fwd(xs, mixer_norm_w[i], f"mix_norm_{i}")
        rec = {"x_in": xs, "h": h}
        if i % 2 == 0:
            qkv = _mm(h, w_ai[j, :, :3 * D], "nn", BF16, f"attn_qkv_{i}")
            f = _mm(h, w_ai[j, :, 3 * D:], "nn", F32, f"attn_gate_{i}")
            cT, c_cols = _gate_fwd(f, b_f_pad[j:j + 1], P, f"gate_fwd_{i}")
            o, lseT = _attn_fwd(qkv, cT, P, scale, f"attn_fwd_{i}")
            x_mid = _mm(o, w_ao[j], "nn", F32, f"attn_out_{i}", res=xs)
            rec.update(qkv=qkv, f=f, cT=cT, c_cols=c_cols, o=o, lseT=lseT)
        else:
            a = _mm(h, w_si[j], "nn", BF16, f"sgu_in_{i}")
            gated = _sgu_fwd(a, ln_g[j], ln_b[j], w_tril_b[j], sgu_bias[j], f"sgu_fwd_{i}")
            x_mid = _mm(gated, w_so[j], "nn", F32, f"sgu_out_{i}", res=xs)
            rec.update(a=a, gated=gated)
        h2 = _rmsnorm_fwd(x_mid, ffn_norm_w[i], f"ffn_norm_{i}")
        fa = _mm(h2, w_fi[i], "nn", BF16, f"ffn_in_{i}")
        s = _silu_mul(fa, f"ffn_act_{i}")
        xs = _mm(s, w_fo[i], "nn", F32, f"ffn_out_{i}", res=x_mid)
        rec.update(x_mid=x_mid, h2=h2, fa=fa, s=s)
        saved.append(rec)

    gx, loss_acc, dw_final = _loss_head(xs, final_norm_w, loss_target.reshape(T, D), "loss_head")
    loss = lax.psum(loss_acc[0, 0], ("x", "y", "c"))

    n_attn, n_sgu = attn_w_in.shape[0], sgu_w_in.shape[0]
    d_mixer_norm, d_ffn_norm = [None] * depth, [None] * depth
    d_ai, d_ao, d_bf = [None] * n_attn, [None] * n_attn, [None] * n_attn
    d_si, d_so, d_lng, d_lnb, d_ws, d_bs = ([None] * n_sgu for _ in range(6))
    d_fi, d_fo = [None] * depth, [None] * depth
    for i in reversed(range(depth)):
        j = i // 2
        rec = saved[i]
        ds = _mm(gx, w_fo[i], "nt", F32, f"ffn_out_bwd_{i}")
        d_fo[i] = _mm(rec["s"], gx, "tn", F32, f"ffn_out_wgrad_{i}")
        da = _silu_mul_bwd(ds, rec["fa"], f"ffn_act_bwd_{i}")
        dh2 = _mm(da, w_fi[i], "nt", F32, f"ffn_in_bwd_{i}")
        d_fi[i] = _mm(rec["h2"], da, "tn", F32, f"ffn_in_wgrad_{i}")
        gx, dwn = _rmsnorm_bwd(dh2, rec["x_mid"], ffn_norm_w[i], gx, f"ffn_norm_bwd_{i}")
        d_ffn_norm[i] = dwn[0]
        if i % 2 == 0:
            do = _mm(gx, w_ao[j], "nt", BF16, f"attn_out_bwd_{i}")
            d_ao[j] = _mm(rec["o"], gx, "tn", F32, f"attn_out_wgrad_{i}")
            dT = _attn_delta(do, rec["o"], P, f"attn_delta_{i}")
            dq, dk, dv, dc_cols, drowT = _attn_bwd(rec["qkv"], do, rec["lseT"], dT, rec["cT"], rec["c_cols"],
                                                   P, scale, f"attn_bwd_{i}")
            df, dbf = _gate_bwd(dc_cols, drowT, rec["f"], b_f_pad[j:j + 1], P, f"gate_bwd_{i}")
            d_bf[j] = dbf[0, :H]
            dproj = jnp.concatenate([dq, dk, dv, df.astype(BF16)], axis=1)
            dh = _mm(dproj, w_ai[j], "nt", F32, f"attn_in_bwd_{i}")
            d_ai[j] = _mm(rec["h"], dproj, "tn", F32, f"attn_in_wgrad_{i}")[:, :3 * D + H]
        else:
            dgated = _mm(gx, w_so[j], "nt", F32, f"sgu_out_bwd_{i}")
            d_so[j] = _mm(rec["gated"], gx, "tn", F32, f"sgu_out_wgrad_{i}")
            da_s, dws, dbias, dlng, dlnb = _sgu_bwd(rec["a"], dgated, ln_g[j], ln_b[j], w_tril_b[j],
                                                    w_tril_tb[j], sgu_bias[j], f"sgu_bwd_{i}")
            d_ws[j] = jnp.tril(dws)
            d_bs[j] = jnp.sum(dbias.reshape(LANES, G, LANES), axis=2).T
            d_lng[j], d_lnb[j] = dlng[0], dlnb[0]
            dh = _mm(da_s, w_si[j], "nt", F32, f"sgu_in_bwd_{i}")
            d_si[j] = _mm(rec["h"], da_s, "tn", F32, f"sgu_in_wgrad_{i}")
        gx, dwn = _rmsnorm_bwd(dh, rec["x_in"], mixer_norm_w[i], gx, f"mix_norm_bwd_{i}")
        d_mixer_norm[i] = dwn[0]
    grad_x = gx.reshape(x.shape)

    full_grads = [
        _cols_to_chips(jnp.stack(d_ai)), _rows_to_chips(jnp.stack(d_ao)),
        _cols_to_chips(jnp.stack(d_si)), _rows_to_chips(jnp.stack(d_so)),
        _cols_to_chips(jnp.stack(d_fi)), _rows_to_chips(jnp.stack(d_fo)),
        jnp.transpose(jnp.stack(d_lng).reshape(n_sgu, 4, W // 4), (1, 0, 2)),
        jnp.transpose(jnp.stack(d_lnb).reshape(n_sgu, 4, W // 4), (1, 0, 2)),
    ]
    g_all = jnp.stack([_pack([fg[k] for fg in full_grads], 512) for k in range(4)])
    from_sibling = _sibling_halves(g_all, "grad_sibling_halves")
    pair = _pair_sum(g_all, from_sibling, c_idx, "grad_pair_sum")
    from_chips = _chip_exchange(pair, "grad_chip_exchange")
    my_half = _sum_parts(from_chips, "grad_chip_sum")
    g_shard = _join_halves(my_half, "grad_join_halves")
    d_shard, m_shard, v_shard = _adamw(g_shard, w_pack, _pack(sharded_m, 512), _pack(sharded_v, 512), "adamw_sharded")
    g_sh = _unpack(g_shard, shard_shapes)
    d_sh = _unpack(d_shard, shard_shapes)
    m_sh = _unpack(m_shard, shard_shapes)
    v_sh = _unpack(v_shard, shard_shapes)

    repl = [mixer_norm_w, attn_b_f, sgu_w_s, sgu_b_s, ffn_norm_w, final_norm_w]
    repl_m = [m_mixer_norm_w, m_attn_b_f, m_sgu_w_s, m_sgu_b_s, m_ffn_norm_w, m_final_norm_w]
    repl_v = [v_mixer_norm_w, v_attn_b_f, v_sgu_w_s, v_sgu_b_s, v_ffn_norm_w, v_final_norm_w]
    repl_shapes = [a.shape for a in repl]
    repl_grads = [jnp.stack(d_mixer_norm), jnp.stack(d_bf), jnp.stack(d_ws), jnp.stack(d_bs),
                  jnp.stack(d_ffn_norm), dw_final[0]]
    parts = _gather_all(_pack(repl_grads, SUBLANES), "grad_gather_replicated")
    g_rep, d_rep, m_rep, v_rep = _adamw_sum(parts, _pack(repl, SUBLANES), _pack(repl_m, SUBLANES),
                                            _pack(repl_v, SUBLANES), "adamw_replicated")
    g_r = _unpack(g_rep, repl_shapes)
    d_r = _unpack(d_rep, repl_shapes)
    m_r = _unpack(m_rep, repl_shapes)
    v_r = _unpack(v_rep, repl_shapes)

    def ordered(sh, rp):
        ai, ao, si, so, fi, fo, lng, lnb = sh
        mn, bf, ws, bs, fn, fin = rp
        return [mn, ai, bf, ao, si, lng, lnb, ws, bs, so, fn, fi, fo, fin]

    return (loss, grad_x, *ordered(g_sh, g_r), *ordered(d_sh, d_r), *ordered(m_sh, m_r), *ordered(v_sh, v_r))
```

```python
import functools
import math

import jax
import jax.numpy as jnp
from jax import lax
from jax.experimental import pallas as pl
from jax.experimental.pallas import tpu as pltpu

F32 = jnp.float32
BF16 = jnp.bfloat16
NORM_EPS = 1e-6
LN_EPS = 1e-5
ADAM_LR = 0.001
ADAM_B1 = 0.9
ADAM_B2 = 0.999
ADAM_EPS = 1e-08
ADAM_WD = 0.01
ADAM_STEP = 10

LANES = 128
SUBLANES = 8
PACK_COLS = 1024
VMEM_LIMIT = 56 * 1024 * 1024
NEG_BIG = -1e30
MESH = pl.DeviceIdType.MESH


def _cp():
    return pltpu.CompilerParams(vmem_limit_bytes=VMEM_LIMIT)


def _tile(n, cap, mult):
    best = None
    d = mult
    while d <= min(n, cap):
        if n % d == 0:
            best = d
        d += mult
    return n if best is None else best


def _hbm():
    return pl.BlockSpec(memory_space=pltpu.HBM)


def _rmsnorm_fwd(x, w, name):
    T, D = x.shape
    tm = _tile(T, 512, SUBLANES)

    def body(x_ref, w_ref, h_ref):
        xf = x_ref[...]
        r = lax.rsqrt(jnp.mean(xf * xf, axis=-1, keepdims=True) + NORM_EPS)
        h_ref[...] = (xf * r * w_ref[...]).astype(BF16)

    return pl.pallas_call(
        body, grid=(T // tm,),
        in_specs=[pl.BlockSpec((tm, D), lambda i: (i, 0)), pl.BlockSpec((1, D), lambda i: (0, 0))],
        out_specs=pl.BlockSpec((tm, D), lambda i: (i, 0)),
        out_shape=jax.ShapeDtypeStruct((T, D), BF16), name=name, compiler_params=_cp(),
    )(x, w.reshape(1, D))


def _rmsnorm_bwd(dh, x, w, dres, name):
    T, D = x.shape
    tm = _tile(T, 512, SUBLANES)

    def body(dh_ref, x_ref, w_ref, dres_ref, dx_ref, dw_ref):
        @pl.when(pl.program_id(0) == 0)
        def _():
            dw_ref[...] = jnp.zeros_like(dw_ref)

        xf = x_ref[...]
        r = lax.rsqrt(jnp.mean(xf * xf, axis=-1, keepdims=True) + NORM_EPS)
        xhat = xf * r
        dhv = dh_ref[...]
        dxhat = dhv * w_ref[...]
        dx_ref[...] = dres_ref[...] + r * (dxhat - xhat * jnp.mean(dxhat * xhat, axis=-1, keepdims=True))
        dw_ref[...] += jnp.sum(dhv * xhat, axis=0, keepdims=True)

    row = pl.BlockSpec((tm, D), lambda i: (i, 0))
    return pl.pallas_call(
        body, grid=(T // tm,),
        in_specs=[row, row, pl.BlockSpec((1, D), lambda i: (0, 0)), row],
        out_specs=[row, pl.BlockSpec((SUBLANES, D), lambda i: (0, 0))],
        out_shape=[jax.ShapeDtypeStruct((T, D), F32), jax.ShapeDtypeStruct((SUBLANES, D), F32)],
        name=name, compiler_params=_cp(),
    )(dh, x, w.reshape(1, D), dres)


def _mm(a, b, mode, out_dtype, name, res=None):
    if mode == "tn":
        kt, M = a.shape
        N = b.shape[1]
        tm = _tile(M, 1408, LANES)
        tn = _tile(N, 1408, LANES)
        tk = _tile(kt, 512, 16)

        def body(a_ref, b_ref, o_ref):
            @pl.when(pl.program_id(2) == 0)
            def _():
                o_ref[...] = jnp.zeros_like(o_ref)

            o_ref[...] += lax.dot_general(
                a_ref[...].astype(BF16), b_ref[...].astype(BF16), (((0,), (0,)), ((), ())),
                preferred_element_type=F32)

        return pl.pallas_call(
            body, grid=(M // tm, N // tn, kt // tk),
            in_specs=[pl.BlockSpec((tk, tm), lambda i, j, k: (k, i)),
                      pl.BlockSpec((tk, tn), lambda i, j, k: (k, j))],
            out_specs=pl.BlockSpec((tm, tn), lambda i, j, k: (i, j)),
            out_shape=jax.ShapeDtypeStruct((M, N), F32), name=name, compiler_params=_cp(),
        )(a, b)

    M, K = a.shape
    N = b.shape[1] if mode == "nn" else b.shape[0]
    tm = _tile(M, 512, 16)
    cap = max(LANES, min(1408, ((6 << 20) // (2 * K)) // LANES * LANES))
    tn = _tile(N, cap, LANES)
    dims = (((1,), (0,)), ((), ())) if mode == "nn" else (((1,), (1,)), ((), ()))

    def body(*refs):
        if res is None:
            a_ref, b_ref, o_ref = refs
        else:
            a_ref, b_ref, r_ref, o_ref = refs
        acc = lax.dot_general(a_ref[...].astype(BF16), b_ref[...].astype(BF16), dims,
                              preferred_element_type=F32)
        if res is not None:
            acc = acc + r_ref[...]
        o_ref[...] = acc.astype(out_dtype)

    b_spec = (pl.BlockSpec((K, tn), lambda i, j: (0, j)) if mode == "nn"
              else pl.BlockSpec((tn, K), lambda i, j: (j, 0)))
    in_specs = [pl.BlockSpec((tm, K), lambda i, j: (i, 0)), b_spec]
    args = [a, b]
    if res is not None:
        in_specs.append(pl.BlockSpec((tm, tn), lambda i, j: (i, j)))
        args.append(res)
    return pl.pallas_call(
        body, grid=(M // tm, N // tn), in_specs=in_specs,
        out_specs=pl.BlockSpec((tm, tn), lambda i, j: (i, j)),
        out_shape=jax.ShapeDtypeStruct((M, N), out_dtype), name=name, compiler_params=_cp(),
    )(*args)


def _silu_mul(a, name):
    T, F2 = a.shape
    F = F2 // 2
    tm = _tile(T, 256, 16)

    def body(a_ref, s_ref):
        g = a_ref[:, :F].astype(F32)
        u = a_ref[:, F:].astype(F32)
        s_ref[...] = (g * jax.nn.sigmoid(g) * u).astype(BF16)

    return pl.pallas_call(
        body, grid=(T // tm,),
        in_specs=[pl.BlockSpec((tm, F2), lambda i: (i, 0))],
        out_specs=pl.BlockSpec((tm, F), lambda i: (i, 0)),
        out_shape=jax.ShapeDtypeStruct((T, F), BF16), name=name, compiler_params=_cp(),
    )(a)


def _silu_mul_bwd(ds, a, name):
    T, F2 = a.shape
    F = F2 // 2
    tm = _tile(T, 256, 16)

    def body(ds_ref, a_ref, da_ref):
        g = a_ref[:, :F].astype(F32)
        u = a_ref[:, F:].astype(F32)
        dsv = ds_ref[...]
        sg = jax.nn.sigmoid(g)
        da_ref[:, :F] = (dsv * u * (sg * (1.0 + g * (1.0 - sg)))).astype(BF16)
        da_ref[:, F:] = (dsv * (g * sg)).astype(BF16)

    return pl.pallas_call(
        body, grid=(T // tm,),
        in_specs=[pl.BlockSpec((tm, F), lambda i: (i, 0)), pl.BlockSpec((tm, F2), lambda i: (i, 0))],
        out_specs=pl.BlockSpec((tm, F2), lambda i: (i, 0)),
        out_shape=jax.ShapeDtypeStruct((T, F2), BF16), name=name, compiler_params=_cp(),
    )(ds, a)


def _loss_head(x, w, tgt, name):
    T, D = x.shape
    tm = _tile(T, 512, SUBLANES)

    def body(x_ref, w_ref, t_ref, dx_ref, loss_ref, dw_ref):
        @pl.when(pl.program_id(0) == 0)
        def _():
            loss_ref[...] = jnp.zeros_like(loss_ref)
            dw_ref[...] = jnp.zeros_like(dw_ref)

        xf = x_ref[...]
        wv = w_ref[...]
        r = lax.rsqrt(jnp.mean(xf * xf, axis=-1, keepdims=True) + NORM_EPS)
        xhat = xf * r
        err = xhat * wv - t_ref[...]
        per_tok = jnp.mean(err * err, axis=-1, keepdims=True)
        loss_ref[...] += 0.5 * jnp.sum(per_tok, axis=0, keepdims=True)
        dy = err * (1.0 / D)
        dxhat = dy * wv
        dx_ref[...] = r * (dxhat - xhat * jnp.mean(dxhat * xhat, axis=-1, keepdims=True))
        dw_ref[...] += jnp.sum(dy * xhat, axis=0, keepdims=True)

    row = pl.BlockSpec((tm, D), lambda i: (i, 0))
    return pl.pallas_call(
        body, grid=(T // tm,),
        in_specs=[row, pl.BlockSpec((1, D), lambda i: (0, 0)), row],
        out_specs=[row, pl.BlockSpec((SUBLANES, LANES), lambda i: (0, 0)),
                   pl.BlockSpec((SUBLANES, D), lambda i: (0, 0))],
        out_shape=[jax.ShapeDtypeStruct((T, D), F32), jax.ShapeDtypeStruct((SUBLANES, LANES), F32),
                   jax.ShapeDtypeStruct((SUBLANES, D), F32)],
        name=name, compiler_params=_cp(),
    )(x, w.reshape(1, D), tgt)


def _split3(v):
    hi = v.astype(BF16)
    r1 = v - hi.astype(F32)
    mid = r1.astype(BF16)
    lo = (r1 - mid.astype(F32)).astype(BF16)
    return hi, mid, lo


def _tri_dot(tri, v):
    out = None
    for piece in _split3(v):
        t = jnp.dot(tri, piece, preferred_element_type=F32)
        out = t if out is None else out + t
    return out


def _q_block(T):
    return _tile(T, 256, LANES)


def _gate_fwd(f, b_f, P, name):
    T = f.shape[0]
    tb = _q_block(T)

    def body(f_ref, b_ref, ct_ref, cc_ref, c0_ref, carry):
        @pl.when(pl.program_id(0) == 0)
        def _():
            carry[...] = jnp.zeros_like(carry)

        z = f_ref[...] + b_ref[...]
        logf = jnp.minimum(z, 0.0) - jnp.log(1.0 + jnp.exp(-jnp.abs(z)))
        row = lax.broadcasted_iota(jnp.int32, (tb, tb), 0)
        col = lax.broadcasted_iota(jnp.int32, (tb, tb), 1)
        tri = (col <= row).astype(BF16)
        c = _tri_dot(tri, logf) + carry[0:1, :]
        carry[...] = jnp.broadcast_to(c[tb - 1:tb, :], carry.shape)
        first = jnp.broadcast_to(c[0:1, :], c.shape)
        for p in range(P):
            shifted = c if p == 0 else pltpu.roll(c, LANES - 2 * p, 1)
            cc_ref[p] = shifted
            ct_ref[p] = shifted.T[0:SUBLANES, :]
            c0_ref[p] = (first if p == 0 else pltpu.roll(first, LANES - 2 * p, 1)).T[0:SUBLANES, :]

    rows = pl.BlockSpec((P, SUBLANES, tb), lambda i: (0, 0, i))
    return pl.pallas_call(
        body, grid=(T // tb,),
        in_specs=[pl.BlockSpec((tb, LANES), lambda i: (i, 0)), pl.BlockSpec((1, LANES), lambda i: (0, 0))],
        out_specs=[rows, pl.BlockSpec((P, tb, LANES), lambda i: (0, i, 0)), rows],
        out_shape=[jax.ShapeDtypeStruct((P, SUBLANES, T), F32), jax.ShapeDtypeStruct((P, T, LANES), F32),
                   jax.ShapeDtypeStruct((P, SUBLANES, T), F32)],
        scratch_shapes=[pltpu.VMEM((SUBLANES, LANES), F32)],
        name=name, compiler_params=_cp(),
    )(f, b_f)


def _gate_bwd(dc_cols, drowT, f, b_f, P, name):
    T = f.shape[0]
    tb = _tile(T, 256, LANES)
    nb = T // tb

    def body(dc_ref, dr_ref, f_ref, b_ref, df_ref, db_ref, carry):
        @pl.when(pl.program_id(0) == 0)
        def _():
            carry[...] = jnp.zeros_like(carry)
            db_ref[...] = jnp.zeros_like(db_ref)

        lane = lax.broadcasted_iota(jnp.int32, (tb, LANES), 1)
        dc = jnp.zeros((tb, LANES), F32)
        for p in range(P):
            rows = jnp.concatenate([dr_ref[p], jnp.zeros((LANES - SUBLANES, tb), F32)], axis=0)
            part = jnp.where(lane < 2, dc_ref[p] + rows.T, 0.0)
            dc = dc + (part if p == 0 else pltpu.roll(part, 2 * p, 1))
        row = lax.broadcasted_iota(jnp.int32, (tb, tb), 0)
        col = lax.broadcasted_iota(jnp.int32, (tb, tb), 1)
        tri = (col >= row).astype(BF16)
        dlogf = _tri_dot(tri, dc) + carry[0:1, :]
        carry[...] = jnp.broadcast_to(dlogf[0:1, :], carry.shape)
        z = f_ref[...] + b_ref[...]
        df = jnp.where(lane < 2 * P, dlogf * jax.nn.sigmoid(-z), 0.0)
        df_ref[...] = df
        db_ref[...] += jnp.sum(df, axis=0, keepdims=True)

    return pl.pallas_call(
        body, grid=(nb,),
        in_specs=[pl.BlockSpec((P, tb, LANES), lambda i: (0, nb - 1 - i, 0)),
                  pl.BlockSpec((P, SUBLANES, tb), lambda i: (0, 0, nb - 1 - i)),
                  pl.BlockSpec((tb, LANES), lambda i: (nb - 1 - i, 0)),
                  pl.BlockSpec((1, LANES), lambda i: (0, 0))],
        out_specs=[pl.BlockSpec((tb, LANES), lambda i: (nb - 1 - i, 0)),
                   pl.BlockSpec((SUBLANES, LANES), lambda i: (0, 0))],
        out_shape=[jax.ShapeDtypeStruct((T, LANES), F32), jax.ShapeDtypeStruct((SUBLANES, LANES), F32)],
        scratch_shapes=[pltpu.VMEM((SUBLANES, LANES), F32)],
        name=name, compiler_params=_cp(),
    )(dc_cols, drowT, f, b_f)


def _nt(a, b):
    return lax.dot_general(a, b, (((1,), (1,)), ((), ())), preferred_element_type=F32)


def _attn_fwd(qkv, cT, P, scale, name):
    T = qkv.shape[0]
    tq = _q_block(T)
    tw = _tile(T, 4 * tq, tq)
    nq = T // tq

    def body(q_ref, k_ref, v_ref, c_ref, o_ref, lse_ref):
        i = pl.program_id(1)
        lane = lax.broadcasted_iota(jnp.int32, (1, LANES), 1)
        q = (q_ref[...].astype(F32) * scale).astype(BF16)
        q_heads = (jnp.where(lane < 64, q, jnp.zeros_like(q)), jnp.where(lane >= 64, q, jnp.zeros_like(q)))
        c0 = c_ref[0, :, pl.ds(pl.multiple_of(i * tq, tq), LANES)][:, 0:1]

        def block(start, width, carry, masked):
            k = k_ref[pl.ds(start, width), :]
            v = v_ref[pl.ds(start, width), :]
            bias = c0 - c_ref[0, :, pl.ds(start, width)]
            new = []
            for a in range(2):
                m, l, acc = carry[a]
                s = _nt(q_heads[a], k) + bias[a:a + 1, :]
                if masked:
                    row = lax.broadcasted_iota(jnp.int32, (tq, width), 0)
                    col = lax.broadcasted_iota(jnp.int32, (tq, width), 1)
                    s = jnp.where(col <= row, s, NEG_BIG)
                m_new = jnp.maximum(m, jnp.max(s, axis=1, keepdims=True))
                alpha = jnp.exp(m - m_new)
                p = jnp.exp(s - m_new)
                l = alpha * l + jnp.sum(p, axis=1, keepdims=True)
                acc = alpha * acc + jnp.dot(p.astype(BF16), v, preferred_element_type=F32)
                new.append((m_new, l, acc))
            return tuple(new)

        init = tuple((jnp.full((tq, 1), NEG_BIG, F32), jnp.zeros((tq, 1), F32), jnp.zeros((tq, LANES), F32))
                     for _ in range(2))
        n_wide = (i * tq) // tw
        carry = lax.fori_loop(
            0, n_wide, lambda j, cr: block(pl.multiple_of(j * tw, tw), tw, cr, False), init)
        carry = lax.fori_loop(
            n_wide * (tw // tq), i, lambda j, cr: block(pl.multiple_of(j * tq, tq), tq, cr, False), carry)
        (m0, l0, a0), (m1, l1, a1) = block(pl.multiple_of(i * tq, tq), tq, carry, True)
        o_ref[...] = jnp.where(lane < 64, a0 / l0, a1 / l1).astype(BF16)
        lse = jnp.where(lane == 0, m0 + jnp.log(l0), jnp.where(lane == 1, m1 + jnp.log(l1), 0.0))
        lse_ref[0] = lse.T[0:SUBLANES, :]

    return pl.pallas_call(
        body, grid=(P, nq),
        in_specs=[pl.BlockSpec((tq, LANES), lambda p, i: (i, p)),
                  pl.BlockSpec((T, LANES), lambda p, i: (0, P + p)),
                  pl.BlockSpec((T, LANES), lambda p, i: (0, 2 * P + p)),
                  pl.BlockSpec((1, SUBLANES, T), lambda p, i: (p, 0, 0))],
        out_specs=[pl.BlockSpec((tq, LANES), lambda p, i: (i, p)),
                   pl.BlockSpec((1, SUBLANES, tq), lambda p, i: (p, 0, i))],
        out_shape=[jax.ShapeDtypeStruct((T, LANES * P), BF16), jax.ShapeDtypeStruct((P, SUBLANES, T), F32)],
        name=name, compiler_params=_cp(),
    )(qkv, qkv, qkv, cT)


def _attn_delta(do, o, P, name):
    T, D = o.shape
    tb = _tile(T, 256, LANES)

    def body(do_ref, o_ref, d_ref):
        lane = lax.broadcasted_iota(jnp.int32, (1, LANES), 1)
        for p in range(P):
            cols = slice(p * LANES, (p + 1) * LANES)
            prod = do_ref[:, cols].astype(F32) * o_ref[:, cols].astype(F32)
            d0 = jnp.sum(jnp.where(lane < 64, prod, 0.0), axis=1, keepdims=True)
            d1 = jnp.sum(jnp.where(lane >= 64, prod, 0.0), axis=1, keepdims=True)
            both = jnp.where(lane == 0, d0, jnp.where(lane == 1, d1, 0.0))
            d_ref[p] = both.T[0:SUBLANES, :]

    return pl.pallas_call(
        body, grid=(T // tb,),
        in_specs=[pl.BlockSpec((tb, D), lambda i: (i, 0)), pl.BlockSpec((tb, D), lambda i: (i, 0))],
        out_specs=pl.BlockSpec((P, SUBLANES, tb), lambda i: (0, 0, i)),
        out_shape=jax.ShapeDtypeStruct((P, SUBLANES, T), F32), name=name, compiler_params=_cp(),
    )(do, o)


def _attn_bwd(qkv, do, lseT, dT, c0T, c_cols, P, scale, name):
    T = qkv.shape[0]
    tq = _q_block(T)
    tw = _tile(T, 4 * tq, tq)
    nq = T // tq
    r = tw // tq

    def body(q_ref, do_ref, k_ref, v_ref, lse_ref, d_ref, c0_ref, cc_ref,
             dq_ref, dk_ref, dv_ref, dc_ref, drow_ref, dq_acc):
        j = pl.program_id(1)

        @pl.when(j == 0)
        def _():
            dq_acc[...] = jnp.zeros_like(dq_acc)
            drow_ref[...] = jnp.zeros_like(drow_ref)

        lane = lax.broadcasted_iota(jnp.int32, (1, LANES), 1)
        k = k_ref[...]
        v = v_ref[...]
        zero = jnp.zeros_like(k)
        k_heads = (jnp.where(lane < 64, k, zero), jnp.where(lane >= 64, k, zero))
        v_heads = (jnp.where(lane < 64, v, zero), jnp.where(lane >= 64, v, zero))
        cc = cc_ref[0]
        c_first = (cc[0:1, 0:1], cc[0:1, 1:2])
        c_rel = (cc[:, 0:1] - c_first[0], cc[:, 1:2] - c_first[1])

        def block(start, width, carry, masked):
            q = (q_ref[pl.ds(start, width), :].astype(F32) * scale).astype(BF16)
            dov = do_ref[pl.ds(start, width), :]
            lse = lse_ref[0, :, pl.ds(start, width)]
            dlt = d_ref[0, :, pl.ds(start, width)]
            c0 = c0_ref[0, :, pl.ds(start, width)]
            dq_blk = jnp.zeros((width, LANES), F32)
            new = []
            for a in range(2):
                dk_a, dv_a, dcol = carry[a]
                rowv = lse[a:a + 1, :] + (c_first[a] - c0[a:a + 1, :])
                st = _nt(k_heads[a], q)
                pt = jnp.exp((st - c_rel[a]) - rowv)
                if masked:
                    row = lax.broadcasted_iota(jnp.int32, (tq, width), 0)
                    col = lax.broadcasted_iota(jnp.int32, (tq, width), 1)
                    pt = jnp.where(col >= row, pt, 0.0)
                dpt = _nt(v_heads[a], dov)
                dst = pt * (dpt - dlt[a:a + 1, :])
                dst_b = dst.astype(BF16)
                dv_a = dv_a + jnp.dot(pt.astype(BF16), dov, preferred_element_type=F32)
                dk_a = dk_a + jnp.dot(dst_b, q, preferred_element_type=F32)
                dq_blk = dq_blk + lax.dot_general(dst_b, k_heads[a], (((0,), (0,)), ((), ())),
                                                  preferred_element_type=F32)
                drow_ref[0, a:a + 1, pl.ds(start, width)] += jnp.sum(dst, axis=0, keepdims=True)
                new.append((dk_a, dv_a, dcol + jnp.sum(dst, axis=1, keepdims=True)))
            dq_acc[pl.ds(start, width), :] += dq_blk
            return tuple(new)

        init = tuple((jnp.zeros((tq, LANES), F32), jnp.zeros((tq, LANES), F32), jnp.zeros((tq, 1), F32))
                     for _ in range(2))
        carry = block(pl.multiple_of(j * tq, tq), tq, init, True)
        first_wide = (j + r) // r
        carry = lax.fori_loop(
            j + 1, jnp.minimum(first_wide * r, nq),
            lambda i, cr: block(pl.multiple_of(i * tq, tq), tq, cr, False), carry)
        (dk0, dv0, ds0), (dk1, dv1, ds1) = lax.fori_loop(
            first_wide, nq // r, lambda i, cr: block(pl.multiple_of(i * tw, tw), tw, cr, False), carry)
        dk_ref[...] = jnp.where(lane < 64, dk0, dk1).astype(BF16)
        dv_ref[...] = jnp.where(lane < 64, dv0, dv1).astype(BF16)
        dc_ref[0] = jnp.where(lane == 0, -ds0, jnp.where(lane == 1, -ds1, 0.0))

        @pl.when(j == nq - 1)
        def _():
            dq_ref[...] = (dq_acc[...] * scale).astype(BF16)

    full = lambda col: pl.BlockSpec((T, LANES), lambda p, j: (0, col(p)))
    blk = lambda col: pl.BlockSpec((tq, LANES), lambda p, j: (j, col(p)))
    rows = pl.BlockSpec((1, SUBLANES, T), lambda p, j: (p, 0, 0))
    cols = pl.BlockSpec((1, tq, LANES), lambda p, j: (p, j, 0))
    D = LANES * P
    return pl.pallas_call(
        body, grid=(P, nq),
        in_specs=[full(lambda p: p), full(lambda p: p), blk(lambda p: P + p), blk(lambda p: 2 * P + p),
                  rows, rows, rows, cols],
        out_specs=[full(lambda p: p), blk(lambda p: p), blk(lambda p: p), cols, rows],
        out_shape=[jax.ShapeDtypeStruct((T, D), BF16), jax.ShapeDtypeStruct((T, D), BF16),
                   jax.ShapeDtypeStruct((T, D), BF16), jax.ShapeDtypeStruct((P, T, LANES), F32),
                   jax.ShapeDtypeStruct((P, SUBLANES, T), F32)],
        scratch_shapes=[pltpu.VMEM((T, LANES), F32)],
        name=name, compiler_params=_cp(),
    )(qkv, do, qkv, qkv, lseT, dT, c0T, c_cols)


_SQRT_HALF = 0.7071067811865476
_INV_SQRT_2PI = 0.3989422804014327


def _gelu(v):
    return 0.5 * v * (1.0 + lax.erf(v * _SQRT_HALF))


def _gelu_grad(v):
    return 0.5 * (1.0 + lax.erf(v * _SQRT_HALF)) + v * (_INV_SQRT_2PI * jnp.exp(-0.5 * v * v))


def _sgu_fwd(a, ln_g, ln_b, w_tril, bias, name):
    T, W2 = a.shape
    W = W2 // 2
    G = w_tril.shape[0]
    tb = _tile(T, 256, LANES)

    def body(a_ref, g_ref, b_ref, w_ref, bias_ref, out_ref):
        zu = _gelu(a_ref[:, :W].astype(F32))
        zv = _gelu(a_ref[:, W:].astype(F32))
        mu = jnp.mean(zv, axis=-1, keepdims=True)
        d = zv - mu
        rstd = lax.rsqrt(jnp.mean(d * d, axis=-1, keepdims=True) + LN_EPS)
        vn = (d * rstd * g_ref[...] + b_ref[...]).astype(BF16)
        for c in range(tb // LANES):
            rs = slice(c * LANES, (c + 1) * LANES)
            for g in range(G):
                cs = slice(g * LANES, (g + 1) * LANES)
                mixed = jnp.dot(w_ref[g], vn[rs, cs], preferred_element_type=F32) + bias_ref[:, cs]
                out_ref[rs, cs] = (zu[rs, cs] * mixed).astype(BF16)

    return pl.pallas_call(
        body, grid=(T // tb,),
        in_specs=[pl.BlockSpec((tb, W2), lambda i: (i, 0)), pl.BlockSpec((1, W), lambda i: (0, 0)),
                  pl.BlockSpec((1, W), lambda i: (0, 0)), pl.BlockSpec((G, LANES, LANES), lambda i: (0, 0, 0)),
                  pl.BlockSpec((LANES, W), lambda i: (0, 0))],
        out_specs=pl.BlockSpec((tb, W), lambda i: (i, 0)),
        out_shape=jax.ShapeDtypeStruct((T, W), BF16), name=name, compiler_params=_cp(),
    )(a, ln_g.reshape(1, W), ln_b.reshape(1, W), w_tril, bias)


def _sgu_bwd(a, dgated, ln_g, ln_b, w_tril, w_tril_t, bias, name):
    T, W2 = a.shape
    W = W2 // 2
    G = w_tril.shape[0]
    tb = _tile(T, 256, LANES)

    def body(a_ref, dg_ref, g_ref, b_ref, w_ref, wt_ref, bias_ref,
             da_ref, dws_ref, dbias_ref, dlng_ref, dlnb_ref, dvn_ref):
        @pl.when(pl.program_id(0) == 0)
        def _():
            dws_ref[...] = jnp.zeros_like(dws_ref)
            dbias_ref[...] = jnp.zeros_like(dbias_ref)
            dlng_ref[...] = jnp.zeros_like(dlng_ref)
            dlnb_ref[...] = jnp.zeros_like(dlnb_ref)

        up = a_ref[:, :W].astype(F32)
        vp = a_ref[:, W:].astype(F32)
        zu = _gelu(up)
        zv = _gelu(vp)
        mu = jnp.mean(zv, axis=-1, keepdims=True)
        d = zv - mu
        rstd = lax.rsqrt(jnp.mean(d * d, axis=-1, keepdims=True) + LN_EPS)
        vhat = d * rstd
        gam = g_ref[...]
        vn = (vhat * gam + b_ref[...]).astype(BF16)
        dgated = dg_ref[...]
        for c in range(tb // LANES):
            rs = slice(c * LANES, (c + 1) * LANES)
            for g in range(G):
                cs = slice(g * LANES, (g + 1) * LANES)
                vb = vn[rs, cs]
                mixed = jnp.dot(w_ref[g], vb, preferred_element_type=F32) + bias_ref[:, cs]
                dgt = dgated[rs, cs]
                da_ref[rs, cs] = (dgt * mixed * _gelu_grad(up[rs, cs])).astype(BF16)
                dmx = dgt * zu[rs, cs]
                dbias_ref[:, cs] += dmx
                dmb = dmx.astype(BF16)
                dws_ref[g] += _nt(dmb, vb)
                dvn_ref[rs, cs] = jnp.dot(wt_ref[g], dmb, preferred_element_type=F32)
        dvn = dvn_ref[...]
        dlng_ref[...] += jnp.sum(dvn * vhat, axis=0, keepdims=True)
        dlnb_ref[...] += jnp.sum(dvn, axis=0, keepdims=True)
        dvh = dvn * gam
        dzv = rstd * (dvh - jnp.mean(dvh, axis=-1, keepdims=True)
                      - vhat * jnp.mean(dvh * vhat, axis=-1, keepdims=True))
        da_ref[:, W:] = (dzv * _gelu_grad(vp)).astype(BF16)

    const2 = lambda shape: pl.BlockSpec(shape, lambda i: (0, 0))
    const3 = pl.BlockSpec((G, LANES, LANES), lambda i: (0, 0, 0))
    return pl.pallas_call(
        body, grid=(T // tb,),
        in_specs=[pl.BlockSpec((tb, W2), lambda i: (i, 0)), pl.BlockSpec((tb, W), lambda i: (i, 0)),
                  const2((1, W)), const2((1, W)), const3, const3, const2((LANES, W))],
        out_specs=[pl.BlockSpec((tb, W2), lambda i: (i, 0)), const3, const2((LANES, W)),
                   const2((SUBLANES, W)), const2((SUBLANES, W))],
        out_shape=[jax.ShapeDtypeStruct((T, W2), BF16), jax.ShapeDtypeStruct((G, LANES, LANES), F32),
                   jax.ShapeDtypeStruct((LANES, W), F32), jax.ShapeDtypeStruct((SUBLANES, W), F32),
                   jax.ShapeDtypeStruct((SUBLANES, W), F32)],
        scratch_shapes=[pltpu.VMEM((tb, W), F32)],
        name=name, compiler_params=_cp(),
    )(a, dgated, ln_g.reshape(1, W), ln_b.reshape(1, W), w_tril, w_tril_t, bias)


def _adam_math(w, g, m, v):
    m = ADAM_B1 * m + (1.0 - ADAM_B1) * g
    v = ADAM_B2 * v + (1.0 - ADAM_B2) * (g * g)
    m_hat = m / (1.0 - ADAM_B1 ** ADAM_STEP)
    v_hat = v / (1.0 - ADAM_B2 ** ADAM_STEP)
    delta = -ADAM_LR * (m_hat / (jnp.sqrt(v_hat) + ADAM_EPS) + ADAM_WD * w)
    return delta, m, v


def _adamw(g, w, m, v, name):
    R, C = w.shape
    tb = _tile(R, 512, SUBLANES)

    def body(g_ref, w_ref, m_ref, v_ref, d_ref, mo_ref, vo_ref):
        d, mm, vv = _adam_math(w_ref[...], g_ref[...], m_ref[...], v_ref[...])
        d_ref[...] = d
        mo_ref[...] = mm
        vo_ref[...] = vv

    row = pl.BlockSpec((tb, C), lambda i: (i, 0))
    sds = jax.ShapeDtypeStruct((R, C), F32)
    return pl.pallas_call(body, grid=(R // tb,), in_specs=[row] * 4, out_specs=[row] * 3,
                          out_shape=[sds] * 3, name=name, compiler_params=_cp())(g, w, m, v)


def _adamw_sum(parts, w, m, v, name):
    K, R, C = parts.shape
    tb = _tile(R, 128, SUBLANES)

    def body(p_ref, w_ref, m_ref, v_ref, g_ref, d_ref, mo_ref, vo_ref):
        g = p_ref[0]
        for k in range(1, K):
            g = g + p_ref[k]
        d, mm, vv = _adam_math(w_ref[...], g, m_ref[...], v_ref[...])
        g_ref[...] = g
        d_ref[...] = d
        mo_ref[...] = mm
        vo_ref[...] = vv

    row = pl.BlockSpec((tb, C), lambda i: (i, 0))
    sds = jax.ShapeDtypeStruct((R, C), F32)
    return pl.pallas_call(
        body, grid=(R // tb,),
        in_specs=[pl.BlockSpec((K, tb, C), lambda i: (0, i, 0)), row, row, row],
        out_specs=[row] * 4, out_shape=[sds] * 4, name=name, compiler_params=_cp())(parts, w, m, v)


def _pair_sum(g_all, recv, c_idx, name):
    K, R, C = g_all.shape
    rh = R // 2
    tb = _tile(rh, 512, SUBLANES)
    nb = rh // tb

    def body(c_ref, a_ref, b_ref, o_ref):
        o_ref[...] = a_ref[...] + b_ref[...]

    return pl.pallas_call(
        body,
        grid_spec=pltpu.PrefetchScalarGridSpec(
            num_scalar_prefetch=1, grid=(K, nb),
            in_specs=[pl.BlockSpec((1, tb, C), lambda k, i, c: (k, c[0] * nb + i, 0)),
                      pl.BlockSpec((1, tb, C), lambda k, i, c: (k, i, 0))],
            out_specs=pl.BlockSpec((1, tb, C), lambda k, i, c: (k, i, 0))),
        out_shape=jax.ShapeDtypeStruct((K, rh, C), F32), name=name, compiler_params=_cp(),
    )(c_idx, g_all, recv)


def _sum_parts(parts, name):
    K, R, C = parts.shape
    tb = _tile(R, 512, SUBLANES)

    def body(p_ref, o_ref):
        g = p_ref[0]
        for k in range(1, K):
            g = g + p_ref[k]
        o_ref[...] = g

    return pl.pallas_call(
        body, grid=(R // tb,), in_specs=[pl.BlockSpec((K, tb, C), lambda i: (0, i, 0))],
        out_specs=pl.BlockSpec((tb, C), lambda i: (i, 0)),
        out_shape=jax.ShapeDtypeStruct((R, C), F32), name=name, compiler_params=_cp())(parts)


_CHIP_RELATIONS = ((1, 0), (0, 1), (1, 1))


def _position():
    return lax.axis_index("x"), lax.axis_index("y"), lax.axis_index("c")


def _flip(v, bit):
    return 1 - v if bit else v


def _gather_weights(w_pack, ln_pack, name):
    R, C = w_pack.shape

    def body(w_ref, ln_ref, ow_ref, oln_ref, local_sems, send_sems, recv_sems):
        x, y, c = _position()
        me = 2 * x + y
        own_w = pltpu.make_async_copy(w_ref, ow_ref.at[me], local_sems.at[0])
        own_ln = pltpu.make_async_copy(ln_ref, oln_ref.at[me], local_sems.at[1])
        own_w.start()
        own_ln.start()

        def copies(r, slot):
            dx, dy = _CHIP_RELATIONS[r]
            peer = (_flip(x, dx), _flip(y, dy), c)
            cw = pltpu.make_async_remote_copy(
                src_ref=w_ref, dst_ref=ow_ref.at[slot], send_sem=send_sems.at[2 * r],
                recv_sem=recv_sems.at[2 * r], device_id=peer, device_id_type=MESH)
            cl = pltpu.make_async_remote_copy(
                src_ref=ln_ref, dst_ref=oln_ref.at[slot], send_sem=send_sems.at[2 * r + 1],
                recv_sem=recv_sems.at[2 * r + 1], device_id=peer, device_id_type=MESH)
            return cw, cl

        sent = [copies(r, me) for r in range(3)]
        for cw, cl in sent:
            cw.start()
            cl.start()
        for r in range(3):
            dx, dy = _CHIP_RELATIONS[r]
            cw, cl = copies(r, 2 * _flip(x, dx) + _flip(y, dy))
            cw.wait_recv()
            cl.wait_recv()
        for cw, cl in sent:
            cw.wait_send()
            cl.wait_send()
        own_w.wait()
        own_ln.wait()

    return pl.pallas_call(
        body, in_specs=[_hbm(), _hbm()], out_specs=[_hbm(), _hbm()],
        out_shape=[jax.ShapeDtypeStruct((4, R, C), w_pack.dtype), jax.ShapeDtypeStruct((4, SUBLANES, C), F32)],
        scratch_shapes=[pltpu.SemaphoreType.DMA((2,)), pltpu.SemaphoreType.DMA((6,)),
                        pltpu.SemaphoreType.DMA((6,))],
        name=name, compiler_params=_cp(),
    )(w_pack, ln_pack)


def _sibling_halves(g_all, name):
    K, R, C = g_all.shape
    rh = R // 2

    def body(g_ref, o_ref, send_sem, recv_sem):
        x, y, c = _position()
        start = pl.multiple_of((1 - c) * rh, SUBLANES)
        cp = pltpu.make_async_remote_copy(
            src_ref=g_ref.at[:, pl.ds(start, rh), :], dst_ref=o_ref, send_sem=send_sem, recv_sem=recv_sem,
            device_id=(x, y, 1 - c), device_id_type=MESH)
        cp.start()
        cp.wait_recv()
        cp.wait_send()

    return pl.pallas_call(
        body, in_specs=[_hbm()], out_specs=_hbm(),
        out_shape=jax.ShapeDtypeStruct((K, rh, C), F32),
        scratch_shapes=[pltpu.SemaphoreType.DMA(()), pltpu.SemaphoreType.DMA(())],
        name=name, compiler_params=_cp(),
    )(g_all)


def _chip_exchange(parts, name):
    K, R, C = parts.shape

    def body(p_ref, o_ref, local_sem, send_sems, recv_sems):
        x, y, c = _position()
        me = 2 * x + y
        own = pltpu.make_async_copy(p_ref.at[me], o_ref.at[me], local_sem)
        own.start()

        def copy(r, src_slot, dst_slot):
            dx, dy = _CHIP_RELATIONS[r]
            return pltpu.make_async_remote_copy(
                src_ref=p_ref.at[src_slot], dst_ref=o_ref.at[dst_slot], send_sem=send_sems.at[r],
                recv_sem=recv_sems.at[r], device_id=(_flip(x, dx), _flip(y, dy), c), device_id_type=MESH)

        def chip(r):
            dx, dy = _CHIP_RELATIONS[r]
            return 2 * _flip(x, dx) + _flip(y, dy)

        sent = [copy(r, chip(r), me) for r in range(3)]
        for cp in sent:
            cp.start()
        for r in range(3):
            copy(r, me, chip(r)).wait_recv()
        for cp in sent:
            cp.wait_send()
        own.wait()

    return pl.pallas_call(
        body, in_specs=[_hbm()], out_specs=_hbm(),
        out_shape=jax.ShapeDtypeStruct((K, R, C), F32),
        scratch_shapes=[pltpu.SemaphoreType.DMA(()), pltpu.SemaphoreType.DMA((3,)),
                        pltpu.SemaphoreType.DMA((3,))],
        name=name, compiler_params=_cp(),
    )(parts)


def _join_halves(half, name):
    rh, C = half.shape

    def body(h_ref, o_ref, local_sem, send_sem, recv_sem):
        x, y, c = _position()
        mine = pl.ds(pl.multiple_of(c * rh, SUBLANES), rh)
        theirs = pl.ds(pl.multiple_of((1 - c) * rh, SUBLANES), rh)
        own = pltpu.make_async_copy(h_ref, o_ref.at[mine, :], local_sem)
        own.start()
        cp = pltpu.make_async_remote_copy(
            src_ref=h_ref, dst_ref=o_ref.at[mine, :], send_sem=send_sem, recv_sem=recv_sem,
            device_id=(x, y, 1 - c), device_id_type=MESH)
        cp.start()
        pltpu.make_async_remote_copy(
            src_ref=h_ref, dst_ref=o_ref.at[theirs, :], send_sem=send_sem, recv_sem=recv_sem,
            device_id=(x, y, 1 - c), device_id_type=MESH).wait_recv()
        cp.wait_send()
        own.wait()

    return pl.pallas_call(
        body, in_specs=[_hbm()], out_specs=_hbm(),
        out_shape=jax.ShapeDtypeStruct((2 * rh, C), F32),
        scratch_shapes=[pltpu.SemaphoreType.DMA(()), pltpu.SemaphoreType.DMA(()), pltpu.SemaphoreType.DMA(())],
        name=name, compiler_params=_cp(),
    )(half)


def _gather_all(part, name):
    R, C = part.shape
    masks = [(b >> 2 & 1, b >> 1 & 1, b & 1) for b in range(1, 8)]

    def body(p_ref, o_ref, local_sem, send_sems, recv_sems):
        x, y, c = _position()
        me = 4 * x + 2 * y + c
        own = pltpu.make_async_copy(p_ref, o_ref.at[me], local_sem)
        own.start()

        def copy(r, slot):
            dx, dy, dc = masks[r]
            return pltpu.make_async_remote_copy(
                src_ref=p_ref, dst_ref=o_ref.at[slot], send_sem=send_sems.at[r], recv_sem=recv_sems.at[r],
                device_id=(_flip(x, dx), _flip(y, dy), _flip(c, dc)), device_id_type=MESH)

        sent = [copy(r, me) for r in range(7)]
        for cp in sent:
            cp.start()
        for r in range(7):
            dx, dy, dc = masks[r]
            copy(r, 4 * _flip(x, dx) + 2 * _flip(y, dy) + _flip(c, dc)).wait_recv()
        for cp in sent:
            cp.wait_send()
        own.wait()

    return pl.pallas_call(
        body, in_specs=[_hbm()], out_specs=_hbm(),
        out_shape=jax.ShapeDtypeStruct((8, R, C), F32),
        scratch_shapes=[pltpu.SemaphoreType.DMA(()), pltpu.SemaphoreType.DMA((7,)),
                        pltpu.SemaphoreType.DMA((7,))],
        name=name, compiler_params=_cp(),
    )(part)


def _pack(arrs, row_mult):
    flat = jnp.concatenate([a.reshape(-1).astype(F32) for a in arrs])
    rows = -(-flat.shape[0] // PACK_COLS)
    rows = -(-rows // row_mult) * row_mult
    flat = jnp.pad(flat, (0, rows * PACK_COLS - flat.shape[0]))
    return flat.reshape(rows, PACK_COLS)


def _unpack(buf, shapes):
    lead = buf.shape[:-2]
    flat = buf.reshape(lead + (-1,))
    out, off = [], 0
    for shp in shapes:
        n = math.prod(shp)
        out.append(flat[..., off:off + n].reshape(lead + tuple(shp)))
        off += n
    return out


def _cols_from_chips(g):
    k, L, A, n = g.shape
    return jnp.transpose(g, (1, 2, 0, 3)).reshape(L, A, k * n)


def _rows_from_chips(g):
    k, L, n, B = g.shape
    return jnp.transpose(g, (1, 0, 2, 3)).reshape(L, k * n, B)


def _cols_to_chips(full, k=4):
    L, A, N = full.shape
    return jnp.transpose(full.reshape(L, A, k, N // k), (2, 0, 1, 3))


def _rows_to_chips(full, k=4):
    L, N, B = full.shape
    return jnp.transpose(full.reshape(L, k, N // k, B), (1, 0, 2, 3))


def kernel(x, mixer_norm_w, attn_w_in, attn_b_f, attn_w_out, sgu_w_in, sgu_ln_g, sgu_ln_b, sgu_w_s, sgu_b_s, sgu_w_out, ffn_norm_w, ffn_w_in, ffn_w_out, final_norm_w, loss_target, m_mixer_norm_w, m_attn_w_in, m_attn_b_f, m_attn_w_out, m_sgu_w_in, m_sgu_ln_g, m_sgu_ln_b, m_sgu_w_s, m_sgu_b_s, m_sgu_w_out, m_ffn_norm_w, m_ffn_w_in, m_ffn_w_out, m_final_norm_w, v_mixer_norm_w, v_attn_w_in, v_attn_b_f, v_attn_w_out, v_sgu_w_in, v_sgu_ln_g, v_sgu_ln_b, v_sgu_w_s, v_sgu_b_s, v_sgu_w_out, v_ffn_norm_w, v_ffn_w_in, v_ffn_w_out, v_final_norm_w):
    T, D = x.shape[1], x.shape[2]
    depth = mixer_norm_w.shape[0]
    H = attn_b_f.shape[1]
    P = D // LANES
    assert D % LANES == 0 and D // H == 64 and 2 * P == H and 2 * P <= LANES
    G = sgu_w_s.shape[1]
    W = sgu_w_out.shape[1] * 4
    assert sgu_w_s.shape[2] == LANES and W == G * LANES
    scale = float(D // H) ** -0.5
    f_pad = LANES
    c_idx = lax.axis_index("c").astype(jnp.int32).reshape(1)

    sharded = [attn_w_in, attn_w_out, sgu_w_in, sgu_w_out, ffn_w_in, ffn_w_out, sgu_ln_g, sgu_ln_b]
    sharded_m = [m_attn_w_in, m_attn_w_out, m_sgu_w_in, m_sgu_w_out, m_ffn_w_in, m_ffn_w_out, m_sgu_ln_g, m_sgu_ln_b]
    sharded_v = [v_attn_w_in, v_attn_w_out, v_sgu_w_in, v_sgu_w_out, v_ffn_w_in, v_ffn_w_out, v_sgu_ln_g, v_sgu_ln_b]
    shard_shapes = [a.shape for a in sharded]
    w_pack = _pack(sharded, 512)
    ln_pack = _pack([sgu_ln_g, sgu_ln_b], SUBLANES)
    gat_w, gat_ln = _gather_weights(w_pack.astype(BF16), ln_pack, "gather_weights")
    g_ai, g_ao, g_si, g_so, g_fi, g_fo, _, _ = _unpack(gat_w, shard_shapes)
    g_lng, g_lnb = _unpack(gat_ln, [sgu_ln_g.shape, sgu_ln_b.shape])
    w_ai = _cols_from_chips(g_ai)
    w_ai = jnp.pad(w_ai, ((0, 0), (0, 0), (0, 3 * D + f_pad - w_ai.shape[2])))
    w_ao = _rows_from_chips(g_ao)
    w_si = _cols_from_chips(g_si)
    w_so = _rows_from_chips(g_so)
    w_fi = _cols_from_chips(g_fi)
    w_fo = _rows_from_chips(g_fo)
    ln_g = jnp.transpose(g_lng, (1, 0, 2)).reshape(sgu_ln_g.shape[0], W)
    ln_b = jnp.transpose(g_lnb, (1, 0, 2)).reshape(sgu_ln_b.shape[0], W)
    w_tril = jnp.tril(sgu_w_s)
    w_tril_b = w_tril.astype(BF16)
    w_tril_tb = jnp.swapaxes(w_tril, 2, 3).astype(BF16)
    sgu_bias = jnp.repeat(jnp.swapaxes(sgu_b_s, 1, 2), LANES, axis=2)
    b_f_pad = jnp.pad(attn_b_f, ((0, 0), (0, LANES - H)))

    xs = x.reshape(T, D)
    saved = []
    for i in range(depth):
        j = i // 2
        h = _rmsnorm_fwd(xs, mixer_norm_w[i], f"mix_norm_{i}")
        rec = {"x_in": xs, "h": h}
        if i % 2 == 0:
            qkv = _mm(h, w_ai[j, :, :3 * D], "nn", BF16, f"attn_qkv_{i}")
            f = _mm(h, w_ai[j, :, 3 * D:], "nn", F32, f"attn_gate_{i}")
            cT, c_cols, c0T = _gate_fwd(f, b_f_pad[j:j + 1], P, f"gate_fwd_{i}")
            o, lseT = _attn_fwd(qkv, cT, P, scale, f"attn_fwd_{i}")
            x_mid = _mm(o, w_ao[j], "nn", F32, f"attn_out_{i}", res=xs)
            rec.update(qkv=qkv, f=f, c0T=c0T, c_cols=c_cols, o=o, lseT=lseT)
        else:
            a = _mm(h, w_si[j], "nn", BF16, f"sgu_in_{i}")
            gated = _sgu_fwd(a, ln_g[j], ln_b[j], w_tril_b[j], sgu_bias[j], f"sgu_fwd_{i}")
            x_mid = _mm(gated, w_so[j], "nn", F32, f"sgu_out_{i}", res=xs)
            rec.update(a=a, gated=gated)
        h2 = _rmsnorm_fwd(x_mid, ffn_norm_w[i], f"ffn_norm_{i}")
        fa = _mm(h2, w_fi[i], "nn", BF16, f"ffn_in_{i}")
        s = _silu_mul(fa, f"ffn_act_{i}")
        xs = _mm(s, w_fo[i], "nn", F32, f"ffn_out_{i}", res=x_mid)
        rec.update(x_mid=x_mid, h2=h2, fa=fa, s=s)
        saved.append(rec)

    gx, loss_acc, dw_final = _loss_head(xs, final_norm_w, loss_target.reshape(T, D), "loss_head")
    loss = lax.psum(loss_acc[0, 0], ("x", "y", "c"))

    n_attn, n_sgu = attn_w_in.shape[0], sgu_w_in.shape[0]
    d_mixer_norm, d_ffn_norm = [None] * depth, [None] * depth
    d_ai, d_ao, d_bf = [None] * n_attn, [None] * n_attn, [None] * n_attn
    d_si, d_so, d_lng, d_lnb, d_ws, d_bs = ([None] * n_sgu for _ in range(6))
    d_fi, d_fo = [None] * depth, [None] * depth
    for i in reversed(range(depth)):
        j = i // 2
        rec = saved[i]
        ds = _mm(gx, w_fo[i], "nt", F32, f"ffn_out_bwd_{i}")
        d_fo[i] = _mm(rec["s"], gx, "tn", F32, f"ffn_out_wgrad_{i}")
        da = _silu_mul_bwd(ds, rec["fa"], f"ffn_act_bwd_{i}")
        dh2 = _mm(da, w_fi[i], "nt", F32, f"ffn_in_bwd_{i}")
        d_fi[i] = _mm(rec["h2"], da, "tn", F32, f"ffn_in_wgrad_{i}")
        gx, dwn = _rmsnorm_bwd(dh2, rec["x_mid"], ffn_norm_w[i], gx, f"ffn_norm_bwd_{i}")
        d_ffn_norm[i] = dwn[0]
        if i % 2 == 0:
            do = _mm(gx, w_ao[j], "nt", BF16, f"attn_out_bwd_{i}")
            d_ao[j] = _mm(rec["o"], gx, "tn", F32, f"attn_out_wgrad_{i}")
            dT = _attn_delta(do, rec["o"], P, f"attn_delta_{i}")
            dq, dk, dv, dc_cols, drowT = _attn_bwd(rec["qkv"], do, rec["lseT"], dT, rec["c0T"], rec["c_cols"],
                                                   P, scale, f"attn_bwd_{i}")
            df, dbf = _gate_bwd(dc_cols, drowT, rec["f"], b_f_pad[j:j + 1], P, f"gate_bwd_{i}")
            d_bf[j] = dbf[0, :H]
            dproj = jnp.concatenate([dq, dk, dv, df.astype(BF16)], axis=1)
            dh = _mm(dproj, w_ai[j], "nt", F32, f"attn_in_bwd_{i}")
            d_ai[j] = _mm(rec["h"], dproj, "tn", F32, f"attn_in_wgrad_{i}")[:, :3 * D + H]
        else:
            dgated = _mm(gx, w_so[j], "nt", F32, f"sgu_out_bwd_{i}")
            d_so[j] = _mm(rec["gated"], gx, "tn", F32, f"sgu_out_wgrad_{i}")
            da_s, dws, dbias, dlng, dlnb = _sgu_bwd(rec["a"], dgated, ln_g[j], ln_b[j], w_tril_b[j],
                                                    w_tril_tb[j], sgu_bias[j], f"sgu_bwd_{i}")
            d_ws[j] = jnp.tril(dws)
            d_bs[j] = jnp.sum(dbias.reshape(LANES, G, LANES), axis=2).T
            d_lng[j], d_lnb[j] = dlng[0], dlnb[0]
            dh = _mm(da_s, w_si[j], "nt", F32, f"sgu_in_bwd_{i}")
            d_si[j] = _mm(rec["h"], da_s, "tn", F32, f"sgu_in_wgrad_{i}")
        gx, dwn = _rmsnorm_bwd(dh, rec["x_in"], mixer_norm_w[i], gx, f"mix_norm_bwd_{i}")
        d_mixer_norm[i] = dwn[0]
    grad_x = gx.reshape(x.shape)

    full_grads = [
        _cols_to_chips(jnp.stack(d_ai)), _rows_to_chips(jnp.stack(d_ao)),
        _cols_to_chips(jnp.stack(d_si)), _rows_to_chips(jnp.stack(d_so)),
        _cols_to_chips(jnp.stack(d_fi)), _rows_to_chips(jnp.stack(d_fo)),
        jnp.transpose(jnp.stack(d_lng).reshape(n_sgu, 4, W // 4), (1, 0, 2)),
        jnp.transpose(jnp.stack(d_lnb).reshape(n_sgu, 4, W // 4), (1, 0, 2)),
    ]
    g_all = jnp.stack([_pack([fg[k] for fg in full_grads], 512) for k in range(4)])
    from_sibling = _sibling_halves(g_all, "grad_sibling_halves")
    pair = _pair_sum(g_all, from_sibling, c_idx, "grad_pair_sum")
    from_chips = _chip_exchange(pair, "grad_chip_exchange")
    my_half = _sum_parts(from_chips, "grad_chip_sum")
    g_shard = _join_halves(my_half, "grad_join_halves")
    d_shard, m_shard, v_shard = _adamw(g_shard, w_pack, _pack(sharded_m, 512), _pack(sharded_v, 512), "adamw_sharded")
    g_sh = _unpack(g_shard, shard_shapes)
    d_sh = _unpack(d_shard, shard_shapes)
    m_sh = _unpack(m_shard, shard_shapes)
    v_sh = _unpack(v_shard, shard_shapes)

    repl = [mixer_norm_w, attn_b_f, sgu_w_s, sgu_b_s, ffn_norm_w, final_norm_w]
    repl_m = [m_mixer_norm_w, m_attn_b_f, m_sgu_w_s, m_sgu_b_s, m_ffn_norm_w, m_final_norm_w]
    repl_v = [v_mixer_norm_w, v_attn_b_f, v_sgu_w_s, v_sgu_b_s, v_ffn_norm_w, v_final_norm_w]
    repl_shapes = [a.shape for a in repl]
    repl_grads = [jnp.stack(d_mixer_norm), jnp.stack(d_bf), jnp.stack(d_ws), jnp.stack(d_bs),
                  jnp.stack(d_ffn_norm), dw_final[0]]
    parts = _gather_all(_pack(repl_grads, SUBLANES), "grad_gather_replicated")
    g_rep, d_rep, m_rep, v_rep = _adamw_sum(parts, _pack(repl, SUBLANES), _pack(repl_m, SUBLANES),
                                            _pack(repl_v, SUBLANES), "adamw_replicated")
    g_r = _unpack(g_rep, repl_shapes)
    d_r = _unpack(d_rep, repl_shapes)
    m_r = _unpack(m_rep, repl_shapes)
    v_r = _unpack(v_rep, repl_shapes)

    def ordered(sh, rp):
        ai, ao, si, so, fi, fo, lng, lnb = sh
        mn, bf, ws, bs, fn, fin = rp
        return [mn, ai, bf, ao, si, lng, lnb, ws, bs, so, fn, fi, fo, fin]

    return (loss, grad_x, *ordered(g_sh, g_r), *ordered(d_sh, d_r), *ordered(m_sh, m_r), *ordered(v_sh, v_r))
```

```python
import functools
import math

import jax
import jax.numpy as jnp
from jax import lax
from jax.experimental import pallas as pl
from jax.experimental.pallas import tpu as pltpu

F32 = jnp.float32
BF16 = jnp.bfloat16
NORM_EPS = 1e-6
LN_EPS = 1e-5
ADAM_LR = 0.001
ADAM_B1 = 0.9
ADAM_B2 = 0.999
ADAM_EPS = 1e-08
ADAM_WD = 0.01
ADAM_STEP = 10

LANES = 128
SUBLANES = 8
PACK_COLS = 1024
VMEM_LIMIT = 56 * 1024 * 1024
NEG_BIG = -1e30
MESH = pl.DeviceIdType.MESH


def _cp():
    return pltpu.CompilerParams(vmem_limit_bytes=VMEM_LIMIT)


def _tile(n, cap, mult):
    best = None
    d = mult
    while d <= min(n, cap):
        if n % d == 0:
            best = d
        d += mult
    return n if best is None else best


def _hbm():
    return pl.BlockSpec(memory_space=pltpu.HBM)


def _rmsnorm_fwd(x, w, name):
    T, D = x.shape
    tm = _tile(T, 512, SUBLANES)

    def body(x_ref, w_ref, h_ref):
        xf = x_ref[...]
        r = lax.rsqrt(jnp.mean(xf * xf, axis=-1, keepdims=True) + NORM_EPS)
        h_ref[...] = (xf * r * w_ref[...]).astype(BF16)

    return pl.pallas_call(
        body, grid=(T // tm,),
        in_specs=[pl.BlockSpec((tm, D), lambda i: (i, 0)), pl.BlockSpec((1, D), lambda i: (0, 0))],
        out_specs=pl.BlockSpec((tm, D), lambda i: (i, 0)),
        out_shape=jax.ShapeDtypeStruct((T, D), BF16), name=name, compiler_params=_cp(),
    )(x, w.reshape(1, D))


def _rmsnorm_bwd(dh, x, w, dres, name):
    T, D = x.shape
    tm = _tile(T, 512, SUBLANES)

    def body(dh_ref, x_ref, w_ref, dres_ref, dx_ref, dw_ref):
        @pl.when(pl.program_id(0) == 0)
        def _():
            dw_ref[...] = jnp.zeros_like(dw_ref)

        xf = x_ref[...]
        r = lax.rsqrt(jnp.mean(xf * xf, axis=-1, keepdims=True) + NORM_EPS)
        xhat = xf * r
        dhv = dh_ref[...]
        dxhat = dhv * w_ref[...]
        dx_ref[...] = dres_ref[...] + r * (dxhat - xhat * jnp.mean(dxhat * xhat, axis=-1, keepdims=True))
        dw_ref[...] += jnp.sum(dhv * xhat, axis=0, keepdims=True)

    row = pl.BlockSpec((tm, D), lambda i: (i, 0))
    return pl.pallas_call(
        body, grid=(T // tm,),
        in_specs=[row, row, pl.BlockSpec((1, D), lambda i: (0, 0)), row],
        out_specs=[row, pl.BlockSpec((SUBLANES, D), lambda i: (0, 0))],
        out_shape=[jax.ShapeDtypeStruct((T, D), F32), jax.ShapeDtypeStruct((SUBLANES, D), F32)],
        name=name, compiler_params=_cp(),
    )(dh, x, w.reshape(1, D), dres)


def _mm(a, b, mode, out_dtype, name, res=None):
    if mode == "tn":
        kt, M = a.shape
        N = b.shape[1]
        tm = _tile(M, 1408, LANES)
        tn = _tile(N, 1408, LANES)
        tk = _tile(kt, 512, 16)

        def body(a_ref, b_ref, o_ref):
            @pl.when(pl.program_id(2) == 0)
            def _():
                o_ref[...] = jnp.zeros_like(o_ref)

            o_ref[...] += lax.dot_general(
                a_ref[...].astype(BF16), b_ref[...].astype(BF16), (((0,), (0,)), ((), ())),
                preferred_element_type=F32)

        return pl.pallas_call(
            body, grid=(M // tm, N // tn, kt // tk),
            in_specs=[pl.BlockSpec((tk, tm), lambda i, j, k: (k, i)),
                      pl.BlockSpec((tk, tn), lambda i, j, k: (k, j))],
            out_specs=pl.BlockSpec((tm, tn), lambda i, j, k: (i, j)),
            out_shape=jax.ShapeDtypeStruct((M, N), F32), name=name, compiler_params=_cp(),
        )(a, b)

    M, K = a.shape
    N = b.shape[1] if mode == "nn" else b.shape[0]
    tm = _tile(M, 512, 16)
    cap = max(LANES, min(1408, ((6 << 20) // (2 * K)) // LANES * LANES))
    tn = _tile(N, cap, LANES)
    dims = (((1,), (0,)), ((), ())) if mode == "nn" else (((1,), (1,)), ((), ()))

    def body(*refs):
        if res is None:
            a_ref, b_ref, o_ref = refs
        else:
            a_ref, b_ref, r_ref, o_ref = refs
        acc = lax.dot_general(a_ref[...].astype(BF16), b_ref[...].astype(BF16), dims,
                              preferred_element_type=F32)
        if res is not None:
            acc = acc + r_ref[...]
        o_ref[...] = acc.astype(out_dtype)

    b_spec = (pl.BlockSpec((K, tn), lambda i, j: (0, j)) if mode == "nn"
              else pl.BlockSpec((tn, K), lambda i, j: (j, 0)))
    in_specs = [pl.BlockSpec((tm, K), lambda i, j: (i, 0)), b_spec]
    args = [a, b]
    if res is not None:
        in_specs.append(pl.BlockSpec((tm, tn), lambda i, j: (i, j)))
        args.append(res)
    return pl.pallas_call(
        body, grid=(M // tm, N // tn), in_specs=in_specs,
        out_specs=pl.BlockSpec((tm, tn), lambda i, j: (i, j)),
        out_shape=jax.ShapeDtypeStruct((M, N), out_dtype), name=name, compiler_params=_cp(),
    )(*args)


def _ffn_tile(F):
    return _tile(F, 1408, LANES)


def _interleave_gu(w, tf):
    lead, F2 = w.shape[:-1], w.shape[-1]
    n = F2 // (2 * tf)
    return jnp.swapaxes(w.reshape(lead + (2, n, tf)), -3, -2).reshape(lead + (F2,))


def _deinterleave_gu(w, tf):
    lead, F2 = w.shape[:-1], w.shape[-1]
    n = F2 // (2 * tf)
    return jnp.swapaxes(w.reshape(lead + (n, 2, tf)), -3, -2).reshape(lead + (F2,))


def _ffn_in_act(h, w_il, name):
    T, D = h.shape
    F = w_il.shape[1] // 2
    tf = _ffn_tile(F)
    tm = _tile(T, 256, 16)

    def body(h_ref, w_ref, a_ref, s_ref):
        acc = jnp.dot(h_ref[...], w_ref[...], preferred_element_type=F32)
        a_ref[...] = acc.astype(BF16)
        g = acc[:, :tf]
        s_ref[...] = (g * jax.nn.sigmoid(g) * acc[:, tf:]).astype(BF16)

    return pl.pallas_call(
        body, grid=(F // tf, T // tm),
        in_specs=[pl.BlockSpec((tm, D), lambda j, i: (i, 0)), pl.BlockSpec((D, 2 * tf), lambda j, i: (0, j))],
        out_specs=[pl.BlockSpec((tm, 2 * tf), lambda j, i: (i, j)), pl.BlockSpec((tm, tf), lambda j, i: (i, j))],
        out_shape=[jax.ShapeDtypeStruct((T, 2 * F), BF16), jax.ShapeDtypeStruct((T, F), BF16)],
        name=name, compiler_params=_cp(),
    )(h, w_il)


def _ffn_out_bwd_act(gx, w_out, a_il, name):
    T, D = gx.shape
    F = w_out.shape[0]
    tf = _ffn_tile(F)
    tm = _tile(T, 256, 16)

    def body(gx_ref, w_ref, a_ref, da_ref):
        ds = _nt(gx_ref[...].astype(BF16), w_ref[...])
        g = a_ref[:, :tf].astype(F32)
        u = a_ref[:, tf:].astype(F32)
        sg = jax.nn.sigmoid(g)
        da_ref[:, :tf] = (ds * u * (sg * (1.0 + g * (1.0 - sg)))).astype(BF16)
        da_ref[:, tf:] = (ds * (g * sg)).astype(BF16)

    return pl.pallas_call(
        body, grid=(F // tf, T // tm),
        in_specs=[pl.BlockSpec((tm, D), lambda j, i: (i, 0)), pl.BlockSpec((tf, D), lambda j, i: (j, 0)),
                  pl.BlockSpec((tm, 2 * tf), lambda j, i: (i, j))],
        out_specs=pl.BlockSpec((tm, 2 * tf), lambda j, i: (i, j)),
        out_shape=jax.ShapeDtypeStruct((T, 2 * F), BF16), name=name, compiler_params=_cp(),
    )(gx, w_out, a_il)


def _loss_head(x, w, tgt, name):
    T, D = x.shape
    tm = _tile(T, 512, SUBLANES)

    def body(x_ref, w_ref, t_ref, dx_ref, loss_ref, dw_ref):
        @pl.when(pl.program_id(0) == 0)
        def _():
            loss_ref[...] = jnp.zeros_like(loss_ref)
            dw_ref[...] = jnp.zeros_like(dw_ref)

        xf = x_ref[...]
        wv = w_ref[...]
        r = lax.rsqrt(jnp.mean(xf * xf, axis=-1, keepdims=True) + NORM_EPS)
        xhat = xf * r
        err = xhat * wv - t_ref[...]
        per_tok = jnp.mean(err * err, axis=-1, keepdims=True)
        loss_ref[...] += 0.5 * jnp.sum(per_tok, axis=0, keepdims=True)
        dy = err * (1.0 / D)
        dxhat = dy * wv
        dx_ref[...] = r * (dxhat - xhat * jnp.mean(dxhat * xhat, axis=-1, keepdims=True))
        dw_ref[...] += jnp.sum(dy * xhat, axis=0, keepdims=True)

    row = pl.BlockSpec((tm, D), lambda i: (i, 0))
    return pl.pallas_call(
        body, grid=(T // tm,),
        in_specs=[row, pl.BlockSpec((1, D), lambda i: (0, 0)), row],
        out_specs=[row, pl.BlockSpec((SUBLANES, LANES), lambda i: (0, 0)),
                   pl.BlockSpec((SUBLANES, D), lambda i: (0, 0))],
        out_shape=[jax.ShapeDtypeStruct((T, D), F32), jax.ShapeDtypeStruct((SUBLANES, LANES), F32),
                   jax.ShapeDtypeStruct((SUBLANES, D), F32)],
        name=name, compiler_params=_cp(),
    )(x, w.reshape(1, D), tgt)


def _split3(v):
    hi = v.astype(BF16)
    r1 = v - hi.astype(F32)
    mid = r1.astype(BF16)
    lo = (r1 - mid.astype(F32)).astype(BF16)
    return hi, mid, lo


def _tri_dot(tri, v):
    out = None
    for piece in _split3(v):
        t = jnp.dot(tri, piece, preferred_element_type=F32)
        out = t if out is None else out + t
    return out


def _q_block(T):
    return _tile(T, 256, LANES)


def _gate_fwd(f, b_f, P, name):
    T = f.shape[0]
    tb = _q_block(T)

    def body(f_ref, b_ref, ct_ref, cc_ref, c0_ref, carry):
        @pl.when(pl.program_id(0) == 0)
        def _():
            carry[...] = jnp.zeros_like(carry)

        z = f_ref[...] + b_ref[...]
        logf = jnp.minimum(z, 0.0) - jnp.log(1.0 + jnp.exp(-jnp.abs(z)))
        row = lax.broadcasted_iota(jnp.int32, (tb, tb), 0)
        col = lax.broadcasted_iota(jnp.int32, (tb, tb), 1)
        tri = (col <= row).astype(BF16)
        c = _tri_dot(tri, logf) + carry[0:1, :]
        carry[...] = jnp.broadcast_to(c[tb - 1:tb, :], carry.shape)
        first = jnp.broadcast_to(c[0:1, :], c.shape)
        for p in range(P):
            shifted = c if p == 0 else pltpu.roll(c, LANES - 2 * p, 1)
            cc_ref[p] = shifted
            ct_ref[p] = shifted.T[0:SUBLANES, :]
            c0_ref[p] = (first if p == 0 else pltpu.roll(first, LANES - 2 * p, 1)).T[0:SUBLANES, :]

    rows = pl.BlockSpec((P, SUBLANES, tb), lambda i: (0, 0, i))
    return pl.pallas_call(
        body, grid=(T // tb,),
        in_specs=[pl.BlockSpec((tb, LANES), lambda i: (i, 0)), pl.BlockSpec((1, LANES), lambda i: (0, 0))],
        out_specs=[rows, pl.BlockSpec((P, tb, LANES), lambda i: (0, i, 0)), rows],
        out_shape=[jax.ShapeDtypeStruct((P, SUBLANES, T), F32), jax.ShapeDtypeStruct((P, T, LANES), F32),
                   jax.ShapeDtypeStruct((P, SUBLANES, T), F32)],
        scratch_shapes=[pltpu.VMEM((SUBLANES, LANES), F32)],
        name=name, compiler_params=_cp(),
    )(f, b_f)


def _gate_bwd(dc_cols, drowT, f, b_f, P, name):
    T = f.shape[0]
    tb = _tile(T, 256, LANES)
    nb = T // tb

    def body(dc_ref, dr_ref, f_ref, b_ref, df_ref, db_ref, carry):
        @pl.when(pl.program_id(0) == 0)
        def _():
            carry[...] = jnp.zeros_like(carry)
            db_ref[...] = jnp.zeros_like(db_ref)

        lane = lax.broadcasted_iota(jnp.int32, (tb, LANES), 1)
        dc = jnp.zeros((tb, LANES), F32)
        for p in range(P):
            rows = jnp.concatenate([dr_ref[p], jnp.zeros((LANES - SUBLANES, tb), F32)], axis=0)
            part = jnp.where(lane < 2, dc_ref[p] + rows.T, 0.0)
            dc = dc + (part if p == 0 else pltpu.roll(part, 2 * p, 1))
        row = lax.broadcasted_iota(jnp.int32, (tb, tb), 0)
        col = lax.broadcasted_iota(jnp.int32, (tb, tb), 1)
        tri = (col >= row).astype(BF16)
        dlogf = _tri_dot(tri, dc) + carry[0:1, :]
        carry[...] = jnp.broadcast_to(dlogf[0:1, :], carry.shape)
        z = f_ref[...] + b_ref[...]
        df = jnp.where(lane < 2 * P, dlogf * jax.nn.sigmoid(-z), 0.0)
        df_ref[...] = df
        db_ref[...] += jnp.sum(df, axis=0, keepdims=True)

    return pl.pallas_call(
        body, grid=(nb,),
        in_specs=[pl.BlockSpec((P, tb, LANES), lambda i: (0, nb - 1 - i, 0)),
                  pl.BlockSpec((P, SUBLANES, tb), lambda i: (0, 0, nb - 1 - i)),
                  pl.BlockSpec((tb, LANES), lambda i: (nb - 1 - i, 0)),
                  pl.BlockSpec((1, LANES), lambda i: (0, 0))],
        out_specs=[pl.BlockSpec((tb, LANES), lambda i: (nb - 1 - i, 0)),
                   pl.BlockSpec((SUBLANES, LANES), lambda i: (0, 0))],
        out_shape=[jax.ShapeDtypeStruct((T, LANES), F32), jax.ShapeDtypeStruct((SUBLANES, LANES), F32)],
        scratch_shapes=[pltpu.VMEM((SUBLANES, LANES), F32)],
        name=name, compiler_params=_cp(),
    )(dc_cols, drowT, f, b_f)


def _nt(a, b):
    return lax.dot_general(a, b, (((1,), (1,)), ((), ())), preferred_element_type=F32)


def _attn_fwd(qkv, cT, P, scale, name):
    T = qkv.shape[0]
    tq = _q_block(T)
    tw = _tile(T, 4 * tq, tq)
    nq = T // tq

    def body(q_ref, k_ref, v_ref, c_ref, o_ref, lse_ref):
        i = pl.program_id(1)
        lane = lax.broadcasted_iota(jnp.int32, (1, LANES), 1)
        q = (q_ref[...].astype(F32) * scale).astype(BF16)
        q_heads = (jnp.where(lane < 64, q, jnp.zeros_like(q)), jnp.where(lane >= 64, q, jnp.zeros_like(q)))
        c0 = c_ref[0, :, pl.ds(pl.multiple_of(i * tq, tq), LANES)][:, 0:1]

        def block(start, width, carry, masked):
            k = k_ref[pl.ds(start, width), :]
            v = v_ref[pl.ds(start, width), :]
            one = jnp.ones_like(v)
            v_heads = (jnp.where(lane < 64, v, one), jnp.where(lane >= 64, v, one))
            bias = c0 - c_ref[0, :, pl.ds(start, width)]
            new = []
            for a in range(2):
                m, acc = carry[a]
                s = _nt(q_heads[a], k) + bias[a:a + 1, :]
                if masked:
                    row = lax.broadcasted_iota(jnp.int32, (tq, width), 0)
                    col = lax.broadcasted_iota(jnp.int32, (tq, width), 1)
                    s = jnp.where(col <= row, s, NEG_BIG)
                m_new = jnp.maximum(m, jnp.max(s, axis=1, keepdims=True))
                p = jnp.exp(s - m_new)
                acc = jnp.exp(m - m_new) * acc + jnp.dot(p.astype(BF16), v_heads[a], preferred_element_type=F32)
                new.append((m_new, acc))
            return tuple(new)

        init = tuple((jnp.full((tq, 1), NEG_BIG, F32), jnp.zeros((tq, LANES), F32)) for _ in range(2))
        n_wide = (i * tq) // tw
        carry = lax.fori_loop(
            0, n_wide, lambda j, cr: block(pl.multiple_of(j * tw, tw), tw, cr, False), init)
        carry = lax.fori_loop(
            n_wide * (tw // tq), i, lambda j, cr: block(pl.multiple_of(j * tq, tq), tq, cr, False), carry)
        (m0, a0), (m1, a1) = block(pl.multiple_of(i * tq, tq), tq, carry, True)
        sums = jnp.where(lane < 64, pltpu.roll(a0, 64, 1), pltpu.roll(a1, 64, 1))
        o_ref[...] = (jnp.where(lane < 64, a0, a1) / sums).astype(BF16)
        l0, l1 = a0[:, 64:65], a1[:, 0:1]
        lse = jnp.where(lane == 0, m0 + jnp.log(l0), jnp.where(lane == 1, m1 + jnp.log(l1), 0.0))
        lse_ref[0] = lse.T[0:SUBLANES, :]

    return pl.pallas_call(
        body, grid=(P, nq),
        in_specs=[pl.BlockSpec((tq, LANES), lambda p, i: (i, p)),
                  pl.BlockSpec((T, LANES), lambda p, i: (0, P + p)),
                  pl.BlockSpec((T, LANES), lambda p, i: (0, 2 * P + p)),
                  pl.BlockSpec((1, SUBLANES, T), lambda p, i: (p, 0, 0))],
        out_specs=[pl.BlockSpec((tq, LANES), lambda p, i: (i, p)),
                   pl.BlockSpec((1, SUBLANES, tq), lambda p, i: (p, 0, i))],
        out_shape=[jax.ShapeDtypeStruct((T, LANES * P), BF16), jax.ShapeDtypeStruct((P, SUBLANES, T), F32)],
        name=name, compiler_params=_cp(),
    )(qkv, qkv, qkv, cT)


def _attn_delta(do, o, P, name):
    T, D = o.shape
    tb = _tile(T, 256, LANES)

    def body(do_ref, o_ref, d_ref):
        lane = lax.broadcasted_iota(jnp.int32, (1, LANES), 1)
        for p in range(P):
            cols = slice(p * LANES, (p + 1) * LANES)
            prod = do_ref[:, cols].astype(F32) * o_ref[:, cols].astype(F32)
            d0 = jnp.sum(jnp.where(lane < 64, prod, 0.0), axis=1, keepdims=True)
            d1 = jnp.sum(jnp.where(lane >= 64, prod, 0.0), axis=1, keepdims=True)
            both = jnp.where(lane == 0, d0, jnp.where(lane == 1, d1, 0.0))
            d_ref[p] = both.T[0:SUBLANES, :]

    return pl.pallas_call(
        body, grid=(T // tb,),
        in_specs=[pl.BlockSpec((tb, D), lambda i: (i, 0)), pl.BlockSpec((tb, D), lambda i: (i, 0))],
        out_specs=pl.BlockSpec((P, SUBLANES, tb), lambda i: (0, 0, i)),
        out_shape=jax.ShapeDtypeStruct((P, SUBLANES, T), F32), name=name, compiler_params=_cp(),
    )(do, o)


def _attn_bwd(qkv, do, lseT, dT, c0T, c_cols, P, scale, name):
    T = qkv.shape[0]
    tq = _q_block(T)
    tw = _tile(T, 4 * tq, tq)
    nq = T // tq
    r = tw // tq

    def body(q_ref, do_ref, k_ref, v_ref, lse_ref, d_ref, c0_ref, cc_ref,
             dq_ref, dk_ref, dv_ref, dc_ref, drow_ref, dq_acc0, dq_acc1):
        j = pl.program_id(1)

        @pl.when(j == 0)
        def _():
            dq_acc0[...] = jnp.zeros_like(dq_acc0)
            dq_acc1[...] = jnp.zeros_like(dq_acc1)

        lane = lax.broadcasted_iota(jnp.int32, (1, LANES), 1)
        in_head = (lane < 64, lane >= 64)
        k = k_ref[...]
        v = v_ref[...]
        zero = jnp.zeros_like(k)
        one = jnp.ones_like(k)
        k_heads = tuple(jnp.where(h, k, zero) for h in in_head)
        v_heads = tuple(jnp.where(h, v, zero) for h in in_head)
        k_ones = tuple(jnp.where(h, k, one) for h in in_head)
        cc = cc_ref[0]
        c_first = (cc[0:1, 0:1], cc[0:1, 1:2])
        c_rel = (cc[:, 0:1] - c_first[0], cc[:, 1:2] - c_first[1])
        dq_accs = (dq_acc0, dq_acc1)

        def block(start, width, carry, masked):
            q = (q_ref[pl.ds(start, width), :].astype(F32) * scale).astype(BF16)
            q_one = jnp.ones_like(q)
            dov = do_ref[pl.ds(start, width), :]
            lse = lse_ref[0, :, pl.ds(start, width)]
            dlt = d_ref[0, :, pl.ds(start, width)]
            c0 = c0_ref[0, :, pl.ds(start, width)]
            new = []
            for a in range(2):
                dk_a, dv_a = carry[a]
                rowv = lse[a:a + 1, :] + (c_first[a] - c0[a:a + 1, :])
                st = _nt(k_heads[a], q)
                pt = jnp.exp((st - c_rel[a]) - rowv)
                if masked:
                    row = lax.broadcasted_iota(jnp.int32, (tq, width), 0)
                    col = lax.broadcasted_iota(jnp.int32, (tq, width), 1)
                    pt = jnp.where(col >= row, pt, 0.0)
                dpt = _nt(v_heads[a], dov)
                dst_b = (pt * (dpt - dlt[a:a + 1, :])).astype(BF16)
                dv_a = dv_a + jnp.dot(pt.astype(BF16), dov, preferred_element_type=F32)
                dk_a = dk_a + jnp.dot(dst_b, jnp.where(in_head[a], q, q_one), preferred_element_type=F32)
                dq_accs[a][pl.ds(start, width), :] += lax.dot_general(
                    dst_b, k_ones[a], (((0,), (0,)), ((), ())), preferred_element_type=F32)
                new.append((dk_a, dv_a))
            return tuple(new)

        init = tuple((jnp.zeros((tq, LANES), F32), jnp.zeros((tq, LANES), F32)) for _ in range(2))
        carry = block(pl.multiple_of(j * tq, tq), tq, init, True)
        first_wide = (j + r) // r
        carry = lax.fori_loop(
            j + 1, jnp.minimum(first_wide * r, nq),
            lambda i, cr: block(pl.multiple_of(i * tq, tq), tq, cr, False), carry)
        (dk0, dv0), (dk1, dv1) = lax.fori_loop(
            first_wide, nq // r, lambda i, cr: block(pl.multiple_of(i * tw, tw), tw, cr, False), carry)
        dk_ref[...] = jnp.where(lane < 64, dk0, dk1).astype(BF16)
        dv_ref[...] = jnp.where(lane < 64, dv0, dv1).astype(BF16)
        dc_ref[0] = jnp.where(lane == 0, -dk0[:, 64:65], jnp.where(lane == 1, -dk1[:, 0:1], 0.0))

        @pl.when(j == nq - 1)
        def _():
            def finish(i, _):
                rows = pl.ds(pl.multiple_of(i * tq, tq), tq)
                a0 = dq_acc0[rows, :]
                a1 = dq_acc1[rows, :]
                dq_ref[rows, :] = (jnp.where(lane < 64, a0, a1) * scale).astype(BF16)
                sums = jnp.where(lane == 0, a0[:, 64:65], jnp.where(lane == 1, a1[:, 0:1], 0.0))
                drow_ref[0, :, rows] = sums.T[0:SUBLANES, :]
                return 0

            lax.fori_loop(0, nq, finish, 0)

    full = lambda col: pl.BlockSpec((T, LANES), lambda p, j: (0, col(p)))
    blk = lambda col: pl.BlockSpec((tq, LANES), lambda p, j: (j, col(p)))
    rows = pl.BlockSpec((1, SUBLANES, T), lambda p, j: (p, 0, 0))
    cols = pl.BlockSpec((1, tq, LANES), lambda p, j: (p, j, 0))
    D = LANES * P
    return pl.pallas_call(
        body, grid=(P, nq),
        in_specs=[full(lambda p: p), full(lambda p: p), blk(lambda p: P + p), blk(lambda p: 2 * P + p),
                  rows, rows, rows, cols],
        out_specs=[full(lambda p: p), blk(lambda p: p), blk(lambda p: p), cols, rows],
        out_shape=[jax.ShapeDtypeStruct((T, D), BF16), jax.ShapeDtypeStruct((T, D), BF16),
                   jax.ShapeDtypeStruct((T, D), BF16), jax.ShapeDtypeStruct((P, T, LANES), F32),
                   jax.ShapeDtypeStruct((P, SUBLANES, T), F32)],
        scratch_shapes=[pltpu.VMEM((T, LANES), F32), pltpu.VMEM((T, LANES), F32)],
        name=name, compiler_params=_cp(),
    )(qkv, do, qkv, qkv, lseT, dT, c0T, c_cols)


_SQRT_HALF = 0.7071067811865476
_INV_SQRT_2PI = 0.3989422804014327


def _gelu(v):
    return 0.5 * v * (1.0 + lax.erf(v * _SQRT_HALF))


def _gelu_grad(v):
    return 0.5 * (1.0 + lax.erf(v * _SQRT_HALF)) + v * (_INV_SQRT_2PI * jnp.exp(-0.5 * v * v))


def _sgu_fwd(a, ln_g, ln_b, w_tril, bias, name):
    T, W2 = a.shape
    W = W2 // 2
    G = w_tril.shape[0]
    tb = _tile(T, 256, LANES)

    def body(a_ref, g_ref, b_ref, w_ref, bias_ref, out_ref):
        zu = _gelu(a_ref[:, :W].astype(F32))
        zv = _gelu(a_ref[:, W:].astype(F32))
        mu = jnp.mean(zv, axis=-1, keepdims=True)
        d = zv - mu
        rstd = lax.rsqrt(jnp.mean(d * d, axis=-1, keepdims=True) + LN_EPS)
        vn = (d * rstd * g_ref[...] + b_ref[...]).astype(BF16)
        for c in range(tb // LANES):
            rs = slice(c * LANES, (c + 1) * LANES)
            for g in range(G):
                cs = slice(g * LANES, (g + 1) * LANES)
                mixed = jnp.dot(w_ref[g], vn[rs, cs], preferred_element_type=F32) + bias_ref[:, cs]
                out_ref[rs, cs] = (zu[rs, cs] * mixed).astype(BF16)

    return pl.pallas_call(
        body, grid=(T // tb,),
        in_specs=[pl.BlockSpec((tb, W2), lambda i: (i, 0)), pl.BlockSpec((1, W), lambda i: (0, 0)),
                  pl.BlockSpec((1, W), lambda i: (0, 0)), pl.BlockSpec((G, LANES, LANES), lambda i: (0, 0, 0)),
                  pl.BlockSpec((LANES, W), lambda i: (0, 0))],
        out_specs=pl.BlockSpec((tb, W), lambda i: (i, 0)),
        out_shape=jax.ShapeDtypeStruct((T, W), BF16), name=name, compiler_params=_cp(),
    )(a, ln_g.reshape(1, W), ln_b.reshape(1, W), w_tril, bias)


def _sgu_bwd(a, dgated, ln_g, ln_b, w_tril, w_tril_t, bias, name):
    T, W2 = a.shape
    W = W2 // 2
    G = w_tril.shape[0]
    tb = _tile(T, 256, LANES)

    def body(a_ref, dg_ref, g_ref, b_ref, w_ref, wt_ref, bias_ref,
             da_ref, dws_ref, dbias_ref, dlng_ref, dlnb_ref, dvn_ref):
        @pl.when(pl.program_id(0) == 0)
        def _():
            dws_ref[...] = jnp.zeros_like(dws_ref)
            dbias_ref[...] = jnp.zeros_like(dbias_ref)
            dlng_ref[...] = jnp.zeros_like(dlng_ref)
            dlnb_ref[...] = jnp.zeros_like(dlnb_ref)

        up = a_ref[:, :W].astype(F32)
        vp = a_ref[:, W:].astype(F32)
        zu = _gelu(up)
        zv = _gelu(vp)
        mu = jnp.mean(zv, axis=-1, keepdims=True)
        d = zv - mu
        rstd = lax.rsqrt(jnp.mean(d * d, axis=-1, keepdims=True) + LN_EPS)
        vhat = d * rstd
        gam = g_ref[...]
        vn = (vhat * gam + b_ref[...]).astype(BF16)
        dgated = dg_ref[...]
        for c in range(tb // LANES):
            rs = slice(c * LANES, (c + 1) * LANES)
            for g in range(G):
                cs = slice(g * LANES, (g + 1) * LANES)
                vb = vn[rs, cs]
                mixed = jnp.dot(w_ref[g], vb, preferred_element_type=F32) + bias_ref[:, cs]
                dgt = dgated[rs, cs]
                da_ref[rs, cs] = (dgt * mixed * _gelu_grad(up[rs, cs])).astype(BF16)
                dmx = dgt * zu[rs, cs]
                dbias_ref[:, cs] += dmx
                dmb = dmx.astype(BF16)
                dws_ref[g] += _nt(dmb, vb)
                dvn_ref[rs, cs] = jnp.dot(wt_ref[g], dmb, preferred_element_type=F32)
        dvn = dvn_ref[...]
        dlng_ref[...] += jnp.sum(dvn * vhat, axis=0, keepdims=True)
        dlnb_ref[...] += jnp.sum(dvn, axis=0, keepdims=True)
        dvh = dvn * gam
        dzv = rstd * (dvh - jnp.mean(dvh, axis=-1, keepdims=True)
                      - vhat * jnp.mean(dvh * vhat, axis=-1, keepdims=True))
        da_ref[:, W:] = (dzv * _gelu_grad(vp)).astype(BF16)

    const2 = lambda shape: pl.BlockSpec(shape, lambda i: (0, 0))
    const3 = pl.BlockSpec((G, LANES, LANES), lambda i: (0, 0, 0))
    return pl.pallas_call(
        body, grid=(T // tb,),
        in_specs=[pl.BlockSpec((tb, W2), lambda i: (i, 0)), pl.BlockSpec((tb, W), lambda i: (i, 0)),
                  const2((1, W)), const2((1, W)), const3, const3, const2((LANES, W))],
        out_specs=[pl.BlockSpec((tb, W2), lambda i: (i, 0)), const3, const2((LANES, W)),
                   const2((SUBLANES, W)), const2((SUBLANES, W))],
        out_shape=[jax.ShapeDtypeStruct((T, W2), BF16), jax.ShapeDtypeStruct((G, LANES, LANES), F32),
                   jax.ShapeDtypeStruct((LANES, W), F32), jax.ShapeDtypeStruct((SUBLANES, W), F32),
                   jax.ShapeDtypeStruct((SUBLANES, W), F32)],
        scratch_shapes=[pltpu.VMEM((tb, W), F32)],
        name=name, compiler_params=_cp(),
    )(a, dgated, ln_g.reshape(1, W), ln_b.reshape(1, W), w_tril, w_tril_t, bias)


def _adam_math(w, g, m, v):
    m = ADAM_B1 * m + (1.0 - ADAM_B1) * g
    v = ADAM_B2 * v + (1.0 - ADAM_B2) * (g * g)
    m_hat = m / (1.0 - ADAM_B1 ** ADAM_STEP)
    v_hat = v / (1.0 - ADAM_B2 ** ADAM_STEP)
    delta = -ADAM_LR * (m_hat / (jnp.sqrt(v_hat) + ADAM_EPS) + ADAM_WD * w)
    return delta, m, v


def _adamw_halves(mine, theirs, c_idx, w, m, v, name):
    R, C = w.shape
    rh = R // 2
    tb = _tile(rh, 512, SUBLANES)
    nb = rh // tb

    def body(c_ref, a_ref, b_ref, w_ref, m_ref, v_ref, g_ref, d_ref, mo_ref, vo_ref):
        g = jnp.where(pl.program_id(0) == c_ref[0], a_ref[...], b_ref[...])
        d, mm, vv = _adam_math(w_ref[...], g, m_ref[...], v_ref[...])
        g_ref[...] = g
        d_ref[...] = d
        mo_ref[...] = mm
        vo_ref[...] = vv

    half = pl.BlockSpec((tb, C), lambda h, i, c: (i, 0))
    row = pl.BlockSpec((tb, C), lambda h, i, c: (h * nb + i, 0))
    sds = jax.ShapeDtypeStruct((R, C), F32)
    return pl.pallas_call(
        body,
        grid_spec=pltpu.PrefetchScalarGridSpec(
            num_scalar_prefetch=1, grid=(2, nb), in_specs=[half, half, row, row, row], out_specs=[row] * 4),
        out_shape=[sds] * 4, name=name, compiler_params=_cp())(c_idx, mine, theirs, w, m, v)


def _adamw_sum(parts, w, m, v, name):
    K, R, C = parts.shape
    tb = _tile(R, 128, SUBLANES)

    def body(p_ref, w_ref, m_ref, v_ref, g_ref, d_ref, mo_ref, vo_ref):
        g = p_ref[0]
        for k in range(1, K):
            g = g + p_ref[k]
        d, mm, vv = _adam_math(w_ref[...], g, m_ref[...], v_ref[...])
        g_ref[...] = g
        d_ref[...] = d
        mo_ref[...] = mm
        vo_ref[...] = vv

    row = pl.BlockSpec((tb, C), lambda i: (i, 0))
    sds = jax.ShapeDtypeStruct((R, C), F32)
    return pl.pallas_call(
        body, grid=(R // tb,),
        in_specs=[pl.BlockSpec((K, tb, C), lambda i: (0, i, 0)), row, row, row],
        out_specs=[row] * 4, out_shape=[sds] * 4, name=name, compiler_params=_cp())(parts, w, m, v)


def _pair_sum(g_all, recv, c_idx, name):
    K, R, C = g_all.shape
    rh = R // 2
    tb = _tile(rh, 512, SUBLANES)
    nb = rh // tb

    def body(c_ref, a_ref, b_ref, o_ref):
        o_ref[...] = a_ref[...] + b_ref[...]

    return pl.pallas_call(
        body,
        grid_spec=pltpu.PrefetchScalarGridSpec(
            num_scalar_prefetch=1, grid=(K, nb),
            in_specs=[pl.BlockSpec((1, tb, C), lambda k, i, c: (k, c[0] * nb + i, 0)),
                      pl.BlockSpec((1, tb, C), lambda k, i, c: (k, i, 0))],
            out_specs=pl.BlockSpec((1, tb, C), lambda k, i, c: (k, i, 0))),
        out_shape=jax.ShapeDtypeStruct((K, rh, C), F32), name=name, compiler_params=_cp(),
    )(c_idx, g_all, recv)


def _sum_parts(parts, name):
    K, R, C = parts.shape
    tb = _tile(R, 512, SUBLANES)

    def body(p_ref, o_ref):
        g = p_ref[0]
        for k in range(1, K):
            g = g + p_ref[k]
        o_ref[...] = g

    return pl.pallas_call(
        body, grid=(R // tb,), in_specs=[pl.BlockSpec((K, tb, C), lambda i: (0, i, 0))],
        out_specs=pl.BlockSpec((tb, C), lambda i: (i, 0)),
        out_shape=jax.ShapeDtypeStruct((R, C), F32), name=name, compiler_params=_cp())(parts)


_CHIP_RELATIONS = ((1, 0), (0, 1), (1, 1))


def _position():
    return lax.axis_index("x"), lax.axis_index("y"), lax.axis_index("c")


def _flip(v, bit):
    return 1 - v if bit else v


def _gather_weights(w_pack, ln_pack, name):
    R, C = w_pack.shape

    def body(w_ref, ln_ref, ow_ref, oln_ref, local_sems, send_sems, recv_sems):
        x, y, c = _position()
        me = 2 * x + y
        own_w = pltpu.make_async_copy(w_ref, ow_ref.at[me], local_sems.at[0])
        own_ln = pltpu.make_async_copy(ln_ref, oln_ref.at[me], local_sems.at[1])
        own_w.start()
        own_ln.start()

        def copies(r, slot):
            dx, dy = _CHIP_RELATIONS[r]
            peer = (_flip(x, dx), _flip(y, dy), c)
            cw = pltpu.make_async_remote_copy(
                src_ref=w_ref, dst_ref=ow_ref.at[slot], send_sem=send_sems.at[2 * r],
                recv_sem=recv_sems.at[2 * r], device_id=peer, device_id_type=MESH)
            cl = pltpu.make_async_remote_copy(
                src_ref=ln_ref, dst_ref=oln_ref.at[slot], send_sem=send_sems.at[2 * r + 1],
                recv_sem=recv_sems.at[2 * r + 1], device_id=peer, device_id_type=MESH)
            return cw, cl

        sent = [copies(r, me) for r in range(3)]
        for cw, cl in sent:
            cw.start()
            cl.start()
        for r in range(3):
            dx, dy = _CHIP_RELATIONS[r]
            cw, cl = copies(r, 2 * _flip(x, dx) + _flip(y, dy))
            cw.wait_recv()
            cl.wait_recv()
        for cw, cl in sent:
            cw.wait_send()
            cl.wait_send()
        own_w.wait()
        own_ln.wait()

    return pl.pallas_call(
        body, in_specs=[_hbm(), _hbm()], out_specs=[_hbm(), _hbm()],
        out_shape=[jax.ShapeDtypeStruct((4, R, C), w_pack.dtype), jax.ShapeDtypeStruct((4, SUBLANES, C), F32)],
        scratch_shapes=[pltpu.SemaphoreType.DMA((2,)), pltpu.SemaphoreType.DMA((6,)),
                        pltpu.SemaphoreType.DMA((6,))],
        name=name, compiler_params=_cp(),
    )(w_pack, ln_pack)


def _sibling_halves(g_all, name):
    K, R, C = g_all.shape
    rh = R // 2

    def body(g_ref, o_ref, send_sem, recv_sem):
        x, y, c = _position()
        start = pl.multiple_of((1 - c) * rh, SUBLANES)
        cp = pltpu.make_async_remote_copy(
            src_ref=g_ref.at[:, pl.ds(start, rh), :], dst_ref=o_ref, send_sem=send_sem, recv_sem=recv_sem,
            device_id=(x, y, 1 - c), device_id_type=MESH)
        cp.start()
        cp.wait_recv()
        cp.wait_send()

    return pl.pallas_call(
        body, in_specs=[_hbm()], out_specs=_hbm(),
        out_shape=jax.ShapeDtypeStruct((K, rh, C), F32),
        scratch_shapes=[pltpu.SemaphoreType.DMA(()), pltpu.SemaphoreType.DMA(())],
        name=name, compiler_params=_cp(),
    )(g_all)


def _chip_exchange(parts, name):
    K, R, C = parts.shape

    def body(p_ref, o_ref, local_sem, send_sems, recv_sems):
        x, y, c = _position()
        me = 2 * x + y
        own = pltpu.make_async_copy(p_ref.at[me], o_ref.at[me], local_sem)
        own.start()

        def copy(r, src_slot, dst_slot):
            dx, dy = _CHIP_RELATIONS[r]
            return pltpu.make_async_remote_copy(
                src_ref=p_ref.at[src_slot], dst_ref=o_ref.at[dst_slot], send_sem=send_sems.at[r],
                recv_sem=recv_sems.at[r], device_id=(_flip(x, dx), _flip(y, dy), c), device_id_type=MESH)

        def chip(r):
            dx, dy = _CHIP_RELATIONS[r]
            return 2 * _flip(x, dx) + _flip(y, dy)

        sent = [copy(r, chip(r), me) for r in range(3)]
        for cp in sent:
            cp.start()
        for r in range(3):
            copy(r, me, chip(r)).wait_recv()
        for cp in sent:
            cp.wait_send()
        own.wait()

    return pl.pallas_call(
        body, in_specs=[_hbm()], out_specs=_hbm(),
        out_shape=jax.ShapeDtypeStruct((K, R, C), F32),
        scratch_shapes=[pltpu.SemaphoreType.DMA(()), pltpu.SemaphoreType.DMA((3,)),
                        pltpu.SemaphoreType.DMA((3,))],
        name=name, compiler_params=_cp(),
    )(parts)


def _swap_with_sibling(half, name):
    rh, C = half.shape

    def body(h_ref, o_ref, send_sem, recv_sem):
        x, y, c = _position()
        cp = pltpu.make_async_remote_copy(
            src_ref=h_ref, dst_ref=o_ref, send_sem=send_sem, recv_sem=recv_sem,
            device_id=(x, y, 1 - c), device_id_type=MESH)
        cp.start()
        cp.wait_recv()
        cp.wait_send()

    return pl.pallas_call(
        body, in_specs=[_hbm()], out_specs=_hbm(),
        out_shape=jax.ShapeDtypeStruct((rh, C), F32),
        scratch_shapes=[pltpu.SemaphoreType.DMA(()), pltpu.SemaphoreType.DMA(())],
        name=name, compiler_params=_cp(),
    )(half)


def _gather_all(part, name):
    R, C = part.shape
    masks = [(b >> 2 & 1, b >> 1 & 1, b & 1) for b in range(1, 8)]

    def body(p_ref, o_ref, local_sem, send_sems, recv_sems):
        x, y, c = _position()
        me = 4 * x + 2 * y + c
        own = pltpu.make_async_copy(p_ref, o_ref.at[me], local_sem)
        own.start()

        def copy(r, slot):
            dx, dy, dc = masks[r]
            return pltpu.make_async_remote_copy(
                src_ref=p_ref, dst_ref=o_ref.at[slot], send_sem=send_sems.at[r], recv_sem=recv_sems.at[r],
                device_id=(_flip(x, dx), _flip(y, dy), _flip(c, dc)), device_id_type=MESH)

        sent = [copy(r, me) for r in range(7)]
        for cp in sent:
            cp.start()
        for r in range(7):
            dx, dy, dc = masks[r]
            copy(r, 4 * _flip(x, dx) + 2 * _flip(y, dy) + _flip(c, dc)).wait_recv()
        for cp in sent:
            cp.wait_send()
        own.wait()

    return pl.pallas_call(
        body, in_specs=[_hbm()], out_specs=_hbm(),
        out_shape=jax.ShapeDtypeStruct((8, R, C), F32),
        scratch_shapes=[pltpu.SemaphoreType.DMA(()), pltpu.SemaphoreType.DMA((7,)),
                        pltpu.SemaphoreType.DMA((7,))],
        name=name, compiler_params=_cp(),
    )(part)


def _pack(arrs, row_mult):
    flat = jnp.concatenate([a.reshape(-1).astype(F32) for a in arrs])
    rows = -(-flat.shape[0] // PACK_COLS)
    rows = -(-rows // row_mult) * row_mult
    flat = jnp.pad(flat, (0, rows * PACK_COLS - flat.shape[0]))
    return flat.reshape(rows, PACK_COLS)


def _unpack(buf, shapes):
    lead = buf.shape[:-2]
    flat = buf.reshape(lead + (-1,))
    out, off = [], 0
    for shp in shapes:
        n = math.prod(shp)
        out.append(flat[..., off:off + n].reshape(lead + tuple(shp)))
        off += n
    return out


def _cols_from_chips(g):
    k, L, A, n = g.shape
    return jnp.transpose(g, (1, 2, 0, 3)).reshape(L, A, k * n)


def _rows_from_chips(g):
    k, L, n, B = g.shape
    return jnp.transpose(g, (1, 0, 2, 3)).reshape(L, k * n, B)


def _cols_to_chips(full, k=4):
    L, A, N = full.shape
    return jnp.transpose(full.reshape(L, A, k, N // k), (2, 0, 1, 3))


def _rows_to_chips(full, k=4):
    L, N, B = full.shape
    return jnp.transpose(full.reshape(L, k, N // k, B), (1, 0, 2, 3))


def kernel(x, mixer_norm_w, attn_w_in, attn_b_f, attn_w_out, sgu_w_in, sgu_ln_g, sgu_ln_b, sgu_w_s, sgu_b_s, sgu_w_out, ffn_norm_w, ffn_w_in, ffn_w_out, final_norm_w, loss_target, m_mixer_norm_w, m_attn_w_in, m_attn_b_f, m_attn_w_out, m_sgu_w_in, m_sgu_ln_g, m_sgu_ln_b, m_sgu_w_s, m_sgu_b_s, m_sgu_w_out, m_ffn_norm_w, m_ffn_w_in, m_ffn_w_out, m_final_norm_w, v_mixer_norm_w, v_attn_w_in, v_attn_b_f, v_attn_w_out, v_sgu_w_in, v_sgu_ln_g, v_sgu_ln_b, v_sgu_w_s, v_sgu_b_s, v_sgu_w_out, v_ffn_norm_w, v_ffn_w_in, v_ffn_w_out, v_final_norm_w):
    T, D = x.shape[1], x.shape[2]
    depth = mixer_norm_w.shape[0]
    H = attn_b_f.shape[1]
    P = D // LANES
    assert D % LANES == 0 and D // H == 64 and 2 * P == H and 2 * P <= LANES
    G = sgu_w_s.shape[1]
    W = sgu_w_out.shape[1] * 4
    assert sgu_w_s.shape[2] == LANES and W == G * LANES
    scale = float(D // H) ** -0.5
    f_pad = LANES
    c_idx = lax.axis_index("c").astype(jnp.int32).reshape(1)

    sharded = [attn_w_in, attn_w_out, sgu_w_in, sgu_w_out, ffn_w_in, ffn_w_out, sgu_ln_g, sgu_ln_b]
    sharded_m = [m_attn_w_in, m_attn_w_out, m_sgu_w_in, m_sgu_w_out, m_ffn_w_in, m_ffn_w_out, m_sgu_ln_g, m_sgu_ln_b]
    sharded_v = [v_attn_w_in, v_attn_w_out, v_sgu_w_in, v_sgu_w_out, v_ffn_w_in, v_ffn_w_out, v_sgu_ln_g, v_sgu_ln_b]
    shard_shapes = [a.shape for a in sharded]
    w_pack = _pack(sharded, 512)
    ln_pack = _pack([sgu_ln_g, sgu_ln_b], SUBLANES)
    gat_w, gat_ln = _gather_weights(w_pack.astype(BF16), ln_pack, "gather_weights")
    g_ai, g_ao, g_si, g_so, g_fi, g_fo, _, _ = _unpack(gat_w, shard_shapes)
    g_lng, g_lnb = _unpack(gat_ln, [sgu_ln_g.shape, sgu_ln_b.shape])
    w_ai = _cols_from_chips(g_ai)
    w_ai = jnp.pad(w_ai, ((0, 0), (0, 0), (0, 3 * D + f_pad - w_ai.shape[2])))
    w_ao = _rows_from_chips(g_ao)
    w_si = _cols_from_chips(g_si)
    w_so = _rows_from_chips(g_so)
    w_fo = _rows_from_chips(g_fo)
    ffn_tf = _ffn_tile(w_fo.shape[1])
    w_fi_il = _interleave_gu(_cols_from_chips(g_fi), ffn_tf)
    ln_g = jnp.transpose(g_lng, (1, 0, 2)).reshape(sgu_ln_g.shape[0], W)
    ln_b = jnp.transpose(g_lnb, (1, 0, 2)).reshape(sgu_ln_b.shape[0], W)
    w_tril = jnp.tril(sgu_w_s)
    w_tril_b = w_tril.astype(BF16)
    w_tril_tb = jnp.swapaxes(w_tril, 2, 3).astype(BF16)
    sgu_bias = jnp.repeat(jnp.swapaxes(sgu_b_s, 1, 2), LANES, axis=2)
    b_f_pad = jnp.pad(attn_b_f, ((0, 0), (0, LANES - H)))

    xs = x.reshape(T, D)
    saved = []
    for i in range(depth):
        j = i // 2
        h = _rmsnorm_fwd(xs, mixer_norm_w[i], f"mix_norm_{i}")
        rec = {"x_in": xs, "h": h}
        if i % 2 == 0:
            qkv = _mm(h, w_ai[j, :, :3 * D], "nn", BF16, f"attn_qkv_{i}")
            f = _mm(h, w_ai[j, :, 3 * D:], "nn", F32, f"attn_gate_{i}")
            cT, c_cols, c0T = _gate_fwd(f, b_f_pad[j:j + 1], P, f"gate_fwd_{i}")
            o, lseT = _attn_fwd(qkv, cT, P, scale, f"attn_fwd_{i}")
            x_mid = _mm(o, w_ao[j], "nn", F32, f"attn_out_{i}", res=xs)
            rec.update(qkv=qkv, f=f, c0T=c0T, c_cols=c_cols, o=o, lseT=lseT)
        else:
            a = _mm(h, w_si[j], "nn", BF16, f"sgu_in_{i}")
            gated = _sgu_fwd(a, ln_g[j], ln_b[j], w_tril_b[j], sgu_bias[j], f"sgu_fwd_{i}")
            x_mid = _mm(gated, w_so[j], "nn", F32, f"sgu_out_{i}", res=xs)
            rec.update(a=a, gated=gated)
        h2 = _rmsnorm_fwd(x_mid, ffn_norm_w[i], f"ffn_norm_{i}")
        fa, s = _ffn_in_act(h2, w_fi_il[i], f"ffn_in_{i}")
        xs = _mm(s, w_fo[i], "nn", F32, f"ffn_out_{i}", res=x_mid)
        rec.update(x_mid=x_mid, h2=h2, fa=fa, s=s)
        saved.append(rec)

    gx, loss_acc, dw_final = _loss_head(xs, final_norm_w, loss_target.reshape(T, D), "loss_head")
    loss = lax.psum(loss_acc[0, 0], ("x", "y", "c"))

    n_attn, n_sgu = attn_w_in.shape[0], sgu_w_in.shape[0]
    d_mixer_norm, d_ffn_norm = [None] * depth, [None] * depth
    d_ai, d_ao, d_bf = [None] * n_attn, [None] * n_attn, [None] * n_attn
    d_si, d_so, d_lng, d_lnb, d_ws, d_bs = ([None] * n_sgu for _ in range(6))
    d_fi, d_fo = [None] * depth, [None] * depth
    for i in reversed(range(depth)):
        j = i // 2
        rec = saved[i]
        d_fo[i] = _mm(rec["s"], gx, "tn", F32, f"ffn_out_wgrad_{i}")
        da = _ffn_out_bwd_act(gx, w_fo[i], rec["fa"], f"ffn_out_bwd_{i}")
        dh2 = _mm(da, w_fi_il[i], "nt", F32, f"ffn_in_bwd_{i}")
        d_fi[i] = _deinterleave_gu(_mm(rec["h2"], da, "tn", F32, f"ffn_in_wgrad_{i}"), ffn_tf)
        gx, dwn = _rmsnorm_bwd(dh2, rec["x_mid"], ffn_norm_w[i], gx, f"ffn_norm_bwd_{i}")
        d_ffn_norm[i] = dwn[0]
        if i % 2 == 0:
            do = _mm(gx, w_ao[j], "nt", BF16, f"attn_out_bwd_{i}")
            d_ao[j] = _mm(rec["o"], gx, "tn", F32, f"attn_out_wgrad_{i}")
            dT = _attn_delta(do, rec["o"], P, f"attn_delta_{i}")
            dq, dk, dv, dc_cols, drowT = _attn_bwd(rec["qkv"], do, rec["lseT"], dT, rec["c0T"], rec["c_cols"],
                                                   P, scale, f"attn_bwd_{i}")
            df, dbf = _gate_bwd(dc_cols, drowT, rec["f"], b_f_pad[j:j + 1], P, f"gate_bwd_{i}")
            d_bf[j] = dbf[0, :H]
            dproj = jnp.concatenate([dq, dk, dv, df.astype(BF16)], axis=1)
            dh = _mm(dproj, w_ai[j], "nt", F32, f"attn_in_bwd_{i}")
            d_ai[j] = _mm(rec["h"], dproj, "tn", F32, f"attn_in_wgrad_{i}")[:, :3 * D + H]
        else:
            dgated = _mm(gx, w_so[j], "nt", F32, f"sgu_out_bwd_{i}")
            d_so[j] = _mm(rec["gated"], gx, "tn", F32, f"sgu_out_wgrad_{i}")
            da_s, dws, dbias, dlng, dlnb = _sgu_bwd(rec["a"], dgated, ln_g[j], ln_b[j], w_tril_b[j],
                                                    w_tril_tb[j], sgu_bias[j], f"sgu_bwd_{i}")
            d_ws[j] = jnp.tril(dws)
            d_bs[j] = jnp.sum(dbias.reshape(LANES, G, LANES), axis=2).T
            d_lng[j], d_lnb[j] = dlng[0], dlnb[0]
            dh = _mm(da_s, w_si[j], "nt", F32, f"sgu_in_bwd_{i}")
            d_si[j] = _mm(rec["h"], da_s, "tn", F32, f"sgu_in_wgrad_{i}")
        gx, dwn = _rmsnorm_bwd(dh, rec["x_in"], mixer_norm_w[i], gx, f"mix_norm_bwd_{i}")
        d_mixer_norm[i] = dwn[0]
    grad_x = gx.reshape(x.shape)

    full_grads = [
        _cols_to_chips(jnp.stack(d_ai)), _rows_to_chips(jnp.stack(d_ao)),
        _cols_to_chips(jnp.stack(d_si)), _rows_to_chips(jnp.stack(d_so)),
        _cols_to_chips(jnp.stack(d_fi)), _rows_to_chips(jnp.stack(d_fo)),
        jnp.transpose(jnp.stack(d_lng).reshape(n_sgu, 4, W // 4), (1, 0, 2)),
        jnp.transpose(jnp.stack(d_lnb).reshape(n_sgu, 4, W // 4), (1, 0, 2)),
    ]
    g_all = jnp.stack([_pack([fg[k] for fg in full_grads], 512) for k in range(4)])
    from_sibling = _sibling_halves(g_all, "grad_sibling_halves")
    pair = _pair_sum(g_all, from_sibling, c_idx, "grad_pair_sum")
    from_chips = _chip_exchange(pair, "grad_chip_exchange")
    my_half = _sum_parts(from_chips, "grad_chip_sum")
    sibling_half = _swap_with_sibling(my_half, "grad_swap_halves")
    g_shard, d_shard, m_shard, v_shard = _adamw_halves(
        my_half, sibling_half, c_idx, w_pack, _pack(sharded_m, 512), _pack(sharded_v, 512), "adamw_sharded")
    g_sh = _unpack(g_shard, shard_shapes)
    d_sh = _unpack(d_shard, shard_shapes)
    m_sh = _unpack(m_shard, shard_shapes)
    v_sh = _unpack(v_shard, shard_shapes)

    repl = [mixer_norm_w, attn_b_f, sgu_w_s, sgu_b_s, ffn_norm_w, final_norm_w]
    repl_m = [m_mixer_norm_w, m_attn_b_f, m_sgu_w_s, m_sgu_b_s, m_ffn_norm_w, m_final_norm_w]
    repl_v = [v_mixer_norm_w, v_attn_b_f, v_sgu_w_s, v_sgu_b_s, v_ffn_norm_w, v_final_norm_w]
    repl_shapes = [a.shape for a in repl]
    repl_grads = [jnp.stack(d_mixer_norm), jnp.stack(d_bf), jnp.stack(d_ws), jnp.stack(d_bs),
                  jnp.stack(d_ffn_norm), dw_final[0]]
    parts = _gather_all(_pack(repl_grads, SUBLANES), "grad_gather_replicated")
    g_rep, d_rep, m_rep, v_rep = _adamw_sum(parts, _pack(repl, SUBLANES), _pack(repl_m, SUBLANES),
                                            _pack(repl_v, SUBLANES), "adamw_replicated")
    g_r = _unpack(g_rep, repl_shapes)
    d_r = _unpack(d_rep, repl_shapes)
    m_r = _unpack(m_rep, repl_shapes)
    v_r = _unpack(v_rep, repl_shapes)

    def ordered(sh, rp):
        ai, ao, si, so, fi, fo, lng, lnb = sh
        mn, bf, ws, bs, fn, fin = rp
        return [mn, ai, bf, ao, si, lng, lnb, ws, bs, so, fn, fi, fo, fin]

    return (loss, grad_x, *ordered(g_sh, g_r), *ordered(d_sh, d_r), *ordered(m_sh, m_r), *ordered(v_sh, v_r))
```

```python
import functools
import math

import jax
import jax.numpy as jnp
from jax import lax
from jax.experimental import pallas as pl
from jax.experimental.pallas import tpu as pltpu

F32 = jnp.float32
BF16 = jnp.bfloat16
NORM_EPS = 1e-6
LN_EPS = 1e-5
ADAM_LR = 0.001
ADAM_B1 = 0.9
ADAM_B2 = 0.999
ADAM_EPS = 1e-08
ADAM_WD = 0.01
ADAM_STEP = 10

LANES = 128
SUBLANES = 8
PACK_COLS = 1024
VMEM_LIMIT = 56 * 1024 * 1024
NEG_BIG = -1e30
MESH = pl.DeviceIdType.MESH


def _cp():
    return pltpu.CompilerParams(vmem_limit_bytes=VMEM_LIMIT)


def _tile(n, cap, mult):
    best = None
    d = mult
    while d <= min(n, cap):
        if n % d == 0:
            best = d
        d += mult
    return n if best is None else best


def _hbm():
    return pl.BlockSpec(memory_space=pltpu.HBM)


def _rmsnorm_fwd(x, w, name):
    T, D = x.shape
    tm = _tile(T, 512, SUBLANES)

    def body(x_ref, w_ref, h_ref):
        xf = x_ref[...]
        r = lax.rsqrt(jnp.mean(xf * xf, axis=-1, keepdims=True) + NORM_EPS)
        h_ref[...] = (xf * r * w_ref[...]).astype(BF16)

    return pl.pallas_call(
        body, grid=(T // tm,),
        in_specs=[pl.BlockSpec((tm, D), lambda i: (i, 0)), pl.BlockSpec((1, D), lambda i: (0, 0))],
        out_specs=pl.BlockSpec((tm, D), lambda i: (i, 0)),
        out_shape=jax.ShapeDtypeStruct((T, D), BF16), name=name, compiler_params=_cp(),
    )(x, w.reshape(1, D))


def _rmsnorm_bwd(dh, x, w, dres, name):
    T, D = x.shape
    tm = _tile(T, 512, SUBLANES)

    def body(dh_ref, x_ref, w_ref, dres_ref, dx_ref, dw_ref):
        @pl.when(pl.program_id(0) == 0)
        def _():
            dw_ref[...] = jnp.zeros_like(dw_ref)

        xf = x_ref[...]
        r = lax.rsqrt(jnp.mean(xf * xf, axis=-1, keepdims=True) + NORM_EPS)
        xhat = xf * r
        dhv = dh_ref[...]
        dxhat = dhv * w_ref[...]
        dx_ref[...] = dres_ref[...] + r * (dxhat - xhat * jnp.mean(dxhat * xhat, axis=-1, keepdims=True))
        dw_ref[...] += jnp.sum(dhv * xhat, axis=0, keepdims=True)

    row = pl.BlockSpec((tm, D), lambda i: (i, 0))
    return pl.pallas_call(
        body, grid=(T // tm,),
        in_specs=[row, row, pl.BlockSpec((1, D), lambda i: (0, 0)), row],
        out_specs=[row, pl.BlockSpec((SUBLANES, D), lambda i: (0, 0))],
        out_shape=[jax.ShapeDtypeStruct((T, D), F32), jax.ShapeDtypeStruct((SUBLANES, D), F32)],
        name=name, compiler_params=_cp(),
    )(dh, x, w.reshape(1, D), dres)


def _mm(a, b, mode, out_dtype, name, res=None):
    if mode == "tn":
        kt, M = a.shape
        N = b.shape[1]
        tm = _tile(M, 1408, LANES)
        tn = _tile(N, 1408, LANES)
        tk = _tile(kt, 512, 16)

        def body(a_ref, b_ref, o_ref):
            @pl.when(pl.program_id(2) == 0)
            def _():
                o_ref[...] = jnp.zeros_like(o_ref)

            o_ref[...] += lax.dot_general(
                a_ref[...].astype(BF16), b_ref[...].astype(BF16), (((0,), (0,)), ((), ())),
                preferred_element_type=F32)

        return pl.pallas_call(
            body, grid=(M // tm, N // tn, kt // tk),
            in_specs=[pl.BlockSpec((tk, tm), lambda i, j, k: (k, i)),
                      pl.BlockSpec((tk, tn), lambda i, j, k: (k, j))],
            out_specs=pl.BlockSpec((tm, tn), lambda i, j, k: (i, j)),
            out_shape=jax.ShapeDtypeStruct((M, N), F32), name=name, compiler_params=_cp(),
        )(a, b)

    M, K = a.shape
    N = b.shape[1] if mode == "nn" else b.shape[0]
    tm = _tile(M, 512, 16)
    cap = max(LANES, min(1408, ((6 << 20) // (2 * K)) // LANES * LANES))
    tn = _tile(N, cap, LANES)
    dims = (((1,), (0,)), ((), ())) if mode == "nn" else (((1,), (1,)), ((), ()))

    def body(*refs):
        if res is None:
            a_ref, b_ref, o_ref = refs
        else:
            a_ref, b_ref, r_ref, o_ref = refs
        acc = lax.dot_general(a_ref[...].astype(BF16), b_ref[...].astype(BF16), dims,
                              preferred_element_type=F32)
        if res is not None:
            acc = acc + r_ref[...]
        o_ref[...] = acc.astype(out_dtype)

    b_spec = (pl.BlockSpec((K, tn), lambda i, j: (0, j)) if mode == "nn"
              else pl.BlockSpec((tn, K), lambda i, j: (j, 0)))
    in_specs = [pl.BlockSpec((tm, K), lambda i, j: (i, 0)), b_spec]
    args = [a, b]
    if res is not None:
        in_specs.append(pl.BlockSpec((tm, tn), lambda i, j: (i, j)))
        args.append(res)
    return pl.pallas_call(
        body, grid=(M // tm, N // tn), in_specs=in_specs,
        out_specs=pl.BlockSpec((tm, tn), lambda i, j: (i, j)),
        out_shape=jax.ShapeDtypeStruct((M, N), out_dtype), name=name, compiler_params=_cp(),
    )(*args)


def _ffn_tile(F):
    return _tile(F, 1408, LANES)


def _interleave_gu(w, tf):
    lead, F2 = w.shape[:-1], w.shape[-1]
    n = F2 // (2 * tf)
    return jnp.swapaxes(w.reshape(lead + (2, n, tf)), -3, -2).reshape(lead + (F2,))


def _deinterleave_gu(w, tf):
    lead, F2 = w.shape[:-1], w.shape[-1]
    n = F2 // (2 * tf)
    return jnp.swapaxes(w.reshape(lead + (n, 2, tf)), -3, -2).reshape(lead + (F2,))


def _ffn_in_act(h, w_il, name):
    T, D = h.shape
    F = w_il.shape[1] // 2
    tf = _ffn_tile(F)
    tm = _tile(T, 256, 16)

    def body(h_ref, w_ref, a_ref, s_ref):
        acc = jnp.dot(h_ref[...], w_ref[...], preferred_element_type=F32)
        a_ref[...] = acc.astype(BF16)
        g = acc[:, :tf]
        s_ref[...] = (g * jax.nn.sigmoid(g) * acc[:, tf:]).astype(BF16)

    return pl.pallas_call(
        body, grid=(F // tf, T // tm),
        in_specs=[pl.BlockSpec((tm, D), lambda j, i: (i, 0)), pl.BlockSpec((D, 2 * tf), lambda j, i: (0, j))],
        out_specs=[pl.BlockSpec((tm, 2 * tf), lambda j, i: (i, j)), pl.BlockSpec((tm, tf), lambda j, i: (i, j))],
        out_shape=[jax.ShapeDtypeStruct((T, 2 * F), BF16), jax.ShapeDtypeStruct((T, F), BF16)],
        name=name, compiler_params=_cp(),
    )(h, w_il)


def _ffn_out_bwd_act(gx, w_out, a_il, name):
    T, D = gx.shape
    F = w_out.shape[0]
    tf = _ffn_tile(F)
    tm = _tile(T, 256, 16)

    def body(gx_ref, w_ref, a_ref, da_ref):
        ds = _nt(gx_ref[...].astype(BF16), w_ref[...])
        g = a_ref[:, :tf].astype(F32)
        u = a_ref[:, tf:].astype(F32)
        sg = jax.nn.sigmoid(g)
        da_ref[:, :tf] = (ds * u * (sg * (1.0 + g * (1.0 - sg)))).astype(BF16)
        da_ref[:, tf:] = (ds * (g * sg)).astype(BF16)

    return pl.pallas_call(
        body, grid=(F // tf, T // tm),
        in_specs=[pl.BlockSpec((tm, D), lambda j, i: (i, 0)), pl.BlockSpec((tf, D), lambda j, i: (j, 0)),
                  pl.BlockSpec((tm, 2 * tf), lambda j, i: (i, j))],
        out_specs=pl.BlockSpec((tm, 2 * tf), lambda j, i: (i, j)),
        out_shape=jax.ShapeDtypeStruct((T, 2 * F), BF16), name=name, compiler_params=_cp(),
    )(gx, w_out, a_il)


def _loss_head(x, w, tgt, name):
    T, D = x.shape
    tm = _tile(T, 512, SUBLANES)

    def body(x_ref, w_ref, t_ref, dx_ref, loss_ref, dw_ref):
        @pl.when(pl.program_id(0) == 0)
        def _():
            loss_ref[...] = jnp.zeros_like(loss_ref)
            dw_ref[...] = jnp.zeros_like(dw_ref)

        xf = x_ref[...]
        wv = w_ref[...]
        r = lax.rsqrt(jnp.mean(xf * xf, axis=-1, keepdims=True) + NORM_EPS)
        xhat = xf * r
        err = xhat * wv - t_ref[...]
        per_tok = jnp.mean(err * err, axis=-1, keepdims=True)
        loss_ref[...] += 0.5 * jnp.sum(per_tok, axis=0, keepdims=True)
        dy = err * (1.0 / D)
        dxhat = dy * wv
        dx_ref[...] = r * (dxhat - xhat * jnp.mean(dxhat * xhat, axis=-1, keepdims=True))
        dw_ref[...] += jnp.sum(dy * xhat, axis=0, keepdims=True)

    row = pl.BlockSpec((tm, D), lambda i: (i, 0))
    return pl.pallas_call(
        body, grid=(T // tm,),
        in_specs=[row, pl.BlockSpec((1, D), lambda i: (0, 0)), row],
        out_specs=[row, pl.BlockSpec((SUBLANES, LANES), lambda i: (0, 0)),
                   pl.BlockSpec((SUBLANES, D), lambda i: (0, 0))],
        out_shape=[jax.ShapeDtypeStruct((T, D), F32), jax.ShapeDtypeStruct((SUBLANES, LANES), F32),
                   jax.ShapeDtypeStruct((SUBLANES, D), F32)],
        name=name, compiler_params=_cp(),
    )(x, w.reshape(1, D), tgt)


def _split3(v):
    hi = v.astype(BF16)
    r1 = v - hi.astype(F32)
    mid = r1.astype(BF16)
    lo = (r1 - mid.astype(F32)).astype(BF16)
    return hi, mid, lo


def _tri_dot(tri, v):
    out = None
    for piece in _split3(v):
        t = jnp.dot(tri, piece, preferred_element_type=F32)
        out = t if out is None else out + t
    return out


def _q_block(T):
    return _tile(T, 256, LANES)


def _gate_fwd(f, b_f, P, name):
    T = f.shape[0]
    tb = _q_block(T)

    def body(f_ref, b_ref, ct_ref, cc_ref, c0_ref, carry):
        @pl.when(pl.program_id(0) == 0)
        def _():
            carry[...] = jnp.zeros_like(carry)

        z = f_ref[...] + b_ref[...]
        logf = jnp.minimum(z, 0.0) - jnp.log(1.0 + jnp.exp(-jnp.abs(z)))
        row = lax.broadcasted_iota(jnp.int32, (tb, tb), 0)
        col = lax.broadcasted_iota(jnp.int32, (tb, tb), 1)
        tri = (col <= row).astype(BF16)
        c = _tri_dot(tri, logf) + carry[0:1, :]
        carry[...] = jnp.broadcast_to(c[tb - 1:tb, :], carry.shape)
        first = jnp.broadcast_to(c[0:1, :], c.shape)
        for p in range(P):
            shifted = c if p == 0 else pltpu.roll(c, LANES - 2 * p, 1)
            cc_ref[p] = shifted
            ct_ref[p] = shifted.T[0:SUBLANES, :]
            c0_ref[p] = (first if p == 0 else pltpu.roll(first, LANES - 2 * p, 1)).T[0:SUBLANES, :]

    rows = pl.BlockSpec((P, SUBLANES, tb), lambda i: (0, 0, i))
    return pl.pallas_call(
        body, grid=(T // tb,),
        in_specs=[pl.BlockSpec((tb, LANES), lambda i: (i, 0)), pl.BlockSpec((1, LANES), lambda i: (0, 0))],
        out_specs=[rows, pl.BlockSpec((P, tb, LANES), lambda i: (0, i, 0)), rows],
        out_shape=[jax.ShapeDtypeStruct((P, SUBLANES, T), F32), jax.ShapeDtypeStruct((P, T, LANES), F32),
                   jax.ShapeDtypeStruct((P, SUBLANES, T), F32)],
        scratch_shapes=[pltpu.VMEM((SUBLANES, LANES), F32)],
        name=name, compiler_params=_cp(),
    )(f, b_f)


def _gate_bwd(dc_cols, drowT, f, b_f, P, name):
    T = f.shape[0]
    tb = _tile(T, 256, LANES)
    nb = T // tb

    def body(dc_ref, dr_ref, f_ref, b_ref, df_ref, db_ref, carry):
        @pl.when(pl.program_id(0) == 0)
        def _():
            carry[...] = jnp.zeros_like(carry)
            db_ref[...] = jnp.zeros_like(db_ref)

        lane = lax.broadcasted_iota(jnp.int32, (tb, LANES), 1)
        dc = jnp.zeros((tb, LANES), F32)
        for p in range(P):
            rows = jnp.concatenate([dr_ref[p], jnp.zeros((LANES - SUBLANES, tb), F32)], axis=0)
            part = jnp.where(lane < 2, dc_ref[p] + rows.T, 0.0)
            dc = dc + (part if p == 0 else pltpu.roll(part, 2 * p, 1))
        row = lax.broadcasted_iota(jnp.int32, (tb, tb), 0)
        col = lax.broadcasted_iota(jnp.int32, (tb, tb), 1)
        tri = (col >= row).astype(BF16)
        dlogf = _tri_dot(tri, dc) + carry[0:1, :]
        carry[...] = jnp.broadcast_to(dlogf[0:1, :], carry.shape)
        z = f_ref[...] + b_ref[...]
        df = jnp.where(lane < 2 * P, dlogf * jax.nn.sigmoid(-z), 0.0)
        df_ref[...] = df
        db_ref[...] += jnp.sum(df, axis=0, keepdims=True)

    return pl.pallas_call(
        body, grid=(nb,),
        in_specs=[pl.BlockSpec((P, tb, LANES), lambda i: (0, nb - 1 - i, 0)),
                  pl.BlockSpec((P, SUBLANES, tb), lambda i: (0, 0, nb - 1 - i)),
                  pl.BlockSpec((tb, LANES), lambda i: (nb - 1 - i, 0)),
                  pl.BlockSpec((1, LANES), lambda i: (0, 0))],
        out_specs=[pl.BlockSpec((tb, LANES), lambda i: (nb - 1 - i, 0)),
                   pl.BlockSpec((SUBLANES, LANES), lambda i: (0, 0))],
        out_shape=[jax.ShapeDtypeStruct((T, LANES), F32), jax.ShapeDtypeStruct((SUBLANES, LANES), F32)],
        scratch_shapes=[pltpu.VMEM((SUBLANES, LANES), F32)],
        name=name, compiler_params=_cp(),
    )(dc_cols, drowT, f, b_f)


def _nt(a, b):
    return lax.dot_general(a, b, (((1,), (1,)), ((), ())), preferred_element_type=F32)


def _attn_fwd(qkv, cT, P, scale, name):
    T = qkv.shape[0]
    tq = _q_block(T)
    tw = _tile(T, 4 * tq, 2 * tq)
    cw = tw // 2
    assert cw % tq == 0, "the sequence must split into chunks of whole query blocks"
    nq = T // tq

    def body(q_ref, k_ref, v_ref, c_ref, o_ref, lse_ref, s_scr):
        i = pl.program_id(1)
        lane = lax.broadcasted_iota(jnp.int32, (1, LANES), 1)
        q = (q_ref[...].astype(F32) * scale).astype(BF16)
        q_heads = (jnp.where(lane < 64, q, jnp.zeros_like(q)), jnp.where(lane >= 64, q, jnp.zeros_like(q)))
        c0 = c_ref[0, :, pl.ds(pl.multiple_of(i * tq, tq), LANES)][:, 0:1]

        def scores(start, width, a):
            bias = c0 - c_ref[0, :, pl.ds(start, width)]
            return _nt(q_heads[a], k_ref[pl.ds(start, width), :]) + bias[a:a + 1, :]

        def softmax_pv(start, width, s_of, carry):
            v = v_ref[pl.ds(start, width), :]
            one = jnp.ones_like(v)
            v_heads = (jnp.where(lane < 64, v, one), jnp.where(lane >= 64, v, one))
            new = []
            for a in range(2):
                m, acc = carry[a]
                s = s_of(a)
                m_new = jnp.maximum(m, jnp.max(s, axis=1, keepdims=True))
                p = jnp.exp(s - m_new)
                acc = jnp.exp(m - m_new) * acc + jnp.dot(p.astype(BF16), v_heads[a], preferred_element_type=F32)
                new.append((m_new, acc))
            return tuple(new)

        def fill(start, buf):
            for a in range(2):
                s_scr[2 * buf + a] = scores(start, cw, a)

        def wide(j, carry):
            base = pl.multiple_of(j * tw, tw)
            fill(base + cw, 1)
            carry = softmax_pv(base, cw, lambda a: s_scr[a], carry)
            fill(base + tw, 0)
            return softmax_pv(base + cw, cw, lambda a: s_scr[2 + a], carry)

        init = tuple((jnp.full((tq, 1), NEG_BIG, F32), jnp.zeros((tq, LANES), F32)) for _ in range(2))
        n_wide = (i * tq) // tw
        fill(0, 0)
        carry = lax.fori_loop(0, n_wide, wide, init)

        base = pl.multiple_of(n_wide * tw, tw)
        ahead = i * tq - base
        col_minus_row = (lax.broadcasted_iota(jnp.int32, (tq, cw), 1)
                         - lax.broadcasted_iota(jnp.int32, (tq, cw), 0))

        def causal(buf, first_key):
            return lambda a: jnp.where(col_minus_row <= ahead - first_key, s_scr[2 * buf + a], NEG_BIG)

        def one_chunk(cr):
            return softmax_pv(base, cw, causal(0, 0), cr)

        def two_chunks(cr):
            fill(base + cw, 1)
            cr = softmax_pv(base, cw, causal(0, 0), cr)
            return softmax_pv(base + cw, cw, causal(1, cw), cr)

        (m0, a0), (m1, a1) = lax.cond(ahead >= cw, two_chunks, one_chunk, carry)
        sums = jnp.where(lane < 64, pltpu.roll(a0, 64, 1), pltpu.roll(a1, 64, 1))
        o_ref[...] = (jnp.where(lane < 64, a0, a1) / sums).astype(BF16)
        l0, l1 = a0[:, 64:65], a1[:, 0:1]
        lse = jnp.where(lane == 0, m0 + jnp.log(l0), jnp.where(lane == 1, m1 + jnp.log(l1), 0.0))
        lse_ref[0] = lse.T[0:SUBLANES, :]

    return pl.pallas_call(
        body, grid=(P, nq),
        in_specs=[pl.BlockSpec((tq, LANES), lambda p, i: (i, p)),
                  pl.BlockSpec((T, LANES), lambda p, i: (0, P + p)),
                  pl.BlockSpec((T, LANES), lambda p, i: (0, 2 * P + p)),
                  pl.BlockSpec((1, SUBLANES, T), lambda p, i: (p, 0, 0))],
        out_specs=[pl.BlockSpec((tq, LANES), lambda p, i: (i, p)),
                   pl.BlockSpec((1, SUBLANES, tq), lambda p, i: (p, 0, i))],
        out_shape=[jax.ShapeDtypeStruct((T, LANES * P), BF16), jax.ShapeDtypeStruct((P, SUBLANES, T), F32)],
        scratch_shapes=[pltpu.VMEM((4, tq, cw), F32)],
        name=name, compiler_params=_cp(),
    )(qkv, qkv, qkv, cT)


def _attn_delta(do, o, P, name):
    T, D = o.shape
    tb = _tile(T, 256, LANES)

    def body(do_ref, o_ref, d_ref):
        lane = lax.broadcasted_iota(jnp.int32, (1, LANES), 1)
        for p in range(P):
            cols = slice(p * LANES, (p + 1) * LANES)
            prod = do_ref[:, cols].astype(F32) * o_ref[:, cols].astype(F32)
            d0 = jnp.sum(jnp.where(lane < 64, prod, 0.0), axis=1, keepdims=True)
            d1 = jnp.sum(jnp.where(lane >= 64, prod, 0.0), axis=1, keepdims=True)
            both = jnp.where(lane == 0, d0, jnp.where(lane == 1, d1, 0.0))
            d_ref[p] = both.T[0:SUBLANES, :]

    return pl.pallas_call(
        body, grid=(T // tb,),
        in_specs=[pl.BlockSpec((tb, D), lambda i: (i, 0)), pl.BlockSpec((tb, D), lambda i: (i, 0))],
        out_specs=pl.BlockSpec((P, SUBLANES, tb), lambda i: (0, 0, i)),
        out_shape=jax.ShapeDtypeStruct((P, SUBLANES, T), F32), name=name, compiler_params=_cp(),
    )(do, o)


def _attn_bwd(qkv, do, lseT, dT, c0T, c_cols, P, scale, name):
    T = qkv.shape[0]
    tq = _q_block(T)
    tw = _tile(T, 4 * tq, tq)
    nq = T // tq
    r = tw // tq

    def body(q_ref, do_ref, k_ref, v_ref, lse_ref, d_ref, c0_ref, cc_ref,
             dq_ref, dk_ref, dv_ref, dc_ref, drow_ref, dq_acc0, dq_acc1):
        j = pl.program_id(1)

        @pl.when(j == 0)
        def _():
            dq_acc0[...] = jnp.zeros_like(dq_acc0)
            dq_acc1[...] = jnp.zeros_like(dq_acc1)

        lane = lax.broadcasted_iota(jnp.int32, (1, LANES), 1)
        in_head = (lane < 64, lane >= 64)
        k = k_ref[...]
        v = v_ref[...]
        zero = jnp.zeros_like(k)
        one = jnp.ones_like(k)
        k_heads = tuple(jnp.where(h, k, zero) for h in in_head)
        v_heads = tuple(jnp.where(h, v, zero) for h in in_head)
        k_ones = tuple(jnp.where(h, k, one) for h in in_head)
        cc = cc_ref[0]
        c_first = (cc[0:1, 0:1], cc[0:1, 1:2])
        c_rel = (cc[:, 0:1] - c_first[0], cc[:, 1:2] - c_first[1])
        dq_accs = (dq_acc0, dq_acc1)

        def block(start, width, carry, masked):
            q = (q_ref[pl.ds(start, width), :].astype(F32) * scale).astype(BF16)
            q_one = jnp.ones_like(q)
            dov = do_ref[pl.ds(start, width), :]
            lse = lse_ref[0, :, pl.ds(start, width)]
            dlt = d_ref[0, :, pl.ds(start, width)]
            c0 = c0_ref[0, :, pl.ds(start, width)]
            new = []
            for a in range(2):
                dk_a, dv_a = carry[a]
                rowv = lse[a:a + 1, :] + (c_first[a] - c0[a:a + 1, :])
                st = _nt(k_heads[a], q)
                pt = jnp.exp((st - c_rel[a]) - rowv)
                if masked:
                    row = lax.broadcasted_iota(jnp.int32, (tq, width), 0)
                    col = lax.broadcasted_iota(jnp.int32, (tq, width), 1)
                    pt = jnp.where(col >= row, pt, 0.0)
                dpt = _nt(v_heads[a], dov)
                dst_b = (pt * (dpt - dlt[a:a + 1, :])).astype(BF16)
                dv_a = dv_a + jnp.dot(pt.astype(BF16), dov, preferred_element_type=F32)
                dk_a = dk_a + jnp.dot(dst_b, jnp.where(in_head[a], q, q_one), preferred_element_type=F32)
                dq_accs[a][pl.ds(start, width), :] += lax.dot_general(
                    dst_b, k_ones[a], (((0,), (0,)), ((), ())), preferred_element_type=F32)
                new.append((dk_a, dv_a))
            return tuple(new)

        init = tuple((jnp.zeros((tq, LANES), F32), jnp.zeros((tq, LANES), F32)) for _ in range(2))
        carry = block(pl.multiple_of(j * tq, tq), tq, init, True)
        first_wide = (j + r) // r
        carry = lax.fori_loop(
            j + 1, jnp.minimum(first_wide * r, nq),
            lambda i, cr: block(pl.multiple_of(i * tq, tq), tq, cr, False), carry)
        (dk0, dv0), (dk1, dv1) = lax.fori_loop(
            first_wide, nq // r, lambda i, cr: block(pl.multiple_of(i * tw, tw), tw, cr, False), carry)
        dk_ref[...] = jnp.where(lane < 64, dk0, dk1).astype(BF16)
        dv_ref[...] = jnp.where(lane < 64, dv0, dv1).astype(BF16)
        dc_ref[0] = jnp.where(lane == 0, -dk0[:, 64:65], jnp.where(lane == 1, -dk1[:, 0:1], 0.0))

        @pl.when(j == nq - 1)
        def _():
            def finish(i, _):
                rows = pl.ds(pl.multiple_of(i * tq, tq), tq)
                a0 = dq_acc0[rows, :]
                a1 = dq_acc1[rows, :]
                dq_ref[rows, :] = (jnp.where(lane < 64, a0, a1) * scale).astype(BF16)
                sums = jnp.where(lane == 0, a0[:, 64:65], jnp.where(lane == 1, a1[:, 0:1], 0.0))
                drow_ref[0, :, rows] = sums.T[0:SUBLANES, :]
                return 0

            lax.fori_loop(0, nq, finish, 0)

    full = lambda col: pl.BlockSpec((T, LANES), lambda p, j: (0, col(p)))
    blk = lambda col: pl.BlockSpec((tq, LANES), lambda p, j: (j, col(p)))
    rows = pl.BlockSpec((1, SUBLANES, T), lambda p, j: (p, 0, 0))
    cols = pl.BlockSpec((1, tq, LANES), lambda p, j: (p, j, 0))
    D = LANES * P
    return pl.pallas_call(
        body, grid=(P, nq),
        in_specs=[full(lambda p: p), full(lambda p: p), blk(lambda p: P + p), blk(lambda p: 2 * P + p),
                  rows, rows, rows, cols],
        out_specs=[full(lambda p: p), blk(lambda p: p), blk(lambda p: p), cols, rows],
        out_shape=[jax.ShapeDtypeStruct((T, D), BF16), jax.ShapeDtypeStruct((T, D), BF16),
                   jax.ShapeDtypeStruct((T, D), BF16), jax.ShapeDtypeStruct((P, T, LANES), F32),
                   jax.ShapeDtypeStruct((P, SUBLANES, T), F32)],
        scratch_shapes=[pltpu.VMEM((T, LANES), F32), pltpu.VMEM((T, LANES), F32)],
        name=name, compiler_params=_cp(),
    )(qkv, do, qkv, qkv, lseT, dT, c0T, c_cols)


_SQRT_HALF = 0.7071067811865476
_INV_SQRT_2PI = 0.3989422804014327


def _gelu(v):
    return 0.5 * v * (1.0 + lax.erf(v * _SQRT_HALF))


def _gelu_grad(v):
    return 0.5 * (1.0 + lax.erf(v * _SQRT_HALF)) + v * (_INV_SQRT_2PI * jnp.exp(-0.5 * v * v))


def _sgu_fwd(a, ln_g, ln_b, w_tril, bias, name):
    T, W2 = a.shape
    W = W2 // 2
    G = w_tril.shape[0]
    tb = _tile(T, 256, LANES)

    def body(a_ref, g_ref, b_ref, w_ref, bias_ref, out_ref):
        zu = _gelu(a_ref[:, :W].astype(F32))
        zv = _gelu(a_ref[:, W:].astype(F32))
        mu = jnp.mean(zv, axis=-1, keepdims=True)
        d = zv - mu
        rstd = lax.rsqrt(jnp.mean(d * d, axis=-1, keepdims=True) + LN_EPS)
        vn = (d * rstd * g_ref[...] + b_ref[...]).astype(BF16)
        for c in range(tb // LANES):
            rs = slice(c * LANES, (c + 1) * LANES)
            for g in range(G):
                cs = slice(g * LANES, (g + 1) * LANES)
                mixed = jnp.dot(w_ref[g], vn[rs, cs], preferred_element_type=F32) + bias_ref[:, cs]
                out_ref[rs, cs] = (zu[rs, cs] * mixed).astype(BF16)

    return pl.pallas_call(
        body, grid=(T // tb,),
        in_specs=[pl.BlockSpec((tb, W2), lambda i: (i, 0)), pl.BlockSpec((1, W), lambda i: (0, 0)),
                  pl.BlockSpec((1, W), lambda i: (0, 0)), pl.BlockSpec((G, LANES, LANES), lambda i: (0, 0, 0)),
                  pl.BlockSpec((LANES, W), lambda i: (0, 0))],
        out_specs=pl.BlockSpec((tb, W), lambda i: (i, 0)),
        out_shape=jax.ShapeDtypeStruct((T, W), BF16), name=name, compiler_params=_cp(),
    )(a, ln_g.reshape(1, W), ln_b.reshape(1, W), w_tril, bias)


def _sgu_bwd(a, dgated, ln_g, ln_b, w_tril, w_tril_t, bias, name):
    T, W2 = a.shape
    W = W2 // 2
    G = w_tril.shape[0]
    tb = _tile(T, 256, LANES)

    def body(a_ref, dg_ref, g_ref, b_ref, w_ref, wt_ref, bias_ref,
             da_ref, dws_ref, dbias_ref, dlng_ref, dlnb_ref, dvn_ref):
        @pl.when(pl.program_id(0) == 0)
        def _():
            dws_ref[...] = jnp.zeros_like(dws_ref)
            dbias_ref[...] = jnp.zeros_like(dbias_ref)
            dlng_ref[...] = jnp.zeros_like(dlng_ref)
            dlnb_ref[...] = jnp.zeros_like(dlnb_ref)

        up = a_ref[:, :W].astype(F32)
        vp = a_ref[:, W:].astype(F32)
        zu = _gelu(up)
        zv = _gelu(vp)
        mu = jnp.mean(zv, axis=-1, keepdims=True)
        d = zv - mu
        rstd = lax.rsqrt(jnp.mean(d * d, axis=-1, keepdims=True) + LN_EPS)
        vhat = d * rstd
        gam = g_ref[...]
        vn = (vhat * gam + b_ref[...]).astype(BF16)
        dgated = dg_ref[...]
        for c in range(tb // LANES):
            rs = slice(c * LANES, (c + 1) * LANES)
            for g in range(G):
                cs = slice(g * LANES, (g + 1) * LANES)
                vb = vn[rs, cs]
                mixed = jnp.dot(w_ref[g], vb, preferred_element_type=F32) + bias_ref[:, cs]
                dgt = dgated[rs, cs]
                da_ref[rs, cs] = (dgt * mixed * _gelu_grad(up[rs, cs])).astype(BF16)
                dmx = dgt * zu[rs, cs]
                dbias_ref[:, cs] += dmx
                dmb = dmx.astype(BF16)
                dws_ref[g] += _nt(dmb, vb)
                dvn_ref[rs, cs] = jnp.dot(wt_ref[g], dmb, preferred_element_type=F32)
        dvn = dvn_ref[...]
        dlng_ref[...] += jnp.sum(dvn * vhat, axis=0, keepdims=True)
        dlnb_ref[...] += jnp.sum(dvn, axis=0, keepdims=True)
        dvh = dvn * gam
        dzv = rstd * (dvh - jnp.mean(dvh, axis=-1, keepdims=True)
                      - vhat * jnp.mean(dvh * vhat, axis=-1, keepdims=True))
        da_ref[:, W:] = (dzv * _gelu_grad(vp)).astype(BF16)

    const2 = lambda shape: pl.BlockSpec(shape, lambda i: (0, 0))
    const3 = pl.BlockSpec((G, LANES, LANES), lambda i: (0, 0, 0))
    return pl.pallas_call(
        body, grid=(T // tb,),
        in_specs=[pl.BlockSpec((tb, W2), lambda i: (i, 0)), pl.BlockSpec((tb, W), lambda i: (i, 0)),
                  const2((1, W)), const2((1, W)), const3, const3, const2((LANES, W))],
        out_specs=[pl.BlockSpec((tb, W2), lambda i: (i, 0)), const3, const2((LANES, W)),
                   const2((SUBLANES, W)), const2((SUBLANES, W))],
        out_shape=[jax.ShapeDtypeStruct((T, W2), BF16), jax.ShapeDtypeStruct((G, LANES, LANES), F32),
                   jax.ShapeDtypeStruct((LANES, W), F32), jax.ShapeDtypeStruct((SUBLANES, W), F32),
                   jax.ShapeDtypeStruct((SUBLANES, W), F32)],
        scratch_shapes=[pltpu.VMEM((tb, W), F32)],
        name=name, compiler_params=_cp(),
    )(a, dgated, ln_g.reshape(1, W), ln_b.reshape(1, W), w_tril, w_tril_t, bias)


def _adam_math(w, g, m, v):
    m = ADAM_B1 * m + (1.0 - ADAM_B1) * g
    v = ADAM_B2 * v + (1.0 - ADAM_B2) * (g * g)
    m_hat = m / (1.0 - ADAM_B1 ** ADAM_STEP)
    v_hat = v / (1.0 - ADAM_B2 ** ADAM_STEP)
    delta = -ADAM_LR * (m_hat / (jnp.sqrt(v_hat) + ADAM_EPS) + ADAM_WD * w)
    return delta, m, v


def _adamw_halves(mine, theirs, c_idx, w, m, v, name):
    R, C = w.shape
    rh = R // 2
    tb = _tile(rh, 512, SUBLANES)
    nb = rh // tb

    def body(c_ref, a_ref, b_ref, w_ref, m_ref, v_ref, g_ref, d_ref, mo_ref, vo_ref):
        g = jnp.where(pl.program_id(0) == c_ref[0], a_ref[...], b_ref[...])
        d, mm, vv = _adam_math(w_ref[...], g, m_ref[...], v_ref[...])
        g_ref[...] = g
        d_ref[...] = d
        mo_ref[...] = mm
        vo_ref[...] = vv

    half = pl.BlockSpec((tb, C), lambda h, i, c: (i, 0))
    row = pl.BlockSpec((tb, C), lambda h, i, c: (h * nb + i, 0))
    sds = jax.ShapeDtypeStruct((R, C), F32)
    return pl.pallas_call(
        body,
        grid_spec=pltpu.PrefetchScalarGridSpec(
            num_scalar_prefetch=1, grid=(2, nb), in_specs=[half, half, row, row, row], out_specs=[row] * 4),
        out_shape=[sds] * 4, name=name, compiler_params=_cp())(c_idx, mine, theirs, w, m, v)


def _adamw_sum(parts, w, m, v, name):
    K, R, C = parts.shape
    tb = _tile(R, 128, SUBLANES)

    def body(p_ref, w_ref, m_ref, v_ref, g_ref, d_ref, mo_ref, vo_ref):
        g = p_ref[0]
        for k in range(1, K):
            g = g + p_ref[k]
        d, mm, vv = _adam_math(w_ref[...], g, m_ref[...], v_ref[...])
        g_ref[...] = g
        d_ref[...] = d
        mo_ref[...] = mm
        vo_ref[...] = vv

    row = pl.BlockSpec((tb, C), lambda i: (i, 0))
    sds = jax.ShapeDtypeStruct((R, C), F32)
    return pl.pallas_call(
        body, grid=(R // tb,),
        in_specs=[pl.BlockSpec((K, tb, C), lambda i: (0, i, 0)), row, row, row],
        out_specs=[row] * 4, out_shape=[sds] * 4, name=name, compiler_params=_cp())(parts, w, m, v)


def _pair_sum(g_all, recv, c_idx, name):
    K, R, C = g_all.shape
    rh = R // 2
    tb = _tile(rh, 512, 16)
    nb = rh // tb

    def body(c_ref, a_ref, b_ref, o_ref):
        o_ref[...] = (a_ref[...] + b_ref[...]).astype(BF16)

    return pl.pallas_call(
        body,
        grid_spec=pltpu.PrefetchScalarGridSpec(
            num_scalar_prefetch=1, grid=(K, nb),
            in_specs=[pl.BlockSpec((1, tb, C), lambda k, i, c: (k, c[0] * nb + i, 0)),
                      pl.BlockSpec((1, tb, C), lambda k, i, c: (k, i, 0))],
            out_specs=pl.BlockSpec((1, tb, C), lambda k, i, c: (k, i, 0))),
        out_shape=jax.ShapeDtypeStruct((K, rh, C), BF16), name=name, compiler_params=_cp(),
    )(c_idx, g_all, recv)


def _sum_parts(parts, name):
    K, R, C = parts.shape
    tb = _tile(R, 512, 16)

    def body(p_ref, o_ref):
        g = p_ref[0].astype(F32)
        for k in range(1, K):
            g = g + p_ref[k].astype(F32)
        o_ref[...] = g

    return pl.pallas_call(
        body, grid=(R // tb,), in_specs=[pl.BlockSpec((K, tb, C), lambda i: (0, i, 0))],
        out_specs=pl.BlockSpec((tb, C), lambda i: (i, 0)),
        out_shape=jax.ShapeDtypeStruct((R, C), F32), name=name, compiler_params=_cp())(parts)


_CHIP_RELATIONS = ((1, 0), (0, 1), (1, 1))


def _position():
    return lax.axis_index("x"), lax.axis_index("y"), lax.axis_index("c")


def _flip(v, bit):
    return 1 - v if bit else v


def _gather_weights(w_pack, ln_pack, name):
    R, C = w_pack.shape

    def body(w_ref, ln_ref, ow_ref, oln_ref, local_sems, send_sems, recv_sems):
        x, y, c = _position()
        me = 2 * x + y
        own_w = pltpu.make_async_copy(w_ref, ow_ref.at[me], local_sems.at[0])
        own_ln = pltpu.make_async_copy(ln_ref, oln_ref.at[me], local_sems.at[1])
        own_w.start()
        own_ln.start()

        def copies(r, slot):
            dx, dy = _CHIP_RELATIONS[r]
            peer = (_flip(x, dx), _flip(y, dy), c)
            cw = pltpu.make_async_remote_copy(
                src_ref=w_ref, dst_ref=ow_ref.at[slot], send_sem=send_sems.at[2 * r],
                recv_sem=recv_sems.at[2 * r], device_id=peer, device_id_type=MESH)
            cl = pltpu.make_async_remote_copy(
                src_ref=ln_ref, dst_ref=oln_ref.at[slot], send_sem=send_sems.at[2 * r + 1],
                recv_sem=recv_sems.at[2 * r + 1], device_id=peer, device_id_type=MESH)
            return cw, cl

        sent = [copies(r, me) for r in range(3)]
        for cw, cl in sent:
            cw.start()
            cl.start()
        for r in range(3):
            dx, dy = _CHIP_RELATIONS[r]
            cw, cl = copies(r, 2 * _flip(x, dx) + _flip(y, dy))
            cw.wait_recv()
            cl.wait_recv()
        for cw, cl in sent:
            cw.wait_send()
            cl.wait_send()
        own_w.wait()
        own_ln.wait()

    return pl.pallas_call(
        body, in_specs=[_hbm(), _hbm()], out_specs=[_hbm(), _hbm()],
        out_shape=[jax.ShapeDtypeStruct((4, R, C), w_pack.dtype), jax.ShapeDtypeStruct((4, SUBLANES, C), F32)],
        scratch_shapes=[pltpu.SemaphoreType.DMA((2,)), pltpu.SemaphoreType.DMA((6,)),
                        pltpu.SemaphoreType.DMA((6,))],
        name=name, compiler_params=_cp(),
    )(w_pack, ln_pack)


def _sibling_halves(g_all, name):
    K, R, C = g_all.shape
    rh = R // 2

    def body(g_ref, o_ref, send_sem, recv_sem):
        x, y, c = _position()
        start = pl.multiple_of((1 - c) * rh, SUBLANES)
        cp = pltpu.make_async_remote_copy(
            src_ref=g_ref.at[:, pl.ds(start, rh), :], dst_ref=o_ref, send_sem=send_sem, recv_sem=recv_sem,
            device_id=(x, y, 1 - c), device_id_type=MESH)
        cp.start()
        cp.wait_recv()
        cp.wait_send()

    return pl.pallas_call(
        body, in_specs=[_hbm()], out_specs=_hbm(),
        out_shape=jax.ShapeDtypeStruct((K, rh, C), F32),
        scratch_shapes=[pltpu.SemaphoreType.DMA(()), pltpu.SemaphoreType.DMA(())],
        name=name, compiler_params=_cp(),
    )(g_all)


def _chip_exchange(parts, name):
    K, R, C = parts.shape

    def body(p_ref, o_ref, local_sem, send_sems, recv_sems):
        x, y, c = _position()
        me = 2 * x + y
        own = pltpu.make_async_copy(p_ref.at[me], o_ref.at[me], local_sem)
        own.start()

        def copy(r, src_slot, dst_slot):
            dx, dy = _CHIP_RELATIONS[r]
            return pltpu.make_async_remote_copy(
                src_ref=p_ref.at[src_slot], dst_ref=o_ref.at[dst_slot], send_sem=send_sems.at[r],
                recv_sem=recv_sems.at[r], device_id=(_flip(x, dx), _flip(y, dy), c), device_id_type=MESH)

        def chip(r):
            dx, dy = _CHIP_RELATIONS[r]
            return 2 * _flip(x, dx) + _flip(y, dy)

        sent = [copy(r, chip(r), me) for r in range(3)]
        for cp in sent:
            cp.start()
        for r in range(3):
            copy(r, me, chip(r)).wait_recv()
        for cp in sent:
            cp.wait_send()
        own.wait()

    return pl.pallas_call(
        body, in_specs=[_hbm()], out_specs=_hbm(),
        out_shape=jax.ShapeDtypeStruct((K, R, C), parts.dtype),
        scratch_shapes=[pltpu.SemaphoreType.DMA(()), pltpu.SemaphoreType.DMA((3,)),
                        pltpu.SemaphoreType.DMA((3,))],
        name=name, compiler_params=_cp(),
    )(parts)


def _swap_with_sibling(half, name):
    rh, C = half.shape

    def body(h_ref, o_ref, send_sem, recv_sem):
        x, y, c = _position()
        cp = pltpu.make_async_remote_copy(
            src_ref=h_ref, dst_ref=o_ref, send_sem=send_sem, recv_sem=recv_sem,
            device_id=(x, y, 1 - c), device_id_type=MESH)
        cp.start()
        cp.wait_recv()
        cp.wait_send()

    return pl.pallas_call(
        body, in_specs=[_hbm()], out_specs=_hbm(),
        out_shape=jax.ShapeDtypeStruct((rh, C), F32),
        scratch_shapes=[pltpu.SemaphoreType.DMA(()), pltpu.SemaphoreType.DMA(())],
        name=name, compiler_params=_cp(),
    )(half)


def _gather_all(part, name):
    R, C = part.shape
    masks = [(b >> 2 & 1, b >> 1 & 1, b & 1) for b in range(1, 8)]

    def body(p_ref, o_ref, local_sem, send_sems, recv_sems):
        x, y, c = _position()
        me = 4 * x + 2 * y + c
        own = pltpu.make_async_copy(p_ref, o_ref.at[me], local_sem)
        own.start()

        def copy(r, slot):
            dx, dy, dc = masks[r]
            return pltpu.make_async_remote_copy(
                src_ref=p_ref, dst_ref=o_ref.at[slot], send_sem=send_sems.at[r], recv_sem=recv_sems.at[r],
                device_id=(_flip(x, dx), _flip(y, dy), _flip(c, dc)), device_id_type=MESH)

        sent = [copy(r, me) for r in range(7)]
        for cp in sent:
            cp.start()
        for r in range(7):
            dx, dy, dc = masks[r]
            copy(r, 4 * _flip(x, dx) + 2 * _flip(y, dy) + _flip(c, dc)).wait_recv()
        for cp in sent:
            cp.wait_send()
        own.wait()

    return pl.pallas_call(
        body, in_specs=[_hbm()], out_specs=_hbm(),
        out_shape=jax.ShapeDtypeStruct((8, R, C), F32),
        scratch_shapes=[pltpu.SemaphoreType.DMA(()), pltpu.SemaphoreType.DMA((7,)),
                        pltpu.SemaphoreType.DMA((7,))],
        name=name, compiler_params=_cp(),
    )(part)


def _pack(arrs, row_mult):
    n = sum(math.prod(a.shape) for a in arrs)
    rows = -(-n // PACK_COLS)
    rows = -(-rows // row_mult) * row_mult
    pieces = [a.reshape(-1).astype(F32) for a in arrs]
    if rows * PACK_COLS > n:
        pieces.append(jnp.zeros((rows * PACK_COLS - n,), F32))
    return jnp.concatenate(pieces).reshape(rows, PACK_COLS)


def _unpack(buf, shapes):
    lead = buf.shape[:-2]
    flat = buf.reshape(lead + (-1,))
    out, off = [], 0
    for shp in shapes:
        n = math.prod(shp)
        out.append(flat[..., off:off + n].reshape(lead + tuple(shp)))
        off += n
    return out


def _cols_from_chips(g):
    k, L, A, n = g.shape
    return jnp.transpose(g, (1, 2, 0, 3)).reshape(L, A, k * n)


def _rows_from_chips(g):
    k, L, n, B = g.shape
    return jnp.transpose(g, (1, 0, 2, 3)).reshape(L, k * n, B)


def _cols_to_chips(full, k=4):
    L, A, N = full.shape
    return jnp.transpose(full.reshape(L, A, k, N // k), (2, 0, 1, 3))


def _rows_to_chips(full, k=4):
    L, N, B = full.shape
    return jnp.transpose(full.reshape(L, k, N // k, B), (1, 0, 2, 3))


def kernel(x, mixer_norm_w, attn_w_in, attn_b_f, attn_w_out, sgu_w_in, sgu_ln_g, sgu_ln_b, sgu_w_s, sgu_b_s, sgu_w_out, ffn_norm_w, ffn_w_in, ffn_w_out, final_norm_w, loss_target, m_mixer_norm_w, m_attn_w_in, m_attn_b_f, m_attn_w_out, m_sgu_w_in, m_sgu_ln_g, m_sgu_ln_b, m_sgu_w_s, m_sgu_b_s, m_sgu_w_out, m_ffn_norm_w, m_ffn_w_in, m_ffn_w_out, m_final_norm_w, v_mixer_norm_w, v_attn_w_in, v_attn_b_f, v_attn_w_out, v_sgu_w_in, v_sgu_ln_g, v_sgu_ln_b, v_sgu_w_s, v_sgu_b_s, v_sgu_w_out, v_ffn_norm_w, v_ffn_w_in, v_ffn_w_out, v_final_norm_w):
    T, D = x.shape[1], x.shape[2]
    depth = mixer_norm_w.shape[0]
    H = attn_b_f.shape[1]
    P = D // LANES
    assert D % LANES == 0 and D // H == 64 and 2 * P == H and 2 * P <= LANES
    G = sgu_w_s.shape[1]
    W = sgu_w_out.shape[1] * 4
    assert sgu_w_s.shape[2] == LANES and W == G * LANES
    scale = float(D // H) ** -0.5
    f_pad = LANES
    c_idx = lax.axis_index("c").astype(jnp.int32).reshape(1)

    sharded = [attn_w_in, attn_w_out, sgu_w_in, sgu_w_out, ffn_w_in, ffn_w_out, sgu_ln_g, sgu_ln_b]
    sharded_m = [m_attn_w_in, m_attn_w_out, m_sgu_w_in, m_sgu_w_out, m_ffn_w_in, m_ffn_w_out, m_sgu_ln_g, m_sgu_ln_b]
    sharded_v = [v_attn_w_in, v_attn_w_out, v_sgu_w_in, v_sgu_w_out, v_ffn_w_in, v_ffn_w_out, v_sgu_ln_g, v_sgu_ln_b]
    shard_shapes = [a.shape for a in sharded]
    w_pack = _pack(sharded, 512)
    ln_pack = _pack([sgu_ln_g, sgu_ln_b], SUBLANES)
    gat_w, gat_ln = _gather_weights(w_pack.astype(BF16), ln_pack, "gather_weights")
    g_ai, g_ao, g_si, g_so, g_fi, g_fo, _, _ = _unpack(gat_w, shard_shapes)
    g_lng, g_lnb = _unpack(gat_ln, [sgu_ln_g.shape, sgu_ln_b.shape])
    w_ai = _cols_from_chips(g_ai)
    w_ai = jnp.pad(w_ai, ((0, 0), (0, 0), (0, 3 * D + f_pad - w_ai.shape[2])))
    w_ao = _rows_from_chips(g_ao)
    w_si = _cols_from_chips(g_si)
    w_so = _rows_from_chips(g_so)
    w_fo = _rows_from_chips(g_fo)
    ffn_tf = _ffn_tile(w_fo.shape[1])
    w_fi_il = _interleave_gu(_cols_from_chips(g_fi), ffn_tf)
    ln_g = jnp.transpose(g_lng, (1, 0, 2)).reshape(sgu_ln_g.shape[0], W)
    ln_b = jnp.transpose(g_lnb, (1, 0, 2)).reshape(sgu_ln_b.shape[0], W)
    w_tril = jnp.tril(sgu_w_s)
    w_tril_b = w_tril.astype(BF16)
    w_tril_tb = jnp.swapaxes(w_tril, 2, 3).astype(BF16)
    sgu_bias = jnp.repeat(jnp.swapaxes(sgu_b_s, 1, 2), LANES, axis=2)
    b_f_pad = jnp.pad(attn_b_f, ((0, 0), (0, LANES - H)))

    xs = x.reshape(T, D)
    saved = []
    for i in range(depth):
        j = i // 2
        h = _rmsnorm_fwd(xs, mixer_norm_w[i], f"mix_norm_{i}")
        rec = {"x_in": xs, "h": h}
        if i % 2 == 0:
            qkv = _mm(h, w_ai[j, :, :3 * D], "nn", BF16, f"attn_qkv_{i}")
            f = _mm(h, w_ai[j, :, 3 * D:], "nn", F32, f"attn_gate_{i}")
            cT, c_cols, c0T = _gate_fwd(f, b_f_pad[j:j + 1], P, f"gate_fwd_{i}")
            o, lseT = _attn_fwd(qkv, cT, P, scale, f"attn_fwd_{i}")
            x_mid = _mm(o, w_ao[j], "nn", F32, f"attn_out_{i}", res=xs)
            rec.update(qkv=qkv, f=f, c0T=c0T, c_cols=c_cols, o=o, lseT=lseT)
        else:
            a = _mm(h, w_si[j], "nn", BF16, f"sgu_in_{i}")
            gated = _sgu_fwd(a, ln_g[j], ln_b[j], w_tril_b[j], sgu_bias[j], f"sgu_fwd_{i}")
            x_mid = _mm(gated, w_so[j], "nn", F32, f"sgu_out_{i}", res=xs)
            rec.update(a=a, gated=gated)
        h2 = _rmsnorm_fwd(x_mid, ffn_norm_w[i], f"ffn_norm_{i}")
        fa, s = _ffn_in_act(h2, w_fi_il[i], f"ffn_in_{i}")
        xs = _mm(s, w_fo[i], "nn", F32, f"ffn_out_{i}", res=x_mid)
        rec.update(x_mid=x_mid, h2=h2, fa=fa, s=s)
        saved.append(rec)

    gx, loss_acc, dw_final = _loss_head(xs, final_norm_w, loss_target.reshape(T, D), "loss_head")
    loss = lax.psum(loss_acc[0, 0], ("x", "y", "c"))

    n_attn, n_sgu = attn_w_in.shape[0], sgu_w_in.shape[0]
    d_mixer_norm, d_ffn_norm = [None] * depth, [None] * depth
    d_ai, d_ao, d_bf = [None] * n_attn, [None] * n_attn, [None] * n_attn
    d_si, d_so, d_lng, d_lnb, d_ws, d_bs = ([None] * n_sgu for _ in range(6))
    d_fi, d_fo = [None] * depth, [None] * depth
    for i in reversed(range(depth)):
        j = i // 2
        rec = saved[i]
        d_fo[i] = _mm(rec["s"], gx, "tn", F32, f"ffn_out_wgrad_{i}")
        da = _ffn_out_bwd_act(gx, w_fo[i], rec["fa"], f"ffn_out_bwd_{i}")
        dh2 = _mm(da, w_fi_il[i], "nt", F32, f"ffn_in_bwd_{i}")
        d_fi[i] = _deinterleave_gu(_mm(rec["h2"], da, "tn", F32, f"ffn_in_wgrad_{i}"), ffn_tf)
        gx, dwn = _rmsnorm_bwd(dh2, rec["x_mid"], ffn_norm_w[i], gx, f"ffn_norm_bwd_{i}")
        d_ffn_norm[i] = dwn[0]
        if i % 2 == 0:
            do = _mm(gx, w_ao[j], "nt", BF16, f"attn_out_bwd_{i}")
            d_ao[j] = _mm(rec["o"], gx, "tn", F32, f"attn_out_wgrad_{i}")
            dT = _attn_delta(do, rec["o"], P, f"attn_delta_{i}")
            dq, dk, dv, dc_cols, drowT = _attn_bwd(rec["qkv"], do, rec["lseT"], dT, rec["c0T"], rec["c_cols"],
                                                   P, scale, f"attn_bwd_{i}")
            df, dbf = _gate_bwd(dc_cols, drowT, rec["f"], b_f_pad[j:j + 1], P, f"gate_bwd_{i}")
            d_bf[j] = dbf[0, :H]
            dproj = jnp.concatenate([dq, dk, dv, df.astype(BF16)], axis=1)
            dh = _mm(dproj, w_ai[j], "nt", F32, f"attn_in_bwd_{i}")
            d_ai[j] = _mm(rec["h"], dproj, "tn", F32, f"attn_in_wgrad_{i}")[:, :3 * D + H]
        else:
            dgated = _mm(gx, w_so[j], "nt", F32, f"sgu_out_bwd_{i}")
            d_so[j] = _mm(rec["gated"], gx, "tn", F32, f"sgu_out_wgrad_{i}")
            da_s, dws, dbias, dlng, dlnb = _sgu_bwd(rec["a"], dgated, ln_g[j], ln_b[j], w_tril_b[j],
                                                    w_tril_tb[j], sgu_bias[j], f"sgu_bwd_{i}")
            d_ws[j] = jnp.tril(dws)
            d_bs[j] = jnp.sum(dbias.reshape(LANES, G, LANES), axis=2).T
            d_lng[j], d_lnb[j] = dlng[0], dlnb[0]
            dh = _mm(da_s, w_si[j], "nt", F32, f"sgu_in_bwd_{i}")
            d_si[j] = _mm(rec["h"], da_s, "tn", F32, f"sgu_in_wgrad_{i}")
        gx, dwn = _rmsnorm_bwd(dh, rec["x_in"], mixer_norm_w[i], gx, f"mix_norm_bwd_{i}")
        d_mixer_norm[i] = dwn[0]
    grad_x = gx.reshape(x.shape)

    full_grads = [
        _cols_to_chips(jnp.stack(d_ai)), _rows_to_chips(jnp.stack(d_ao)),
        _cols_to_chips(jnp.stack(d_si)), _rows_to_chips(jnp.stack(d_so)),
        _cols_to_chips(jnp.stack(d_fi)), _rows_to_chips(jnp.stack(d_fo)),
        jnp.transpose(jnp.stack(d_lng).reshape(n_sgu, 4, W // 4), (1, 0, 2)),
        jnp.transpose(jnp.stack(d_lnb).reshape(n_sgu, 4, W // 4), (1, 0, 2)),
    ]
    g_all = jnp.stack([_pack([fg[k] for fg in full_grads], 512) for k in range(4)])
    from_sibling = _sibling_halves(g_all, "grad_sibling_halves")
    pair = _pair_sum(g_all, from_sibling, c_idx, "grad_pair_sum")
    from_chips = _chip_exchange(pair, "grad_chip_exchange")
    my_half = _sum_parts(from_chips, "grad_chip_sum")
    sibling_half = _swap_with_sibling(my_half, "grad_swap_halves")
    g_shard, d_shard, m_shard, v_shard = _adamw_halves(
        my_half, sibling_half, c_idx, w_pack, _pack(sharded_m, 512), _pack(sharded_v, 512), "adamw_sharded")
    g_sh = _unpack(g_shard, shard_shapes)
    d_sh = _unpack(d_shard, shard_shapes)
    m_sh = _unpack(m_shard, shard_shapes)
    v_sh = _unpack(v_shard, shard_shapes)

    repl = [mixer_norm_w, attn_b_f, sgu_w_s, sgu_b_s, ffn_norm_w, final_norm_w]
    repl_m = [m_mixer_norm_w, m_attn_b_f, m_sgu_w_s, m_sgu_b_s, m_ffn_norm_w, m_final_norm_w]
    repl_v = [v_mixer_norm_w, v_attn_b_f, v_sgu_w_s, v_sgu_b_s, v_ffn_norm_w, v_final_norm_w]
    repl_shapes = [a.shape for a in repl]
    repl_grads = [jnp.stack(d_mixer_norm), jnp.stack(d_bf), jnp.stack(d_ws), jnp.stack(d_bs),
                  jnp.stack(d_ffn_norm), dw_final[0]]
    parts = _gather_all(_pack(repl_grads, SUBLANES), "grad_gather_replicated")
    g_rep, d_rep, m_rep, v_rep = _adamw_sum(parts, _pack(repl, SUBLANES), _pack(repl_m, SUBLANES),
                                            _pack(repl_v, SUBLANES), "adamw_replicated")
    g_r = _unpack(g_rep, repl_shapes)
    d_r = _unpack(d_rep, repl_shapes)
    m_r = _unpack(m_rep, repl_shapes)
    v_r = _unpack(v_rep, repl_shapes)

    def ordered(sh, rp):
        ai, ao, si, so, fi, fo, lng, lnb = sh
        mn, bf, ws, bs, fn, fin = rp
        return [mn, ai, bf, ao, si, lng, lnb, ws, bs, so, fn, fi, fo, fin]

    return (loss, grad_x, *ordered(g_sh, g_r), *ordered(d_sh, d_r), *ordered(m_sh, m_r), *ordered(v_sh, v_r))
```

```python
import functools
import math

import jax
import jax.numpy as jnp
from jax import lax
from jax.experimental import pallas as pl
from jax.experimental.pallas import tpu as pltpu

F32 = jnp.float32
BF16 = jnp.bfloat16
NORM_EPS = 1e-6
LN_EPS = 1e-5
ADAM_LR = 0.001
ADAM_B1 = 0.9
ADAM_B2 = 0.999
ADAM_EPS = 1e-08
ADAM_WD = 0.01
ADAM_STEP = 10

LANES = 128
SUBLANES = 8
PACK_COLS = 1024
VMEM_LIMIT = 56 * 1024 * 1024
NEG_BIG = -1e30
MESH = pl.DeviceIdType.MESH


def _cp():
    return pltpu.CompilerParams(vmem_limit_bytes=VMEM_LIMIT)


def _tile(n, cap, mult):
    best = None
    d = mult
    while d <= min(n, cap):
        if n % d == 0:
            best = d
        d += mult
    return n if best is None else best


def _hbm():
    return pl.BlockSpec(memory_space=pltpu.HBM)


def _rmsnorm_fwd(x, w, name):
    T, D = x.shape
    tm = _tile(T, 512, SUBLANES)

    def body(x_ref, w_ref, h_ref):
        xf = x_ref[...]
        r = lax.rsqrt(jnp.mean(xf * xf, axis=-1, keepdims=True) + NORM_EPS)
        h_ref[...] = (xf * r * w_ref[...]).astype(BF16)

    return pl.pallas_call(
        body, grid=(T // tm,),
        in_specs=[pl.BlockSpec((tm, D), lambda i: (i, 0)), pl.BlockSpec((1, D), lambda i: (0, 0))],
        out_specs=pl.BlockSpec((tm, D), lambda i: (i, 0)),
        out_shape=jax.ShapeDtypeStruct((T, D), BF16), name=name, compiler_params=_cp(),
    )(x, w.reshape(1, D))


def _rmsnorm_bwd(dh, x, w, dres, name):
    T, D = x.shape
    tm = _tile(T, 512, SUBLANES)

    def body(dh_ref, x_ref, w_ref, dres_ref, dx_ref, dw_ref):
        @pl.when(pl.program_id(0) == 0)
        def _():
            dw_ref[...] = jnp.zeros_like(dw_ref)

        xf = x_ref[...]
        r = lax.rsqrt(jnp.mean(xf * xf, axis=-1, keepdims=True) + NORM_EPS)
        xhat = xf * r
        dhv = dh_ref[...]
        dxhat = dhv * w_ref[...]
        dx_ref[...] = dres_ref[...] + r * (dxhat - xhat * jnp.mean(dxhat * xhat, axis=-1, keepdims=True))
        dw_ref[...] += jnp.sum(dhv * xhat, axis=0, keepdims=True)

    row = pl.BlockSpec((tm, D), lambda i: (i, 0))
    return pl.pallas_call(
        body, grid=(T // tm,),
        in_specs=[row, row, pl.BlockSpec((1, D), lambda i: (0, 0)), row],
        out_specs=[row, pl.BlockSpec((SUBLANES, D), lambda i: (0, 0))],
        out_shape=[jax.ShapeDtypeStruct((T, D), F32), jax.ShapeDtypeStruct((SUBLANES, D), F32)],
        name=name, compiler_params=_cp(),
    )(dh, x, w.reshape(1, D), dres)


def _mm(a, b, mode, out_dtype, name, res=None):
    if mode == "tn":
        kt, M = a.shape
        N = b.shape[1]
        tm = _tile(M, 1408, LANES)
        tn = _tile(N, 1408, LANES)
        tk = _tile(kt, 512, 16)

        def body(a_ref, b_ref, o_ref):
            @pl.when(pl.program_id(2) == 0)
            def _():
                o_ref[...] = jnp.zeros_like(o_ref)

            o_ref[...] += lax.dot_general(
                a_ref[...].astype(BF16), b_ref[...].astype(BF16), (((0,), (0,)), ((), ())),
                preferred_element_type=F32)

        return pl.pallas_call(
            body, grid=(M // tm, N // tn, kt // tk),
            in_specs=[pl.BlockSpec((tk, tm), lambda i, j, k: (k, i)),
                      pl.BlockSpec((tk, tn), lambda i, j, k: (k, j))],
            out_specs=pl.BlockSpec((tm, tn), lambda i, j, k: (i, j)),
            out_shape=jax.ShapeDtypeStruct((M, N), F32), name=name, compiler_params=_cp(),
        )(a, b)

    M, K = a.shape
    N = b.shape[1] if mode == "nn" else b.shape[0]
    tm = _tile(M, 512, 16)
    cap = max(LANES, min(1408, ((6 << 20) // (2 * K)) // LANES * LANES))
    tn = _tile(N, cap, LANES)
    dims = (((1,), (0,)), ((), ())) if mode == "nn" else (((1,), (1,)), ((), ()))

    def body(*refs):
        if res is None:
            a_ref, b_ref, o_ref = refs
        else:
            a_ref, b_ref, r_ref, o_ref = refs
        acc = lax.dot_general(a_ref[...].astype(BF16), b_ref[...].astype(BF16), dims,
                              preferred_element_type=F32)
        if res is not None:
            acc = acc + r_ref[...]
        o_ref[...] = acc.astype(out_dtype)

    b_spec = (pl.BlockSpec((K, tn), lambda i, j: (0, j)) if mode == "nn"
              else pl.BlockSpec((tn, K), lambda i, j: (j, 0)))
    in_specs = [pl.BlockSpec((tm, K), lambda i, j: (i, 0)), b_spec]
    args = [a, b]
    if res is not None:
        in_specs.append(pl.BlockSpec((tm, tn), lambda i, j: (i, j)))
        args.append(res)
    return pl.pallas_call(
        body, grid=(M // tm, N // tn), in_specs=in_specs,
        out_specs=pl.BlockSpec((tm, tn), lambda i, j: (i, j)),
        out_shape=jax.ShapeDtypeStruct((M, N), out_dtype), name=name, compiler_params=_cp(),
    )(*args)


def _ffn_in_act(h, w5, layer, name):
    T, D = h.shape
    n = w5.shape[-1]
    tm = _tile(T, 256, 16)

    def body(h_ref, w_ref, a_ref, s_ref):
        hv = h_ref[...]
        g = jnp.dot(hv, w_ref[0, 0, 0], preferred_element_type=F32)
        u = jnp.dot(hv, w_ref[1, 0, 0], preferred_element_type=F32)
        a_ref[0, 0] = g.astype(BF16)
        a_ref[1, 0] = u.astype(BF16)
        s_ref[...] = (g * jax.nn.sigmoid(g) * u).astype(BF16)

    return pl.pallas_call(
        body, grid=(2, T // tm),
        in_specs=[pl.BlockSpec((tm, D), lambda j, i: (i, 0)),
                  pl.BlockSpec((2, 1, 1, D, n), lambda j, i: (0, j, layer, 0, 0))],
        out_specs=[pl.BlockSpec((2, 1, tm, n), lambda j, i: (0, j, i, 0)), pl.BlockSpec((tm, n), lambda j, i: (i, j))],
        out_shape=[jax.ShapeDtypeStruct((2, 2, T, n), BF16), jax.ShapeDtypeStruct((T, 2 * n), BF16)],
        name=name, compiler_params=_cp(),
    )(h, w5)


def _ffn_out_bwd_act(gx, w_out, a4, name):
    T, D = gx.shape
    n = a4.shape[-1]
    tm = _tile(T, 256, 16)

    def body(gx_ref, w_ref, a_ref, da_ref):
        ds = _nt(gx_ref[...].astype(BF16), w_ref[...])
        g = a_ref[0, 0].astype(F32)
        u = a_ref[1, 0].astype(F32)
        sg = jax.nn.sigmoid(g)
        da_ref[0, 0] = (ds * u * (sg * (1.0 + g * (1.0 - sg)))).astype(BF16)
        da_ref[1, 0] = (ds * (g * sg)).astype(BF16)

    blk = pl.BlockSpec((2, 1, tm, n), lambda j, i: (0, j, i, 0))
    return pl.pallas_call(
        body, grid=(2, T // tm),
        in_specs=[pl.BlockSpec((tm, D), lambda j, i: (i, 0)), pl.BlockSpec((n, D), lambda j, i: (j, 0)), blk],
        out_specs=blk,
        out_shape=jax.ShapeDtypeStruct((2, 2, T, n), BF16), name=name, compiler_params=_cp(),
    )(gx, w_out, a4)


def _mm_nt_shards(a4, w4, layer, name):
    K, T, n = a4.shape
    D = w4.shape[2]
    tm = _tile(T, 512, 16)
    tn = _tile(D, 512, LANES)

    def body(a_ref, w_ref, o_ref):
        @pl.when(pl.program_id(2) == 0)
        def _():
            o_ref[...] = jnp.zeros_like(o_ref)

        o_ref[...] += _nt(a_ref[0], w_ref[0, 0])

    return pl.pallas_call(
        body, grid=(T // tm, D // tn, K),
        in_specs=[pl.BlockSpec((1, tm, n), lambda i, j, k: (k, i, 0)),
                  pl.BlockSpec((1, 1, tn, n), lambda i, j, k: (k, layer, j, 0))],
        out_specs=pl.BlockSpec((tm, tn), lambda i, j, k: (i, j)),
        out_shape=jax.ShapeDtypeStruct((T, D), F32), name=name, compiler_params=_cp(),
    )(a4, w4)


def _mm_tn_shards(h, a4, name):
    K, T, n = a4.shape
    D = h.shape[1]
    tk = _tile(T, 512, 16)

    def body(h_ref, a_ref, o_ref):
        @pl.when(pl.program_id(1) == 0)
        def _():
            o_ref[...] = jnp.zeros_like(o_ref)

        o_ref[0] += lax.dot_general(h_ref[...], a_ref[0], (((0,), (0,)), ((), ())), preferred_element_type=F32)

    return pl.pallas_call(
        body, grid=(K, T // tk),
        in_specs=[pl.BlockSpec((tk, D), lambda k, t: (t, 0)), pl.BlockSpec((1, tk, n), lambda k, t: (k, t, 0))],
        out_specs=pl.BlockSpec((1, D, n), lambda k, t: (k, 0, 0)),
        out_shape=jax.ShapeDtypeStruct((K, D, n), F32), name=name, compiler_params=_cp(),
    )(h, a4)


def _loss_head(x, w, tgt, name):
    T, D = x.shape
    tm = _tile(T, 512, SUBLANES)

    def body(x_ref, w_ref, t_ref, dx_ref, loss_ref, dw_ref):
        @pl.when(pl.program_id(0) == 0)
        def _():
            loss_ref[...] = jnp.zeros_like(loss_ref)
            dw_ref[...] = jnp.zeros_like(dw_ref)

        xf = x_ref[...]
        wv = w_ref[...]
        r = lax.rsqrt(jnp.mean(xf * xf, axis=-1, keepdims=True) + NORM_EPS)
        xhat = xf * r
        err = xhat * wv - t_ref[...]
        per_tok = jnp.mean(err * err, axis=-1, keepdims=True)
        loss_ref[...] += 0.5 * jnp.sum(per_tok, axis=0, keepdims=True)
        dy = err * (1.0 / D)
        dxhat = dy * wv
        dx_ref[...] = r * (dxhat - xhat * jnp.mean(dxhat * xhat, axis=-1, keepdims=True))
        dw_ref[...] += jnp.sum(dy * xhat, axis=0, keepdims=True)

    row = pl.BlockSpec((tm, D), lambda i: (i, 0))
    return pl.pallas_call(
        body, grid=(T // tm,),
        in_specs=[row, pl.BlockSpec((1, D), lambda i: (0, 0)), row],
        out_specs=[row, pl.BlockSpec((SUBLANES, LANES), lambda i: (0, 0)),
                   pl.BlockSpec((SUBLANES, D), lambda i: (0, 0))],
        out_shape=[jax.ShapeDtypeStruct((T, D), F32), jax.ShapeDtypeStruct((SUBLANES, LANES), F32),
                   jax.ShapeDtypeStruct((SUBLANES, D), F32)],
        name=name, compiler_params=_cp(),
    )(x, w.reshape(1, D), tgt)


def _split3(v):
    hi = v.astype(BF16)
    r1 = v - hi.astype(F32)
    mid = r1.astype(BF16)
    lo = (r1 - mid.astype(F32)).astype(BF16)
    return hi, mid, lo


def _tri_dot(tri, v):
    out = None
    for piece in _split3(v):
        t = jnp.dot(tri, piece, preferred_element_type=F32)
        out = t if out is None else out + t
    return out


def _q_block(T):
    return _tile(T, 256, LANES)


def _gate_fwd(f, b_f, P, name):
    T = f.shape[0]
    tb = _q_block(T)

    def body(f_ref, b_ref, ct_ref, cc_ref, c0_ref, carry):
        @pl.when(pl.program_id(0) == 0)
        def _():
            carry[...] = jnp.zeros_like(carry)

        z = f_ref[...] + b_ref[...]
        logf = jnp.minimum(z, 0.0) - jnp.log(1.0 + jnp.exp(-jnp.abs(z)))
        row = lax.broadcasted_iota(jnp.int32, (tb, tb), 0)
        col = lax.broadcasted_iota(jnp.int32, (tb, tb), 1)
        tri = (col <= row).astype(BF16)
        c = _tri_dot(tri, logf) + carry[0:1, :]
        carry[...] = jnp.broadcast_to(c[tb - 1:tb, :], carry.shape)
        first = jnp.broadcast_to(c[0:1, :], c.shape)
        for p in range(P):
            shifted = c if p == 0 else pltpu.roll(c, LANES - 2 * p, 1)
            cc_ref[p] = shifted
            ct_ref[p] = shifted.T[0:SUBLANES, :]
            c0_ref[p] = (first if p == 0 else pltpu.roll(first, LANES - 2 * p, 1)).T[0:SUBLANES, :]

    rows = pl.BlockSpec((P, SUBLANES, tb), lambda i: (0, 0, i))
    return pl.pallas_call(
        body, grid=(T // tb,),
        in_specs=[pl.BlockSpec((tb, LANES), lambda i: (i, 0)), pl.BlockSpec((1, LANES), lambda i: (0, 0))],
        out_specs=[rows, pl.BlockSpec((P, tb, LANES), lambda i: (0, i, 0)), rows],
        out_shape=[jax.ShapeDtypeStruct((P, SUBLANES, T), F32), jax.ShapeDtypeStruct((P, T, LANES), F32),
                   jax.ShapeDtypeStruct((P, SUBLANES, T), F32)],
        scratch_shapes=[pltpu.VMEM((SUBLANES, LANES), F32)],
        name=name, compiler_params=_cp(),
    )(f, b_f)


def _gate_bwd(dc_cols, drowT, f, b_f, P, name):
    T = f.shape[0]
    tb = _tile(T, 256, LANES)
    nb = T // tb

    def body(dc_ref, dr_ref, f_ref, b_ref, df_ref, db_ref, carry):
        @pl.when(pl.program_id(0) == 0)
        def _():
            carry[...] = jnp.zeros_like(carry)
            db_ref[...] = jnp.zeros_like(db_ref)

        lane = lax.broadcasted_iota(jnp.int32, (tb, LANES), 1)
        dc = jnp.zeros((tb, LANES), F32)
        for p in range(P):
            rows = jnp.concatenate([dr_ref[p], jnp.zeros((LANES - SUBLANES, tb), F32)], axis=0)
            part = jnp.where(lane < 2, dc_ref[p] + rows.T, 0.0)
            dc = dc + (part if p == 0 else pltpu.roll(part, 2 * p, 1))
        row = lax.broadcasted_iota(jnp.int32, (tb, tb), 0)
        col = lax.broadcasted_iota(jnp.int32, (tb, tb), 1)
        tri = (col >= row).astype(BF16)
        dlogf = _tri_dot(tri, dc) + carry[0:1, :]
        carry[...] = jnp.broadcast_to(dlogf[0:1, :], carry.shape)
        z = f_ref[...] + b_ref[...]
        df = jnp.where(lane < 2 * P, dlogf * jax.nn.sigmoid(-z), 0.0)
        df_ref[...] = df
        db_ref[...] += jnp.sum(df, axis=0, keepdims=True)

    return pl.pallas_call(
        body, grid=(nb,),
        in_specs=[pl.BlockSpec((P, tb, LANES), lambda i: (0, nb - 1 - i, 0)),
                  pl.BlockSpec((P, SUBLANES, tb), lambda i: (0, 0, nb - 1 - i)),
                  pl.BlockSpec((tb, LANES), lambda i: (nb - 1 - i, 0)),
                  pl.BlockSpec((1, LANES), lambda i: (0, 0))],
        out_specs=[pl.BlockSpec((tb, LANES), lambda i: (nb - 1 - i, 0)),
                   pl.BlockSpec((SUBLANES, LANES), lambda i: (0, 0))],
        out_shape=[jax.ShapeDtypeStruct((T, LANES), F32), jax.ShapeDtypeStruct((SUBLANES, LANES), F32)],
        scratch_shapes=[pltpu.VMEM((SUBLANES, LANES), F32)],
        name=name, compiler_params=_cp(),
    )(dc_cols, drowT, f, b_f)


def _nt(a, b):
    return lax.dot_general(a, b, (((1,), (1,)), ((), ())), preferred_element_type=F32)


def _attn_fwd(qkv, cT, P, scale, name):
    T = qkv.shape[0]
    tq = _q_block(T)
    tw = _tile(T, 4 * tq, 2 * tq)
    cw = tw // 2
    assert cw % tq == 0, "the sequence must split into chunks of whole query blocks"
    nq = T // tq

    def body(q_ref, k_ref, v_ref, c_ref, o_ref, lse_ref, s_scr):
        i = pl.program_id(1)
        lane = lax.broadcasted_iota(jnp.int32, (1, LANES), 1)
        q = (q_ref[...].astype(F32) * scale).astype(BF16)
        q_heads = (jnp.where(lane < 64, q, jnp.zeros_like(q)), jnp.where(lane >= 64, q, jnp.zeros_like(q)))
        c0 = c_ref[0, :, pl.ds(pl.multiple_of(i * tq, tq), LANES)][:, 0:1]

        def scores(start, width, a):
            bias = c0 - c_ref[0, :, pl.ds(start, width)]
            return _nt(q_heads[a], k_ref[pl.ds(start, width), :]) + bias[a:a + 1, :]

        def softmax_pv(start, width, s_of, carry):
            v = v_ref[pl.ds(start, width), :]
            one = jnp.ones_like(v)
            v_heads = (jnp.where(lane < 64, v, one), jnp.where(lane >= 64, v, one))
            new = []
            for a in range(2):
                m, acc = carry[a]
                s = s_of(a)
                m_new = jnp.maximum(m, jnp.max(s, axis=1, keepdims=True))
                p = jnp.exp(s - m_new)
                acc = jnp.exp(m - m_new) * acc + jnp.dot(p.astype(BF16), v_heads[a], preferred_element_type=F32)
                new.append((m_new, acc))
            return tuple(new)

        def fill(start, buf):
            for a in range(2):
                s_scr[2 * buf + a] = scores(start, cw, a)

        def wide(j, carry):
            base = pl.multiple_of(j * tw, tw)
            fill(base + cw, 1)
            carry = softmax_pv(base, cw, lambda a: s_scr[a], carry)
            fill(base + tw, 0)
            return softmax_pv(base + cw, cw, lambda a: s_scr[2 + a], carry)

        init = tuple((jnp.full((tq, 1), NEG_BIG, F32), jnp.zeros((tq, LANES), F32)) for _ in range(2))
        n_wide = (i * tq) // tw
        fill(0, 0)
        carry = lax.fori_loop(0, n_wide, wide, init)

        base = pl.multiple_of(n_wide * tw, tw)
        ahead = i * tq - base
        col_minus_row = (lax.broadcasted_iota(jnp.int32, (tq, cw), 1)
                         - lax.broadcasted_iota(jnp.int32, (tq, cw), 0))

        def causal(buf, first_key):
            return lambda a: jnp.where(col_minus_row <= ahead - first_key, s_scr[2 * buf + a], NEG_BIG)

        def one_chunk(cr):
            return softmax_pv(base, cw, causal(0, 0), cr)

        def two_chunks(cr):
            fill(base + cw, 1)
            cr = softmax_pv(base, cw, causal(0, 0), cr)
            return softmax_pv(base + cw, cw, causal(1, cw), cr)

        (m0, a0), (m1, a1) = lax.cond(ahead >= cw, two_chunks, one_chunk, carry)
        sums = jnp.where(lane < 64, pltpu.roll(a0, 64, 1), pltpu.roll(a1, 64, 1))
        o_ref[...] = (jnp.where(lane < 64, a0, a1) / sums).astype(BF16)
        l0, l1 = a0[:, 64:65], a1[:, 0:1]
        lse = jnp.where(lane == 0, m0 + jnp.log(l0), jnp.where(lane == 1, m1 + jnp.log(l1), 0.0))
        lse_ref[0] = lse.T[0:SUBLANES, :]

    return pl.pallas_call(
        body, grid=(P, nq),
        in_specs=[pl.BlockSpec((tq, LANES), lambda p, i: (i, p)),
                  pl.BlockSpec((T, LANES), lambda p, i: (0, P + p)),
                  pl.BlockSpec((T, LANES), lambda p, i: (0, 2 * P + p)),
                  pl.BlockSpec((1, SUBLANES, T), lambda p, i: (p, 0, 0))],
        out_specs=[pl.BlockSpec((tq, LANES), lambda p, i: (i, p)),
                   pl.BlockSpec((1, SUBLANES, tq), lambda p, i: (p, 0, i))],
        out_shape=[jax.ShapeDtypeStruct((T, LANES * P), BF16), jax.ShapeDtypeStruct((P, SUBLANES, T), F32)],
        scratch_shapes=[pltpu.VMEM((4, tq, cw), F32)],
        name=name, compiler_params=_cp(),
    )(qkv, qkv, qkv, cT)


def _attn_delta(do, o, P, name):
    T, D = o.shape
    tb = _tile(T, 256, LANES)

    def body(do_ref, o_ref, d_ref):
        lane = lax.broadcasted_iota(jnp.int32, (1, LANES), 1)
        for p in range(P):
            cols = slice(p * LANES, (p + 1) * LANES)
            prod = do_ref[:, cols].astype(F32) * o_ref[:, cols].astype(F32)
            d0 = jnp.sum(jnp.where(lane < 64, prod, 0.0), axis=1, keepdims=True)
            d1 = jnp.sum(jnp.where(lane >= 64, prod, 0.0), axis=1, keepdims=True)
            both = jnp.where(lane == 0, d0, jnp.where(lane == 1, d1, 0.0))
            d_ref[p] = both.T[0:SUBLANES, :]

    return pl.pallas_call(
        body, grid=(T // tb,),
        in_specs=[pl.BlockSpec((tb, D), lambda i: (i, 0)), pl.BlockSpec((tb, D), lambda i: (i, 0))],
        out_specs=pl.BlockSpec((P, SUBLANES, tb), lambda i: (0, 0, i)),
        out_shape=jax.ShapeDtypeStruct((P, SUBLANES, T), F32), name=name, compiler_params=_cp(),
    )(do, o)


def _attn_bwd(qkv, do, lseT, dT, c0T, c_cols, P, scale, name):
    T = qkv.shape[0]
    tq = _q_block(T)
    tw = _tile(T, 4 * tq, 2 * tq)
    cw = tw // 2
    assert cw % tq == 0, "the sequence must split into chunks of whole query blocks"
    nq = T // tq

    def body(q_ref, do_ref, k_ref, v_ref, lse_ref, d_ref, c0_ref, cc_ref,
             dq_ref, dk_ref, dv_ref, dc_ref, drow_ref, dq_acc0, dq_acc1):
        j = pl.program_id(1)

        @pl.when(j == 0)
        def _():
            dq_acc0[...] = jnp.zeros_like(dq_acc0)
            dq_acc1[...] = jnp.zeros_like(dq_acc1)

        lane = lax.broadcasted_iota(jnp.int32, (1, LANES), 1)
        in_head = (lane < 64, lane >= 64)
        k = k_ref[...]
        v = v_ref[...]
        zero = jnp.zeros_like(k)
        one = jnp.ones_like(k)
        k_heads = tuple(jnp.where(h, k, zero) for h in in_head)
        v_heads = tuple(jnp.where(h, v, zero) for h in in_head)
        k_ones = tuple(jnp.where(h, k, one) for h in in_head)
        cc = cc_ref[0]
        c_first = (cc[0:1, 0:1], cc[0:1, 1:2])
        c_rel = (cc[:, 0:1] - c_first[0], cc[:, 1:2] - c_first[1])
        dq_accs = (dq_acc0, dq_acc1)

        def block(start, width, carry, first_query=None):
            q = (q_ref[pl.ds(start, width), :].astype(F32) * scale).astype(BF16)
            q_one = jnp.ones_like(q)
            dov = do_ref[pl.ds(start, width), :]
            lse = lse_ref[0, :, pl.ds(start, width)]
            dlt = d_ref[0, :, pl.ds(start, width)]
            c0 = c0_ref[0, :, pl.ds(start, width)]
            new = []
            for a in range(2):
                dk_a, dv_a = carry[a]
                rowv = lse[a:a + 1, :] + (c_first[a] - c0[a:a + 1, :])
                st = _nt(k_heads[a], q)
                pt = jnp.exp((st - c_rel[a]) - rowv)
                if first_query is not None:
                    row = lax.broadcasted_iota(jnp.int32, (tq, width), 0)
                    col = lax.broadcasted_iota(jnp.int32, (tq, width), 1)
                    pt = jnp.where(col - row >= first_query, pt, 0.0)
                dpt = _nt(v_heads[a], dov)
                dst_b = (pt * (dpt - dlt[a:a + 1, :])).astype(BF16)
                dv_a = dv_a + jnp.dot(pt.astype(BF16), dov, preferred_element_type=F32)
                dk_a = dk_a + jnp.dot(dst_b, jnp.where(in_head[a], q, q_one), preferred_element_type=F32)
                dq_accs[a][pl.ds(start, width), :] += lax.dot_general(
                    dst_b, k_ones[a], (((0,), (0,)), ((), ())), preferred_element_type=F32)
                new.append((dk_a, dv_a))
            return tuple(new)

        init = tuple((jnp.zeros((tq, LANES), F32), jnp.zeros((tq, LANES), F32)) for _ in range(2))
        first_key = j * tq
        diag = pl.multiple_of((first_key // cw) * cw, cw)
        carry = block(diag, cw, init, first_key - diag)
        first_wide = first_key // tw + 1
        carry = lax.cond(
            diag + cw < first_wide * tw,
            lambda cr: block(pl.multiple_of(diag + cw, cw), cw, cr), lambda cr: cr, carry)
        (dk0, dv0), (dk1, dv1) = lax.fori_loop(
            first_wide, T // tw, lambda i, cr: block(pl.multiple_of(i * tw, tw), tw, cr), carry)
        dk_ref[...] = jnp.where(lane < 64, dk0, dk1).astype(BF16)
        dv_ref[...] = jnp.where(lane < 64, dv0, dv1).astype(BF16)
        dc_ref[0] = jnp.where(lane == 0, -dk0[:, 64:65], jnp.where(lane == 1, -dk1[:, 0:1], 0.0))

        @pl.when(j == nq - 1)
        def _():
            def finish(i, _):
                rows = pl.ds(pl.multiple_of(i * tq, tq), tq)
                a0 = dq_acc0[rows, :]
                a1 = dq_acc1[rows, :]
                dq_ref[rows, :] = (jnp.where(lane < 64, a0, a1) * scale).astype(BF16)
                sums = jnp.where(lane == 0, a0[:, 64:65], jnp.where(lane == 1, a1[:, 0:1], 0.0))
                drow_ref[0, :, rows] = sums.T[0:SUBLANES, :]
                return 0

            lax.fori_loop(0, nq, finish, 0)

    full = lambda col: pl.BlockSpec((T, LANES), lambda p, j: (0, col(p)))
    blk = lambda col: pl.BlockSpec((tq, LANES), lambda p, j: (j, col(p)))
    rows = pl.BlockSpec((1, SUBLANES, T), lambda p, j: (p, 0, 0))
    cols = pl.BlockSpec((1, tq, LANES), lambda p, j: (p, j, 0))
    D = LANES * P
    return pl.pallas_call(
        body, grid=(P, nq),
        in_specs=[full(lambda p: p), full(lambda p: p), blk(lambda p: P + p), blk(lambda p: 2 * P + p),
                  rows, rows, rows, cols],
        out_specs=[full(lambda p: p), blk(lambda p: p), blk(lambda p: p), cols, rows],
        out_shape=[jax.ShapeDtypeStruct((T, D), BF16), jax.ShapeDtypeStruct((T, D), BF16),
                   jax.ShapeDtypeStruct((T, D), BF16), jax.ShapeDtypeStruct((P, T, LANES), F32),
                   jax.ShapeDtypeStruct((P, SUBLANES, T), F32)],
        scratch_shapes=[pltpu.VMEM((T, LANES), F32), pltpu.VMEM((T, LANES), F32)],
        name=name, compiler_params=_cp(),
    )(qkv, do, qkv, qkv, lseT, dT, c0T, c_cols)


_SQRT_HALF = 0.7071067811865476
_INV_SQRT_2PI = 0.3989422804014327


def _gelu(v):
    return 0.5 * v * (1.0 + lax.erf(v * _SQRT_HALF))


def _gelu_grad(v):
    return 0.5 * (1.0 + lax.erf(v * _SQRT_HALF)) + v * (_INV_SQRT_2PI * jnp.exp(-0.5 * v * v))


def _sgu_fwd(a, ln_g, ln_b, w_tril, bias, name):
    T, W2 = a.shape
    W = W2 // 2
    G = w_tril.shape[0]
    tb = _tile(T, 256, LANES)

    def body(a_ref, g_ref, b_ref, w_ref, bias_ref, out_ref):
        zu = _gelu(a_ref[:, :W].astype(F32))
        zv = _gelu(a_ref[:, W:].astype(F32))
        mu = jnp.mean(zv, axis=-1, keepdims=True)
        d = zv - mu
        rstd = lax.rsqrt(jnp.mean(d * d, axis=-1, keepdims=True) + LN_EPS)
        vn = (d * rstd * g_ref[...] + b_ref[...]).astype(BF16)
        for c in range(tb // LANES):
            rs = slice(c * LANES, (c + 1) * LANES)
            for g in range(G):
                cs = slice(g * LANES, (g + 1) * LANES)
                mixed = jnp.dot(w_ref[g], vn[rs, cs], preferred_element_type=F32) + bias_ref[:, cs]
                out_ref[rs, cs] = (zu[rs, cs] * mixed).astype(BF16)

    return pl.pallas_call(
        body, grid=(T // tb,),
        in_specs=[pl.BlockSpec((tb, W2), lambda i: (i, 0)), pl.BlockSpec((1, W), lambda i: (0, 0)),
                  pl.BlockSpec((1, W), lambda i: (0, 0)), pl.BlockSpec((G, LANES, LANES), lambda i: (0, 0, 0)),
                  pl.BlockSpec((LANES, W), lambda i: (0, 0))],
        out_specs=pl.BlockSpec((tb, W), lambda i: (i, 0)),
        out_shape=jax.ShapeDtypeStruct((T, W), BF16), name=name, compiler_params=_cp(),
    )(a, ln_g.reshape(1, W), ln_b.reshape(1, W), w_tril, bias)


def _sgu_bwd(a, dgated, ln_g, ln_b, w_tril, w_tril_t, bias, name):
    T, W2 = a.shape
    W = W2 // 2
    G = w_tril.shape[0]
    tb = _tile(T, 256, LANES)

    def body(a_ref, dg_ref, g_ref, b_ref, w_ref, wt_ref, bias_ref,
             da_ref, dws_ref, dbias_ref, dlng_ref, dlnb_ref, dvn_ref):
        @pl.when(pl.program_id(0) == 0)
        def _():
            dws_ref[...] = jnp.zeros_like(dws_ref)
            dbias_ref[...] = jnp.zeros_like(dbias_ref)
            dlng_ref[...] = jnp.zeros_like(dlng_ref)
            dlnb_ref[...] = jnp.zeros_like(dlnb_ref)

        up = a_ref[:, :W].astype(F32)
        vp = a_ref[:, W:].astype(F32)
        zu = _gelu(up)
        zv = _gelu(vp)
        mu = jnp.mean(zv, axis=-1, keepdims=True)
        d = zv - mu
        rstd = lax.rsqrt(jnp.mean(d * d, axis=-1, keepdims=True) + LN_EPS)
        vhat = d * rstd
        gam = g_ref[...]
        vn = (vhat * gam + b_ref[...]).astype(BF16)
        dgated = dg_ref[...]
        for c in range(tb // LANES):
            rs = slice(c * LANES, (c + 1) * LANES)
            for g in range(G):
                cs = slice(g * LANES, (g + 1) * LANES)
                vb = vn[rs, cs]
                mixed = jnp.dot(w_ref[g], vb, preferred_element_type=F32) + bias_ref[:, cs]
                dgt = dgated[rs, cs]
                da_ref[rs, cs] = (dgt * mixed * _gelu_grad(up[rs, cs])).astype(BF16)
                dmx = dgt * zu[rs, cs]
                dbias_ref[:, cs] += dmx
                dmb = dmx.astype(BF16)
                dws_ref[g] += _nt(dmb, vb)
                dvn_ref[rs, cs] = jnp.dot(wt_ref[g], dmb, preferred_element_type=F32)
        dvn = dvn_ref[...]
        dlng_ref[...] += jnp.sum(dvn * vhat, axis=0, keepdims=True)
        dlnb_ref[...] += jnp.sum(dvn, axis=0, keepdims=True)
        dvh = dvn * gam
        dzv = rstd * (dvh - jnp.mean(dvh, axis=-1, keepdims=True)
                      - vhat * jnp.mean(dvh * vhat, axis=-1, keepdims=True))
        da_ref[:, W:] = (dzv * _gelu_grad(vp)).astype(BF16)

    const2 = lambda shape: pl.BlockSpec(shape, lambda i: (0, 0))
    const3 = pl.BlockSpec((G, LANES, LANES), lambda i: (0, 0, 0))
    return pl.pallas_call(
        body, grid=(T // tb,),
        in_specs=[pl.BlockSpec((tb, W2), lambda i: (i, 0)), pl.BlockSpec((tb, W), lambda i: (i, 0)),
                  const2((1, W)), const2((1, W)), const3, const3, const2((LANES, W))],
        out_specs=[pl.BlockSpec((tb, W2), lambda i: (i, 0)), const3, const2((LANES, W)),
                   const2((SUBLANES, W)), const2((SUBLANES, W))],
        out_shape=[jax.ShapeDtypeStruct((T, W2), BF16), jax.ShapeDtypeStruct((G, LANES, LANES), F32),
                   jax.ShapeDtypeStruct((LANES, W), F32), jax.ShapeDtypeStruct((SUBLANES, W), F32),
                   jax.ShapeDtypeStruct((SUBLANES, W), F32)],
        scratch_shapes=[pltpu.VMEM((tb, W), F32)],
        name=name, compiler_params=_cp(),
    )(a, dgated, ln_g.reshape(1, W), ln_b.reshape(1, W), w_tril, w_tril_t, bias)


def _adam_math(w, g, m, v):
    m = ADAM_B1 * m + (1.0 - ADAM_B1) * g
    v = ADAM_B2 * v + (1.0 - ADAM_B2) * (g * g)
    m_hat = m / (1.0 - ADAM_B1 ** ADAM_STEP)
    v_hat = v / (1.0 - ADAM_B2 ** ADAM_STEP)
    delta = -ADAM_LR * (m_hat / (jnp.sqrt(v_hat) + ADAM_EPS) + ADAM_WD * w)
    return delta, m, v


def _adamw_halves(mine, theirs, c_idx, w, m, v, name):
    R, C = w.shape
    rh = R // 2
    tb = _tile(rh, 512, SUBLANES)
    nb = rh // tb

    def body(c_ref, a_ref, b_ref, w_ref, m_ref, v_ref, g_ref, d_ref, mo_ref, vo_ref):
        g = jnp.where(pl.program_id(0) == c_ref[0], a_ref[...], b_ref[...])
        d, mm, vv = _adam_math(w_ref[...], g, m_ref[...], v_ref[...])
        g_ref[...] = g
        d_ref[...] = d
        mo_ref[...] = mm
        vo_ref[...] = vv

    half = pl.BlockSpec((tb, C), lambda h, i, c: (i, 0))
    row = pl.BlockSpec((tb, C), lambda h, i, c: (h * nb + i, 0))
    sds = jax.ShapeDtypeStruct((R, C), F32)
    return pl.pallas_call(
        body,
        grid_spec=pltpu.PrefetchScalarGridSpec(
            num_scalar_prefetch=1, grid=(2, nb), in_specs=[half, half, row, row, row], out_specs=[row] * 4),
        out_shape=[sds] * 4, name=name, compiler_params=_cp())(c_idx, mine, theirs, w, m, v)


def _adamw_sum(parts, w, m, v, name):
    K, R, C = parts.shape
    tb = _tile(R, 128, SUBLANES)

    def body(p_ref, w_ref, m_ref, v_ref, g_ref, d_ref, mo_ref, vo_ref):
        g = p_ref[0]
        for k in range(1, K):
            g = g + p_ref[k]
        d, mm, vv = _adam_math(w_ref[...], g, m_ref[...], v_ref[...])
        g_ref[...] = g
        d_ref[...] = d
        mo_ref[...] = mm
        vo_ref[...] = vv

    row = pl.BlockSpec((tb, C), lambda i: (i, 0))
    sds = jax.ShapeDtypeStruct((R, C), F32)
    return pl.pallas_call(
        body, grid=(R // tb,),
        in_specs=[pl.BlockSpec((K, tb, C), lambda i: (0, i, 0)), row, row, row],
        out_specs=[row] * 4, out_shape=[sds] * 4, name=name, compiler_params=_cp())(parts, w, m, v)


def _pair_sum(g_all, recv, c_idx, name):
    K, R, C = g_all.shape
    rh = R // 2
    tb = _tile(rh, 512, 16)
    nb = rh // tb

    def body(c_ref, a_ref, b_ref, o_ref):
        o_ref[...] = (a_ref[...] + b_ref[...]).astype(BF16)

    return pl.pallas_call(
        body,
        grid_spec=pltpu.PrefetchScalarGridSpec(
            num_scalar_prefetch=1, grid=(K, nb),
            in_specs=[pl.BlockSpec((1, tb, C), lambda k, i, c: (k, c[0] * nb + i, 0)),
                      pl.BlockSpec((1, tb, C), lambda k, i, c: (k, i, 0))],
            out_specs=pl.BlockSpec((1, tb, C), lambda k, i, c: (k, i, 0))),
        out_shape=jax.ShapeDtypeStruct((K, rh, C), BF16), name=name, compiler_params=_cp(),
    )(c_idx, g_all, recv)


def _sum_parts(parts, name):
    K, R, C = parts.shape
    tb = _tile(R, 512, 16)

    def body(p_ref, o_ref):
        g = p_ref[0].astype(F32)
        for k in range(1, K):
            g = g + p_ref[k].astype(F32)
        o_ref[...] = g

    return pl.pallas_call(
        body, grid=(R // tb,), in_specs=[pl.BlockSpec((K, tb, C), lambda i: (0, i, 0))],
        out_specs=pl.BlockSpec((tb, C), lambda i: (i, 0)),
        out_shape=jax.ShapeDtypeStruct((R, C), F32), name=name, compiler_params=_cp())(parts)


_CHIP_RELATIONS = ((1, 0), (0, 1), (1, 1))


def _position():
    return lax.axis_index("x"), lax.axis_index("y"), lax.axis_index("c")


def _flip(v, bit):
    return 1 - v if bit else v


def _gather_weights(w_pack, ln_pack, name):
    R, C = w_pack.shape
    rh = R // 2

    def half(c):
        return pl.ds(pl.multiple_of(c * rh, 16), rh)

    def between_chips(w_ref, ln_ref, ow_ref, oln_ref, local_sem, send_sems, recv_sems):
        x, y, c = _position()
        me = 2 * x + y
        own_ln = pltpu.make_async_copy(ln_ref, oln_ref.at[me], local_sem)
        own_ln.start()

        def copies(r, slot):
            dx, dy = _CHIP_RELATIONS[r]
            peer = (_flip(x, dx), _flip(y, dy), c)
            cw = pltpu.make_async_remote_copy(
                src_ref=w_ref.at[half(c), :], dst_ref=ow_ref.at[slot, half(c), :], send_sem=send_sems.at[2 * r],
                recv_sem=recv_sems.at[2 * r], device_id=peer, device_id_type=MESH)
            cl = pltpu.make_async_remote_copy(
                src_ref=ln_ref, dst_ref=oln_ref.at[slot], send_sem=send_sems.at[2 * r + 1],
                recv_sem=recv_sems.at[2 * r + 1], device_id=peer, device_id_type=MESH)
            return cw, cl

        sent = [copies(r, me) for r in range(3)]
        for cw, cl in sent:
            cw.start()
            cl.start()
        for r in range(3):
            dx, dy = _CHIP_RELATIONS[r]
            cw, cl = copies(r, 2 * _flip(x, dx) + _flip(y, dy))
            cw.wait_recv()
            cl.wait_recv()
        for cw, cl in sent:
            cw.wait_send()
            cl.wait_send()
        own_ln.wait()

    halves, lns = pl.pallas_call(
        between_chips, in_specs=[_hbm(), _hbm()], out_specs=[_hbm(), _hbm()],
        out_shape=[jax.ShapeDtypeStruct((4, R, C), w_pack.dtype), jax.ShapeDtypeStruct((4, SUBLANES, C), F32)],
        scratch_shapes=[pltpu.SemaphoreType.DMA(()), pltpu.SemaphoreType.DMA((6,)), pltpu.SemaphoreType.DMA((6,))],
        name=name + "_ici", compiler_params=_cp(),
    )(w_pack, ln_pack)

    def to_sibling(g_ref, o_ref, send_sems, recv_sems):
        x, y, c = _position()

        def copy(r, rows):
            dx, dy = _CHIP_RELATIONS[r]
            slot = 2 * _flip(x, dx) + _flip(y, dy)
            return pltpu.make_async_remote_copy(
                src_ref=g_ref.at[slot, rows, :], dst_ref=o_ref.at[slot, rows, :], send_sem=send_sems.at[r],
                recv_sem=recv_sems.at[r], device_id=(x, y, 1 - c), device_id_type=MESH)

        sent = [copy(r, half(c)) for r in range(3)]
        for cp in sent:
            cp.start()
        for r in range(3):
            copy(r, half(1 - c)).wait_recv()
        for cp in sent:
            cp.wait_send()

    full = pl.pallas_call(
        to_sibling, in_specs=[_hbm()], out_specs=_hbm(), input_output_aliases={0: 0},
        out_shape=jax.ShapeDtypeStruct((4, R, C), w_pack.dtype),
        scratch_shapes=[pltpu.SemaphoreType.DMA((3,)), pltpu.SemaphoreType.DMA((3,))],
        name=name + "_d2d", compiler_params=_cp(),
    )(halves)
    return full, lns


def _sibling_halves(g_all, name):
    K, R, C = g_all.shape
    rh = R // 2

    def body(g_ref, o_ref, send_sem, recv_sem):
        x, y, c = _position()
        start = pl.multiple_of((1 - c) * rh, SUBLANES)
        cp = pltpu.make_async_remote_copy(
            src_ref=g_ref.at[:, pl.ds(start, rh), :], dst_ref=o_ref, send_sem=send_sem, recv_sem=recv_sem,
            device_id=(x, y, 1 - c), device_id_type=MESH)
        cp.start()
        cp.wait_recv()
        cp.wait_send()

    return pl.pallas_call(
        body, in_specs=[_hbm()], out_specs=_hbm(),
        out_shape=jax.ShapeDtypeStruct((K, rh, C), F32),
        scratch_shapes=[pltpu.SemaphoreType.DMA(()), pltpu.SemaphoreType.DMA(())],
        name=name, compiler_params=_cp(),
    )(g_all)


def _chip_exchange(parts, name):
    K, R, C = parts.shape

    def body(p_ref, o_ref, local_sem, send_sems, recv_sems):
        x, y, c = _position()
        me = 2 * x + y
        own = pltpu.make_async_copy(p_ref.at[me], o_ref.at[me], local_sem)
        own.start()

        def copy(r, src_slot, dst_slot):
            dx, dy = _CHIP_RELATIONS[r]
            return pltpu.make_async_remote_copy(
                src_ref=p_ref.at[src_slot], dst_ref=o_ref.at[dst_slot], send_sem=send_sems.at[r],
                recv_sem=recv_sems.at[r], device_id=(_flip(x, dx), _flip(y, dy), c), device_id_type=MESH)

        def chip(r):
            dx, dy = _CHIP_RELATIONS[r]
            return 2 * _flip(x, dx) + _flip(y, dy)

        sent = [copy(r, chip(r), me) for r in range(3)]
        for cp in sent:
            cp.start()
        for r in range(3):
            copy(r, me, chip(r)).wait_recv()
        for cp in sent:
            cp.wait_send()
        own.wait()

    return pl.pallas_call(
        body, in_specs=[_hbm()], out_specs=_hbm(),
        out_shape=jax.ShapeDtypeStruct((K, R, C), parts.dtype),
        scratch_shapes=[pltpu.SemaphoreType.DMA(()), pltpu.SemaphoreType.DMA((3,)),
                        pltpu.SemaphoreType.DMA((3,))],
        name=name, compiler_params=_cp(),
    )(parts)


def _swap_with_sibling(half, name):
    rh, C = half.shape

    def body(h_ref, o_ref, send_sem, recv_sem):
        x, y, c = _position()
        cp = pltpu.make_async_remote_copy(
            src_ref=h_ref, dst_ref=o_ref, send_sem=send_sem, recv_sem=recv_sem,
            device_id=(x, y, 1 - c), device_id_type=MESH)
        cp.start()
        cp.wait_recv()
        cp.wait_send()

    return pl.pallas_call(
        body, in_specs=[_hbm()], out_specs=_hbm(),
        out_shape=jax.ShapeDtypeStruct((rh, C), F32),
        scratch_shapes=[pltpu.SemaphoreType.DMA(()), pltpu.SemaphoreType.DMA(())],
        name=name, compiler_params=_cp(),
    )(half)


def _gather_all(part, name):
    R, C = part.shape
    masks = [(b >> 2 & 1, b >> 1 & 1, b & 1) for b in range(1, 8)]

    def body(p_ref, o_ref, local_sem, send_sems, recv_sems):
        x, y, c = _position()
        me = 4 * x + 2 * y + c
        own = pltpu.make_async_copy(p_ref, o_ref.at[me], local_sem)
        own.start()

        def copy(r, slot):
            dx, dy, dc = masks[r]
            return pltpu.make_async_remote_copy(
                src_ref=p_ref, dst_ref=o_ref.at[slot], send_sem=send_sems.at[r], recv_sem=recv_sems.at[r],
                device_id=(_flip(x, dx), _flip(y, dy), _flip(c, dc)), device_id_type=MESH)

        sent = [copy(r, me) for r in range(7)]
        for cp in sent:
            cp.start()
        for r in range(7):
            dx, dy, dc = masks[r]
            copy(r, 4 * _flip(x, dx) + 2 * _flip(y, dy) + _flip(c, dc)).wait_recv()
        for cp in sent:
            cp.wait_send()
        own.wait()

    return pl.pallas_call(
        body, in_specs=[_hbm()], out_specs=_hbm(),
        out_shape=jax.ShapeDtypeStruct((8, R, C), F32),
        scratch_shapes=[pltpu.SemaphoreType.DMA(()), pltpu.SemaphoreType.DMA((7,)),
                        pltpu.SemaphoreType.DMA((7,))],
        name=name, compiler_params=_cp(),
    )(part)


def _pack(arrs, row_mult):
    n = sum(math.prod(a.shape) for a in arrs)
    rows = -(-n // PACK_COLS)
    rows = -(-rows // row_mult) * row_mult
    pieces = [a.reshape(-1).astype(F32) for a in arrs]
    if rows * PACK_COLS > n:
        pieces.append(jnp.zeros((rows * PACK_COLS - n,), F32))
    return jnp.concatenate(pieces).reshape(rows, PACK_COLS)


def _unpack(buf, shapes):
    lead = buf.shape[:-2]
    flat = buf.reshape(lead + (-1,))
    out, off = [], 0
    for shp in shapes:
        n = math.prod(shp)
        out.append(flat[..., off:off + n].reshape(lead + tuple(shp)))
        off += n
    return out


def _cols_from_chips(g):
    k, L, A, n = g.shape
    return jnp.transpose(g, (1, 2, 0, 3)).reshape(L, A, k * n)


def _rows_from_chips(g):
    k, L, n, B = g.shape
    return jnp.transpose(g, (1, 0, 2, 3)).reshape(L, k * n, B)


def _cols_to_chips(full, k=4):
    L, A, N = full.shape
    return jnp.transpose(full.reshape(L, A, k, N // k), (2, 0, 1, 3))


def _rows_to_chips(full, k=4):
    L, N, B = full.shape
    return jnp.transpose(full.reshape(L, k, N // k, B), (1, 0, 2, 3))


def kernel(x, mixer_norm_w, attn_w_in, attn_b_f, attn_w_out, sgu_w_in, sgu_ln_g, sgu_ln_b, sgu_w_s, sgu_b_s, sgu_w_out, ffn_norm_w, ffn_w_in, ffn_w_out, final_norm_w, loss_target, m_mixer_norm_w, m_attn_w_in, m_attn_b_f, m_attn_w_out, m_sgu_w_in, m_sgu_ln_g, m_sgu_ln_b, m_sgu_w_s, m_sgu_b_s, m_sgu_w_out, m_ffn_norm_w, m_ffn_w_in, m_ffn_w_out, m_final_norm_w, v_mixer_norm_w, v_attn_w_in, v_attn_b_f, v_attn_w_out, v_sgu_w_in, v_sgu_ln_g, v_sgu_ln_b, v_sgu_w_s, v_sgu_b_s, v_sgu_w_out, v_ffn_norm_w, v_ffn_w_in, v_ffn_w_out, v_final_norm_w):
    T, D = x.shape[1], x.shape[2]
    depth = mixer_norm_w.shape[0]
    H = attn_b_f.shape[1]
    P = D // LANES
    assert D % LANES == 0 and D // H == 64 and 2 * P == H and 2 * P <= LANES
    G = sgu_w_s.shape[1]
    W = sgu_w_out.shape[1] * 4
    assert sgu_w_s.shape[2] == LANES and W == G * LANES
    scale = float(D // H) ** -0.5
    f_pad = LANES
    c_idx = lax.axis_index("c").astype(jnp.int32).reshape(1)

    sharded = [attn_w_in, attn_w_out, sgu_w_in, sgu_w_out, ffn_w_in, ffn_w_out, sgu_ln_g, sgu_ln_b]
    sharded_m = [m_attn_w_in, m_attn_w_out, m_sgu_w_in, m_sgu_w_out, m_ffn_w_in, m_ffn_w_out, m_sgu_ln_g, m_sgu_ln_b]
    sharded_v = [v_attn_w_in, v_attn_w_out, v_sgu_w_in, v_sgu_w_out, v_ffn_w_in, v_ffn_w_out, v_sgu_ln_g, v_sgu_ln_b]
    shard_shapes = [a.shape for a in sharded]
    w_pack = _pack(sharded, 512)
    ln_pack = _pack([sgu_ln_g, sgu_ln_b], SUBLANES)
    w_pack_b = w_pack.astype(BF16)
    gat_w, gat_ln = _gather_weights(w_pack_b, ln_pack, "gather_weights")
    my_chip = 2 * lax.axis_index("x") + lax.axis_index("y")
    gat_w = lax.dynamic_update_index_in_dim(gat_w, w_pack_b, my_chip, 0)
    g_ai, g_ao, g_si, g_so, g_fi, g_fo, _, _ = _unpack(gat_w, shard_shapes)
    g_lng, g_lnb = _unpack(gat_ln, [sgu_ln_g.shape, sgu_ln_b.shape])
    w_ai = _cols_from_chips(g_ai)
    w_ai = jnp.pad(w_ai, ((0, 0), (0, 0), (0, 3 * D + f_pad - w_ai.shape[2])))
    w_ao = _rows_from_chips(g_ao)
    w_si = _cols_from_chips(g_si)
    w_so = _rows_from_chips(g_so)
    w_fo = _rows_from_chips(g_fo)
    w_fi5 = g_fi.reshape((2, 2) + g_fi.shape[1:])
    ln_g = jnp.transpose(g_lng, (1, 0, 2)).reshape(sgu_ln_g.shape[0], W)
    ln_b = jnp.transpose(g_lnb, (1, 0, 2)).reshape(sgu_ln_b.shape[0], W)
    w_tril = jnp.tril(sgu_w_s)
    w_tril_b = w_tril.astype(BF16)
    w_tril_tb = jnp.swapaxes(w_tril, 2, 3).astype(BF16)
    sgu_bias = jnp.repeat(jnp.swapaxes(sgu_b_s, 1, 2), LANES, axis=2)
    b_f_pad = jnp.pad(attn_b_f, ((0, 0), (0, LANES - H)))

    xs = x.reshape(T, D)
    saved = []
    for i in range(depth):
        j = i // 2
        h = _rmsnorm_fwd(xs, mixer_norm_w[i], f"mix_norm_{i}")
        rec = {"x_in": xs, "h": h}
        if i % 2 == 0:
            qkv = _mm(h, w_ai[j, :, :3 * D], "nn", BF16, f"attn_qkv_{i}")
            f = _mm(h, w_ai[j, :, 3 * D:], "nn", F32, f"attn_gate_{i}")
            cT, c_cols, c0T = _gate_fwd(f, b_f_pad[j:j + 1], P, f"gate_fwd_{i}")
            o, lseT = _attn_fwd(qkv, cT, P, scale, f"attn_fwd_{i}")
            x_mid = _mm(o, w_ao[j], "nn", F32, f"attn_out_{i}", res=xs)
            rec.update(qkv=qkv, f=f, c0T=c0T, c_cols=c_cols, o=o, lseT=lseT)
        else:
            a = _mm(h, w_si[j], "nn", BF16, f"sgu_in_{i}")
            gated = _sgu_fwd(a, ln_g[j], ln_b[j], w_tril_b[j], sgu_bias[j], f"sgu_fwd_{i}")
            x_mid = _mm(gated, w_so[j], "nn", F32, f"sgu_out_{i}", res=xs)
            rec.update(a=a, gated=gated)
        h2 = _rmsnorm_fwd(x_mid, ffn_norm_w[i], f"ffn_norm_{i}")
        fa, s = _ffn_in_act(h2, w_fi5, i, f"ffn_in_{i}")
        xs = _mm(s, w_fo[i], "nn", F32, f"ffn_out_{i}", res=x_mid)
        rec.update(x_mid=x_mid, h2=h2, fa=fa, s=s)
        saved.append(rec)

    gx, loss_acc, dw_final = _loss_head(xs, final_norm_w, loss_target.reshape(T, D), "loss_head")
    loss = lax.psum(loss_acc[0, 0], ("x", "y", "c"))

    n_attn, n_sgu = attn_w_in.shape[0], sgu_w_in.shape[0]
    d_mixer_norm, d_ffn_norm = [None] * depth, [None] * depth
    d_ai, d_ao, d_bf = [None] * n_attn, [None] * n_attn, [None] * n_attn
    d_si, d_so, d_lng, d_lnb, d_ws, d_bs = ([None] * n_sgu for _ in range(6))
    d_fi, d_fo = [None] * depth, [None] * depth
    for i in reversed(range(depth)):
        j = i // 2
        rec = saved[i]
        d_fo[i] = _mm(rec["s"], gx, "tn", F32, f"ffn_out_wgrad_{i}")
        da = _ffn_out_bwd_act(gx, w_fo[i], rec["fa"], f"ffn_out_bwd_{i}")
        da = da.reshape((4,) + da.shape[2:])
        dh2 = _mm_nt_shards(da, g_fi, i, f"ffn_in_bwd_{i}")
        d_fi[i] = _mm_tn_shards(rec["h2"], da, f"ffn_in_wgrad_{i}")
        gx, dwn = _rmsnorm_bwd(dh2, rec["x_mid"], ffn_norm_w[i], gx, f"ffn_norm_bwd_{i}")
        d_ffn_norm[i] = dwn[0]
        if i % 2 == 0:
            do = _mm(gx, w_ao[j], "nt", BF16, f"attn_out_bwd_{i}")
            d_ao[j] = _mm(rec["o"], gx, "tn", F32, f"attn_out_wgrad_{i}")
            dT = _attn_delta(do, rec["o"], P, f"attn_delta_{i}")
            dq, dk, dv, dc_cols, drowT = _attn_bwd(rec["qkv"], do, rec["lseT"], dT, rec["c0T"], rec["c_cols"],
                                                   P, scale, f"attn_bwd_{i}")
            df, dbf = _gate_bwd(dc_cols, drowT, rec["f"], b_f_pad[j:j + 1], P, f"gate_bwd_{i}")
            d_bf[j] = dbf[0, :H]
            dproj = jnp.concatenate([dq, dk, dv, df.astype(BF16)], axis=1)
            dh = _mm(dproj, w_ai[j], "nt", F32, f"attn_in_bwd_{i}")
            d_ai[j] = _mm(rec["h"], dproj, "tn", F32, f"attn_in_wgrad_{i}")[:, :3 * D + H]
        else:
            dgated = _mm(gx, w_so[j], "nt", F32, f"sgu_out_bwd_{i}")
            d_so[j] = _mm(rec["gated"], gx, "tn", F32, f"sgu_out_wgrad_{i}")
            da_s, dws, dbias, dlng, dlnb = _sgu_bwd(rec["a"], dgated, ln_g[j], ln_b[j], w_tril_b[j],
                                                    w_tril_tb[j], sgu_bias[j], f"sgu_bwd_{i}")
            d_ws[j] = jnp.tril(dws)
            d_bs[j] = jnp.sum(dbias.reshape(LANES, G, LANES), axis=2).T
            d_lng[j], d_lnb[j] = dlng[0], dlnb[0]
            dh = _mm(da_s, w_si[j], "nt", F32, f"sgu_in_bwd_{i}")
            d_si[j] = _mm(rec["h"], da_s, "tn", F32, f"sgu_in_wgrad_{i}")
        gx, dwn = _rmsnorm_bwd(dh, rec["x_in"], mixer_norm_w[i], gx, f"mix_norm_bwd_{i}")
        d_mixer_norm[i] = dwn[0]
    grad_x = gx.reshape(x.shape)

    full_grads = [
        _cols_to_chips(jnp.stack(d_ai)), _rows_to_chips(jnp.stack(d_ao)),
        _cols_to_chips(jnp.stack(d_si)), _rows_to_chips(jnp.stack(d_so)),
        jnp.stack(d_fi, axis=1), _rows_to_chips(jnp.stack(d_fo)),
        jnp.transpose(jnp.stack(d_lng).reshape(n_sgu, 4, W // 4), (1, 0, 2)),
        jnp.transpose(jnp.stack(d_lnb).reshape(n_sgu, 4, W // 4), (1, 0, 2)),
    ]
    g_all = jnp.stack([_pack([fg[k] for fg in full_grads], 512) for k in range(4)])
    from_sibling = _sibling_halves(g_all, "grad_sibling_halves")
    pair = _pair_sum(g_all, from_sibling, c_idx, "grad_pair_sum")
    from_chips = _chip_exchange(pair, "grad_chip_exchange")
    my_half = _sum_parts(from_chips, "grad_chip_sum")
    sibling_half = _swap_with_sibling(my_half, "grad_swap_halves")
    g_shard, d_shard, m_shard, v_shard = _adamw_halves(
        my_half, sibling_half, c_idx, w_pack, _pack(sharded_m, 512), _pack(sharded_v, 512), "adamw_sharded")
    g_sh = _unpack(g_shard, shard_shapes)
    d_sh = _unpack(d_shard, shard_shapes)
    m_sh = _unpack(m_shard, shard_shapes)
    v_sh = _unpack(v_shard, shard_shapes)

    repl = [mixer_norm_w, attn_b_f, sgu_w_s, sgu_b_s, ffn_norm_w, final_norm_w]
    repl_m = [m_mixer_norm_w, m_attn_b_f, m_sgu_w_s, m_sgu_b_s, m_ffn_norm_w, m_final_norm_w]
    repl_v = [v_mixer_norm_w, v_attn_b_f, v_sgu_w_s, v_sgu_b_s, v_ffn_norm_w, v_final_norm_w]
    repl_shapes = [a.shape for a in repl]
    repl_grads = [jnp.stack(d_mixer_norm), jnp.stack(d_bf), jnp.stack(d_ws), jnp.stack(d_bs),
                  jnp.stack(d_ffn_norm), dw_final[0]]
    parts = _gather_all(_pack(repl_grads, SUBLANES), "grad_gather_replicated")
    g_rep, d_rep, m_rep, v_rep = _adamw_sum(parts, _pack(repl, SUBLANES), _pack(repl_m, SUBLANES),
                                            _pack(repl_v, SUBLANES), "adamw_replicated")
    g_r = _unpack(g_rep, repl_shapes)
    d_r = _unpack(d_rep, repl_shapes)
    m_r = _unpack(m_rep, repl_shapes)
    v_r = _unpack(v_rep, repl_shapes)

    def ordered(sh, rp):
        ai, ao, si, so, fi, fo, lng, lnb = sh
        mn, bf, ws, bs, fn, fin = rp
        return [mn, ai, bf, ao, si, lng, lnb, ws, bs, so, fn, fi, fo, fin]

    return (loss, grad_x, *ordered(g_sh, g_r), *ordered(d_sh, d_r), *ordered(m_sh, m_r), *ordered(v_sh, v_r))
```

```python
import functools
import math

import jax
import jax.numpy as jnp
from jax import lax
from jax.experimental import pallas as pl
from jax.experimental.pallas import tpu as pltpu

F32 = jnp.float32
BF16 = jnp.bfloat16
NORM_EPS = 1e-6
LN_EPS = 1e-5
ADAM_LR = 0.001
ADAM_B1 = 0.9
ADAM_B2 = 0.999
ADAM_EPS = 1e-08
ADAM_WD = 0.01
ADAM_STEP = 10

LANES = 128
SUBLANES = 8
PACK_COLS = 1024
VMEM_LIMIT = 56 * 1024 * 1024
NEG_BIG = -1e30
MESH = pl.DeviceIdType.MESH


def _cp():
    return pltpu.CompilerParams(vmem_limit_bytes=VMEM_LIMIT)


def _tile(n, cap, mult):
    best = None
    d = mult
    while d <= min(n, cap):
        if n % d == 0:
            best = d
        d += mult
    return n if best is None else best


def _row_tile(rows, cols):
    cap = max(16, (512 * 1024 // cols) // 16 * 16)
    return _tile(rows, cap, 16)


def _hbm():
    return pl.BlockSpec(memory_space=pltpu.HBM)


def _rmsnorm_fwd(x, w, name):
    T, D = x.shape
    tm = _tile(T, 512, SUBLANES)

    def body(x_ref, w_ref, h_ref):
        xf = x_ref[...]
        r = lax.rsqrt(jnp.mean(xf * xf, axis=-1, keepdims=True) + NORM_EPS)
        h_ref[...] = (xf * r * w_ref[...]).astype(BF16)

    return pl.pallas_call(
        body, grid=(T // tm,),
        in_specs=[pl.BlockSpec((tm, D), lambda i: (i, 0)), pl.BlockSpec((1, D), lambda i: (0, 0))],
        out_specs=pl.BlockSpec((tm, D), lambda i: (i, 0)),
        out_shape=jax.ShapeDtypeStruct((T, D), BF16), name=name, compiler_params=_cp(),
    )(x, w.reshape(1, D))


def _rmsnorm_bwd(dh, x, w, dres, name):
    T, D = x.shape
    tm = _tile(T, 512, SUBLANES)

    def body(dh_ref, x_ref, w_ref, dres_ref, dx_ref, dw_ref):
        @pl.when(pl.program_id(0) == 0)
        def _():
            dw_ref[...] = jnp.zeros_like(dw_ref)

        xf = x_ref[...]
        r = lax.rsqrt(jnp.mean(xf * xf, axis=-1, keepdims=True) + NORM_EPS)
        xhat = xf * r
        dhv = dh_ref[...]
        dxhat = dhv * w_ref[...]
        dx_ref[...] = dres_ref[...] + r * (dxhat - xhat * jnp.mean(dxhat * xhat, axis=-1, keepdims=True))
        dw_ref[...] += jnp.sum(dhv * xhat, axis=0, keepdims=True)

    row = pl.BlockSpec((tm, D), lambda i: (i, 0))
    return pl.pallas_call(
        body, grid=(T // tm,),
        in_specs=[row, row, pl.BlockSpec((1, D), lambda i: (0, 0)), row],
        out_specs=[row, pl.BlockSpec((SUBLANES, D), lambda i: (0, 0))],
        out_shape=[jax.ShapeDtypeStruct((T, D), F32), jax.ShapeDtypeStruct((SUBLANES, D), F32)],
        name=name, compiler_params=_cp(),
    )(dh, x, w.reshape(1, D), dres)


def _mm(a, b, mode, out_dtype, name, res=None):
    if mode == "tn":
        kt, M = a.shape
        N = b.shape[1]
        tm = _tile(M, 1408, LANES)
        tn = _tile(N, 1408, LANES)
        tk = _tile(kt, 512, 16)

        def body(a_ref, b_ref, o_ref):
            @pl.when(pl.program_id(2) == 0)
            def _():
                o_ref[...] = jnp.zeros_like(o_ref)

            o_ref[...] += lax.dot_general(
                a_ref[...].astype(BF16), b_ref[...].astype(BF16), (((0,), (0,)), ((), ())),
                preferred_element_type=F32)

        return pl.pallas_call(
            body, grid=(M // tm, N // tn, kt // tk),
            in_specs=[pl.BlockSpec((tk, tm), lambda i, j, k: (k, i)),
                      pl.BlockSpec((tk, tn), lambda i, j, k: (k, j))],
            out_specs=pl.BlockSpec((tm, tn), lambda i, j, k: (i, j)),
            out_shape=jax.ShapeDtypeStruct((M, N), F32), name=name, compiler_params=_cp(),
        )(a, b)

    M, K = a.shape
    N = b.shape[1] if mode == "nn" else b.shape[0]
    tm = _tile(M, 512, 16)
    cap = max(LANES, min(1408, ((6 << 20) // (2 * K)) // LANES * LANES))
    tn = _tile(N, cap, LANES)
    dims = (((1,), (0,)), ((), ())) if mode == "nn" else (((1,), (1,)), ((), ()))

    def body(*refs):
        if res is None:
            a_ref, b_ref, o_ref = refs
        else:
            a_ref, b_ref, r_ref, o_ref = refs
        acc = lax.dot_general(a_ref[...].astype(BF16), b_ref[...].astype(BF16), dims,
                              preferred_element_type=F32)
        if res is not None:
            acc = acc + r_ref[...]
        o_ref[...] = acc.astype(out_dtype)

    b_spec = (pl.BlockSpec((K, tn), lambda i, j: (0, j)) if mode == "nn"
              else pl.BlockSpec((tn, K), lambda i, j: (j, 0)))
    in_specs = [pl.BlockSpec((tm, K), lambda i, j: (i, 0)), b_spec]
    args = [a, b]
    if res is not None:
        in_specs.append(pl.BlockSpec((tm, tn), lambda i, j: (i, j)))
        args.append(res)
    return pl.pallas_call(
        body, grid=(M // tm, N // tn), in_specs=in_specs,
        out_specs=pl.BlockSpec((tm, tn), lambda i, j: (i, j)),
        out_shape=jax.ShapeDtypeStruct((M, N), out_dtype), name=name, compiler_params=_cp(),
    )(*args)


def _ffn_in_act(h, w5, layer, name):
    T, D = h.shape
    n = w5.shape[-1]
    tm = _tile(T, 256, 16)

    def body(h_ref, w_ref, a_ref, s_ref):
        hv = h_ref[...]
        g = jnp.dot(hv, w_ref[0, 0, 0], preferred_element_type=F32)
        u = jnp.dot(hv, w_ref[1, 0, 0], preferred_element_type=F32)
        a_ref[0, 0] = g.astype(BF16)
        a_ref[1, 0] = u.astype(BF16)
        s_ref[...] = (g * jax.nn.sigmoid(g) * u).astype(BF16)

    return pl.pallas_call(
        body, grid=(2, T // tm),
        in_specs=[pl.BlockSpec((tm, D), lambda j, i: (i, 0)),
                  pl.BlockSpec((2, 1, 1, D, n), lambda j, i: (0, j, layer, 0, 0))],
        out_specs=[pl.BlockSpec((2, 1, tm, n), lambda j, i: (0, j, i, 0)), pl.BlockSpec((tm, n), lambda j, i: (i, j))],
        out_shape=[jax.ShapeDtypeStruct((2, 2, T, n), BF16), jax.ShapeDtypeStruct((T, 2 * n), BF16)],
        name=name, compiler_params=_cp(),
    )(h, w5)


def _ffn_out_bwd_act(gx, w_out, a4, name):
    T, D = gx.shape
    n = a4.shape[-1]
    tm = _tile(T, 256, 16)

    def body(gx_ref, w_ref, a_ref, da_ref):
        ds = _nt(gx_ref[...].astype(BF16), w_ref[...])
        g = a_ref[0, 0].astype(F32)
        u = a_ref[1, 0].astype(F32)
        sg = jax.nn.sigmoid(g)
        da_ref[0, 0] = (ds * u * (sg * (1.0 + g * (1.0 - sg)))).astype(BF16)
        da_ref[1, 0] = (ds * (g * sg)).astype(BF16)

    blk = pl.BlockSpec((2, 1, tm, n), lambda j, i: (0, j, i, 0))
    return pl.pallas_call(
        body, grid=(2, T // tm),
        in_specs=[pl.BlockSpec((tm, D), lambda j, i: (i, 0)), pl.BlockSpec((n, D), lambda j, i: (j, 0)), blk],
        out_specs=blk,
        out_shape=jax.ShapeDtypeStruct((2, 2, T, n), BF16), name=name, compiler_params=_cp(),
    )(gx, w_out, a4)


def _mm_nt_shards(a4, w4, layer, name):
    K, T, n = a4.shape
    D = w4.shape[2]
    tm = _tile(T, 512, 16)
    tn = _tile(D, 512, LANES)

    def body(a_ref, w_ref, o_ref):
        acc = _nt(a_ref[0], w_ref[0, 0])
        for k in range(1, K):
            acc = acc + _nt(a_ref[k], w_ref[k, 0])
        o_ref[...] = acc

    return pl.pallas_call(
        body, grid=(T // tm, D // tn),
        in_specs=[pl.BlockSpec((K, tm, n), lambda i, j: (0, i, 0)),
                  pl.BlockSpec((K, 1, tn, n), lambda i, j: (0, layer, j, 0))],
        out_specs=pl.BlockSpec((tm, tn), lambda i, j: (i, j)),
        out_shape=jax.ShapeDtypeStruct((T, D), F32), name=name, compiler_params=_cp(),
    )(a4, w4)


def _mm_tn_shards(h, a4, name):
    K, T, n = a4.shape
    D = h.shape[1]
    tk = _tile(T, 512, 16)

    def body(h_ref, a_ref, o_ref):
        @pl.when(pl.program_id(1) == 0)
        def _():
            o_ref[...] = jnp.zeros_like(o_ref)

        o_ref[0] += lax.dot_general(h_ref[...], a_ref[0], (((0,), (0,)), ((), ())), preferred_element_type=F32)

    return pl.pallas_call(
        body, grid=(K, T // tk),
        in_specs=[pl.BlockSpec((tk, D), lambda k, t: (t, 0)), pl.BlockSpec((1, tk, n), lambda k, t: (k, t, 0))],
        out_specs=pl.BlockSpec((1, D, n), lambda k, t: (k, 0, 0)),
        out_shape=jax.ShapeDtypeStruct((K, D, n), F32), name=name, compiler_params=_cp(),
    )(h, a4)


def _loss_head(x, w, tgt, name):
    T, D = x.shape
    tm = _tile(T, 512, SUBLANES)

    def body(x_ref, w_ref, t_ref, dx_ref, loss_ref, dw_ref):
        @pl.when(pl.program_id(0) == 0)
        def _():
            loss_ref[...] = jnp.zeros_like(loss_ref)
            dw_ref[...] = jnp.zeros_like(dw_ref)

        xf = x_ref[...]
        wv = w_ref[...]
        r = lax.rsqrt(jnp.mean(xf * xf, axis=-1, keepdims=True) + NORM_EPS)
        xhat = xf * r
        err = xhat * wv - t_ref[...]
        per_tok = jnp.mean(err * err, axis=-1, keepdims=True)
        loss_ref[...] += 0.5 * jnp.sum(per_tok, axis=0, keepdims=True)
        dy = err * (1.0 / D)
        dxhat = dy * wv
        dx_ref[...] = r * (dxhat - xhat * jnp.mean(dxhat * xhat, axis=-1, keepdims=True))
        dw_ref[...] += jnp.sum(dy * xhat, axis=0, keepdims=True)

    row = pl.BlockSpec((tm, D), lambda i: (i, 0))
    return pl.pallas_call(
        body, grid=(T // tm,),
        in_specs=[row, pl.BlockSpec((1, D), lambda i: (0, 0)), row],
        out_specs=[row, pl.BlockSpec((SUBLANES, LANES), lambda i: (0, 0)),
                   pl.BlockSpec((SUBLANES, D), lambda i: (0, 0))],
        out_shape=[jax.ShapeDtypeStruct((T, D), F32), jax.ShapeDtypeStruct((SUBLANES, LANES), F32),
                   jax.ShapeDtypeStruct((SUBLANES, D), F32)],
        name=name, compiler_params=_cp(),
    )(x, w.reshape(1, D), tgt)


def _split3(v):
    hi = v.astype(BF16)
    r1 = v - hi.astype(F32)
    mid = r1.astype(BF16)
    lo = (r1 - mid.astype(F32)).astype(BF16)
    return hi, mid, lo


def _tri_dot(tri, v):
    out = None
    for piece in _split3(v):
        t = jnp.dot(tri, piece, preferred_element_type=F32)
        out = t if out is None else out + t
    return out


def _q_block(T):
    return _tile(T, 256, LANES)


def _gate_fwd(f, b_f, P, name):
    T = f.shape[0]
    tb = _q_block(T)

    def body(f_ref, b_ref, ct_ref, cc_ref, c0_ref, carry):
        @pl.when(pl.program_id(0) == 0)
        def _():
            carry[...] = jnp.zeros_like(carry)

        z = f_ref[...] + b_ref[...]
        logf = jnp.minimum(z, 0.0) - jnp.log(1.0 + jnp.exp(-jnp.abs(z)))
        row = lax.broadcasted_iota(jnp.int32, (tb, tb), 0)
        col = lax.broadcasted_iota(jnp.int32, (tb, tb), 1)
        tri = (col <= row).astype(BF16)
        c = _tri_dot(tri, logf) + carry[0:1, :]
        carry[...] = jnp.broadcast_to(c[tb - 1:tb, :], carry.shape)
        first = jnp.broadcast_to(c[0:1, :], c.shape)
        for p in range(P):
            shifted = c if p == 0 else pltpu.roll(c, LANES - 2 * p, 1)
            cc_ref[p] = shifted
            ct_ref[p] = shifted.T[0:SUBLANES, :]
            c0_ref[p] = (first if p == 0 else pltpu.roll(first, LANES - 2 * p, 1)).T[0:SUBLANES, :]

    rows = pl.BlockSpec((P, SUBLANES, tb), lambda i: (0, 0, i))
    return pl.pallas_call(
        body, grid=(T // tb,),
        in_specs=[pl.BlockSpec((tb, LANES), lambda i: (i, 0)), pl.BlockSpec((1, LANES), lambda i: (0, 0))],
        out_specs=[rows, pl.BlockSpec((P, tb, LANES), lambda i: (0, i, 0)), rows],
        out_shape=[jax.ShapeDtypeStruct((P, SUBLANES, T), F32), jax.ShapeDtypeStruct((P, T, LANES), F32),
                   jax.ShapeDtypeStruct((P, SUBLANES, T), F32)],
        scratch_shapes=[pltpu.VMEM((SUBLANES, LANES), F32)],
        name=name, compiler_params=_cp(),
    )(f, b_f)


def _gate_bwd(dc_cols, drowT, f, b_f, P, name):
    T = f.shape[0]
    tb = _tile(T, 256, LANES)
    nb = T // tb

    def body(dc_ref, dr_ref, f_ref, b_ref, df_ref, db_ref, carry):
        @pl.when(pl.program_id(0) == 0)
        def _():
            carry[...] = jnp.zeros_like(carry)
            db_ref[...] = jnp.zeros_like(db_ref)

        lane = lax.broadcasted_iota(jnp.int32, (tb, LANES), 1)
        dc = jnp.zeros((tb, LANES), F32)
        for p in range(P):
            rows = jnp.concatenate([dr_ref[p], jnp.zeros((LANES - SUBLANES, tb), F32)], axis=0)
            part = jnp.where(lane < 2, dc_ref[p] + rows.T, 0.0)
            dc = dc + (part if p == 0 else pltpu.roll(part, 2 * p, 1))
        row = lax.broadcasted_iota(jnp.int32, (tb, tb), 0)
        col = lax.broadcasted_iota(jnp.int32, (tb, tb), 1)
        tri = (col >= row).astype(BF16)
        dlogf = _tri_dot(tri, dc) + carry[0:1, :]
        carry[...] = jnp.broadcast_to(dlogf[0:1, :], carry.shape)
        z = f_ref[...] + b_ref[...]
        df = jnp.where(lane < 2 * P, dlogf * jax.nn.sigmoid(-z), 0.0)
        df_ref[...] = df
        db_ref[...] += jnp.sum(df, axis=0, keepdims=True)

    return pl.pallas_call(
        body, grid=(nb,),
        in_specs=[pl.BlockSpec((P, tb, LANES), lambda i: (0, nb - 1 - i, 0)),
                  pl.BlockSpec((P, SUBLANES, tb), lambda i: (0, 0, nb - 1 - i)),
                  pl.BlockSpec((tb, LANES), lambda i: (nb - 1 - i, 0)),
                  pl.BlockSpec((1, LANES), lambda i: (0, 0))],
        out_specs=[pl.BlockSpec((tb, LANES), lambda i: (nb - 1 - i, 0)),
                   pl.BlockSpec((SUBLANES, LANES), lambda i: (0, 0))],
        out_shape=[jax.ShapeDtypeStruct((T, LANES), F32), jax.ShapeDtypeStruct((SUBLANES, LANES), F32)],
        scratch_shapes=[pltpu.VMEM((SUBLANES, LANES), F32)],
        name=name, compiler_params=_cp(),
    )(dc_cols, drowT, f, b_f)


def _nt(a, b):
    return lax.dot_general(a, b, (((1,), (1,)), ((), ())), preferred_element_type=F32)


def _attn_fwd(qkv, cT, P, scale, name):
    T = qkv.shape[0]
    tq = _q_block(T)
    tw = _tile(T, 4 * tq, 2 * tq)
    cw = tw // 2
    assert cw % tq == 0, "the sequence must split into chunks of whole query blocks"
    nq = T // tq

    def body(q_ref, k_ref, v_ref, c_ref, o_ref, lse_ref, s_scr):
        i = pl.program_id(1)
        lane = lax.broadcasted_iota(jnp.int32, (1, LANES), 1)
        q = (q_ref[...].astype(F32) * scale).astype(BF16)
        q_heads = (jnp.where(lane < 64, q, jnp.zeros_like(q)), jnp.where(lane >= 64, q, jnp.zeros_like(q)))
        c0 = c_ref[0, :, pl.ds(pl.multiple_of(i * tq, tq), LANES)][:, 0:1]

        def scores(start, width, a):
            bias = c0 - c_ref[0, :, pl.ds(start, width)]
            return _nt(q_heads[a], k_ref[pl.ds(start, width), :]) + bias[a:a + 1, :]

        def softmax_pv(start, width, s_of, carry):
            v = v_ref[pl.ds(start, width), :]
            one = jnp.ones_like(v)
            v_heads = (jnp.where(lane < 64, v, one), jnp.where(lane >= 64, v, one))
            new = []
            for a in range(2):
                m, acc = carry[a]
                s = s_of(a)
                m_new = jnp.maximum(m, jnp.max(s, axis=1, keepdims=True))
                p = jnp.exp(s - m_new)
                acc = jnp.exp(m - m_new) * acc + jnp.dot(p.astype(BF16), v_heads[a], preferred_element_type=F32)
                new.append((m_new, acc))
            return tuple(new)

        def fill(start, buf):
            for a in range(2):
                s_scr[2 * buf + a] = scores(start, cw, a)

        def wide(j, carry):
            base = pl.multiple_of(j * tw, tw)
            fill(base + cw, 1)
            carry = softmax_pv(base, cw, lambda a: s_scr[a], carry)
            fill(base + tw, 0)
            return softmax_pv(base + cw, cw, lambda a: s_scr[2 + a], carry)

        init = tuple((jnp.full((tq, 1), NEG_BIG, F32), jnp.zeros((tq, LANES), F32)) for _ in range(2))
        n_wide = (i * tq) // tw
        fill(0, 0)
        carry = lax.fori_loop(0, n_wide, wide, init)

        base = pl.multiple_of(n_wide * tw, tw)
        ahead = i * tq - base
        col_minus_row = (lax.broadcasted_iota(jnp.int32, (tq, cw), 1)
                         - lax.broadcasted_iota(jnp.int32, (tq, cw), 0))

        def causal(buf, first_key):
            return lambda a: jnp.where(col_minus_row <= ahead - first_key, s_scr[2 * buf + a], NEG_BIG)

        def one_chunk(cr):
            return softmax_pv(base, cw, causal(0, 0), cr)

        def two_chunks(cr):
            fill(base + cw, 1)
            cr = softmax_pv(base, cw, causal(0, 0), cr)
            return softmax_pv(base + cw, cw, causal(1, cw), cr)

        (m0, a0), (m1, a1) = lax.cond(ahead >= cw, two_chunks, one_chunk, carry)
        sums = jnp.where(lane < 64, pltpu.roll(a0, 64, 1), pltpu.roll(a1, 64, 1))
        o_ref[...] = (jnp.where(lane < 64, a0, a1) / sums).astype(BF16)
        l0, l1 = a0[:, 64:65], a1[:, 0:1]
        lse = jnp.where(lane == 0, m0 + jnp.log(l0), jnp.where(lane == 1, m1 + jnp.log(l1), 0.0))
        lse_ref[0] = lse.T[0:SUBLANES, :]

    return pl.pallas_call(
        body, grid=(P, nq),
        in_specs=[pl.BlockSpec((tq, LANES), lambda p, i: (i, p)),
                  pl.BlockSpec((T, LANES), lambda p, i: (0, P + p)),
                  pl.BlockSpec((T, LANES), lambda p, i: (0, 2 * P + p)),
                  pl.BlockSpec((1, SUBLANES, T), lambda p, i: (p, 0, 0))],
        out_specs=[pl.BlockSpec((tq, LANES), lambda p, i: (i, p)),
                   pl.BlockSpec((1, SUBLANES, tq), lambda p, i: (p, 0, i))],
        out_shape=[jax.ShapeDtypeStruct((T, LANES * P), BF16), jax.ShapeDtypeStruct((P, SUBLANES, T), F32)],
        scratch_shapes=[pltpu.VMEM((4, tq, cw), F32)],
        name=name, compiler_params=_cp(),
    )(qkv, qkv, qkv, cT)


def _attn_delta(do, o, P, name):
    T, D = o.shape
    tb = _tile(T, 256, LANES)

    def body(do_ref, o_ref, d_ref):
        lane = lax.broadcasted_iota(jnp.int32, (1, LANES), 1)
        for p in range(P):
            cols = slice(p * LANES, (p + 1) * LANES)
            prod = do_ref[:, cols].astype(F32) * o_ref[:, cols].astype(F32)
            d0 = jnp.sum(jnp.where(lane < 64, prod, 0.0), axis=1, keepdims=True)
            d1 = jnp.sum(jnp.where(lane >= 64, prod, 0.0), axis=1, keepdims=True)
            both = jnp.where(lane == 0, d0, jnp.where(lane == 1, d1, 0.0))
            d_ref[p] = both.T[0:SUBLANES, :]

    return pl.pallas_call(
        body, grid=(T // tb,),
        in_specs=[pl.BlockSpec((tb, D), lambda i: (i, 0)), pl.BlockSpec((tb, D), lambda i: (i, 0))],
        out_specs=pl.BlockSpec((P, SUBLANES, tb), lambda i: (0, 0, i)),
        out_shape=jax.ShapeDtypeStruct((P, SUBLANES, T), F32), name=name, compiler_params=_cp(),
    )(do, o)


def _attn_bwd(qkv, do, lseT, dT, c0T, c_cols, P, scale, name):
    T = qkv.shape[0]
    tq = _q_block(T)
    tw = _tile(T, 4 * tq, 2 * tq)
    cw = tw // 2
    assert cw % tq == 0, "the sequence must split into chunks of whole query blocks"
    nq = T // tq

    def body(q_ref, do_ref, k_ref, v_ref, lse_ref, d_ref, c0_ref, cc_ref,
             dq_ref, dk_ref, dv_ref, dc_ref, drow_ref, dq_acc0, dq_acc1):
        j = pl.program_id(1)

        @pl.when(j == 0)
        def _():
            dq_acc0[...] = jnp.zeros_like(dq_acc0)
            dq_acc1[...] = jnp.zeros_like(dq_acc1)

        lane = lax.broadcasted_iota(jnp.int32, (1, LANES), 1)
        in_head = (lane < 64, lane >= 64)
        k = k_ref[...]
        v = v_ref[...]
        zero = jnp.zeros_like(k)
        one = jnp.ones_like(k)
        k_heads = tuple(jnp.where(h, k, zero) for h in in_head)
        v_heads = tuple(jnp.where(h, v, zero) for h in in_head)
        k_ones = tuple(jnp.where(h, k, one) for h in in_head)
        cc = cc_ref[0]
        c_first = (cc[0:1, 0:1], cc[0:1, 1:2])
        c_rel = (cc[:, 0:1] - c_first[0], cc[:, 1:2] - c_first[1])
        dq_accs = (dq_acc0, dq_acc1)

        def block(start, width, carry, first_query=None):
            q = (q_ref[pl.ds(start, width), :].astype(F32) * scale).astype(BF16)
            q_one = jnp.ones_like(q)
            dov = do_ref[pl.ds(start, width), :]
            lse = lse_ref[0, :, pl.ds(start, width)]
            dlt = d_ref[0, :, pl.ds(start, width)]
            c0 = c0_ref[0, :, pl.ds(start, width)]
            new = []
            for a in range(2):
                dk_a, dv_a = carry[a]
                rowv = lse[a:a + 1, :] + (c_first[a] - c0[a:a + 1, :])
                st = _nt(k_heads[a], q)
                pt = jnp.exp((st - c_rel[a]) - rowv)
                if first_query is not None:
                    row = lax.broadcasted_iota(jnp.int32, (tq, width), 0)
                    col = lax.broadcasted_iota(jnp.int32, (tq, width), 1)
                    pt = jnp.where(col - row >= first_query, pt, 0.0)
                dpt = _nt(v_heads[a], dov)
                dst_b = (pt * (dpt - dlt[a:a + 1, :])).astype(BF16)
                dv_a = dv_a + jnp.dot(pt.astype(BF16), dov, preferred_element_type=F32)
                dk_a = dk_a + jnp.dot(dst_b, jnp.where(in_head[a], q, q_one), preferred_element_type=F32)
                dq_accs[a][pl.ds(start, width), :] += lax.dot_general(
                    dst_b, k_ones[a], (((0,), (0,)), ((), ())), preferred_element_type=F32)
                new.append((dk_a, dv_a))
            return tuple(new)

        init = tuple((jnp.zeros((tq, LANES), F32), jnp.zeros((tq, LANES), F32)) for _ in range(2))
        first_key = j * tq
        diag = pl.multiple_of((first_key // cw) * cw, cw)
        carry = block(diag, cw, init, first_key - diag)
        first_wide = first_key // tw + 1
        carry = lax.cond(
            diag + cw < first_wide * tw,
            lambda cr: block(pl.multiple_of(diag + cw, cw), cw, cr), lambda cr: cr, carry)
        (dk0, dv0), (dk1, dv1) = lax.fori_loop(
            first_wide, T // tw, lambda i, cr: block(pl.multiple_of(i * tw, tw), tw, cr), carry)
        dk_ref[...] = jnp.where(lane < 64, dk0, dk1).astype(BF16)
        dv_ref[...] = jnp.where(lane < 64, dv0, dv1).astype(BF16)
        dc_ref[0] = jnp.where(lane == 0, -dk0[:, 64:65], jnp.where(lane == 1, -dk1[:, 0:1], 0.0))

        @pl.when(j == nq - 1)
        def _():
            def finish(i, _):
                rows = pl.ds(pl.multiple_of(i * tq, tq), tq)
                a0 = dq_acc0[rows, :]
                a1 = dq_acc1[rows, :]
                dq_ref[rows, :] = (jnp.where(lane < 64, a0, a1) * scale).astype(BF16)
                sums = jnp.where(lane == 0, a0[:, 64:65], jnp.where(lane == 1, a1[:, 0:1], 0.0))
                drow_ref[0, :, rows] = sums.T[0:SUBLANES, :]
                return 0

            lax.fori_loop(0, nq, finish, 0)

    full = lambda col: pl.BlockSpec((T, LANES), lambda p, j: (0, col(p)))
    blk = lambda col: pl.BlockSpec((tq, LANES), lambda p, j: (j, col(p)))
    rows = pl.BlockSpec((1, SUBLANES, T), lambda p, j: (p, 0, 0))
    cols = pl.BlockSpec((1, tq, LANES), lambda p, j: (p, j, 0))
    D = LANES * P
    return pl.pallas_call(
        body, grid=(P, nq),
        in_specs=[full(lambda p: p), full(lambda p: p), blk(lambda p: P + p), blk(lambda p: 2 * P + p),
                  rows, rows, rows, cols],
        out_specs=[full(lambda p: p), blk(lambda p: p), blk(lambda p: p), cols, rows],
        out_shape=[jax.ShapeDtypeStruct((T, D), BF16), jax.ShapeDtypeStruct((T, D), BF16),
                   jax.ShapeDtypeStruct((T, D), BF16), jax.ShapeDtypeStruct((P, T, LANES), F32),
                   jax.ShapeDtypeStruct((P, SUBLANES, T), F32)],
        scratch_shapes=[pltpu.VMEM((T, LANES), F32), pltpu.VMEM((T, LANES), F32)],
        name=name, compiler_params=_cp(),
    )(qkv, do, qkv, qkv, lseT, dT, c0T, c_cols)


_SQRT_HALF = 0.7071067811865476
_INV_SQRT_2PI = 0.3989422804014327


def _gelu(v):
    return 0.5 * v * (1.0 + lax.erf(v * _SQRT_HALF))


def _gelu_grad(v):
    return 0.5 * (1.0 + lax.erf(v * _SQRT_HALF)) + v * (_INV_SQRT_2PI * jnp.exp(-0.5 * v * v))


def _sgu_fwd(a, ln_g, ln_b, w_tril, bias, name):
    T, W2 = a.shape
    W = W2 // 2
    G = w_tril.shape[0]
    tb = _tile(T, 256, LANES)

    def body(a_ref, g_ref, b_ref, w_ref, bias_ref, out_ref):
        zu = _gelu(a_ref[:, :W].astype(F32))
        zv = _gelu(a_ref[:, W:].astype(F32))
        mu = jnp.mean(zv, axis=-1, keepdims=True)
        d = zv - mu
        rstd = lax.rsqrt(jnp.mean(d * d, axis=-1, keepdims=True) + LN_EPS)
        vn = (d * rstd * g_ref[...] + b_ref[...]).astype(BF16)
        for c in range(tb // LANES):
            rs = slice(c * LANES, (c + 1) * LANES)
            for g in range(G):
                cs = slice(g * LANES, (g + 1) * LANES)
                mixed = jnp.dot(w_ref[g], vn[rs, cs], preferred_element_type=F32) + bias_ref[:, cs]
                out_ref[rs, cs] = (zu[rs, cs] * mixed).astype(BF16)

    return pl.pallas_call(
        body, grid=(T // tb,),
        in_specs=[pl.BlockSpec((tb, W2), lambda i: (i, 0)), pl.BlockSpec((1, W), lambda i: (0, 0)),
                  pl.BlockSpec((1, W), lambda i: (0, 0)), pl.BlockSpec((G, LANES, LANES), lambda i: (0, 0, 0)),
                  pl.BlockSpec((LANES, W), lambda i: (0, 0))],
        out_specs=pl.BlockSpec((tb, W), lambda i: (i, 0)),
        out_shape=jax.ShapeDtypeStruct((T, W), BF16), name=name, compiler_params=_cp(),
    )(a, ln_g.reshape(1, W), ln_b.reshape(1, W), w_tril, bias)


def _sgu_bwd(a, dgated, ln_g, ln_b, w_tril, w_tril_t, bias, name):
    T, W2 = a.shape
    W = W2 // 2
    G = w_tril.shape[0]
    tb = _tile(T, 256, LANES)

    def body(a_ref, dg_ref, g_ref, b_ref, w_ref, wt_ref, bias_ref,
             da_ref, dws_ref, dbias_ref, dlng_ref, dlnb_ref, dvn_ref):
        @pl.when(pl.program_id(0) == 0)
        def _():
            dws_ref[...] = jnp.zeros_like(dws_ref)
            dbias_ref[...] = jnp.zeros_like(dbias_ref)
            dlng_ref[...] = jnp.zeros_like(dlng_ref)
            dlnb_ref[...] = jnp.zeros_like(dlnb_ref)

        up = a_ref[:, :W].astype(F32)
        vp = a_ref[:, W:].astype(F32)
        zu = _gelu(up)
        zv = _gelu(vp)
        mu = jnp.mean(zv, axis=-1, keepdims=True)
        d = zv - mu
        rstd = lax.rsqrt(jnp.mean(d * d, axis=-1, keepdims=True) + LN_EPS)
        vhat = d * rstd
        gam = g_ref[...]
        vn = (vhat * gam + b_ref[...]).astype(BF16)
        dgated = dg_ref[...]
        for c in range(tb // LANES):
            rs = slice(c * LANES, (c + 1) * LANES)
            for g in range(G):
                cs = slice(g * LANES, (g + 1) * LANES)
                vb = vn[rs, cs]
                mixed = jnp.dot(w_ref[g], vb, preferred_element_type=F32) + bias_ref[:, cs]
                dgt = dgated[rs, cs]
                da_ref[rs, cs] = (dgt * mixed * _gelu_grad(up[rs, cs])).astype(BF16)
                dmx = dgt * zu[rs, cs]
                dbias_ref[:, cs] += dmx
                dmb = dmx.astype(BF16)
                dws_ref[g] += _nt(dmb, vb)
                dvn_ref[rs, cs] = jnp.dot(wt_ref[g], dmb, preferred_element_type=F32)
        dvn = dvn_ref[...]
        dlng_ref[...] += jnp.sum(dvn * vhat, axis=0, keepdims=True)
        dlnb_ref[...] += jnp.sum(dvn, axis=0, keepdims=True)
        dvh = dvn * gam
        dzv = rstd * (dvh - jnp.mean(dvh, axis=-1, keepdims=True)
                      - vhat * jnp.mean(dvh * vhat, axis=-1, keepdims=True))
        da_ref[:, W:] = (dzv * _gelu_grad(vp)).astype(BF16)

    const2 = lambda shape: pl.BlockSpec(shape, lambda i: (0, 0))
    const3 = pl.BlockSpec((G, LANES, LANES), lambda i: (0, 0, 0))
    return pl.pallas_call(
        body, grid=(T // tb,),
        in_specs=[pl.BlockSpec((tb, W2), lambda i: (i, 0)), pl.BlockSpec((tb, W), lambda i: (i, 0)),
                  const2((1, W)), const2((1, W)), const3, const3, const2((LANES, W))],
        out_specs=[pl.BlockSpec((tb, W2), lambda i: (i, 0)), const3, const2((LANES, W)),
                   const2((SUBLANES, W)), const2((SUBLANES, W))],
        out_shape=[jax.ShapeDtypeStruct((T, W2), BF16), jax.ShapeDtypeStruct((G, LANES, LANES), F32),
                   jax.ShapeDtypeStruct((LANES, W), F32), jax.ShapeDtypeStruct((SUBLANES, W), F32),
                   jax.ShapeDtypeStruct((SUBLANES, W), F32)],
        scratch_shapes=[pltpu.VMEM((tb, W), F32)],
        name=name, compiler_params=_cp(),
    )(a, dgated, ln_g.reshape(1, W), ln_b.reshape(1, W), w_tril, w_tril_t, bias)


def _adam_math(w, g, m, v):
    m = ADAM_B1 * m + (1.0 - ADAM_B1) * g
    v = ADAM_B2 * v + (1.0 - ADAM_B2) * (g * g)
    m_hat = m / (1.0 - ADAM_B1 ** ADAM_STEP)
    v_hat = v / (1.0 - ADAM_B2 ** ADAM_STEP)
    delta = -ADAM_LR * (m_hat / (jnp.sqrt(v_hat) + ADAM_EPS) + ADAM_WD * w)
    return delta, m, v


def _adamw_halves(mine, theirs, c_idx, w, m, v, name):
    R, C = w.shape
    rh = R // 2
    tb = _row_tile(rh, C)
    nb = rh // tb

    def body(c_ref, a_ref, b_ref, w_ref, m_ref, v_ref, g_ref, d_ref, mo_ref, vo_ref):
        g = jnp.where(pl.program_id(0) == c_ref[0], a_ref[...], b_ref[...])
        d, mm, vv = _adam_math(w_ref[...], g, m_ref[...], v_ref[...])
        g_ref[...] = g
        d_ref[...] = d
        mo_ref[...] = mm
        vo_ref[...] = vv

    half = pl.BlockSpec((tb, C), lambda h, i, c: (i, 0))
    row = pl.BlockSpec((tb, C), lambda h, i, c: (h * nb + i, 0))
    sds = jax.ShapeDtypeStruct((R, C), F32)
    return pl.pallas_call(
        body,
        grid_spec=pltpu.PrefetchScalarGridSpec(
            num_scalar_prefetch=1, grid=(2, nb), in_specs=[half, half, row, row, row], out_specs=[row] * 4),
        out_shape=[sds] * 4, name=name, compiler_params=_cp())(c_idx, mine, theirs, w, m, v)


def _adamw_sum(parts, w, m, v, name):
    K, R, C = parts.shape
    tb = _tile(R, 128, SUBLANES)

    def body(p_ref, w_ref, m_ref, v_ref, g_ref, d_ref, mo_ref, vo_ref):
        g = p_ref[0]
        for k in range(1, K):
            g = g + p_ref[k]
        d, mm, vv = _adam_math(w_ref[...], g, m_ref[...], v_ref[...])
        g_ref[...] = g
        d_ref[...] = d
        mo_ref[...] = mm
        vo_ref[...] = vv

    row = pl.BlockSpec((tb, C), lambda i: (i, 0))
    sds = jax.ShapeDtypeStruct((R, C), F32)
    return pl.pallas_call(
        body, grid=(R // tb,),
        in_specs=[pl.BlockSpec((K, tb, C), lambda i: (0, i, 0)), row, row, row],
        out_specs=[row] * 4, out_shape=[sds] * 4, name=name, compiler_params=_cp())(parts, w, m, v)


def _pair_sum(g_all, recv, c_idx, name):
    K, R, C = g_all.shape
    rh = R // 2
    tb = _row_tile(rh, C)
    nb = rh // tb

    def body(c_ref, a_ref, b_ref, o_ref):
        o_ref[...] = (a_ref[...] + b_ref[...]).astype(BF16)

    return pl.pallas_call(
        body,
        grid_spec=pltpu.PrefetchScalarGridSpec(
            num_scalar_prefetch=1, grid=(K, nb),
            in_specs=[pl.BlockSpec((1, tb, C), lambda k, i, c: (k, c[0] * nb + i, 0)),
                      pl.BlockSpec((1, tb, C), lambda k, i, c: (k, i, 0))],
            out_specs=pl.BlockSpec((1, tb, C), lambda k, i, c: (k, i, 0))),
        out_shape=jax.ShapeDtypeStruct((K, rh, C), BF16), name=name, compiler_params=_cp(),
    )(c_idx, g_all, recv)


def _sum_parts(parts, name):
    K, R, C = parts.shape
    tb = _row_tile(R, C)

    def body(p_ref, o_ref):
        g = p_ref[0].astype(F32)
        for k in range(1, K):
            g = g + p_ref[k].astype(F32)
        o_ref[...] = g

    return pl.pallas_call(
        body, grid=(R // tb,), in_specs=[pl.BlockSpec((K, tb, C), lambda i: (0, i, 0))],
        out_specs=pl.BlockSpec((tb, C), lambda i: (i, 0)),
        out_shape=jax.ShapeDtypeStruct((R, C), F32), name=name, compiler_params=_cp())(parts)


_CHIP_RELATIONS = ((1, 0), (0, 1), (1, 1))


def _position():
    return lax.axis_index("x"), lax.axis_index("y"), lax.axis_index("c")


def _flip(v, bit):
    return 1 - v if bit else v


def _gather_weights(w_pack, side, name):
    R, C = w_pack.shape
    rh = R // 2
    n_side = 0 if side is None else 1

    def half(c):
        return pl.ds(pl.multiple_of(c * rh, 16), rh)

    def between_chips(*refs):
        if n_side:
            w_ref, s_ref, ow_ref, os_ref, local_sem, send_sems, recv_sems = refs
        else:
            w_ref, ow_ref, send_sems, recv_sems = refs
        x, y, c = _position()
        me = 2 * x + y
        if n_side:
            own_side = pltpu.make_async_copy(s_ref, os_ref.at[me], local_sem)
            own_side.start()

        def copies(r, slot):
            dx, dy = _CHIP_RELATIONS[r]
            peer = (_flip(x, dx), _flip(y, dy), c)
            out = [pltpu.make_async_remote_copy(
                src_ref=w_ref.at[half(c), :], dst_ref=ow_ref.at[slot, half(c), :], send_sem=send_sems.at[2 * r],
                recv_sem=recv_sems.at[2 * r], device_id=peer, device_id_type=MESH)]
            if n_side:
                out.append(pltpu.make_async_remote_copy(
                    src_ref=s_ref, dst_ref=os_ref.at[slot], send_sem=send_sems.at[2 * r + 1],
                    recv_sem=recv_sems.at[2 * r + 1], device_id=peer, device_id_type=MESH))
            return out

        sent = [cp for r in range(3) for cp in copies(r, me)]
        for cp in sent:
            cp.start()
        for r in range(3):
            dx, dy = _CHIP_RELATIONS[r]
            for cp in copies(r, 2 * _flip(x, dx) + _flip(y, dy)):
                cp.wait_recv()
        for cp in sent:
            cp.wait_send()
        if n_side:
            own_side.wait()

    sems = [pltpu.SemaphoreType.DMA((6,)), pltpu.SemaphoreType.DMA((6,))]
    gathered = jax.ShapeDtypeStruct((4, R, C), w_pack.dtype)
    if n_side:
        halves, sides = pl.pallas_call(
            between_chips, in_specs=[_hbm(), _hbm()], out_specs=[_hbm(), _hbm()],
            out_shape=[gathered, jax.ShapeDtypeStruct((4,) + side.shape, side.dtype)],
            scratch_shapes=[pltpu.SemaphoreType.DMA(())] + sems,
            name=name + "_ici", compiler_params=_cp(),
        )(w_pack, side)
    else:
        sides = None
        halves = pl.pallas_call(
            between_chips, in_specs=[_hbm()], out_specs=_hbm(), out_shape=gathered, scratch_shapes=sems,
            name=name + "_ici", compiler_params=_cp(),
        )(w_pack)

    def to_sibling(g_ref, o_ref, send_sems, recv_sems):
        x, y, c = _position()

        def copy(r, rows):
            dx, dy = _CHIP_RELATIONS[r]
            slot = 2 * _flip(x, dx) + _flip(y, dy)
            return pltpu.make_async_remote_copy(
                src_ref=g_ref.at[slot, rows, :], dst_ref=o_ref.at[slot, rows, :], send_sem=send_sems.at[r],
                recv_sem=recv_sems.at[r], device_id=(x, y, 1 - c), device_id_type=MESH)

        sent = [copy(r, half(c)) for r in range(3)]
        for cp in sent:
            cp.start()
        for r in range(3):
            copy(r, half(1 - c)).wait_recv()
        for cp in sent:
            cp.wait_send()

    full = pl.pallas_call(
        to_sibling, in_specs=[_hbm()], out_specs=_hbm(), input_output_aliases={0: 0},
        out_shape=jax.ShapeDtypeStruct((4, R, C), w_pack.dtype),
        scratch_shapes=[pltpu.SemaphoreType.DMA((3,)), pltpu.SemaphoreType.DMA((3,))],
        name=name + "_d2d", compiler_params=_cp(),
    )(halves)
    return full, sides


def _sibling_halves(g_all, name):
    K, R, C = g_all.shape
    rh = R // 2

    def body(g_ref, o_ref, send_sem, recv_sem):
        x, y, c = _position()
        start = pl.multiple_of((1 - c) * rh, SUBLANES)
        cp = pltpu.make_async_remote_copy(
            src_ref=g_ref.at[:, pl.ds(start, rh), :], dst_ref=o_ref, send_sem=send_sem, recv_sem=recv_sem,
            device_id=(x, y, 1 - c), device_id_type=MESH)
        cp.start()
        cp.wait_recv()
        cp.wait_send()

    return pl.pallas_call(
        body, in_specs=[_hbm()], out_specs=_hbm(),
        out_shape=jax.ShapeDtypeStruct((K, rh, C), F32),
        scratch_shapes=[pltpu.SemaphoreType.DMA(()), pltpu.SemaphoreType.DMA(())],
        name=name, compiler_params=_cp(),
    )(g_all)


def _chip_exchange(parts, name):
    K, R, C = parts.shape

    def body(p_ref, o_ref, local_sem, send_sems, recv_sems):
        x, y, c = _position()
        me = 2 * x + y
        own = pltpu.make_async_copy(p_ref.at[me], o_ref.at[me], local_sem)
        own.start()

        def copy(r, src_slot, dst_slot):
            dx, dy = _CHIP_RELATIONS[r]
            return pltpu.make_async_remote_copy(
                src_ref=p_ref.at[src_slot], dst_ref=o_ref.at[dst_slot], send_sem=send_sems.at[r],
                recv_sem=recv_sems.at[r], device_id=(_flip(x, dx), _flip(y, dy), c), device_id_type=MESH)

        def chip(r):
            dx, dy = _CHIP_RELATIONS[r]
            return 2 * _flip(x, dx) + _flip(y, dy)

        sent = [copy(r, chip(r), me) for r in range(3)]
        for cp in sent:
            cp.start()
        for r in range(3):
            copy(r, me, chip(r)).wait_recv()
        for cp in sent:
            cp.wait_send()
        own.wait()

    return pl.pallas_call(
        body, in_specs=[_hbm()], out_specs=_hbm(),
        out_shape=jax.ShapeDtypeStruct((K, R, C), parts.dtype),
        scratch_shapes=[pltpu.SemaphoreType.DMA(()), pltpu.SemaphoreType.DMA((3,)),
                        pltpu.SemaphoreType.DMA((3,))],
        name=name, compiler_params=_cp(),
    )(parts)


def _swap_with_sibling(half, name):
    rh, C = half.shape

    def body(h_ref, o_ref, send_sem, recv_sem):
        x, y, c = _position()
        cp = pltpu.make_async_remote_copy(
            src_ref=h_ref, dst_ref=o_ref, send_sem=send_sem, recv_sem=recv_sem,
            device_id=(x, y, 1 - c), device_id_type=MESH)
        cp.start()
        cp.wait_recv()
        cp.wait_send()

    return pl.pallas_call(
        body, in_specs=[_hbm()], out_specs=_hbm(),
        out_shape=jax.ShapeDtypeStruct((rh, C), F32),
        scratch_shapes=[pltpu.SemaphoreType.DMA(()), pltpu.SemaphoreType.DMA(())],
        name=name, compiler_params=_cp(),
    )(half)


def _gather_all(part, name):
    R, C = part.shape
    masks = [(b >> 2 & 1, b >> 1 & 1, b & 1) for b in range(1, 8)]

    def body(p_ref, o_ref, local_sem, send_sems, recv_sems):
        x, y, c = _position()
        me = 4 * x + 2 * y + c
        own = pltpu.make_async_copy(p_ref, o_ref.at[me], local_sem)
        own.start()

        def copy(r, slot):
            dx, dy, dc = masks[r]
            return pltpu.make_async_remote_copy(
                src_ref=p_ref, dst_ref=o_ref.at[slot], send_sem=send_sems.at[r], recv_sem=recv_sems.at[r],
                device_id=(_flip(x, dx), _flip(y, dy), _flip(c, dc)), device_id_type=MESH)

        sent = [copy(r, me) for r in range(7)]
        for cp in sent:
            cp.start()
        for r in range(7):
            dx, dy, dc = masks[r]
            copy(r, 4 * _flip(x, dx) + 2 * _flip(y, dy) + _flip(c, dc)).wait_recv()
        for cp in sent:
            cp.wait_send()
        own.wait()

    return pl.pallas_call(
        body, in_specs=[_hbm()], out_specs=_hbm(),
        out_shape=jax.ShapeDtypeStruct((8, R, C), F32),
        scratch_shapes=[pltpu.SemaphoreType.DMA(()), pltpu.SemaphoreType.DMA((7,)),
                        pltpu.SemaphoreType.DMA((7,))],
        name=name, compiler_params=_cp(),
    )(part)


def _pack(arrs, row_mult, cols=PACK_COLS, lead=0):
    head = arrs[0].shape[:lead]
    pieces = []
    for a in arrs:
        flat = a.astype(F32).reshape(head + (-1,))
        fill = -flat.shape[-1] % cols
        if fill:
            flat = jnp.concatenate([flat, jnp.zeros(head + (fill,), F32)], axis=-1)
        pieces.append(flat.reshape(head + (-1, cols)))
    rows = sum(p.shape[lead] for p in pieces)
    fill = -rows % row_mult
    if fill:
        pieces.append(jnp.zeros(head + (fill, cols), F32))
    return jnp.concatenate(pieces, axis=lead) if len(pieces) > 1 else pieces[0]


def _unpack(buf, shapes):
    lead = buf.shape[:-2]
    cols = buf.shape[-1]
    out, off = [], 0
    for shp in shapes:
        n = math.prod(shp)
        rows = -(-n // cols)
        piece = buf[..., off:off + rows, :]
        if rows * cols != n:
            piece = piece.reshape(lead + (-1,))[..., :n]
        out.append(piece.reshape(lead + tuple(shp)))
        off += rows
    return out


def _cols_from_chips(g):
    k, L, A, n = g.shape
    return jnp.transpose(g, (1, 2, 0, 3)).reshape(L, A, k * n)


def _rows_from_chips(g):
    k, L, n, B = g.shape
    return jnp.transpose(g, (1, 0, 2, 3)).reshape(L, k * n, B)


def _cols_to_chips(full, k=4):
    L, A, N = full.shape
    return jnp.transpose(full.reshape(L, A, k, N // k), (2, 0, 1, 3))


def _rows_to_chips(full, k=4):
    L, N, B = full.shape
    return jnp.transpose(full.reshape(L, k, N // k, B), (1, 0, 2, 3))


def kernel(x, mixer_norm_w, attn_w_in, attn_b_f, attn_w_out, sgu_w_in, sgu_ln_g, sgu_ln_b, sgu_w_s, sgu_b_s, sgu_w_out, ffn_norm_w, ffn_w_in, ffn_w_out, final_norm_w, loss_target, m_mixer_norm_w, m_attn_w_in, m_attn_b_f, m_attn_w_out, m_sgu_w_in, m_sgu_ln_g, m_sgu_ln_b, m_sgu_w_s, m_sgu_b_s, m_sgu_w_out, m_ffn_norm_w, m_ffn_w_in, m_ffn_w_out, m_final_norm_w, v_mixer_norm_w, v_attn_w_in, v_attn_b_f, v_attn_w_out, v_sgu_w_in, v_sgu_ln_g, v_sgu_ln_b, v_sgu_w_s, v_sgu_b_s, v_sgu_w_out, v_ffn_norm_w, v_ffn_w_in, v_ffn_w_out, v_final_norm_w):
    T, D = x.shape[1], x.shape[2]
    depth = mixer_norm_w.shape[0]
    H = attn_b_f.shape[1]
    P = D // LANES
    assert D % LANES == 0 and D // H == 64 and 2 * P == H and 2 * P <= LANES
    G = sgu_w_s.shape[1]
    W = sgu_w_out.shape[1] * 4
    assert sgu_w_s.shape[2] == LANES and W == G * LANES
    scale = float(D // H) ** -0.5
    f_pad = LANES
    c_idx = lax.axis_index("c").astype(jnp.int32).reshape(1)

    groups = [
        ([attn_w_out, sgu_w_in, sgu_w_out, ffn_w_out, sgu_ln_g, sgu_ln_b],
         [m_attn_w_out, m_sgu_w_in, m_sgu_w_out, m_ffn_w_out, m_sgu_ln_g, m_sgu_ln_b],
         [v_attn_w_out, v_sgu_w_in, v_sgu_w_out, v_ffn_w_out, v_sgu_ln_g, v_sgu_ln_b]),
        ([ffn_w_in], [m_ffn_w_in], [v_ffn_w_in]),
        ([attn_w_in], [m_attn_w_in], [v_attn_w_in]),
    ]
    group_cols = [D, ffn_w_in.shape[2], attn_w_in.shape[2]]
    group_shapes = [[a.shape for a in g[0]] for g in groups]
    w_packs = [_pack(g[0], 512, cols) for g, cols in zip(groups, group_cols)]
    ln_pack = _pack([sgu_ln_g, sgu_ln_b], SUBLANES)
    my_chip = 2 * lax.axis_index("x") + lax.axis_index("y")
    gathered = []
    for t, w_pack in enumerate(w_packs):
        w_pack_b = w_pack.astype(BF16)
        gat, side = _gather_weights(w_pack_b, ln_pack if t == 0 else None, f"gather_weights_{t}")
        if t == 0:
            gat_ln = side
        gathered.append(_unpack(lax.dynamic_update_index_in_dim(gat, w_pack_b, my_chip, 0), group_shapes[t]))
    (g_ao, g_si, g_so, g_fo, _, _), (g_fi,), (g_ai,) = gathered
    g_lng, g_lnb = _unpack(gat_ln, [sgu_ln_g.shape, sgu_ln_b.shape])
    w_ai = _cols_from_chips(g_ai)
    w_ai = jnp.pad(w_ai, ((0, 0), (0, 0), (0, 3 * D + f_pad - w_ai.shape[2])))
    w_ao = _rows_from_chips(g_ao)
    w_si = _cols_from_chips(g_si)
    w_so = _rows_from_chips(g_so)
    w_fo = _rows_from_chips(g_fo)
    w_fi5 = g_fi.reshape((2, 2) + g_fi.shape[1:])
    ln_g = jnp.transpose(g_lng, (1, 0, 2)).reshape(sgu_ln_g.shape[0], W)
    ln_b = jnp.transpose(g_lnb, (1, 0, 2)).reshape(sgu_ln_b.shape[0], W)
    w_tril = jnp.tril(sgu_w_s)
    w_tril_b = w_tril.astype(BF16)
    w_tril_tb = jnp.swapaxes(w_tril, 2, 3).astype(BF16)
    sgu_bias = jnp.repeat(jnp.swapaxes(sgu_b_s, 1, 2), LANES, axis=2)
    b_f_pad = jnp.pad(attn_b_f, ((0, 0), (0, LANES - H)))

    xs = x.reshape(T, D)
    saved = []
    for i in range(depth):
        j = i // 2
        h = _rmsnorm_fwd(xs, mixer_norm_w[i], f"mix_norm_{i}")
        rec = {"x_in": xs, "h": h}
        if i % 2 == 0:
            qkv = _mm(h, w_ai[j, :, :3 * D], "nn", BF16, f"attn_qkv_{i}")
            f = _mm(h, w_ai[j, :, 3 * D:], "nn", F32, f"attn_gate_{i}")
            cT, c_cols, c0T = _gate_fwd(f, b_f_pad[j:j + 1], P, f"gate_fwd_{i}")
            o, lseT = _attn_fwd(qkv, cT, P, scale, f"attn_fwd_{i}")
            x_mid = _mm(o, w_ao[j], "nn", F32, f"attn_out_{i}", res=xs)
            rec.update(qkv=qkv, f=f, c0T=c0T, c_cols=c_cols, o=o, lseT=lseT)
        else:
            a = _mm(h, w_si[j], "nn", BF16, f"sgu_in_{i}")
            gated = _sgu_fwd(a, ln_g[j], ln_b[j], w_tril_b[j], sgu_bias[j], f"sgu_fwd_{i}")
            x_mid = _mm(gated, w_so[j], "nn", F32, f"sgu_out_{i}", res=xs)
            rec.update(a=a, gated=gated)
        h2 = _rmsnorm_fwd(x_mid, ffn_norm_w[i], f"ffn_norm_{i}")
        fa, s = _ffn_in_act(h2, w_fi5, i, f"ffn_in_{i}")
        xs = _mm(s, w_fo[i], "nn", F32, f"ffn_out_{i}", res=x_mid)
        rec.update(x_mid=x_mid, h2=h2, fa=fa, s=s)
        saved.append(rec)

    gx, loss_acc, dw_final = _loss_head(xs, final_norm_w, loss_target.reshape(T, D), "loss_head")
    loss = lax.psum(loss_acc[0, 0], ("x", "y", "c"))

    n_attn, n_sgu = attn_w_in.shape[0], sgu_w_in.shape[0]
    d_mixer_norm, d_ffn_norm = [None] * depth, [None] * depth
    d_ai, d_ao, d_bf = [None] * n_attn, [None] * n_attn, [None] * n_attn
    d_si, d_so, d_lng, d_lnb, d_ws, d_bs = ([None] * n_sgu for _ in range(6))
    d_fi, d_fo = [None] * depth, [None] * depth
    for i in reversed(range(depth)):
        j = i // 2
        rec = saved[i]
        d_fo[i] = _mm(rec["s"], gx, "tn", F32, f"ffn_out_wgrad_{i}")
        da = _ffn_out_bwd_act(gx, w_fo[i], rec["fa"], f"ffn_out_bwd_{i}")
        da = da.reshape((4,) + da.shape[2:])
        dh2 = _mm_nt_shards(da, g_fi, i, f"ffn_in_bwd_{i}")
        d_fi[i] = _mm_tn_shards(rec["h2"], da, f"ffn_in_wgrad_{i}")
        gx, dwn = _rmsnorm_bwd(dh2, rec["x_mid"], ffn_norm_w[i], gx, f"ffn_norm_bwd_{i}")
        d_ffn_norm[i] = dwn[0]
        if i % 2 == 0:
            do = _mm(gx, w_ao[j], "nt", BF16, f"attn_out_bwd_{i}")
            d_ao[j] = _mm(rec["o"], gx, "tn", F32, f"attn_out_wgrad_{i}")
            dT = _attn_delta(do, rec["o"], P, f"attn_delta_{i}")
            dq, dk, dv, dc_cols, drowT = _attn_bwd(rec["qkv"], do, rec["lseT"], dT, rec["c0T"], rec["c_cols"],
                                                   P, scale, f"attn_bwd_{i}")
            df, dbf = _gate_bwd(dc_cols, drowT, rec["f"], b_f_pad[j:j + 1], P, f"gate_bwd_{i}")
            d_bf[j] = dbf[0, :H]
            dproj = jnp.concatenate([dq, dk, dv, df.astype(BF16)], axis=1)
            dh = _mm(dproj, w_ai[j], "nt", F32, f"attn_in_bwd_{i}")
            d_ai[j] = _mm(rec["h"], dproj, "tn", F32, f"attn_in_wgrad_{i}")[:, :3 * D + H]
        else:
            dgated = _mm(gx, w_so[j], "nt", F32, f"sgu_out_bwd_{i}")
            d_so[j] = _mm(rec["gated"], gx, "tn", F32, f"sgu_out_wgrad_{i}")
            da_s, dws, dbias, dlng, dlnb = _sgu_bwd(rec["a"], dgated, ln_g[j], ln_b[j], w_tril_b[j],
                                                    w_tril_tb[j], sgu_bias[j], f"sgu_bwd_{i}")
            d_ws[j] = jnp.tril(dws)
            d_bs[j] = jnp.sum(dbias.reshape(LANES, G, LANES), axis=2).T
            d_lng[j], d_lnb[j] = dlng[0], dlnb[0]
            dh = _mm(da_s, w_si[j], "nt", F32, f"sgu_in_bwd_{i}")
            d_si[j] = _mm(rec["h"], da_s, "tn", F32, f"sgu_in_wgrad_{i}")
        gx, dwn = _rmsnorm_bwd(dh, rec["x_in"], mixer_norm_w[i], gx, f"mix_norm_bwd_{i}")
        d_mixer_norm[i] = dwn[0]
    grad_x = gx.reshape(x.shape)

    group_grads = [
        [_rows_to_chips(jnp.stack(d_ao)), _cols_to_chips(jnp.stack(d_si)), _rows_to_chips(jnp.stack(d_so)),
         _rows_to_chips(jnp.stack(d_fo)),
         jnp.transpose(jnp.stack(d_lng).reshape(n_sgu, 4, W // 4), (1, 0, 2)),
         jnp.transpose(jnp.stack(d_lnb).reshape(n_sgu, 4, W // 4), (1, 0, 2))],
        [jnp.stack(d_fi, axis=1)],
        [_cols_to_chips(jnp.stack(d_ai))],
    ]
    reduced = []
    for t, (grads, cols) in enumerate(zip(group_grads, group_cols)):
        g_all = _pack(grads, 512, cols, lead=1)
        from_sibling = _sibling_halves(g_all, f"grad_sibling_halves_{t}")
        pair = _pair_sum(g_all, from_sibling, c_idx, f"grad_pair_sum_{t}")
        from_chips = _chip_exchange(pair, f"grad_chip_exchange_{t}")
        my_half = _sum_parts(from_chips, f"grad_chip_sum_{t}")
        sibling_half = _swap_with_sibling(my_half, f"grad_swap_halves_{t}")
        packs = _adamw_halves(my_half, sibling_half, c_idx, w_packs[t], _pack(groups[t][1], 512, cols),
                              _pack(groups[t][2], 512, cols), f"adamw_sharded_{t}")
        reduced.append([_unpack(p, group_shapes[t]) for p in packs])

    def sharded_outputs(which):
        (ao, si, so, fo, lng, lnb), (fi,), (ai,) = (reduced[t][which] for t in range(3))
        return [ai, ao, si, so, fi, fo, lng, lnb]

    g_sh, d_sh, m_sh, v_sh = (sharded_outputs(w) for w in range(4))

    repl = [mixer_norm_w, attn_b_f, sgu_w_s, sgu_b_s, ffn_norm_w, final_norm_w]
    repl_m = [m_mixer_norm_w, m_attn_b_f, m_sgu_w_s, m_sgu_b_s, m_ffn_norm_w, m_final_norm_w]
    repl_v = [v_mixer_norm_w, v_attn_b_f, v_sgu_w_s, v_sgu_b_s, v_ffn_norm_w, v_final_norm_w]
    repl_shapes = [a.shape for a in repl]
    repl_grads = [jnp.stack(d_mixer_norm), jnp.stack(d_bf), jnp.stack(d_ws), jnp.stack(d_bs),
                  jnp.stack(d_ffn_norm), dw_final[0]]
    parts = _gather_all(_pack(repl_grads, SUBLANES), "grad_gather_replicated")
    g_rep, d_rep, m_rep, v_rep = _adamw_sum(parts, _pack(repl, SUBLANES), _pack(repl_m, SUBLANES),
                                            _pack(repl_v, SUBLANES), "adamw_replicated")
    g_r = _unpack(g_rep, repl_shapes)
    d_r = _unpack(d_rep, repl_shapes)
    m_r = _unpack(m_rep, repl_shapes)
    v_r = _unpack(v_rep, repl_shapes)

    def ordered(sh, rp):
        ai, ao, si, so, fi, fo, lng, lnb = sh
        mn, bf, ws, bs, fn, fin = rp
        return [mn, ai, bf, ao, si, lng, lnb, ws, bs, so, fn, fi, fo, fin]

    return (loss, grad_x, *ordered(g_sh, g_r), *ordered(d_sh, d_r), *ordered(m_sh, m_r), *ordered(v_sh, v_r))
```

```python
import functools
import math

import jax
import jax.numpy as jnp
from jax import lax
from jax.experimental import pallas as pl
from jax.experimental.pallas import tpu as pltpu

F32 = jnp.float32
BF16 = jnp.bfloat16
NORM_EPS = 1e-6
LN_EPS = 1e-5
ADAM_LR = 0.001
ADAM_B1 = 0.9
ADAM_B2 = 0.999
ADAM_EPS = 1e-08
ADAM_WD = 0.01
ADAM_STEP = 10

LANES = 128
SUBLANES = 8
PACK_COLS = 1024
VMEM_LIMIT = 56 * 1024 * 1024
NEG_BIG = -1e30
MESH = pl.DeviceIdType.MESH


def _cp():
    return pltpu.CompilerParams(vmem_limit_bytes=VMEM_LIMIT)


def _tile(n, cap, mult):
    best = None
    d = mult
    while d <= min(n, cap):
        if n % d == 0:
            best = d
        d += mult
    return n if best is None else best


def _row_tile(rows, cols):
    cap = max(16, (512 * 1024 // cols) // 16 * 16)
    return _tile(rows, cap, 16)


def _hbm():
    return pl.BlockSpec(memory_space=pltpu.HBM)


def _rmsnorm_fwd(x, w, name):
    T, D = x.shape
    tm = _tile(T, 512, SUBLANES)

    def body(x_ref, w_ref, h_ref):
        xf = x_ref[...]
        r = lax.rsqrt(jnp.mean(xf * xf, axis=-1, keepdims=True) + NORM_EPS)
        h_ref[...] = (xf * r * w_ref[...]).astype(BF16)

    return pl.pallas_call(
        body, grid=(T // tm,),
        in_specs=[pl.BlockSpec((tm, D), lambda i: (i, 0)), pl.BlockSpec((1, D), lambda i: (0, 0))],
        out_specs=pl.BlockSpec((tm, D), lambda i: (i, 0)),
        out_shape=jax.ShapeDtypeStruct((T, D), BF16), name=name, compiler_params=_cp(),
    )(x, w.reshape(1, D))


def _nt_norm_bwd(a3, b4, layer, x, w, dres, name):
    S, T, Ks = a3.shape
    D = x.shape[1]
    tm = _tile(T, 256, 16)

    def body(a_ref, b_ref, x_ref, w_ref, dres_ref, dx_ref, dw_ref):
        @pl.when(pl.program_id(0) == 0)
        def _():
            dw_ref[...] = jnp.zeros_like(dw_ref)

        dh = _nt(a_ref[0].astype(BF16), b_ref[0, 0])
        for s in range(1, S):
            dh = dh + _nt(a_ref[s].astype(BF16), b_ref[s, 0])
        xf = x_ref[...]
        r = lax.rsqrt(jnp.mean(xf * xf, axis=-1, keepdims=True) + NORM_EPS)
        xhat = xf * r
        dxhat = dh * w_ref[...]
        dx_ref[...] = dres_ref[...] + r * (dxhat - xhat * jnp.mean(dxhat * xhat, axis=-1, keepdims=True))
        dw_ref[...] += jnp.sum(dh * xhat, axis=0, keepdims=True)

    row = pl.BlockSpec((tm, D), lambda i: (i, 0))
    return pl.pallas_call(
        body, grid=(T // tm,),
        in_specs=[pl.BlockSpec((S, tm, Ks), lambda i: (0, i, 0)),
                  pl.BlockSpec((S, 1, D, Ks), lambda i: (0, layer, 0, 0)),
                  row, pl.BlockSpec((1, D), lambda i: (0, 0)), row],
        out_specs=[row, pl.BlockSpec((SUBLANES, D), lambda i: (0, 0))],
        out_shape=[jax.ShapeDtypeStruct((T, D), F32), jax.ShapeDtypeStruct((SUBLANES, D), F32)],
        name=name, compiler_params=_cp(),
    )(a3, b4, x, w.reshape(1, D), dres)


def _mm(a, b, mode, out_dtype, name, res=None):
    if mode == "tn":
        kt, M = a.shape
        N = b.shape[1]
        tm = _tile(M, 1408, LANES)
        tn = _tile(N, 1408, LANES)
        tk = _tile(kt, 1024, 16)

        def body(a_ref, b_ref, o_ref):
            @pl.when(pl.program_id(2) == 0)
            def _():
                o_ref[...] = jnp.zeros_like(o_ref)

            o_ref[...] += lax.dot_general(
                a_ref[...].astype(BF16), b_ref[...].astype(BF16), (((0,), (0,)), ((), ())),
                preferred_element_type=F32)

        return pl.pallas_call(
            body, grid=(M // tm, N // tn, kt // tk),
            in_specs=[pl.BlockSpec((tk, tm), lambda i, j, k: (k, i)),
                      pl.BlockSpec((tk, tn), lambda i, j, k: (k, j))],
            out_specs=pl.BlockSpec((tm, tn), lambda i, j, k: (i, j)),
            out_shape=jax.ShapeDtypeStruct((M, N), F32), name=name, compiler_params=_cp(),
        )(a, b)

    M, K = a.shape
    N = b.shape[1] if mode == "nn" else b.shape[0]
    tm = _tile(M, 512, 16)
    cap = min(3072, (6 << 20) // (2 * K), (4 << 20) // (tm * jnp.dtype(out_dtype).itemsize))
    tn = _tile(N, max(LANES, cap // LANES * LANES), LANES)
    dims = (((1,), (0,)), ((), ())) if mode == "nn" else (((1,), (1,)), ((), ()))

    def body(*refs):
        if res is None:
            a_ref, b_ref, o_ref = refs
        else:
            a_ref, b_ref, r_ref, o_ref = refs
        acc = lax.dot_general(a_ref[...].astype(BF16), b_ref[...].astype(BF16), dims,
                              preferred_element_type=F32)
        if res is not None:
            acc = acc + r_ref[...]
        o_ref[...] = acc.astype(out_dtype)

    b_spec = (pl.BlockSpec((K, tn), lambda j, i: (0, j)) if mode == "nn"
              else pl.BlockSpec((tn, K), lambda j, i: (j, 0)))
    in_specs = [pl.BlockSpec((tm, K), lambda j, i: (i, 0)), b_spec]
    args = [a, b]
    if res is not None:
        in_specs.append(pl.BlockSpec((tm, tn), lambda j, i: (i, j)))
        args.append(res)
    return pl.pallas_call(
        body, grid=(N // tn, M // tm), in_specs=in_specs,
        out_specs=pl.BlockSpec((tm, tn), lambda j, i: (i, j)),
        out_shape=jax.ShapeDtypeStruct((M, N), out_dtype), name=name, compiler_params=_cp(),
    )(*args)


def _ffn_in_act(h, w5, layer, name):
    T, D = h.shape
    n = w5.shape[-1]
    tm = _tile(T, 256, 16)

    def body(h_ref, w_ref, a_ref, s_ref):
        hv = h_ref[...]
        g = jnp.dot(hv, w_ref[0, 0, 0], preferred_element_type=F32)
        u = jnp.dot(hv, w_ref[1, 0, 0], preferred_element_type=F32)
        a_ref[0, 0] = g.astype(BF16)
        a_ref[1, 0] = u.astype(BF16)
        s_ref[...] = (g * jax.nn.sigmoid(g) * u).astype(BF16)

    return pl.pallas_call(
        body, grid=(2, T // tm),
        in_specs=[pl.BlockSpec((tm, D), lambda j, i: (i, 0)),
                  pl.BlockSpec((2, 1, 1, D, n), lambda j, i: (0, j, layer, 0, 0))],
        out_specs=[pl.BlockSpec((2, 1, tm, n), lambda j, i: (0, j, i, 0)), pl.BlockSpec((tm, n), lambda j, i: (i, j))],
        out_shape=[jax.ShapeDtypeStruct((2, 2, T, n), BF16), jax.ShapeDtypeStruct((T, 2 * n), BF16)],
        name=name, compiler_params=_cp(),
    )(h, w5)


def _ffn_out_bwd_act(gx, w_out, a4, name):
    T, D = gx.shape
    n = a4.shape[-1]
    tm = _tile(T, 256, 16)

    def body(gx_ref, w_ref, a_ref, da_ref):
        ds = _nt(gx_ref[...].astype(BF16), w_ref[...])
        g = a_ref[0, 0].astype(F32)
        u = a_ref[1, 0].astype(F32)
        sg = jax.nn.sigmoid(g)
        da_ref[0, 0] = (ds * u * (sg * (1.0 + g * (1.0 - sg)))).astype(BF16)
        da_ref[1, 0] = (ds * (g * sg)).astype(BF16)

    blk = pl.BlockSpec((2, 1, tm, n), lambda j, i: (0, j, i, 0))
    return pl.pallas_call(
        body, grid=(2, T // tm),
        in_specs=[pl.BlockSpec((tm, D), lambda j, i: (i, 0)), pl.BlockSpec((n, D), lambda j, i: (j, 0)), blk],
        out_specs=blk,
        out_shape=jax.ShapeDtypeStruct((2, 2, T, n), BF16), name=name, compiler_params=_cp(),
    )(gx, w_out, a4)


def _mm_tn_shards(h, a4, name):
    K, T, n = a4.shape
    D = h.shape[1]
    tk = _tile(T, 1024, 16)

    def body(h_ref, a_ref, o_ref):
        @pl.when(pl.program_id(1) == 0)
        def _():
            o_ref[...] = jnp.zeros_like(o_ref)

        o_ref[0] += lax.dot_general(h_ref[...], a_ref[0], (((0,), (0,)), ((), ())), preferred_element_type=F32)

    return pl.pallas_call(
        body, grid=(K, T // tk),
        in_specs=[pl.BlockSpec((tk, D), lambda k, t: (t, 0)), pl.BlockSpec((1, tk, n), lambda k, t: (k, t, 0))],
        out_specs=pl.BlockSpec((1, D, n), lambda k, t: (k, 0, 0)),
        out_shape=jax.ShapeDtypeStruct((K, D, n), F32), name=name, compiler_params=_cp(),
    )(h, a4)


def _loss_head(x, w, tgt, name):
    T, D = x.shape
    tm = _tile(T, 512, SUBLANES)

    def body(x_ref, w_ref, t_ref, dx_ref, loss_ref, dw_ref):
        @pl.when(pl.program_id(0) == 0)
        def _():
            loss_ref[...] = jnp.zeros_like(loss_ref)
            dw_ref[...] = jnp.zeros_like(dw_ref)

        xf = x_ref[...]
        wv = w_ref[...]
        r = lax.rsqrt(jnp.mean(xf * xf, axis=-1, keepdims=True) + NORM_EPS)
        xhat = xf * r
        err = xhat * wv - t_ref[...]
        per_tok = jnp.mean(err * err, axis=-1, keepdims=True)
        loss_ref[...] += 0.5 * jnp.sum(per_tok, axis=0, keepdims=True)
        dy = err * (1.0 / D)
        dxhat = dy * wv
        dx_ref[...] = r * (dxhat - xhat * jnp.mean(dxhat * xhat, axis=-1, keepdims=True))
        dw_ref[...] += jnp.sum(dy * xhat, axis=0, keepdims=True)

    row = pl.BlockSpec((tm, D), lambda i: (i, 0))
    return pl.pallas_call(
        body, grid=(T // tm,),
        in_specs=[row, pl.BlockSpec((1, D), lambda i: (0, 0)), row],
        out_specs=[row, pl.BlockSpec((SUBLANES, LANES), lambda i: (0, 0)),
                   pl.BlockSpec((SUBLANES, D), lambda i: (0, 0))],
        out_shape=[jax.ShapeDtypeStruct((T, D), F32), jax.ShapeDtypeStruct((SUBLANES, LANES), F32),
                   jax.ShapeDtypeStruct((SUBLANES, D), F32)],
        name=name, compiler_params=_cp(),
    )(x, w.reshape(1, D), tgt)


def _split3(v):
    hi = v.astype(BF16)
    r1 = v - hi.astype(F32)
    mid = r1.astype(BF16)
    lo = (r1 - mid.astype(F32)).astype(BF16)
    return hi, mid, lo


def _tri_dot(tri, v):
    out = None
    for piece in _split3(v):
        t = jnp.dot(tri, piece, preferred_element_type=F32)
        out = t if out is None else out + t
    return out


def _q_block(T):
    return _tile(T, 256, LANES)


def _gate_fwd(f, b_f, P, name):
    T = f.shape[0]
    tb = _q_block(T)

    def body(f_ref, b_ref, ct_ref, cc_ref, c0_ref, carry):
        @pl.when(pl.program_id(0) == 0)
        def _():
            carry[...] = jnp.zeros_like(carry)

        z = f_ref[...] + b_ref[...]
        logf = jnp.minimum(z, 0.0) - jnp.log(1.0 + jnp.exp(-jnp.abs(z)))
        row = lax.broadcasted_iota(jnp.int32, (tb, tb), 0)
        col = lax.broadcasted_iota(jnp.int32, (tb, tb), 1)
        tri = (col <= row).astype(BF16)
        c = _tri_dot(tri, logf) + carry[0:1, :]
        carry[...] = jnp.broadcast_to(c[tb - 1:tb, :], carry.shape)
        first = jnp.broadcast_to(c[0:1, :], c.shape)
        for p in range(P):
            shifted = c if p == 0 else pltpu.roll(c, LANES - 2 * p, 1)
            cc_ref[p] = shifted
            ct_ref[p] = shifted.T[0:SUBLANES, :]
            c0_ref[p] = (first if p == 0 else pltpu.roll(first, LANES - 2 * p, 1)).T[0:SUBLANES, :]

    rows = pl.BlockSpec((P, SUBLANES, tb), lambda i: (0, 0, i))
    return pl.pallas_call(
        body, grid=(T // tb,),
        in_specs=[pl.BlockSpec((tb, LANES), lambda i: (i, 0)), pl.BlockSpec((1, LANES), lambda i: (0, 0))],
        out_specs=[rows, pl.BlockSpec((P, tb, LANES), lambda i: (0, i, 0)), rows],
        out_shape=[jax.ShapeDtypeStruct((P, SUBLANES, T), F32), jax.ShapeDtypeStruct((P, T, LANES), F32),
                   jax.ShapeDtypeStruct((P, SUBLANES, T), F32)],
        scratch_shapes=[pltpu.VMEM((SUBLANES, LANES), F32)],
        name=name, compiler_params=_cp(),
    )(f, b_f)


def _gate_bwd(dc_cols, drowT, f, b_f, P, name):
    T = f.shape[0]
    tb = _tile(T, 256, LANES)
    nb = T // tb

    def body(dc_ref, dr_ref, f_ref, b_ref, df_ref, db_ref, carry):
        @pl.when(pl.program_id(0) == 0)
        def _():
            carry[...] = jnp.zeros_like(carry)
            db_ref[...] = jnp.zeros_like(db_ref)

        lane = lax.broadcasted_iota(jnp.int32, (tb, LANES), 1)
        dc = jnp.zeros((tb, LANES), F32)
        for p in range(P):
            rows = jnp.concatenate([dr_ref[p], jnp.zeros((LANES - SUBLANES, tb), F32)], axis=0)
            part = jnp.where(lane < 2, dc_ref[p] + rows.T, 0.0)
            dc = dc + (part if p == 0 else pltpu.roll(part, 2 * p, 1))
        row = lax.broadcasted_iota(jnp.int32, (tb, tb), 0)
        col = lax.broadcasted_iota(jnp.int32, (tb, tb), 1)
        tri = (col >= row).astype(BF16)
        dlogf = _tri_dot(tri, dc) + carry[0:1, :]
        carry[...] = jnp.broadcast_to(dlogf[0:1, :], carry.shape)
        z = f_ref[...] + b_ref[...]
        df = jnp.where(lane < 2 * P, dlogf * jax.nn.sigmoid(-z), 0.0)
        df_ref[...] = df
        db_ref[...] += jnp.sum(df, axis=0, keepdims=True)

    return pl.pallas_call(
        body, grid=(nb,),
        in_specs=[pl.BlockSpec((P, tb, LANES), lambda i: (0, nb - 1 - i, 0)),
                  pl.BlockSpec((P, SUBLANES, tb), lambda i: (0, 0, nb - 1 - i)),
                  pl.BlockSpec((tb, LANES), lambda i: (nb - 1 - i, 0)),
                  pl.BlockSpec((1, LANES), lambda i: (0, 0))],
        out_specs=[pl.BlockSpec((tb, LANES), lambda i: (nb - 1 - i, 0)),
                   pl.BlockSpec((SUBLANES, LANES), lambda i: (0, 0))],
        out_shape=[jax.ShapeDtypeStruct((T, LANES), F32), jax.ShapeDtypeStruct((SUBLANES, LANES), F32)],
        scratch_shapes=[pltpu.VMEM((SUBLANES, LANES), F32)],
        name=name, compiler_params=_cp(),
    )(dc_cols, drowT, f, b_f)


def _nt(a, b):
    return lax.dot_general(a, b, (((1,), (1,)), ((), ())), preferred_element_type=F32)


def _attn_fwd(qkv, cT, P, scale, name):
    T = qkv.shape[0]
    tq = _q_block(T)
    tw = _tile(T, 4 * tq, 2 * tq)
    cw = tw // 2
    assert cw % tq == 0, "the sequence must split into chunks of whole query blocks"
    nq = T // tq

    def body(q_ref, k_ref, v_ref, c_ref, o_ref, lse_ref, s_scr):
        i = pl.program_id(1)
        lane = lax.broadcasted_iota(jnp.int32, (1, LANES), 1)
        q = (q_ref[...].astype(F32) * scale).astype(BF16)
        q_heads = (jnp.where(lane < 64, q, jnp.zeros_like(q)), jnp.where(lane >= 64, q, jnp.zeros_like(q)))
        c0 = c_ref[0, :, pl.ds(pl.multiple_of(i * tq, tq), LANES)][:, 0:1]

        def scores(start, width, a):
            bias = c0 - c_ref[0, :, pl.ds(start, width)]
            return _nt(q_heads[a], k_ref[pl.ds(start, width), :]) + bias[a:a + 1, :]

        def softmax_pv(start, width, s_of, carry):
            v = v_ref[pl.ds(start, width), :]
            one = jnp.ones_like(v)
            v_heads = (jnp.where(lane < 64, v, one), jnp.where(lane >= 64, v, one))
            new = []
            for a in range(2):
                m, acc = carry[a]
                s = s_of(a)
                m_new = jnp.maximum(m, jnp.max(s, axis=1, keepdims=True))
                p = jnp.exp(s - m_new)
                acc = jnp.exp(m - m_new) * acc + jnp.dot(p.astype(BF16), v_heads[a], preferred_element_type=F32)
                new.append((m_new, acc))
            return tuple(new)

        def fill(start, buf):
            for a in range(2):
                s_scr[2 * buf + a] = scores(start, cw, a)

        def wide(j, carry):
            base = pl.multiple_of(j * tw, tw)
            fill(base + cw, 1)
            carry = softmax_pv(base, cw, lambda a: s_scr[a], carry)
            fill(base + tw, 0)
            return softmax_pv(base + cw, cw, lambda a: s_scr[2 + a], carry)

        init = tuple((jnp.full((tq, 1), NEG_BIG, F32), jnp.zeros((tq, LANES), F32)) for _ in range(2))
        n_wide = (i * tq) // tw
        fill(0, 0)
        carry = lax.fori_loop(0, n_wide, wide, init)

        base = pl.multiple_of(n_wide * tw, tw)
        ahead = i * tq - base
        col_minus_row = (lax.broadcasted_iota(jnp.int32, (tq, cw), 1)
                         - lax.broadcasted_iota(jnp.int32, (tq, cw), 0))

        def causal(buf, first_key):
            return lambda a: jnp.where(col_minus_row <= ahead - first_key, s_scr[2 * buf + a], NEG_BIG)

        def one_chunk(cr):
            return softmax_pv(base, cw, causal(0, 0), cr)

        def two_chunks(cr):
            fill(base + cw, 1)
            cr = softmax_pv(base, cw, causal(0, 0), cr)
            return softmax_pv(base + cw, cw, causal(1, cw), cr)

        (m0, a0), (m1, a1) = lax.cond(ahead >= cw, two_chunks, one_chunk, carry)
        sums = jnp.where(lane < 64, pltpu.roll(a0, 64, 1), pltpu.roll(a1, 64, 1))
        o_ref[...] = (jnp.where(lane < 64, a0, a1) / sums).astype(BF16)
        l0, l1 = a0[:, 64:65], a1[:, 0:1]
        lse = jnp.where(lane == 0, m0 + jnp.log(l0), jnp.where(lane == 1, m1 + jnp.log(l1), 0.0))
        lse_ref[0] = lse.T[0:SUBLANES, :]

    return pl.pallas_call(
        body, grid=(P, nq),
        in_specs=[pl.BlockSpec((tq, LANES), lambda p, i: (i, p)),
                  pl.BlockSpec((T, LANES), lambda p, i: (0, P + p)),
                  pl.BlockSpec((T, LANES), lambda p, i: (0, 2 * P + p)),
                  pl.BlockSpec((1, SUBLANES, T), lambda p, i: (p, 0, 0))],
        out_specs=[pl.BlockSpec((tq, LANES), lambda p, i: (i, p)),
                   pl.BlockSpec((1, SUBLANES, tq), lambda p, i: (p, 0, i))],
        out_shape=[jax.ShapeDtypeStruct((T, LANES * P), BF16), jax.ShapeDtypeStruct((P, SUBLANES, T), F32)],
        scratch_shapes=[pltpu.VMEM((4, tq, cw), F32)],
        name=name, compiler_params=_cp(),
    )(qkv, qkv, qkv, cT)


def _attn_delta(do, o, P, name):
    T, D = o.shape
    tb = _tile(T, 256, LANES)

    def body(do_ref, o_ref, d_ref):
        lane = lax.broadcasted_iota(jnp.int32, (1, LANES), 1)
        for p in range(P):
            cols = slice(p * LANES, (p + 1) * LANES)
            prod = do_ref[:, cols].astype(F32) * o_ref[:, cols].astype(F32)
            d0 = jnp.sum(jnp.where(lane < 64, prod, 0.0), axis=1, keepdims=True)
            d1 = jnp.sum(jnp.where(lane >= 64, prod, 0.0), axis=1, keepdims=True)
            both = jnp.where(lane == 0, d0, jnp.where(lane == 1, d1, 0.0))
            d_ref[p] = both.T[0:SUBLANES, :]

    return pl.pallas_call(
        body, grid=(T // tb,),
        in_specs=[pl.BlockSpec((tb, D), lambda i: (i, 0)), pl.BlockSpec((tb, D), lambda i: (i, 0))],
        out_specs=pl.BlockSpec((P, SUBLANES, tb), lambda i: (0, 0, i)),
        out_shape=jax.ShapeDtypeStruct((P, SUBLANES, T), F32), name=name, compiler_params=_cp(),
    )(do, o)


def _attn_bwd(qkv, do, lseT, dT, c0T, c_cols, P, scale, name):
    T = qkv.shape[0]
    tq = _q_block(T)
    tw = _tile(T, 4 * tq, 2 * tq)
    cw = tw // 2
    assert cw % tq == 0, "the sequence must split into chunks of whole query blocks"
    nq = T // tq

    def body(q_ref, do_ref, k_ref, v_ref, lse_ref, d_ref, c0_ref, cc_ref,
             dq_ref, dk_ref, dv_ref, dc_ref, drow_ref, dq_acc0, dq_acc1, s_scr):
        j = pl.program_id(1)

        @pl.when(j == 0)
        def _():
            dq_acc0[...] = jnp.zeros_like(dq_acc0)
            dq_acc1[...] = jnp.zeros_like(dq_acc1)

        lane = lax.broadcasted_iota(jnp.int32, (1, LANES), 1)
        in_head = (lane < 64, lane >= 64)
        k = k_ref[...]
        v = v_ref[...]
        zero = jnp.zeros_like(k)
        one = jnp.ones_like(k)
        k_heads = tuple(jnp.where(h, k, zero) for h in in_head)
        v_heads = tuple(jnp.where(h, v, zero) for h in in_head)
        k_ones = tuple(jnp.where(h, k, one) for h in in_head)
        cc = cc_ref[0]
        c_first = (cc[0:1, 0:1], cc[0:1, 1:2])
        c_rel = (cc[:, 0:1] - c_first[0], cc[:, 1:2] - c_first[1])
        dq_accs = (dq_acc0, dq_acc1)

        def scaled_q(start, width):
            return (q_ref[pl.ds(start, width), :].astype(F32) * scale).astype(BF16)

        def block(start, width, carry, first_query=None, scores=None):
            q = scaled_q(start, width)
            q_one = jnp.ones_like(q)
            dov = do_ref[pl.ds(start, width), :]
            lse = lse_ref[0, :, pl.ds(start, width)]
            dlt = d_ref[0, :, pl.ds(start, width)]
            c0 = c0_ref[0, :, pl.ds(start, width)]
            new = []
            for a in range(2):
                dk_a, dv_a = carry[a]
                rowv = lse[a:a + 1, :] + (c_first[a] - c0[a:a + 1, :])
                st = _nt(k_heads[a], q) if scores is None else scores(a)
                pt = jnp.exp((st - c_rel[a]) - rowv)
                if first_query is not None:
                    row = lax.broadcasted_iota(jnp.int32, (tq, width), 0)
                    col = lax.broadcasted_iota(jnp.int32, (tq, width), 1)
                    pt = jnp.where(col - row >= first_query, pt, 0.0)
                dpt = _nt(v_heads[a], dov)
                dst_b = (pt * (dpt - dlt[a:a + 1, :])).astype(BF16)
                dv_a = dv_a + jnp.dot(pt.astype(BF16), dov, preferred_element_type=F32)
                dk_a = dk_a + jnp.dot(dst_b, jnp.where(in_head[a], q, q_one), preferred_element_type=F32)
                dq_accs[a][pl.ds(start, width), :] += lax.dot_general(
                    dst_b, k_ones[a], (((0,), (0,)), ((), ())), preferred_element_type=F32)
                new.append((dk_a, dv_a))
            return tuple(new)

        init = tuple((jnp.zeros((tq, LANES), F32), jnp.zeros((tq, LANES), F32)) for _ in range(2))
        first_key = j * tq
        diag = pl.multiple_of((first_key // cw) * cw, cw)
        carry = block(diag, cw, init, first_key - diag)
        first_wide = first_key // tw + 1
        carry = lax.cond(
            diag + cw < first_wide * tw,
            lambda cr: block(pl.multiple_of(diag + cw, cw), cw, cr), lambda cr: cr, carry)

        last = T // tw - 1

        def fill(trip, buf):
            q = scaled_q(pl.multiple_of(jnp.minimum(trip, last) * tw, tw), tw)
            for a in range(2):
                s_scr[2 * buf + a] = _nt(k_heads[a], q)

        def trip(i, buf, cr):
            return block(pl.multiple_of(i * tw, tw), tw, cr, scores=lambda a: s_scr[2 * buf + a])

        def two_trips(p, cr):
            i = first_wide + 2 * p
            fill(i + 1, 1)
            cr = trip(i, 0, cr)
            fill(i + 2, 0)
            return trip(i + 1, 1, cr)

        n_trips = last + 1 - first_wide
        fill(first_wide, 0)
        carry = lax.fori_loop(0, n_trips // 2, two_trips, carry)
        (dk0, dv0), (dk1, dv1) = lax.cond(n_trips % 2 == 1, lambda cr: trip(last, 0, cr), lambda cr: cr, carry)
        dk_ref[...] = jnp.where(lane < 64, dk0, dk1).astype(BF16)
        dv_ref[...] = jnp.where(lane < 64, dv0, dv1).astype(BF16)
        dc_ref[0] = jnp.where(lane == 0, -dk0[:, 64:65], jnp.where(lane == 1, -dk1[:, 0:1], 0.0))

        @pl.when(j == nq - 1)
        def _():
            def finish(i, _):
                rows = pl.ds(pl.multiple_of(i * tq, tq), tq)
                a0 = dq_acc0[rows, :]
                a1 = dq_acc1[rows, :]
                dq_ref[rows, :] = (jnp.where(lane < 64, a0, a1) * scale).astype(BF16)
                sums = jnp.where(lane == 0, a0[:, 64:65], jnp.where(lane == 1, a1[:, 0:1], 0.0))
                drow_ref[0, :, rows] = sums.T[0:SUBLANES, :]
                return 0

            lax.fori_loop(0, nq, finish, 0)

    full = lambda col: pl.BlockSpec((T, LANES), lambda p, j: (0, col(p)))
    blk = lambda col: pl.BlockSpec((tq, LANES), lambda p, j: (j, col(p)))
    rows = pl.BlockSpec((1, SUBLANES, T), lambda p, j: (p, 0, 0))
    cols = pl.BlockSpec((1, tq, LANES), lambda p, j: (p, j, 0))
    D = LANES * P
    return pl.pallas_call(
        body, grid=(P, nq),
        in_specs=[full(lambda p: p), full(lambda p: p), blk(lambda p: P + p), blk(lambda p: 2 * P + p),
                  rows, rows, rows, cols],
        out_specs=[full(lambda p: p), blk(lambda p: p), blk(lambda p: p), cols, rows],
        out_shape=[jax.ShapeDtypeStruct((T, D), BF16), jax.ShapeDtypeStruct((T, D), BF16),
                   jax.ShapeDtypeStruct((T, D), BF16), jax.ShapeDtypeStruct((P, T, LANES), F32),
                   jax.ShapeDtypeStruct((P, SUBLANES, T), F32)],
        scratch_shapes=[pltpu.VMEM((T, LANES), F32), pltpu.VMEM((T, LANES), F32), pltpu.VMEM((4, tq, tw), F32)],
        name=name, compiler_params=_cp(),
    )(qkv, do, qkv, qkv, lseT, dT, c0T, c_cols)


_SQRT_HALF = 0.7071067811865476
_INV_SQRT_2PI = 0.3989422804014327


def _gelu(v):
    return 0.5 * v * (1.0 + lax.erf(v * _SQRT_HALF))


def _gelu_and_grad(v):
    cdf = 0.5 * (1.0 + lax.erf(v * _SQRT_HALF))
    return v * cdf, cdf + v * (_INV_SQRT_2PI * jnp.exp(-0.5 * v * v))


def _sgu_fwd(a, ln_g, ln_b, w_tril, bias, name):
    T, W2 = a.shape
    W = W2 // 2
    G = w_tril.shape[0]
    tb = _tile(T, 256, LANES)

    def body(a_ref, g_ref, b_ref, w_ref, bias_ref, out_ref):
        zu = _gelu(a_ref[:, :W].astype(F32))
        zv = _gelu(a_ref[:, W:].astype(F32))
        mu = jnp.mean(zv, axis=-1, keepdims=True)
        d = zv - mu
        rstd = lax.rsqrt(jnp.mean(d * d, axis=-1, keepdims=True) + LN_EPS)
        vn = (d * rstd * g_ref[...] + b_ref[...]).astype(BF16)
        for c in range(tb // LANES):
            rs = slice(c * LANES, (c + 1) * LANES)
            for g in range(G):
                cs = slice(g * LANES, (g + 1) * LANES)
                mixed = jnp.dot(w_ref[g], vn[rs, cs], preferred_element_type=F32) + bias_ref[:, cs]
                out_ref[rs, cs] = (zu[rs, cs] * mixed).astype(BF16)

    return pl.pallas_call(
        body, grid=(T // tb,),
        in_specs=[pl.BlockSpec((tb, W2), lambda i: (i, 0)), pl.BlockSpec((1, W), lambda i: (0, 0)),
                  pl.BlockSpec((1, W), lambda i: (0, 0)), pl.BlockSpec((G, LANES, LANES), lambda i: (0, 0, 0)),
                  pl.BlockSpec((LANES, W), lambda i: (0, 0))],
        out_specs=pl.BlockSpec((tb, W), lambda i: (i, 0)),
        out_shape=jax.ShapeDtypeStruct((T, W), BF16), name=name, compiler_params=_cp(),
    )(a, ln_g.reshape(1, W), ln_b.reshape(1, W), w_tril, bias)


def _sgu_bwd(a, dgated, ln_g, ln_b, w_tril, w_tril_t, bias, name):
    T, W2 = a.shape
    W = W2 // 2
    G = w_tril.shape[0]
    tb = _tile(T, 256, LANES)

    def body(a_ref, dg_ref, g_ref, b_ref, w_ref, wt_ref, bias_ref,
             da_ref, dws_ref, dbias_ref, dlng_ref, dlnb_ref, dvn_ref):
        @pl.when(pl.program_id(0) == 0)
        def _():
            dws_ref[...] = jnp.zeros_like(dws_ref)
            dbias_ref[...] = jnp.zeros_like(dbias_ref)
            dlng_ref[...] = jnp.zeros_like(dlng_ref)
            dlnb_ref[...] = jnp.zeros_like(dlnb_ref)

        up = a_ref[:, :W].astype(F32)
        vp = a_ref[:, W:].astype(F32)
        zu, gu = _gelu_and_grad(up)
        zv, gv = _gelu_and_grad(vp)
        mu = jnp.mean(zv, axis=-1, keepdims=True)
        d = zv - mu
        rstd = lax.rsqrt(jnp.mean(d * d, axis=-1, keepdims=True) + LN_EPS)
        vhat = d * rstd
        gam = g_ref[...]
        vn = (vhat * gam + b_ref[...]).astype(BF16)
        dgated = dg_ref[...]
        for c in range(tb // LANES):
            rs = slice(c * LANES, (c + 1) * LANES)
            for g in range(G):
                cs = slice(g * LANES, (g + 1) * LANES)
                vb = vn[rs, cs]
                mixed = jnp.dot(w_ref[g], vb, preferred_element_type=F32) + bias_ref[:, cs]
                dgt = dgated[rs, cs]
                da_ref[rs, cs] = (dgt * mixed * gu[rs, cs]).astype(BF16)
                dmx = dgt * zu[rs, cs]
                dbias_ref[:, cs] += dmx
                dmb = dmx.astype(BF16)
                dws_ref[g] += _nt(dmb, vb)
                dvn_ref[rs, cs] = jnp.dot(wt_ref[g], dmb, preferred_element_type=F32)
        dvn = dvn_ref[...]
        dlng_ref[...] += jnp.sum(dvn * vhat, axis=0, keepdims=True)
        dlnb_ref[...] += jnp.sum(dvn, axis=0, keepdims=True)
        dvh = dvn * gam
        dzv = rstd * (dvh - jnp.mean(dvh, axis=-1, keepdims=True)
                      - vhat * jnp.mean(dvh * vhat, axis=-1, keepdims=True))
        da_ref[:, W:] = (dzv * gv).astype(BF16)

    const2 = lambda shape: pl.BlockSpec(shape, lambda i: (0, 0))
    const3 = pl.BlockSpec((G, LANES, LANES), lambda i: (0, 0, 0))
    return pl.pallas_call(
        body, grid=(T // tb,),
        in_specs=[pl.BlockSpec((tb, W2), lambda i: (i, 0)), pl.BlockSpec((tb, W), lambda i: (i, 0)),
                  const2((1, W)), const2((1, W)), const3, const3, const2((LANES, W))],
        out_specs=[pl.BlockSpec((tb, W2), lambda i: (i, 0)), const3, const2((LANES, W)),
                   const2((SUBLANES, W)), const2((SUBLANES, W))],
        out_shape=[jax.ShapeDtypeStruct((T, W2), BF16), jax.ShapeDtypeStruct((G, LANES, LANES), F32),
                   jax.ShapeDtypeStruct((LANES, W), F32), jax.ShapeDtypeStruct((SUBLANES, W), F32),
                   jax.ShapeDtypeStruct((SUBLANES, W), F32)],
        scratch_shapes=[pltpu.VMEM((tb, W), F32)],
        name=name, compiler_params=_cp(),
    )(a, dgated, ln_g.reshape(1, W), ln_b.reshape(1, W), w_tril, w_tril_t, bias)


def _adam_math(w, g, m, v):
    m = ADAM_B1 * m + (1.0 - ADAM_B1) * g
    v = ADAM_B2 * v + (1.0 - ADAM_B2) * (g * g)
    m_hat = m / (1.0 - ADAM_B1 ** ADAM_STEP)
    v_hat = v / (1.0 - ADAM_B2 ** ADAM_STEP)
    delta = -ADAM_LR * (m_hat / (jnp.sqrt(v_hat) + ADAM_EPS) + ADAM_WD * w)
    return delta, m, v


def _adamw_halves(mine, theirs, c_idx, w, m, v, name):
    R, C = w.shape
    rh = R // 2
    tb = _row_tile(rh, C)
    nb = rh // tb

    def body(c_ref, a_ref, b_ref, w_ref, m_ref, v_ref, g_ref, d_ref, mo_ref, vo_ref):
        g = jnp.where(pl.program_id(0) == c_ref[0], a_ref[...], b_ref[...])
        d, mm, vv = _adam_math(w_ref[...], g, m_ref[...], v_ref[...])
        g_ref[...] = g
        d_ref[...] = d
        mo_ref[...] = mm
        vo_ref[...] = vv

    half = pl.BlockSpec((tb, C), lambda h, i, c: (i, 0))
    row = pl.BlockSpec((tb, C), lambda h, i, c: (h * nb + i, 0))
    sds = jax.ShapeDtypeStruct((R, C), F32)
    return pl.pallas_call(
        body,
        grid_spec=pltpu.PrefetchScalarGridSpec(
            num_scalar_prefetch=1, grid=(2, nb), in_specs=[half, half, row, row, row], out_specs=[row] * 4),
        out_shape=[sds] * 4, name=name, compiler_params=_cp())(c_idx, mine, theirs, w, m, v)


def _adamw_sum(parts, w, m, v, name):
    K, R, C = parts.shape
    tb = _tile(R, 128, SUBLANES)

    def body(p_ref, w_ref, m_ref, v_ref, g_ref, d_ref, mo_ref, vo_ref):
        g = p_ref[0]
        for k in range(1, K):
            g = g + p_ref[k]
        d, mm, vv = _adam_math(w_ref[...], g, m_ref[...], v_ref[...])
        g_ref[...] = g
        d_ref[...] = d
        mo_ref[...] = mm
        vo_ref[...] = vv

    row = pl.BlockSpec((tb, C), lambda i: (i, 0))
    sds = jax.ShapeDtypeStruct((R, C), F32)
    return pl.pallas_call(
        body, grid=(R // tb,),
        in_specs=[pl.BlockSpec((K, tb, C), lambda i: (0, i, 0)), row, row, row],
        out_specs=[row] * 4, out_shape=[sds] * 4, name=name, compiler_params=_cp())(parts, w, m, v)


def _pair_sum(g_all, recv, c_idx, name):
    K, R, C = g_all.shape
    rh = R // 2
    tb = _row_tile(rh, C)
    nb = rh // tb

    def body(c_ref, a_ref, b_ref, o_ref):
        o_ref[...] = (a_ref[...] + b_ref[...]).astype(BF16)

    return pl.pallas_call(
        body,
        grid_spec=pltpu.PrefetchScalarGridSpec(
            num_scalar_prefetch=1, grid=(K, nb),
            in_specs=[pl.BlockSpec((1, tb, C), lambda k, i, c: (k, c[0] * nb + i, 0)),
                      pl.BlockSpec((1, tb, C), lambda k, i, c: (k, i, 0))],
            out_specs=pl.BlockSpec((1, tb, C), lambda k, i, c: (k, i, 0))),
        out_shape=jax.ShapeDtypeStruct((K, rh, C), BF16), name=name, compiler_params=_cp(),
    )(c_idx, g_all, recv)


def _sum_parts(parts, name):
    K, R, C = parts.shape
    tb = _row_tile(R, C)

    def body(p_ref, o_ref):
        g = p_ref[0].astype(F32)
        for k in range(1, K):
            g = g + p_ref[k].astype(F32)
        o_ref[...] = g

    return pl.pallas_call(
        body, grid=(R // tb,), in_specs=[pl.BlockSpec((K, tb, C), lambda i: (0, i, 0))],
        out_specs=pl.BlockSpec((tb, C), lambda i: (i, 0)),
        out_shape=jax.ShapeDtypeStruct((R, C), F32), name=name, compiler_params=_cp())(parts)


_CHIP_RELATIONS = ((1, 0), (0, 1), (1, 1))


def _position():
    return lax.axis_index("x"), lax.axis_index("y"), lax.axis_index("c")


def _flip(v, bit):
    return 1 - v if bit else v


def _gather_weights(w_pack, side, name):
    R, C = w_pack.shape
    rh = R // 2
    n_side = 0 if side is None else 1

    def half(c):
        return pl.ds(pl.multiple_of(c * rh, 16), rh)

    def between_chips(*refs):
        if n_side:
            w_ref, s_ref, ow_ref, os_ref, local_sem, send_sems, recv_sems = refs
        else:
            w_ref, ow_ref, send_sems, recv_sems = refs
        x, y, c = _position()
        me = 2 * x + y
        if n_side:
            own_side = pltpu.make_async_copy(s_ref, os_ref.at[me], local_sem)
            own_side.start()

        def copies(r, slot):
            dx, dy = _CHIP_RELATIONS[r]
            peer = (_flip(x, dx), _flip(y, dy), c)
            out = [pltpu.make_async_remote_copy(
                src_ref=w_ref.at[half(c), :], dst_ref=ow_ref.at[slot, half(c), :], send_sem=send_sems.at[2 * r],
                recv_sem=recv_sems.at[2 * r], device_id=peer, device_id_type=MESH)]
            if n_side:
                out.append(pltpu.make_async_remote_copy(
                    src_ref=s_ref, dst_ref=os_ref.at[slot], send_sem=send_sems.at[2 * r + 1],
                    recv_sem=recv_sems.at[2 * r + 1], device_id=peer, device_id_type=MESH))
            return out

        sent = [cp for r in range(3) for cp in copies(r, me)]
        for cp in sent:
            cp.start()
        for r in range(3):
            dx, dy = _CHIP_RELATIONS[r]
            for cp in copies(r, 2 * _flip(x, dx) + _flip(y, dy)):
                cp.wait_recv()
        for cp in sent:
            cp.wait_send()
        if n_side:
            own_side.wait()

    sems = [pltpu.SemaphoreType.DMA((6,)), pltpu.SemaphoreType.DMA((6,))]
    gathered = jax.ShapeDtypeStruct((4, R, C), w_pack.dtype)
    if n_side:
        halves, sides = pl.pallas_call(
            between_chips, in_specs=[_hbm(), _hbm()], out_specs=[_hbm(), _hbm()],
            out_shape=[gathered, jax.ShapeDtypeStruct((4,) + side.shape, side.dtype)],
            scratch_shapes=[pltpu.SemaphoreType.DMA(())] + sems,
            name=name + "_ici", compiler_params=_cp(),
        )(w_pack, side)
    else:
        sides = None
        halves = pl.pallas_call(
            between_chips, in_specs=[_hbm()], out_specs=_hbm(), out_shape=gathered, scratch_shapes=sems,
            name=name + "_ici", compiler_params=_cp(),
        )(w_pack)

    def to_sibling(g_ref, o_ref, send_sems, recv_sems):
        x, y, c = _position()

        def copy(r, rows):
            dx, dy = _CHIP_RELATIONS[r]
            slot = 2 * _flip(x, dx) + _flip(y, dy)
            return pltpu.make_async_remote_copy(
                src_ref=g_ref.at[slot, rows, :], dst_ref=o_ref.at[slot, rows, :], send_sem=send_sems.at[r],
                recv_sem=recv_sems.at[r], device_id=(x, y, 1 - c), device_id_type=MESH)

        sent = [copy(r, half(c)) for r in range(3)]
        for cp in sent:
            cp.start()
        for r in range(3):
            copy(r, half(1 - c)).wait_recv()
        for cp in sent:
            cp.wait_send()

    full = pl.pallas_call(
        to_sibling, in_specs=[_hbm()], out_specs=_hbm(), input_output_aliases={0: 0},
        out_shape=jax.ShapeDtypeStruct((4, R, C), w_pack.dtype),
        scratch_shapes=[pltpu.SemaphoreType.DMA((3,)), pltpu.SemaphoreType.DMA((3,))],
        name=name + "_d2d", compiler_params=_cp(),
    )(halves)
    return full, sides


def _sibling_halves(g_all, name):
    K, R, C = g_all.shape
    rh = R // 2

    def body(g_ref, o_ref, send_sem, recv_sem):
        x, y, c = _position()
        start = pl.multiple_of((1 - c) * rh, SUBLANES)
        cp = pltpu.make_async_remote_copy(
            src_ref=g_ref.at[:, pl.ds(start, rh), :], dst_ref=o_ref, send_sem=send_sem, recv_sem=recv_sem,
            device_id=(x, y, 1 - c), device_id_type=MESH)
        cp.start()
        cp.wait_recv()
        cp.wait_send()

    return pl.pallas_call(
        body, in_specs=[_hbm()], out_specs=_hbm(),
        out_shape=jax.ShapeDtypeStruct((K, rh, C), F32),
        scratch_shapes=[pltpu.SemaphoreType.DMA(()), pltpu.SemaphoreType.DMA(())],
        name=name, compiler_params=_cp(),
    )(g_all)


def _chip_exchange(parts, name):
    K, R, C = parts.shape

    def body(p_ref, o_ref, local_sem, send_sems, recv_sems):
        x, y, c = _position()
        me = 2 * x + y
        own = pltpu.make_async_copy(p_ref.at[me], o_ref.at[me], local_sem)
        own.start()

        def copy(r, src_slot, dst_slot):
            dx, dy = _CHIP_RELATIONS[r]
            return pltpu.make_async_remote_copy(
                src_ref=p_ref.at[src_slot], dst_ref=o_ref.at[dst_slot], send_sem=send_sems.at[r],
                recv_sem=recv_sems.at[r], device_id=(_flip(x, dx), _flip(y, dy), c), device_id_type=MESH)

        def chip(r):
            dx, dy = _CHIP_RELATIONS[r]
            return 2 * _flip(x, dx) + _flip(y, dy)

        sent = [copy(r, chip(r), me) for r in range(3)]
        for cp in sent:
            cp.start()
        for r in range(3):
            copy(r, me, chip(r)).wait_recv()
        for cp in sent:
            cp.wait_send()
        own.wait()

    return pl.pallas_call(
        body, in_specs=[_hbm()], out_specs=_hbm(),
        out_shape=jax.ShapeDtypeStruct((K, R, C), parts.dtype),
        scratch_shapes=[pltpu.SemaphoreType.DMA(()), pltpu.SemaphoreType.DMA((3,)),
                        pltpu.SemaphoreType.DMA((3,))],
        name=name, compiler_params=_cp(),
    )(parts)


def _swap_with_sibling(half, name):
    rh, C = half.shape

    def body(h_ref, o_ref, send_sem, recv_sem):
        x, y, c = _position()
        cp = pltpu.make_async_remote_copy(
            src_ref=h_ref, dst_ref=o_ref, send_sem=send_sem, recv_sem=recv_sem,
            device_id=(x, y, 1 - c), device_id_type=MESH)
        cp.start()
        cp.wait_recv()
        cp.wait_send()

    return pl.pallas_call(
        body, in_specs=[_hbm()], out_specs=_hbm(),
        out_shape=jax.ShapeDtypeStruct((rh, C), F32),
        scratch_shapes=[pltpu.SemaphoreType.DMA(()), pltpu.SemaphoreType.DMA(())],
        name=name, compiler_params=_cp(),
    )(half)


def _gather_all(part, name):
    R, C = part.shape
    masks = [(b >> 2 & 1, b >> 1 & 1, b & 1) for b in range(1, 8)]

    def body(p_ref, o_ref, local_sem, send_sems, recv_sems):
        x, y, c = _position()
        me = 4 * x + 2 * y + c
        own = pltpu.make_async_copy(p_ref, o_ref.at[me], local_sem)
        own.start()

        def copy(r, slot):
            dx, dy, dc = masks[r]
            return pltpu.make_async_remote_copy(
                src_ref=p_ref, dst_ref=o_ref.at[slot], send_sem=send_sems.at[r], recv_sem=recv_sems.at[r],
                device_id=(_flip(x, dx), _flip(y, dy), _flip(c, dc)), device_id_type=MESH)

        sent = [copy(r, me) for r in range(7)]
        for cp in sent:
            cp.start()
        for r in range(7):
            dx, dy, dc = masks[r]
            copy(r, 4 * _flip(x, dx) + 2 * _flip(y, dy) + _flip(c, dc)).wait_recv()
        for cp in sent:
            cp.wait_send()
        own.wait()

    return pl.pallas_call(
        body, in_specs=[_hbm()], out_specs=_hbm(),
        out_shape=jax.ShapeDtypeStruct((8, R, C), F32),
        scratch_shapes=[pltpu.SemaphoreType.DMA(()), pltpu.SemaphoreType.DMA((7,)),
                        pltpu.SemaphoreType.DMA((7,))],
        name=name, compiler_params=_cp(),
    )(part)


def _pack(arrs, row_mult, cols=PACK_COLS, lead=0):
    head = arrs[0].shape[:lead]
    pieces = []
    for a in arrs:
        flat = a.astype(F32).reshape(head + (-1,))
        fill = -flat.shape[-1] % cols
        if fill:
            flat = jnp.concatenate([flat, jnp.zeros(head + (fill,), F32)], axis=-1)
        pieces.append(flat.reshape(head + (-1, cols)))
    rows = sum(p.shape[lead] for p in pieces)
    fill = -rows % row_mult
    if fill:
        pieces.append(jnp.zeros(head + (fill, cols), F32))
    return jnp.concatenate(pieces, axis=lead) if len(pieces) > 1 else pieces[0]


def _unpack(buf, shapes):
    lead = buf.shape[:-2]
    cols = buf.shape[-1]
    out, off = [], 0
    for shp in shapes:
        n = math.prod(shp)
        rows = -(-n // cols)
        piece = buf[..., off:off + rows, :]
        if rows * cols != n:
            piece = piece.reshape(lead + (-1,))[..., :n]
        out.append(piece.reshape(lead + tuple(shp)))
        off += rows
    return out


def _cols_from_chips(g):
    k, L, A, n = g.shape
    return jnp.transpose(g, (1, 2, 0, 3)).reshape(L, A, k * n)


def _rows_from_chips(g):
    k, L, n, B = g.shape
    return jnp.transpose(g, (1, 0, 2, 3)).reshape(L, k * n, B)


def _cols_to_chips(full, k=4):
    L, A, N = full.shape
    return jnp.transpose(full.reshape(L, A, k, N // k), (2, 0, 1, 3))


def _rows_to_chips(full, k=4):
    L, N, B = full.shape
    return jnp.transpose(full.reshape(L, k, N // k, B), (1, 0, 2, 3))


def kernel(x, mixer_norm_w, attn_w_in, attn_b_f, attn_w_out, sgu_w_in, sgu_ln_g, sgu_ln_b, sgu_w_s, sgu_b_s, sgu_w_out, ffn_norm_w, ffn_w_in, ffn_w_out, final_norm_w, loss_target, m_mixer_norm_w, m_attn_w_in, m_attn_b_f, m_attn_w_out, m_sgu_w_in, m_sgu_ln_g, m_sgu_ln_b, m_sgu_w_s, m_sgu_b_s, m_sgu_w_out, m_ffn_norm_w, m_ffn_w_in, m_ffn_w_out, m_final_norm_w, v_mixer_norm_w, v_attn_w_in, v_attn_b_f, v_attn_w_out, v_sgu_w_in, v_sgu_ln_g, v_sgu_ln_b, v_sgu_w_s, v_sgu_b_s, v_sgu_w_out, v_ffn_norm_w, v_ffn_w_in, v_ffn_w_out, v_final_norm_w):
    T, D = x.shape[1], x.shape[2]
    depth = mixer_norm_w.shape[0]
    H = attn_b_f.shape[1]
    P = D // LANES
    assert D % LANES == 0 and D // H == 64 and 2 * P == H and 2 * P <= LANES
    G = sgu_w_s.shape[1]
    W = sgu_w_out.shape[1] * 4
    assert sgu_w_s.shape[2] == LANES and W == G * LANES
    scale = float(D // H) ** -0.5
    f_pad = LANES
    c_idx = lax.axis_index("c").astype(jnp.int32).reshape(1)

    groups = [
        ([attn_w_out, sgu_w_in, sgu_w_out, ffn_w_out, sgu_ln_g, sgu_ln_b],
         [m_attn_w_out, m_sgu_w_in, m_sgu_w_out, m_ffn_w_out, m_sgu_ln_g, m_sgu_ln_b],
         [v_attn_w_out, v_sgu_w_in, v_sgu_w_out, v_ffn_w_out, v_sgu_ln_g, v_sgu_ln_b]),
        ([ffn_w_in], [m_ffn_w_in], [v_ffn_w_in]),
        ([attn_w_in], [m_attn_w_in], [v_attn_w_in]),
    ]
    group_cols = [D, ffn_w_in.shape[2], attn_w_in.shape[2]]
    group_shapes = [[a.shape for a in g[0]] for g in groups]
    w_packs = [_pack(g[0], 512, cols) for g, cols in zip(groups, group_cols)]
    ln_pack = _pack([sgu_ln_g, sgu_ln_b], SUBLANES)
    my_chip = 2 * lax.axis_index("x") + lax.axis_index("y")
    gathered = []
    for t, w_pack in enumerate(w_packs):
        w_pack_b = w_pack.astype(BF16)
        gat, side = _gather_weights(w_pack_b, ln_pack if t == 0 else None, f"gather_weights_{t}")
        if t == 0:
            gat_ln = side
        gathered.append(_unpack(lax.dynamic_update_index_in_dim(gat, w_pack_b, my_chip, 0), group_shapes[t]))
    (g_ao, g_si, g_so, g_fo, _, _), (g_fi,), (g_ai,) = gathered
    g_lng, g_lnb = _unpack(gat_ln, [sgu_ln_g.shape, sgu_ln_b.shape])
    w_ai = _cols_from_chips(g_ai)
    w_ai = jnp.pad(w_ai, ((0, 0), (0, 0), (0, 3 * D + f_pad - w_ai.shape[2])))
    w_ao = _rows_from_chips(g_ao)
    w_si = _cols_from_chips(g_si)
    w_so = _rows_from_chips(g_so)
    w_fo = _rows_from_chips(g_fo)
    w_fi5 = g_fi.reshape((2, 2) + g_fi.shape[1:])
    ln_g = jnp.transpose(g_lng, (1, 0, 2)).reshape(sgu_ln_g.shape[0], W)
    ln_b = jnp.transpose(g_lnb, (1, 0, 2)).reshape(sgu_ln_b.shape[0], W)
    w_tril = jnp.tril(sgu_w_s)
    w_tril_b = w_tril.astype(BF16)
    w_tril_tb = jnp.swapaxes(w_tril, 2, 3).astype(BF16)
    sgu_bias = jnp.repeat(jnp.swapaxes(sgu_b_s, 1, 2), LANES, axis=2)
    b_f_pad = jnp.pad(attn_b_f, ((0, 0), (0, LANES - H)))

    xs = x.reshape(T, D)
    saved = []
    for i in range(depth):
        j = i // 2
        h = _rmsnorm_fwd(xs, mixer_norm_w[i], f"mix_norm_{i}")
        rec = {"x_in": xs, "h": h}
        if i % 2 == 0:
            qkv = _mm(h, w_ai[j, :, :3 * D], "nn", BF16, f"attn_qkv_{i}")
            f = _mm(h, w_ai[j, :, 3 * D:], "nn", F32, f"attn_gate_{i}")
            cT, c_cols, c0T = _gate_fwd(f, b_f_pad[j:j + 1], P, f"gate_fwd_{i}")
            o, lseT = _attn_fwd(qkv, cT, P, scale, f"attn_fwd_{i}")
            x_mid = _mm(o, w_ao[j], "nn", F32, f"attn_out_{i}", res=xs)
            rec.update(qkv=qkv, f=f, c0T=c0T, c_cols=c_cols, o=o, lseT=lseT)
        else:
            a = _mm(h, w_si[j], "nn", BF16, f"sgu_in_{i}")
            gated = _sgu_fwd(a, ln_g[j], ln_b[j], w_tril_b[j], sgu_bias[j], f"sgu_fwd_{i}")
            x_mid = _mm(gated, w_so[j], "nn", F32, f"sgu_out_{i}", res=xs)
            rec.update(a=a, gated=gated)
        h2 = _rmsnorm_fwd(x_mid, ffn_norm_w[i], f"ffn_norm_{i}")
        fa, s = _ffn_in_act(h2, w_fi5, i, f"ffn_in_{i}")
        xs = _mm(s, w_fo[i], "nn", F32, f"ffn_out_{i}", res=x_mid)
        rec.update(x_mid=x_mid, h2=h2, fa=fa, s=s)
        saved.append(rec)

    gx, loss_acc, dw_final = _loss_head(xs, final_norm_w, loss_target.reshape(T, D), "loss_head")
    loss = lax.psum(loss_acc[0, 0], ("x", "y", "c"))

    n_attn, n_sgu = attn_w_in.shape[0], sgu_w_in.shape[0]
    d_mixer_norm, d_ffn_norm = [None] * depth, [None] * depth
    d_ai, d_ao, d_bf = [None] * n_attn, [None] * n_attn, [None] * n_attn
    d_si, d_so, d_lng, d_lnb, d_ws, d_bs = ([None] * n_sgu for _ in range(6))
    d_fi, d_fo = [None] * depth, [None] * depth
    for i in reversed(range(depth)):
        j = i // 2
        rec = saved[i]
        d_fo[i] = _mm(rec["s"], gx, "tn", F32, f"ffn_out_wgrad_{i}")
        da = _ffn_out_bwd_act(gx, w_fo[i], rec["fa"], f"ffn_out_bwd_{i}")
        da = da.reshape((4,) + da.shape[2:])
        d_fi[i] = _mm_tn_shards(rec["h2"], da, f"ffn_in_wgrad_{i}")
        gx, dwn = _nt_norm_bwd(da, g_fi, i, rec["x_mid"], ffn_norm_w[i], gx, f"ffn_in_bwd_{i}")
        d_ffn_norm[i] = dwn[0]
        if i % 2 == 0:
            do = _mm(gx, w_ao[j], "nt", BF16, f"attn_out_bwd_{i}")
            d_ao[j] = _mm(rec["o"], gx, "tn", F32, f"attn_out_wgrad_{i}")
            dT = _attn_delta(do, rec["o"], P, f"attn_delta_{i}")
            dq, dk, dv, dc_cols, drowT = _attn_bwd(rec["qkv"], do, rec["lseT"], dT, rec["c0T"], rec["c_cols"],
                                                   P, scale, f"attn_bwd_{i}")
            df, dbf = _gate_bwd(dc_cols, drowT, rec["f"], b_f_pad[j:j + 1], P, f"gate_bwd_{i}")
            d_bf[j] = dbf[0, :H]
            dproj = jnp.concatenate([dq, dk, dv, df.astype(BF16)], axis=1)
            d_ai[j] = _mm(rec["h"], dproj, "tn", F32, f"attn_in_wgrad_{i}")[:, :3 * D + H]
            dmix, w_mix = dproj, w_ai
        else:
            dgated = _mm(gx, w_so[j], "nt", F32, f"sgu_out_bwd_{i}")
            d_so[j] = _mm(rec["gated"], gx, "tn", F32, f"sgu_out_wgrad_{i}")
            da_s, dws, dbias, dlng, dlnb = _sgu_bwd(rec["a"], dgated, ln_g[j], ln_b[j], w_tril_b[j],
                                                    w_tril_tb[j], sgu_bias[j], f"sgu_bwd_{i}")
            d_ws[j] = jnp.tril(dws)
            d_bs[j] = jnp.sum(dbias.reshape(LANES, G, LANES), axis=2).T
            d_lng[j], d_lnb[j] = dlng[0], dlnb[0]
            d_si[j] = _mm(rec["h"], da_s, "tn", F32, f"sgu_in_wgrad_{i}")
            dmix, w_mix = da_s, w_si
        gx, dwn = _nt_norm_bwd(dmix[None], w_mix[None], j, rec["x_in"], mixer_norm_w[i], gx, f"mixer_in_bwd_{i}")
        d_mixer_norm[i] = dwn[0]
    grad_x = gx.reshape(x.shape)

    group_grads = [
        [_rows_to_chips(jnp.stack(d_ao)), _cols_to_chips(jnp.stack(d_si)), _rows_to_chips(jnp.stack(d_so)),
         _rows_to_chips(jnp.stack(d_fo)),
         jnp.transpose(jnp.stack(d_lng).reshape(n_sgu, 4, W // 4), (1, 0, 2)),
         jnp.transpose(jnp.stack(d_lnb).reshape(n_sgu, 4, W // 4), (1, 0, 2))],
        [jnp.stack(d_fi, axis=1)],
        [_cols_to_chips(jnp.stack(d_ai))],
    ]
    reduced = []
    for t, (grads, cols) in enumerate(zip(group_grads, group_cols)):
        g_all = _pack(grads, 512, cols, lead=1)
        from_sibling = _sibling_halves(g_all, f"grad_sibling_halves_{t}")
        pair = _pair_sum(g_all, from_sibling, c_idx, f"grad_pair_sum_{t}")
        from_chips = _chip_exchange(pair, f"grad_chip_exchange_{t}")
        my_half = _sum_parts(from_chips, f"grad_chip_sum_{t}")
        sibling_half = _swap_with_sibling(my_half, f"grad_swap_halves_{t}")
        packs = _adamw_halves(my_half, sibling_half, c_idx, w_packs[t], _pack(groups[t][1], 512, cols),
                              _pack(groups[t][2], 512, cols), f"adamw_sharded_{t}")
        reduced.append([_unpack(p, group_shapes[t]) for p in packs])

    def sharded_outputs(which):
        (ao, si, so, fo, lng, lnb), (fi,), (ai,) = (reduced[t][which] for t in range(3))
        return [ai, ao, si, so, fi, fo, lng, lnb]

    g_sh, d_sh, m_sh, v_sh = (sharded_outputs(w) for w in range(4))

    repl = [mixer_norm_w, attn_b_f, sgu_w_s, sgu_b_s, ffn_norm_w, final_norm_w]
    repl_m = [m_mixer_norm_w, m_attn_b_f, m_sgu_w_s, m_sgu_b_s, m_ffn_norm_w, m_final_norm_w]
    repl_v = [v_mixer_norm_w, v_attn_b_f, v_sgu_w_s, v_sgu_b_s, v_ffn_norm_w, v_final_norm_w]
    repl_shapes = [a.shape for a in repl]
    repl_grads = [jnp.stack(d_mixer_norm), jnp.stack(d_bf), jnp.stack(d_ws), jnp.stack(d_bs),
                  jnp.stack(d_ffn_norm), dw_final[0]]
    parts = _gather_all(_pack(repl_grads, SUBLANES), "grad_gather_replicated")
    g_rep, d_rep, m_rep, v_rep = _adamw_sum(parts, _pack(repl, SUBLANES), _pack(repl_m, SUBLANES),
                                            _pack(repl_v, SUBLANES), "adamw_replicated")
    g_r = _unpack(g_rep, repl_shapes)
    d_r = _unpack(d_rep, repl_shapes)
    m_r = _unpack(m_rep, repl_shapes)
    v_r = _unpack(v_rep, repl_shapes)

    def ordered(sh, rp):
        ai, ao, si, so, fi, fo, lng, lnb = sh
        mn, bf, ws, bs, fn, fin = rp
        return [mn, ai, bf, ao, si, lng, lnb, ws, bs, so, fn, fi, fo, fin]

    return (loss, grad_x, *ordered(g_sh, g_r), *ordered(d_sh, d_r), *ordered(m_sh, m_r), *ordered(v_sh, v_r))
```

```python
import functools
import math

import jax
import jax.numpy as jnp
from jax import lax
from jax.experimental import pallas as pl
from jax.experimental.pallas import tpu as pltpu

F32 = jnp.float32
BF16 = jnp.bfloat16
NORM_EPS = 1e-6
LN_EPS = 1e-5
ADAM_LR = 0.001
ADAM_B1 = 0.9
ADAM_B2 = 0.999
ADAM_EPS = 1e-08
ADAM_WD = 0.01
ADAM_STEP = 10

LANES = 128
SUBLANES = 8
PACK_COLS = 1024
VMEM_LIMIT = 56 * 1024 * 1024
NEG_BIG = -1e30
MESH = pl.DeviceIdType.MESH


def _cp():
    return pltpu.CompilerParams(vmem_limit_bytes=VMEM_LIMIT)


def _tile(n, cap, mult):
    best = None
    d = mult
    while d <= min(n, cap):
        if n % d == 0:
            best = d
        d += mult
    return n if best is None else best


def _row_tile(rows, cols):
    cap = max(16, (512 * 1024 // cols) // 16 * 16)
    return _tile(rows, cap, 16)


def _hbm():
    return pl.BlockSpec(memory_space=pltpu.HBM)


def _nt_norm_bwd(a3, b4, layer, x, w, dres, name):
    S, T, Ks = a3.shape
    D = x.shape[1]
    tm = _tile(T, 256, 16)

    def body(a_ref, b_ref, x_ref, w_ref, dres_ref, dx_ref, dw_ref):
        @pl.when(pl.program_id(0) == 0)
        def _():
            dw_ref[...] = jnp.zeros_like(dw_ref)

        dh = _nt(a_ref[0].astype(BF16), b_ref[0, 0])
        for s in range(1, S):
            dh = dh + _nt(a_ref[s].astype(BF16), b_ref[s, 0])
        xf = x_ref[...]
        r = lax.rsqrt(jnp.mean(xf * xf, axis=-1, keepdims=True) + NORM_EPS)
        xhat = xf * r
        dxhat = dh * w_ref[...]
        dx_ref[...] = dres_ref[...] + r * (dxhat - xhat * jnp.mean(dxhat * xhat, axis=-1, keepdims=True))
        dw_ref[...] += jnp.sum(dh * xhat, axis=0, keepdims=True)

    row = pl.BlockSpec((tm, D), lambda i: (i, 0))
    return pl.pallas_call(
        body, grid=(T // tm,),
        in_specs=[pl.BlockSpec((S, tm, Ks), lambda i: (0, i, 0)),
                  pl.BlockSpec((S, 1, D, Ks), lambda i: (0, layer, 0, 0)),
                  row, pl.BlockSpec((1, D), lambda i: (0, 0)), row],
        out_specs=[row, pl.BlockSpec((SUBLANES, D), lambda i: (0, 0))],
        out_shape=[jax.ShapeDtypeStruct((T, D), F32), jax.ShapeDtypeStruct((SUBLANES, D), F32)],
        name=name, compiler_params=_cp(),
    )(a3, b4, x, w.reshape(1, D), dres)


def _mm(a, b, mode, out_dtype, name, res=None):
    if mode == "tn":
        kt, M = a.shape
        N = b.shape[1]
        tm = _tile(M, 1408, LANES)
        tn = _tile(N, 1408, LANES)
        tk = _tile(kt, 1024, 16)

        def body(a_ref, b_ref, o_ref):
            @pl.when(pl.program_id(2) == 0)
            def _():
                o_ref[...] = jnp.zeros_like(o_ref)

            o_ref[...] += lax.dot_general(
                a_ref[...].astype(BF16), b_ref[...].astype(BF16), (((0,), (0,)), ((), ())),
                preferred_element_type=F32)

        return pl.pallas_call(
            body, grid=(M // tm, N // tn, kt // tk),
            in_specs=[pl.BlockSpec((tk, tm), lambda i, j, k: (k, i)),
                      pl.BlockSpec((tk, tn), lambda i, j, k: (k, j))],
            out_specs=pl.BlockSpec((tm, tn), lambda i, j, k: (i, j)),
            out_shape=jax.ShapeDtypeStruct((M, N), F32), name=name, compiler_params=_cp(),
        )(a, b)

    M, K = a.shape
    N = b.shape[1] if mode == "nn" else b.shape[0]
    tm = _tile(M, 512, 16)
    cap = min(3072, (6 << 20) // (2 * K), (4 << 20) // (tm * jnp.dtype(out_dtype).itemsize))
    tn = _tile(N, max(LANES, cap // LANES * LANES), LANES)
    dims = (((1,), (0,)), ((), ())) if mode == "nn" else (((1,), (1,)), ((), ()))

    def body(*refs):
        if res is None:
            a_ref, b_ref, o_ref = refs
        else:
            a_ref, b_ref, r_ref, o_ref = refs
        acc = lax.dot_general(a_ref[...].astype(BF16), b_ref[...].astype(BF16), dims,
                              preferred_element_type=F32)
        if res is not None:
            acc = acc + r_ref[...]
        o_ref[...] = acc.astype(out_dtype)

    b_spec = (pl.BlockSpec((K, tn), lambda j, i: (0, j)) if mode == "nn"
              else pl.BlockSpec((tn, K), lambda j, i: (j, 0)))
    in_specs = [pl.BlockSpec((tm, K), lambda j, i: (i, 0)), b_spec]
    args = [a, b]
    if res is not None:
        in_specs.append(pl.BlockSpec((tm, tn), lambda j, i: (i, j)))
        args.append(res)
    return pl.pallas_call(
        body, grid=(N // tn, M // tm), in_specs=in_specs,
        out_specs=pl.BlockSpec((tm, tn), lambda j, i: (i, j)),
        out_shape=jax.ShapeDtypeStruct((M, N), out_dtype), name=name, compiler_params=_cp(),
    )(*args)


def _normed(x_ref, w_ref):
    xf = x_ref[...]
    r = lax.rsqrt(jnp.mean(xf * xf, axis=-1, keepdims=True) + NORM_EPS)
    return (xf * r * w_ref[...]).astype(BF16)


def _ffn_in_act(x, norm_w, w5, layer, name):
    T, D = x.shape
    n = w5.shape[-1]
    tm = _tile(T, 256, 16)

    def body(x_ref, nw_ref, w_ref, h_ref, a_ref, s_ref):
        hv = _normed(x_ref, nw_ref)
        h_ref[...] = hv
        for half in range(2):
            g = jnp.dot(hv, w_ref[0, half, 0], preferred_element_type=F32)
            u = jnp.dot(hv, w_ref[1, half, 0], preferred_element_type=F32)
            a_ref[0, half] = g.astype(BF16)
            a_ref[1, half] = u.astype(BF16)
            s_ref[:, half * n:(half + 1) * n] = (g * jax.nn.sigmoid(g) * u).astype(BF16)

    return pl.pallas_call(
        body, grid=(T // tm,),
        in_specs=[pl.BlockSpec((tm, D), lambda i: (i, 0)), pl.BlockSpec((1, D), lambda i: (0, 0)),
                  pl.BlockSpec((2, 2, 1, D, n), lambda i: (0, 0, layer, 0, 0))],
        out_specs=[pl.BlockSpec((tm, D), lambda i: (i, 0)),
                   pl.BlockSpec((2, 2, tm, n), lambda i: (0, 0, i, 0)), pl.BlockSpec((tm, 2 * n), lambda i: (i, 0))],
        out_shape=[jax.ShapeDtypeStruct((T, D), BF16), jax.ShapeDtypeStruct((2, 2, T, n), BF16),
                   jax.ShapeDtypeStruct((T, 2 * n), BF16)],
        name=name, compiler_params=_cp(),
    )(x, norm_w.reshape(1, D), w5)


def _norm_mm(x, norm_w, b, splits, name):
    T, D = x.shape
    N = b.shape[1]
    assert sum(wd for wd, _ in splits) == N
    tm = _tile(T, 256, 16)

    def body(x_ref, nw_ref, b_ref, h_ref, *outs):
        hv = _normed(x_ref, nw_ref)
        h_ref[...] = hv
        off = 0
        for o_ref, (wd, dt) in zip(outs, splits):
            o_ref[...] = jnp.dot(hv, b_ref[:, off:off + wd], preferred_element_type=F32).astype(dt)
            off += wd

    return pl.pallas_call(
        body, grid=(T // tm,),
        in_specs=[pl.BlockSpec((tm, D), lambda i: (i, 0)), pl.BlockSpec((1, D), lambda i: (0, 0)),
                  pl.BlockSpec((D, N), lambda i: (0, 0))],
        out_specs=[pl.BlockSpec((tm, D), lambda i: (i, 0))] + [pl.BlockSpec((tm, wd), lambda i: (i, 0)) for wd, _ in splits],
        out_shape=[jax.ShapeDtypeStruct((T, D), BF16)] + [jax.ShapeDtypeStruct((T, wd), dt) for wd, dt in splits],
        name=name, compiler_params=_cp(),
    )(x, norm_w.reshape(1, D), b)


def _ffn_out_bwd_act(gx, w_out, a4, name):
    T, D = gx.shape
    n = a4.shape[-1]
    tm = _tile(T, 256, 16)

    def body(gx_ref, w_ref, a_ref, da_ref):
        ds = _nt(gx_ref[...].astype(BF16), w_ref[...])
        g = a_ref[0, 0].astype(F32)
        u = a_ref[1, 0].astype(F32)
        sg = jax.nn.sigmoid(g)
        da_ref[0, 0] = (ds * u * (sg * (1.0 + g * (1.0 - sg)))).astype(BF16)
        da_ref[1, 0] = (ds * (g * sg)).astype(BF16)

    blk = pl.BlockSpec((2, 1, tm, n), lambda j, i: (0, j, i, 0))
    return pl.pallas_call(
        body, grid=(2, T // tm),
        in_specs=[pl.BlockSpec((tm, D), lambda j, i: (i, 0)), pl.BlockSpec((n, D), lambda j, i: (j, 0)), blk],
        out_specs=blk,
        out_shape=jax.ShapeDtypeStruct((2, 2, T, n), BF16), name=name, compiler_params=_cp(),
    )(gx, w_out, a4)


def _mm_tn_shards(h, a4, name):
    K, T, n = a4.shape
    D = h.shape[1]
    tk = _tile(T, 1024, 16)

    def body(h_ref, a_ref, o_ref):
        @pl.when(pl.program_id(1) == 0)
        def _():
            o_ref[...] = jnp.zeros_like(o_ref)

        o_ref[0] += lax.dot_general(h_ref[...], a_ref[0], (((0,), (0,)), ((), ())), preferred_element_type=F32)

    return pl.pallas_call(
        body, grid=(K, T // tk),
        in_specs=[pl.BlockSpec((tk, D), lambda k, t: (t, 0)), pl.BlockSpec((1, tk, n), lambda k, t: (k, t, 0))],
        out_specs=pl.BlockSpec((1, D, n), lambda k, t: (k, 0, 0)),
        out_shape=jax.ShapeDtypeStruct((K, D, n), F32), name=name, compiler_params=_cp(),
    )(h, a4)


def _loss_head(x, w, tgt, name):
    T, D = x.shape
    tm = _tile(T, 512, SUBLANES)

    def body(x_ref, w_ref, t_ref, dx_ref, loss_ref, dw_ref):
        @pl.when(pl.program_id(0) == 0)
        def _():
            loss_ref[...] = jnp.zeros_like(loss_ref)
            dw_ref[...] = jnp.zeros_like(dw_ref)

        xf = x_ref[...]
        wv = w_ref[...]
        r = lax.rsqrt(jnp.mean(xf * xf, axis=-1, keepdims=True) + NORM_EPS)
        xhat = xf * r
        err = xhat * wv - t_ref[...]
        per_tok = jnp.mean(err * err, axis=-1, keepdims=True)
        loss_ref[...] += 0.5 * jnp.sum(per_tok, axis=0, keepdims=True)
        dy = err * (1.0 / D)
        dxhat = dy * wv
        dx_ref[...] = r * (dxhat - xhat * jnp.mean(dxhat * xhat, axis=-1, keepdims=True))
        dw_ref[...] += jnp.sum(dy * xhat, axis=0, keepdims=True)

    row = pl.BlockSpec((tm, D), lambda i: (i, 0))
    return pl.pallas_call(
        body, grid=(T // tm,),
        in_specs=[row, pl.BlockSpec((1, D), lambda i: (0, 0)), row],
        out_specs=[row, pl.BlockSpec((SUBLANES, LANES), lambda i: (0, 0)),
                   pl.BlockSpec((SUBLANES, D), lambda i: (0, 0))],
        out_shape=[jax.ShapeDtypeStruct((T, D), F32), jax.ShapeDtypeStruct((SUBLANES, LANES), F32),
                   jax.ShapeDtypeStruct((SUBLANES, D), F32)],
        name=name, compiler_params=_cp(),
    )(x, w.reshape(1, D), tgt)


def _split3(v):
    hi = v.astype(BF16)
    r1 = v - hi.astype(F32)
    mid = r1.astype(BF16)
    lo = (r1 - mid.astype(F32)).astype(BF16)
    return hi, mid, lo


def _tri_dot(tri, v):
    out = None
    for piece in _split3(v):
        t = jnp.dot(tri, piece, preferred_element_type=F32)
        out = t if out is None else out + t
    return out


def _q_block(T):
    return _tile(T, 256, LANES)


def _gate_fwd(f, b_f, P, name):
    T = f.shape[0]
    tb = _q_block(T)

    def body(f_ref, b_ref, ct_ref, cc_ref, c0_ref, carry):
        @pl.when(pl.program_id(0) == 0)
        def _():
            carry[...] = jnp.zeros_like(carry)

        z = f_ref[...] + b_ref[...]
        logf = jnp.minimum(z, 0.0) - jnp.log(1.0 + jnp.exp(-jnp.abs(z)))
        row = lax.broadcasted_iota(jnp.int32, (tb, tb), 0)
        col = lax.broadcasted_iota(jnp.int32, (tb, tb), 1)
        tri = (col <= row).astype(BF16)
        c = _tri_dot(tri, logf) + carry[0:1, :]
        carry[...] = jnp.broadcast_to(c[tb - 1:tb, :], carry.shape)
        first = jnp.broadcast_to(c[0:1, :], c.shape)
        for p in range(P):
            shifted = c if p == 0 else pltpu.roll(c, LANES - 2 * p, 1)
            cc_ref[p] = shifted
            ct_ref[p] = shifted.T[0:SUBLANES, :]
            c0_ref[p] = (first if p == 0 else pltpu.roll(first, LANES - 2 * p, 1)).T[0:SUBLANES, :]

    rows = pl.BlockSpec((P, SUBLANES, tb), lambda i: (0, 0, i))
    return pl.pallas_call(
        body, grid=(T // tb,),
        in_specs=[pl.BlockSpec((tb, LANES), lambda i: (i, 0)), pl.BlockSpec((1, LANES), lambda i: (0, 0))],
        out_specs=[rows, pl.BlockSpec((P, tb, LANES), lambda i: (0, i, 0)), rows],
        out_shape=[jax.ShapeDtypeStruct((P, SUBLANES, T), F32), jax.ShapeDtypeStruct((P, T, LANES), F32),
                   jax.ShapeDtypeStruct((P, SUBLANES, T), F32)],
        scratch_shapes=[pltpu.VMEM((SUBLANES, LANES), F32)],
        name=name, compiler_params=_cp(),
    )(f, b_f)


def _gate_bwd(dc_cols, drowT, f, b_f, P, name):
    T = f.shape[0]
    tb = _tile(T, 256, LANES)
    nb = T // tb

    def body(dc_ref, dr_ref, f_ref, b_ref, df_ref, db_ref, carry):
        @pl.when(pl.program_id(0) == 0)
        def _():
            carry[...] = jnp.zeros_like(carry)
            db_ref[...] = jnp.zeros_like(db_ref)

        lane = lax.broadcasted_iota(jnp.int32, (tb, LANES), 1)
        dc = jnp.zeros((tb, LANES), F32)
        for p in range(P):
            rows = jnp.concatenate([dr_ref[p], jnp.zeros((LANES - SUBLANES, tb), F32)], axis=0)
            part = jnp.where(lane < 2, dc_ref[p] + rows.T, 0.0)
            dc = dc + (part if p == 0 else pltpu.roll(part, 2 * p, 1))
        row = lax.broadcasted_iota(jnp.int32, (tb, tb), 0)
        col = lax.broadcasted_iota(jnp.int32, (tb, tb), 1)
        tri = (col >= row).astype(BF16)
        dlogf = _tri_dot(tri, dc) + carry[0:1, :]
        carry[...] = jnp.broadcast_to(dlogf[0:1, :], carry.shape)
        z = f_ref[...] + b_ref[...]
        df = jnp.where(lane < 2 * P, dlogf * jax.nn.sigmoid(-z), 0.0)
        df_ref[...] = df
        db_ref[...] += jnp.sum(df, axis=0, keepdims=True)

    return pl.pallas_call(
        body, grid=(nb,),
        in_specs=[pl.BlockSpec((P, tb, LANES), lambda i: (0, nb - 1 - i, 0)),
                  pl.BlockSpec((P, SUBLANES, tb), lambda i: (0, 0, nb - 1 - i)),
                  pl.BlockSpec((tb, LANES), lambda i: (nb - 1 - i, 0)),
                  pl.BlockSpec((1, LANES), lambda i: (0, 0))],
        out_specs=[pl.BlockSpec((tb, LANES), lambda i: (nb - 1 - i, 0)),
                   pl.BlockSpec((SUBLANES, LANES), lambda i: (0, 0))],
        out_shape=[jax.ShapeDtypeStruct((T, LANES), F32), jax.ShapeDtypeStruct((SUBLANES, LANES), F32)],
        scratch_shapes=[pltpu.VMEM((SUBLANES, LANES), F32)],
        name=name, compiler_params=_cp(),
    )(dc_cols, drowT, f, b_f)


def _nt(a, b):
    return lax.dot_general(a, b, (((1,), (1,)), ((), ())), preferred_element_type=F32)


def _attn_fwd(qkv, cT, P, scale, name):
    T = qkv.shape[0]
    tq = _q_block(T)
    tw = _tile(T, 8 * tq, 2 * tq)
    cw = tw // 2
    assert cw % tq == 0, "the sequence must split into chunks of whole query blocks"
    nq = T // tq

    def body(q_ref, k_ref, v_ref, c_ref, o_ref, lse_ref, s_scr):
        i = pl.program_id(1)
        lane = lax.broadcasted_iota(jnp.int32, (1, LANES), 1)
        q = (q_ref[...].astype(F32) * scale).astype(BF16)
        q_heads = (jnp.where(lane < 64, q, jnp.zeros_like(q)), jnp.where(lane >= 64, q, jnp.zeros_like(q)))
        c0 = c_ref[0, :, pl.ds(pl.multiple_of(i * tq, tq), LANES)][:, 0:1]

        def scores(start, width, a):
            bias = c0 - c_ref[0, :, pl.ds(start, width)]
            return _nt(q_heads[a], k_ref[pl.ds(start, width), :]) + bias[a:a + 1, :]

        def softmax_pv(start, width, s_of, carry):
            v = v_ref[pl.ds(start, width), :]
            one = jnp.ones_like(v)
            v_heads = (jnp.where(lane < 64, v, one), jnp.where(lane >= 64, v, one))
            new = []
            for a in range(2):
                m, acc = carry[a]
                s = s_of(a)
                m_new = jnp.maximum(m, jnp.max(s, axis=1, keepdims=True))
                p = jnp.exp(s - m_new)
                acc = jnp.exp(m - m_new) * acc + jnp.dot(p.astype(BF16), v_heads[a], preferred_element_type=F32)
                new.append((m_new, acc))
            return tuple(new)

        def fill(start, buf):
            for a in range(2):
                s_scr[2 * buf + a] = scores(start, cw, a)

        def wide(j, carry):
            base = pl.multiple_of(j * tw, tw)
            fill(base + cw, 1)
            carry = softmax_pv(base, cw, lambda a: s_scr[a], carry)
            fill(base + tw, 0)
            return softmax_pv(base + cw, cw, lambda a: s_scr[2 + a], carry)

        init = tuple((jnp.full((tq, 1), NEG_BIG, F32), jnp.zeros((tq, LANES), F32)) for _ in range(2))
        n_wide = (i * tq) // tw
        fill(0, 0)
        carry = lax.fori_loop(0, n_wide, wide, init)

        base = pl.multiple_of(n_wide * tw, tw)
        ahead = i * tq - base
        col_minus_row = (lax.broadcasted_iota(jnp.int32, (tq, cw), 1)
                         - lax.broadcasted_iota(jnp.int32, (tq, cw), 0))

        def causal(buf, first_key):
            return lambda a: jnp.where(col_minus_row <= ahead - first_key, s_scr[2 * buf + a], NEG_BIG)

        def one_chunk(cr):
            return softmax_pv(base, cw, causal(0, 0), cr)

        def two_chunks(cr):
            fill(base + cw, 1)
            cr = softmax_pv(base, cw, causal(0, 0), cr)
            return softmax_pv(base + cw, cw, causal(1, cw), cr)

        (m0, a0), (m1, a1) = lax.cond(ahead >= cw, two_chunks, one_chunk, carry)
        sums = jnp.where(lane < 64, pltpu.roll(a0, 64, 1), pltpu.roll(a1, 64, 1))
        o_ref[...] = (jnp.where(lane < 64, a0, a1) / sums).astype(BF16)
        l0, l1 = a0[:, 64:65], a1[:, 0:1]
        lse = jnp.where(lane == 0, m0 + jnp.log(l0), jnp.where(lane == 1, m1 + jnp.log(l1), 0.0))
        lse_ref[0] = lse.T[0:SUBLANES, :]

    return pl.pallas_call(
        body, grid=(P, nq),
        in_specs=[pl.BlockSpec((tq, LANES), lambda p, i: (i, p)),
                  pl.BlockSpec((T, LANES), lambda p, i: (0, P + p)),
                  pl.BlockSpec((T, LANES), lambda p, i: (0, 2 * P + p)),
                  pl.BlockSpec((1, SUBLANES, T), lambda p, i: (p, 0, 0))],
        out_specs=[pl.BlockSpec((tq, LANES), lambda p, i: (i, p)),
                   pl.BlockSpec((1, SUBLANES, tq), lambda p, i: (p, 0, i))],
        out_shape=[jax.ShapeDtypeStruct((T, LANES * P), BF16), jax.ShapeDtypeStruct((P, SUBLANES, T), F32)],
        scratch_shapes=[pltpu.VMEM((4, tq, cw), F32)],
        name=name, compiler_params=_cp(),
    )(qkv, qkv, qkv, cT)


def _attn_delta(do, o, P, name):
    T, D = o.shape
    tb = _tile(T, 256, LANES)

    def body(do_ref, o_ref, d_ref):
        lane = lax.broadcasted_iota(jnp.int32, (1, LANES), 1)
        for p in range(P):
            cols = slice(p * LANES, (p + 1) * LANES)
            prod = do_ref[:, cols].astype(F32) * o_ref[:, cols].astype(F32)
            d0 = jnp.sum(jnp.where(lane < 64, prod, 0.0), axis=1, keepdims=True)
            d1 = jnp.sum(jnp.where(lane >= 64, prod, 0.0), axis=1, keepdims=True)
            both = jnp.where(lane == 0, d0, jnp.where(lane == 1, d1, 0.0))
            d_ref[p] = both.T[0:SUBLANES, :]

    return pl.pallas_call(
        body, grid=(T // tb,),
        in_specs=[pl.BlockSpec((tb, D), lambda i: (i, 0)), pl.BlockSpec((tb, D), lambda i: (i, 0))],
        out_specs=pl.BlockSpec((P, SUBLANES, tb), lambda i: (0, 0, i)),
        out_shape=jax.ShapeDtypeStruct((P, SUBLANES, T), F32), name=name, compiler_params=_cp(),
    )(do, o)


def _attn_bwd(qkv, do, lseT, dT, c0T, c_cols, P, scale, name):
    T = qkv.shape[0]
    tq = _q_block(T)
    tw = _tile(T, 4 * tq, 2 * tq)
    cw = tw // 2
    assert cw % tq == 0, "the sequence must split into chunks of whole query blocks"
    nq = T // tq

    def body(q_ref, do_ref, k_ref, v_ref, lse_ref, d_ref, c0_ref, cc_ref,
             dq_ref, dk_ref, dv_ref, dc_ref, drow_ref, dq_acc0, dq_acc1, s_scr):
        j = pl.program_id(1)

        @pl.when(j == 0)
        def _():
            dq_acc0[...] = jnp.zeros_like(dq_acc0)
            dq_acc1[...] = jnp.zeros_like(dq_acc1)

        lane = lax.broadcasted_iota(jnp.int32, (1, LANES), 1)
        in_head = (lane < 64, lane >= 64)
        k = k_ref[...]
        v = v_ref[...]
        zero = jnp.zeros_like(k)
        one = jnp.ones_like(k)
        k_heads = tuple(jnp.where(h, k, zero) for h in in_head)
        v_heads = tuple(jnp.where(h, v, zero) for h in in_head)
        k_ones = tuple(jnp.where(h, k, one) for h in in_head)
        cc = cc_ref[0]
        c_first = (cc[0:1, 0:1], cc[0:1, 1:2])
        c_rel = (cc[:, 0:1] - c_first[0], cc[:, 1:2] - c_first[1])
        dq_accs = (dq_acc0, dq_acc1)

        def scaled_q(start, width):
            return (q_ref[pl.ds(start, width), :].astype(F32) * scale).astype(BF16)

        def block(start, width, carry, first_query=None, scores=None):
            q = scaled_q(start, width)
            q_one = jnp.ones_like(q)
            dov = do_ref[pl.ds(start, width), :]
            lse = lse_ref[0, :, pl.ds(start, width)]
            dlt = d_ref[0, :, pl.ds(start, width)]
            c0 = c0_ref[0, :, pl.ds(start, width)]
            new = []
            for a in range(2):
                dk_a, dv_a = carry[a]
                rowv = lse[a:a + 1, :] + (c_first[a] - c0[a:a + 1, :])
                st = _nt(k_heads[a], q) if scores is None else scores(a)
                pt = jnp.exp((st - c_rel[a]) - rowv)
                if first_query is not None:
                    row = lax.broadcasted_iota(jnp.int32, (tq, width), 0)
                    col = lax.broadcasted_iota(jnp.int32, (tq, width), 1)
                    pt = jnp.where(col - row >= first_query, pt, 0.0)
                dpt = _nt(v_heads[a], dov)
                dst_b = (pt * (dpt - dlt[a:a + 1, :])).astype(BF16)
                dv_a = dv_a + jnp.dot(pt.astype(BF16), dov, preferred_element_type=F32)
                dk_a = dk_a + jnp.dot(dst_b, jnp.where(in_head[a], q, q_one), preferred_element_type=F32)
                dq_accs[a][pl.ds(start, width), :] += lax.dot_general(
                    dst_b, k_ones[a], (((0,), (0,)), ((), ())), preferred_element_type=F32)
                new.append((dk_a, dv_a))
            return tuple(new)

        init = tuple((jnp.zeros((tq, LANES), F32), jnp.zeros((tq, LANES), F32)) for _ in range(2))
        first_key = j * tq
        diag = pl.multiple_of((first_key // cw) * cw, cw)
        carry = block(diag, cw, init, first_key - diag)
        first_wide = first_key // tw + 1
        carry = lax.cond(
            diag + cw < first_wide * tw,
            lambda cr: block(pl.multiple_of(diag + cw, cw), cw, cr), lambda cr: cr, carry)

        last = T // tw - 1

        def fill(trip, buf):
            q = scaled_q(pl.multiple_of(jnp.minimum(trip, last) * tw, tw), tw)
            for a in range(2):
                s_scr[2 * buf + a] = _nt(k_heads[a], q)

        def trip(i, buf, cr):
            return block(pl.multiple_of(i * tw, tw), tw, cr, scores=lambda a: s_scr[2 * buf + a])

        def two_trips(p, cr):
            i = first_wide + 2 * p
            fill(i + 1, 1)
            cr = trip(i, 0, cr)
            fill(i + 2, 0)
            return trip(i + 1, 1, cr)

        n_trips = last + 1 - first_wide
        fill(first_wide, 0)
        carry = lax.fori_loop(0, n_trips // 2, two_trips, carry)
        (dk0, dv0), (dk1, dv1) = lax.cond(n_trips % 2 == 1, lambda cr: trip(last, 0, cr), lambda cr: cr, carry)
        dk_ref[...] = jnp.where(lane < 64, dk0, dk1).astype(BF16)
        dv_ref[...] = jnp.where(lane < 64, dv0, dv1).astype(BF16)
        dc_ref[0] = jnp.where(lane == 0, -dk0[:, 64:65], jnp.where(lane == 1, -dk1[:, 0:1], 0.0))

        @pl.when(j == nq - 1)
        def _():
            def finish(i, _):
                rows = pl.ds(pl.multiple_of(i * tq, tq), tq)
                a0 = dq_acc0[rows, :]
                a1 = dq_acc1[rows, :]
                dq_ref[rows, :] = (jnp.where(lane < 64, a0, a1) * scale).astype(BF16)
                sums = jnp.where(lane == 0, a0[:, 64:65], jnp.where(lane == 1, a1[:, 0:1], 0.0))
                drow_ref[0, :, rows] = sums.T[0:SUBLANES, :]
                return 0

            lax.fori_loop(0, nq, finish, 0)

    full = lambda col: pl.BlockSpec((T, LANES), lambda p, j: (0, col(p)))
    blk = lambda col: pl.BlockSpec((tq, LANES), lambda p, j: (j, col(p)))
    rows = pl.BlockSpec((1, SUBLANES, T), lambda p, j: (p, 0, 0))
    cols = pl.BlockSpec((1, tq, LANES), lambda p, j: (p, j, 0))
    D = LANES * P
    return pl.pallas_call(
        body, grid=(P, nq),
        in_specs=[full(lambda p: p), full(lambda p: p), blk(lambda p: P + p), blk(lambda p: 2 * P + p),
                  rows, rows, rows, cols],
        out_specs=[full(lambda p: p), blk(lambda p: p), blk(lambda p: p), cols, rows],
        out_shape=[jax.ShapeDtypeStruct((T, D), BF16), jax.ShapeDtypeStruct((T, D), BF16),
                   jax.ShapeDtypeStruct((T, D), BF16), jax.ShapeDtypeStruct((P, T, LANES), F32),
                   jax.ShapeDtypeStruct((P, SUBLANES, T), F32)],
        scratch_shapes=[pltpu.VMEM((T, LANES), F32), pltpu.VMEM((T, LANES), F32), pltpu.VMEM((4, tq, tw), F32)],
        name=name, compiler_params=_cp(),
    )(qkv, do, qkv, qkv, lseT, dT, c0T, c_cols)


_SQRT_HALF = 0.7071067811865476
_INV_SQRT_2PI = 0.3989422804014327


def _gelu(v):
    return 0.5 * v * (1.0 + lax.erf(v * _SQRT_HALF))


def _gelu_and_grad(v):
    cdf = 0.5 * (1.0 + lax.erf(v * _SQRT_HALF))
    return v * cdf, cdf + v * (_INV_SQRT_2PI * jnp.exp(-0.5 * v * v))


def _sgu_fwd(a, ln_g, ln_b, w_tril, bias, name):
    T, W2 = a.shape
    W = W2 // 2
    G = w_tril.shape[0]
    tb = _tile(T, 256, LANES)

    def body(a_ref, g_ref, b_ref, w_ref, bias_ref, out_ref):
        zu = _gelu(a_ref[:, :W].astype(F32))
        zv = _gelu(a_ref[:, W:].astype(F32))
        mu = jnp.mean(zv, axis=-1, keepdims=True)
        d = zv - mu
        rstd = lax.rsqrt(jnp.mean(d * d, axis=-1, keepdims=True) + LN_EPS)
        vn = (d * rstd * g_ref[...] + b_ref[...]).astype(BF16)
        for c in range(tb // LANES):
            rs = slice(c * LANES, (c + 1) * LANES)
            for g in range(G):
                cs = slice(g * LANES, (g + 1) * LANES)
                mixed = jnp.dot(w_ref[g], vn[rs, cs], preferred_element_type=F32) + bias_ref[:, cs]
                out_ref[rs, cs] = (zu[rs, cs] * mixed).astype(BF16)

    return pl.pallas_call(
        body, grid=(T // tb,),
        in_specs=[pl.BlockSpec((tb, W2), lambda i: (i, 0)), pl.BlockSpec((1, W), lambda i: (0, 0)),
                  pl.BlockSpec((1, W), lambda i: (0, 0)), pl.BlockSpec((G, LANES, LANES), lambda i: (0, 0, 0)),
                  pl.BlockSpec((LANES, W), lambda i: (0, 0))],
        out_specs=pl.BlockSpec((tb, W), lambda i: (i, 0)),
        out_shape=jax.ShapeDtypeStruct((T, W), BF16), name=name, compiler_params=_cp(),
    )(a, ln_g.reshape(1, W), ln_b.reshape(1, W), w_tril, bias)


def _sgu_bwd(a, dgated, ln_g, ln_b, w_tril, w_tril_t, bias, name):
    T, W2 = a.shape
    W = W2 // 2
    G = w_tril.shape[0]
    tb = _tile(T, 256, LANES)

    def body(a_ref, dg_ref, g_ref, b_ref, w_ref, wt_ref, bias_ref,
             da_ref, dws_ref, dbias_ref, dlng_ref, dlnb_ref, dvn_ref):
        @pl.when(pl.program_id(0) == 0)
        def _():
            dws_ref[...] = jnp.zeros_like(dws_ref)
            dbias_ref[...] = jnp.zeros_like(dbias_ref)
            dlng_ref[...] = jnp.zeros_like(dlng_ref)
            dlnb_ref[...] = jnp.zeros_like(dlnb_ref)

        up = a_ref[:, :W].astype(F32)
        vp = a_ref[:, W:].astype(F32)
        zu, gu = _gelu_and_grad(up)
        zv, gv = _gelu_and_grad(vp)
        mu = jnp.mean(zv, axis=-1, keepdims=True)
        d = zv - mu
        rstd = lax.rsqrt(jnp.mean(d * d, axis=-1, keepdims=True) + LN_EPS)
        vhat = d * rstd
        gam = g_ref[...]
        vn = (vhat * gam + b_ref[...]).astype(BF16)
        dgated = dg_ref[...]
        for c in range(tb // LANES):
            rs = slice(c * LANES, (c + 1) * LANES)
            for g in range(G):
                cs = slice(g * LANES, (g + 1) * LANES)
                vb = vn[rs, cs]
                mixed = jnp.dot(w_ref[g], vb, preferred_element_type=F32) + bias_ref[:, cs]
                dgt = dgated[rs, cs]
                da_ref[rs, cs] = (dgt * mixed * gu[rs, cs]).astype(BF16)
                dmx = dgt * zu[rs, cs]
                dbias_ref[:, cs] += dmx
                dmb = dmx.astype(BF16)
                dws_ref[g] += _nt(dmb, vb)
                dvn_ref[rs, cs] = jnp.dot(wt_ref[g], dmb, preferred_element_type=F32)
        dvn = dvn_ref[...]
        dlng_ref[...] += jnp.sum(dvn * vhat, axis=0, keepdims=True)
        dlnb_ref[...] += jnp.sum(dvn, axis=0, keepdims=True)
        dvh = dvn * gam
        dzv = rstd * (dvh - jnp.mean(dvh, axis=-1, keepdims=True)
                      - vhat * jnp.mean(dvh * vhat, axis=-1, keepdims=True))
        da_ref[:, W:] = (dzv * gv).astype(BF16)

    const2 = lambda shape: pl.BlockSpec(shape, lambda i: (0, 0))
    const3 = pl.BlockSpec((G, LANES, LANES), lambda i: (0, 0, 0))
    return pl.pallas_call(
        body, grid=(T // tb,),
        in_specs=[pl.BlockSpec((tb, W2), lambda i: (i, 0)), pl.BlockSpec((tb, W), lambda i: (i, 0)),
                  const2((1, W)), const2((1, W)), const3, const3, const2((LANES, W))],
        out_specs=[pl.BlockSpec((tb, W2), lambda i: (i, 0)), const3, const2((LANES, W)),
                   const2((SUBLANES, W)), const2((SUBLANES, W))],
        out_shape=[jax.ShapeDtypeStruct((T, W2), BF16), jax.ShapeDtypeStruct((G, LANES, LANES), F32),
                   jax.ShapeDtypeStruct((LANES, W), F32), jax.ShapeDtypeStruct((SUBLANES, W), F32),
                   jax.ShapeDtypeStruct((SUBLANES, W), F32)],
        scratch_shapes=[pltpu.VMEM((tb, W), F32)],
        name=name, compiler_params=_cp(),
    )(a, dgated, ln_g.reshape(1, W), ln_b.reshape(1, W), w_tril, w_tril_t, bias)


def _adam_math(w, g, m, v):
    m = ADAM_B1 * m + (1.0 - ADAM_B1) * g
    v = ADAM_B2 * v + (1.0 - ADAM_B2) * (g * g)
    m_hat = m / (1.0 - ADAM_B1 ** ADAM_STEP)
    v_hat = v / (1.0 - ADAM_B2 ** ADAM_STEP)
    delta = -ADAM_LR * (m_hat / (jnp.sqrt(v_hat) + ADAM_EPS) + ADAM_WD * w)
    return delta, m, v


def _adamw_halves(mine, theirs, c_idx, w, m, v, name):
    R, C = w.shape
    rh = R // 2
    tb = _row_tile(rh, C)
    nb = rh // tb

    def body(c_ref, a_ref, b_ref, w_ref, m_ref, v_ref, g_ref, d_ref, mo_ref, vo_ref):
        g = jnp.where(pl.program_id(0) == c_ref[0], a_ref[...], b_ref[...])
        d, mm, vv = _adam_math(w_ref[...], g, m_ref[...], v_ref[...])
        g_ref[...] = g
        d_ref[...] = d
        mo_ref[...] = mm
        vo_ref[...] = vv

    half = pl.BlockSpec((tb, C), lambda h, i, c: (i, 0))
    row = pl.BlockSpec((tb, C), lambda h, i, c: (h * nb + i, 0))
    sds = jax.ShapeDtypeStruct((R, C), F32)
    return pl.pallas_call(
        body,
        grid_spec=pltpu.PrefetchScalarGridSpec(
            num_scalar_prefetch=1, grid=(2, nb), in_specs=[half, half, row, row, row], out_specs=[row] * 4),
        out_shape=[sds] * 4, name=name, compiler_params=_cp())(c_idx, mine, theirs, w, m, v)


def _adamw_sum(parts, w, m, v, name):
    K, R, C = parts.shape
    tb = _tile(R, 128, SUBLANES)

    def body(p_ref, w_ref, m_ref, v_ref, g_ref, d_ref, mo_ref, vo_ref):
        g = p_ref[0]
        for k in range(1, K):
            g = g + p_ref[k]
        d, mm, vv = _adam_math(w_ref[...], g, m_ref[...], v_ref[...])
        g_ref[...] = g
        d_ref[...] = d
        mo_ref[...] = mm
        vo_ref[...] = vv

    row = pl.BlockSpec((tb, C), lambda i: (i, 0))
    sds = jax.ShapeDtypeStruct((R, C), F32)
    return pl.pallas_call(
        body, grid=(R // tb,),
        in_specs=[pl.BlockSpec((K, tb, C), lambda i: (0, i, 0)), row, row, row],
        out_specs=[row] * 4, out_shape=[sds] * 4, name=name, compiler_params=_cp())(parts, w, m, v)


def _pair_sum(g_all, recv, c_idx, name):
    K, R, C = g_all.shape
    rh = R // 2
    tb = _row_tile(rh, C)
    nb = rh // tb

    def body(c_ref, a_ref, b_ref, o_ref):
        o_ref[...] = (a_ref[...] + b_ref[...]).astype(BF16)

    return pl.pallas_call(
        body,
        grid_spec=pltpu.PrefetchScalarGridSpec(
            num_scalar_prefetch=1, grid=(K, nb),
            in_specs=[pl.BlockSpec((1, tb, C), lambda k, i, c: (k, c[0] * nb + i, 0)),
                      pl.BlockSpec((1, tb, C), lambda k, i, c: (k, i, 0))],
            out_specs=pl.BlockSpec((1, tb, C), lambda k, i, c: (k, i, 0))),
        out_shape=jax.ShapeDtypeStruct((K, rh, C), BF16), name=name, compiler_params=_cp(),
    )(c_idx, g_all, recv)


def _sum_parts(parts, name):
    K, R, C = parts.shape
    tb = _row_tile(R, C)

    def body(p_ref, o_ref):
        g = p_ref[0].astype(F32)
        for k in range(1, K):
            g = g + p_ref[k].astype(F32)
        o_ref[...] = g

    return pl.pallas_call(
        body, grid=(R // tb,), in_specs=[pl.BlockSpec((K, tb, C), lambda i: (0, i, 0))],
        out_specs=pl.BlockSpec((tb, C), lambda i: (i, 0)),
        out_shape=jax.ShapeDtypeStruct((R, C), F32), name=name, compiler_params=_cp())(parts)


_CHIP_RELATIONS = ((1, 0), (0, 1), (1, 1))


def _position():
    return lax.axis_index("x"), lax.axis_index("y"), lax.axis_index("c")


def _flip(v, bit):
    return 1 - v if bit else v


def _gather_weights(w_pack, side, name):
    R, C = w_pack.shape
    rh = R // 2
    n_side = 0 if side is None else 1

    def half(c):
        return pl.ds(pl.multiple_of(c * rh, 16), rh)

    def between_chips(*refs):
        if n_side:
            w_ref, s_ref, ow_ref, os_ref, local_sem, send_sems, recv_sems = refs
        else:
            w_ref, ow_ref, send_sems, recv_sems = refs
        x, y, c = _position()
        me = 2 * x + y
        if n_side:
            own_side = pltpu.make_async_copy(s_ref, os_ref.at[me], local_sem)
            own_side.start()

        def copies(r, slot):
            dx, dy = _CHIP_RELATIONS[r]
            peer = (_flip(x, dx), _flip(y, dy), c)
            out = [pltpu.make_async_remote_copy(
                src_ref=w_ref.at[half(c), :], dst_ref=ow_ref.at[slot, half(c), :], send_sem=send_sems.at[2 * r],
                recv_sem=recv_sems.at[2 * r], device_id=peer, device_id_type=MESH)]
            if n_side:
                out.append(pltpu.make_async_remote_copy(
                    src_ref=s_ref, dst_ref=os_ref.at[slot], send_sem=send_sems.at[2 * r + 1],
                    recv_sem=recv_sems.at[2 * r + 1], device_id=peer, device_id_type=MESH))
            return out

        sent = [cp for r in range(3) for cp in copies(r, me)]
        for cp in sent:
            cp.start()
        for r in range(3):
            dx, dy = _CHIP_RELATIONS[r]
            for cp in copies(r, 2 * _flip(x, dx) + _flip(y, dy)):
                cp.wait_recv()
        for cp in sent:
            cp.wait_send()
        if n_side:
            own_side.wait()

    sems = [pltpu.SemaphoreType.DMA((6,)), pltpu.SemaphoreType.DMA((6,))]
    gathered = jax.ShapeDtypeStruct((4, R, C), w_pack.dtype)
    if n_side:
        halves, sides = pl.pallas_call(
            between_chips, in_specs=[_hbm(), _hbm()], out_specs=[_hbm(), _hbm()],
            out_shape=[gathered, jax.ShapeDtypeStruct((4,) + side.shape, side.dtype)],
            scratch_shapes=[pltpu.SemaphoreType.DMA(())] + sems,
            name=name + "_ici", compiler_params=_cp(),
        )(w_pack, side)
    else:
        sides = None
        halves = pl.pallas_call(
            between_chips, in_specs=[_hbm()], out_specs=_hbm(), out_shape=gathered, scratch_shapes=sems,
            name=name + "_ici", compiler_params=_cp(),
        )(w_pack)

    def to_sibling(g_ref, o_ref, send_sems, recv_sems):
        x, y, c = _position()

        def copy(r, rows):
            dx, dy = _CHIP_RELATIONS[r]
            slot = 2 * _flip(x, dx) + _flip(y, dy)
            return pltpu.make_async_remote_copy(
                src_ref=g_ref.at[slot, rows, :], dst_ref=o_ref.at[slot, rows, :], send_sem=send_sems.at[r],
                recv_sem=recv_sems.at[r], device_id=(x, y, 1 - c), device_id_type=MESH)

        sent = [copy(r, half(c)) for r in range(3)]
        for cp in sent:
            cp.start()
        for r in range(3):
            copy(r, half(1 - c)).wait_recv()
        for cp in sent:
            cp.wait_send()

    full = pl.pallas_call(
        to_sibling, in_specs=[_hbm()], out_specs=_hbm(), input_output_aliases={0: 0},
        out_shape=jax.ShapeDtypeStruct((4, R, C), w_pack.dtype),
        scratch_shapes=[pltpu.SemaphoreType.DMA((3,)), pltpu.SemaphoreType.DMA((3,))],
        name=name + "_d2d", compiler_params=_cp(),
    )(halves)
    return full, sides


def _sibling_halves(g_all, name):
    K, R, C = g_all.shape
    rh = R // 2

    def body(g_ref, o_ref, send_sem, recv_sem):
        x, y, c = _position()
        start = pl.multiple_of((1 - c) * rh, SUBLANES)
        cp = pltpu.make_async_remote_copy(
            src_ref=g_ref.at[:, pl.ds(start, rh), :], dst_ref=o_ref, send_sem=send_sem, recv_sem=recv_sem,
            device_id=(x, y, 1 - c), device_id_type=MESH)
        cp.start()
        cp.wait_recv()
        cp.wait_send()

    return pl.pallas_call(
        body, in_specs=[_hbm()], out_specs=_hbm(),
        out_shape=jax.ShapeDtypeStruct((K, rh, C), F32),
        scratch_shapes=[pltpu.SemaphoreType.DMA(()), pltpu.SemaphoreType.DMA(())],
        name=name, compiler_params=_cp(),
    )(g_all)


def _chip_exchange(parts, name):
    K, R, C = parts.shape

    def body(p_ref, o_ref, local_sem, send_sems, recv_sems):
        x, y, c = _position()
        me = 2 * x + y
        own = pltpu.make_async_copy(p_ref.at[me], o_ref.at[me], local_sem)
        own.start()

        def copy(r, src_slot, dst_slot):
            dx, dy = _CHIP_RELATIONS[r]
            return pltpu.make_async_remote_copy(
                src_ref=p_ref.at[src_slot], dst_ref=o_ref.at[dst_slot], send_sem=send_sems.at[r],
                recv_sem=recv_sems.at[r], device_id=(_flip(x, dx), _flip(y, dy), c), device_id_type=MESH)

        def chip(r):
            dx, dy = _CHIP_RELATIONS[r]
            return 2 * _flip(x, dx) + _flip(y, dy)

        sent = [copy(r, chip(r), me) for r in range(3)]
        for cp in sent:
            cp.start()
        for r in range(3):
            copy(r, me, chip(r)).wait_recv()
        for cp in sent:
            cp.wait_send()
        own.wait()

    return pl.pallas_call(
        body, in_specs=[_hbm()], out_specs=_hbm(),
        out_shape=jax.ShapeDtypeStruct((K, R, C), parts.dtype),
        scratch_shapes=[pltpu.SemaphoreType.DMA(()), pltpu.SemaphoreType.DMA((3,)),
                        pltpu.SemaphoreType.DMA((3,))],
        name=name, compiler_params=_cp(),
    )(parts)


def _swap_with_sibling(half, name):
    rh, C = half.shape

    def body(h_ref, o_ref, send_sem, recv_sem):
        x, y, c = _position()
        cp = pltpu.make_async_remote_copy(
            src_ref=h_ref, dst_ref=o_ref, send_sem=send_sem, recv_sem=recv_sem,
            device_id=(x, y, 1 - c), device_id_type=MESH)
        cp.start()
        cp.wait_recv()
        cp.wait_send()

    return pl.pallas_call(
        body, in_specs=[_hbm()], out_specs=_hbm(),
        out_shape=jax.ShapeDtypeStruct((rh, C), F32),
        scratch_shapes=[pltpu.SemaphoreType.DMA(()), pltpu.SemaphoreType.DMA(())],
        name=name, compiler_params=_cp(),
    )(half)


def _gather_all(part, name):
    R, C = part.shape
    masks = [(b >> 2 & 1, b >> 1 & 1, b & 1) for b in range(1, 8)]

    def body(p_ref, o_ref, local_sem, send_sems, recv_sems):
        x, y, c = _position()
        me = 4 * x + 2 * y + c
        own = pltpu.make_async_copy(p_ref, o_ref.at[me], local_sem)
        own.start()

        def copy(r, slot):
            dx, dy, dc = masks[r]
            return pltpu.make_async_remote_copy(
                src_ref=p_ref, dst_ref=o_ref.at[slot], send_sem=send_sems.at[r], recv_sem=recv_sems.at[r],
                device_id=(_flip(x, dx), _flip(y, dy), _flip(c, dc)), device_id_type=MESH)

        sent = [copy(r, me) for r in range(7)]
        for cp in sent:
            cp.start()
        for r in range(7):
            dx, dy, dc = masks[r]
            copy(r, 4 * _flip(x, dx) + 2 * _flip(y, dy) + _flip(c, dc)).wait_recv()
        for cp in sent:
            cp.wait_send()
        own.wait()

    return pl.pallas_call(
        body, in_specs=[_hbm()], out_specs=_hbm(),
        out_shape=jax.ShapeDtypeStruct((8, R, C), F32),
        scratch_shapes=[pltpu.SemaphoreType.DMA(()), pltpu.SemaphoreType.DMA((7,)),
                        pltpu.SemaphoreType.DMA((7,))],
        name=name, compiler_params=_cp(),
    )(part)


def _pack(arrs, row_mult, cols=PACK_COLS, lead=0):
    head = arrs[0].shape[:lead]
    pieces = []
    for a in arrs:
        flat = a.astype(F32).reshape(head + (-1,))
        fill = -flat.shape[-1] % cols
        if fill:
            flat = jnp.concatenate([flat, jnp.zeros(head + (fill,), F32)], axis=-1)
        pieces.append(flat.reshape(head + (-1, cols)))
    rows = sum(p.shape[lead] for p in pieces)
    fill = -rows % row_mult
    if fill:
        pieces.append(jnp.zeros(head + (fill, cols), F32))
    return jnp.concatenate(pieces, axis=lead) if len(pieces) > 1 else pieces[0]


def _unpack(buf, shapes):
    lead = buf.shape[:-2]
    cols = buf.shape[-1]
    out, off = [], 0
    for shp in shapes:
        n = math.prod(shp)
        rows = -(-n // cols)
        piece = buf[..., off:off + rows, :]
        if rows * cols != n:
            piece = piece.reshape(lead + (-1,))[..., :n]
        out.append(piece.reshape(lead + tuple(shp)))
        off += rows
    return out


def _cols_from_chips(g):
    k, L, A, n = g.shape
    return jnp.transpose(g, (1, 2, 0, 3)).reshape(L, A, k * n)


def _rows_from_chips(g):
    k, L, n, B = g.shape
    return jnp.transpose(g, (1, 0, 2, 3)).reshape(L, k * n, B)


def _cols_to_chips(full, k=4):
    L, A, N = full.shape
    return jnp.transpose(full.reshape(L, A, k, N // k), (2, 0, 1, 3))


def _rows_to_chips(full, k=4):
    L, N, B = full.shape
    return jnp.transpose(full.reshape(L, k, N // k, B), (1, 0, 2, 3))


def kernel(x, mixer_norm_w, attn_w_in, attn_b_f, attn_w_out, sgu_w_in, sgu_ln_g, sgu_ln_b, sgu_w_s, sgu_b_s, sgu_w_out, ffn_norm_w, ffn_w_in, ffn_w_out, final_norm_w, loss_target, m_mixer_norm_w, m_attn_w_in, m_attn_b_f, m_attn_w_out, m_sgu_w_in, m_sgu_ln_g, m_sgu_ln_b, m_sgu_w_s, m_sgu_b_s, m_sgu_w_out, m_ffn_norm_w, m_ffn_w_in, m_ffn_w_out, m_final_norm_w, v_mixer_norm_w, v_attn_w_in, v_attn_b_f, v_attn_w_out, v_sgu_w_in, v_sgu_ln_g, v_sgu_ln_b, v_sgu_w_s, v_sgu_b_s, v_sgu_w_out, v_ffn_norm_w, v_ffn_w_in, v_ffn_w_out, v_final_norm_w):
    T, D = x.shape[1], x.shape[2]
    depth = mixer_norm_w.shape[0]
    H = attn_b_f.shape[1]
    P = D // LANES
    assert D % LANES == 0 and D // H == 64 and 2 * P == H and 2 * P <= LANES
    G = sgu_w_s.shape[1]
    W = sgu_w_out.shape[1] * 4
    assert sgu_w_s.shape[2] == LANES and W == G * LANES
    scale = float(D // H) ** -0.5
    f_pad = LANES
    c_idx = lax.axis_index("c").astype(jnp.int32).reshape(1)

    groups = [
        ([attn_w_out, sgu_w_in, sgu_w_out, ffn_w_out, sgu_ln_g, sgu_ln_b],
         [m_attn_w_out, m_sgu_w_in, m_sgu_w_out, m_ffn_w_out, m_sgu_ln_g, m_sgu_ln_b],
         [v_attn_w_out, v_sgu_w_in, v_sgu_w_out, v_ffn_w_out, v_sgu_ln_g, v_sgu_ln_b]),
        ([ffn_w_in], [m_ffn_w_in], [v_ffn_w_in]),
        ([attn_w_in], [m_attn_w_in], [v_attn_w_in]),
    ]
    group_cols = [D, ffn_w_in.shape[2], attn_w_in.shape[2]]
    group_shapes = [[a.shape for a in g[0]] for g in groups]
    w_packs = [_pack(g[0], 512, cols) for g, cols in zip(groups, group_cols)]
    ln_pack = _pack([sgu_ln_g, sgu_ln_b], SUBLANES)
    my_chip = 2 * lax.axis_index("x") + lax.axis_index("y")
    gathered = []
    for t, w_pack in enumerate(w_packs):
        w_pack_b = w_pack.astype(BF16)
        gat, side = _gather_weights(w_pack_b, ln_pack if t == 0 else None, f"gather_weights_{t}")
        if t == 0:
            gat_ln = side
        gathered.append(_unpack(lax.dynamic_update_index_in_dim(gat, w_pack_b, my_chip, 0), group_shapes[t]))
    (g_ao, g_si, g_so, g_fo, _, _), (g_fi,), (g_ai,) = gathered
    g_lng, g_lnb = _unpack(gat_ln, [sgu_ln_g.shape, sgu_ln_b.shape])
    w_ai = _cols_from_chips(g_ai)
    w_ai = jnp.pad(w_ai, ((0, 0), (0, 0), (0, 3 * D + f_pad - w_ai.shape[2])))
    w_ao = _rows_from_chips(g_ao)
    w_si = _cols_from_chips(g_si)
    w_so = _rows_from_chips(g_so)
    w_fo = _rows_from_chips(g_fo)
    w_fi5 = g_fi.reshape((2, 2) + g_fi.shape[1:])
    ln_g = jnp.transpose(g_lng, (1, 0, 2)).reshape(sgu_ln_g.shape[0], W)
    ln_b = jnp.transpose(g_lnb, (1, 0, 2)).reshape(sgu_ln_b.shape[0], W)
    w_tril = jnp.tril(sgu_w_s)
    w_tril_b = w_tril.astype(BF16)
    w_tril_tb = jnp.swapaxes(w_tril, 2, 3).astype(BF16)
    sgu_bias = jnp.repeat(jnp.swapaxes(sgu_b_s, 1, 2), LANES, axis=2)
    b_f_pad = jnp.pad(attn_b_f, ((0, 0), (0, LANES - H)))

    xs = x.reshape(T, D)
    saved = []
    for i in range(depth):
        j = i // 2
        rec = {"x_in": xs}
        if i % 2 == 0:
            h, qkv, f = _norm_mm(xs, mixer_norm_w[i], w_ai[j], ((3 * D, BF16), (f_pad, F32)), f"attn_qkv_{i}")
            cT, c_cols, c0T = _gate_fwd(f, b_f_pad[j:j + 1], P, f"gate_fwd_{i}")
            o, lseT = _attn_fwd(qkv, cT, P, scale, f"attn_fwd_{i}")
            x_mid = _mm(o, w_ao[j], "nn", F32, f"attn_out_{i}", res=xs)
            rec.update(qkv=qkv, f=f, c0T=c0T, c_cols=c_cols, o=o, lseT=lseT)
        else:
            h, a = _norm_mm(xs, mixer_norm_w[i], w_si[j], ((2 * W, BF16),), f"sgu_in_{i}")
            gated = _sgu_fwd(a, ln_g[j], ln_b[j], w_tril_b[j], sgu_bias[j], f"sgu_fwd_{i}")
            x_mid = _mm(gated, w_so[j], "nn", F32, f"sgu_out_{i}", res=xs)
            rec.update(a=a, gated=gated)
        h2, fa, s = _ffn_in_act(x_mid, ffn_norm_w[i], w_fi5, i, f"ffn_in_{i}")
        xs = _mm(s, w_fo[i], "nn", F32, f"ffn_out_{i}", res=x_mid)
        rec.update(h=h, x_mid=x_mid, h2=h2, fa=fa, s=s)
        saved.append(rec)

    gx, loss_acc, dw_final = _loss_head(xs, final_norm_w, loss_target.reshape(T, D), "loss_head")
    loss = lax.psum(loss_acc[0, 0], ("x", "y", "c"))

    n_attn, n_sgu = attn_w_in.shape[0], sgu_w_in.shape[0]
    d_mixer_norm, d_ffn_norm = [None] * depth, [None] * depth
    d_ai, d_ao, d_bf = [None] * n_attn, [None] * n_attn, [None] * n_attn
    d_si, d_so, d_lng, d_lnb, d_ws, d_bs = ([None] * n_sgu for _ in range(6))
    d_fi, d_fo = [None] * depth, [None] * depth
    for i in reversed(range(depth)):
        j = i // 2
        rec = saved[i]
        d_fo[i] = _mm(rec["s"], gx, "tn", F32, f"ffn_out_wgrad_{i}")
        da = _ffn_out_bwd_act(gx, w_fo[i], rec["fa"], f"ffn_out_bwd_{i}")
        da = da.reshape((4,) + da.shape[2:])
        d_fi[i] = _mm_tn_shards(rec["h2"], da, f"ffn_in_wgrad_{i}")
        gx, dwn = _nt_norm_bwd(da, g_fi, i, rec["x_mid"], ffn_norm_w[i], gx, f"ffn_in_bwd_{i}")
        d_ffn_norm[i] = dwn[0]
        if i % 2 == 0:
            do = _mm(gx, w_ao[j], "nt", BF16, f"attn_out_bwd_{i}")
            d_ao[j] = _mm(rec["o"], gx, "tn", F32, f"attn_out_wgrad_{i}")
            dT = _attn_delta(do, rec["o"], P, f"attn_delta_{i}")
            dq, dk, dv, dc_cols, drowT = _attn_bwd(rec["qkv"], do, rec["lseT"], dT, rec["c0T"], rec["c_cols"],
                                                   P, scale, f"attn_bwd_{i}")
            df, dbf = _gate_bwd(dc_cols, drowT, rec["f"], b_f_pad[j:j + 1], P, f"gate_bwd_{i}")
            d_bf[j] = dbf[0, :H]
            dproj = jnp.concatenate([dq, dk, dv, df.astype(BF16)], axis=1)
            d_ai[j] = _mm(rec["h"], dproj, "tn", F32, f"attn_in_wgrad_{i}")[:, :3 * D + H]
            dmix, w_mix = dproj, w_ai
        else:
            dgated = _mm(gx, w_so[j], "nt", F32, f"sgu_out_bwd_{i}")
            d_so[j] = _mm(rec["gated"], gx, "tn", F32, f"sgu_out_wgrad_{i}")
            da_s, dws, dbias, dlng, dlnb = _sgu_bwd(rec["a"], dgated, ln_g[j], ln_b[j], w_tril_b[j],
                                                    w_tril_tb[j], sgu_bias[j], f"sgu_bwd_{i}")
            d_ws[j] = jnp.tril(dws)
            d_bs[j] = jnp.sum(dbias.reshape(LANES, G, LANES), axis=2).T
            d_lng[j], d_lnb[j] = dlng[0], dlnb[0]
            d_si[j] = _mm(rec["h"], da_s, "tn", F32, f"sgu_in_wgrad_{i}")
            dmix, w_mix = da_s, w_si
        gx, dwn = _nt_norm_bwd(dmix[None], w_mix[None], j, rec["x_in"], mixer_norm_w[i], gx, f"mixer_in_bwd_{i}")
        d_mixer_norm[i] = dwn[0]
    grad_x = gx.reshape(x.shape)

    group_grads = [
        [_rows_to_chips(jnp.stack(d_ao)), _cols_to_chips(jnp.stack(d_si)), _rows_to_chips(jnp.stack(d_so)),
         _rows_to_chips(jnp.stack(d_fo)),
         jnp.transpose(jnp.stack(d_lng).reshape(n_sgu, 4, W // 4), (1, 0, 2)),
         jnp.transpose(jnp.stack(d_lnb).reshape(n_sgu, 4, W // 4), (1, 0, 2))],
        [jnp.stack(d_fi, axis=1)],
        [_cols_to_chips(jnp.stack(d_ai))],
    ]
    reduced = []
    for t, (grads, cols) in enumerate(zip(group_grads, group_cols)):
        g_all = _pack(grads, 512, cols, lead=1)
        from_sibling = _sibling_halves(g_all, f"grad_sibling_halves_{t}")
        pair = _pair_sum(g_all, from_sibling, c_idx, f"grad_pair_sum_{t}")
        from_chips = _chip_exchange(pair, f"grad_chip_exchange_{t}")
        my_half = _sum_parts(from_chips, f"grad_chip_sum_{t}")
        sibling_half = _swap_with_sibling(my_half, f"grad_swap_halves_{t}")
        packs = _adamw_halves(my_half, sibling_half, c_idx, w_packs[t], _pack(groups[t][1], 512, cols),
                              _pack(groups[t][2], 512, cols), f"adamw_sharded_{t}")
        reduced.append([_unpack(p, group_shapes[t]) for p in packs])

    def sharded_outputs(which):
        (ao, si, so, fo, lng, lnb), (fi,), (ai,) = (reduced[t][which] for t in range(3))
        return [ai, ao, si, so, fi, fo, lng, lnb]

    g_sh, d_sh, m_sh, v_sh = (sharded_outputs(w) for w in range(4))

    repl = [mixer_norm_w, attn_b_f, sgu_w_s, sgu_b_s, ffn_norm_w, final_norm_w]
    repl_m = [m_mixer_norm_w, m_attn_b_f, m_sgu_w_s, m_sgu_b_s, m_ffn_norm_w, m_final_norm_w]
    repl_v = [v_mixer_norm_w, v_attn_b_f, v_sgu_w_s, v_sgu_b_s, v_ffn_norm_w, v_final_norm_w]
    repl_shapes = [a.shape for a in repl]
    repl_grads = [jnp.stack(d_mixer_norm), jnp.stack(d_bf), jnp.stack(d_ws), jnp.stack(d_bs),
                  jnp.stack(d_ffn_norm), dw_final[0]]
    parts = _gather_all(_pack(repl_grads, SUBLANES), "grad_gather_replicated")
    g_rep, d_rep, m_rep, v_rep = _adamw_sum(parts, _pack(repl, SUBLANES), _pack(repl_m, SUBLANES),
                                            _pack(repl_v, SUBLANES), "adamw_replicated")
    g_r = _unpack(g_rep, repl_shapes)
    d_r = _unpack(d_rep, repl_shapes)
    m_r = _unpack(m_rep, repl_shapes)
    v_r = _unpack(v_rep, repl_shapes)

    def ordered(sh, rp):
        ai, ao, si, so, fi, fo, lng, lnb = sh
        mn, bf, ws, bs, fn, fin = rp
        return [mn, ai, bf, ao, si, lng, lnb, ws, bs, so, fn, fi, fo, fin]

    return (loss, grad_x, *ordered(g_sh, g_r), *ordered(d_sh, d_r), *ordered(m_sh, m_r), *ordered(v_sh, v_r))
```

```python
import functools
import math

import jax
import jax.numpy as jnp
from jax import lax
from jax.experimental import pallas as pl
from jax.experimental.pallas import tpu as pltpu

F32 = jnp.float32
BF16 = jnp.bfloat16
NORM_EPS = 1e-6
LN_EPS = 1e-5
ADAM_LR = 0.001
ADAM_B1 = 0.9
ADAM_B2 = 0.999
ADAM_EPS = 1e-08
ADAM_WD = 0.01
ADAM_STEP = 10

LANES = 128
SUBLANES = 8
PACK_COLS = 1024
VMEM_LIMIT = 56 * 1024 * 1024
NEG_BIG = -1e30
LOG2E = 1.4426950408889634
MESH = pl.DeviceIdType.MESH


def _cp():
    return pltpu.CompilerParams(vmem_limit_bytes=VMEM_LIMIT)


def _tile(n, cap, mult):
    best = None
    d = mult
    while d <= min(n, cap):
        if n % d == 0:
            best = d
        d += mult
    return n if best is None else best


def _row_tile(rows, cols):
    cap = max(16, (512 * 1024 // cols) // 16 * 16)
    return _tile(rows, cap, 16)


def _hbm():
    return pl.BlockSpec(memory_space=pltpu.HBM)


def _nt_norm_bwd(a3, b4, layer, x, w, dres, name):
    S, T, Ks = a3.shape
    D = x.shape[1]
    tm = _tile(T, 256, 16)

    def body(a_ref, b_ref, x_ref, w_ref, dres_ref, dx_ref, dw_ref):
        @pl.when(pl.program_id(0) == 0)
        def _():
            dw_ref[...] = jnp.zeros_like(dw_ref)

        dh = _nt(a_ref[0].astype(BF16), b_ref[0, 0])
        for s in range(1, S):
            dh = dh + _nt(a_ref[s].astype(BF16), b_ref[s, 0])
        xf = x_ref[...]
        r = lax.rsqrt(jnp.mean(xf * xf, axis=-1, keepdims=True) + NORM_EPS)
        xhat = xf * r
        dxhat = dh * w_ref[...]
        dx_ref[...] = dres_ref[...] + r * (dxhat - xhat * jnp.mean(dxhat * xhat, axis=-1, keepdims=True))
        dw_ref[...] += jnp.sum(dh * xhat, axis=0, keepdims=True)

    row = pl.BlockSpec((tm, D), lambda i: (i, 0))
    return pl.pallas_call(
        body, grid=(T // tm,),
        in_specs=[pl.BlockSpec((S, tm, Ks), lambda i: (0, i, 0)),
                  pl.BlockSpec((S, 1, D, Ks), lambda i: (0, layer, 0, 0)),
                  row, pl.BlockSpec((1, D), lambda i: (0, 0)), row],
        out_specs=[row, pl.BlockSpec((SUBLANES, D), lambda i: (0, 0))],
        out_shape=[jax.ShapeDtypeStruct((T, D), F32), jax.ShapeDtypeStruct((SUBLANES, D), F32)],
        name=name, compiler_params=_cp(),
    )(a3, b4, x, w.reshape(1, D), dres)


def _mm(a, b, mode, out_dtype, name, res=None):
    if mode == "tn":
        kt, M = a.shape
        N = b.shape[1]
        tm = _tile(M, 1408, LANES)
        tn = _tile(N, 1408, LANES)
        tk = _tile(kt, 1024, 16)

        def body(a_ref, b_ref, o_ref):
            @pl.when(pl.program_id(2) == 0)
            def _():
                o_ref[...] = jnp.zeros_like(o_ref)

            o_ref[...] += lax.dot_general(
                a_ref[...].astype(BF16), b_ref[...].astype(BF16), (((0,), (0,)), ((), ())),
                preferred_element_type=F32)

        return pl.pallas_call(
            body, grid=(M // tm, N // tn, kt // tk),
            in_specs=[pl.BlockSpec((tk, tm), lambda i, j, k: (k, i)),
                      pl.BlockSpec((tk, tn), lambda i, j, k: (k, j))],
            out_specs=pl.BlockSpec((tm, tn), lambda i, j, k: (i, j)),
            out_shape=jax.ShapeDtypeStruct((M, N), F32), name=name, compiler_params=_cp(),
        )(a, b)

    M, K = a.shape
    N = b.shape[1] if mode == "nn" else b.shape[0]
    tm = _tile(M, 512, 16)
    cap = min(3072, (6 << 20) // (2 * K), (4 << 20) // (tm * jnp.dtype(out_dtype).itemsize))
    tn = _tile(N, max(LANES, cap // LANES * LANES), LANES)
    dims = (((1,), (0,)), ((), ())) if mode == "nn" else (((1,), (1,)), ((), ()))

    def body(*refs):
        if res is None:
            a_ref, b_ref, o_ref = refs
        else:
            a_ref, b_ref, r_ref, o_ref = refs
        acc = lax.dot_general(a_ref[...].astype(BF16), b_ref[...].astype(BF16), dims,
                              preferred_element_type=F32)
        if res is not None:
            acc = acc + r_ref[...]
        o_ref[...] = acc.astype(out_dtype)

    b_spec = (pl.BlockSpec((K, tn), lambda j, i: (0, j)) if mode == "nn"
              else pl.BlockSpec((tn, K), lambda j, i: (j, 0)))
    in_specs = [pl.BlockSpec((tm, K), lambda j, i: (i, 0)), b_spec]
    args = [a, b]
    if res is not None:
        in_specs.append(pl.BlockSpec((tm, tn), lambda j, i: (i, j)))
        args.append(res)
    return pl.pallas_call(
        body, grid=(N // tn, M // tm), in_specs=in_specs,
        out_specs=pl.BlockSpec((tm, tn), lambda j, i: (i, j)),
        out_shape=jax.ShapeDtypeStruct((M, N), out_dtype), name=name, compiler_params=_cp(),
    )(*args)


def _normed(x_ref, w_ref):
    xf = x_ref[...]
    r = lax.rsqrt(jnp.mean(xf * xf, axis=-1, keepdims=True) + NORM_EPS)
    return (xf * r * w_ref[...]).astype(BF16)


def _ffn_in_act(x, norm_w, w5, layer, name):
    T, D = x.shape
    n = w5.shape[-1]
    tm = _tile(T, 256, 16)

    def body(x_ref, nw_ref, w_ref, h_ref, a_ref, s_ref):
        hv = _normed(x_ref, nw_ref)
        h_ref[...] = hv
        for half in range(2):
            g = jnp.dot(hv, w_ref[0, half, 0], preferred_element_type=F32)
            u = jnp.dot(hv, w_ref[1, half, 0], preferred_element_type=F32)
            a_ref[0, half] = g.astype(BF16)
            a_ref[1, half] = u.astype(BF16)
            s_ref[:, half * n:(half + 1) * n] = (g * jax.nn.sigmoid(g) * u).astype(BF16)

    return pl.pallas_call(
        body, grid=(T // tm,),
        in_specs=[pl.BlockSpec((tm, D), lambda i: (i, 0)), pl.BlockSpec((1, D), lambda i: (0, 0)),
                  pl.BlockSpec((2, 2, 1, D, n), lambda i: (0, 0, layer, 0, 0))],
        out_specs=[pl.BlockSpec((tm, D), lambda i: (i, 0)),
                   pl.BlockSpec((2, 2, tm, n), lambda i: (0, 0, i, 0)), pl.BlockSpec((tm, 2 * n), lambda i: (i, 0))],
        out_shape=[jax.ShapeDtypeStruct((T, D), BF16), jax.ShapeDtypeStruct((2, 2, T, n), BF16),
                   jax.ShapeDtypeStruct((T, 2 * n), BF16)],
        name=name, compiler_params=_cp(),
    )(x, norm_w.reshape(1, D), w5)


def _norm_mm(x, norm_w, b, splits, name):
    T, D = x.shape
    N = b.shape[1]
    assert sum(wd for wd, _ in splits) == N
    tm = _tile(T, 256, 16)

    def body(x_ref, nw_ref, b_ref, h_ref, *outs):
        hv = _normed(x_ref, nw_ref)
        h_ref[...] = hv
        off = 0
        for o_ref, (wd, dt) in zip(outs, splits):
            o_ref[...] = jnp.dot(hv, b_ref[:, off:off + wd], preferred_element_type=F32).astype(dt)
            off += wd

    return pl.pallas_call(
        body, grid=(T // tm,),
        in_specs=[pl.BlockSpec((tm, D), lambda i: (i, 0)), pl.BlockSpec((1, D), lambda i: (0, 0)),
                  pl.BlockSpec((D, N), lambda i: (0, 0))],
        out_specs=[pl.BlockSpec((tm, D), lambda i: (i, 0))] + [pl.BlockSpec((tm, wd), lambda i: (i, 0)) for wd, _ in splits],
        out_shape=[jax.ShapeDtypeStruct((T, D), BF16)] + [jax.ShapeDtypeStruct((T, wd), dt) for wd, dt in splits],
        name=name, compiler_params=_cp(),
    )(x, norm_w.reshape(1, D), b)


def _ffn_out_bwd_act(gx, w_out, a4, name):
    T, D = gx.shape
    n = a4.shape[-1]
    tm = _tile(T, 256, 16)

    step = 3 * LANES if n % LANES == 0 and n > 3 * LANES else n
    pieces = [(c, min(step, n - c)) for c in range(0, n, step)]

    def body(gx_ref, w_ref, a_ref, da_ref):
        gxb = gx_ref[...].astype(BF16)
        for c, wd in pieces:
            ds = _nt(gxb, w_ref[c:c + wd, :])
            g = a_ref[0, 0, :, c:c + wd].astype(F32)
            u = a_ref[1, 0, :, c:c + wd].astype(F32)
            sg = jax.nn.sigmoid(g)
            da_ref[0, 0, :, c:c + wd] = (ds * u * (sg * (1.0 + g * (1.0 - sg)))).astype(BF16)
            da_ref[1, 0, :, c:c + wd] = (ds * (g * sg)).astype(BF16)

    blk = pl.BlockSpec((2, 1, tm, n), lambda j, i: (0, j, i, 0))
    return pl.pallas_call(
        body, grid=(2, T // tm),
        in_specs=[pl.BlockSpec((tm, D), lambda j, i: (i, 0)), pl.BlockSpec((n, D), lambda j, i: (j, 0)), blk],
        out_specs=blk,
        out_shape=jax.ShapeDtypeStruct((2, 2, T, n), BF16), name=name, compiler_params=_cp(),
    )(gx, w_out, a4)


def _mm_tn_shards(h, a4, name):
    K, T, n = a4.shape
    D = h.shape[1]
    tk = _tile(T, 1024, 16)

    def body(h_ref, a_ref, o_ref):
        @pl.when(pl.program_id(1) == 0)
        def _():
            o_ref[...] = jnp.zeros_like(o_ref)

        o_ref[0] += lax.dot_general(h_ref[...], a_ref[0], (((0,), (0,)), ((), ())), preferred_element_type=F32)

    return pl.pallas_call(
        body, grid=(K, T // tk),
        in_specs=[pl.BlockSpec((tk, D), lambda k, t: (t, 0)), pl.BlockSpec((1, tk, n), lambda k, t: (k, t, 0))],
        out_specs=pl.BlockSpec((1, D, n), lambda k, t: (k, 0, 0)),
        out_shape=jax.ShapeDtypeStruct((K, D, n), F32), name=name, compiler_params=_cp(),
    )(h, a4)


def _loss_head(x, w, tgt, name):
    T, D = x.shape
    tm = _tile(T, 512, SUBLANES)

    def body(x_ref, w_ref, t_ref, dx_ref, loss_ref, dw_ref):
        @pl.when(pl.program_id(0) == 0)
        def _():
            loss_ref[...] = jnp.zeros_like(loss_ref)
            dw_ref[...] = jnp.zeros_like(dw_ref)

        xf = x_ref[...]
        wv = w_ref[...]
        r = lax.rsqrt(jnp.mean(xf * xf, axis=-1, keepdims=True) + NORM_EPS)
        xhat = xf * r
        err = xhat * wv - t_ref[...]
        per_tok = jnp.mean(err * err, axis=-1, keepdims=True)
        loss_ref[...] += 0.5 * jnp.sum(per_tok, axis=0, keepdims=True)
        dy = err * (1.0 / D)
        dxhat = dy * wv
        dx_ref[...] = r * (dxhat - xhat * jnp.mean(dxhat * xhat, axis=-1, keepdims=True))
        dw_ref[...] += jnp.sum(dy * xhat, axis=0, keepdims=True)

    row = pl.BlockSpec((tm, D), lambda i: (i, 0))
    return pl.pallas_call(
        body, grid=(T // tm,),
        in_specs=[row, pl.BlockSpec((1, D), lambda i: (0, 0)), row],
        out_specs=[row, pl.BlockSpec((SUBLANES, LANES), lambda i: (0, 0)),
                   pl.BlockSpec((SUBLANES, D), lambda i: (0, 0))],
        out_shape=[jax.ShapeDtypeStruct((T, D), F32), jax.ShapeDtypeStruct((SUBLANES, LANES), F32),
                   jax.ShapeDtypeStruct((SUBLANES, D), F32)],
        name=name, compiler_params=_cp(),
    )(x, w.reshape(1, D), tgt)


def _split3(v):
    hi = v.astype(BF16)
    r1 = v - hi.astype(F32)
    mid = r1.astype(BF16)
    lo = (r1 - mid.astype(F32)).astype(BF16)
    return hi, mid, lo


def _tri_dot(tri, v):
    out = None
    for piece in _split3(v):
        t = jnp.dot(tri, piece, preferred_element_type=F32)
        out = t if out is None else out + t
    return out


def _q_block(T):
    return _tile(T, 256, LANES)


def _gate_fwd(f, b_f, P, name):
    T = f.shape[0]
    tb = _q_block(T)

    def body(f_ref, b_ref, ct_ref, cc_ref, c0_ref, carry):
        @pl.when(pl.program_id(0) == 0)
        def _():
            carry[...] = jnp.zeros_like(carry)

        z = f_ref[...] + b_ref[...]
        logf = jnp.minimum(z, 0.0) - jnp.log(1.0 + jnp.exp(-jnp.abs(z)))
        row = lax.broadcasted_iota(jnp.int32, (tb, tb), 0)
        col = lax.broadcasted_iota(jnp.int32, (tb, tb), 1)
        tri = (col <= row).astype(BF16)
        c = _tri_dot(tri, logf) + carry[0:1, :]
        carry[...] = jnp.broadcast_to(c[tb - 1:tb, :], carry.shape)
        first = jnp.broadcast_to(c[0:1, :], c.shape)
        for p in range(P):
            shifted = c if p == 0 else pltpu.roll(c, LANES - 2 * p, 1)
            cc_ref[p] = shifted
            ct_ref[p] = shifted.T[0:SUBLANES, :]
            c0_ref[p] = (first if p == 0 else pltpu.roll(first, LANES - 2 * p, 1)).T[0:SUBLANES, :]

    rows = pl.BlockSpec((P, SUBLANES, tb), lambda i: (0, 0, i))
    return pl.pallas_call(
        body, grid=(T // tb,),
        in_specs=[pl.BlockSpec((tb, LANES), lambda i: (i, 0)), pl.BlockSpec((1, LANES), lambda i: (0, 0))],
        out_specs=[rows, pl.BlockSpec((P, tb, LANES), lambda i: (0, i, 0)), rows],
        out_shape=[jax.ShapeDtypeStruct((P, SUBLANES, T), F32), jax.ShapeDtypeStruct((P, T, LANES), F32),
                   jax.ShapeDtypeStruct((P, SUBLANES, T), F32)],
        scratch_shapes=[pltpu.VMEM((SUBLANES, LANES), F32)],
        name=name, compiler_params=_cp(),
    )(f, b_f)


def _gate_bwd(dc_cols, drowT, f, b_f, P, name):
    T = f.shape[0]
    tb = _tile(T, 256, LANES)
    nb = T // tb

    def body(dc_ref, dr_ref, f_ref, b_ref, df_ref, db_ref, carry):
        @pl.when(pl.program_id(0) == 0)
        def _():
            carry[...] = jnp.zeros_like(carry)
            db_ref[...] = jnp.zeros_like(db_ref)

        lane = lax.broadcasted_iota(jnp.int32, (tb, LANES), 1)
        dc = jnp.zeros((tb, LANES), F32)
        for p in range(P):
            rows = jnp.concatenate([dr_ref[p], jnp.zeros((LANES - SUBLANES, tb), F32)], axis=0)
            part = jnp.where(lane < 2, dc_ref[p] + rows.T, 0.0)
            dc = dc + (part if p == 0 else pltpu.roll(part, 2 * p, 1))
        row = lax.broadcasted_iota(jnp.int32, (tb, tb), 0)
        col = lax.broadcasted_iota(jnp.int32, (tb, tb), 1)
        tri = (col >= row).astype(BF16)
        dlogf = _tri_dot(tri, dc) + carry[0:1, :]
        carry[...] = jnp.broadcast_to(dlogf[0:1, :], carry.shape)
        z = f_ref[...] + b_ref[...]
        df = jnp.where(lane < 2 * P, dlogf * jax.nn.sigmoid(-z), 0.0)
        df_ref[...] = df
        db_ref[...] += jnp.sum(df, axis=0, keepdims=True)

    return pl.pallas_call(
        body, grid=(nb,),
        in_specs=[pl.BlockSpec((P, tb, LANES), lambda i: (0, nb - 1 - i, 0)),
                  pl.BlockSpec((P, SUBLANES, tb), lambda i: (0, 0, nb - 1 - i)),
                  pl.BlockSpec((tb, LANES), lambda i: (nb - 1 - i, 0)),
                  pl.BlockSpec((1, LANES), lambda i: (0, 0))],
        out_specs=[pl.BlockSpec((tb, LANES), lambda i: (nb - 1 - i, 0)),
                   pl.BlockSpec((SUBLANES, LANES), lambda i: (0, 0))],
        out_shape=[jax.ShapeDtypeStruct((T, LANES), F32), jax.ShapeDtypeStruct((SUBLANES, LANES), F32)],
        scratch_shapes=[pltpu.VMEM((SUBLANES, LANES), F32)],
        name=name, compiler_params=_cp(),
    )(dc_cols, drowT, f, b_f)


def _nt(a, b):
    return lax.dot_general(a, b, (((1,), (1,)), ((), ())), preferred_element_type=F32)


def _attn_fwd(qkv, cT, P, scale, name):
    T = qkv.shape[0]
    tq = _q_block(T)
    tw = _tile(T, 8 * tq, 2 * tq)
    cw = tw // 2
    assert cw % tq == 0, "the sequence must split into chunks of whole query blocks"
    nq = T // tq

    def body(q_ref, k_ref, v_ref, c_ref, o_ref, lse_ref, s_scr):
        i = pl.program_id(1)
        lane = lax.broadcasted_iota(jnp.int32, (1, LANES), 1)
        q = (q_ref[...].astype(F32) * (scale * LOG2E)).astype(BF16)
        q_heads = (jnp.where(lane < 64, q, jnp.zeros_like(q)), jnp.where(lane >= 64, q, jnp.zeros_like(q)))
        c0 = c_ref[0, :, pl.ds(pl.multiple_of(i * tq, tq), LANES)][:, 0:1]

        def scores(start, width, a):
            bias = (c0 - c_ref[0, :, pl.ds(start, width)]) * LOG2E
            return _nt(q_heads[a], k_ref[pl.ds(start, width), :]) + bias[a:a + 1, :]

        def softmax_pv(start, width, s_of, carry):
            v = v_ref[pl.ds(start, width), :]
            one = jnp.ones_like(v)
            v_heads = (jnp.where(lane < 64, v, one), jnp.where(lane >= 64, v, one))
            new = []
            for a in range(2):
                m, acc = carry[a]
                s = s_of(a)
                m_new = jnp.maximum(m, jnp.max(s, axis=1, keepdims=True))
                p = jnp.exp2(s - m_new)
                acc = jnp.exp2(m - m_new) * acc + jnp.dot(p.astype(BF16), v_heads[a], preferred_element_type=F32)
                new.append((m_new, acc))
            return tuple(new)

        def fill(start, buf):
            for a in range(2):
                s_scr[2 * buf + a] = scores(start, cw, a)

        def wide(j, carry):
            base = pl.multiple_of(j * tw, tw)
            fill(base + cw, 1)
            carry = softmax_pv(base, cw, lambda a: s_scr[a], carry)
            fill(base + tw, 0)
            return softmax_pv(base + cw, cw, lambda a: s_scr[2 + a], carry)

        init = tuple((jnp.full((tq, 1), NEG_BIG, F32), jnp.zeros((tq, LANES), F32)) for _ in range(2))
        n_wide = (i * tq) // tw
        fill(0, 0)
        carry = lax.fori_loop(0, n_wide, wide, init)

        base = pl.multiple_of(n_wide * tw, tw)
        ahead = i * tq - base
        col_minus_row = (lax.broadcasted_iota(jnp.int32, (tq, cw), 1)
                         - lax.broadcasted_iota(jnp.int32, (tq, cw), 0))

        def causal(buf, first_key):
            return lambda a: jnp.where(col_minus_row <= ahead - first_key, s_scr[2 * buf + a], NEG_BIG)

        def one_chunk(cr):
            return softmax_pv(base, cw, causal(0, 0), cr)

        def two_chunks(cr):
            fill(base + cw, 1)
            cr = softmax_pv(base, cw, causal(0, 0), cr)
            return softmax_pv(base + cw, cw, causal(1, cw), cr)

        (m0, a0), (m1, a1) = lax.cond(ahead >= cw, two_chunks, one_chunk, carry)
        sums = jnp.where(lane < 64, pltpu.roll(a0, 64, 1), pltpu.roll(a1, 64, 1))
        o_ref[...] = (jnp.where(lane < 64, a0, a1) / sums).astype(BF16)
        l0, l1 = a0[:, 64:65], a1[:, 0:1]
        lse = jnp.where(lane == 0, m0 + jnp.log2(l0), jnp.where(lane == 1, m1 + jnp.log2(l1), 0.0))
        lse_ref[0] = lse.T[0:SUBLANES, :]

    return pl.pallas_call(
        body, grid=(P, nq),
        in_specs=[pl.BlockSpec((tq, LANES), lambda p, i: (i, p)),
                  pl.BlockSpec((T, LANES), lambda p, i: (0, P + p)),
                  pl.BlockSpec((T, LANES), lambda p, i: (0, 2 * P + p)),
                  pl.BlockSpec((1, SUBLANES, T), lambda p, i: (p, 0, 0))],
        out_specs=[pl.BlockSpec((tq, LANES), lambda p, i: (i, p)),
                   pl.BlockSpec((1, SUBLANES, tq), lambda p, i: (p, 0, i))],
        out_shape=[jax.ShapeDtypeStruct((T, LANES * P), BF16), jax.ShapeDtypeStruct((P, SUBLANES, T), F32)],
        scratch_shapes=[pltpu.VMEM((4, tq, cw), F32)],
        name=name, compiler_params=_cp(),
    )(qkv, qkv, qkv, cT)


def _attn_delta(do, o, P, name):
    T, D = o.shape
    tb = _tile(T, 256, LANES)

    def body(do_ref, o_ref, d_ref):
        lane = lax.broadcasted_iota(jnp.int32, (1, LANES), 1)
        for p in range(P):
            cols = slice(p * LANES, (p + 1) * LANES)
            prod = do_ref[:, cols].astype(F32) * o_ref[:, cols].astype(F32)
            d0 = jnp.sum(jnp.where(lane < 64, prod, 0.0), axis=1, keepdims=True)
            d1 = jnp.sum(jnp.where(lane >= 64, prod, 0.0), axis=1, keepdims=True)
            both = jnp.where(lane == 0, d0, jnp.where(lane == 1, d1, 0.0))
            d_ref[p] = both.T[0:SUBLANES, :]

    return pl.pallas_call(
        body, grid=(T // tb,),
        in_specs=[pl.BlockSpec((tb, D), lambda i: (i, 0)), pl.BlockSpec((tb, D), lambda i: (i, 0))],
        out_specs=pl.BlockSpec((P, SUBLANES, tb), lambda i: (0, 0, i)),
        out_shape=jax.ShapeDtypeStruct((P, SUBLANES, T), F32), name=name, compiler_params=_cp(),
    )(do, o)


def _attn_bwd(qkv, do, lseT, dT, c0T, c_cols, P, scale, name):
    T = qkv.shape[0]
    tq = _q_block(T)
    tw = _tile(T, 4 * tq, 2 * tq)
    cw = tw // 2
    assert cw % tq == 0, "the sequence must split into chunks of whole query blocks"
    nq = T // tq

    def body(q_ref, do_ref, k_ref, v_ref, lse_ref, d_ref, c0_ref, cc_ref,
             dq_ref, dk_ref, dv_ref, dc_ref, drow_ref, dq_acc0, dq_acc1, s_scr):
        j = pl.program_id(1)

        @pl.when(j == 0)
        def _():
            dq_acc0[...] = jnp.zeros_like(dq_acc0)
            dq_acc1[...] = jnp.zeros_like(dq_acc1)

        lane = lax.broadcasted_iota(jnp.int32, (1, LANES), 1)
        in_head = (lane < 64, lane >= 64)
        k = k_ref[...]
        v = v_ref[...]
        zero = jnp.zeros_like(k)
        one = jnp.ones_like(k)
        k_heads = tuple(jnp.where(h, k, zero) for h in in_head)
        v_heads = tuple(jnp.where(h, v, zero) for h in in_head)
        k_ones = tuple(jnp.where(h, k, one) for h in in_head)
        cc = cc_ref[0]
        c_first = (cc[0:1, 0:1], cc[0:1, 1:2])
        c_rel = ((cc[:, 0:1] - c_first[0]) * LOG2E, (cc[:, 1:2] - c_first[1]) * LOG2E)
        dq_accs = (dq_acc0, dq_acc1)

        def scaled_q(start, width, factor):
            return (q_ref[pl.ds(start, width), :].astype(F32) * factor).astype(BF16)

        def block(start, width, carry, first_query=None, scores=None):
            q = scaled_q(start, width, scale)
            q_one = jnp.ones_like(q)
            dov = do_ref[pl.ds(start, width), :]
            lse = lse_ref[0, :, pl.ds(start, width)]
            dlt = d_ref[0, :, pl.ds(start, width)]
            c0 = c0_ref[0, :, pl.ds(start, width)]
            new = []
            for a in range(2):
                dk_a, dv_a = carry[a]
                rowv = lse[a:a + 1, :] + (c_first[a] - c0[a:a + 1, :]) * LOG2E
                if scores is None:
                    st = _nt(k_heads[a], scaled_q(start, width, scale * LOG2E))
                else:
                    st = scores(a)
                pt = jnp.exp2((st - c_rel[a]) - rowv)
                if first_query is not None:
                    row = lax.broadcasted_iota(jnp.int32, (tq, width), 0)
                    col = lax.broadcasted_iota(jnp.int32, (tq, width), 1)
                    pt = jnp.where(col - row >= first_query, pt, 0.0)
                dpt = _nt(v_heads[a], dov)
                dst_b = (pt * (dpt - dlt[a:a + 1, :])).astype(BF16)
                dv_a = dv_a + jnp.dot(pt.astype(BF16), dov, preferred_element_type=F32)
                dk_a = dk_a + jnp.dot(dst_b, jnp.where(in_head[a], q, q_one), preferred_element_type=F32)
                dq_accs[a][pl.ds(start, width), :] += lax.dot_general(
                    dst_b, k_ones[a], (((0,), (0,)), ((), ())), preferred_element_type=F32)
                new.append((dk_a, dv_a))
            return tuple(new)

        init = tuple((jnp.zeros((tq, LANES), F32), jnp.zeros((tq, LANES), F32)) for _ in range(2))
        first_key = j * tq
        diag = pl.multiple_of((first_key // cw) * cw, cw)
        carry = block(diag, cw, init, first_key - diag)
        first_wide = first_key // tw + 1
        carry = lax.cond(
            diag + cw < first_wide * tw,
            lambda cr: block(pl.multiple_of(diag + cw, cw), cw, cr), lambda cr: cr, carry)

        last = T // tw - 1

        def fill(trip, buf):
            q = scaled_q(pl.multiple_of(jnp.minimum(trip, last) * tw, tw), tw, scale * LOG2E)
            for a in range(2):
                s_scr[2 * buf + a] = _nt(k_heads[a], q)

        def trip(i, buf, cr):
            return block(pl.multiple_of(i * tw, tw), tw, cr, scores=lambda a: s_scr[2 * buf + a])

        def two_trips(p, cr):
            i = first_wide + 2 * p
            fill(i + 1, 1)
            cr = trip(i, 0, cr)
            fill(i + 2, 0)
            return trip(i + 1, 1, cr)

        n_trips = last + 1 - first_wide
        fill(first_wide, 0)
        carry = lax.fori_loop(0, n_trips // 2, two_trips, carry)
        (dk0, dv0), (dk1, dv1) = lax.cond(n_trips % 2 == 1, lambda cr: trip(last, 0, cr), lambda cr: cr, carry)
        dk_ref[...] = jnp.where(lane < 64, dk0, dk1).astype(BF16)
        dv_ref[...] = jnp.where(lane < 64, dv0, dv1).astype(BF16)
        dc_ref[0] = jnp.where(lane == 0, -dk0[:, 64:65], jnp.where(lane == 1, -dk1[:, 0:1], 0.0))

        @pl.when(j == nq - 1)
        def _():
            def finish(i, _):
                rows = pl.ds(pl.multiple_of(i * tq, tq), tq)
                a0 = dq_acc0[rows, :]
                a1 = dq_acc1[rows, :]
                dq_ref[rows, :] = (jnp.where(lane < 64, a0, a1) * scale).astype(BF16)
                sums = jnp.where(lane == 0, a0[:, 64:65], jnp.where(lane == 1, a1[:, 0:1], 0.0))
                drow_ref[0, :, rows] = sums.T[0:SUBLANES, :]
                return 0

            lax.fori_loop(0, nq, finish, 0)

    full = lambda col: pl.BlockSpec((T, LANES), lambda p, j: (0, col(p)))
    blk = lambda col: pl.BlockSpec((tq, LANES), lambda p, j: (j, col(p)))
    rows = pl.BlockSpec((1, SUBLANES, T), lambda p, j: (p, 0, 0))
    cols = pl.BlockSpec((1, tq, LANES), lambda p, j: (p, j, 0))
    D = LANES * P
    return pl.pallas_call(
        body, grid=(P, nq),
        in_specs=[full(lambda p: p), full(lambda p: p), blk(lambda p: P + p), blk(lambda p: 2 * P + p),
                  rows, rows, rows, cols],
        out_specs=[full(lambda p: p), blk(lambda p: p), blk(lambda p: p), cols, rows],
        out_shape=[jax.ShapeDtypeStruct((T, D), BF16), jax.ShapeDtypeStruct((T, D), BF16),
                   jax.ShapeDtypeStruct((T, D), BF16), jax.ShapeDtypeStruct((P, T, LANES), F32),
                   jax.ShapeDtypeStruct((P, SUBLANES, T), F32)],
        scratch_shapes=[pltpu.VMEM((T, LANES), F32), pltpu.VMEM((T, LANES), F32), pltpu.VMEM((4, tq, tw), F32)],
        name=name, compiler_params=_cp(),
    )(qkv, do, qkv, qkv, lseT, dT, c0T, c_cols)


_SQRT_HALF = 0.7071067811865476
_INV_SQRT_2PI = 0.3989422804014327


def _gelu(v):
    return 0.5 * v * (1.0 + lax.erf(v * _SQRT_HALF))


def _gelu_and_grad(v):
    cdf = 0.5 * (1.0 + lax.erf(v * _SQRT_HALF))
    return v * cdf, cdf + v * (_INV_SQRT_2PI * jnp.exp(-0.5 * v * v))


def _sgu_fwd(a, ln_g, ln_b, w_tril, bias, name):
    T, W2 = a.shape
    W = W2 // 2
    G = w_tril.shape[0]
    tb = _tile(T, 256, LANES)

    def body(a_ref, g_ref, b_ref, w_ref, bias_ref, out_ref):
        zu = _gelu(a_ref[:, :W].astype(F32))
        zv = _gelu(a_ref[:, W:].astype(F32))
        mu = jnp.mean(zv, axis=-1, keepdims=True)
        d = zv - mu
        rstd = lax.rsqrt(jnp.mean(d * d, axis=-1, keepdims=True) + LN_EPS)
        vn = (d * rstd * g_ref[...] + b_ref[...]).astype(BF16)
        for c in range(tb // LANES):
            rs = slice(c * LANES, (c + 1) * LANES)
            for g in range(G):
                cs = slice(g * LANES, (g + 1) * LANES)
                mixed = jnp.dot(w_ref[g], vn[rs, cs], preferred_element_type=F32) + bias_ref[:, cs]
                out_ref[rs, cs] = (zu[rs, cs] * mixed).astype(BF16)

    return pl.pallas_call(
        body, grid=(T // tb,),
        in_specs=[pl.BlockSpec((tb, W2), lambda i: (i, 0)), pl.BlockSpec((1, W), lambda i: (0, 0)),
                  pl.BlockSpec((1, W), lambda i: (0, 0)), pl.BlockSpec((G, LANES, LANES), lambda i: (0, 0, 0)),
                  pl.BlockSpec((LANES, W), lambda i: (0, 0))],
        out_specs=pl.BlockSpec((tb, W), lambda i: (i, 0)),
        out_shape=jax.ShapeDtypeStruct((T, W), BF16), name=name, compiler_params=_cp(),
    )(a, ln_g.reshape(1, W), ln_b.reshape(1, W), w_tril, bias)


def _sgu_bwd(a, dgated, ln_g, ln_b, w_tril, w_tril_t, bias, name):
    T, W2 = a.shape
    W = W2 // 2
    G = w_tril.shape[0]
    tb = _tile(T, 256, LANES)

    def body(a_ref, dg_ref, g_ref, b_ref, w_ref, wt_ref, bias_ref,
             da_ref, dws_ref, dbias_ref, dlng_ref, dlnb_ref, dvn_ref):
        @pl.when(pl.program_id(0) == 0)
        def _():
            dws_ref[...] = jnp.zeros_like(dws_ref)
            dbias_ref[...] = jnp.zeros_like(dbias_ref)
            dlng_ref[...] = jnp.zeros_like(dlng_ref)
            dlnb_ref[...] = jnp.zeros_like(dlnb_ref)

        up = a_ref[:, :W].astype(F32)
        vp = a_ref[:, W:].astype(F32)
        zu, gu = _gelu_and_grad(up)
        zv, gv = _gelu_and_grad(vp)
        mu = jnp.mean(zv, axis=-1, keepdims=True)
        d = zv - mu
        rstd = lax.rsqrt(jnp.mean(d * d, axis=-1, keepdims=True) + LN_EPS)
        vhat = d * rstd
        gam = g_ref[...]
        vn = (vhat * gam + b_ref[...]).astype(BF16)
        dgated = dg_ref[...]
        for c in range(tb // LANES):
            rs = slice(c * LANES, (c + 1) * LANES)
            for g in range(G):
                cs = slice(g * LANES, (g + 1) * LANES)
                vb = vn[rs, cs]
                mixed = jnp.dot(w_ref[g], vb, preferred_element_type=F32) + bias_ref[:, cs]
                dgt = dgated[rs, cs]
                da_ref[rs, cs] = (dgt * mixed * gu[rs, cs]).astype(BF16)
                dmx = dgt * zu[rs, cs]
                dbias_ref[:, cs] += dmx
                dmb = dmx.astype(BF16)
                dws_ref[g] += _nt(dmb, vb)
                dvn_ref[rs, cs] = jnp.dot(wt_ref[g], dmb, preferred_element_type=F32)
        dvn = dvn_ref[...]
        dlng_ref[...] += jnp.sum(dvn * vhat, axis=0, keepdims=True)
        dlnb_ref[...] += jnp.sum(dvn, axis=0, keepdims=True)
        dvh = dvn * gam
        dzv = rstd * (dvh - jnp.mean(dvh, axis=-1, keepdims=True)
                      - vhat * jnp.mean(dvh * vhat, axis=-1, keepdims=True))
        da_ref[:, W:] = (dzv * gv).astype(BF16)

    const2 = lambda shape: pl.BlockSpec(shape, lambda i: (0, 0))
    const3 = pl.BlockSpec((G, LANES, LANES), lambda i: (0, 0, 0))
    return pl.pallas_call(
        body, grid=(T // tb,),
        in_specs=[pl.BlockSpec((tb, W2), lambda i: (i, 0)), pl.BlockSpec((tb, W), lambda i: (i, 0)),
                  const2((1, W)), const2((1, W)), const3, const3, const2((LANES, W))],
        out_specs=[pl.BlockSpec((tb, W2), lambda i: (i, 0)), const3, const2((LANES, W)),
                   const2((SUBLANES, W)), const2((SUBLANES, W))],
        out_shape=[jax.ShapeDtypeStruct((T, W2), BF16), jax.ShapeDtypeStruct((G, LANES, LANES), F32),
                   jax.ShapeDtypeStruct((LANES, W), F32), jax.ShapeDtypeStruct((SUBLANES, W), F32),
                   jax.ShapeDtypeStruct((SUBLANES, W), F32)],
        scratch_shapes=[pltpu.VMEM((tb, W), F32)],
        name=name, compiler_params=_cp(),
    )(a, dgated, ln_g.reshape(1, W), ln_b.reshape(1, W), w_tril, w_tril_t, bias)


def _adam_math(w, g, m, v):
    m = ADAM_B1 * m + (1.0 - ADAM_B1) * g
    v = ADAM_B2 * v + (1.0 - ADAM_B2) * (g * g)
    m_hat = m / (1.0 - ADAM_B1 ** ADAM_STEP)
    v_hat = v / (1.0 - ADAM_B2 ** ADAM_STEP)
    delta = -ADAM_LR * (m_hat / (jnp.sqrt(v_hat) + ADAM_EPS) + ADAM_WD * w)
    return delta, m, v


def _adamw_halves(mine, theirs, c_idx, w, m, v, name):
    R, C = w.shape
    rh = R // 2
    tb = _row_tile(rh, C)
    nb = rh // tb

    def body(c_ref, a_ref, b_ref, w_ref, m_ref, v_ref, g_ref, d_ref, mo_ref, vo_ref):
        g = jnp.where(pl.program_id(0) == c_ref[0], a_ref[...], b_ref[...])
        d, mm, vv = _adam_math(w_ref[...], g, m_ref[...], v_ref[...])
        g_ref[...] = g
        d_ref[...] = d
        mo_ref[...] = mm
        vo_ref[...] = vv

    half = pl.BlockSpec((tb, C), lambda h, i, c: (i, 0))
    row = pl.BlockSpec((tb, C), lambda h, i, c: (h * nb + i, 0))
    sds = jax.ShapeDtypeStruct((R, C), F32)
    return pl.pallas_call(
        body,
        grid_spec=pltpu.PrefetchScalarGridSpec(
            num_scalar_prefetch=1, grid=(2, nb), in_specs=[half, half, row, row, row], out_specs=[row] * 4),
        out_shape=[sds] * 4, name=name, compiler_params=_cp())(c_idx, mine, theirs, w, m, v)


def _adamw_sum(parts, w, m, v, name):
    K, R, C = parts.shape
    tb = _tile(R, 128, SUBLANES)

    def body(p_ref, w_ref, m_ref, v_ref, g_ref, d_ref, mo_ref, vo_ref):
        g = p_ref[0]
        for k in range(1, K):
            g = g + p_ref[k]
        d, mm, vv = _adam_math(w_ref[...], g, m_ref[...], v_ref[...])
        g_ref[...] = g
        d_ref[...] = d
        mo_ref[...] = mm
        vo_ref[...] = vv

    row = pl.BlockSpec((tb, C), lambda i: (i, 0))
    sds = jax.ShapeDtypeStruct((R, C), F32)
    return pl.pallas_call(
        body, grid=(R // tb,),
        in_specs=[pl.BlockSpec((K, tb, C), lambda i: (0, i, 0)), row, row, row],
        out_specs=[row] * 4, out_shape=[sds] * 4, name=name, compiler_params=_cp())(parts, w, m, v)


def _pair_sum(g_all, recv, c_idx, name):
    K, R, C = g_all.shape
    rh = R // 2
    tb = _row_tile(rh, C)
    nb = rh // tb

    def body(c_ref, a_ref, b_ref, o_ref):
        o_ref[...] = (a_ref[...] + b_ref[...]).astype(BF16)

    return pl.pallas_call(
        body,
        grid_spec=pltpu.PrefetchScalarGridSpec(
            num_scalar_prefetch=1, grid=(K, nb),
            in_specs=[pl.BlockSpec((1, tb, C), lambda k, i, c: (k, c[0] * nb + i, 0)),
                      pl.BlockSpec((1, tb, C), lambda k, i, c: (k, i, 0))],
            out_specs=pl.BlockSpec((1, tb, C), lambda k, i, c: (k, i, 0))),
        out_shape=jax.ShapeDtypeStruct((K, rh, C), BF16), name=name, compiler_params=_cp(),
    )(c_idx, g_all, recv)


def _sum_parts(parts, name):
    K, R, C = parts.shape
    tb = _row_tile(R, C)

    def body(p_ref, o_ref):
        g = p_ref[0].astype(F32)
        for k in range(1, K):
            g = g + p_ref[k].astype(F32)
        o_ref[...] = g

    return pl.pallas_call(
        body, grid=(R // tb,), in_specs=[pl.BlockSpec((K, tb, C), lambda i: (0, i, 0))],
        out_specs=pl.BlockSpec((tb, C), lambda i: (i, 0)),
        out_shape=jax.ShapeDtypeStruct((R, C), F32), name=name, compiler_params=_cp())(parts)


_CHIP_RELATIONS = ((1, 0), (0, 1), (1, 1))


def _position():
    return lax.axis_index("x"), lax.axis_index("y"), lax.axis_index("c")


def _flip(v, bit):
    return 1 - v if bit else v


def _gather_weights(w_pack, side, name):
    R, C = w_pack.shape
    rh = R // 2
    n_side = 0 if side is None else 1

    def half(c):
        return pl.ds(pl.multiple_of(c * rh, 16), rh)

    def between_chips(*refs):
        if n_side:
            w_ref, s_ref, ow_ref, os_ref, local_sem, send_sems, recv_sems = refs
        else:
            w_ref, ow_ref, send_sems, recv_sems = refs
        x, y, c = _position()
        me = 2 * x + y
        if n_side:
            own_side = pltpu.make_async_copy(s_ref, os_ref.at[me], local_sem)
            own_side.start()

        def copies(r, slot):
            dx, dy = _CHIP_RELATIONS[r]
            peer = (_flip(x, dx), _flip(y, dy), c)
            out = [pltpu.make_async_remote_copy(
                src_ref=w_ref.at[half(c), :], dst_ref=ow_ref.at[slot, half(c), :], send_sem=send_sems.at[2 * r],
                recv_sem=recv_sems.at[2 * r], device_id=peer, device_id_type=MESH)]
            if n_side:
                out.append(pltpu.make_async_remote_copy(
                    src_ref=s_ref, dst_ref=os_ref.at[slot], send_sem=send_sems.at[2 * r + 1],
                    recv_sem=recv_sems.at[2 * r + 1], device_id=peer, device_id_type=MESH))
            return out

        sent = [cp for r in range(3) for cp in copies(r, me)]
        for cp in sent:
            cp.start()
        for r in range(3):
            dx, dy = _CHIP_RELATIONS[r]
            for cp in copies(r, 2 * _flip(x, dx) + _flip(y, dy)):
                cp.wait_recv()
        for cp in sent:
            cp.wait_send()
        if n_side:
            own_side.wait()

    sems = [pltpu.SemaphoreType.DMA((6,)), pltpu.SemaphoreType.DMA((6,))]
    gathered = jax.ShapeDtypeStruct((4, R, C), w_pack.dtype)
    if n_side:
        halves, sides = pl.pallas_call(
            between_chips, in_specs=[_hbm(), _hbm()], out_specs=[_hbm(), _hbm()],
            out_shape=[gathered, jax.ShapeDtypeStruct((4,) + side.shape, side.dtype)],
            scratch_shapes=[pltpu.SemaphoreType.DMA(())] + sems,
            name=name + "_ici", compiler_params=_cp(),
        )(w_pack, side)
    else:
        sides = None
        halves = pl.pallas_call(
            between_chips, in_specs=[_hbm()], out_specs=_hbm(), out_shape=gathered, scratch_shapes=sems,
            name=name + "_ici", compiler_params=_cp(),
        )(w_pack)

    def to_sibling(g_ref, o_ref, send_sems, recv_sems):
        x, y, c = _position()

        def copy(r, rows):
            dx, dy = _CHIP_RELATIONS[r]
            slot = 2 * _flip(x, dx) + _flip(y, dy)
            return pltpu.make_async_remote_copy(
                src_ref=g_ref.at[slot, rows, :], dst_ref=o_ref.at[slot, rows, :], send_sem=send_sems.at[r],
                recv_sem=recv_sems.at[r], device_id=(x, y, 1 - c), device_id_type=MESH)

        sent = [copy(r, half(c)) for r in range(3)]
        for cp in sent:
            cp.start()
        for r in range(3):
            copy(r, half(1 - c)).wait_recv()
        for cp in sent:
            cp.wait_send()

    full = pl.pallas_call(
        to_sibling, in_specs=[_hbm()], out_specs=_hbm(), input_output_aliases={0: 0},
        out_shape=jax.ShapeDtypeStruct((4, R, C), w_pack.dtype),
        scratch_shapes=[pltpu.SemaphoreType.DMA((3,)), pltpu.SemaphoreType.DMA((3,))],
        name=name + "_d2d", compiler_params=_cp(),
    )(halves)
    return full, sides


def _sibling_halves(g_all, name):
    K, R, C = g_all.shape
    rh = R // 2

    def body(g_ref, o_ref, send_sem, recv_sem):
        x, y, c = _position()
        start = pl.multiple_of((1 - c) * rh, SUBLANES)
        cp = pltpu.make_async_remote_copy(
            src_ref=g_ref.at[:, pl.ds(start, rh), :], dst_ref=o_ref, send_sem=send_sem, recv_sem=recv_sem,
            device_id=(x, y, 1 - c), device_id_type=MESH)
        cp.start()
        cp.wait_recv()
        cp.wait_send()

    return pl.pallas_call(
        body, in_specs=[_hbm()], out_specs=_hbm(),
        out_shape=jax.ShapeDtypeStruct((K, rh, C), F32),
        scratch_shapes=[pltpu.SemaphoreType.DMA(()), pltpu.SemaphoreType.DMA(())],
        name=name, compiler_params=_cp(),
    )(g_all)


def _chip_exchange(parts, name):
    K, R, C = parts.shape

    def body(p_ref, o_ref, local_sem, send_sems, recv_sems):
        x, y, c = _position()
        me = 2 * x + y
        own = pltpu.make_async_copy(p_ref.at[me], o_ref.at[me], local_sem)
        own.start()

        def copy(r, src_slot, dst_slot):
            dx, dy = _CHIP_RELATIONS[r]
            return pltpu.make_async_remote_copy(
                src_ref=p_ref.at[src_slot], dst_ref=o_ref.at[dst_slot], send_sem=send_sems.at[r],
                recv_sem=recv_sems.at[r], device_id=(_flip(x, dx), _flip(y, dy), c), device_id_type=MESH)

        def chip(r):
            dx, dy = _CHIP_RELATIONS[r]
            return 2 * _flip(x, dx) + _flip(y, dy)

        sent = [copy(r, chip(r), me) for r in range(3)]
        for cp in sent:
            cp.start()
        for r in range(3):
            copy(r, me, chip(r)).wait_recv()
        for cp in sent:
            cp.wait_send()
        own.wait()

    return pl.pallas_call(
        body, in_specs=[_hbm()], out_specs=_hbm(),
        out_shape=jax.ShapeDtypeStruct((K, R, C), parts.dtype),
        scratch_shapes=[pltpu.SemaphoreType.DMA(()), pltpu.SemaphoreType.DMA((3,)),
                        pltpu.SemaphoreType.DMA((3,))],
        name=name, compiler_params=_cp(),
    )(parts)


def _swap_with_sibling(half, name):
    rh, C = half.shape

    def body(h_ref, o_ref, send_sem, recv_sem):
        x, y, c = _position()
        cp = pltpu.make_async_remote_copy(
            src_ref=h_ref, dst_ref=o_ref, send_sem=send_sem, recv_sem=recv_sem,
            device_id=(x, y, 1 - c), device_id_type=MESH)
        cp.start()
        cp.wait_recv()
        cp.wait_send()

    return pl.pallas_call(
        body, in_specs=[_hbm()], out_specs=_hbm(),
        out_shape=jax.ShapeDtypeStruct((rh, C), F32),
        scratch_shapes=[pltpu.SemaphoreType.DMA(()), pltpu.SemaphoreType.DMA(())],
        name=name, compiler_params=_cp(),
    )(half)


def _gather_all(part, name):
    R, C = part.shape
    masks = [(b >> 2 & 1, b >> 1 & 1, b & 1) for b in range(1, 8)]

    def body(p_ref, o_ref, local_sem, send_sems, recv_sems):
        x, y, c = _position()
        me = 4 * x + 2 * y + c
        own = pltpu.make_async_copy(p_ref, o_ref.at[me], local_sem)
        own.start()

        def copy(r, slot):
            dx, dy, dc = masks[r]
            return pltpu.make_async_remote_copy(
                src_ref=p_ref, dst_ref=o_ref.at[slot], send_sem=send_sems.at[r], recv_sem=recv_sems.at[r],
                device_id=(_flip(x, dx), _flip(y, dy), _flip(c, dc)), device_id_type=MESH)

        sent = [copy(r, me) for r in range(7)]
        for cp in sent:
            cp.start()
        for r in range(7):
            dx, dy, dc = masks[r]
            copy(r, 4 * _flip(x, dx) + 2 * _flip(y, dy) + _flip(c, dc)).wait_recv()
        for cp in sent:
            cp.wait_send()
        own.wait()

    return pl.pallas_call(
        body, in_specs=[_hbm()], out_specs=_hbm(),
        out_shape=jax.ShapeDtypeStruct((8, R, C), F32),
        scratch_shapes=[pltpu.SemaphoreType.DMA(()), pltpu.SemaphoreType.DMA((7,)),
                        pltpu.SemaphoreType.DMA((7,))],
        name=name, compiler_params=_cp(),
    )(part)


def _pack(arrs, row_mult, cols=PACK_COLS, lead=0):
    head = arrs[0].shape[:lead]
    pieces = []
    for a in arrs:
        flat = a.astype(F32).reshape(head + (-1,))
        fill = -flat.shape[-1] % cols
        if fill:
            flat = jnp.concatenate([flat, jnp.zeros(head + (fill,), F32)], axis=-1)
        pieces.append(flat.reshape(head + (-1, cols)))
    rows = sum(p.shape[lead] for p in pieces)
    fill = -rows % row_mult
    if fill:
        pieces.append(jnp.zeros(head + (fill, cols), F32))
    return jnp.concatenate(pieces, axis=lead) if len(pieces) > 1 else pieces[0]


def _unpack(buf, shapes):
    lead = buf.shape[:-2]
    cols = buf.shape[-1]
    out, off = [], 0
    for shp in shapes:
        n = math.prod(shp)
        rows = -(-n // cols)
        piece = buf[..., off:off + rows, :]
        if rows * cols != n:
            piece = piece.reshape(lead + (-1,))[..., :n]
        out.append(piece.reshape(lead + tuple(shp)))
        off += rows
    return out


def _cols_from_chips(g):
    k, L, A, n = g.shape
    return jnp.transpose(g, (1, 2, 0, 3)).reshape(L, A, k * n)


def _rows_from_chips(g):
    k, L, n, B = g.shape
    return jnp.transpose(g, (1, 0, 2, 3)).reshape(L, k * n, B)


def _cols_to_chips(full, k=4):
    L, A, N = full.shape
    return jnp.transpose(full.reshape(L, A, k, N // k), (2, 0, 1, 3))


def _rows_to_chips(full, k=4):
    L, N, B = full.shape
    return jnp.transpose(full.reshape(L, k, N // k, B), (1, 0, 2, 3))


def kernel(x, mixer_norm_w, attn_w_in, attn_b_f, attn_w_out, sgu_w_in, sgu_ln_g, sgu_ln_b, sgu_w_s, sgu_b_s, sgu_w_out, ffn_norm_w, ffn_w_in, ffn_w_out, final_norm_w, loss_target, m_mixer_norm_w, m_attn_w_in, m_attn_b_f, m_attn_w_out, m_sgu_w_in, m_sgu_ln_g, m_sgu_ln_b, m_sgu_w_s, m_sgu_b_s, m_sgu_w_out, m_ffn_norm_w, m_ffn_w_in, m_ffn_w_out, m_final_norm_w, v_mixer_norm_w, v_attn_w_in, v_attn_b_f, v_attn_w_out, v_sgu_w_in, v_sgu_ln_g, v_sgu_ln_b, v_sgu_w_s, v_sgu_b_s, v_sgu_w_out, v_ffn_norm_w, v_ffn_w_in, v_ffn_w_out, v_final_norm_w):
    T, D = x.shape[1], x.shape[2]
    depth = mixer_norm_w.shape[0]
    H = attn_b_f.shape[1]
    P = D // LANES
    assert D % LANES == 0 and D // H == 64 and 2 * P == H and 2 * P <= LANES
    G = sgu_w_s.shape[1]
    W = sgu_w_out.shape[1] * 4
    assert sgu_w_s.shape[2] == LANES and W == G * LANES
    scale = float(D // H) ** -0.5
    f_pad = LANES
    c_idx = lax.axis_index("c").astype(jnp.int32).reshape(1)

    groups = [
        ([attn_w_out, sgu_w_in, sgu_w_out, ffn_w_out, sgu_ln_g, sgu_ln_b],
         [m_attn_w_out, m_sgu_w_in, m_sgu_w_out, m_ffn_w_out, m_sgu_ln_g, m_sgu_ln_b],
         [v_attn_w_out, v_sgu_w_in, v_sgu_w_out, v_ffn_w_out, v_sgu_ln_g, v_sgu_ln_b]),
        ([ffn_w_in], [m_ffn_w_in], [v_ffn_w_in]),
        ([attn_w_in], [m_attn_w_in], [v_attn_w_in]),
    ]
    group_cols = [D, ffn_w_in.shape[2], attn_w_in.shape[2]]
    group_shapes = [[a.shape for a in g[0]] for g in groups]
    w_packs = [_pack(g[0], 512, cols) for g, cols in zip(groups, group_cols)]
    ln_pack = _pack([sgu_ln_g, sgu_ln_b], SUBLANES)
    my_chip = 2 * lax.axis_index("x") + lax.axis_index("y")
    gathered = []
    for t, w_pack in enumerate(w_packs):
        w_pack_b = w_pack.astype(BF16)
        gat, side = _gather_weights(w_pack_b, ln_pack if t == 0 else None, f"gather_weights_{t}")
        if t == 0:
            gat_ln = side
        gathered.append(_unpack(lax.dynamic_update_index_in_dim(gat, w_pack_b, my_chip, 0), group_shapes[t]))
    (g_ao, g_si, g_so, g_fo, _, _), (g_fi,), (g_ai,) = gathered
    g_lng, g_lnb = _unpack(gat_ln, [sgu_ln_g.shape, sgu_ln_b.shape])
    w_ai = _cols_from_chips(g_ai)
    w_ai = jnp.pad(w_ai, ((0, 0), (0, 0), (0, 3 * D + f_pad - w_ai.shape[2])))
    w_ao = _rows_from_chips(g_ao)
    w_si = _cols_from_chips(g_si)
    w_so = _rows_from_chips(g_so)
    w_fo = _rows_from_chips(g_fo)
    w_fi5 = g_fi.reshape((2, 2) + g_fi.shape[1:])
    ln_g = jnp.transpose(g_lng, (1, 0, 2)).reshape(sgu_ln_g.shape[0], W)
    ln_b = jnp.transpose(g_lnb, (1, 0, 2)).reshape(sgu_ln_b.shape[0], W)
    w_tril = jnp.tril(sgu_w_s)
    w_tril_b = w_tril.astype(BF16)
    w_tril_tb = jnp.swapaxes(w_tril, 2, 3).astype(BF16)
    sgu_bias = jnp.repeat(jnp.swapaxes(sgu_b_s, 1, 2), LANES, axis=2)
    b_f_pad = jnp.pad(attn_b_f, ((0, 0), (0, LANES - H)))

    xs = x.reshape(T, D)
    saved = []
    for i in range(depth):
        j = i // 2
        rec = {"x_in": xs}
        if i % 2 == 0:
            h, qkv, f = _norm_mm(xs, mixer_norm_w[i], w_ai[j], ((3 * D, BF16), (f_pad, F32)), f"attn_qkv_{i}")
            cT, c_cols, c0T = _gate_fwd(f, b_f_pad[j:j + 1], P, f"gate_fwd_{i}")
            o, lseT = _attn_fwd(qkv, cT, P, scale, f"attn_fwd_{i}")
            x_mid = _mm(o, w_ao[j], "nn", F32, f"attn_out_{i}", res=xs)
            rec.update(qkv=qkv, f=f, c0T=c0T, c_cols=c_cols, o=o, lseT=lseT)
        else:
            h, a = _norm_mm(xs, mixer_norm_w[i], w_si[j], ((2 * W, BF16),), f"sgu_in_{i}")
            gated = _sgu_fwd(a, ln_g[j], ln_b[j], w_tril_b[j], sgu_bias[j], f"sgu_fwd_{i}")
            x_mid = _mm(gated, w_so[j], "nn", F32, f"sgu_out_{i}", res=xs)
            rec.update(a=a, gated=gated)
        h2, fa, s = _ffn_in_act(x_mid, ffn_norm_w[i], w_fi5, i, f"ffn_in_{i}")
        xs = _mm(s, w_fo[i], "nn", F32, f"ffn_out_{i}", res=x_mid)
        rec.update(h=h, x_mid=x_mid, h2=h2, fa=fa, s=s)
        saved.append(rec)

    gx, loss_acc, dw_final = _loss_head(xs, final_norm_w, loss_target.reshape(T, D), "loss_head")
    loss = lax.psum(loss_acc[0, 0], ("x", "y", "c"))

    n_attn, n_sgu = attn_w_in.shape[0], sgu_w_in.shape[0]
    d_mixer_norm, d_ffn_norm = [None] * depth, [None] * depth
    d_ai, d_ao, d_bf = [None] * n_attn, [None] * n_attn, [None] * n_attn
    d_si, d_so, d_lng, d_lnb, d_ws, d_bs = ([None] * n_sgu for _ in range(6))
    d_fi, d_fo = [None] * depth, [None] * depth
    for i in reversed(range(depth)):
        j = i // 2
        rec = saved[i]
        d_fo[i] = _mm(rec["s"], gx, "tn", F32, f"ffn_out_wgrad_{i}")
        da = _ffn_out_bwd_act(gx, w_fo[i], rec["fa"], f"ffn_out_bwd_{i}")
        da = da.reshape((4,) + da.shape[2:])
        d_fi[i] = _mm_tn_shards(rec["h2"], da, f"ffn_in_wgrad_{i}")
        gx, dwn = _nt_norm_bwd(da, g_fi, i, rec["x_mid"], ffn_norm_w[i], gx, f"ffn_in_bwd_{i}")
        d_ffn_norm[i] = dwn[0]
        if i % 2 == 0:
            do = _mm(gx, w_ao[j], "nt", BF16, f"attn_out_bwd_{i}")
            d_ao[j] = _mm(rec["o"], gx, "tn", F32, f"attn_out_wgrad_{i}")
            dT = _attn_delta(do, rec["o"], P, f"attn_delta_{i}")
            dq, dk, dv, dc_cols, drowT = _attn_bwd(rec["qkv"], do, rec["lseT"], dT, rec["c0T"], rec["c_cols"],
                                                   P, scale, f"attn_bwd_{i}")
            df, dbf = _gate_bwd(dc_cols, drowT, rec["f"], b_f_pad[j:j + 1], P, f"gate_bwd_{i}")
            d_bf[j] = dbf[0, :H]
            dproj = jnp.concatenate([dq, dk, dv, df.astype(BF16)], axis=1)
            d_ai[j] = _mm(rec["h"], dproj, "tn", F32, f"attn_in_wgrad_{i}")[:, :3 * D + H]
            dmix, w_mix = dproj, w_ai
        else:
            dgated = _mm(gx, w_so[j], "nt", F32, f"sgu_out_bwd_{i}")
            d_so[j] = _mm(rec["gated"], gx, "tn", F32, f"sgu_out_wgrad_{i}")
            da_s, dws, dbias, dlng, dlnb = _sgu_bwd(rec["a"], dgated, ln_g[j], ln_b[j], w_tril_b[j],
                                                    w_tril_tb[j], sgu_bias[j], f"sgu_bwd_{i}")
            d_ws[j] = jnp.tril(dws)
            d_bs[j] = jnp.sum(dbias.reshape(LANES, G, LANES), axis=2).T
            d_lng[j], d_lnb[j] = dlng[0], dlnb[0]
            d_si[j] = _mm(rec["h"], da_s, "tn", F32, f"sgu_in_wgrad_{i}")
            dmix, w_mix = da_s, w_si
        gx, dwn = _nt_norm_bwd(dmix[None], w_mix[None], j, rec["x_in"], mixer_norm_w[i], gx, f"mixer_in_bwd_{i}")
        d_mixer_norm[i] = dwn[0]
    grad_x = gx.reshape(x.shape)

    group_grads = [
        [_rows_to_chips(jnp.stack(d_ao)), _cols_to_chips(jnp.stack(d_si)), _rows_to_chips(jnp.stack(d_so)),
         _rows_to_chips(jnp.stack(d_fo)),
         jnp.transpose(jnp.stack(d_lng).reshape(n_sgu, 4, W // 4), (1, 0, 2)),
         jnp.transpose(jnp.stack(d_lnb).reshape(n_sgu, 4, W // 4), (1, 0, 2))],
        [jnp.stack(d_fi, axis=1)],
        [_cols_to_chips(jnp.stack(d_ai))],
    ]
    reduced = []
    for t, (grads, cols) in enumerate(zip(group_grads, group_cols)):
        g_all = _pack(grads, 512, cols, lead=1)
        from_sibling = _sibling_halves(g_all, f"grad_sibling_halves_{t}")
        pair = _pair_sum(g_all, from_sibling, c_idx, f"grad_pair_sum_{t}")
        from_chips = _chip_exchange(pair, f"grad_chip_exchange_{t}")
        my_half = _sum_parts(from_chips, f"grad_chip_sum_{t}")
        sibling_half = _swap_with_sibling(my_half, f"grad_swap_halves_{t}")
        packs = _adamw_halves(my_half, sibling_half, c_idx, w_packs[t], _pack(groups[t][1], 512, cols),
                              _pack(groups[t][2], 512, cols), f"adamw_sharded_{t}")
        reduced.append([_unpack(p, group_shapes[t]) for p in packs])

    def sharded_outputs(which):
        (ao, si, so, fo, lng, lnb), (fi,), (ai,) = (reduced[t][which] for t in range(3))
        return [ai, ao, si, so, fi, fo, lng, lnb]

    g_sh, d_sh, m_sh, v_sh = (sharded_outputs(w) for w in range(4))

    repl = [mixer_norm_w, attn_b_f, sgu_w_s, sgu_b_s, ffn_norm_w, final_norm_w]
    repl_m = [m_mixer_norm_w, m_attn_b_f, m_sgu_w_s, m_sgu_b_s, m_ffn_norm_w, m_final_norm_w]
    repl_v = [v_mixer_norm_w, v_attn_b_f, v_sgu_w_s, v_sgu_b_s, v_ffn_norm_w, v_final_norm_w]
    repl_shapes = [a.shape for a in repl]
    repl_grads = [jnp.stack(d_mixer_norm), jnp.stack(d_bf), jnp.stack(d_ws), jnp.stack(d_bs),
                  jnp.stack(d_ffn_norm), dw_final[0]]
    parts = _gather_all(_pack(repl_grads, SUBLANES), "grad_gather_replicated")
    g_rep, d_rep, m_rep, v_rep = _adamw_sum(parts, _pack(repl, SUBLANES), _pack(repl_m, SUBLANES),
                                            _pack(repl_v, SUBLANES), "adamw_replicated")
    g_r = _unpack(g_rep, repl_shapes)
    d_r = _unpack(d_rep, repl_shapes)
    m_r = _unpack(m_rep, repl_shapes)
    v_r = _unpack(v_rep, repl_shapes)

    def ordered(sh, rp):
        ai, ao, si, so, fi, fo, lng, lnb = sh
        mn, bf, ws, bs, fn, fin = rp
        return [mn, ai, bf, ao, si, lng, lnb, ws, bs, so, fn, fi, fo, fin]

    return (loss, grad_x, *ordered(g_sh, g_r), *ordered(d_sh, d_r), *ordered(m_sh, m_r), *ordered(v_sh, v_r))
```

```python
import math

import jax
import jax.numpy as jnp
from jax import lax
from jax.experimental import pallas as pl
from jax.experimental.pallas import tpu as pltpu

F32 = jnp.float32
BF16 = jnp.bfloat16
NORM_EPS = 1e-6
LN_EPS = 1e-5
ADAM_LR = 0.001
ADAM_B1 = 0.9
ADAM_B2 = 0.999
ADAM_EPS = 1e-08
ADAM_WD = 0.01
ADAM_STEP = 10

LANES = 128
SUBLANES = 8
PACK_COLS = 1024
VMEM_LIMIT = 56 * 1024 * 1024
NEG_BIG = -1e30
LOG2E = 1.4426950408889634
MESH = pl.DeviceIdType.MESH


def _cp():
    return pltpu.CompilerParams(vmem_limit_bytes=VMEM_LIMIT)


def _tile(n, cap, mult):
    best = None
    d = mult
    while d <= min(n, cap):
        if n % d == 0:
            best = d
        d += mult
    return n if best is None else best


def _row_tile(rows, cols):
    cap = max(16, (512 * 1024 // cols) // 16 * 16)
    return _tile(rows, cap, 16)


def _hbm():
    return pl.BlockSpec(memory_space=pltpu.HBM)


def _nt_norm_bwd(a3, b4, layer, x, w, dres, name):
    S, T, Ks = a3.shape
    D = x.shape[1]
    tm = _tile(T, 256, 16)

    def body(a_ref, b_ref, x_ref, w_ref, dres_ref, dx_ref, dw_ref):
        @pl.when(pl.program_id(0) == 0)
        def _():
            dw_ref[...] = jnp.zeros_like(dw_ref)

        dh = _nt(a_ref[0].astype(BF16), b_ref[0, 0])
        for s in range(1, S):
            dh = dh + _nt(a_ref[s].astype(BF16), b_ref[s, 0])
        xf = x_ref[...]
        r = lax.rsqrt(jnp.mean(xf * xf, axis=-1, keepdims=True) + NORM_EPS)
        xhat = xf * r
        dxhat = dh * w_ref[...]
        dx_ref[...] = dres_ref[...] + r * (dxhat - xhat * jnp.mean(dxhat * xhat, axis=-1, keepdims=True))
        dw_ref[...] += jnp.sum(dh * xhat, axis=0, keepdims=True)

    row = pl.BlockSpec((tm, D), lambda i: (i, 0))
    return pl.pallas_call(
        body, grid=(T // tm,),
        in_specs=[pl.BlockSpec((S, tm, Ks), lambda i: (0, i, 0)),
                  pl.BlockSpec((S, 1, D, Ks), lambda i: (0, layer, 0, 0)),
                  row, pl.BlockSpec((1, D), lambda i: (0, 0)), row],
        out_specs=[row, pl.BlockSpec((SUBLANES, D), lambda i: (0, 0))],
        out_shape=[jax.ShapeDtypeStruct((T, D), F32), jax.ShapeDtypeStruct((SUBLANES, D), F32)],
        name=name, compiler_params=_cp(),
    )(a3, b4, x, w.reshape(1, D), dres)


def _mm(a, b, mode, out_dtype, name, res=None):
    if mode == "tn":
        kt, M = a.shape
        N = b.shape[1]
        tm = _tile(M, 1408, LANES)
        tn = _tile(N, 1408, LANES)
        tk = _tile(kt, 1024, 16)

        def body(a_ref, b_ref, o_ref):
            @pl.when(pl.program_id(2) == 0)
            def _():
                o_ref[...] = jnp.zeros_like(o_ref)

            o_ref[...] += lax.dot_general(
                a_ref[...].astype(BF16), b_ref[...].astype(BF16), (((0,), (0,)), ((), ())),
                preferred_element_type=F32)

        return pl.pallas_call(
            body, grid=(M // tm, N // tn, kt // tk),
            in_specs=[pl.BlockSpec((tk, tm), lambda i, j, k: (k, i)),
                      pl.BlockSpec((tk, tn), lambda i, j, k: (k, j))],
            out_specs=pl.BlockSpec((tm, tn), lambda i, j, k: (i, j)),
            out_shape=jax.ShapeDtypeStruct((M, N), F32), name=name, compiler_params=_cp(),
        )(a, b)

    M, K = a.shape
    N = b.shape[1] if mode == "nn" else b.shape[0]
    tm = _tile(M, 512, 16)
    cap = min(3072, (6 << 20) // (2 * K), (4 << 20) // (tm * jnp.dtype(out_dtype).itemsize))
    tn = _tile(N, max(LANES, cap // LANES * LANES), LANES)
    dims = (((1,), (0,)), ((), ())) if mode == "nn" else (((1,), (1,)), ((), ()))

    def body(*refs):
        if res is None:
            a_ref, b_ref, o_ref = refs
        else:
            a_ref, b_ref, r_ref, o_ref = refs
        acc = lax.dot_general(a_ref[...].astype(BF16), b_ref[...].astype(BF16), dims,
                              preferred_element_type=F32)
        if res is not None:
            acc = acc + r_ref[...]
        o_ref[...] = acc.astype(out_dtype)

    b_spec = (pl.BlockSpec((K, tn), lambda j, i: (0, j)) if mode == "nn"
              else pl.BlockSpec((tn, K), lambda j, i: (j, 0)))
    in_specs = [pl.BlockSpec((tm, K), lambda j, i: (i, 0)), b_spec]
    args = [a, b]
    if res is not None:
        in_specs.append(pl.BlockSpec((tm, tn), lambda j, i: (i, j)))
        args.append(res)
    return pl.pallas_call(
        body, grid=(N // tn, M // tm), in_specs=in_specs,
        out_specs=pl.BlockSpec((tm, tn), lambda j, i: (i, j)),
        out_shape=jax.ShapeDtypeStruct((M, N), out_dtype), name=name, compiler_params=_cp(),
    )(*args)


def _normed(x_ref, w_ref):
    xf = x_ref[...]
    r = lax.rsqrt(jnp.mean(xf * xf, axis=-1, keepdims=True) + NORM_EPS)
    return (xf * r * w_ref[...]).astype(BF16)


def _ffn_in_act(x, norm_w, w5, layer, name):
    T, D = x.shape
    n = w5.shape[-1]
    tm = _tile(T, 256, 16)

    def body(x_ref, nw_ref, w_ref, h_ref, a_ref, s_ref):
        hv = _normed(x_ref, nw_ref)
        h_ref[...] = hv
        for half in range(2):
            g = jnp.dot(hv, w_ref[0, half, 0], preferred_element_type=F32)
            u = jnp.dot(hv, w_ref[1, half, 0], preferred_element_type=F32)
            a_ref[0, half] = g.astype(BF16)
            a_ref[1, half] = u.astype(BF16)
            s_ref[:, half * n:(half + 1) * n] = (g * jax.nn.sigmoid(g) * u).astype(BF16)

    return pl.pallas_call(
        body, grid=(T // tm,),
        in_specs=[pl.BlockSpec((tm, D), lambda i: (i, 0)), pl.BlockSpec((1, D), lambda i: (0, 0)),
                  pl.BlockSpec((2, 2, 1, D, n), lambda i: (0, 0, layer, 0, 0))],
        out_specs=[pl.BlockSpec((tm, D), lambda i: (i, 0)),
                   pl.BlockSpec((2, 2, tm, n), lambda i: (0, 0, i, 0)), pl.BlockSpec((tm, 2 * n), lambda i: (i, 0))],
        out_shape=[jax.ShapeDtypeStruct((T, D), BF16), jax.ShapeDtypeStruct((2, 2, T, n), BF16),
                   jax.ShapeDtypeStruct((T, 2 * n), BF16)],
        name=name, compiler_params=_cp(),
    )(x, norm_w.reshape(1, D), w5)


def _norm_mm(x, norm_w, b, splits, name):
    T, D = x.shape
    N = b.shape[1]
    assert sum(wd for wd, _ in splits) == N
    tm = _tile(T, 256, 16)

    def body(x_ref, nw_ref, b_ref, h_ref, *outs):
        hv = _normed(x_ref, nw_ref)
        h_ref[...] = hv
        off = 0
        for o_ref, (wd, dt) in zip(outs, splits):
            o_ref[...] = jnp.dot(hv, b_ref[:, off:off + wd], preferred_element_type=F32).astype(dt)
            off += wd

    return pl.pallas_call(
        body, grid=(T // tm,),
        in_specs=[pl.BlockSpec((tm, D), lambda i: (i, 0)), pl.BlockSpec((1, D), lambda i: (0, 0)),
                  pl.BlockSpec((D, N), lambda i: (0, 0))],
        out_specs=[pl.BlockSpec((tm, D), lambda i: (i, 0))] + [pl.BlockSpec((tm, wd), lambda i: (i, 0)) for wd, _ in splits],
        out_shape=[jax.ShapeDtypeStruct((T, D), BF16)] + [jax.ShapeDtypeStruct((T, wd), dt) for wd, dt in splits],
        name=name, compiler_params=_cp(),
    )(x, norm_w.reshape(1, D), b)


def _ffn_out_bwd_act(gx, w_out, a4, name):
    T, D = gx.shape
    n = a4.shape[-1]
    tm = _tile(T, 256, 16)

    step = 3 * LANES if n % LANES == 0 and n > 3 * LANES else n
    pieces = [(c, min(step, n - c)) for c in range(0, n, step)]

    def body(gx_ref, w_ref, a_ref, da_ref):
        gxb = gx_ref[...].astype(BF16)
        for c, wd in pieces:
            ds = _nt(gxb, w_ref[c:c + wd, :])
            g = a_ref[0, 0, :, c:c + wd].astype(F32)
            u = a_ref[1, 0, :, c:c + wd].astype(F32)
            sg = jax.nn.sigmoid(g)
            da_ref[0, 0, :, c:c + wd] = (ds * u * (sg * (1.0 + g * (1.0 - sg)))).astype(BF16)
            da_ref[1, 0, :, c:c + wd] = (ds * (g * sg)).astype(BF16)

    blk = pl.BlockSpec((2, 1, tm, n), lambda j, i: (0, j, i, 0))
    return pl.pallas_call(
        body, grid=(2, T // tm),
        in_specs=[pl.BlockSpec((tm, D), lambda j, i: (i, 0)), pl.BlockSpec((n, D), lambda j, i: (j, 0)), blk],
        out_specs=blk,
        out_shape=jax.ShapeDtypeStruct((2, 2, T, n), BF16), name=name, compiler_params=_cp(),
    )(gx, w_out, a4)


def _mm_tn_shards(h, a4, name):
    K, T, n = a4.shape
    D = h.shape[1]
    tk = _tile(T, 1024, 16)

    def body(h_ref, a_ref, o_ref):
        @pl.when(pl.program_id(1) == 0)
        def _():
            o_ref[...] = jnp.zeros_like(o_ref)

        o_ref[0] += lax.dot_general(h_ref[...], a_ref[0], (((0,), (0,)), ((), ())), preferred_element_type=F32)

    return pl.pallas_call(
        body, grid=(K, T // tk),
        in_specs=[pl.BlockSpec((tk, D), lambda k, t: (t, 0)), pl.BlockSpec((1, tk, n), lambda k, t: (k, t, 0))],
        out_specs=pl.BlockSpec((1, D, n), lambda k, t: (k, 0, 0)),
        out_shape=jax.ShapeDtypeStruct((K, D, n), F32), name=name, compiler_params=_cp(),
    )(h, a4)


def _loss_head(x, w, tgt, name):
    T, D = x.shape
    tm = _tile(T, 512, SUBLANES)

    def body(x_ref, w_ref, t_ref, dx_ref, loss_ref, dw_ref):
        @pl.when(pl.program_id(0) == 0)
        def _():
            loss_ref[...] = jnp.zeros_like(loss_ref)
            dw_ref[...] = jnp.zeros_like(dw_ref)

        xf = x_ref[...]
        wv = w_ref[...]
        r = lax.rsqrt(jnp.mean(xf * xf, axis=-1, keepdims=True) + NORM_EPS)
        xhat = xf * r
        err = xhat * wv - t_ref[...]
        per_tok = jnp.mean(err * err, axis=-1, keepdims=True)
        loss_ref[...] += 0.5 * jnp.sum(per_tok, axis=0, keepdims=True)
        dy = err * (1.0 / D)
        dxhat = dy * wv
        dx_ref[...] = r * (dxhat - xhat * jnp.mean(dxhat * xhat, axis=-1, keepdims=True))
        dw_ref[...] += jnp.sum(dy * xhat, axis=0, keepdims=True)

    row = pl.BlockSpec((tm, D), lambda i: (i, 0))
    return pl.pallas_call(
        body, grid=(T // tm,),
        in_specs=[row, pl.BlockSpec((1, D), lambda i: (0, 0)), row],
        out_specs=[row, pl.BlockSpec((SUBLANES, LANES), lambda i: (0, 0)),
                   pl.BlockSpec((SUBLANES, D), lambda i: (0, 0))],
        out_shape=[jax.ShapeDtypeStruct((T, D), F32), jax.ShapeDtypeStruct((SUBLANES, LANES), F32),
                   jax.ShapeDtypeStruct((SUBLANES, D), F32)],
        name=name, compiler_params=_cp(),
    )(x, w.reshape(1, D), tgt)


def _split3(v):
    hi = v.astype(BF16)
    r1 = v - hi.astype(F32)
    mid = r1.astype(BF16)
    lo = (r1 - mid.astype(F32)).astype(BF16)
    return hi, mid, lo


def _tri_dot(tri, v):
    out = None
    for piece in _split3(v):
        t = jnp.dot(tri, piece, preferred_element_type=F32)
        out = t if out is None else out + t
    return out


def _q_block(T):
    return _tile(T, 256, LANES)


def _gate_fwd(f, b_f, P, name):
    T = f.shape[0]
    tb = _q_block(T)

    def body(f_ref, b_ref, ct_ref, cc_ref, c0_ref, carry):
        @pl.when(pl.program_id(0) == 0)
        def _():
            carry[...] = jnp.zeros_like(carry)

        z = f_ref[...] + b_ref[...]
        logf = jnp.minimum(z, 0.0) - jnp.log(1.0 + jnp.exp(-jnp.abs(z)))
        row = lax.broadcasted_iota(jnp.int32, (tb, tb), 0)
        col = lax.broadcasted_iota(jnp.int32, (tb, tb), 1)
        tri = (col <= row).astype(BF16)
        c = _tri_dot(tri, logf) + carry[0:1, :]
        carry[...] = jnp.broadcast_to(c[tb - 1:tb, :], carry.shape)
        first = jnp.broadcast_to(c[0:1, :], c.shape)
        for p in range(P):
            shifted = c if p == 0 else pltpu.roll(c, LANES - 2 * p, 1)
            cc_ref[p] = shifted
            ct_ref[p] = shifted.T[0:SUBLANES, :]
            c0_ref[p] = (first if p == 0 else pltpu.roll(first, LANES - 2 * p, 1)).T[0:SUBLANES, :]

    rows = pl.BlockSpec((P, SUBLANES, tb), lambda i: (0, 0, i))
    return pl.pallas_call(
        body, grid=(T // tb,),
        in_specs=[pl.BlockSpec((tb, LANES), lambda i: (i, 0)), pl.BlockSpec((1, LANES), lambda i: (0, 0))],
        out_specs=[rows, pl.BlockSpec((P, tb, LANES), lambda i: (0, i, 0)), rows],
        out_shape=[jax.ShapeDtypeStruct((P, SUBLANES, T), F32), jax.ShapeDtypeStruct((P, T, LANES), F32),
                   jax.ShapeDtypeStruct((P, SUBLANES, T), F32)],
        scratch_shapes=[pltpu.VMEM((SUBLANES, LANES), F32)],
        name=name, compiler_params=_cp(),
    )(f, b_f)


def _gate_bwd(dc_cols, drowT, f, b_f, P, name):
    T = f.shape[0]
    tb = _tile(T, 256, LANES)
    nb = T // tb

    def body(dc_ref, dr_ref, f_ref, b_ref, df_ref, db_ref, carry):
        @pl.when(pl.program_id(0) == 0)
        def _():
            carry[...] = jnp.zeros_like(carry)
            db_ref[...] = jnp.zeros_like(db_ref)

        lane = lax.broadcasted_iota(jnp.int32, (tb, LANES), 1)
        dc = jnp.zeros((tb, LANES), F32)
        for p in range(P):
            rows = jnp.concatenate([dr_ref[p], jnp.zeros((LANES - SUBLANES, tb), F32)], axis=0)
            part = jnp.where(lane < 2, dc_ref[p] + rows.T, 0.0)
            dc = dc + (part if p == 0 else pltpu.roll(part, 2 * p, 1))
        row = lax.broadcasted_iota(jnp.int32, (tb, tb), 0)
        col = lax.broadcasted_iota(jnp.int32, (tb, tb), 1)
        tri = (col >= row).astype(BF16)
        dlogf = _tri_dot(tri, dc) + carry[0:1, :]
        carry[...] = jnp.broadcast_to(dlogf[0:1, :], carry.shape)
        z = f_ref[...] + b_ref[...]
        df = jnp.where(lane < 2 * P, dlogf * jax.nn.sigmoid(-z), 0.0)
        df_ref[...] = df
        db_ref[...] += jnp.sum(df, axis=0, keepdims=True)

    return pl.pallas_call(
        body, grid=(nb,),
        in_specs=[pl.BlockSpec((P, tb, LANES), lambda i: (0, nb - 1 - i, 0)),
                  pl.BlockSpec((P, SUBLANES, tb), lambda i: (0, 0, nb - 1 - i)),
                  pl.BlockSpec((tb, LANES), lambda i: (nb - 1 - i, 0)),
                  pl.BlockSpec((1, LANES), lambda i: (0, 0))],
        out_specs=[pl.BlockSpec((tb, LANES), lambda i: (nb - 1 - i, 0)),
                   pl.BlockSpec((SUBLANES, LANES), lambda i: (0, 0))],
        out_shape=[jax.ShapeDtypeStruct((T, LANES), F32), jax.ShapeDtypeStruct((SUBLANES, LANES), F32)],
        scratch_shapes=[pltpu.VMEM((SUBLANES, LANES), F32)],
        name=name, compiler_params=_cp(),
    )(dc_cols, drowT, f, b_f)


def _nt(a, b):
    return lax.dot_general(a, b, (((1,), (1,)), ((), ())), preferred_element_type=F32)


def _attn_fwd(qkv, cT, P, scale, name):
    T = qkv.shape[0]
    tq = _q_block(T)
    tw = _tile(T, 8 * tq, 2 * tq)
    cw = tw // 2
    assert cw % tq == 0, "the sequence must split into chunks of whole query blocks"
    nq = T // tq

    def body(q_ref, k_ref, v_ref, c_ref, o_ref, lse_ref, s_scr):
        i = pl.program_id(1)
        lane = lax.broadcasted_iota(jnp.int32, (1, LANES), 1)
        q = (q_ref[...].astype(F32) * (scale * LOG2E)).astype(BF16)
        q_heads = (jnp.where(lane < 64, q, jnp.zeros_like(q)), jnp.where(lane >= 64, q, jnp.zeros_like(q)))
        c0 = c_ref[0, :, pl.ds(pl.multiple_of(i * tq, tq), LANES)][:, 0:1]

        def scores(start, width, a):
            bias = (c0 - c_ref[0, :, pl.ds(start, width)]) * LOG2E
            return _nt(q_heads[a], k_ref[pl.ds(start, width), :]) + bias[a:a + 1, :]

        def softmax_pv(start, width, s_of, carry):
            v = v_ref[pl.ds(start, width), :]
            one = jnp.ones_like(v)
            v_heads = (jnp.where(lane < 64, v, one), jnp.where(lane >= 64, v, one))
            new = []
            for a in range(2):
                m, acc = carry[a]
                s = s_of(a)
                m_new = jnp.maximum(m, jnp.max(s, axis=1, keepdims=True))
                p = jnp.exp2(s - m_new)
                acc = jnp.exp2(m - m_new) * acc + jnp.dot(p.astype(BF16), v_heads[a], preferred_element_type=F32)
                new.append((m_new, acc))
            return tuple(new)

        def fill(start, buf):
            for a in range(2):
                s_scr[2 * buf + a] = scores(start, cw, a)

        def wide(j, carry):
            base = pl.multiple_of(j * tw, tw)
            fill(base + cw, 1)
            carry = softmax_pv(base, cw, lambda a: s_scr[a], carry)
            fill(base + tw, 0)
            return softmax_pv(base + cw, cw, lambda a: s_scr[2 + a], carry)

        init = tuple((jnp.full((tq, 1), NEG_BIG, F32), jnp.zeros((tq, LANES), F32)) for _ in range(2))
        n_wide = (i * tq) // tw
        fill(0, 0)
        carry = lax.fori_loop(0, n_wide, wide, init)

        base = pl.multiple_of(n_wide * tw, tw)
        ahead = i * tq - base
        col_minus_row = (lax.broadcasted_iota(jnp.int32, (tq, cw), 1)
                         - lax.broadcasted_iota(jnp.int32, (tq, cw), 0))

        def causal(buf, first_key):
            return lambda a: jnp.where(col_minus_row <= ahead - first_key, s_scr[2 * buf + a], NEG_BIG)

        def one_chunk(cr):
            return softmax_pv(base, cw, causal(0, 0), cr)

        def two_chunks(cr):
            fill(base + cw, 1)
            cr = softmax_pv(base, cw, causal(0, 0), cr)
            return softmax_pv(base + cw, cw, causal(1, cw), cr)

        (m0, a0), (m1, a1) = lax.cond(ahead >= cw, two_chunks, one_chunk, carry)
        sums = jnp.where(lane < 64, pltpu.roll(a0, 64, 1), pltpu.roll(a1, 64, 1))
        o_ref[...] = (jnp.where(lane < 64, a0, a1) / sums).astype(BF16)
        l0, l1 = a0[:, 64:65], a1[:, 0:1]
        lse = jnp.where(lane == 0, m0 + jnp.log2(l0), jnp.where(lane == 1, m1 + jnp.log2(l1), 0.0))
        lse_ref[0] = lse.T[0:SUBLANES, :]

    return pl.pallas_call(
        body, grid=(P, nq),
        in_specs=[pl.BlockSpec((tq, LANES), lambda p, i: (i, p)),
                  pl.BlockSpec((T, LANES), lambda p, i: (0, P + p)),
                  pl.BlockSpec((T, LANES), lambda p, i: (0, 2 * P + p)),
                  pl.BlockSpec((1, SUBLANES, T), lambda p, i: (p, 0, 0))],
        out_specs=[pl.BlockSpec((tq, LANES), lambda p, i: (i, p)),
                   pl.BlockSpec((1, SUBLANES, tq), lambda p, i: (p, 0, i))],
        out_shape=[jax.ShapeDtypeStruct((T, LANES * P), BF16), jax.ShapeDtypeStruct((P, SUBLANES, T), F32)],
        scratch_shapes=[pltpu.VMEM((4, tq, cw), F32)],
        name=name, compiler_params=_cp(),
    )(qkv, qkv, qkv, cT)


def _attn_delta(do, o, P, name):
    T, D = o.shape
    tb = _tile(T, 256, LANES)

    def body(do_ref, o_ref, d_ref):
        lane = lax.broadcasted_iota(jnp.int32, (1, LANES), 1)
        for p in range(P):
            cols = slice(p * LANES, (p + 1) * LANES)
            prod = do_ref[:, cols].astype(F32) * o_ref[:, cols].astype(F32)
            d0 = jnp.sum(jnp.where(lane < 64, prod, 0.0), axis=1, keepdims=True)
            d1 = jnp.sum(jnp.where(lane >= 64, prod, 0.0), axis=1, keepdims=True)
            both = jnp.where(lane == 0, d0, jnp.where(lane == 1, d1, 0.0))
            d_ref[p] = both.T[0:SUBLANES, :]

    return pl.pallas_call(
        body, grid=(T // tb,),
        in_specs=[pl.BlockSpec((tb, D), lambda i: (i, 0)), pl.BlockSpec((tb, D), lambda i: (i, 0))],
        out_specs=pl.BlockSpec((P, SUBLANES, tb), lambda i: (0, 0, i)),
        out_shape=jax.ShapeDtypeStruct((P, SUBLANES, T), F32), name=name, compiler_params=_cp(),
    )(do, o)


def _attn_bwd(qkv, do, lseT, dT, c0T, c_cols, P, scale, name):
    T = qkv.shape[0]
    tq = _q_block(T)
    tw = _tile(T, 4 * tq, 2 * tq)
    cw = tw // 2
    assert cw % tq == 0, "the sequence must split into chunks of whole query blocks"
    nq = T // tq

    def body(q_ref, do_ref, k_ref, v_ref, lse_ref, d_ref, c0_ref, cc_ref,
             dq_ref, dk_ref, dv_ref, dc_ref, drow_ref, dq_acc0, dq_acc1, s_scr):
        j = pl.program_id(1)

        @pl.when(j == 0)
        def _():
            dq_acc0[...] = jnp.zeros_like(dq_acc0)
            dq_acc1[...] = jnp.zeros_like(dq_acc1)

        lane = lax.broadcasted_iota(jnp.int32, (1, LANES), 1)
        in_head = (lane < 64, lane >= 64)
        k = k_ref[...]
        v = v_ref[...]
        zero = jnp.zeros_like(k)
        one = jnp.ones_like(k)
        k_heads = tuple(jnp.where(h, k, zero) for h in in_head)
        v_heads = tuple(jnp.where(h, v, zero) for h in in_head)
        k_ones = tuple(jnp.where(h, k, one) for h in in_head)
        cc = cc_ref[0]
        c_first = (cc[0:1, 0:1], cc[0:1, 1:2])
        c_rel = ((cc[:, 0:1] - c_first[0]) * LOG2E, (cc[:, 1:2] - c_first[1]) * LOG2E)
        dq_accs = (dq_acc0, dq_acc1)

        def scaled_q(start, width, factor):
            return (q_ref[pl.ds(start, width), :].astype(F32) * factor).astype(BF16)

        def block(start, width, carry, first_query=None, scores=None):
            q = scaled_q(start, width, scale)
            q_one = jnp.ones_like(q)
            dov = do_ref[pl.ds(start, width), :]
            lse = lse_ref[0, :, pl.ds(start, width)]
            dlt = d_ref[0, :, pl.ds(start, width)]
            c0 = c0_ref[0, :, pl.ds(start, width)]
            new = []
            for a in range(2):
                dk_a, dv_a = carry[a]
                rowv = lse[a:a + 1, :] + (c_first[a] - c0[a:a + 1, :]) * LOG2E
                if scores is None:
                    st = _nt(k_heads[a], scaled_q(start, width, scale * LOG2E))
                else:
                    st = scores(a)
                pt = jnp.exp2((st - c_rel[a]) - rowv)
                if first_query is not None:
                    row = lax.broadcasted_iota(jnp.int32, (tq, width), 0)
                    col = lax.broadcasted_iota(jnp.int32, (tq, width), 1)
                    pt = jnp.where(col - row >= first_query, pt, 0.0)
                dpt = _nt(v_heads[a], dov)
                dst_b = (pt * (dpt - dlt[a:a + 1, :])).astype(BF16)
                dv_a = dv_a + jnp.dot(pt.astype(BF16), dov, preferred_element_type=F32)
                dk_a = dk_a + jnp.dot(dst_b, jnp.where(in_head[a], q, q_one), preferred_element_type=F32)
                dq_accs[a][pl.ds(start, width), :] += lax.dot_general(
                    dst_b, k_ones[a], (((0,), (0,)), ((), ())), preferred_element_type=F32)
                new.append((dk_a, dv_a))
            return tuple(new)

        first_key = j * tq
        first_wide = first_key // tw + 1
        last = T // tw - 1

        def fill(trip, buf):
            q = scaled_q(pl.multiple_of(jnp.minimum(trip, last) * tw, tw), tw, scale * LOG2E)
            for a in range(2):
                s_scr[2 * buf + a] = _nt(k_heads[a], q)

        def trip(i, buf, cr):
            return block(pl.multiple_of(i * tw, tw), tw, cr, scores=lambda a: s_scr[2 * buf + a])

        def two_trips(p, cr):
            i = first_wide + 2 * p
            fill(i + 1, 1)
            cr = trip(i, 0, cr)
            fill(i + 2, 0)
            return trip(i + 1, 1, cr)

        init = tuple((jnp.zeros((tq, LANES), F32), jnp.zeros((tq, LANES), F32)) for _ in range(2))
        fill(first_wide, 0)
        diag = pl.multiple_of((first_key // cw) * cw, cw)
        carry = block(diag, cw, init, first_key - diag)
        carry = lax.cond(
            diag + cw < first_wide * tw,
            lambda cr: block(pl.multiple_of(diag + cw, cw), cw, cr), lambda cr: cr, carry)
        n_trips = last + 1 - first_wide
        carry = lax.fori_loop(0, n_trips // 2, two_trips, carry)
        (dk0, dv0), (dk1, dv1) = lax.cond(n_trips % 2 == 1, lambda cr: trip(last, 0, cr), lambda cr: cr, carry)
        dk_ref[...] = jnp.where(lane < 64, dk0, dk1).astype(BF16)
        dv_ref[...] = jnp.where(lane < 64, dv0, dv1).astype(BF16)
        dc_ref[0] = jnp.where(lane == 0, -dk0[:, 64:65], jnp.where(lane == 1, -dk1[:, 0:1], 0.0))

        @pl.when(j == nq - 1)
        def _():
            def finish(i, _):
                rows = pl.ds(pl.multiple_of(i * tq, tq), tq)
                a0 = dq_acc0[rows, :]
                a1 = dq_acc1[rows, :]
                dq_ref[rows, :] = (jnp.where(lane < 64, a0, a1) * scale).astype(BF16)
                sums = jnp.where(lane == 0, a0[:, 64:65], jnp.where(lane == 1, a1[:, 0:1], 0.0))
                drow_ref[0, :, rows] = sums.T[0:SUBLANES, :]
                return 0

            lax.fori_loop(0, nq, finish, 0)

    full = lambda col: pl.BlockSpec((T, LANES), lambda p, j: (0, col(p)))
    blk = lambda col: pl.BlockSpec((tq, LANES), lambda p, j: (j, col(p)))
    rows = pl.BlockSpec((1, SUBLANES, T), lambda p, j: (p, 0, 0))
    cols = pl.BlockSpec((1, tq, LANES), lambda p, j: (p, j, 0))
    D = LANES * P
    return pl.pallas_call(
        body, grid=(P, nq),
        in_specs=[full(lambda p: p), full(lambda p: p), blk(lambda p: P + p), blk(lambda p: 2 * P + p),
                  rows, rows, rows, cols],
        out_specs=[full(lambda p: p), blk(lambda p: p), blk(lambda p: p), cols, rows],
        out_shape=[jax.ShapeDtypeStruct((T, D), BF16), jax.ShapeDtypeStruct((T, D), BF16),
                   jax.ShapeDtypeStruct((T, D), BF16), jax.ShapeDtypeStruct((P, T, LANES), F32),
                   jax.ShapeDtypeStruct((P, SUBLANES, T), F32)],
        scratch_shapes=[pltpu.VMEM((T, LANES), F32), pltpu.VMEM((T, LANES), F32), pltpu.VMEM((4, tq, tw), F32)],
        name=name, compiler_params=_cp(),
    )(qkv, do, qkv, qkv, lseT, dT, c0T, c_cols)


_SQRT_HALF = 0.7071067811865476
_INV_SQRT_2PI = 0.3989422804014327


def _gelu(v):
    return 0.5 * v * (1.0 + lax.erf(v * _SQRT_HALF))


def _gelu_and_grad(v):
    cdf = 0.5 * (1.0 + lax.erf(v * _SQRT_HALF))
    return v * cdf, cdf + v * (_INV_SQRT_2PI * jnp.exp(-0.5 * v * v))


def _sgu_fwd(a, ln_g, ln_b, w_tril, bias, name):
    T, W2 = a.shape
    W = W2 // 2
    G = w_tril.shape[0]
    tb = _tile(T, 256, LANES)

    def body(a_ref, g_ref, b_ref, w_ref, bias_ref, out_ref):
        zu = _gelu(a_ref[:, :W].astype(F32))
        zv = _gelu(a_ref[:, W:].astype(F32))
        mu = jnp.mean(zv, axis=-1, keepdims=True)
        d = zv - mu
        rstd = lax.rsqrt(jnp.mean(d * d, axis=-1, keepdims=True) + LN_EPS)
        vn = (d * rstd * g_ref[...] + b_ref[...]).astype(BF16)
        for c in range(tb // LANES):
            rs = slice(c * LANES, (c + 1) * LANES)
            for g in range(G):
                cs = slice(g * LANES, (g + 1) * LANES)
                mixed = jnp.dot(w_ref[g], vn[rs, cs], preferred_element_type=F32) + bias_ref[:, cs]
                out_ref[rs, cs] = (zu[rs, cs] * mixed).astype(BF16)

    return pl.pallas_call(
        body, grid=(T // tb,),
        in_specs=[pl.BlockSpec((tb, W2), lambda i: (i, 0)), pl.BlockSpec((1, W), lambda i: (0, 0)),
                  pl.BlockSpec((1, W), lambda i: (0, 0)), pl.BlockSpec((G, LANES, LANES), lambda i: (0, 0, 0)),
                  pl.BlockSpec((LANES, W), lambda i: (0, 0))],
        out_specs=pl.BlockSpec((tb, W), lambda i: (i, 0)),
        out_shape=jax.ShapeDtypeStruct((T, W), BF16), name=name, compiler_params=_cp(),
    )(a, ln_g.reshape(1, W), ln_b.reshape(1, W), w_tril, bias)


def _sgu_bwd(a, dgated, ln_g, ln_b, w_tril, w_tril_t, bias, name):
    T, W2 = a.shape
    W = W2 // 2
    G = w_tril.shape[0]
    tb = _tile(T, 256, LANES)

    def body(a_ref, dg_ref, g_ref, b_ref, w_ref, wt_ref, bias_ref,
             da_ref, dws_ref, dbias_ref, dlng_ref, dlnb_ref, dvn_ref):
        @pl.when(pl.program_id(0) == 0)
        def _():
            dws_ref[...] = jnp.zeros_like(dws_ref)
            dbias_ref[...] = jnp.zeros_like(dbias_ref)
            dlng_ref[...] = jnp.zeros_like(dlng_ref)
            dlnb_ref[...] = jnp.zeros_like(dlnb_ref)

        up = a_ref[:, :W].astype(F32)
        vp = a_ref[:, W:].astype(F32)
        zu, gu = _gelu_and_grad(up)
        zv, gv = _gelu_and_grad(vp)
        mu = jnp.mean(zv, axis=-1, keepdims=True)
        d = zv - mu
        rstd = lax.rsqrt(jnp.mean(d * d, axis=-1, keepdims=True) + LN_EPS)
        vhat = d * rstd
        gam = g_ref[...]
        vn = (vhat * gam + b_ref[...]).astype(BF16)
        dgated = dg_ref[...]
        for c in range(tb // LANES):
            rs = slice(c * LANES, (c + 1) * LANES)
            for g in range(G):
                cs = slice(g * LANES, (g + 1) * LANES)
                vb = vn[rs, cs]
                mixed = jnp.dot(w_ref[g], vb, preferred_element_type=F32) + bias_ref[:, cs]
                dgt = dgated[rs, cs]
                da_ref[rs, cs] = (dgt * mixed * gu[rs, cs]).astype(BF16)
                dmx = dgt * zu[rs, cs]
                dbias_ref[:, cs] += dmx
                dmb = dmx.astype(BF16)
                dws_ref[g] += _nt(dmb, vb)
                dvn_ref[rs, cs] = jnp.dot(wt_ref[g], dmb, preferred_element_type=F32)
        dvn = dvn_ref[...]
        dlng_ref[...] += jnp.sum(dvn * vhat, axis=0, keepdims=True)
        dlnb_ref[...] += jnp.sum(dvn, axis=0, keepdims=True)
        dvh = dvn * gam
        dzv = rstd * (dvh - jnp.mean(dvh, axis=-1, keepdims=True)
                      - vhat * jnp.mean(dvh * vhat, axis=-1, keepdims=True))
        da_ref[:, W:] = (dzv * gv).astype(BF16)

    const2 = lambda shape: pl.BlockSpec(shape, lambda i: (0, 0))
    const3 = pl.BlockSpec((G, LANES, LANES), lambda i: (0, 0, 0))
    return pl.pallas_call(
        body, grid=(T // tb,),
        in_specs=[pl.BlockSpec((tb, W2), lambda i: (i, 0)), pl.BlockSpec((tb, W), lambda i: (i, 0)),
                  const2((1, W)), const2((1, W)), const3, const3, const2((LANES, W))],
        out_specs=[pl.BlockSpec((tb, W2), lambda i: (i, 0)), const3, const2((LANES, W)),
                   const2((SUBLANES, W)), const2((SUBLANES, W))],
        out_shape=[jax.ShapeDtypeStruct((T, W2), BF16), jax.ShapeDtypeStruct((G, LANES, LANES), F32),
                   jax.ShapeDtypeStruct((LANES, W), F32), jax.ShapeDtypeStruct((SUBLANES, W), F32),
                   jax.ShapeDtypeStruct((SUBLANES, W), F32)],
        scratch_shapes=[pltpu.VMEM((tb, W), F32)],
        name=name, compiler_params=_cp(),
    )(a, dgated, ln_g.reshape(1, W), ln_b.reshape(1, W), w_tril, w_tril_t, bias)


def _adam_math(w, g, m, v):
    m = ADAM_B1 * m + (1.0 - ADAM_B1) * g
    v = ADAM_B2 * v + (1.0 - ADAM_B2) * (g * g)
    m_hat = m / (1.0 - ADAM_B1 ** ADAM_STEP)
    v_hat = v / (1.0 - ADAM_B2 ** ADAM_STEP)
    delta = -ADAM_LR * (m_hat / (jnp.sqrt(v_hat) + ADAM_EPS) + ADAM_WD * w)
    return delta, m, v


def _adamw_halves(mine, theirs, c_idx, w, m, v, name):
    R, C = w.shape
    rh = R // 2
    tb = _row_tile(rh, C)
    nb = rh // tb

    def body(c_ref, a_ref, b_ref, w_ref, m_ref, v_ref, g_ref, d_ref, mo_ref, vo_ref):
        g = jnp.where(pl.program_id(0) == c_ref[0], a_ref[...], b_ref[...])
        d, mm, vv = _adam_math(w_ref[...], g, m_ref[...], v_ref[...])
        g_ref[...] = g
        d_ref[...] = d
        mo_ref[...] = mm
        vo_ref[...] = vv

    half = pl.BlockSpec((tb, C), lambda h, i, c: (i, 0))
    row = pl.BlockSpec((tb, C), lambda h, i, c: (h * nb + i, 0))
    sds = jax.ShapeDtypeStruct((R, C), F32)
    return pl.pallas_call(
        body,
        grid_spec=pltpu.PrefetchScalarGridSpec(
            num_scalar_prefetch=1, grid=(2, nb), in_specs=[half, half, row, row, row], out_specs=[row] * 4),
        out_shape=[sds] * 4, name=name, compiler_params=_cp())(c_idx, mine, theirs, w, m, v)


def _adamw_sum(parts, w, m, v, name):
    K, R, C = parts.shape
    tb = _tile(R, 128, SUBLANES)

    def body(p_ref, w_ref, m_ref, v_ref, g_ref, d_ref, mo_ref, vo_ref):
        g = p_ref[0]
        for k in range(1, K):
            g = g + p_ref[k]
        d, mm, vv = _adam_math(w_ref[...], g, m_ref[...], v_ref[...])
        g_ref[...] = g
        d_ref[...] = d
        mo_ref[...] = mm
        vo_ref[...] = vv

    row = pl.BlockSpec((tb, C), lambda i: (i, 0))
    sds = jax.ShapeDtypeStruct((R, C), F32)
    return pl.pallas_call(
        body, grid=(R // tb,),
        in_specs=[pl.BlockSpec((K, tb, C), lambda i: (0, i, 0)), row, row, row],
        out_specs=[row] * 4, out_shape=[sds] * 4, name=name, compiler_params=_cp())(parts, w, m, v)


def _pair_sum(g_all, recv, c_idx, name):
    K, R, C = g_all.shape
    rh = R // 2
    tb = _row_tile(rh, C)
    nb = rh // tb

    def body(c_ref, a_ref, b_ref, o_ref):
        o_ref[...] = (a_ref[...] + b_ref[...]).astype(BF16)

    return pl.pallas_call(
        body,
        grid_spec=pltpu.PrefetchScalarGridSpec(
            num_scalar_prefetch=1, grid=(K, nb),
            in_specs=[pl.BlockSpec((1, tb, C), lambda k, i, c: (k, c[0] * nb + i, 0)),
                      pl.BlockSpec((1, tb, C), lambda k, i, c: (k, i, 0))],
            out_specs=pl.BlockSpec((1, tb, C), lambda k, i, c: (k, i, 0))),
        out_shape=jax.ShapeDtypeStruct((K, rh, C), BF16), name=name, compiler_params=_cp(),
    )(c_idx, g_all, recv)


def _sum_parts(parts, name):
    K, R, C = parts.shape
    tb = _row_tile(R, C)

    def body(p_ref, o_ref):
        g = p_ref[0].astype(F32)
        for k in range(1, K):
            g = g + p_ref[k].astype(F32)
        o_ref[...] = g

    return pl.pallas_call(
        body, grid=(R // tb,), in_specs=[pl.BlockSpec((K, tb, C), lambda i: (0, i, 0))],
        out_specs=pl.BlockSpec((tb, C), lambda i: (i, 0)),
        out_shape=jax.ShapeDtypeStruct((R, C), F32), name=name, compiler_params=_cp())(parts)


_CHIP_RELATIONS = ((1, 0), (0, 1), (1, 1))


def _position():
    return lax.axis_index("x"), lax.axis_index("y"), lax.axis_index("c")


def _flip(v, bit):
    return 1 - v if bit else v


def _gather_weights(w_pack, side, name):
    R, C = w_pack.shape
    rh = R // 2
    n_side = 0 if side is None else 1

    def half(c):
        return pl.ds(pl.multiple_of(c * rh, 16), rh)

    def between_chips(*refs):
        if n_side:
            w_ref, s_ref, ow_ref, os_ref, local_sem, send_sems, recv_sems = refs
        else:
            w_ref, ow_ref, send_sems, recv_sems = refs
        x, y, c = _position()
        me = 2 * x + y
        if n_side:
            own_side = pltpu.make_async_copy(s_ref, os_ref.at[me], local_sem)
            own_side.start()

        def copies(r, slot):
            dx, dy = _CHIP_RELATIONS[r]
            peer = (_flip(x, dx), _flip(y, dy), c)
            out = [pltpu.make_async_remote_copy(
                src_ref=w_ref.at[half(c), :], dst_ref=ow_ref.at[slot, half(c), :], send_sem=send_sems.at[2 * r],
                recv_sem=recv_sems.at[2 * r], device_id=peer, device_id_type=MESH)]
            if n_side:
                out.append(pltpu.make_async_remote_copy(
                    src_ref=s_ref, dst_ref=os_ref.at[slot], send_sem=send_sems.at[2 * r + 1],
                    recv_sem=recv_sems.at[2 * r + 1], device_id=peer, device_id_type=MESH))
            return out

        sent = [cp for r in range(3) for cp in copies(r, me)]
        for cp in sent:
            cp.start()
        for r in range(3):
            dx, dy = _CHIP_RELATIONS[r]
            for cp in copies(r, 2 * _flip(x, dx) + _flip(y, dy)):
                cp.wait_recv()
        for cp in sent:
            cp.wait_send()
        if n_side:
            own_side.wait()

    sems = [pltpu.SemaphoreType.DMA((6,)), pltpu.SemaphoreType.DMA((6,))]
    gathered = jax.ShapeDtypeStruct((4, R, C), w_pack.dtype)
    if n_side:
        halves, sides = pl.pallas_call(
            between_chips, in_specs=[_hbm(), _hbm()], out_specs=[_hbm(), _hbm()],
            out_shape=[gathered, jax.ShapeDtypeStruct((4,) + side.shape, side.dtype)],
            scratch_shapes=[pltpu.SemaphoreType.DMA(())] + sems,
            name=name + "_ici", compiler_params=_cp(),
        )(w_pack, side)
    else:
        sides = None
        halves = pl.pallas_call(
            between_chips, in_specs=[_hbm()], out_specs=_hbm(), out_shape=gathered, scratch_shapes=sems,
            name=name + "_ici", compiler_params=_cp(),
        )(w_pack)

    def to_sibling(g_ref, o_ref, send_sems, recv_sems):
        x, y, c = _position()

        def copy(r, rows):
            dx, dy = _CHIP_RELATIONS[r]
            slot = 2 * _flip(x, dx) + _flip(y, dy)
            return pltpu.make_async_remote_copy(
                src_ref=g_ref.at[slot, rows, :], dst_ref=o_ref.at[slot, rows, :], send_sem=send_sems.at[r],
                recv_sem=recv_sems.at[r], device_id=(x, y, 1 - c), device_id_type=MESH)

        sent = [copy(r, half(c)) for r in range(3)]
        for cp in sent:
            cp.start()
        for r in range(3):
            copy(r, half(1 - c)).wait_recv()
        for cp in sent:
            cp.wait_send()

    full = pl.pallas_call(
        to_sibling, in_specs=[_hbm()], out_specs=_hbm(), input_output_aliases={0: 0},
        out_shape=jax.ShapeDtypeStruct((4, R, C), w_pack.dtype),
        scratch_shapes=[pltpu.SemaphoreType.DMA((3,)), pltpu.SemaphoreType.DMA((3,))],
        name=name + "_d2d", compiler_params=_cp(),
    )(halves)
    return full, sides


def _sibling_halves(g_all, name):
    K, R, C = g_all.shape
    rh = R // 2

    def body(g_ref, o_ref, send_sem, recv_sem):
        x, y, c = _position()
        start = pl.multiple_of((1 - c) * rh, SUBLANES)
        cp = pltpu.make_async_remote_copy(
            src_ref=g_ref.at[:, pl.ds(start, rh), :], dst_ref=o_ref, send_sem=send_sem, recv_sem=recv_sem,
            device_id=(x, y, 1 - c), device_id_type=MESH)
        cp.start()
        cp.wait_recv()
        cp.wait_send()

    return pl.pallas_call(
        body, in_specs=[_hbm()], out_specs=_hbm(),
        out_shape=jax.ShapeDtypeStruct((K, rh, C), F32),
        scratch_shapes=[pltpu.SemaphoreType.DMA(()), pltpu.SemaphoreType.DMA(())],
        name=name, compiler_params=_cp(),
    )(g_all)


def _chip_exchange(parts, name):
    K, R, C = parts.shape

    def body(p_ref, o_ref, local_sem, send_sems, recv_sems):
        x, y, c = _position()
        me = 2 * x + y
        own = pltpu.make_async_copy(p_ref.at[me], o_ref.at[me], local_sem)
        own.start()

        def copy(r, src_slot, dst_slot):
            dx, dy = _CHIP_RELATIONS[r]
            return pltpu.make_async_remote_copy(
                src_ref=p_ref.at[src_slot], dst_ref=o_ref.at[dst_slot], send_sem=send_sems.at[r],
                recv_sem=recv_sems.at[r], device_id=(_flip(x, dx), _flip(y, dy), c), device_id_type=MESH)

        def chip(r):
            dx, dy = _CHIP_RELATIONS[r]
            return 2 * _flip(x, dx) + _flip(y, dy)

        sent = [copy(r, chip(r), me) for r in range(3)]
        for cp in sent:
            cp.start()
        for r in range(3):
            copy(r, me, chip(r)).wait_recv()
        for cp in sent:
            cp.wait_send()
        own.wait()

    return pl.pallas_call(
        body, in_specs=[_hbm()], out_specs=_hbm(),
        out_shape=jax.ShapeDtypeStruct((K, R, C), parts.dtype),
        scratch_shapes=[pltpu.SemaphoreType.DMA(()), pltpu.SemaphoreType.DMA((3,)),
                        pltpu.SemaphoreType.DMA((3,))],
        name=name, compiler_params=_cp(),
    )(parts)


def _swap_with_sibling(half, name):
    rh, C = half.shape

    def body(h_ref, o_ref, send_sem, recv_sem):
        x, y, c = _position()
        cp = pltpu.make_async_remote_copy(
            src_ref=h_ref, dst_ref=o_ref, send_sem=send_sem, recv_sem=recv_sem,
            device_id=(x, y, 1 - c), device_id_type=MESH)
        cp.start()
        cp.wait_recv()
        cp.wait_send()

    return pl.pallas_call(
        body, in_specs=[_hbm()], out_specs=_hbm(),
        out_shape=jax.ShapeDtypeStruct((rh, C), F32),
        scratch_shapes=[pltpu.SemaphoreType.DMA(()), pltpu.SemaphoreType.DMA(())],
        name=name, compiler_params=_cp(),
    )(half)


def _gather_all(part, name):
    R, C = part.shape
    masks = [(b >> 2 & 1, b >> 1 & 1, b & 1) for b in range(1, 8)]

    def body(p_ref, o_ref, local_sem, send_sems, recv_sems):
        x, y, c = _position()
        me = 4 * x + 2 * y + c
        own = pltpu.make_async_copy(p_ref, o_ref.at[me], local_sem)
        own.start()

        def copy(r, slot):
            dx, dy, dc = masks[r]
            return pltpu.make_async_remote_copy(
                src_ref=p_ref, dst_ref=o_ref.at[slot], send_sem=send_sems.at[r], recv_sem=recv_sems.at[r],
                device_id=(_flip(x, dx), _flip(y, dy), _flip(c, dc)), device_id_type=MESH)

        sent = [copy(r, me) for r in range(7)]
        for cp in sent:
            cp.start()
        for r in range(7):
            dx, dy, dc = masks[r]
            copy(r, 4 * _flip(x, dx) + 2 * _flip(y, dy) + _flip(c, dc)).wait_recv()
        for cp in sent:
            cp.wait_send()
        own.wait()

    return pl.pallas_call(
        body, in_specs=[_hbm()], out_specs=_hbm(),
        out_shape=jax.ShapeDtypeStruct((8, R, C), F32),
        scratch_shapes=[pltpu.SemaphoreType.DMA(()), pltpu.SemaphoreType.DMA((7,)),
                        pltpu.SemaphoreType.DMA((7,))],
        name=name, compiler_params=_cp(),
    )(part)


def _pack(arrs, row_mult, cols=PACK_COLS, lead=0):
    head = arrs[0].shape[:lead]
    pieces = []
    for a in arrs:
        flat = a.astype(F32).reshape(head + (-1,))
        fill = -flat.shape[-1] % cols
        if fill:
            flat = jnp.concatenate([flat, jnp.zeros(head + (fill,), F32)], axis=-1)
        pieces.append(flat.reshape(head + (-1, cols)))
    rows = sum(p.shape[lead] for p in pieces)
    fill = -rows % row_mult
    if fill:
        pieces.append(jnp.zeros(head + (fill, cols), F32))
    return jnp.concatenate(pieces, axis=lead) if len(pieces) > 1 else pieces[0]


def _unpack(buf, shapes):
    lead = buf.shape[:-2]
    cols = buf.shape[-1]
    out, off = [], 0
    for shp in shapes:
        n = math.prod(shp)
        rows = -(-n // cols)
        piece = buf[..., off:off + rows, :]
        if rows * cols != n:
            piece = piece.reshape(lead + (-1,))[..., :n]
        out.append(piece.reshape(lead + tuple(shp)))
        off += rows
    return out


def _cols_from_chips(g):
    k, L, A, n = g.shape
    return jnp.transpose(g, (1, 2, 0, 3)).reshape(L, A, k * n)


def _rows_from_chips(g):
    k, L, n, B = g.shape
    return jnp.transpose(g, (1, 0, 2, 3)).reshape(L, k * n, B)


def _cols_to_chips(full, k=4):
    L, A, N = full.shape
    return jnp.transpose(full.reshape(L, A, k, N // k), (2, 0, 1, 3))


def _rows_to_chips(full, k=4):
    L, N, B = full.shape
    return jnp.transpose(full.reshape(L, k, N // k, B), (1, 0, 2, 3))


def kernel(x, mixer_norm_w, attn_w_in, attn_b_f, attn_w_out, sgu_w_in, sgu_ln_g, sgu_ln_b, sgu_w_s, sgu_b_s, sgu_w_out, ffn_norm_w, ffn_w_in, ffn_w_out, final_norm_w, loss_target, m_mixer_norm_w, m_attn_w_in, m_attn_b_f, m_attn_w_out, m_sgu_w_in, m_sgu_ln_g, m_sgu_ln_b, m_sgu_w_s, m_sgu_b_s, m_sgu_w_out, m_ffn_norm_w, m_ffn_w_in, m_ffn_w_out, m_final_norm_w, v_mixer_norm_w, v_attn_w_in, v_attn_b_f, v_attn_w_out, v_sgu_w_in, v_sgu_ln_g, v_sgu_ln_b, v_sgu_w_s, v_sgu_b_s, v_sgu_w_out, v_ffn_norm_w, v_ffn_w_in, v_ffn_w_out, v_final_norm_w):
    T, D = x.shape[1], x.shape[2]
    depth = mixer_norm_w.shape[0]
    H = attn_b_f.shape[1]
    P = D // LANES
    assert D % LANES == 0 and D // H == 64 and 2 * P == H and 2 * P <= LANES
    G = sgu_w_s.shape[1]
    W = sgu_w_out.shape[1] * 4
    assert sgu_w_s.shape[2] == LANES and W == G * LANES
    scale = float(D // H) ** -0.5
    f_pad = LANES
    c_idx = lax.axis_index("c").astype(jnp.int32).reshape(1)

    groups = [
        ([attn_w_out, sgu_w_in, sgu_w_out, ffn_w_out, sgu_ln_g, sgu_ln_b],
         [m_attn_w_out, m_sgu_w_in, m_sgu_w_out, m_ffn_w_out, m_sgu_ln_g, m_sgu_ln_b],
         [v_attn_w_out, v_sgu_w_in, v_sgu_w_out, v_ffn_w_out, v_sgu_ln_g, v_sgu_ln_b]),
        ([ffn_w_in], [m_ffn_w_in], [v_ffn_w_in]),
        ([attn_w_in], [m_attn_w_in], [v_attn_w_in]),
    ]
    group_cols = [D, ffn_w_in.shape[2], attn_w_in.shape[2]]
    group_shapes = [[a.shape for a in g[0]] for g in groups]
    w_packs = [_pack(g[0], 512, cols) for g, cols in zip(groups, group_cols)]
    ln_pack = _pack([sgu_ln_g, sgu_ln_b], SUBLANES)
    my_chip = 2 * lax.axis_index("x") + lax.axis_index("y")
    gathered = []
    for t, w_pack in enumerate(w_packs):
        w_pack_b = w_pack.astype(BF16)
        gat, side = _gather_weights(w_pack_b, ln_pack if t == 0 else None, f"gather_weights_{t}")
        if t == 0:
            gat_ln = side
        gathered.append(_unpack(lax.dynamic_update_index_in_dim(gat, w_pack_b, my_chip, 0), group_shapes[t]))
    (g_ao, g_si, g_so, g_fo, _, _), (g_fi,), (g_ai,) = gathered
    g_lng, g_lnb = _unpack(gat_ln, [sgu_ln_g.shape, sgu_ln_b.shape])
    w_ai = _cols_from_chips(g_ai)
    w_ai = jnp.pad(w_ai, ((0, 0), (0, 0), (0, 3 * D + f_pad - w_ai.shape[2])))
    w_ao = _rows_from_chips(g_ao)
    w_si = _cols_from_chips(g_si)
    w_so = _rows_from_chips(g_so)
    w_fo = _rows_from_chips(g_fo)
    w_fi5 = g_fi.reshape((2, 2) + g_fi.shape[1:])
    ln_g = jnp.transpose(g_lng, (1, 0, 2)).reshape(sgu_ln_g.shape[0], W)
    ln_b = jnp.transpose(g_lnb, (1, 0, 2)).reshape(sgu_ln_b.shape[0], W)
    w_tril = jnp.tril(sgu_w_s)
    w_tril_b = w_tril.astype(BF16)
    w_tril_tb = jnp.swapaxes(w_tril, 2, 3).astype(BF16)
    sgu_bias = jnp.repeat(jnp.swapaxes(sgu_b_s, 1, 2), LANES, axis=2)
    b_f_pad = jnp.pad(attn_b_f, ((0, 0), (0, LANES - H)))

    xs = x.reshape(T, D)
    saved = []
    for i in range(depth):
        j = i // 2
        rec = {"x_in": xs}
        if i % 2 == 0:
            h, qkv, f = _norm_mm(xs, mixer_norm_w[i], w_ai[j], ((3 * D, BF16), (f_pad, F32)), f"attn_qkv_{i}")
            cT, c_cols, c0T = _gate_fwd(f, b_f_pad[j:j + 1], P, f"gate_fwd_{i}")
            o, lseT = _attn_fwd(qkv, cT, P, scale, f"attn_fwd_{i}")
            x_mid = _mm(o, w_ao[j], "nn", F32, f"attn_out_{i}", res=xs)
            rec.update(qkv=qkv, f=f, c0T=c0T, c_cols=c_cols, o=o, lseT=lseT)
        else:
            h, a = _norm_mm(xs, mixer_norm_w[i], w_si[j], ((2 * W, BF16),), f"sgu_in_{i}")
            gated = _sgu_fwd(a, ln_g[j], ln_b[j], w_tril_b[j], sgu_bias[j], f"sgu_fwd_{i}")
            x_mid = _mm(gated, w_so[j], "nn", F32, f"sgu_out_{i}", res=xs)
            rec.update(a=a, gated=gated)
        h2, fa, s = _ffn_in_act(x_mid, ffn_norm_w[i], w_fi5, i, f"ffn_in_{i}")
        xs = _mm(s, w_fo[i], "nn", F32, f"ffn_out_{i}", res=x_mid)
        rec.update(h=h, x_mid=x_mid, h2=h2, fa=fa, s=s)
        saved.append(rec)

    gx, loss_acc, dw_final = _loss_head(xs, final_norm_w, loss_target.reshape(T, D), "loss_head")
    loss = lax.psum(loss_acc[0, 0], ("x", "y", "c"))

    n_attn, n_sgu = attn_w_in.shape[0], sgu_w_in.shape[0]
    d_mixer_norm, d_ffn_norm = [None] * depth, [None] * depth
    d_ai, d_ao, d_bf = [None] * n_attn, [None] * n_attn, [None] * n_attn
    d_si, d_so, d_lng, d_lnb, d_ws, d_bs = ([None] * n_sgu for _ in range(6))
    d_fi, d_fo = [None] * depth, [None] * depth
    for i in reversed(range(depth)):
        j = i // 2
        rec = saved[i]
        d_fo[i] = _mm(rec["s"], gx, "tn", F32, f"ffn_out_wgrad_{i}")
        da = _ffn_out_bwd_act(gx, w_fo[i], rec["fa"], f"ffn_out_bwd_{i}")
        da = da.reshape((4,) + da.shape[2:])
        d_fi[i] = _mm_tn_shards(rec["h2"], da, f"ffn_in_wgrad_{i}")
        gx, dwn = _nt_norm_bwd(da, g_fi, i, rec["x_mid"], ffn_norm_w[i], gx, f"ffn_in_bwd_{i}")
        d_ffn_norm[i] = dwn[0]
        if i % 2 == 0:
            do = _mm(gx, w_ao[j], "nt", BF16, f"attn_out_bwd_{i}")
            d_ao[j] = _mm(rec["o"], gx, "tn", F32, f"attn_out_wgrad_{i}")
            dT = _attn_delta(do, rec["o"], P, f"attn_delta_{i}")
            dq, dk, dv, dc_cols, drowT = _attn_bwd(rec["qkv"], do, rec["lseT"], dT, rec["c0T"], rec["c_cols"],
                                                   P, scale, f"attn_bwd_{i}")
            df, dbf = _gate_bwd(dc_cols, drowT, rec["f"], b_f_pad[j:j + 1], P, f"gate_bwd_{i}")
            d_bf[j] = dbf[0, :H]
            dproj = jnp.concatenate([dq, dk, dv, df.astype(BF16)], axis=1)
            d_ai[j] = _mm(rec["h"], dproj, "tn", F32, f"attn_in_wgrad_{i}")[:, :3 * D + H]
            dmix, w_mix = dproj, w_ai
        else:
            dgated = _mm(gx, w_so[j], "nt", F32, f"sgu_out_bwd_{i}")
            d_so[j] = _mm(rec["gated"], gx, "tn", F32, f"sgu_out_wgrad_{i}")
            da_s, dws, dbias, dlng, dlnb = _sgu_bwd(rec["a"], dgated, ln_g[j], ln_b[j], w_tril_b[j],
                                                    w_tril_tb[j], sgu_bias[j], f"sgu_bwd_{i}")
            d_ws[j] = jnp.tril(dws)
            d_bs[j] = jnp.sum(dbias.reshape(LANES, G, LANES), axis=2).T
            d_lng[j], d_lnb[j] = dlng[0], dlnb[0]
            d_si[j] = _mm(rec["h"], da_s, "tn", F32, f"sgu_in_wgrad_{i}")
            dmix, w_mix = da_s, w_si
        gx, dwn = _nt_norm_bwd(dmix[None], w_mix[None], j, rec["x_in"], mixer_norm_w[i], gx, f"mixer_in_bwd_{i}")
        d_mixer_norm[i] = dwn[0]
    grad_x = gx.reshape(x.shape)

    group_grads = [
        [_rows_to_chips(jnp.stack(d_ao)), _cols_to_chips(jnp.stack(d_si)), _rows_to_chips(jnp.stack(d_so)),
         _rows_to_chips(jnp.stack(d_fo)),
         jnp.transpose(jnp.stack(d_lng).reshape(n_sgu, 4, W // 4), (1, 0, 2)),
         jnp.transpose(jnp.stack(d_lnb).reshape(n_sgu, 4, W // 4), (1, 0, 2))],
        [jnp.stack(d_fi, axis=1)],
        [_cols_to_chips(jnp.stack(d_ai))],
    ]
    reduced = []
    for t, (grads, cols) in enumerate(zip(group_grads, group_cols)):
        g_all = _pack(grads, 512, cols, lead=1)
        from_sibling = _sibling_halves(g_all, f"grad_sibling_halves_{t}")
        pair = _pair_sum(g_all, from_sibling, c_idx, f"grad_pair_sum_{t}")
        from_chips = _chip_exchange(pair, f"grad_chip_exchange_{t}")
        my_half = _sum_parts(from_chips, f"grad_chip_sum_{t}")
        sibling_half = _swap_with_sibling(my_half, f"grad_swap_halves_{t}")
        packs = _adamw_halves(my_half, sibling_half, c_idx, w_packs[t], _pack(groups[t][1], 512, cols),
                              _pack(groups[t][2], 512, cols), f"adamw_sharded_{t}")
        reduced.append([_unpack(p, group_shapes[t]) for p in packs])

    def sharded_outputs(which):
        (ao, si, so, fo, lng, lnb), (fi,), (ai,) = (reduced[t][which] for t in range(3))
        return [ai, ao, si, so, fi, fo, lng, lnb]

    g_sh, d_sh, m_sh, v_sh = (sharded_outputs(w) for w in range(4))

    repl = [mixer_norm_w, attn_b_f, sgu_w_s, sgu_b_s, ffn_norm_w, final_norm_w]
    repl_m = [m_mixer_norm_w, m_attn_b_f, m_sgu_w_s, m_sgu_b_s, m_ffn_norm_w, m_final_norm_w]
    repl_v = [v_mixer_norm_w, v_attn_b_f, v_sgu_w_s, v_sgu_b_s, v_ffn_norm_w, v_final_norm_w]
    repl_shapes = [a.shape for a in repl]
    repl_grads = [jnp.stack(d_mixer_norm), jnp.stack(d_bf), jnp.stack(d_ws), jnp.stack(d_bs),
                  jnp.stack(d_ffn_norm), dw_final[0]]
    parts = _gather_all(_pack(repl_grads, SUBLANES), "grad_gather_replicated")
    g_rep, d_rep, m_rep, v_rep = _adamw_sum(parts, _pack(repl, SUBLANES), _pack(repl_m, SUBLANES),
                                            _pack(repl_v, SUBLANES), "adamw_replicated")
    g_r = _unpack(g_rep, repl_shapes)
    d_r = _unpack(d_rep, repl_shapes)
    m_r = _unpack(m_rep, repl_shapes)
    v_r = _unpack(v_rep, repl_shapes)

    def ordered(sh, rp):
        ai, ao, si, so, fi, fo, lng, lnb = sh
        mn, bf, ws, bs, fn, fin = rp
        return [mn, ai, bf, ao, si, lng, lnb, ws, bs, so, fn, fi, fo, fin]

    return (loss, grad_x, *ordered(g_sh, g_r), *ordered(d_sh, d_r), *ordered(m_sh, m_r), *ordered(v_sh, v_r))
```

```python
import math

import jax
import jax.numpy as jnp
from jax import lax
from jax.experimental import pallas as pl
from jax.experimental.pallas import tpu as pltpu

F32 = jnp.float32
BF16 = jnp.bfloat16
NORM_EPS = 1e-6
LN_EPS = 1e-5
ADAM_LR = 0.001
ADAM_B1 = 0.9
ADAM_B2 = 0.999
ADAM_EPS = 1e-08
ADAM_WD = 0.01
ADAM_STEP = 10

LANES = 128
SUBLANES = 8
PACK_COLS = 1024
VMEM_LIMIT = 56 * 1024 * 1024
NEG_BIG = -1e30
LOG2E = 1.4426950408889634
MESH = pl.DeviceIdType.MESH


def _cp():
    return pltpu.CompilerParams(vmem_limit_bytes=VMEM_LIMIT)


def _tile(n, cap, mult):
    best = None
    d = mult
    while d <= min(n, cap):
        if n % d == 0:
            best = d
        d += mult
    return n if best is None else best


def _row_tile(rows, cols):
    cap = max(16, (512 * 1024 // cols) // 16 * 16)
    return _tile(rows, cap, 16)


def _hbm():
    return pl.BlockSpec(memory_space=pltpu.HBM)


def _nt_norm_bwd(a3, b4, layer, x, w, dres, name):
    S, T, Ks = a3.shape
    D = x.shape[1]
    tm = _tile(T, 256, 16)

    def body(a_ref, b_ref, x_ref, w_ref, dres_ref, dx_ref, dw_ref):
        @pl.when(pl.program_id(0) == 0)
        def _():
            dw_ref[...] = jnp.zeros_like(dw_ref)

        dh = _nt(a_ref[0].astype(BF16), b_ref[0, 0])
        for s in range(1, S):
            dh = dh + _nt(a_ref[s].astype(BF16), b_ref[s, 0])
        xf = x_ref[...]
        r = lax.rsqrt(jnp.mean(xf * xf, axis=-1, keepdims=True) + NORM_EPS)
        xhat = xf * r
        dxhat = dh * w_ref[...]
        dx_ref[...] = dres_ref[...] + r * (dxhat - xhat * jnp.mean(dxhat * xhat, axis=-1, keepdims=True))
        dw_ref[...] += jnp.sum(dh * xhat, axis=0, keepdims=True)

    row = pl.BlockSpec((tm, D), lambda i: (i, 0))
    return pl.pallas_call(
        body, grid=(T // tm,),
        in_specs=[pl.BlockSpec((S, tm, Ks), lambda i: (0, i, 0)),
                  pl.BlockSpec((S, 1, D, Ks), lambda i: (0, layer, 0, 0)),
                  row, pl.BlockSpec((1, D), lambda i: (0, 0)), row],
        out_specs=[row, pl.BlockSpec((SUBLANES, D), lambda i: (0, 0))],
        out_shape=[jax.ShapeDtypeStruct((T, D), F32), jax.ShapeDtypeStruct((SUBLANES, D), F32)],
        name=name, compiler_params=_cp(),
    )(a3, b4, x, w.reshape(1, D), dres)


def _mm(a, b, mode, out_dtype, name, res=None):
    if mode == "tn":
        kt, M = a.shape
        N = b.shape[1]
        tm = _tile(M, 1408, LANES)
        tn = _tile(N, 1408, LANES)
        tk = _tile(kt, 1024, 16)

        def body(a_ref, b_ref, o_ref):
            @pl.when(pl.program_id(2) == 0)
            def _():
                o_ref[...] = jnp.zeros_like(o_ref)

            o_ref[...] += lax.dot_general(
                a_ref[...].astype(BF16), b_ref[...].astype(BF16), (((0,), (0,)), ((), ())),
                preferred_element_type=F32)

        return pl.pallas_call(
            body, grid=(M // tm, N // tn, kt // tk),
            in_specs=[pl.BlockSpec((tk, tm), lambda i, j, k: (k, i)),
                      pl.BlockSpec((tk, tn), lambda i, j, k: (k, j))],
            out_specs=pl.BlockSpec((tm, tn), lambda i, j, k: (i, j)),
            out_shape=jax.ShapeDtypeStruct((M, N), F32), name=name, compiler_params=_cp(),
        )(a, b)

    M, K = a.shape
    N = b.shape[1] if mode == "nn" else b.shape[0]
    tm = _tile(M, 512, 16)
    cap = min(3072, (6 << 20) // (2 * K), (4 << 20) // (tm * jnp.dtype(out_dtype).itemsize))
    tn = _tile(N, max(LANES, cap // LANES * LANES), LANES)
    dims = (((1,), (0,)), ((), ())) if mode == "nn" else (((1,), (1,)), ((), ()))

    def body(*refs):
        if res is None:
            a_ref, b_ref, o_ref = refs
        else:
            a_ref, b_ref, r_ref, o_ref = refs
        acc = lax.dot_general(a_ref[...].astype(BF16), b_ref[...].astype(BF16), dims,
                              preferred_element_type=F32)
        if res is not None:
            acc = acc + r_ref[...]
        o_ref[...] = acc.astype(out_dtype)

    b_spec = (pl.BlockSpec((K, tn), lambda j, i: (0, j)) if mode == "nn"
              else pl.BlockSpec((tn, K), lambda j, i: (j, 0)))
    in_specs = [pl.BlockSpec((tm, K), lambda j, i: (i, 0)), b_spec]
    args = [a, b]
    if res is not None:
        in_specs.append(pl.BlockSpec((tm, tn), lambda j, i: (i, j)))
        args.append(res)
    return pl.pallas_call(
        body, grid=(N // tn, M // tm), in_specs=in_specs,
        out_specs=pl.BlockSpec((tm, tn), lambda j, i: (i, j)),
        out_shape=jax.ShapeDtypeStruct((M, N), out_dtype), name=name, compiler_params=_cp(),
    )(*args)


def _normed(x_ref, w_ref):
    xf = x_ref[...]
    r = lax.rsqrt(jnp.mean(xf * xf, axis=-1, keepdims=True) + NORM_EPS)
    return (xf * r * w_ref[...]).astype(BF16)


def _ffn_in_act(x, norm_w, w5, layer, name):
    T, D = x.shape
    n = w5.shape[-1]
    tm = _tile(T, 256, 16)

    def body(x_ref, nw_ref, w_ref, h_ref, a_ref, s_ref):
        hv = _normed(x_ref, nw_ref)
        h_ref[...] = hv
        for half in range(2):
            g = jnp.dot(hv, w_ref[0, half, 0], preferred_element_type=F32)
            u = jnp.dot(hv, w_ref[1, half, 0], preferred_element_type=F32)
            a_ref[0, half] = g.astype(BF16)
            a_ref[1, half] = u.astype(BF16)
            s_ref[:, half * n:(half + 1) * n] = (g * jax.nn.sigmoid(g) * u).astype(BF16)

    return pl.pallas_call(
        body, grid=(T // tm,),
        in_specs=[pl.BlockSpec((tm, D), lambda i: (i, 0)), pl.BlockSpec((1, D), lambda i: (0, 0)),
                  pl.BlockSpec((2, 2, 1, D, n), lambda i: (0, 0, layer, 0, 0))],
        out_specs=[pl.BlockSpec((tm, D), lambda i: (i, 0)),
                   pl.BlockSpec((2, 2, tm, n), lambda i: (0, 0, i, 0)), pl.BlockSpec((tm, 2 * n), lambda i: (i, 0))],
        out_shape=[jax.ShapeDtypeStruct((T, D), BF16), jax.ShapeDtypeStruct((2, 2, T, n), BF16),
                   jax.ShapeDtypeStruct((T, 2 * n), BF16)],
        name=name, compiler_params=_cp(),
    )(x, norm_w.reshape(1, D), w5)


def _norm_mm(x, norm_w, b, splits, name):
    T, D = x.shape
    N = b.shape[1]
    assert sum(wd for wd, _ in splits) == N
    tm = _tile(T, 256, 16)

    def body(x_ref, nw_ref, b_ref, h_ref, *outs):
        hv = _normed(x_ref, nw_ref)
        h_ref[...] = hv
        off = 0
        for o_ref, (wd, dt) in zip(outs, splits):
            o_ref[...] = jnp.dot(hv, b_ref[:, off:off + wd], preferred_element_type=F32).astype(dt)
            off += wd

    return pl.pallas_call(
        body, grid=(T // tm,),
        in_specs=[pl.BlockSpec((tm, D), lambda i: (i, 0)), pl.BlockSpec((1, D), lambda i: (0, 0)),
                  pl.BlockSpec((D, N), lambda i: (0, 0))],
        out_specs=[pl.BlockSpec((tm, D), lambda i: (i, 0))] + [pl.BlockSpec((tm, wd), lambda i: (i, 0)) for wd, _ in splits],
        out_shape=[jax.ShapeDtypeStruct((T, D), BF16)] + [jax.ShapeDtypeStruct((T, wd), dt) for wd, dt in splits],
        name=name, compiler_params=_cp(),
    )(x, norm_w.reshape(1, D), b)


def _ffn_out_bwd_act(gx, w_out, a4, name):
    T, D = gx.shape
    n = a4.shape[-1]
    tm = _tile(T, 256, 16)

    step = 3 * LANES if n % LANES == 0 and n > 3 * LANES else n
    pieces = [(c, min(step, n - c)) for c in range(0, n, step)]

    def body(gx_ref, w_ref, a_ref, da_ref):
        gxb = gx_ref[...].astype(BF16)
        for c, wd in pieces:
            ds = _nt(gxb, w_ref[c:c + wd, :])
            g = a_ref[0, 0, :, c:c + wd].astype(F32)
            u = a_ref[1, 0, :, c:c + wd].astype(F32)
            sg = jax.nn.sigmoid(g)
            da_ref[0, 0, :, c:c + wd] = (ds * u * (sg * (1.0 + g * (1.0 - sg)))).astype(BF16)
            da_ref[1, 0, :, c:c + wd] = (ds * (g * sg)).astype(BF16)

    blk = pl.BlockSpec((2, 1, tm, n), lambda j, i: (0, j, i, 0))
    return pl.pallas_call(
        body, grid=(2, T // tm),
        in_specs=[pl.BlockSpec((tm, D), lambda j, i: (i, 0)), pl.BlockSpec((n, D), lambda j, i: (j, 0)), blk],
        out_specs=blk,
        out_shape=jax.ShapeDtypeStruct((2, 2, T, n), BF16), name=name, compiler_params=_cp(),
    )(gx, w_out, a4)


def _mm_tn_shards(h, a4, name):
    K, T, n = a4.shape
    D = h.shape[1]
    tk = _tile(T, 1024, 16)

    def body(h_ref, a_ref, o_ref):
        @pl.when(pl.program_id(1) == 0)
        def _():
            o_ref[...] = jnp.zeros_like(o_ref)

        o_ref[0] += lax.dot_general(h_ref[...], a_ref[0], (((0,), (0,)), ((), ())), preferred_element_type=F32)

    return pl.pallas_call(
        body, grid=(K, T // tk),
        in_specs=[pl.BlockSpec((tk, D), lambda k, t: (t, 0)), pl.BlockSpec((1, tk, n), lambda k, t: (k, t, 0))],
        out_specs=pl.BlockSpec((1, D, n), lambda k, t: (k, 0, 0)),
        out_shape=jax.ShapeDtypeStruct((K, D, n), F32), name=name, compiler_params=_cp(),
    )(h, a4)


def _loss_head(x, w, tgt, name):
    T, D = x.shape
    tm = _tile(T, 512, SUBLANES)

    def body(x_ref, w_ref, t_ref, dx_ref, loss_ref, dw_ref):
        @pl.when(pl.program_id(0) == 0)
        def _():
            loss_ref[...] = jnp.zeros_like(loss_ref)
            dw_ref[...] = jnp.zeros_like(dw_ref)

        xf = x_ref[...]
        wv = w_ref[...]
        r = lax.rsqrt(jnp.mean(xf * xf, axis=-1, keepdims=True) + NORM_EPS)
        xhat = xf * r
        err = xhat * wv - t_ref[...]
        per_tok = jnp.mean(err * err, axis=-1, keepdims=True)
        loss_ref[...] += 0.5 * jnp.sum(per_tok, axis=0, keepdims=True)
        dy = err * (1.0 / D)
        dxhat = dy * wv
        dx_ref[...] = r * (dxhat - xhat * jnp.mean(dxhat * xhat, axis=-1, keepdims=True))
        dw_ref[...] += jnp.sum(dy * xhat, axis=0, keepdims=True)

    row = pl.BlockSpec((tm, D), lambda i: (i, 0))
    return pl.pallas_call(
        body, grid=(T // tm,),
        in_specs=[row, pl.BlockSpec((1, D), lambda i: (0, 0)), row],
        out_specs=[row, pl.BlockSpec((SUBLANES, LANES), lambda i: (0, 0)),
                   pl.BlockSpec((SUBLANES, D), lambda i: (0, 0))],
        out_shape=[jax.ShapeDtypeStruct((T, D), F32), jax.ShapeDtypeStruct((SUBLANES, LANES), F32),
                   jax.ShapeDtypeStruct((SUBLANES, D), F32)],
        name=name, compiler_params=_cp(),
    )(x, w.reshape(1, D), tgt)


def _split3(v):
    hi = v.astype(BF16)
    r1 = v - hi.astype(F32)
    mid = r1.astype(BF16)
    lo = (r1 - mid.astype(F32)).astype(BF16)
    return hi, mid, lo


def _tri_dot(tri, v):
    out = None
    for piece in _split3(v):
        t = jnp.dot(tri, piece, preferred_element_type=F32)
        out = t if out is None else out + t
    return out


def _q_block(T):
    return _tile(T, 256, LANES)


def _gate_fwd(f, b_f, P, name):
    T = f.shape[0]
    tb = _q_block(T)

    def body(f_ref, b_ref, ct_ref, cc_ref, c0_ref, carry):
        @pl.when(pl.program_id(0) == 0)
        def _():
            carry[...] = jnp.zeros_like(carry)

        z = f_ref[...] + b_ref[...]
        logf = jnp.minimum(z, 0.0) - jnp.log(1.0 + jnp.exp(-jnp.abs(z)))
        row = lax.broadcasted_iota(jnp.int32, (tb, tb), 0)
        col = lax.broadcasted_iota(jnp.int32, (tb, tb), 1)
        tri = (col <= row).astype(BF16)
        c = _tri_dot(tri, logf) + carry[0:1, :]
        carry[...] = jnp.broadcast_to(c[tb - 1:tb, :], carry.shape)
        first = jnp.broadcast_to(c[0:1, :], c.shape)
        for p in range(P):
            shifted = c if p == 0 else pltpu.roll(c, LANES - 2 * p, 1)
            cc_ref[p] = shifted
            ct_ref[p] = shifted.T[0:SUBLANES, :]
            c0_ref[p] = (first if p == 0 else pltpu.roll(first, LANES - 2 * p, 1)).T[0:SUBLANES, :]

    rows = pl.BlockSpec((P, SUBLANES, tb), lambda i: (0, 0, i))
    return pl.pallas_call(
        body, grid=(T // tb,),
        in_specs=[pl.BlockSpec((tb, LANES), lambda i: (i, 0)), pl.BlockSpec((1, LANES), lambda i: (0, 0))],
        out_specs=[rows, pl.BlockSpec((P, tb, LANES), lambda i: (0, i, 0)), rows],
        out_shape=[jax.ShapeDtypeStruct((P, SUBLANES, T), F32), jax.ShapeDtypeStruct((P, T, LANES), F32),
                   jax.ShapeDtypeStruct((P, SUBLANES, T), F32)],
        scratch_shapes=[pltpu.VMEM((SUBLANES, LANES), F32)],
        name=name, compiler_params=_cp(),
    )(f, b_f)


def _gate_bwd(dc_cols, drowT, f, b_f, P, name):
    T = f.shape[0]
    tb = _tile(T, 256, LANES)
    nb = T // tb

    def body(dc_ref, dr_ref, f_ref, b_ref, df_ref, db_ref, carry):
        @pl.when(pl.program_id(0) == 0)
        def _():
            carry[...] = jnp.zeros_like(carry)
            db_ref[...] = jnp.zeros_like(db_ref)

        lane = lax.broadcasted_iota(jnp.int32, (tb, LANES), 1)
        dc = jnp.zeros((tb, LANES), F32)
        for p in range(P):
            rows = jnp.concatenate([dr_ref[p], jnp.zeros((LANES - SUBLANES, tb), F32)], axis=0)
            part = jnp.where(lane < 2, dc_ref[p] + rows.T, 0.0)
            dc = dc + (part if p == 0 else pltpu.roll(part, 2 * p, 1))
        row = lax.broadcasted_iota(jnp.int32, (tb, tb), 0)
        col = lax.broadcasted_iota(jnp.int32, (tb, tb), 1)
        tri = (col >= row).astype(BF16)
        dlogf = _tri_dot(tri, dc) + carry[0:1, :]
        carry[...] = jnp.broadcast_to(dlogf[0:1, :], carry.shape)
        z = f_ref[...] + b_ref[...]
        df = jnp.where(lane < 2 * P, dlogf * jax.nn.sigmoid(-z), 0.0)
        df_ref[...] = df
        db_ref[...] += jnp.sum(df, axis=0, keepdims=True)

    return pl.pallas_call(
        body, grid=(nb,),
        in_specs=[pl.BlockSpec((P, tb, LANES), lambda i: (0, nb - 1 - i, 0)),
                  pl.BlockSpec((P, SUBLANES, tb), lambda i: (0, 0, nb - 1 - i)),
                  pl.BlockSpec((tb, LANES), lambda i: (nb - 1 - i, 0)),
                  pl.BlockSpec((1, LANES), lambda i: (0, 0))],
        out_specs=[pl.BlockSpec((tb, LANES), lambda i: (nb - 1 - i, 0)),
                   pl.BlockSpec((SUBLANES, LANES), lambda i: (0, 0))],
        out_shape=[jax.ShapeDtypeStruct((T, LANES), F32), jax.ShapeDtypeStruct((SUBLANES, LANES), F32)],
        scratch_shapes=[pltpu.VMEM((SUBLANES, LANES), F32)],
        name=name, compiler_params=_cp(),
    )(dc_cols, drowT, f, b_f)


def _nt(a, b):
    return lax.dot_general(a, b, (((1,), (1,)), ((), ())), preferred_element_type=F32)


def _attn_fwd(qkv, cT, P, scale, name, riders=()):
    T = qkv.shape[0]
    n_r = len(riders)
    tq = _q_block(T)
    tw = _tile(T, 8 * tq, 2 * tq)
    cw = tw // 2
    assert cw % tq == 0, "the sequence must split into chunks of whole query blocks"
    nq = T // tq

    def body(q_ref, k_ref, v_ref, c_ref, *rest):
        w_refs, (o_ref, lse_ref), ow_refs = rest[:n_r], rest[n_r:n_r + 2], rest[n_r + 2:2 * n_r + 2]
        s_scr = rest[2 * n_r + 2]
        i = pl.program_id(1)
        if n_r:
            send_sems, recv_sems = rest[2 * n_r + 3:]
            first = (pl.program_id(0) == 0) & (i == 0)
            final = (pl.program_id(0) == P - 1) & (i == nq - 1)

            @pl.when(first)
            def _():
                for t in range(n_r):
                    for cp in _shard_half_copies(w_refs[t], ow_refs[t], send_sems, recv_sems, 3 * t, False):
                        cp.start()

            @pl.when(final)
            def _():
                for t in range(n_r):
                    for cp in _shard_half_copies(w_refs[t], ow_refs[t], send_sems, recv_sems, 3 * t, True):
                        cp.wait_recv()
                    for cp in _shard_half_copies(w_refs[t], ow_refs[t], send_sems, recv_sems, 3 * t, False):
                        cp.wait_send()

        lane = lax.broadcasted_iota(jnp.int32, (1, LANES), 1)
        q = (q_ref[...].astype(F32) * (scale * LOG2E)).astype(BF16)
        q_heads = (jnp.where(lane < 64, q, jnp.zeros_like(q)), jnp.where(lane >= 64, q, jnp.zeros_like(q)))
        c0 = c_ref[0, :, pl.ds(pl.multiple_of(i * tq, tq), LANES)][:, 0:1]

        def scores(start, width, a):
            bias = (c0 - c_ref[0, :, pl.ds(start, width)]) * LOG2E
            return _nt(q_heads[a], k_ref[pl.ds(start, width), :]) + bias[a:a + 1, :]

        def softmax_pv(start, width, s_of, carry):
            v = v_ref[pl.ds(start, width), :]
            one = jnp.ones_like(v)
            v_heads = (jnp.where(lane < 64, v, one), jnp.where(lane >= 64, v, one))
            new = []
            for a in range(2):
                m, acc = carry[a]
                s = s_of(a)
                m_new = jnp.maximum(m, jnp.max(s, axis=1, keepdims=True))
                p = jnp.exp2(s - m_new)
                acc = jnp.exp2(m - m_new) * acc + jnp.dot(p.astype(BF16), v_heads[a], preferred_element_type=F32)
                new.append((m_new, acc))
            return tuple(new)

        def fill(start, buf):
            for a in range(2):
                s_scr[2 * buf + a] = scores(start, cw, a)

        def wide(j, carry):
            base = pl.multiple_of(j * tw, tw)
            fill(base + cw, 1)
            carry = softmax_pv(base, cw, lambda a: s_scr[a], carry)
            fill(base + tw, 0)
            return softmax_pv(base + cw, cw, lambda a: s_scr[2 + a], carry)

        init = tuple((jnp.full((tq, 1), NEG_BIG, F32), jnp.zeros((tq, LANES), F32)) for _ in range(2))
        n_wide = (i * tq) // tw
        fill(0, 0)
        carry = lax.fori_loop(0, n_wide, wide, init)

        base = pl.multiple_of(n_wide * tw, tw)
        ahead = i * tq - base
        col_minus_row = (lax.broadcasted_iota(jnp.int32, (tq, cw), 1)
                         - lax.broadcasted_iota(jnp.int32, (tq, cw), 0))

        def causal(buf, first_key):
            return lambda a: jnp.where(col_minus_row <= ahead - first_key, s_scr[2 * buf + a], NEG_BIG)

        def one_chunk(cr):
            return softmax_pv(base, cw, causal(0, 0), cr)

        def two_chunks(cr):
            fill(base + cw, 1)
            cr = softmax_pv(base, cw, causal(0, 0), cr)
            return softmax_pv(base + cw, cw, causal(1, cw), cr)

        (m0, a0), (m1, a1) = lax.cond(ahead >= cw, two_chunks, one_chunk, carry)
        sums = jnp.where(lane < 64, pltpu.roll(a0, 64, 1), pltpu.roll(a1, 64, 1))
        o_ref[...] = (jnp.where(lane < 64, a0, a1) / sums).astype(BF16)
        l0, l1 = a0[:, 64:65], a1[:, 0:1]
        lse = jnp.where(lane == 0, m0 + jnp.log2(l0), jnp.where(lane == 1, m1 + jnp.log2(l1), 0.0))
        lse_ref[0] = lse.T[0:SUBLANES, :]

    return pl.pallas_call(
        body, grid=(P, nq),
        in_specs=[pl.BlockSpec((tq, LANES), lambda p, i: (i, p)),
                  pl.BlockSpec((T, LANES), lambda p, i: (0, P + p)),
                  pl.BlockSpec((T, LANES), lambda p, i: (0, 2 * P + p)),
                  pl.BlockSpec((1, SUBLANES, T), lambda p, i: (p, 0, 0))] + [_hbm()] * n_r,
        out_specs=[pl.BlockSpec((tq, LANES), lambda p, i: (i, p)),
                   pl.BlockSpec((1, SUBLANES, tq), lambda p, i: (p, 0, i))] + [_hbm()] * n_r,
        out_shape=[jax.ShapeDtypeStruct((T, LANES * P), BF16), jax.ShapeDtypeStruct((P, SUBLANES, T), F32)]
        + [jax.ShapeDtypeStruct((4,) + w.shape, w.dtype) for w in riders],
        scratch_shapes=[pltpu.VMEM((4, tq, cw), F32)]
        + ([pltpu.SemaphoreType.DMA((3 * n_r,)), pltpu.SemaphoreType.DMA((3 * n_r,))] if n_r else []),
        name=name, compiler_params=_cp(),
    )(qkv, qkv, qkv, cT, *riders)


def _attn_delta(do, o, P, name):
    T, D = o.shape
    tb = _tile(T, 256, LANES)

    def body(do_ref, o_ref, d_ref):
        lane = lax.broadcasted_iota(jnp.int32, (1, LANES), 1)
        for p in range(P):
            cols = slice(p * LANES, (p + 1) * LANES)
            prod = do_ref[:, cols].astype(F32) * o_ref[:, cols].astype(F32)
            d0 = jnp.sum(jnp.where(lane < 64, prod, 0.0), axis=1, keepdims=True)
            d1 = jnp.sum(jnp.where(lane >= 64, prod, 0.0), axis=1, keepdims=True)
            both = jnp.where(lane == 0, d0, jnp.where(lane == 1, d1, 0.0))
            d_ref[p] = both.T[0:SUBLANES, :]

    return pl.pallas_call(
        body, grid=(T // tb,),
        in_specs=[pl.BlockSpec((tb, D), lambda i: (i, 0)), pl.BlockSpec((tb, D), lambda i: (i, 0))],
        out_specs=pl.BlockSpec((P, SUBLANES, tb), lambda i: (0, 0, i)),
        out_shape=jax.ShapeDtypeStruct((P, SUBLANES, T), F32), name=name, compiler_params=_cp(),
    )(do, o)


def _attn_bwd(qkv, do, lseT, dT, c0T, c_cols, P, scale, name):
    T = qkv.shape[0]
    tq = _q_block(T)
    tw = _tile(T, 4 * tq, 2 * tq)
    cw = tw // 2
    assert cw % tq == 0, "the sequence must split into chunks of whole query blocks"
    nq = T // tq

    def body(q_ref, do_ref, k_ref, v_ref, lse_ref, d_ref, c0_ref, cc_ref,
             dq_ref, dk_ref, dv_ref, dc_ref, drow_ref, dq_acc0, dq_acc1, s_scr):
        j = pl.program_id(1)

        @pl.when(j == 0)
        def _():
            dq_acc0[...] = jnp.zeros_like(dq_acc0)
            dq_acc1[...] = jnp.zeros_like(dq_acc1)

        lane = lax.broadcasted_iota(jnp.int32, (1, LANES), 1)
        in_head = (lane < 64, lane >= 64)
        k = k_ref[...]
        v = v_ref[...]
        zero = jnp.zeros_like(k)
        one = jnp.ones_like(k)
        k_heads = tuple(jnp.where(h, k, zero) for h in in_head)
        v_heads = tuple(jnp.where(h, v, zero) for h in in_head)
        k_ones = tuple(jnp.where(h, k, one) for h in in_head)
        cc = cc_ref[0]
        c_first = (cc[0:1, 0:1], cc[0:1, 1:2])
        c_rel = ((cc[:, 0:1] - c_first[0]) * LOG2E, (cc[:, 1:2] - c_first[1]) * LOG2E)
        dq_accs = (dq_acc0, dq_acc1)

        def scaled_q(start, width, factor):
            return (q_ref[pl.ds(start, width), :].astype(F32) * factor).astype(BF16)

        def block(start, width, carry, first_query=None, scores=None):
            q = scaled_q(start, width, scale)
            q_one = jnp.ones_like(q)
            dov = do_ref[pl.ds(start, width), :]
            lse = lse_ref[0, :, pl.ds(start, width)]
            dlt = d_ref[0, :, pl.ds(start, width)]
            c0 = c0_ref[0, :, pl.ds(start, width)]
            new = []
            for a in range(2):
                dk_a, dv_a = carry[a]
                rowv = lse[a:a + 1, :] + (c_first[a] - c0[a:a + 1, :]) * LOG2E
                if scores is None:
                    st = _nt(k_heads[a], scaled_q(start, width, scale * LOG2E))
                else:
                    st = scores(a)
                pt = jnp.exp2((st - c_rel[a]) - rowv)
                if first_query is not None:
                    row = lax.broadcasted_iota(jnp.int32, (tq, width), 0)
                    col = lax.broadcasted_iota(jnp.int32, (tq, width), 1)
                    pt = jnp.where(col - row >= first_query, pt, 0.0)
                dpt = _nt(v_heads[a], dov)
                dst_b = (pt * (dpt - dlt[a:a + 1, :])).astype(BF16)
                dv_a = dv_a + jnp.dot(pt.astype(BF16), dov, preferred_element_type=F32)
                dk_a = dk_a + jnp.dot(dst_b, jnp.where(in_head[a], q, q_one), preferred_element_type=F32)
                dq_accs[a][pl.ds(start, width), :] += lax.dot_general(
                    dst_b, k_ones[a], (((0,), (0,)), ((), ())), preferred_element_type=F32)
                new.append((dk_a, dv_a))
            return tuple(new)

        first_key = j * tq
        first_wide = first_key // tw + 1
        last = T // tw - 1

        def fill(trip, buf):
            q = scaled_q(pl.multiple_of(jnp.minimum(trip, last) * tw, tw), tw, scale * LOG2E)
            for a in range(2):
                s_scr[2 * buf + a] = _nt(k_heads[a], q)

        def trip(i, buf, cr):
            return block(pl.multiple_of(i * tw, tw), tw, cr, scores=lambda a: s_scr[2 * buf + a])

        def two_trips(p, cr):
            i = first_wide + 2 * p
            fill(i + 1, 1)
            cr = trip(i, 0, cr)
            fill(i + 2, 0)
            return trip(i + 1, 1, cr)

        init = tuple((jnp.zeros((tq, LANES), F32), jnp.zeros((tq, LANES), F32)) for _ in range(2))
        fill(first_wide, 0)
        diag = pl.multiple_of((first_key // cw) * cw, cw)
        carry = block(diag, cw, init, first_key - diag)
        carry = lax.cond(
            diag + cw < first_wide * tw,
            lambda cr: block(pl.multiple_of(diag + cw, cw), cw, cr), lambda cr: cr, carry)
        n_trips = last + 1 - first_wide
        carry = lax.fori_loop(0, n_trips // 2, two_trips, carry)
        (dk0, dv0), (dk1, dv1) = lax.cond(n_trips % 2 == 1, lambda cr: trip(last, 0, cr), lambda cr: cr, carry)
        dk_ref[...] = jnp.where(lane < 64, dk0, dk1).astype(BF16)
        dv_ref[...] = jnp.where(lane < 64, dv0, dv1).astype(BF16)
        dc_ref[0] = jnp.where(lane == 0, -dk0[:, 64:65], jnp.where(lane == 1, -dk1[:, 0:1], 0.0))

        @pl.when(j == nq - 1)
        def _():
            def finish(i, _):
                rows = pl.ds(pl.multiple_of(i * tq, tq), tq)
                a0 = dq_acc0[rows, :]
                a1 = dq_acc1[rows, :]
                dq_ref[rows, :] = (jnp.where(lane < 64, a0, a1) * scale).astype(BF16)
                sums = jnp.where(lane == 0, a0[:, 64:65], jnp.where(lane == 1, a1[:, 0:1], 0.0))
                drow_ref[0, :, rows] = sums.T[0:SUBLANES, :]
                return 0

            lax.fori_loop(0, nq, finish, 0)

    full = lambda col: pl.BlockSpec((T, LANES), lambda p, j: (0, col(p)))
    blk = lambda col: pl.BlockSpec((tq, LANES), lambda p, j: (j, col(p)))
    rows = pl.BlockSpec((1, SUBLANES, T), lambda p, j: (p, 0, 0))
    cols = pl.BlockSpec((1, tq, LANES), lambda p, j: (p, j, 0))
    D = LANES * P
    return pl.pallas_call(
        body, grid=(P, nq),
        in_specs=[full(lambda p: p), full(lambda p: p), blk(lambda p: P + p), blk(lambda p: 2 * P + p),
                  rows, rows, rows, cols],
        out_specs=[full(lambda p: p), blk(lambda p: p), blk(lambda p: p), cols, rows],
        out_shape=[jax.ShapeDtypeStruct((T, D), BF16), jax.ShapeDtypeStruct((T, D), BF16),
                   jax.ShapeDtypeStruct((T, D), BF16), jax.ShapeDtypeStruct((P, T, LANES), F32),
                   jax.ShapeDtypeStruct((P, SUBLANES, T), F32)],
        scratch_shapes=[pltpu.VMEM((T, LANES), F32), pltpu.VMEM((T, LANES), F32), pltpu.VMEM((4, tq, tw), F32)],
        name=name, compiler_params=_cp(),
    )(qkv, do, qkv, qkv, lseT, dT, c0T, c_cols)


_SQRT_HALF = 0.7071067811865476
_INV_SQRT_2PI = 0.3989422804014327


def _gelu(v):
    return 0.5 * v * (1.0 + lax.erf(v * _SQRT_HALF))


def _gelu_and_grad(v):
    cdf = 0.5 * (1.0 + lax.erf(v * _SQRT_HALF))
    return v * cdf, cdf + v * (_INV_SQRT_2PI * jnp.exp(-0.5 * v * v))


def _sgu_fwd(a, ln_g, ln_b, w_tril, bias, name):
    T, W2 = a.shape
    W = W2 // 2
    G = w_tril.shape[0]
    tb = _tile(T, 256, LANES)

    def body(a_ref, g_ref, b_ref, w_ref, bias_ref, out_ref):
        zu = _gelu(a_ref[:, :W].astype(F32))
        zv = _gelu(a_ref[:, W:].astype(F32))
        mu = jnp.mean(zv, axis=-1, keepdims=True)
        d = zv - mu
        rstd = lax.rsqrt(jnp.mean(d * d, axis=-1, keepdims=True) + LN_EPS)
        vn = (d * rstd * g_ref[...] + b_ref[...]).astype(BF16)
        for c in range(tb // LANES):
            rs = slice(c * LANES, (c + 1) * LANES)
            for g in range(G):
                cs = slice(g * LANES, (g + 1) * LANES)
                mixed = jnp.dot(w_ref[g], vn[rs, cs], preferred_element_type=F32) + bias_ref[:, cs]
                out_ref[rs, cs] = (zu[rs, cs] * mixed).astype(BF16)

    return pl.pallas_call(
        body, grid=(T // tb,),
        in_specs=[pl.BlockSpec((tb, W2), lambda i: (i, 0)), pl.BlockSpec((1, W), lambda i: (0, 0)),
                  pl.BlockSpec((1, W), lambda i: (0, 0)), pl.BlockSpec((G, LANES, LANES), lambda i: (0, 0, 0)),
                  pl.BlockSpec((LANES, W), lambda i: (0, 0))],
        out_specs=pl.BlockSpec((tb, W), lambda i: (i, 0)),
        out_shape=jax.ShapeDtypeStruct((T, W), BF16), name=name, compiler_params=_cp(),
    )(a, ln_g.reshape(1, W), ln_b.reshape(1, W), w_tril, bias)


def _sgu_bwd(a, dgated, ln_g, ln_b, w_tril, w_tril_t, bias, name):
    T, W2 = a.shape
    W = W2 // 2
    G = w_tril.shape[0]
    tb = _tile(T, 256, LANES)

    def body(a_ref, dg_ref, g_ref, b_ref, w_ref, wt_ref, bias_ref,
             da_ref, dws_ref, dbias_ref, dlng_ref, dlnb_ref, dvn_ref):
        @pl.when(pl.program_id(0) == 0)
        def _():
            dws_ref[...] = jnp.zeros_like(dws_ref)
            dbias_ref[...] = jnp.zeros_like(dbias_ref)
            dlng_ref[...] = jnp.zeros_like(dlng_ref)
            dlnb_ref[...] = jnp.zeros_like(dlnb_ref)

        up = a_ref[:, :W].astype(F32)
        vp = a_ref[:, W:].astype(F32)
        zu, gu = _gelu_and_grad(up)
        zv, gv = _gelu_and_grad(vp)
        mu = jnp.mean(zv, axis=-1, keepdims=True)
        d = zv - mu
        rstd = lax.rsqrt(jnp.mean(d * d, axis=-1, keepdims=True) + LN_EPS)
        vhat = d * rstd
        gam = g_ref[...]
        vn = (vhat * gam + b_ref[...]).astype(BF16)
        dgated = dg_ref[...]
        for c in range(tb // LANES):
            rs = slice(c * LANES, (c + 1) * LANES)
            for g in range(G):
                cs = slice(g * LANES, (g + 1) * LANES)
                vb = vn[rs, cs]
                mixed = jnp.dot(w_ref[g], vb, preferred_element_type=F32) + bias_ref[:, cs]
                dgt = dgated[rs, cs]
                da_ref[rs, cs] = (dgt * mixed * gu[rs, cs]).astype(BF16)
                dmx = dgt * zu[rs, cs]
                dbias_ref[:, cs] += dmx
                dmb = dmx.astype(BF16)
                dws_ref[g] += _nt(dmb, vb)
                dvn_ref[rs, cs] = jnp.dot(wt_ref[g], dmb, preferred_element_type=F32)
        dvn = dvn_ref[...]
        dlng_ref[...] += jnp.sum(dvn * vhat, axis=0, keepdims=True)
        dlnb_ref[...] += jnp.sum(dvn, axis=0, keepdims=True)
        dvh = dvn * gam
        dzv = rstd * (dvh - jnp.mean(dvh, axis=-1, keepdims=True)
                      - vhat * jnp.mean(dvh * vhat, axis=-1, keepdims=True))
        da_ref[:, W:] = (dzv * gv).astype(BF16)

    const2 = lambda shape: pl.BlockSpec(shape, lambda i: (0, 0))
    const3 = pl.BlockSpec((G, LANES, LANES), lambda i: (0, 0, 0))
    return pl.pallas_call(
        body, grid=(T // tb,),
        in_specs=[pl.BlockSpec((tb, W2), lambda i: (i, 0)), pl.BlockSpec((tb, W), lambda i: (i, 0)),
                  const2((1, W)), const2((1, W)), const3, const3, const2((LANES, W))],
        out_specs=[pl.BlockSpec((tb, W2), lambda i: (i, 0)), const3, const2((LANES, W)),
                   const2((SUBLANES, W)), const2((SUBLANES, W))],
        out_shape=[jax.ShapeDtypeStruct((T, W2), BF16), jax.ShapeDtypeStruct((G, LANES, LANES), F32),
                   jax.ShapeDtypeStruct((LANES, W), F32), jax.ShapeDtypeStruct((SUBLANES, W), F32),
                   jax.ShapeDtypeStruct((SUBLANES, W), F32)],
        scratch_shapes=[pltpu.VMEM((tb, W), F32)],
        name=name, compiler_params=_cp(),
    )(a, dgated, ln_g.reshape(1, W), ln_b.reshape(1, W), w_tril, w_tril_t, bias)


def _adam_math(w, g, m, v):
    m = ADAM_B1 * m + (1.0 - ADAM_B1) * g
    v = ADAM_B2 * v + (1.0 - ADAM_B2) * (g * g)
    m_hat = m / (1.0 - ADAM_B1 ** ADAM_STEP)
    v_hat = v / (1.0 - ADAM_B2 ** ADAM_STEP)
    delta = -ADAM_LR * (m_hat / (jnp.sqrt(v_hat) + ADAM_EPS) + ADAM_WD * w)
    return delta, m, v


def _adamw_halves(mine, theirs, c_idx, w, m, v, name):
    R, C = w.shape
    rh = R // 2
    tb = _row_tile(rh, C)
    nb = rh // tb

    def body(c_ref, a_ref, b_ref, w_ref, m_ref, v_ref, g_ref, d_ref, mo_ref, vo_ref):
        g = jnp.where(pl.program_id(0) == c_ref[0], a_ref[...], b_ref[...])
        d, mm, vv = _adam_math(w_ref[...], g, m_ref[...], v_ref[...])
        g_ref[...] = g
        d_ref[...] = d
        mo_ref[...] = mm
        vo_ref[...] = vv

    half = pl.BlockSpec((tb, C), lambda h, i, c: (i, 0))
    row = pl.BlockSpec((tb, C), lambda h, i, c: (h * nb + i, 0))
    sds = jax.ShapeDtypeStruct((R, C), F32)
    return pl.pallas_call(
        body,
        grid_spec=pltpu.PrefetchScalarGridSpec(
            num_scalar_prefetch=1, grid=(2, nb), in_specs=[half, half, row, row, row], out_specs=[row] * 4),
        out_shape=[sds] * 4, name=name, compiler_params=_cp())(c_idx, mine, theirs, w, m, v)


def _adamw_sum(parts, w, m, v, name):
    K, R, C = parts.shape
    tb = _tile(R, 128, SUBLANES)

    def body(p_ref, w_ref, m_ref, v_ref, g_ref, d_ref, mo_ref, vo_ref):
        g = p_ref[0]
        for k in range(1, K):
            g = g + p_ref[k]
        d, mm, vv = _adam_math(w_ref[...], g, m_ref[...], v_ref[...])
        g_ref[...] = g
        d_ref[...] = d
        mo_ref[...] = mm
        vo_ref[...] = vv

    row = pl.BlockSpec((tb, C), lambda i: (i, 0))
    sds = jax.ShapeDtypeStruct((R, C), F32)
    return pl.pallas_call(
        body, grid=(R // tb,),
        in_specs=[pl.BlockSpec((K, tb, C), lambda i: (0, i, 0)), row, row, row],
        out_specs=[row] * 4, out_shape=[sds] * 4, name=name, compiler_params=_cp())(parts, w, m, v)


def _pair_sum(g_all, recv, c_idx, name):
    K, R, C = g_all.shape
    rh = R // 2
    tb = _row_tile(rh, C)
    nb = rh // tb

    def body(c_ref, a_ref, b_ref, o_ref):
        o_ref[...] = (a_ref[...] + b_ref[...]).astype(BF16)

    return pl.pallas_call(
        body,
        grid_spec=pltpu.PrefetchScalarGridSpec(
            num_scalar_prefetch=1, grid=(K, nb),
            in_specs=[pl.BlockSpec((1, tb, C), lambda k, i, c: (k, c[0] * nb + i, 0)),
                      pl.BlockSpec((1, tb, C), lambda k, i, c: (k, i, 0))],
            out_specs=pl.BlockSpec((1, tb, C), lambda k, i, c: (k, i, 0))),
        out_shape=jax.ShapeDtypeStruct((K, rh, C), BF16), name=name, compiler_params=_cp(),
    )(c_idx, g_all, recv)


def _sum_parts(parts, name):
    K, R, C = parts.shape
    tb = _row_tile(R, C)

    def body(p_ref, o_ref):
        g = p_ref[0].astype(F32)
        for k in range(1, K):
            g = g + p_ref[k].astype(F32)
        o_ref[...] = g

    return pl.pallas_call(
        body, grid=(R // tb,), in_specs=[pl.BlockSpec((K, tb, C), lambda i: (0, i, 0))],
        out_specs=pl.BlockSpec((tb, C), lambda i: (i, 0)),
        out_shape=jax.ShapeDtypeStruct((R, C), F32), name=name, compiler_params=_cp())(parts)


_CHIP_RELATIONS = ((1, 0), (0, 1), (1, 1))


def _position():
    return lax.axis_index("x"), lax.axis_index("y"), lax.axis_index("c")


def _flip(v, bit):
    return 1 - v if bit else v


def _shard_half_copies(w_ref, ow_ref, send_sems, recv_sems, sem0, incoming):
    x, y, c = _position()
    rh = w_ref.shape[0] // 2
    rows = pl.ds(pl.multiple_of(c * rh, 16), rh)
    out = []
    for r, (dx, dy) in enumerate(_CHIP_RELATIONS):
        px, py = _flip(x, dx), _flip(y, dy)
        slot = 2 * px + py if incoming else 2 * x + y
        out.append(pltpu.make_async_remote_copy(
            src_ref=w_ref.at[rows, :], dst_ref=ow_ref.at[slot, rows, :], send_sem=send_sems.at[sem0 + r],
            recv_sem=recv_sems.at[sem0 + r], device_id=(px, py, c), device_id_type=MESH))
    return out


def _gather_weights(w_pack, side, name):
    n_side = 0 if side is None else 1

    def between_chips(*refs):
        if n_side:
            w_ref, s_ref, ow_ref, os_ref, local_sem, send_sems, recv_sems = refs
        else:
            w_ref, ow_ref, send_sems, recv_sems = refs
        x, y, c = _position()
        me = 2 * x + y
        if n_side:
            own_side = pltpu.make_async_copy(s_ref, os_ref.at[me], local_sem)
            own_side.start()

        def side_copies(incoming):
            out = []
            for r, (dx, dy) in enumerate(_CHIP_RELATIONS):
                px, py = _flip(x, dx), _flip(y, dy)
                out.append(pltpu.make_async_remote_copy(
                    src_ref=s_ref, dst_ref=os_ref.at[2 * px + py if incoming else me],
                    send_sem=send_sems.at[3 + r], recv_sem=recv_sems.at[3 + r],
                    device_id=(px, py, c), device_id_type=MESH))
            return out

        sent = _shard_half_copies(w_ref, ow_ref, send_sems, recv_sems, 0, False) + (side_copies(False) if n_side else [])
        for cp in sent:
            cp.start()
        for cp in _shard_half_copies(w_ref, ow_ref, send_sems, recv_sems, 0, True) + (side_copies(True) if n_side else []):
            cp.wait_recv()
        for cp in sent:
            cp.wait_send()
        if n_side:
            own_side.wait()

    sems = [pltpu.SemaphoreType.DMA((6,)), pltpu.SemaphoreType.DMA((6,))]
    gathered = jax.ShapeDtypeStruct((4,) + w_pack.shape, w_pack.dtype)
    if n_side:
        halves, sides = pl.pallas_call(
            between_chips, in_specs=[_hbm(), _hbm()], out_specs=[_hbm(), _hbm()],
            out_shape=[gathered, jax.ShapeDtypeStruct((4,) + side.shape, side.dtype)],
            scratch_shapes=[pltpu.SemaphoreType.DMA(())] + sems,
            name=name + "_ici", compiler_params=_cp(),
        )(w_pack, side)
    else:
        sides = None
        halves = pl.pallas_call(
            between_chips, in_specs=[_hbm()], out_specs=_hbm(), out_shape=gathered, scratch_shapes=sems,
            name=name + "_ici", compiler_params=_cp(),
        )(w_pack)
    return _hand_to_sibling(halves, name + "_d2d"), sides


def _hand_to_sibling(halves, name):
    _, R, C = halves.shape
    rh = R // 2

    def to_sibling(g_ref, o_ref, send_sems, recv_sems):
        x, y, c = _position()

        def copy(r, cc):
            dx, dy = _CHIP_RELATIONS[r]
            slot = 2 * _flip(x, dx) + _flip(y, dy)
            rows = pl.ds(pl.multiple_of(cc * rh, 16), rh)
            return pltpu.make_async_remote_copy(
                src_ref=g_ref.at[slot, rows, :], dst_ref=o_ref.at[slot, rows, :], send_sem=send_sems.at[r],
                recv_sem=recv_sems.at[r], device_id=(x, y, 1 - c), device_id_type=MESH)

        sent = [copy(r, c) for r in range(3)]
        for cp in sent:
            cp.start()
        for r in range(3):
            copy(r, 1 - c).wait_recv()
        for cp in sent:
            cp.wait_send()

    return pl.pallas_call(
        to_sibling, in_specs=[_hbm()], out_specs=_hbm(), input_output_aliases={0: 0},
        out_shape=jax.ShapeDtypeStruct(halves.shape, halves.dtype),
        scratch_shapes=[pltpu.SemaphoreType.DMA((3,)), pltpu.SemaphoreType.DMA((3,))],
        name=name, compiler_params=_cp(),
    )(halves)


def _sibling_halves(g_all, name):
    K, R, C = g_all.shape
    rh = R // 2

    def body(g_ref, o_ref, send_sem, recv_sem):
        x, y, c = _position()
        start = pl.multiple_of((1 - c) * rh, SUBLANES)
        cp = pltpu.make_async_remote_copy(
            src_ref=g_ref.at[:, pl.ds(start, rh), :], dst_ref=o_ref, send_sem=send_sem, recv_sem=recv_sem,
            device_id=(x, y, 1 - c), device_id_type=MESH)
        cp.start()
        cp.wait_recv()
        cp.wait_send()

    return pl.pallas_call(
        body, in_specs=[_hbm()], out_specs=_hbm(),
        out_shape=jax.ShapeDtypeStruct((K, rh, C), F32),
        scratch_shapes=[pltpu.SemaphoreType.DMA(()), pltpu.SemaphoreType.DMA(())],
        name=name, compiler_params=_cp(),
    )(g_all)


def _chip_exchange(parts, name):
    K, R, C = parts.shape

    def body(p_ref, o_ref, local_sem, send_sems, recv_sems):
        x, y, c = _position()
        me = 2 * x + y
        own = pltpu.make_async_copy(p_ref.at[me], o_ref.at[me], local_sem)
        own.start()

        def copy(r, src_slot, dst_slot):
            dx, dy = _CHIP_RELATIONS[r]
            return pltpu.make_async_remote_copy(
                src_ref=p_ref.at[src_slot], dst_ref=o_ref.at[dst_slot], send_sem=send_sems.at[r],
                recv_sem=recv_sems.at[r], device_id=(_flip(x, dx), _flip(y, dy), c), device_id_type=MESH)

        def chip(r):
            dx, dy = _CHIP_RELATIONS[r]
            return 2 * _flip(x, dx) + _flip(y, dy)

        sent = [copy(r, chip(r), me) for r in range(3)]
        for cp in sent:
            cp.start()
        for r in range(3):
            copy(r, me, chip(r)).wait_recv()
        for cp in sent:
            cp.wait_send()
        own.wait()

    return pl.pallas_call(
        body, in_specs=[_hbm()], out_specs=_hbm(),
        out_shape=jax.ShapeDtypeStruct((K, R, C), parts.dtype),
        scratch_shapes=[pltpu.SemaphoreType.DMA(()), pltpu.SemaphoreType.DMA((3,)),
                        pltpu.SemaphoreType.DMA((3,))],
        name=name, compiler_params=_cp(),
    )(parts)


def _swap_with_sibling(half, name):
    rh, C = half.shape

    def body(h_ref, o_ref, send_sem, recv_sem):
        x, y, c = _position()
        cp = pltpu.make_async_remote_copy(
            src_ref=h_ref, dst_ref=o_ref, send_sem=send_sem, recv_sem=recv_sem,
            device_id=(x, y, 1 - c), device_id_type=MESH)
        cp.start()
        cp.wait_recv()
        cp.wait_send()

    return pl.pallas_call(
        body, in_specs=[_hbm()], out_specs=_hbm(),
        out_shape=jax.ShapeDtypeStruct((rh, C), F32),
        scratch_shapes=[pltpu.SemaphoreType.DMA(()), pltpu.SemaphoreType.DMA(())],
        name=name, compiler_params=_cp(),
    )(half)


def _gather_all(part, name):
    R, C = part.shape
    masks = [(b >> 2 & 1, b >> 1 & 1, b & 1) for b in range(1, 8)]

    def body(p_ref, o_ref, local_sem, send_sems, recv_sems):
        x, y, c = _position()
        me = 4 * x + 2 * y + c
        own = pltpu.make_async_copy(p_ref, o_ref.at[me], local_sem)
        own.start()

        def copy(r, slot):
            dx, dy, dc = masks[r]
            return pltpu.make_async_remote_copy(
                src_ref=p_ref, dst_ref=o_ref.at[slot], send_sem=send_sems.at[r], recv_sem=recv_sems.at[r],
                device_id=(_flip(x, dx), _flip(y, dy), _flip(c, dc)), device_id_type=MESH)

        sent = [copy(r, me) for r in range(7)]
        for cp in sent:
            cp.start()
        for r in range(7):
            dx, dy, dc = masks[r]
            copy(r, 4 * _flip(x, dx) + 2 * _flip(y, dy) + _flip(c, dc)).wait_recv()
        for cp in sent:
            cp.wait_send()
        own.wait()

    return pl.pallas_call(
        body, in_specs=[_hbm()], out_specs=_hbm(),
        out_shape=jax.ShapeDtypeStruct((8, R, C), F32),
        scratch_shapes=[pltpu.SemaphoreType.DMA(()), pltpu.SemaphoreType.DMA((7,)),
                        pltpu.SemaphoreType.DMA((7,))],
        name=name, compiler_params=_cp(),
    )(part)


def _pack(arrs, row_mult, cols=PACK_COLS, lead=0):
    head = arrs[0].shape[:lead]
    pieces = []
    for a in arrs:
        flat = a.astype(F32).reshape(head + (-1,))
        fill = -flat.shape[-1] % cols
        if fill:
            flat = jnp.concatenate([flat, jnp.zeros(head + (fill,), F32)], axis=-1)
        pieces.append(flat.reshape(head + (-1, cols)))
    rows = sum(p.shape[lead] for p in pieces)
    fill = -rows % row_mult
    if fill:
        pieces.append(jnp.zeros(head + (fill, cols), F32))
    return jnp.concatenate(pieces, axis=lead) if len(pieces) > 1 else pieces[0]


def _unpack(buf, shapes):
    lead = buf.shape[:-2]
    cols = buf.shape[-1]
    out, off = [], 0
    for shp in shapes:
        n = math.prod(shp)
        rows = -(-n // cols)
        piece = buf[..., off:off + rows, :]
        if rows * cols != n:
            piece = piece.reshape(lead + (-1,))[..., :n]
        out.append(piece.reshape(lead + tuple(shp)))
        off += rows
    return out


def _cols_from_chips(g):
    k, L, A, n = g.shape
    return jnp.transpose(g, (1, 2, 0, 3)).reshape(L, A, k * n)


def _rows_from_chips(g):
    k, L, n, B = g.shape
    return jnp.transpose(g, (1, 0, 2, 3)).reshape(L, k * n, B)


def _cols_to_chips(full, k=4):
    L, A, N = full.shape
    return jnp.transpose(full.reshape(L, A, k, N // k), (2, 0, 1, 3))


def _rows_to_chips(full, k=4):
    L, N, B = full.shape
    return jnp.transpose(full.reshape(L, k, N // k, B), (1, 0, 2, 3))


def kernel(x, mixer_norm_w, attn_w_in, attn_b_f, attn_w_out, sgu_w_in, sgu_ln_g, sgu_ln_b, sgu_w_s, sgu_b_s, sgu_w_out, ffn_norm_w, ffn_w_in, ffn_w_out, final_norm_w, loss_target, m_mixer_norm_w, m_attn_w_in, m_attn_b_f, m_attn_w_out, m_sgu_w_in, m_sgu_ln_g, m_sgu_ln_b, m_sgu_w_s, m_sgu_b_s, m_sgu_w_out, m_ffn_norm_w, m_ffn_w_in, m_ffn_w_out, m_final_norm_w, v_mixer_norm_w, v_attn_w_in, v_attn_b_f, v_attn_w_out, v_sgu_w_in, v_sgu_ln_g, v_sgu_ln_b, v_sgu_w_s, v_sgu_b_s, v_sgu_w_out, v_ffn_norm_w, v_ffn_w_in, v_ffn_w_out, v_final_norm_w):
    T, D = x.shape[1], x.shape[2]
    depth = mixer_norm_w.shape[0]
    H = attn_b_f.shape[1]
    P = D // LANES
    assert D % LANES == 0 and D // H == 64 and 2 * P == H and 2 * P <= LANES
    G = sgu_w_s.shape[1]
    W = sgu_w_out.shape[1] * 4
    assert sgu_w_s.shape[2] == LANES and W == G * LANES
    scale = float(D // H) ** -0.5
    f_pad = LANES
    c_idx = lax.axis_index("c").astype(jnp.int32).reshape(1)

    groups = [
        ([attn_w_out, sgu_w_in, sgu_w_out, ffn_w_out, sgu_ln_g, sgu_ln_b],
         [m_attn_w_out, m_sgu_w_in, m_sgu_w_out, m_ffn_w_out, m_sgu_ln_g, m_sgu_ln_b],
         [v_attn_w_out, v_sgu_w_in, v_sgu_w_out, v_ffn_w_out, v_sgu_ln_g, v_sgu_ln_b]),
        ([ffn_w_in], [m_ffn_w_in], [v_ffn_w_in]),
        ([attn_w_in], [m_attn_w_in], [v_attn_w_in]),
    ]
    group_cols = [D, ffn_w_in.shape[2], attn_w_in.shape[2]]
    group_shapes = [[a.shape for a in g[0]] for g in groups]
    w_packs = [_pack(g[0], 512, cols) for g, cols in zip(groups, group_cols)]
    ln_pack = _pack([sgu_ln_g, sgu_ln_b], SUBLANES)
    my_chip = 2 * lax.axis_index("x") + lax.axis_index("y")
    w_packs_b = [w.astype(BF16) for w in w_packs]

    def finish_gather(t, gat):
        return _unpack(lax.dynamic_update_index_in_dim(gat, w_packs_b[t], my_chip, 0), group_shapes[t])

    gat, gat_ln = _gather_weights(w_packs_b[2], ln_pack, "gather_weights_2")
    (g_ai,) = finish_gather(2, gat)
    g_lng, g_lnb = _unpack(gat_ln, [sgu_ln_g.shape, sgu_ln_b.shape])
    w_ai = _cols_from_chips(g_ai)
    w_ai = jnp.pad(w_ai, ((0, 0), (0, 0), (0, 3 * D + f_pad - w_ai.shape[2])))
    b_f_pad = jnp.pad(attn_b_f, ((0, 0), (0, LANES - H)))
    xs = x.reshape(T, D)
    h, qkv, f = _norm_mm(xs, mixer_norm_w[0], w_ai[0], ((3 * D, BF16), (f_pad, F32)), "attn_qkv_0")
    cT, c_cols, c0T = _gate_fwd(f, b_f_pad[0:1], P, "gate_fwd_0")
    o, lseT, halves_0, halves_1 = _attn_fwd(qkv, cT, P, scale, "attn_fwd_0", riders=(w_packs_b[0], w_packs_b[1]))
    first_attention = dict(h=h, qkv=qkv, f=f, c0T=c0T, c_cols=c_cols, o=o, lseT=lseT)
    g_ao, g_si, g_so, g_fo, _, _ = finish_gather(0, _hand_to_sibling(halves_0, "gather_weights_0_d2d"))
    (g_fi,) = finish_gather(1, _hand_to_sibling(halves_1, "gather_weights_1_d2d"))
    w_ao = _rows_from_chips(g_ao)
    w_si = _cols_from_chips(g_si)
    w_so = _rows_from_chips(g_so)
    w_fo = _rows_from_chips(g_fo)
    w_fi5 = g_fi.reshape((2, 2) + g_fi.shape[1:])
    ln_g = jnp.transpose(g_lng, (1, 0, 2)).reshape(sgu_ln_g.shape[0], W)
    ln_b = jnp.transpose(g_lnb, (1, 0, 2)).reshape(sgu_ln_b.shape[0], W)
    w_tril = jnp.tril(sgu_w_s)
    w_tril_b = w_tril.astype(BF16)
    w_tril_tb = jnp.swapaxes(w_tril, 2, 3).astype(BF16)
    sgu_bias = jnp.repeat(jnp.swapaxes(sgu_b_s, 1, 2), LANES, axis=2)

    saved = []
    for i in range(depth):
        j = i // 2
        rec = {"x_in": xs}
        if i == 0:
            rec.update(first_attention)
            h, o = rec["h"], rec["o"]
            x_mid = _mm(o, w_ao[j], "nn", F32, f"attn_out_{i}", res=xs)
        elif i % 2 == 0:
            h, qkv, f = _norm_mm(xs, mixer_norm_w[i], w_ai[j], ((3 * D, BF16), (f_pad, F32)), f"attn_qkv_{i}")
            cT, c_cols, c0T = _gate_fwd(f, b_f_pad[j:j + 1], P, f"gate_fwd_{i}")
            o, lseT = _attn_fwd(qkv, cT, P, scale, f"attn_fwd_{i}")
            x_mid = _mm(o, w_ao[j], "nn", F32, f"attn_out_{i}", res=xs)
            rec.update(qkv=qkv, f=f, c0T=c0T, c_cols=c_cols, o=o, lseT=lseT)
        else:
            h, a = _norm_mm(xs, mixer_norm_w[i], w_si[j], ((2 * W, BF16),), f"sgu_in_{i}")
            gated = _sgu_fwd(a, ln_g[j], ln_b[j], w_tril_b[j], sgu_bias[j], f"sgu_fwd_{i}")
            x_mid = _mm(gated, w_so[j], "nn", F32, f"sgu_out_{i}", res=xs)
            rec.update(a=a, gated=gated)
        h2, fa, s = _ffn_in_act(x_mid, ffn_norm_w[i], w_fi5, i, f"ffn_in_{i}")
        xs = _mm(s, w_fo[i], "nn", F32, f"ffn_out_{i}", res=x_mid)
        rec.update(h=h, x_mid=x_mid, h2=h2, fa=fa, s=s)
        saved.append(rec)

    gx, loss_acc, dw_final = _loss_head(xs, final_norm_w, loss_target.reshape(T, D), "loss_head")
    loss = lax.psum(loss_acc[0, 0], ("x", "y", "c"))

    n_attn, n_sgu = attn_w_in.shape[0], sgu_w_in.shape[0]
    d_mixer_norm, d_ffn_norm = [None] * depth, [None] * depth
    d_ai, d_ao, d_bf = [None] * n_attn, [None] * n_attn, [None] * n_attn
    d_si, d_so, d_lng, d_lnb, d_ws, d_bs = ([None] * n_sgu for _ in range(6))
    d_fi, d_fo = [None] * depth, [None] * depth
    for i in reversed(range(depth)):
        j = i // 2
        rec = saved[i]
        d_fo[i] = _mm(rec["s"], gx, "tn", F32, f"ffn_out_wgrad_{i}")
        da = _ffn_out_bwd_act(gx, w_fo[i], rec["fa"], f"ffn_out_bwd_{i}")
        da = da.reshape((4,) + da.shape[2:])
        d_fi[i] = _mm_tn_shards(rec["h2"], da, f"ffn_in_wgrad_{i}")
        gx, dwn = _nt_norm_bwd(da, g_fi, i, rec["x_mid"], ffn_norm_w[i], gx, f"ffn_in_bwd_{i}")
        d_ffn_norm[i] = dwn[0]
        if i % 2 == 0:
            do = _mm(gx, w_ao[j], "nt", BF16, f"attn_out_bwd_{i}")
            d_ao[j] = _mm(rec["o"], gx, "tn", F32, f"attn_out_wgrad_{i}")
            dT = _attn_delta(do, rec["o"], P, f"attn_delta_{i}")
            dq, dk, dv, dc_cols, drowT = _attn_bwd(rec["qkv"], do, rec["lseT"], dT, rec["c0T"], rec["c_cols"],
                                                   P, scale, f"attn_bwd_{i}")
            df, dbf = _gate_bwd(dc_cols, drowT, rec["f"], b_f_pad[j:j + 1], P, f"gate_bwd_{i}")
            d_bf[j] = dbf[0, :H]
            dproj = jnp.concatenate([dq, dk, dv, df.astype(BF16)], axis=1)
            d_ai[j] = _mm(rec["h"], dproj, "tn", F32, f"attn_in_wgrad_{i}")[:, :3 * D + H]
            dmix, w_mix = dproj, w_ai
        else:
            dgated = _mm(gx, w_so[j], "nt", F32, f"sgu_out_bwd_{i}")
            d_so[j] = _mm(rec["gated"], gx, "tn", F32, f"sgu_out_wgrad_{i}")
            da_s, dws, dbias, dlng, dlnb = _sgu_bwd(rec["a"], dgated, ln_g[j], ln_b[j], w_tril_b[j],
                                                    w_tril_tb[j], sgu_bias[j], f"sgu_bwd_{i}")
            d_ws[j] = jnp.tril(dws)
            d_bs[j] = jnp.sum(dbias.reshape(LANES, G, LANES), axis=2).T
            d_lng[j], d_lnb[j] = dlng[0], dlnb[0]
            d_si[j] = _mm(rec["h"], da_s, "tn", F32, f"sgu_in_wgrad_{i}")
            dmix, w_mix = da_s, w_si
        gx, dwn = _nt_norm_bwd(dmix[None], w_mix[None], j, rec["x_in"], mixer_norm_w[i], gx, f"mixer_in_bwd_{i}")
        d_mixer_norm[i] = dwn[0]
    grad_x = gx.reshape(x.shape)

    group_grads = [
        [_rows_to_chips(jnp.stack(d_ao)), _cols_to_chips(jnp.stack(d_si)), _rows_to_chips(jnp.stack(d_so)),
         _rows_to_chips(jnp.stack(d_fo)),
         jnp.transpose(jnp.stack(d_lng).reshape(n_sgu, 4, W // 4), (1, 0, 2)),
         jnp.transpose(jnp.stack(d_lnb).reshape(n_sgu, 4, W // 4), (1, 0, 2))],
        [jnp.stack(d_fi, axis=1)],
        [_cols_to_chips(jnp.stack(d_ai))],
    ]
    reduced = []
    for t, (grads, cols) in enumerate(zip(group_grads, group_cols)):
        g_all = _pack(grads, 512, cols, lead=1)
        from_sibling = _sibling_halves(g_all, f"grad_sibling_halves_{t}")
        pair = _pair_sum(g_all, from_sibling, c_idx, f"grad_pair_sum_{t}")
        from_chips = _chip_exchange(pair, f"grad_chip_exchange_{t}")
        my_half = _sum_parts(from_chips, f"grad_chip_sum_{t}")
        sibling_half = _swap_with_sibling(my_half, f"grad_swap_halves_{t}")
        packs = _adamw_halves(my_half, sibling_half, c_idx, w_packs[t], _pack(groups[t][1], 512, cols),
                              _pack(groups[t][2], 512, cols), f"adamw_sharded_{t}")
        reduced.append([_unpack(p, group_shapes[t]) for p in packs])

    def sharded_outputs(which):
        (ao, si, so, fo, lng, lnb), (fi,), (ai,) = (reduced[t][which] for t in range(3))
        return [ai, ao, si, so, fi, fo, lng, lnb]

    g_sh, d_sh, m_sh, v_sh = (sharded_outputs(w) for w in range(4))

    repl = [mixer_norm_w, attn_b_f, sgu_w_s, sgu_b_s, ffn_norm_w, final_norm_w]
    repl_m = [m_mixer_norm_w, m_attn_b_f, m_sgu_w_s, m_sgu_b_s, m_ffn_norm_w, m_final_norm_w]
    repl_v = [v_mixer_norm_w, v_attn_b_f, v_sgu_w_s, v_sgu_b_s, v_ffn_norm_w, v_final_norm_w]
    repl_shapes = [a.shape for a in repl]
    repl_grads = [jnp.stack(d_mixer_norm), jnp.stack(d_bf), jnp.stack(d_ws), jnp.stack(d_bs),
                  jnp.stack(d_ffn_norm), dw_final[0]]
    parts = _gather_all(_pack(repl_grads, SUBLANES), "grad_gather_replicated")
    g_rep, d_rep, m_rep, v_rep = _adamw_sum(parts, _pack(repl, SUBLANES), _pack(repl_m, SUBLANES),
                                            _pack(repl_v, SUBLANES), "adamw_replicated")
    g_r = _unpack(g_rep, repl_shapes)
    d_r = _unpack(d_rep, repl_shapes)
    m_r = _unpack(m_rep, repl_shapes)
    v_r = _unpack(v_rep, repl_shapes)

    def ordered(sh, rp):
        ai, ao, si, so, fi, fo, lng, lnb = sh
        mn, bf, ws, bs, fn, fin = rp
        return [mn, ai, bf, ao, si, lng, lnb, ws, bs, so, fn, fi, fo, fin]

    return (loss, grad_x, *ordered(g_sh, g_r), *ordered(d_sh, d_r), *ordered(m_sh, m_r), *ordered(v_sh, v_r))
```

```python
import math

import jax
import jax.numpy as jnp
from jax import lax
from jax.experimental import pallas as pl
from jax.experimental.pallas import tpu as pltpu

F32 = jnp.float32
BF16 = jnp.bfloat16
NORM_EPS = 1e-6
LN_EPS = 1e-5
ADAM_LR = 0.001
ADAM_B1 = 0.9
ADAM_B2 = 0.999
ADAM_EPS = 1e-08
ADAM_WD = 0.01
ADAM_STEP = 10

LANES = 128
SUBLANES = 8
PACK_COLS = 1024
VMEM_LIMIT = 56 * 1024 * 1024
NEG_BIG = -1e30
LOG2E = 1.4426950408889634
MESH = pl.DeviceIdType.MESH


def _cp():
    return pltpu.CompilerParams(vmem_limit_bytes=VMEM_LIMIT)


def _tile(n, cap, mult):
    best = None
    d = mult
    while d <= min(n, cap):
        if n % d == 0:
            best = d
        d += mult
    return n if best is None else best


def _row_tile(rows, cols):
    cap = max(16, (512 * 1024 // cols) // 16 * 16)
    return _tile(rows, cap, 16)


def _hbm():
    return pl.BlockSpec(memory_space=pltpu.HBM)


def _nt_norm_bwd(a3, b4, layer, x, w, dres, name):
    S, T, Ks = a3.shape
    D = x.shape[1]
    tm = _tile(T, 256, 16)

    def body(a_ref, b_ref, x_ref, w_ref, dres_ref, dx_ref, dw_ref):
        @pl.when(pl.program_id(0) == 0)
        def _():
            dw_ref[...] = jnp.zeros_like(dw_ref)

        dh = _nt(a_ref[0].astype(BF16), b_ref[0, 0])
        for s in range(1, S):
            dh = dh + _nt(a_ref[s].astype(BF16), b_ref[s, 0])
        xf = x_ref[...]
        r = lax.rsqrt(jnp.mean(xf * xf, axis=-1, keepdims=True) + NORM_EPS)
        xhat = xf * r
        dxhat = dh * w_ref[...]
        dx_ref[...] = dres_ref[...] + r * (dxhat - xhat * jnp.mean(dxhat * xhat, axis=-1, keepdims=True))
        dw_ref[...] += jnp.sum(dh * xhat, axis=0, keepdims=True)

    row = pl.BlockSpec((tm, D), lambda i: (i, 0))
    return pl.pallas_call(
        body, grid=(T // tm,),
        in_specs=[pl.BlockSpec((S, tm, Ks), lambda i: (0, i, 0)),
                  pl.BlockSpec((S, 1, D, Ks), lambda i: (0, layer, 0, 0)),
                  row, pl.BlockSpec((1, D), lambda i: (0, 0)), row],
        out_specs=[row, pl.BlockSpec((SUBLANES, D), lambda i: (0, 0))],
        out_shape=[jax.ShapeDtypeStruct((T, D), F32), jax.ShapeDtypeStruct((SUBLANES, D), F32)],
        name=name, compiler_params=_cp(),
    )(a3, b4, x, w.reshape(1, D), dres)


def _mm(a, b, mode, out_dtype, name, res=None):
    if mode == "tn":
        kt, M = a.shape
        N = b.shape[1]
        tm = _tile(M, 1408, LANES)
        tn = _tile(N, 1408, LANES)
        tk = _tile(kt, 1024, 16)

        def body(a_ref, b_ref, o_ref):
            @pl.when(pl.program_id(2) == 0)
            def _():
                o_ref[...] = jnp.zeros_like(o_ref)

            o_ref[...] += lax.dot_general(
                a_ref[...].astype(BF16), b_ref[...].astype(BF16), (((0,), (0,)), ((), ())),
                preferred_element_type=F32)

        return pl.pallas_call(
            body, grid=(M // tm, N // tn, kt // tk),
            in_specs=[pl.BlockSpec((tk, tm), lambda i, j, k: (k, i)),
                      pl.BlockSpec((tk, tn), lambda i, j, k: (k, j))],
            out_specs=pl.BlockSpec((tm, tn), lambda i, j, k: (i, j)),
            out_shape=jax.ShapeDtypeStruct((M, N), F32), name=name, compiler_params=_cp(),
        )(a, b)

    M, K = a.shape
    N = b.shape[1] if mode == "nn" else b.shape[0]
    tm = _tile(M, 512, 16)
    cap = min(3072, (6 << 20) // (2 * K), (4 << 20) // (tm * jnp.dtype(out_dtype).itemsize))
    tn = _tile(N, max(LANES, cap // LANES * LANES), LANES)
    dims = (((1,), (0,)), ((), ())) if mode == "nn" else (((1,), (1,)), ((), ()))

    def body(*refs):
        if res is None:
            a_ref, b_ref, o_ref = refs
        else:
            a_ref, b_ref, r_ref, o_ref = refs
        acc = lax.dot_general(a_ref[...].astype(BF16), b_ref[...].astype(BF16), dims,
                              preferred_element_type=F32)
        if res is not None:
            acc = acc + r_ref[...]
        o_ref[...] = acc.astype(out_dtype)

    b_spec = (pl.BlockSpec((K, tn), lambda j, i: (0, j)) if mode == "nn"
              else pl.BlockSpec((tn, K), lambda j, i: (j, 0)))
    in_specs = [pl.BlockSpec((tm, K), lambda j, i: (i, 0)), b_spec]
    args = [a, b]
    if res is not None:
        in_specs.append(pl.BlockSpec((tm, tn), lambda j, i: (i, j)))
        args.append(res)
    return pl.pallas_call(
        body, grid=(N // tn, M // tm), in_specs=in_specs,
        out_specs=pl.BlockSpec((tm, tn), lambda j, i: (i, j)),
        out_shape=jax.ShapeDtypeStruct((M, N), out_dtype), name=name, compiler_params=_cp(),
    )(*args)


def _normed(x_ref, w_ref):
    xf = x_ref[...]
    r = lax.rsqrt(jnp.mean(xf * xf, axis=-1, keepdims=True) + NORM_EPS)
    return (xf * r * w_ref[...]).astype(BF16)


def _ffn_in_act(x, norm_w, w5, layer, name):
    T, D = x.shape
    n = w5.shape[-1]
    tm = _tile(T, 256, 16)

    def body(x_ref, nw_ref, w_ref, h_ref, a_ref, s_ref):
        hv = _normed(x_ref, nw_ref)
        h_ref[...] = hv
        for half in range(2):
            g = jnp.dot(hv, w_ref[0, half, 0], preferred_element_type=F32)
            u = jnp.dot(hv, w_ref[1, half, 0], preferred_element_type=F32)
            a_ref[0, half] = g.astype(BF16)
            a_ref[1, half] = u.astype(BF16)
            s_ref[:, half * n:(half + 1) * n] = (g * jax.nn.sigmoid(g) * u).astype(BF16)

    return pl.pallas_call(
        body, grid=(T // tm,),
        in_specs=[pl.BlockSpec((tm, D), lambda i: (i, 0)), pl.BlockSpec((1, D), lambda i: (0, 0)),
                  pl.BlockSpec((2, 2, 1, D, n), lambda i: (0, 0, layer, 0, 0))],
        out_specs=[pl.BlockSpec((tm, D), lambda i: (i, 0)),
                   pl.BlockSpec((2, 2, tm, n), lambda i: (0, 0, i, 0)), pl.BlockSpec((tm, 2 * n), lambda i: (i, 0))],
        out_shape=[jax.ShapeDtypeStruct((T, D), BF16), jax.ShapeDtypeStruct((2, 2, T, n), BF16),
                   jax.ShapeDtypeStruct((T, 2 * n), BF16)],
        name=name, compiler_params=_cp(),
    )(x, norm_w.reshape(1, D), w5)


def _norm_mm(x, norm_w, b, splits, name):
    T, D = x.shape
    N = b.shape[1]
    assert sum(wd for wd, _ in splits) == N
    tm = _tile(T, 256, 16)

    def body(x_ref, nw_ref, b_ref, h_ref, *outs):
        hv = _normed(x_ref, nw_ref)
        h_ref[...] = hv
        off = 0
        for o_ref, (wd, dt) in zip(outs, splits):
            o_ref[...] = jnp.dot(hv, b_ref[:, off:off + wd], preferred_element_type=F32).astype(dt)
            off += wd

    return pl.pallas_call(
        body, grid=(T // tm,),
        in_specs=[pl.BlockSpec((tm, D), lambda i: (i, 0)), pl.BlockSpec((1, D), lambda i: (0, 0)),
                  pl.BlockSpec((D, N), lambda i: (0, 0))],
        out_specs=[pl.BlockSpec((tm, D), lambda i: (i, 0))] + [pl.BlockSpec((tm, wd), lambda i: (i, 0)) for wd, _ in splits],
        out_shape=[jax.ShapeDtypeStruct((T, D), BF16)] + [jax.ShapeDtypeStruct((T, wd), dt) for wd, dt in splits],
        name=name, compiler_params=_cp(),
    )(x, norm_w.reshape(1, D), b)


def _ffn_out_bwd_act(gx, w_out, a4, name):
    T, D = gx.shape
    n = a4.shape[-1]
    tm = _tile(T, 256, 16)

    step = 3 * LANES if n % LANES == 0 and n > 3 * LANES else n
    pieces = [(c, min(step, n - c)) for c in range(0, n, step)]

    def body(gx_ref, w_ref, a_ref, da_ref):
        gxb = gx_ref[...].astype(BF16)
        for c, wd in pieces:
            ds = _nt(gxb, w_ref[c:c + wd, :])
            g = a_ref[0, 0, :, c:c + wd].astype(F32)
            u = a_ref[1, 0, :, c:c + wd].astype(F32)
            sg = jax.nn.sigmoid(g)
            da_ref[0, 0, :, c:c + wd] = (ds * u * (sg * (1.0 + g * (1.0 - sg)))).astype(BF16)
            da_ref[1, 0, :, c:c + wd] = (ds * (g * sg)).astype(BF16)

    blk = pl.BlockSpec((2, 1, tm, n), lambda j, i: (0, j, i, 0))
    return pl.pallas_call(
        body, grid=(2, T // tm),
        in_specs=[pl.BlockSpec((tm, D), lambda j, i: (i, 0)), pl.BlockSpec((n, D), lambda j, i: (j, 0)), blk],
        out_specs=blk,
        out_shape=jax.ShapeDtypeStruct((2, 2, T, n), BF16), name=name, compiler_params=_cp(),
    )(gx, w_out, a4)


def _mm_tn_shards(h, a4, name):
    K, T, n = a4.shape
    D = h.shape[1]
    tk = _tile(T, 1024, 16)

    def body(h_ref, a_ref, o_ref):
        @pl.when(pl.program_id(1) == 0)
        def _():
            o_ref[...] = jnp.zeros_like(o_ref)

        o_ref[0] += lax.dot_general(h_ref[...], a_ref[0], (((0,), (0,)), ((), ())), preferred_element_type=F32)

    return pl.pallas_call(
        body, grid=(K, T // tk),
        in_specs=[pl.BlockSpec((tk, D), lambda k, t: (t, 0)), pl.BlockSpec((1, tk, n), lambda k, t: (k, t, 0))],
        out_specs=pl.BlockSpec((1, D, n), lambda k, t: (k, 0, 0)),
        out_shape=jax.ShapeDtypeStruct((K, D, n), F32), name=name, compiler_params=_cp(),
    )(h, a4)


def _loss_head(x, w, tgt, name):
    T, D = x.shape
    tm = _tile(T, 512, SUBLANES)

    def body(x_ref, w_ref, t_ref, dx_ref, loss_ref, dw_ref):
        @pl.when(pl.program_id(0) == 0)
        def _():
            loss_ref[...] = jnp.zeros_like(loss_ref)
            dw_ref[...] = jnp.zeros_like(dw_ref)

        xf = x_ref[...]
        wv = w_ref[...]
        r = lax.rsqrt(jnp.mean(xf * xf, axis=-1, keepdims=True) + NORM_EPS)
        xhat = xf * r
        err = xhat * wv - t_ref[...]
        per_tok = jnp.mean(err * err, axis=-1, keepdims=True)
        loss_ref[...] += 0.5 * jnp.sum(per_tok, axis=0, keepdims=True)
        dy = err * (1.0 / D)
        dxhat = dy * wv
        dx_ref[...] = r * (dxhat - xhat * jnp.mean(dxhat * xhat, axis=-1, keepdims=True))
        dw_ref[...] += jnp.sum(dy * xhat, axis=0, keepdims=True)

    row = pl.BlockSpec((tm, D), lambda i: (i, 0))
    return pl.pallas_call(
        body, grid=(T // tm,),
        in_specs=[row, pl.BlockSpec((1, D), lambda i: (0, 0)), row],
        out_specs=[row, pl.BlockSpec((SUBLANES, LANES), lambda i: (0, 0)),
                   pl.BlockSpec((SUBLANES, D), lambda i: (0, 0))],
        out_shape=[jax.ShapeDtypeStruct((T, D), F32), jax.ShapeDtypeStruct((SUBLANES, LANES), F32),
                   jax.ShapeDtypeStruct((SUBLANES, D), F32)],
        name=name, compiler_params=_cp(),
    )(x, w.reshape(1, D), tgt)


def _split3(v):
    hi = v.astype(BF16)
    r1 = v - hi.astype(F32)
    mid = r1.astype(BF16)
    lo = (r1 - mid.astype(F32)).astype(BF16)
    return hi, mid, lo


def _tri_dot(tri, v):
    out = None
    for piece in _split3(v):
        t = jnp.dot(tri, piece, preferred_element_type=F32)
        out = t if out is None else out + t
    return out


def _q_block(T):
    return _tile(T, 256, LANES)


def _gate_fwd(f, b_f, P, name):
    T = f.shape[0]
    tb = _q_block(T)

    def body(f_ref, b_ref, ct_ref, cc_ref, c0_ref, carry):
        @pl.when(pl.program_id(0) == 0)
        def _():
            carry[...] = jnp.zeros_like(carry)

        z = f_ref[...] + b_ref[...]
        logf = jnp.minimum(z, 0.0) - jnp.log(1.0 + jnp.exp(-jnp.abs(z)))
        row = lax.broadcasted_iota(jnp.int32, (tb, tb), 0)
        col = lax.broadcasted_iota(jnp.int32, (tb, tb), 1)
        tri = (col <= row).astype(BF16)
        c = _tri_dot(tri, logf) + carry[0:1, :]
        carry[...] = jnp.broadcast_to(c[tb - 1:tb, :], carry.shape)
        first = jnp.broadcast_to(c[0:1, :], c.shape)
        for p in range(P):
            shifted = c if p == 0 else pltpu.roll(c, LANES - 2 * p, 1)
            cc_ref[p] = shifted
            ct_ref[p] = shifted.T[0:SUBLANES, :]
            c0_ref[p] = (first if p == 0 else pltpu.roll(first, LANES - 2 * p, 1)).T[0:SUBLANES, :]

    rows = pl.BlockSpec((P, SUBLANES, tb), lambda i: (0, 0, i))
    return pl.pallas_call(
        body, grid=(T // tb,),
        in_specs=[pl.BlockSpec((tb, LANES), lambda i: (i, 0)), pl.BlockSpec((1, LANES), lambda i: (0, 0))],
        out_specs=[rows, pl.BlockSpec((P, tb, LANES), lambda i: (0, i, 0)), rows],
        out_shape=[jax.ShapeDtypeStruct((P, SUBLANES, T), F32), jax.ShapeDtypeStruct((P, T, LANES), F32),
                   jax.ShapeDtypeStruct((P, SUBLANES, T), F32)],
        scratch_shapes=[pltpu.VMEM((SUBLANES, LANES), F32)],
        name=name, compiler_params=_cp(),
    )(f, b_f)


def _gate_bwd(dc_cols, drowT, f, b_f, P, name):
    T = f.shape[0]
    tb = _tile(T, 256, LANES)
    nb = T // tb

    def body(dc_ref, dr_ref, f_ref, b_ref, df_ref, db_ref, carry):
        @pl.when(pl.program_id(0) == 0)
        def _():
            carry[...] = jnp.zeros_like(carry)
            db_ref[...] = jnp.zeros_like(db_ref)

        lane = lax.broadcasted_iota(jnp.int32, (tb, LANES), 1)
        dc = jnp.zeros((tb, LANES), F32)
        for p in range(P):
            rows = jnp.concatenate([dr_ref[p], jnp.zeros((LANES - SUBLANES, tb), F32)], axis=0)
            part = jnp.where(lane < 2, dc_ref[p] + rows.T, 0.0)
            dc = dc + (part if p == 0 else pltpu.roll(part, 2 * p, 1))
        row = lax.broadcasted_iota(jnp.int32, (tb, tb), 0)
        col = lax.broadcasted_iota(jnp.int32, (tb, tb), 1)
        tri = (col >= row).astype(BF16)
        dlogf = _tri_dot(tri, dc) + carry[0:1, :]
        carry[...] = jnp.broadcast_to(dlogf[0:1, :], carry.shape)
        z = f_ref[...] + b_ref[...]
        df = jnp.where(lane < 2 * P, dlogf * jax.nn.sigmoid(-z), 0.0)
        df_ref[...] = df
        db_ref[...] += jnp.sum(df, axis=0, keepdims=True)

    return pl.pallas_call(
        body, grid=(nb,),
        in_specs=[pl.BlockSpec((P, tb, LANES), lambda i: (0, nb - 1 - i, 0)),
                  pl.BlockSpec((P, SUBLANES, tb), lambda i: (0, 0, nb - 1 - i)),
                  pl.BlockSpec((tb, LANES), lambda i: (nb - 1 - i, 0)),
                  pl.BlockSpec((1, LANES), lambda i: (0, 0))],
        out_specs=[pl.BlockSpec((tb, LANES), lambda i: (nb - 1 - i, 0)),
                   pl.BlockSpec((SUBLANES, LANES), lambda i: (0, 0))],
        out_shape=[jax.ShapeDtypeStruct((T, LANES), F32), jax.ShapeDtypeStruct((SUBLANES, LANES), F32)],
        scratch_shapes=[pltpu.VMEM((SUBLANES, LANES), F32)],
        name=name, compiler_params=_cp(),
    )(dc_cols, drowT, f, b_f)


def _nt(a, b):
    return lax.dot_general(a, b, (((1,), (1,)), ((), ())), preferred_element_type=F32)


def _attn_fwd(qkv, cT, P, scale, name, riders=()):
    T = qkv.shape[0]
    n_r = len(riders)
    tq = _q_block(T)
    tw = _tile(T, 8 * tq, 2 * tq)
    cw = tw // 2
    assert cw % tq == 0, "the sequence must split into chunks of whole query blocks"
    nq = T // tq

    def body(q_ref, k_ref, v_ref, c_ref, *rest):
        w_refs, (o_ref, lse_ref), ow_refs = rest[:n_r], rest[n_r:n_r + 2], rest[n_r + 2:2 * n_r + 2]
        s_scr = rest[2 * n_r + 2]
        i = pl.program_id(1)
        if n_r:
            send_sems, recv_sems = rest[2 * n_r + 3:]
            first = (pl.program_id(0) == 0) & (i == 0)
            final = (pl.program_id(0) == P - 1) & (i == nq - 1)

            @pl.when(first)
            def _():
                for t in range(n_r):
                    for cp in _shard_half_copies(w_refs[t], ow_refs[t], send_sems, recv_sems, 3 * t, False):
                        cp.start()

            @pl.when(final)
            def _():
                for t in range(n_r):
                    for cp in _shard_half_copies(w_refs[t], ow_refs[t], send_sems, recv_sems, 3 * t, True):
                        cp.wait_recv()
                    for cp in _shard_half_copies(w_refs[t], ow_refs[t], send_sems, recv_sems, 3 * t, False):
                        cp.wait_send()

        lane = lax.broadcasted_iota(jnp.int32, (1, LANES), 1)
        q = (q_ref[...].astype(F32) * (scale * LOG2E)).astype(BF16)
        q_heads = (jnp.where(lane < 64, q, jnp.zeros_like(q)), jnp.where(lane >= 64, q, jnp.zeros_like(q)))
        c0 = c_ref[0, :, pl.ds(pl.multiple_of(i * tq, tq), LANES)][:, 0:1]

        def scores(start, width, a):
            bias = (c0 - c_ref[0, :, pl.ds(start, width)]) * LOG2E
            return _nt(q_heads[a], k_ref[pl.ds(start, width), :]) + bias[a:a + 1, :]

        def softmax_pv(start, width, s_of, carry):
            v = v_ref[pl.ds(start, width), :]
            one = jnp.ones_like(v)
            v_heads = (jnp.where(lane < 64, v, one), jnp.where(lane >= 64, v, one))
            new = []
            for a in range(2):
                m, acc = carry[a]
                s = s_of(a)
                m_new = jnp.maximum(m, jnp.max(s, axis=1, keepdims=True))
                p = jnp.exp2(s - m_new)
                acc = jnp.exp2(m - m_new) * acc + jnp.dot(p.astype(BF16), v_heads[a], preferred_element_type=F32)
                new.append((m_new, acc))
            return tuple(new)

        def fill(start, buf):
            for a in range(2):
                s_scr[2 * buf + a] = scores(start, cw, a)

        def wide(j, carry):
            base = pl.multiple_of(j * tw, tw)
            fill(base + cw, 1)
            carry = softmax_pv(base, cw, lambda a: s_scr[a], carry)
            fill(base + tw, 0)
            return softmax_pv(base + cw, cw, lambda a: s_scr[2 + a], carry)

        init = tuple((jnp.full((tq, 1), NEG_BIG, F32), jnp.zeros((tq, LANES), F32)) for _ in range(2))
        n_wide = (i * tq) // tw
        fill(0, 0)
        carry = lax.fori_loop(0, n_wide, wide, init)

        base = pl.multiple_of(n_wide * tw, tw)
        ahead = i * tq - base
        col_minus_row = (lax.broadcasted_iota(jnp.int32, (tq, cw), 1)
                         - lax.broadcasted_iota(jnp.int32, (tq, cw), 0))

        def causal(buf, first_key):
            return lambda a: jnp.where(col_minus_row <= ahead - first_key, s_scr[2 * buf + a], NEG_BIG)

        def one_chunk(cr):
            return softmax_pv(base, cw, causal(0, 0), cr)

        def two_chunks(cr):
            fill(base + cw, 1)
            cr = softmax_pv(base, cw, causal(0, 0), cr)
            return softmax_pv(base + cw, cw, causal(1, cw), cr)

        (m0, a0), (m1, a1) = lax.cond(ahead >= cw, two_chunks, one_chunk, carry)
        sums = jnp.where(lane < 64, pltpu.roll(a0, 64, 1), pltpu.roll(a1, 64, 1))
        o_ref[...] = (jnp.where(lane < 64, a0, a1) / sums).astype(BF16)
        l0, l1 = a0[:, 64:65], a1[:, 0:1]
        lse = jnp.where(lane == 0, m0 + jnp.log2(l0), jnp.where(lane == 1, m1 + jnp.log2(l1), 0.0))
        lse_ref[0] = lse.T[0:SUBLANES, :]

    return pl.pallas_call(
        body, grid=(P, nq),
        in_specs=[pl.BlockSpec((tq, LANES), lambda p, i: (i, p)),
                  pl.BlockSpec((T, LANES), lambda p, i: (0, P + p)),
                  pl.BlockSpec((T, LANES), lambda p, i: (0, 2 * P + p)),
                  pl.BlockSpec((1, SUBLANES, T), lambda p, i: (p, 0, 0))] + [_hbm()] * n_r,
        out_specs=[pl.BlockSpec((tq, LANES), lambda p, i: (i, p)),
                   pl.BlockSpec((1, SUBLANES, tq), lambda p, i: (p, 0, i))] + [_hbm()] * n_r,
        out_shape=[jax.ShapeDtypeStruct((T, LANES * P), BF16), jax.ShapeDtypeStruct((P, SUBLANES, T), F32)]
        + [jax.ShapeDtypeStruct((4,) + w.shape, w.dtype) for w in riders],
        scratch_shapes=[pltpu.VMEM((4, tq, cw), F32)]
        + ([pltpu.SemaphoreType.DMA((3 * n_r,)), pltpu.SemaphoreType.DMA((3 * n_r,))] if n_r else []),
        name=name, compiler_params=_cp(),
    )(qkv, qkv, qkv, cT, *riders)


def _attn_delta(do, o, P, name):
    T, D = o.shape
    tb = _tile(T, 256, LANES)

    def body(do_ref, o_ref, d_ref):
        lane = lax.broadcasted_iota(jnp.int32, (1, LANES), 1)
        for p in range(P):
            cols = slice(p * LANES, (p + 1) * LANES)
            prod = do_ref[:, cols].astype(F32) * o_ref[:, cols].astype(F32)
            d0 = jnp.sum(jnp.where(lane < 64, prod, 0.0), axis=1, keepdims=True)
            d1 = jnp.sum(jnp.where(lane >= 64, prod, 0.0), axis=1, keepdims=True)
            both = jnp.where(lane == 0, d0, jnp.where(lane == 1, d1, 0.0))
            d_ref[p] = both.T[0:SUBLANES, :]

    return pl.pallas_call(
        body, grid=(T // tb,),
        in_specs=[pl.BlockSpec((tb, D), lambda i: (i, 0)), pl.BlockSpec((tb, D), lambda i: (i, 0))],
        out_specs=pl.BlockSpec((P, SUBLANES, tb), lambda i: (0, 0, i)),
        out_shape=jax.ShapeDtypeStruct((P, SUBLANES, T), F32), name=name, compiler_params=_cp(),
    )(do, o)


def _attn_bwd(qkv, do, lseT, dT, c0T, c_cols, P, scale, name, riders=()):
    T = qkv.shape[0]
    n_r = len(riders)
    tq = _q_block(T)
    tw = _tile(T, 4 * tq, 2 * tq)
    cw = tw // 2
    assert cw % tq == 0, "the sequence must split into chunks of whole query blocks"
    nq = T // tq

    def body(q_ref, do_ref, k_ref, v_ref, lse_ref, d_ref, c0_ref, cc_ref, *rest):
        p_refs, (dq_ref, dk_ref, dv_ref, dc_ref, drow_ref) = rest[:n_r], rest[n_r:n_r + 5]
        o_refs = rest[n_r + 5:2 * n_r + 5]
        dq_acc0, dq_acc1, s_scr = rest[2 * n_r + 5:2 * n_r + 8]
        j = pl.program_id(1)
        if n_r:
            local_sems, send_sems, recv_sems = rest[2 * n_r + 8:]

            def exchange(t, incoming):
                return _chip_exchange_copies(p_refs[t], o_refs[t], local_sems.at[t], send_sems, recv_sems, 3 * t, incoming)

            @pl.when((pl.program_id(0) == 0) & (j == 0))
            def _():
                for t in range(n_r):
                    own, sent = exchange(t, False)
                    own.start()
                    for cp in sent:
                        cp.start()

            @pl.when((pl.program_id(0) == P - 1) & (j == nq - 1))
            def _():
                for t in range(n_r):
                    for cp in exchange(t, True)[1]:
                        cp.wait_recv()
                    own, sent = exchange(t, False)
                    for cp in sent:
                        cp.wait_send()
                    own.wait()


        @pl.when(j == 0)
        def _():
            dq_acc0[...] = jnp.zeros_like(dq_acc0)
            dq_acc1[...] = jnp.zeros_like(dq_acc1)

        lane = lax.broadcasted_iota(jnp.int32, (1, LANES), 1)
        in_head = (lane < 64, lane >= 64)
        k = k_ref[...]
        v = v_ref[...]
        zero = jnp.zeros_like(k)
        one = jnp.ones_like(k)
        k_heads = tuple(jnp.where(h, k, zero) for h in in_head)
        v_heads = tuple(jnp.where(h, v, zero) for h in in_head)
        k_ones = tuple(jnp.where(h, k, one) for h in in_head)
        cc = cc_ref[0]
        c_first = (cc[0:1, 0:1], cc[0:1, 1:2])
        c_rel = ((cc[:, 0:1] - c_first[0]) * LOG2E, (cc[:, 1:2] - c_first[1]) * LOG2E)
        dq_accs = (dq_acc0, dq_acc1)

        def scaled_q(start, width, factor):
            return (q_ref[pl.ds(start, width), :].astype(F32) * factor).astype(BF16)

        def block(start, width, carry, first_query=None, scores=None):
            q = scaled_q(start, width, scale)
            q_one = jnp.ones_like(q)
            dov = do_ref[pl.ds(start, width), :]
            lse = lse_ref[0, :, pl.ds(start, width)]
            dlt = d_ref[0, :, pl.ds(start, width)]
            c0 = c0_ref[0, :, pl.ds(start, width)]
            new = []
            for a in range(2):
                dk_a, dv_a = carry[a]
                rowv = lse[a:a + 1, :] + (c_first[a] - c0[a:a + 1, :]) * LOG2E
                if scores is None:
                    st = _nt(k_heads[a], scaled_q(start, width, scale * LOG2E))
                else:
                    st = scores(a)
                pt = jnp.exp2((st - c_rel[a]) - rowv)
                if first_query is not None:
                    row = lax.broadcasted_iota(jnp.int32, (tq, width), 0)
                    col = lax.broadcasted_iota(jnp.int32, (tq, width), 1)
                    pt = jnp.where(col - row >= first_query, pt, 0.0)
                dpt = _nt(v_heads[a], dov)
                dst_b = (pt * (dpt - dlt[a:a + 1, :])).astype(BF16)
                dv_a = dv_a + jnp.dot(pt.astype(BF16), dov, preferred_element_type=F32)
                dk_a = dk_a + jnp.dot(dst_b, jnp.where(in_head[a], q, q_one), preferred_element_type=F32)
                dq_accs[a][pl.ds(start, width), :] += lax.dot_general(
                    dst_b, k_ones[a], (((0,), (0,)), ((), ())), preferred_element_type=F32)
                new.append((dk_a, dv_a))
            return tuple(new)

        first_key = j * tq
        first_wide = first_key // tw + 1
        last = T // tw - 1

        def fill(trip, buf):
            q = scaled_q(pl.multiple_of(jnp.minimum(trip, last) * tw, tw), tw, scale * LOG2E)
            for a in range(2):
                s_scr[2 * buf + a] = _nt(k_heads[a], q)

        def trip(i, buf, cr):
            return block(pl.multiple_of(i * tw, tw), tw, cr, scores=lambda a: s_scr[2 * buf + a])

        def two_trips(p, cr):
            i = first_wide + 2 * p
            fill(i + 1, 1)
            cr = trip(i, 0, cr)
            fill(i + 2, 0)
            return trip(i + 1, 1, cr)

        init = tuple((jnp.zeros((tq, LANES), F32), jnp.zeros((tq, LANES), F32)) for _ in range(2))
        fill(first_wide, 0)
        diag = pl.multiple_of((first_key // cw) * cw, cw)
        carry = block(diag, cw, init, first_key - diag)
        carry = lax.cond(
            diag + cw < first_wide * tw,
            lambda cr: block(pl.multiple_of(diag + cw, cw), cw, cr), lambda cr: cr, carry)
        n_trips = last + 1 - first_wide
        carry = lax.fori_loop(0, n_trips // 2, two_trips, carry)
        (dk0, dv0), (dk1, dv1) = lax.cond(n_trips % 2 == 1, lambda cr: trip(last, 0, cr), lambda cr: cr, carry)
        dk_ref[...] = jnp.where(lane < 64, dk0, dk1).astype(BF16)
        dv_ref[...] = jnp.where(lane < 64, dv0, dv1).astype(BF16)
        dc_ref[0] = jnp.where(lane == 0, -dk0[:, 64:65], jnp.where(lane == 1, -dk1[:, 0:1], 0.0))

        @pl.when(j == nq - 1)
        def _():
            def finish(i, _):
                rows = pl.ds(pl.multiple_of(i * tq, tq), tq)
                a0 = dq_acc0[rows, :]
                a1 = dq_acc1[rows, :]
                dq_ref[rows, :] = (jnp.where(lane < 64, a0, a1) * scale).astype(BF16)
                sums = jnp.where(lane == 0, a0[:, 64:65], jnp.where(lane == 1, a1[:, 0:1], 0.0))
                drow_ref[0, :, rows] = sums.T[0:SUBLANES, :]
                return 0

            lax.fori_loop(0, nq, finish, 0)

    full = lambda col: pl.BlockSpec((T, LANES), lambda p, j: (0, col(p)))
    blk = lambda col: pl.BlockSpec((tq, LANES), lambda p, j: (j, col(p)))
    rows = pl.BlockSpec((1, SUBLANES, T), lambda p, j: (p, 0, 0))
    cols = pl.BlockSpec((1, tq, LANES), lambda p, j: (p, j, 0))
    D = LANES * P
    return pl.pallas_call(
        body, grid=(P, nq),
        in_specs=[full(lambda p: p), full(lambda p: p), blk(lambda p: P + p), blk(lambda p: 2 * P + p),
                  rows, rows, rows, cols] + [_hbm()] * n_r,
        out_specs=[full(lambda p: p), blk(lambda p: p), blk(lambda p: p), cols, rows] + [_hbm()] * n_r,
        out_shape=[jax.ShapeDtypeStruct((T, D), BF16), jax.ShapeDtypeStruct((T, D), BF16),
                   jax.ShapeDtypeStruct((T, D), BF16), jax.ShapeDtypeStruct((P, T, LANES), F32),
                   jax.ShapeDtypeStruct((P, SUBLANES, T), F32)]
        + [jax.ShapeDtypeStruct(r.shape, r.dtype) for r in riders],
        scratch_shapes=[pltpu.VMEM((T, LANES), F32), pltpu.VMEM((T, LANES), F32), pltpu.VMEM((4, tq, tw), F32)]
        + ([pltpu.SemaphoreType.DMA((n_r,)), pltpu.SemaphoreType.DMA((3 * n_r,)),
            pltpu.SemaphoreType.DMA((3 * n_r,))] if n_r else []),
        name=name, compiler_params=_cp(),
    )(qkv, do, qkv, qkv, lseT, dT, c0T, c_cols, *riders)


_SQRT_HALF = 0.7071067811865476
_INV_SQRT_2PI = 0.3989422804014327


def _gelu(v):
    return 0.5 * v * (1.0 + lax.erf(v * _SQRT_HALF))


def _gelu_and_grad(v):
    cdf = 0.5 * (1.0 + lax.erf(v * _SQRT_HALF))
    return v * cdf, cdf + v * (_INV_SQRT_2PI * jnp.exp(-0.5 * v * v))


def _sgu_fwd(a, ln_g, ln_b, w_tril, bias, name):
    T, W2 = a.shape
    W = W2 // 2
    G = w_tril.shape[0]
    tb = _tile(T, 256, LANES)

    def body(a_ref, g_ref, b_ref, w_ref, bias_ref, out_ref):
        zu = _gelu(a_ref[:, :W].astype(F32))
        zv = _gelu(a_ref[:, W:].astype(F32))
        mu = jnp.mean(zv, axis=-1, keepdims=True)
        d = zv - mu
        rstd = lax.rsqrt(jnp.mean(d * d, axis=-1, keepdims=True) + LN_EPS)
        vn = (d * rstd * g_ref[...] + b_ref[...]).astype(BF16)
        for c in range(tb // LANES):
            rs = slice(c * LANES, (c + 1) * LANES)
            for g in range(G):
                cs = slice(g * LANES, (g + 1) * LANES)
                mixed = jnp.dot(w_ref[g], vn[rs, cs], preferred_element_type=F32) + bias_ref[:, cs]
                out_ref[rs, cs] = (zu[rs, cs] * mixed).astype(BF16)

    return pl.pallas_call(
        body, grid=(T // tb,),
        in_specs=[pl.BlockSpec((tb, W2), lambda i: (i, 0)), pl.BlockSpec((1, W), lambda i: (0, 0)),
                  pl.BlockSpec((1, W), lambda i: (0, 0)), pl.BlockSpec((G, LANES, LANES), lambda i: (0, 0, 0)),
                  pl.BlockSpec((LANES, W), lambda i: (0, 0))],
        out_specs=pl.BlockSpec((tb, W), lambda i: (i, 0)),
        out_shape=jax.ShapeDtypeStruct((T, W), BF16), name=name, compiler_params=_cp(),
    )(a, ln_g.reshape(1, W), ln_b.reshape(1, W), w_tril, bias)


def _sgu_bwd(a, dgated, ln_g, ln_b, w_tril, w_tril_t, bias, name):
    T, W2 = a.shape
    W = W2 // 2
    G = w_tril.shape[0]
    tb = _tile(T, 256, LANES)

    def body(a_ref, dg_ref, g_ref, b_ref, w_ref, wt_ref, bias_ref,
             da_ref, dws_ref, dbias_ref, dlng_ref, dlnb_ref, dvn_ref):
        @pl.when(pl.program_id(0) == 0)
        def _():
            dws_ref[...] = jnp.zeros_like(dws_ref)
            dbias_ref[...] = jnp.zeros_like(dbias_ref)
            dlng_ref[...] = jnp.zeros_like(dlng_ref)
            dlnb_ref[...] = jnp.zeros_like(dlnb_ref)

        up = a_ref[:, :W].astype(F32)
        vp = a_ref[:, W:].astype(F32)
        zu, gu = _gelu_and_grad(up)
        zv, gv = _gelu_and_grad(vp)
        mu = jnp.mean(zv, axis=-1, keepdims=True)
        d = zv - mu
        rstd = lax.rsqrt(jnp.mean(d * d, axis=-1, keepdims=True) + LN_EPS)
        vhat = d * rstd
        gam = g_ref[...]
        vn = (vhat * gam + b_ref[...]).astype(BF16)
        dgated = dg_ref[...]
        for c in range(tb // LANES):
            rs = slice(c * LANES, (c + 1) * LANES)
            for g in range(G):
                cs = slice(g * LANES, (g + 1) * LANES)
                vb = vn[rs, cs]
                mixed = jnp.dot(w_ref[g], vb, preferred_element_type=F32) + bias_ref[:, cs]
                dgt = dgated[rs, cs]
                da_ref[rs, cs] = (dgt * mixed * gu[rs, cs]).astype(BF16)
                dmx = dgt * zu[rs, cs]
                dbias_ref[:, cs] += dmx
                dmb = dmx.astype(BF16)
                dws_ref[g] += _nt(dmb, vb)
                dvn_ref[rs, cs] = jnp.dot(wt_ref[g], dmb, preferred_element_type=F32)
        dvn = dvn_ref[...]
        dlng_ref[...] += jnp.sum(dvn * vhat, axis=0, keepdims=True)
        dlnb_ref[...] += jnp.sum(dvn, axis=0, keepdims=True)
        dvh = dvn * gam
        dzv = rstd * (dvh - jnp.mean(dvh, axis=-1, keepdims=True)
                      - vhat * jnp.mean(dvh * vhat, axis=-1, keepdims=True))
        da_ref[:, W:] = (dzv * gv).astype(BF16)

    const2 = lambda shape: pl.BlockSpec(shape, lambda i: (0, 0))
    const3 = pl.BlockSpec((G, LANES, LANES), lambda i: (0, 0, 0))
    return pl.pallas_call(
        body, grid=(T // tb,),
        in_specs=[pl.BlockSpec((tb, W2), lambda i: (i, 0)), pl.BlockSpec((tb, W), lambda i: (i, 0)),
                  const2((1, W)), const2((1, W)), const3, const3, const2((LANES, W))],
        out_specs=[pl.BlockSpec((tb, W2), lambda i: (i, 0)), const3, const2((LANES, W)),
                   const2((SUBLANES, W)), const2((SUBLANES, W))],
        out_shape=[jax.ShapeDtypeStruct((T, W2), BF16), jax.ShapeDtypeStruct((G, LANES, LANES), F32),
                   jax.ShapeDtypeStruct((LANES, W), F32), jax.ShapeDtypeStruct((SUBLANES, W), F32),
                   jax.ShapeDtypeStruct((SUBLANES, W), F32)],
        scratch_shapes=[pltpu.VMEM((tb, W), F32)],
        name=name, compiler_params=_cp(),
    )(a, dgated, ln_g.reshape(1, W), ln_b.reshape(1, W), w_tril, w_tril_t, bias)


def _adam_math(w, g, m, v):
    m = ADAM_B1 * m + (1.0 - ADAM_B1) * g
    v = ADAM_B2 * v + (1.0 - ADAM_B2) * (g * g)
    m_hat = m / (1.0 - ADAM_B1 ** ADAM_STEP)
    v_hat = v / (1.0 - ADAM_B2 ** ADAM_STEP)
    delta = -ADAM_LR * (m_hat / (jnp.sqrt(v_hat) + ADAM_EPS) + ADAM_WD * w)
    return delta, m, v


def _adamw_halves(mine, theirs, c_idx, w, m, v, name):
    R, C = w.shape
    rh = R // 2
    tb = _row_tile(rh, C)
    nb = rh // tb

    def body(c_ref, a_ref, b_ref, w_ref, m_ref, v_ref, g_ref, d_ref, mo_ref, vo_ref):
        g = jnp.where(pl.program_id(0) == c_ref[0], a_ref[...], b_ref[...])
        d, mm, vv = _adam_math(w_ref[...], g, m_ref[...], v_ref[...])
        g_ref[...] = g
        d_ref[...] = d
        mo_ref[...] = mm
        vo_ref[...] = vv

    half = pl.BlockSpec((tb, C), lambda h, i, c: (i, 0))
    row = pl.BlockSpec((tb, C), lambda h, i, c: (h * nb + i, 0))
    sds = jax.ShapeDtypeStruct((R, C), F32)
    return pl.pallas_call(
        body,
        grid_spec=pltpu.PrefetchScalarGridSpec(
            num_scalar_prefetch=1, grid=(2, nb), in_specs=[half, half, row, row, row], out_specs=[row] * 4),
        out_shape=[sds] * 4, name=name, compiler_params=_cp())(c_idx, mine, theirs, w, m, v)


def _adamw_sum(parts, w, m, v, name):
    K, R, C = parts.shape
    tb = _tile(R, 128, SUBLANES)

    def body(p_ref, w_ref, m_ref, v_ref, g_ref, d_ref, mo_ref, vo_ref):
        g = p_ref[0]
        for k in range(1, K):
            g = g + p_ref[k]
        d, mm, vv = _adam_math(w_ref[...], g, m_ref[...], v_ref[...])
        g_ref[...] = g
        d_ref[...] = d
        mo_ref[...] = mm
        vo_ref[...] = vv

    row = pl.BlockSpec((tb, C), lambda i: (i, 0))
    sds = jax.ShapeDtypeStruct((R, C), F32)
    return pl.pallas_call(
        body, grid=(R // tb,),
        in_specs=[pl.BlockSpec((K, tb, C), lambda i: (0, i, 0)), row, row, row],
        out_specs=[row] * 4, out_shape=[sds] * 4, name=name, compiler_params=_cp())(parts, w, m, v)


def _pair_sum(g_all, recv, c_idx, name):
    K, R, C = g_all.shape
    rh = R // 2
    tb = _row_tile(rh, C)
    nb = rh // tb

    def body(c_ref, a_ref, b_ref, o_ref):
        o_ref[...] = (a_ref[...] + b_ref[...]).astype(BF16)

    return pl.pallas_call(
        body,
        grid_spec=pltpu.PrefetchScalarGridSpec(
            num_scalar_prefetch=1, grid=(K, nb),
            in_specs=[pl.BlockSpec((1, tb, C), lambda k, i, c: (k, c[0] * nb + i, 0)),
                      pl.BlockSpec((1, tb, C), lambda k, i, c: (k, i, 0))],
            out_specs=pl.BlockSpec((1, tb, C), lambda k, i, c: (k, i, 0))),
        out_shape=jax.ShapeDtypeStruct((K, rh, C), BF16), name=name, compiler_params=_cp(),
    )(c_idx, g_all, recv)


def _sum_parts(parts, name):
    K, R, C = parts.shape
    tb = _row_tile(R, C)

    def body(p_ref, o_ref):
        g = p_ref[0].astype(F32)
        for k in range(1, K):
            g = g + p_ref[k].astype(F32)
        o_ref[...] = g

    return pl.pallas_call(
        body, grid=(R // tb,), in_specs=[pl.BlockSpec((K, tb, C), lambda i: (0, i, 0))],
        out_specs=pl.BlockSpec((tb, C), lambda i: (i, 0)),
        out_shape=jax.ShapeDtypeStruct((R, C), F32), name=name, compiler_params=_cp())(parts)


_CHIP_RELATIONS = ((1, 0), (0, 1), (1, 1))


def _position():
    return lax.axis_index("x"), lax.axis_index("y"), lax.axis_index("c")


def _flip(v, bit):
    return 1 - v if bit else v


def _shard_half_copies(w_ref, ow_ref, send_sems, recv_sems, sem0, incoming):
    x, y, c = _position()
    rh = w_ref.shape[0] // 2
    rows = pl.ds(pl.multiple_of(c * rh, 16), rh)
    out = []
    for r, (dx, dy) in enumerate(_CHIP_RELATIONS):
        px, py = _flip(x, dx), _flip(y, dy)
        slot = 2 * px + py if incoming else 2 * x + y
        out.append(pltpu.make_async_remote_copy(
            src_ref=w_ref.at[rows, :], dst_ref=ow_ref.at[slot, rows, :], send_sem=send_sems.at[sem0 + r],
            recv_sem=recv_sems.at[sem0 + r], device_id=(px, py, c), device_id_type=MESH))
    return out


def _gather_weights(w_pack, side, name):
    n_side = 0 if side is None else 1

    def between_chips(*refs):
        if n_side:
            w_ref, s_ref, ow_ref, os_ref, local_sem, send_sems, recv_sems = refs
        else:
            w_ref, ow_ref, send_sems, recv_sems = refs
        x, y, c = _position()
        me = 2 * x + y
        if n_side:
            own_side = pltpu.make_async_copy(s_ref, os_ref.at[me], local_sem)
            own_side.start()

        def side_copies(incoming):
            out = []
            for r, (dx, dy) in enumerate(_CHIP_RELATIONS):
                px, py = _flip(x, dx), _flip(y, dy)
                out.append(pltpu.make_async_remote_copy(
                    src_ref=s_ref, dst_ref=os_ref.at[2 * px + py if incoming else me],
                    send_sem=send_sems.at[3 + r], recv_sem=recv_sems.at[3 + r],
                    device_id=(px, py, c), device_id_type=MESH))
            return out

        sent = _shard_half_copies(w_ref, ow_ref, send_sems, recv_sems, 0, False) + (side_copies(False) if n_side else [])
        for cp in sent:
            cp.start()
        for cp in _shard_half_copies(w_ref, ow_ref, send_sems, recv_sems, 0, True) + (side_copies(True) if n_side else []):
            cp.wait_recv()
        for cp in sent:
            cp.wait_send()
        if n_side:
            own_side.wait()

    sems = [pltpu.SemaphoreType.DMA((6,)), pltpu.SemaphoreType.DMA((6,))]
    gathered = jax.ShapeDtypeStruct((4,) + w_pack.shape, w_pack.dtype)
    if n_side:
        halves, sides = pl.pallas_call(
            between_chips, in_specs=[_hbm(), _hbm()], out_specs=[_hbm(), _hbm()],
            out_shape=[gathered, jax.ShapeDtypeStruct((4,) + side.shape, side.dtype)],
            scratch_shapes=[pltpu.SemaphoreType.DMA(())] + sems,
            name=name + "_ici", compiler_params=_cp(),
        )(w_pack, side)
    else:
        sides = None
        halves = pl.pallas_call(
            between_chips, in_specs=[_hbm()], out_specs=_hbm(), out_shape=gathered, scratch_shapes=sems,
            name=name + "_ici", compiler_params=_cp(),
        )(w_pack)
    return _hand_to_sibling(halves, name + "_d2d"), sides


def _hand_to_sibling(halves, name):
    _, R, C = halves.shape
    rh = R // 2

    def to_sibling(g_ref, o_ref, send_sems, recv_sems):
        x, y, c = _position()

        def copy(r, cc):
            dx, dy = _CHIP_RELATIONS[r]
            slot = 2 * _flip(x, dx) + _flip(y, dy)
            rows = pl.ds(pl.multiple_of(cc * rh, 16), rh)
            return pltpu.make_async_remote_copy(
                src_ref=g_ref.at[slot, rows, :], dst_ref=o_ref.at[slot, rows, :], send_sem=send_sems.at[r],
                recv_sem=recv_sems.at[r], device_id=(x, y, 1 - c), device_id_type=MESH)

        sent = [copy(r, c) for r in range(3)]
        for cp in sent:
            cp.start()
        for r in range(3):
            copy(r, 1 - c).wait_recv()
        for cp in sent:
            cp.wait_send()

    return pl.pallas_call(
        to_sibling, in_specs=[_hbm()], out_specs=_hbm(), input_output_aliases={0: 0},
        out_shape=jax.ShapeDtypeStruct(halves.shape, halves.dtype),
        scratch_shapes=[pltpu.SemaphoreType.DMA((3,)), pltpu.SemaphoreType.DMA((3,))],
        name=name, compiler_params=_cp(),
    )(halves)


def _sibling_halves(g_all, name):
    K, R, C = g_all.shape
    rh = R // 2

    def body(g_ref, o_ref, send_sem, recv_sem):
        x, y, c = _position()
        start = pl.multiple_of((1 - c) * rh, SUBLANES)
        cp = pltpu.make_async_remote_copy(
            src_ref=g_ref.at[:, pl.ds(start, rh), :], dst_ref=o_ref, send_sem=send_sem, recv_sem=recv_sem,
            device_id=(x, y, 1 - c), device_id_type=MESH)
        cp.start()
        cp.wait_recv()
        cp.wait_send()

    return pl.pallas_call(
        body, in_specs=[_hbm()], out_specs=_hbm(),
        out_shape=jax.ShapeDtypeStruct((K, rh, C), F32),
        scratch_shapes=[pltpu.SemaphoreType.DMA(()), pltpu.SemaphoreType.DMA(())],
        name=name, compiler_params=_cp(),
    )(g_all)


def _chip_exchange_copies(p_ref, o_ref, local_sem, send_sems, recv_sems, sem0, incoming):
    x, y, c = _position()
    me = 2 * x + y
    out = []
    for r, (dx, dy) in enumerate(_CHIP_RELATIONS):
        px, py = _flip(x, dx), _flip(y, dy)
        src_slot, dst_slot = (me, 2 * px + py) if incoming else (2 * px + py, me)
        out.append(pltpu.make_async_remote_copy(
            src_ref=p_ref.at[src_slot], dst_ref=o_ref.at[dst_slot], send_sem=send_sems.at[sem0 + r],
            recv_sem=recv_sems.at[sem0 + r], device_id=(px, py, c), device_id_type=MESH))
    return pltpu.make_async_copy(p_ref.at[me], o_ref.at[me], local_sem), out


def _chip_exchange(parts, name):
    def body(p_ref, o_ref, local_sem, send_sems, recv_sems):
        own, sent = _chip_exchange_copies(p_ref, o_ref, local_sem, send_sems, recv_sems, 0, False)
        own.start()
        for cp in sent:
            cp.start()
        for cp in _chip_exchange_copies(p_ref, o_ref, local_sem, send_sems, recv_sems, 0, True)[1]:
            cp.wait_recv()
        for cp in sent:
            cp.wait_send()
        own.wait()

    return pl.pallas_call(
        body, in_specs=[_hbm()], out_specs=_hbm(),
        out_shape=jax.ShapeDtypeStruct(parts.shape, parts.dtype),
        scratch_shapes=[pltpu.SemaphoreType.DMA(()), pltpu.SemaphoreType.DMA((3,)),
                        pltpu.SemaphoreType.DMA((3,))],
        name=name, compiler_params=_cp(),
    )(parts)


def _swap_with_sibling(half, name):
    rh, C = half.shape

    def body(h_ref, o_ref, send_sem, recv_sem):
        x, y, c = _position()
        cp = pltpu.make_async_remote_copy(
            src_ref=h_ref, dst_ref=o_ref, send_sem=send_sem, recv_sem=recv_sem,
            device_id=(x, y, 1 - c), device_id_type=MESH)
        cp.start()
        cp.wait_recv()
        cp.wait_send()

    return pl.pallas_call(
        body, in_specs=[_hbm()], out_specs=_hbm(),
        out_shape=jax.ShapeDtypeStruct((rh, C), F32),
        scratch_shapes=[pltpu.SemaphoreType.DMA(()), pltpu.SemaphoreType.DMA(())],
        name=name, compiler_params=_cp(),
    )(half)


def _gather_all(part, name):
    R, C = part.shape
    masks = [(b >> 2 & 1, b >> 1 & 1, b & 1) for b in range(1, 8)]

    def body(p_ref, o_ref, local_sem, send_sems, recv_sems):
        x, y, c = _position()
        me = 4 * x + 2 * y + c
        own = pltpu.make_async_copy(p_ref, o_ref.at[me], local_sem)
        own.start()

        def copy(r, slot):
            dx, dy, dc = masks[r]
            return pltpu.make_async_remote_copy(
                src_ref=p_ref, dst_ref=o_ref.at[slot], send_sem=send_sems.at[r], recv_sem=recv_sems.at[r],
                device_id=(_flip(x, dx), _flip(y, dy), _flip(c, dc)), device_id_type=MESH)

        sent = [copy(r, me) for r in range(7)]
        for cp in sent:
            cp.start()
        for r in range(7):
            dx, dy, dc = masks[r]
            copy(r, 4 * _flip(x, dx) + 2 * _flip(y, dy) + _flip(c, dc)).wait_recv()
        for cp in sent:
            cp.wait_send()
        own.wait()

    return pl.pallas_call(
        body, in_specs=[_hbm()], out_specs=_hbm(),
        out_shape=jax.ShapeDtypeStruct((8, R, C), F32),
        scratch_shapes=[pltpu.SemaphoreType.DMA(()), pltpu.SemaphoreType.DMA((7,)),
                        pltpu.SemaphoreType.DMA((7,))],
        name=name, compiler_params=_cp(),
    )(part)


def _pack(arrs, row_mult, cols=PACK_COLS, lead=0):
    head = arrs[0].shape[:lead]
    pieces = []
    for a in arrs:
        flat = a.astype(F32).reshape(head + (-1,))
        fill = -flat.shape[-1] % cols
        if fill:
            flat = jnp.concatenate([flat, jnp.zeros(head + (fill,), F32)], axis=-1)
        pieces.append(flat.reshape(head + (-1, cols)))
    rows = sum(p.shape[lead] for p in pieces)
    fill = -rows % row_mult
    if fill:
        pieces.append(jnp.zeros(head + (fill, cols), F32))
    return jnp.concatenate(pieces, axis=lead) if len(pieces) > 1 else pieces[0]


def _unpack(buf, shapes):
    lead = buf.shape[:-2]
    cols = buf.shape[-1]
    out, off = [], 0
    for shp in shapes:
        n = math.prod(shp)
        rows = -(-n // cols)
        piece = buf[..., off:off + rows, :]
        if rows * cols != n:
            piece = piece.reshape(lead + (-1,))[..., :n]
        out.append(piece.reshape(lead + tuple(shp)))
        off += rows
    return out


def _cols_from_chips(g):
    k, L, A, n = g.shape
    return jnp.transpose(g, (1, 2, 0, 3)).reshape(L, A, k * n)


def _rows_from_chips(g):
    k, L, n, B = g.shape
    return jnp.transpose(g, (1, 0, 2, 3)).reshape(L, k * n, B)


def _cols_to_chips(full, k=4):
    L, A, N = full.shape
    return jnp.transpose(full.reshape(L, A, k, N // k), (2, 0, 1, 3))


def _rows_to_chips(full, k=4):
    L, N, B = full.shape
    return jnp.transpose(full.reshape(L, k, N // k, B), (1, 0, 2, 3))


def kernel(x, mixer_norm_w, attn_w_in, attn_b_f, attn_w_out, sgu_w_in, sgu_ln_g, sgu_ln_b, sgu_w_s, sgu_b_s, sgu_w_out, ffn_norm_w, ffn_w_in, ffn_w_out, final_norm_w, loss_target, m_mixer_norm_w, m_attn_w_in, m_attn_b_f, m_attn_w_out, m_sgu_w_in, m_sgu_ln_g, m_sgu_ln_b, m_sgu_w_s, m_sgu_b_s, m_sgu_w_out, m_ffn_norm_w, m_ffn_w_in, m_ffn_w_out, m_final_norm_w, v_mixer_norm_w, v_attn_w_in, v_attn_b_f, v_attn_w_out, v_sgu_w_in, v_sgu_ln_g, v_sgu_ln_b, v_sgu_w_s, v_sgu_b_s, v_sgu_w_out, v_ffn_norm_w, v_ffn_w_in, v_ffn_w_out, v_final_norm_w):
    T, D = x.shape[1], x.shape[2]
    depth = mixer_norm_w.shape[0]
    H = attn_b_f.shape[1]
    P = D // LANES
    assert D % LANES == 0 and D // H == 64 and 2 * P == H and 2 * P <= LANES
    G = sgu_w_s.shape[1]
    W = sgu_w_out.shape[1] * 4
    assert sgu_w_s.shape[2] == LANES and W == G * LANES
    scale = float(D // H) ** -0.5
    f_pad = LANES
    c_idx = lax.axis_index("c").astype(jnp.int32).reshape(1)

    groups = [
        ([attn_w_out, sgu_w_in, sgu_w_out, ffn_w_out, sgu_ln_g, sgu_ln_b],
         [m_attn_w_out, m_sgu_w_in, m_sgu_w_out, m_ffn_w_out, m_sgu_ln_g, m_sgu_ln_b],
         [v_attn_w_out, v_sgu_w_in, v_sgu_w_out, v_ffn_w_out, v_sgu_ln_g, v_sgu_ln_b]),
        ([ffn_w_in], [m_ffn_w_in], [v_ffn_w_in]),
        ([attn_w_in], [m_attn_w_in], [v_attn_w_in]),
    ]
    group_cols = [D, ffn_w_in.shape[2], attn_w_in.shape[2]]
    group_shapes = [[a.shape for a in g[0]] for g in groups]
    w_packs = [_pack(g[0], 512, cols) for g, cols in zip(groups, group_cols)]
    ln_pack = _pack([sgu_ln_g, sgu_ln_b], SUBLANES)
    my_chip = 2 * lax.axis_index("x") + lax.axis_index("y")
    w_packs_b = [w.astype(BF16) for w in w_packs]

    def finish_gather(t, gat):
        return _unpack(lax.dynamic_update_index_in_dim(gat, w_packs_b[t], my_chip, 0), group_shapes[t])

    gat, gat_ln = _gather_weights(w_packs_b[2], ln_pack, "gather_weights_2")
    (g_ai,) = finish_gather(2, gat)
    g_lng, g_lnb = _unpack(gat_ln, [sgu_ln_g.shape, sgu_ln_b.shape])
    w_ai = _cols_from_chips(g_ai)
    w_ai = jnp.pad(w_ai, ((0, 0), (0, 0), (0, 3 * D + f_pad - w_ai.shape[2])))
    b_f_pad = jnp.pad(attn_b_f, ((0, 0), (0, LANES - H)))
    xs = x.reshape(T, D)
    h, qkv, f = _norm_mm(xs, mixer_norm_w[0], w_ai[0], ((3 * D, BF16), (f_pad, F32)), "attn_qkv_0")
    cT, c_cols, c0T = _gate_fwd(f, b_f_pad[0:1], P, "gate_fwd_0")
    o, lseT, halves_0, halves_1 = _attn_fwd(qkv, cT, P, scale, "attn_fwd_0", riders=(w_packs_b[0], w_packs_b[1]))
    first_attention = dict(h=h, qkv=qkv, f=f, c0T=c0T, c_cols=c_cols, o=o, lseT=lseT)
    g_ao, g_si, g_so, g_fo, _, _ = finish_gather(0, _hand_to_sibling(halves_0, "gather_weights_0_d2d"))
    (g_fi,) = finish_gather(1, _hand_to_sibling(halves_1, "gather_weights_1_d2d"))
    w_ao = _rows_from_chips(g_ao)
    w_si = _cols_from_chips(g_si)
    w_so = _rows_from_chips(g_so)
    w_fo = _rows_from_chips(g_fo)
    w_fi5 = g_fi.reshape((2, 2) + g_fi.shape[1:])
    ln_g = jnp.transpose(g_lng, (1, 0, 2)).reshape(sgu_ln_g.shape[0], W)
    ln_b = jnp.transpose(g_lnb, (1, 0, 2)).reshape(sgu_ln_b.shape[0], W)
    w_tril = jnp.tril(sgu_w_s)
    w_tril_b = w_tril.astype(BF16)
    w_tril_tb = jnp.swapaxes(w_tril, 2, 3).astype(BF16)
    sgu_bias = jnp.repeat(jnp.swapaxes(sgu_b_s, 1, 2), LANES, axis=2)

    saved = []
    for i in range(depth):
        j = i // 2
        rec = {"x_in": xs}
        if i == 0:
            rec.update(first_attention)
            h, o = rec["h"], rec["o"]
            x_mid = _mm(o, w_ao[j], "nn", F32, f"attn_out_{i}", res=xs)
        elif i % 2 == 0:
            h, qkv, f = _norm_mm(xs, mixer_norm_w[i], w_ai[j], ((3 * D, BF16), (f_pad, F32)), f"attn_qkv_{i}")
            cT, c_cols, c0T = _gate_fwd(f, b_f_pad[j:j + 1], P, f"gate_fwd_{i}")
            o, lseT = _attn_fwd(qkv, cT, P, scale, f"attn_fwd_{i}")
            x_mid = _mm(o, w_ao[j], "nn", F32, f"attn_out_{i}", res=xs)
            rec.update(qkv=qkv, f=f, c0T=c0T, c_cols=c_cols, o=o, lseT=lseT)
        else:
            h, a = _norm_mm(xs, mixer_norm_w[i], w_si[j], ((2 * W, BF16),), f"sgu_in_{i}")
            gated = _sgu_fwd(a, ln_g[j], ln_b[j], w_tril_b[j], sgu_bias[j], f"sgu_fwd_{i}")
            x_mid = _mm(gated, w_so[j], "nn", F32, f"sgu_out_{i}", res=xs)
            rec.update(a=a, gated=gated)
        h2, fa, s = _ffn_in_act(x_mid, ffn_norm_w[i], w_fi5, i, f"ffn_in_{i}")
        xs = _mm(s, w_fo[i], "nn", F32, f"ffn_out_{i}", res=x_mid)
        rec.update(h=h, x_mid=x_mid, h2=h2, fa=fa, s=s)
        saved.append(rec)

    gx, loss_acc, dw_final = _loss_head(xs, final_norm_w, loss_target.reshape(T, D), "loss_head")
    loss = lax.psum(loss_acc[0, 0], ("x", "y", "c"))

    n_attn, n_sgu = attn_w_in.shape[0], sgu_w_in.shape[0]
    d_mixer_norm, d_ffn_norm = [None] * depth, [None] * depth
    d_ai, d_ao, d_bf = [None] * n_attn, [None] * n_attn, [None] * n_attn
    d_si, d_so, d_lng, d_lnb, d_ws, d_bs = ([None] * n_sgu for _ in range(6))
    d_fi, d_fo = [None] * depth, [None] * depth

    def pair_sums(t):
        if t == 0:
            grads = [_rows_to_chips(jnp.stack(d_ao)), _cols_to_chips(jnp.stack(d_si)), _rows_to_chips(jnp.stack(d_so)),
                     _rows_to_chips(jnp.stack(d_fo)),
                     jnp.transpose(jnp.stack(d_lng).reshape(n_sgu, 4, W // 4), (1, 0, 2)),
                     jnp.transpose(jnp.stack(d_lnb).reshape(n_sgu, 4, W // 4), (1, 0, 2))]
        elif t == 1:
            grads = [jnp.stack(d_fi, axis=1)]
        else:
            grads = [_cols_to_chips(jnp.stack(d_ai))]
        g_all = _pack(grads, 512, group_cols[t], lead=1)
        from_sibling = _sibling_halves(g_all, f"grad_sibling_halves_{t}")
        return _pair_sum(g_all, from_sibling, c_idx, f"grad_pair_sum_{t}")

    for i in reversed(range(depth)):
        j = i // 2
        rec = saved[i]
        d_fo[i] = _mm(rec["s"], gx, "tn", F32, f"ffn_out_wgrad_{i}")
        da = _ffn_out_bwd_act(gx, w_fo[i], rec["fa"], f"ffn_out_bwd_{i}")
        da = da.reshape((4,) + da.shape[2:])
        d_fi[i] = _mm_tn_shards(rec["h2"], da, f"ffn_in_wgrad_{i}")
        gx, dwn = _nt_norm_bwd(da, g_fi, i, rec["x_mid"], ffn_norm_w[i], gx, f"ffn_in_bwd_{i}")
        d_ffn_norm[i] = dwn[0]
        if i % 2 == 0:
            do = _mm(gx, w_ao[j], "nt", BF16, f"attn_out_bwd_{i}")
            d_ao[j] = _mm(rec["o"], gx, "tn", F32, f"attn_out_wgrad_{i}")
            dT = _attn_delta(do, rec["o"], P, f"attn_delta_{i}")
            riders = ()
            if i == 0:
                riders = tuple(pair_sums(t) for t in range(2))
            dq, dk, dv, dc_cols, drowT, *from_chips_early = _attn_bwd(
                rec["qkv"], do, rec["lseT"], dT, rec["c0T"], rec["c_cols"], P, scale, f"attn_bwd_{i}", riders=riders)
            df, dbf = _gate_bwd(dc_cols, drowT, rec["f"], b_f_pad[j:j + 1], P, f"gate_bwd_{i}")
            d_bf[j] = dbf[0, :H]
            dproj = jnp.concatenate([dq, dk, dv, df.astype(BF16)], axis=1)
            d_ai[j] = _mm(rec["h"], dproj, "tn", F32, f"attn_in_wgrad_{i}")[:, :3 * D + H]
            dmix, w_mix = dproj, w_ai
        else:
            dgated = _mm(gx, w_so[j], "nt", F32, f"sgu_out_bwd_{i}")
            d_so[j] = _mm(rec["gated"], gx, "tn", F32, f"sgu_out_wgrad_{i}")
            da_s, dws, dbias, dlng, dlnb = _sgu_bwd(rec["a"], dgated, ln_g[j], ln_b[j], w_tril_b[j],
                                                    w_tril_tb[j], sgu_bias[j], f"sgu_bwd_{i}")
            d_ws[j] = jnp.tril(dws)
            d_bs[j] = jnp.sum(dbias.reshape(LANES, G, LANES), axis=2).T
            d_lng[j], d_lnb[j] = dlng[0], dlnb[0]
            d_si[j] = _mm(rec["h"], da_s, "tn", F32, f"sgu_in_wgrad_{i}")
            dmix, w_mix = da_s, w_si
        gx, dwn = _nt_norm_bwd(dmix[None], w_mix[None], j, rec["x_in"], mixer_norm_w[i], gx, f"mixer_in_bwd_{i}")
        d_mixer_norm[i] = dwn[0]
    grad_x = gx.reshape(x.shape)

    reduced = []
    for t, cols in enumerate(group_cols):
        from_chips = from_chips_early[t] if t < 2 else _chip_exchange(pair_sums(t), f"grad_chip_exchange_{t}")
        my_half = _sum_parts(from_chips, f"grad_chip_sum_{t}")
        sibling_half = _swap_with_sibling(my_half, f"grad_swap_halves_{t}")
        packs = _adamw_halves(my_half, sibling_half, c_idx, w_packs[t], _pack(groups[t][1], 512, cols),
                              _pack(groups[t][2], 512, cols), f"adamw_sharded_{t}")
        reduced.append([_unpack(p, group_shapes[t]) for p in packs])

    def sharded_outputs(which):
        (ao, si, so, fo, lng, lnb), (fi,), (ai,) = (reduced[t][which] for t in range(3))
        return [ai, ao, si, so, fi, fo, lng, lnb]

    g_sh, d_sh, m_sh, v_sh = (sharded_outputs(w) for w in range(4))

    repl = [mixer_norm_w, attn_b_f, sgu_w_s, sgu_b_s, ffn_norm_w, final_norm_w]
    repl_m = [m_mixer_norm_w, m_attn_b_f, m_sgu_w_s, m_sgu_b_s, m_ffn_norm_w, m_final_norm_w]
    repl_v = [v_mixer_norm_w, v_attn_b_f, v_sgu_w_s, v_sgu_b_s, v_ffn_norm_w, v_final_norm_w]
    repl_shapes = [a.shape for a in repl]
    repl_grads = [jnp.stack(d_mixer_norm), jnp.stack(d_bf), jnp.stack(d_ws), jnp.stack(d_bs),
                  jnp.stack(d_ffn_norm), dw_final[0]]
    parts = _gather_all(_pack(repl_grads, SUBLANES), "grad_gather_replicated")
    g_rep, d_rep, m_rep, v_rep = _adamw_sum(parts, _pack(repl, SUBLANES), _pack(repl_m, SUBLANES),
                                            _pack(repl_v, SUBLANES), "adamw_replicated")
    g_r = _unpack(g_rep, repl_shapes)
    d_r = _unpack(d_rep, repl_shapes)
    m_r = _unpack(m_rep, repl_shapes)
    v_r = _unpack(v_rep, repl_shapes)

    def ordered(sh, rp):
        ai, ao, si, so, fi, fo, lng, lnb = sh
        mn, bf, ws, bs, fn, fin = rp
        return [mn, ai, bf, ao, si, lng, lnb, ws, bs, so, fn, fi, fo, fin]

    return (loss, grad_x, *ordered(g_sh, g_r), *ordered(d_sh, d_r), *ordered(m_sh, m_r), *ordered(v_sh, v_r))
```

```python
import math

import jax
import jax.numpy as jnp
from jax import lax
from jax.experimental import pallas as pl
from jax.experimental.pallas import tpu as pltpu

F32 = jnp.float32
BF16 = jnp.bfloat16
NORM_EPS = 1e-6
LN_EPS = 1e-5
ADAM_LR = 0.001
ADAM_B1 = 0.9
ADAM_B2 = 0.999
ADAM_EPS = 1e-08
ADAM_WD = 0.01
ADAM_STEP = 10

LANES = 128
SUBLANES = 8
PACK_COLS = 1024
VMEM_LIMIT = 56 * 1024 * 1024
NEG_BIG = -1e30
LOG2E = 1.4426950408889634
MESH = pl.DeviceIdType.MESH


def _cp():
    return pltpu.CompilerParams(vmem_limit_bytes=VMEM_LIMIT)


def _tile(n, cap, mult):
    best = None
    d = mult
    while d <= min(n, cap):
        if n % d == 0:
            best = d
        d += mult
    return n if best is None else best


def _row_tile(rows, cols):
    cap = max(16, (512 * 1024 // cols) // 16 * 16)
    return _tile(rows, cap, 16)


def _hbm():
    return pl.BlockSpec(memory_space=pltpu.HBM)


def _nt_norm_bwd(a3, b4, layer, x, w, dres, name):
    S, T, Ks = a3.shape
    D = x.shape[1]
    tm = _tile(T, 256, 16)

    def body(a_ref, b_ref, x_ref, w_ref, dres_ref, dx_ref, dw_ref):
        @pl.when(pl.program_id(0) == 0)
        def _():
            dw_ref[...] = jnp.zeros_like(dw_ref)

        dh = _nt(a_ref[0].astype(BF16), b_ref[0, 0])
        for s in range(1, S):
            dh = dh + _nt(a_ref[s].astype(BF16), b_ref[s, 0])
        xf = x_ref[...]
        r = lax.rsqrt(jnp.mean(xf * xf, axis=-1, keepdims=True) + NORM_EPS)
        xhat = xf * r
        dxhat = dh * w_ref[...]
        dx_ref[...] = dres_ref[...] + r * (dxhat - xhat * jnp.mean(dxhat * xhat, axis=-1, keepdims=True))
        dw_ref[...] += jnp.sum(dh * xhat, axis=0, keepdims=True)

    row = pl.BlockSpec((tm, D), lambda i: (i, 0))
    return pl.pallas_call(
        body, grid=(T // tm,),
        in_specs=[pl.BlockSpec((S, tm, Ks), lambda i: (0, i, 0)),
                  pl.BlockSpec((S, 1, D, Ks), lambda i: (0, layer, 0, 0)),
                  row, pl.BlockSpec((1, D), lambda i: (0, 0)), row],
        out_specs=[row, pl.BlockSpec((SUBLANES, D), lambda i: (0, 0))],
        out_shape=[jax.ShapeDtypeStruct((T, D), F32), jax.ShapeDtypeStruct((SUBLANES, D), F32)],
        name=name, compiler_params=_cp(),
    )(a3, b4, x, w.reshape(1, D), dres)


def _mm(a, b, mode, out_dtype, name, res=None):
    if mode == "tn":
        kt, M = a.shape
        N = b.shape[1]
        tm = _tile(M, 1408, LANES)
        tn = _tile(N, 1408, LANES)
        tk = _tile(kt, 1024, 16)

        def body(a_ref, b_ref, o_ref):
            @pl.when(pl.program_id(2) == 0)
            def _():
                o_ref[...] = jnp.zeros_like(o_ref)

            o_ref[...] += lax.dot_general(
                a_ref[...].astype(BF16), b_ref[...].astype(BF16), (((0,), (0,)), ((), ())),
                preferred_element_type=F32)

        return pl.pallas_call(
            body, grid=(M // tm, N // tn, kt // tk),
            in_specs=[pl.BlockSpec((tk, tm), lambda i, j, k: (k, i)),
                      pl.BlockSpec((tk, tn), lambda i, j, k: (k, j))],
            out_specs=pl.BlockSpec((tm, tn), lambda i, j, k: (i, j)),
            out_shape=jax.ShapeDtypeStruct((M, N), F32), name=name, compiler_params=_cp(),
        )(a, b)

    M, K = a.shape
    N = b.shape[1] if mode == "nn" else b.shape[0]
    tm = _tile(M, 512, 16)
    cap = min(3072, (6 << 20) // (2 * K), (4 << 20) // (tm * jnp.dtype(out_dtype).itemsize))
    tn = _tile(N, max(LANES, cap // LANES * LANES), LANES)
    dims = (((1,), (0,)), ((), ())) if mode == "nn" else (((1,), (1,)), ((), ()))

    def body(*refs):
        if res is None:
            a_ref, b_ref, o_ref = refs
        else:
            a_ref, b_ref, r_ref, o_ref = refs
        acc = lax.dot_general(a_ref[...].astype(BF16), b_ref[...].astype(BF16), dims,
                              preferred_element_type=F32)
        if res is not None:
            acc = acc + r_ref[...]
        o_ref[...] = acc.astype(out_dtype)

    b_spec = (pl.BlockSpec((K, tn), lambda j, i: (0, j)) if mode == "nn"
              else pl.BlockSpec((tn, K), lambda j, i: (j, 0)))
    in_specs = [pl.BlockSpec((tm, K), lambda j, i: (i, 0)), b_spec]
    args = [a, b]
    if res is not None:
        in_specs.append(pl.BlockSpec((tm, tn), lambda j, i: (i, j)))
        args.append(res)
    return pl.pallas_call(
        body, grid=(N // tn, M // tm), in_specs=in_specs,
        out_specs=pl.BlockSpec((tm, tn), lambda j, i: (i, j)),
        out_shape=jax.ShapeDtypeStruct((M, N), out_dtype), name=name, compiler_params=_cp(),
    )(*args)


def _normed(x_ref, w_ref):
    xf = x_ref[...]
    r = lax.rsqrt(jnp.mean(xf * xf, axis=-1, keepdims=True) + NORM_EPS)
    return (xf * r * w_ref[...]).astype(BF16)


def _ffn_in_act(x, norm_w, w5, layer, name):
    T, D = x.shape
    n = w5.shape[-1]
    tm = _tile(T, 256, 16)

    def body(x_ref, nw_ref, w_ref, h_ref, a_ref, s_ref):
        hv = _normed(x_ref, nw_ref)
        h_ref[...] = hv
        for half in range(2):
            g = jnp.dot(hv, w_ref[0, half, 0], preferred_element_type=F32)
            u = jnp.dot(hv, w_ref[1, half, 0], preferred_element_type=F32)
            a_ref[0, half] = g.astype(BF16)
            a_ref[1, half] = u.astype(BF16)
            s_ref[:, half * n:(half + 1) * n] = (g * jax.nn.sigmoid(g) * u).astype(BF16)

    return pl.pallas_call(
        body, grid=(T // tm,),
        in_specs=[pl.BlockSpec((tm, D), lambda i: (i, 0)), pl.BlockSpec((1, D), lambda i: (0, 0)),
                  pl.BlockSpec((2, 2, 1, D, n), lambda i: (0, 0, layer, 0, 0))],
        out_specs=[pl.BlockSpec((tm, D), lambda i: (i, 0)),
                   pl.BlockSpec((2, 2, tm, n), lambda i: (0, 0, i, 0)), pl.BlockSpec((tm, 2 * n), lambda i: (i, 0))],
        out_shape=[jax.ShapeDtypeStruct((T, D), BF16), jax.ShapeDtypeStruct((2, 2, T, n), BF16),
                   jax.ShapeDtypeStruct((T, 2 * n), BF16)],
        name=name, compiler_params=_cp(),
    )(x, norm_w.reshape(1, D), w5)


def _norm_mm(x, norm_w, b, splits, name):
    T, D = x.shape
    N = b.shape[1]
    assert sum(wd for wd, _ in splits) == N
    tm = _tile(T, 256, 16)

    def body(x_ref, nw_ref, b_ref, h_ref, *outs):
        hv = _normed(x_ref, nw_ref)
        h_ref[...] = hv
        off = 0
        for o_ref, (wd, dt) in zip(outs, splits):
            o_ref[...] = jnp.dot(hv, b_ref[:, off:off + wd], preferred_element_type=F32).astype(dt)
            off += wd

    return pl.pallas_call(
        body, grid=(T // tm,),
        in_specs=[pl.BlockSpec((tm, D), lambda i: (i, 0)), pl.BlockSpec((1, D), lambda i: (0, 0)),
                  pl.BlockSpec((D, N), lambda i: (0, 0))],
        out_specs=[pl.BlockSpec((tm, D), lambda i: (i, 0))] + [pl.BlockSpec((tm, wd), lambda i: (i, 0)) for wd, _ in splits],
        out_shape=[jax.ShapeDtypeStruct((T, D), BF16)] + [jax.ShapeDtypeStruct((T, wd), dt) for wd, dt in splits],
        name=name, compiler_params=_cp(),
    )(x, norm_w.reshape(1, D), b)


def _ffn_out_bwd_act(gx, w_out, a4, name):
    T, D = gx.shape
    n = a4.shape[-1]
    tm = _tile(T, 256, 16)

    step = 3 * LANES if n % LANES == 0 and n > 3 * LANES else n
    pieces = [(c, min(step, n - c)) for c in range(0, n, step)]

    def body(gx_ref, w_ref, a_ref, da_ref):
        gxb = gx_ref[...].astype(BF16)
        for c, wd in pieces:
            ds = _nt(gxb, w_ref[c:c + wd, :])
            g = a_ref[0, 0, :, c:c + wd].astype(F32)
            u = a_ref[1, 0, :, c:c + wd].astype(F32)
            sg = jax.nn.sigmoid(g)
            da_ref[0, 0, :, c:c + wd] = (ds * u * (sg * (1.0 + g * (1.0 - sg)))).astype(BF16)
            da_ref[1, 0, :, c:c + wd] = (ds * (g * sg)).astype(BF16)

    blk = pl.BlockSpec((2, 1, tm, n), lambda j, i: (0, j, i, 0))
    return pl.pallas_call(
        body, grid=(2, T // tm),
        in_specs=[pl.BlockSpec((tm, D), lambda j, i: (i, 0)), pl.BlockSpec((n, D), lambda j, i: (j, 0)), blk],
        out_specs=blk,
        out_shape=jax.ShapeDtypeStruct((2, 2, T, n), BF16), name=name, compiler_params=_cp(),
    )(gx, w_out, a4)


def _mm_tn_shards(h, a4, name):
    K, T, n = a4.shape
    D = h.shape[1]
    tk = _tile(T, 1024, 16)

    def body(h_ref, a_ref, o_ref):
        @pl.when(pl.program_id(1) == 0)
        def _():
            o_ref[...] = jnp.zeros_like(o_ref)

        o_ref[0] += lax.dot_general(h_ref[...], a_ref[0], (((0,), (0,)), ((), ())), preferred_element_type=F32)

    return pl.pallas_call(
        body, grid=(K, T // tk),
        in_specs=[pl.BlockSpec((tk, D), lambda k, t: (t, 0)), pl.BlockSpec((1, tk, n), lambda k, t: (k, t, 0))],
        out_specs=pl.BlockSpec((1, D, n), lambda k, t: (k, 0, 0)),
        out_shape=jax.ShapeDtypeStruct((K, D, n), F32), name=name, compiler_params=_cp(),
    )(h, a4)


def _loss_head(x, w, tgt, name):
    T, D = x.shape
    tm = _tile(T, 512, SUBLANES)

    def body(x_ref, w_ref, t_ref, dx_ref, loss_ref, dw_ref):
        @pl.when(pl.program_id(0) == 0)
        def _():
            loss_ref[...] = jnp.zeros_like(loss_ref)
            dw_ref[...] = jnp.zeros_like(dw_ref)

        xf = x_ref[...]
        wv = w_ref[...]
        r = lax.rsqrt(jnp.mean(xf * xf, axis=-1, keepdims=True) + NORM_EPS)
        xhat = xf * r
        err = xhat * wv - t_ref[...]
        per_tok = jnp.mean(err * err, axis=-1, keepdims=True)
        loss_ref[...] += 0.5 * jnp.sum(per_tok, axis=0, keepdims=True)
        dy = err * (1.0 / D)
        dxhat = dy * wv
        dx_ref[...] = r * (dxhat - xhat * jnp.mean(dxhat * xhat, axis=-1, keepdims=True))
        dw_ref[...] += jnp.sum(dy * xhat, axis=0, keepdims=True)

    row = pl.BlockSpec((tm, D), lambda i: (i, 0))
    return pl.pallas_call(
        body, grid=(T // tm,),
        in_specs=[row, pl.BlockSpec((1, D), lambda i: (0, 0)), row],
        out_specs=[row, pl.BlockSpec((SUBLANES, LANES), lambda i: (0, 0)),
                   pl.BlockSpec((SUBLANES, D), lambda i: (0, 0))],
        out_shape=[jax.ShapeDtypeStruct((T, D), F32), jax.ShapeDtypeStruct((SUBLANES, LANES), F32),
                   jax.ShapeDtypeStruct((SUBLANES, D), F32)],
        name=name, compiler_params=_cp(),
    )(x, w.reshape(1, D), tgt)


def _split3(v):
    hi = v.astype(BF16)
    r1 = v - hi.astype(F32)
    mid = r1.astype(BF16)
    lo = (r1 - mid.astype(F32)).astype(BF16)
    return hi, mid, lo


def _tri_dot(tri, v):
    out = None
    for piece in _split3(v):
        t = jnp.dot(tri, piece, preferred_element_type=F32)
        out = t if out is None else out + t
    return out


def _q_block(T):
    return _tile(T, 256, LANES)


def _gate_fwd(f, b_f, P, name):
    T = f.shape[0]
    tb = _q_block(T)

    def body(f_ref, b_ref, ct_ref, cc_ref, c0_ref, carry):
        @pl.when(pl.program_id(0) == 0)
        def _():
            carry[...] = jnp.zeros_like(carry)

        z = f_ref[...] + b_ref[...]
        logf = jnp.minimum(z, 0.0) - jnp.log(1.0 + jnp.exp(-jnp.abs(z)))
        row = lax.broadcasted_iota(jnp.int32, (tb, tb), 0)
        col = lax.broadcasted_iota(jnp.int32, (tb, tb), 1)
        tri = (col <= row).astype(BF16)
        c = _tri_dot(tri, logf) + carry[0:1, :]
        carry[...] = jnp.broadcast_to(c[tb - 1:tb, :], carry.shape)
        first = jnp.broadcast_to(c[0:1, :], c.shape)
        for p in range(P):
            shifted = c if p == 0 else pltpu.roll(c, LANES - 2 * p, 1)
            cc_ref[p] = shifted
            ct_ref[p] = shifted.T[0:SUBLANES, :]
            c0_ref[p] = (first if p == 0 else pltpu.roll(first, LANES - 2 * p, 1)).T[0:SUBLANES, :]

    rows = pl.BlockSpec((P, SUBLANES, tb), lambda i: (0, 0, i))
    return pl.pallas_call(
        body, grid=(T // tb,),
        in_specs=[pl.BlockSpec((tb, LANES), lambda i: (i, 0)), pl.BlockSpec((1, LANES), lambda i: (0, 0))],
        out_specs=[rows, pl.BlockSpec((P, tb, LANES), lambda i: (0, i, 0)), rows],
        out_shape=[jax.ShapeDtypeStruct((P, SUBLANES, T), F32), jax.ShapeDtypeStruct((P, T, LANES), F32),
                   jax.ShapeDtypeStruct((P, SUBLANES, T), F32)],
        scratch_shapes=[pltpu.VMEM((SUBLANES, LANES), F32)],
        name=name, compiler_params=_cp(),
    )(f, b_f)


def _gate_bwd(dc_cols, drowT, f, b_f, P, name):
    T = f.shape[0]
    tb = _tile(T, 256, LANES)
    nb = T // tb

    def body(dc_ref, dr_ref, f_ref, b_ref, df_ref, db_ref, carry):
        @pl.when(pl.program_id(0) == 0)
        def _():
            carry[...] = jnp.zeros_like(carry)
            db_ref[...] = jnp.zeros_like(db_ref)

        lane = lax.broadcasted_iota(jnp.int32, (tb, LANES), 1)
        dc = jnp.zeros((tb, LANES), F32)
        for p in range(P):
            rows = jnp.concatenate([dr_ref[p], jnp.zeros((LANES - SUBLANES, tb), F32)], axis=0)
            part = jnp.where(lane < 2, dc_ref[p] + rows.T, 0.0)
            dc = dc + (part if p == 0 else pltpu.roll(part, 2 * p, 1))
        row = lax.broadcasted_iota(jnp.int32, (tb, tb), 0)
        col = lax.broadcasted_iota(jnp.int32, (tb, tb), 1)
        tri = (col >= row).astype(BF16)
        dlogf = _tri_dot(tri, dc) + carry[0:1, :]
        carry[...] = jnp.broadcast_to(dlogf[0:1, :], carry.shape)
        z = f_ref[...] + b_ref[...]
        df = jnp.where(lane < 2 * P, dlogf * jax.nn.sigmoid(-z), 0.0)
        df_ref[...] = df
        db_ref[...] += jnp.sum(df, axis=0, keepdims=True)

    return pl.pallas_call(
        body, grid=(nb,),
        in_specs=[pl.BlockSpec((P, tb, LANES), lambda i: (0, nb - 1 - i, 0)),
                  pl.BlockSpec((P, SUBLANES, tb), lambda i: (0, 0, nb - 1 - i)),
                  pl.BlockSpec((tb, LANES), lambda i: (nb - 1 - i, 0)),
                  pl.BlockSpec((1, LANES), lambda i: (0, 0))],
        out_specs=[pl.BlockSpec((tb, LANES), lambda i: (nb - 1 - i, 0)),
                   pl.BlockSpec((SUBLANES, LANES), lambda i: (0, 0))],
        out_shape=[jax.ShapeDtypeStruct((T, LANES), F32), jax.ShapeDtypeStruct((SUBLANES, LANES), F32)],
        scratch_shapes=[pltpu.VMEM((SUBLANES, LANES), F32)],
        name=name, compiler_params=_cp(),
    )(dc_cols, drowT, f, b_f)


def _nt(a, b):
    return lax.dot_general(a, b, (((1,), (1,)), ((), ())), preferred_element_type=F32)


def _attn_fwd(qkv, cT, P, scale, name, riders=()):
    T = qkv.shape[0]
    n_r = len(riders)
    tq = _q_block(T)
    tw = _tile(T, 8 * tq, 2 * tq)
    cw = tw // 2
    assert cw % tq == 0, "the sequence must split into chunks of whole query blocks"
    nq = T // tq

    def body(q_ref, k_ref, v_ref, c_ref, *rest):
        w_refs, (o_ref, lse_ref), ow_refs = rest[:n_r], rest[n_r:n_r + 2], rest[n_r + 2:2 * n_r + 2]
        s_scr = rest[2 * n_r + 2]
        i = pl.program_id(1)
        if n_r:
            send_sems, recv_sems = rest[2 * n_r + 3:]
            first = (pl.program_id(0) == 0) & (i == 0)
            final = (pl.program_id(0) == P - 1) & (i == nq - 1)

            @pl.when(first)
            def _():
                for t in range(n_r):
                    for cp in _shard_half_copies(w_refs[t], ow_refs[t], send_sems, recv_sems, 3 * t, False):
                        cp.start()

            @pl.when(final)
            def _():
                for t in range(n_r):
                    for cp in _shard_half_copies(w_refs[t], ow_refs[t], send_sems, recv_sems, 3 * t, True):
                        cp.wait_recv()
                    for cp in _shard_half_copies(w_refs[t], ow_refs[t], send_sems, recv_sems, 3 * t, False):
                        cp.wait_send()

        lane = lax.broadcasted_iota(jnp.int32, (1, LANES), 1)
        q = (q_ref[...].astype(F32) * (scale * LOG2E)).astype(BF16)
        q_heads = (jnp.where(lane < 64, q, jnp.zeros_like(q)), jnp.where(lane >= 64, q, jnp.zeros_like(q)))
        c0 = c_ref[0, :, pl.ds(pl.multiple_of(i * tq, tq), LANES)][:, 0:1]

        def scores(start, width, a):
            bias = (c0 - c_ref[0, :, pl.ds(start, width)]) * LOG2E
            return _nt(q_heads[a], k_ref[pl.ds(start, width), :]) + bias[a:a + 1, :]

        def softmax_pv(start, width, s_of, carry):
            v = v_ref[pl.ds(start, width), :]
            one = jnp.ones_like(v)
            v_heads = (jnp.where(lane < 64, v, one), jnp.where(lane >= 64, v, one))
            new = []
            for a in range(2):
                m, acc = carry[a]
                s = s_of(a)
                m_new = jnp.maximum(m, jnp.max(s, axis=1, keepdims=True))
                p = jnp.exp2(s - m_new)
                acc = jnp.exp2(m - m_new) * acc + jnp.dot(p.astype(BF16), v_heads[a], preferred_element_type=F32)
                new.append((m_new, acc))
            return tuple(new)

        def fill(start, buf):
            for a in range(2):
                s_scr[2 * buf + a] = scores(start, cw, a)

        def wide(j, carry):
            base = pl.multiple_of(j * tw, tw)
            fill(base + cw, 1)
            carry = softmax_pv(base, cw, lambda a: s_scr[a], carry)
            fill(base + tw, 0)
            return softmax_pv(base + cw, cw, lambda a: s_scr[2 + a], carry)

        init = tuple((jnp.full((tq, 1), NEG_BIG, F32), jnp.zeros((tq, LANES), F32)) for _ in range(2))
        n_wide = (i * tq) // tw
        fill(0, 0)
        carry = lax.fori_loop(0, n_wide, wide, init)

        base = pl.multiple_of(n_wide * tw, tw)
        ahead = i * tq - base
        col_minus_row = (lax.broadcasted_iota(jnp.int32, (tq, cw), 1)
                         - lax.broadcasted_iota(jnp.int32, (tq, cw), 0))

        def causal(buf, first_key):
            return lambda a: jnp.where(col_minus_row <= ahead - first_key, s_scr[2 * buf + a], NEG_BIG)

        def one_chunk(cr):
            return softmax_pv(base, cw, causal(0, 0), cr)

        def two_chunks(cr):
            fill(base + cw, 1)
            cr = softmax_pv(base, cw, causal(0, 0), cr)
            return softmax_pv(base + cw, cw, causal(1, cw), cr)

        (m0, a0), (m1, a1) = lax.cond(ahead >= cw, two_chunks, one_chunk, carry)
        sums = jnp.where(lane < 64, pltpu.roll(a0, 64, 1), pltpu.roll(a1, 64, 1))
        o_ref[...] = (jnp.where(lane < 64, a0, a1) / sums).astype(BF16)
        l0, l1 = a0[:, 64:65], a1[:, 0:1]
        lse = jnp.where(lane == 0, m0 + jnp.log2(l0), jnp.where(lane == 1, m1 + jnp.log2(l1), 0.0))
        lse_ref[0] = lse.T[0:SUBLANES, :]

    return pl.pallas_call(
        body, grid=(P, nq),
        in_specs=[pl.BlockSpec((tq, LANES), lambda p, i: (i, p)),
                  pl.BlockSpec((T, LANES), lambda p, i: (0, P + p)),
                  pl.BlockSpec((T, LANES), lambda p, i: (0, 2 * P + p)),
                  pl.BlockSpec((1, SUBLANES, T), lambda p, i: (p, 0, 0))] + [_hbm()] * n_r,
        out_specs=[pl.BlockSpec((tq, LANES), lambda p, i: (i, p)),
                   pl.BlockSpec((1, SUBLANES, tq), lambda p, i: (p, 0, i))] + [_hbm()] * n_r,
        out_shape=[jax.ShapeDtypeStruct((T, LANES * P), BF16), jax.ShapeDtypeStruct((P, SUBLANES, T), F32)]
        + [jax.ShapeDtypeStruct((4,) + w.shape, w.dtype) for w in riders],
        scratch_shapes=[pltpu.VMEM((4, tq, cw), F32)]
        + ([pltpu.SemaphoreType.DMA((3 * n_r,)), pltpu.SemaphoreType.DMA((3 * n_r,))] if n_r else []),
        name=name, compiler_params=_cp(),
    )(qkv, qkv, qkv, cT, *riders)


def _attn_delta(do, o, P, name):
    T, D = o.shape
    tb = _tile(T, 256, LANES)

    def body(do_ref, o_ref, d_ref):
        lane = lax.broadcasted_iota(jnp.int32, (1, LANES), 1)
        for p in range(P):
            cols = slice(p * LANES, (p + 1) * LANES)
            prod = do_ref[:, cols].astype(F32) * o_ref[:, cols].astype(F32)
            d0 = jnp.sum(jnp.where(lane < 64, prod, 0.0), axis=1, keepdims=True)
            d1 = jnp.sum(jnp.where(lane >= 64, prod, 0.0), axis=1, keepdims=True)
            both = jnp.where(lane == 0, d0, jnp.where(lane == 1, d1, 0.0))
            d_ref[p] = both.T[0:SUBLANES, :]

    return pl.pallas_call(
        body, grid=(T // tb,),
        in_specs=[pl.BlockSpec((tb, D), lambda i: (i, 0)), pl.BlockSpec((tb, D), lambda i: (i, 0))],
        out_specs=pl.BlockSpec((P, SUBLANES, tb), lambda i: (0, 0, i)),
        out_shape=jax.ShapeDtypeStruct((P, SUBLANES, T), F32), name=name, compiler_params=_cp(),
    )(do, o)


def _attn_bwd(qkv, do, lseT, dT, c0T, c_cols, P, scale, name, riders=(), gather_riders=()):
    T = qkv.shape[0]
    n_x, n_g = len(riders), len(gather_riders)
    n_r = n_x + n_g
    tq = _q_block(T)
    tw = _tile(T, 4 * tq, 2 * tq)
    cw = tw // 2
    assert cw % tq == 0, "the sequence must split into chunks of whole query blocks"
    nq = T // tq

    def body(q_ref, do_ref, k_ref, v_ref, lse_ref, d_ref, c0_ref, cc_ref, *rest):
        p_refs, (dq_ref, dk_ref, dv_ref, dc_ref, drow_ref) = rest[:n_r], rest[n_r:n_r + 5]
        o_refs = rest[n_r + 5:2 * n_r + 5]
        dq_acc0, dq_acc1, s_scr = rest[2 * n_r + 5:2 * n_r + 8]
        j = pl.program_id(1)
        if n_r:
            local_sems, send_sems, recv_sems = rest[2 * n_r + 8:]

            def exchange(t, incoming):
                if t < n_x:
                    return _chip_exchange_copies(p_refs[t], o_refs[t], local_sems.at[t], send_sems, recv_sems,
                                                 3 * t, incoming)
                return _gather_all_copies(p_refs[t], o_refs[t], local_sems.at[t], send_sems, recv_sems,
                                          3 * n_x + 7 * (t - n_x), incoming)

            @pl.when((pl.program_id(0) == 0) & (j == 0))
            def _():
                for t in range(n_r):
                    own, sent = exchange(t, False)
                    own.start()
                    for cp in sent:
                        cp.start()

            @pl.when((pl.program_id(0) == P - 1) & (j == nq - 1))
            def _():
                for t in range(n_r):
                    for cp in exchange(t, True)[1]:
                        cp.wait_recv()
                    own, sent = exchange(t, False)
                    for cp in sent:
                        cp.wait_send()
                    own.wait()


        @pl.when(j == 0)
        def _():
            dq_acc0[...] = jnp.zeros_like(dq_acc0)
            dq_acc1[...] = jnp.zeros_like(dq_acc1)

        lane = lax.broadcasted_iota(jnp.int32, (1, LANES), 1)
        in_head = (lane < 64, lane >= 64)
        k = k_ref[...]
        v = v_ref[...]
        zero = jnp.zeros_like(k)
        one = jnp.ones_like(k)
        k_heads = tuple(jnp.where(h, k, zero) for h in in_head)
        v_heads = tuple(jnp.where(h, v, zero) for h in in_head)
        k_ones = tuple(jnp.where(h, k, one) for h in in_head)
        cc = cc_ref[0]
        c_first = (cc[0:1, 0:1], cc[0:1, 1:2])
        c_rel = ((cc[:, 0:1] - c_first[0]) * LOG2E, (cc[:, 1:2] - c_first[1]) * LOG2E)
        dq_accs = (dq_acc0, dq_acc1)

        def scaled_q(start, width, factor):
            return (q_ref[pl.ds(start, width), :].astype(F32) * factor).astype(BF16)

        def block(start, width, carry, first_query=None, scores=None):
            q = scaled_q(start, width, scale)
            q_one = jnp.ones_like(q)
            dov = do_ref[pl.ds(start, width), :]
            lse = lse_ref[0, :, pl.ds(start, width)]
            dlt = d_ref[0, :, pl.ds(start, width)]
            c0 = c0_ref[0, :, pl.ds(start, width)]
            new = []
            for a in range(2):
                dk_a, dv_a = carry[a]
                rowv = lse[a:a + 1, :] + (c_first[a] - c0[a:a + 1, :]) * LOG2E
                if scores is None:
                    st = _nt(k_heads[a], scaled_q(start, width, scale * LOG2E))
                else:
                    st = scores(a)
                pt = jnp.exp2((st - c_rel[a]) - rowv)
                if first_query is not None:
                    row = lax.broadcasted_iota(jnp.int32, (tq, width), 0)
                    col = lax.broadcasted_iota(jnp.int32, (tq, width), 1)
                    pt = jnp.where(col - row >= first_query, pt, 0.0)
                dpt = _nt(v_heads[a], dov)
                dst_b = (pt * (dpt - dlt[a:a + 1, :])).astype(BF16)
                dv_a = dv_a + jnp.dot(pt.astype(BF16), dov, preferred_element_type=F32)
                dk_a = dk_a + jnp.dot(dst_b, jnp.where(in_head[a], q, q_one), preferred_element_type=F32)
                dq_accs[a][pl.ds(start, width), :] += lax.dot_general(
                    dst_b, k_ones[a], (((0,), (0,)), ((), ())), preferred_element_type=F32)
                new.append((dk_a, dv_a))
            return tuple(new)

        first_key = j * tq
        first_wide = first_key // tw + 1
        last = T // tw - 1

        def fill(trip, buf):
            q = scaled_q(pl.multiple_of(jnp.minimum(trip, last) * tw, tw), tw, scale * LOG2E)
            for a in range(2):
                s_scr[2 * buf + a] = _nt(k_heads[a], q)

        def trip(i, buf, cr):
            return block(pl.multiple_of(i * tw, tw), tw, cr, scores=lambda a: s_scr[2 * buf + a])

        def two_trips(p, cr):
            i = first_wide + 2 * p
            fill(i + 1, 1)
            cr = trip(i, 0, cr)
            fill(i + 2, 0)
            return trip(i + 1, 1, cr)

        init = tuple((jnp.zeros((tq, LANES), F32), jnp.zeros((tq, LANES), F32)) for _ in range(2))
        fill(first_wide, 0)
        diag = pl.multiple_of((first_key // cw) * cw, cw)
        carry = block(diag, cw, init, first_key - diag)
        carry = lax.cond(
            diag + cw < first_wide * tw,
            lambda cr: block(pl.multiple_of(diag + cw, cw), cw, cr), lambda cr: cr, carry)
        n_trips = last + 1 - first_wide
        carry = lax.fori_loop(0, n_trips // 2, two_trips, carry)
        (dk0, dv0), (dk1, dv1) = lax.cond(n_trips % 2 == 1, lambda cr: trip(last, 0, cr), lambda cr: cr, carry)
        dk_ref[...] = jnp.where(lane < 64, dk0, dk1).astype(BF16)
        dv_ref[...] = jnp.where(lane < 64, dv0, dv1).astype(BF16)
        dc_ref[0] = jnp.where(lane == 0, -dk0[:, 64:65], jnp.where(lane == 1, -dk1[:, 0:1], 0.0))

        @pl.when(j == nq - 1)
        def _():
            def finish(i, _):
                rows = pl.ds(pl.multiple_of(i * tq, tq), tq)
                a0 = dq_acc0[rows, :]
                a1 = dq_acc1[rows, :]
                dq_ref[rows, :] = (jnp.where(lane < 64, a0, a1) * scale).astype(BF16)
                sums = jnp.where(lane == 0, a0[:, 64:65], jnp.where(lane == 1, a1[:, 0:1], 0.0))
                drow_ref[0, :, rows] = sums.T[0:SUBLANES, :]
                return 0

            lax.fori_loop(0, nq, finish, 0)

    full = lambda col: pl.BlockSpec((T, LANES), lambda p, j: (0, col(p)))
    blk = lambda col: pl.BlockSpec((tq, LANES), lambda p, j: (j, col(p)))
    rows = pl.BlockSpec((1, SUBLANES, T), lambda p, j: (p, 0, 0))
    cols = pl.BlockSpec((1, tq, LANES), lambda p, j: (p, j, 0))
    D = LANES * P
    return pl.pallas_call(
        body, grid=(P, nq),
        in_specs=[full(lambda p: p), full(lambda p: p), blk(lambda p: P + p), blk(lambda p: 2 * P + p),
                  rows, rows, rows, cols] + [_hbm()] * n_r,
        out_specs=[full(lambda p: p), blk(lambda p: p), blk(lambda p: p), cols, rows] + [_hbm()] * n_r,
        out_shape=[jax.ShapeDtypeStruct((T, D), BF16), jax.ShapeDtypeStruct((T, D), BF16),
                   jax.ShapeDtypeStruct((T, D), BF16), jax.ShapeDtypeStruct((P, T, LANES), F32),
                   jax.ShapeDtypeStruct((P, SUBLANES, T), F32)]
        + [jax.ShapeDtypeStruct(r.shape, r.dtype) for r in riders]
        + [jax.ShapeDtypeStruct((8,) + g.shape, g.dtype) for g in gather_riders],
        scratch_shapes=[pltpu.VMEM((T, LANES), F32), pltpu.VMEM((T, LANES), F32), pltpu.VMEM((4, tq, tw), F32)]
        + ([pltpu.SemaphoreType.DMA((n_r,)), pltpu.SemaphoreType.DMA((3 * n_x + 7 * n_g,)),
            pltpu.SemaphoreType.DMA((3 * n_x + 7 * n_g,))] if n_r else []),
        name=name, compiler_params=_cp(),
    )(qkv, do, qkv, qkv, lseT, dT, c0T, c_cols, *riders, *gather_riders)


_SQRT_HALF = 0.7071067811865476
_INV_SQRT_2PI = 0.3989422804014327


def _gelu(v):
    return 0.5 * v * (1.0 + lax.erf(v * _SQRT_HALF))


def _gelu_and_grad(v):
    cdf = 0.5 * (1.0 + lax.erf(v * _SQRT_HALF))
    return v * cdf, cdf + v * (_INV_SQRT_2PI * jnp.exp(-0.5 * v * v))


def _sgu_fwd(a, ln_g, ln_b, w_tril, bias, name):
    T, W2 = a.shape
    W = W2 // 2
    G = w_tril.shape[0]
    tb = _tile(T, 256, LANES)

    def body(a_ref, g_ref, b_ref, w_ref, bias_ref, out_ref):
        zu = _gelu(a_ref[:, :W].astype(F32))
        zv = _gelu(a_ref[:, W:].astype(F32))
        mu = jnp.mean(zv, axis=-1, keepdims=True)
        d = zv - mu
        rstd = lax.rsqrt(jnp.mean(d * d, axis=-1, keepdims=True) + LN_EPS)
        vn = (d * rstd * g_ref[...] + b_ref[...]).astype(BF16)
        for c in range(tb // LANES):
            rs = slice(c * LANES, (c + 1) * LANES)
            for g in range(G):
                cs = slice(g * LANES, (g + 1) * LANES)
                mixed = jnp.dot(w_ref[g], vn[rs, cs], preferred_element_type=F32) + bias_ref[:, cs]
                out_ref[rs, cs] = (zu[rs, cs] * mixed).astype(BF16)

    return pl.pallas_call(
        body, grid=(T // tb,),
        in_specs=[pl.BlockSpec((tb, W2), lambda i: (i, 0)), pl.BlockSpec((1, W), lambda i: (0, 0)),
                  pl.BlockSpec((1, W), lambda i: (0, 0)), pl.BlockSpec((G, LANES, LANES), lambda i: (0, 0, 0)),
                  pl.BlockSpec((LANES, W), lambda i: (0, 0))],
        out_specs=pl.BlockSpec((tb, W), lambda i: (i, 0)),
        out_shape=jax.ShapeDtypeStruct((T, W), BF16), name=name, compiler_params=_cp(),
    )(a, ln_g.reshape(1, W), ln_b.reshape(1, W), w_tril, bias)


def _sgu_bwd(a, dgated, ln_g, ln_b, w_tril, w_tril_t, bias, name):
    T, W2 = a.shape
    W = W2 // 2
    G = w_tril.shape[0]
    tb = _tile(T, 256, LANES)

    def body(a_ref, dg_ref, g_ref, b_ref, w_ref, wt_ref, bias_ref,
             da_ref, dws_ref, dbias_ref, dlng_ref, dlnb_ref, dvn_ref):
        @pl.when(pl.program_id(0) == 0)
        def _():
            dws_ref[...] = jnp.zeros_like(dws_ref)
            dbias_ref[...] = jnp.zeros_like(dbias_ref)
            dlng_ref[...] = jnp.zeros_like(dlng_ref)
            dlnb_ref[...] = jnp.zeros_like(dlnb_ref)

        up = a_ref[:, :W].astype(F32)
        vp = a_ref[:, W:].astype(F32)
        zu, gu = _gelu_and_grad(up)
        zv, gv = _gelu_and_grad(vp)
        mu = jnp.mean(zv, axis=-1, keepdims=True)
        d = zv - mu
        rstd = lax.rsqrt(jnp.mean(d * d, axis=-1, keepdims=True) + LN_EPS)
        vhat = d * rstd
        gam = g_ref[...]
        vn = (vhat * gam + b_ref[...]).astype(BF16)
        dgated = dg_ref[...]
        for c in range(tb // LANES):
            rs = slice(c * LANES, (c + 1) * LANES)
            for g in range(G):
                cs = slice(g * LANES, (g + 1) * LANES)
                vb = vn[rs, cs]
                mixed = jnp.dot(w_ref[g], vb, preferred_element_type=F32) + bias_ref[:, cs]
                dgt = dgated[rs, cs]
                da_ref[rs, cs] = (dgt * mixed * gu[rs, cs]).astype(BF16)
                dmx = dgt * zu[rs, cs]
                dbias_ref[:, cs] += dmx
                dmb = dmx.astype(BF16)
                dws_ref[g] += _nt(dmb, vb)
                dvn_ref[rs, cs] = jnp.dot(wt_ref[g], dmb, preferred_element_type=F32)
        dvn = dvn_ref[...]
        dlng_ref[...] += jnp.sum(dvn * vhat, axis=0, keepdims=True)
        dlnb_ref[...] += jnp.sum(dvn, axis=0, keepdims=True)
        dvh = dvn * gam
        dzv = rstd * (dvh - jnp.mean(dvh, axis=-1, keepdims=True)
                      - vhat * jnp.mean(dvh * vhat, axis=-1, keepdims=True))
        da_ref[:, W:] = (dzv * gv).astype(BF16)

    const2 = lambda shape: pl.BlockSpec(shape, lambda i: (0, 0))
    const3 = pl.BlockSpec((G, LANES, LANES), lambda i: (0, 0, 0))
    return pl.pallas_call(
        body, grid=(T // tb,),
        in_specs=[pl.BlockSpec((tb, W2), lambda i: (i, 0)), pl.BlockSpec((tb, W), lambda i: (i, 0)),
                  const2((1, W)), const2((1, W)), const3, const3, const2((LANES, W))],
        out_specs=[pl.BlockSpec((tb, W2), lambda i: (i, 0)), const3, const2((LANES, W)),
                   const2((SUBLANES, W)), const2((SUBLANES, W))],
        out_shape=[jax.ShapeDtypeStruct((T, W2), BF16), jax.ShapeDtypeStruct((G, LANES, LANES), F32),
                   jax.ShapeDtypeStruct((LANES, W), F32), jax.ShapeDtypeStruct((SUBLANES, W), F32),
                   jax.ShapeDtypeStruct((SUBLANES, W), F32)],
        scratch_shapes=[pltpu.VMEM((tb, W), F32)],
        name=name, compiler_params=_cp(),
    )(a, dgated, ln_g.reshape(1, W), ln_b.reshape(1, W), w_tril, w_tril_t, bias)


def _adam_math(w, g, m, v):
    m = ADAM_B1 * m + (1.0 - ADAM_B1) * g
    v = ADAM_B2 * v + (1.0 - ADAM_B2) * (g * g)
    m_hat = m / (1.0 - ADAM_B1 ** ADAM_STEP)
    v_hat = v / (1.0 - ADAM_B2 ** ADAM_STEP)
    delta = -ADAM_LR * (m_hat / (jnp.sqrt(v_hat) + ADAM_EPS) + ADAM_WD * w)
    return delta, m, v


def _adamw_halves(mine, theirs, c_idx, w, m, v, name):
    R, C = w.shape
    rh = R // 2
    tb = _row_tile(rh, C)
    nb = rh // tb

    def body(c_ref, a_ref, b_ref, w_ref, m_ref, v_ref, g_ref, d_ref, mo_ref, vo_ref):
        g = jnp.where(pl.program_id(0) == c_ref[0], a_ref[...], b_ref[...])
        d, mm, vv = _adam_math(w_ref[...], g, m_ref[...], v_ref[...])
        g_ref[...] = g
        d_ref[...] = d
        mo_ref[...] = mm
        vo_ref[...] = vv

    half = pl.BlockSpec((tb, C), lambda h, i, c: (i, 0))
    row = pl.BlockSpec((tb, C), lambda h, i, c: (h * nb + i, 0))
    sds = jax.ShapeDtypeStruct((R, C), F32)
    return pl.pallas_call(
        body,
        grid_spec=pltpu.PrefetchScalarGridSpec(
            num_scalar_prefetch=1, grid=(2, nb), in_specs=[half, half, row, row, row], out_specs=[row] * 4),
        out_shape=[sds] * 4, name=name, compiler_params=_cp())(c_idx, mine, theirs, w, m, v)


def _adamw_sum(parts, w, m, v, name):
    K, R, C = parts.shape
    tb = _tile(R, 128, SUBLANES)

    def body(p_ref, w_ref, m_ref, v_ref, g_ref, d_ref, mo_ref, vo_ref):
        g = p_ref[0]
        for k in range(1, K):
            g = g + p_ref[k]
        d, mm, vv = _adam_math(w_ref[...], g, m_ref[...], v_ref[...])
        g_ref[...] = g
        d_ref[...] = d
        mo_ref[...] = mm
        vo_ref[...] = vv

    row = pl.BlockSpec((tb, C), lambda i: (i, 0))
    sds = jax.ShapeDtypeStruct((R, C), F32)
    return pl.pallas_call(
        body, grid=(R // tb,),
        in_specs=[pl.BlockSpec((K, tb, C), lambda i: (0, i, 0)), row, row, row],
        out_specs=[row] * 4, out_shape=[sds] * 4, name=name, compiler_params=_cp())(parts, w, m, v)


def _pair_sum(g_all, recv, c_idx, name):
    K, R, C = g_all.shape
    rh = R // 2
    tb = _row_tile(rh, C)
    nb = rh // tb

    def body(c_ref, a_ref, b_ref, o_ref):
        o_ref[...] = (a_ref[...] + b_ref[...]).astype(BF16)

    return pl.pallas_call(
        body,
        grid_spec=pltpu.PrefetchScalarGridSpec(
            num_scalar_prefetch=1, grid=(K, nb),
            in_specs=[pl.BlockSpec((1, tb, C), lambda k, i, c: (k, c[0] * nb + i, 0)),
                      pl.BlockSpec((1, tb, C), lambda k, i, c: (k, i, 0))],
            out_specs=pl.BlockSpec((1, tb, C), lambda k, i, c: (k, i, 0))),
        out_shape=jax.ShapeDtypeStruct((K, rh, C), BF16), name=name, compiler_params=_cp(),
    )(c_idx, g_all, recv)


def _sum_parts(parts, name):
    K, R, C = parts.shape
    tb = _row_tile(R, C)

    def body(p_ref, o_ref):
        g = p_ref[0].astype(F32)
        for k in range(1, K):
            g = g + p_ref[k].astype(F32)
        o_ref[...] = g

    return pl.pallas_call(
        body, grid=(R // tb,), in_specs=[pl.BlockSpec((K, tb, C), lambda i: (0, i, 0))],
        out_specs=pl.BlockSpec((tb, C), lambda i: (i, 0)),
        out_shape=jax.ShapeDtypeStruct((R, C), F32), name=name, compiler_params=_cp())(parts)


_CHIP_RELATIONS = ((1, 0), (0, 1), (1, 1))


def _position():
    return lax.axis_index("x"), lax.axis_index("y"), lax.axis_index("c")


def _flip(v, bit):
    return 1 - v if bit else v


def _shard_half_copies(w_ref, ow_ref, send_sems, recv_sems, sem0, incoming):
    x, y, c = _position()
    rh = w_ref.shape[0] // 2
    rows = pl.ds(pl.multiple_of(c * rh, 16), rh)
    out = []
    for r, (dx, dy) in enumerate(_CHIP_RELATIONS):
        px, py = _flip(x, dx), _flip(y, dy)
        slot = 2 * px + py if incoming else 2 * x + y
        out.append(pltpu.make_async_remote_copy(
            src_ref=w_ref.at[rows, :], dst_ref=ow_ref.at[slot, rows, :], send_sem=send_sems.at[sem0 + r],
            recv_sem=recv_sems.at[sem0 + r], device_id=(px, py, c), device_id_type=MESH))
    return out


def _gather_weights(w_pack, side, name):
    n_side = 0 if side is None else 1

    def between_chips(*refs):
        if n_side:
            w_ref, s_ref, ow_ref, os_ref, local_sem, send_sems, recv_sems = refs
        else:
            w_ref, ow_ref, send_sems, recv_sems = refs
        x, y, c = _position()
        me = 2 * x + y
        if n_side:
            own_side = pltpu.make_async_copy(s_ref, os_ref.at[me], local_sem)
            own_side.start()

        def side_copies(incoming):
            out = []
            for r, (dx, dy) in enumerate(_CHIP_RELATIONS):
                px, py = _flip(x, dx), _flip(y, dy)
                out.append(pltpu.make_async_remote_copy(
                    src_ref=s_ref, dst_ref=os_ref.at[2 * px + py if incoming else me],
                    send_sem=send_sems.at[3 + r], recv_sem=recv_sems.at[3 + r],
                    device_id=(px, py, c), device_id_type=MESH))
            return out

        sent = _shard_half_copies(w_ref, ow_ref, send_sems, recv_sems, 0, False) + (side_copies(False) if n_side else [])
        for cp in sent:
            cp.start()
        for cp in _shard_half_copies(w_ref, ow_ref, send_sems, recv_sems, 0, True) + (side_copies(True) if n_side else []):
            cp.wait_recv()
        for cp in sent:
            cp.wait_send()
        if n_side:
            own_side.wait()

    sems = [pltpu.SemaphoreType.DMA((6,)), pltpu.SemaphoreType.DMA((6,))]
    gathered = jax.ShapeDtypeStruct((4,) + w_pack.shape, w_pack.dtype)
    if n_side:
        halves, sides = pl.pallas_call(
            between_chips, in_specs=[_hbm(), _hbm()], out_specs=[_hbm(), _hbm()],
            out_shape=[gathered, jax.ShapeDtypeStruct((4,) + side.shape, side.dtype)],
            scratch_shapes=[pltpu.SemaphoreType.DMA(())] + sems,
            name=name + "_ici", compiler_params=_cp(),
        )(w_pack, side)
    else:
        sides = None
        halves = pl.pallas_call(
            between_chips, in_specs=[_hbm()], out_specs=_hbm(), out_shape=gathered, scratch_shapes=sems,
            name=name + "_ici", compiler_params=_cp(),
        )(w_pack)
    return _hand_to_sibling(halves, name + "_d2d"), sides


def _hand_to_sibling(halves, name):
    _, R, C = halves.shape
    rh = R // 2

    def to_sibling(g_ref, o_ref, send_sems, recv_sems):
        x, y, c = _position()

        def copy(r, cc):
            dx, dy = _CHIP_RELATIONS[r]
            slot = 2 * _flip(x, dx) + _flip(y, dy)
            rows = pl.ds(pl.multiple_of(cc * rh, 16), rh)
            return pltpu.make_async_remote_copy(
                src_ref=g_ref.at[slot, rows, :], dst_ref=o_ref.at[slot, rows, :], send_sem=send_sems.at[r],
                recv_sem=recv_sems.at[r], device_id=(x, y, 1 - c), device_id_type=MESH)

        sent = [copy(r, c) for r in range(3)]
        for cp in sent:
            cp.start()
        for r in range(3):
            copy(r, 1 - c).wait_recv()
        for cp in sent:
            cp.wait_send()

    return pl.pallas_call(
        to_sibling, in_specs=[_hbm()], out_specs=_hbm(), input_output_aliases={0: 0},
        out_shape=jax.ShapeDtypeStruct(halves.shape, halves.dtype),
        scratch_shapes=[pltpu.SemaphoreType.DMA((3,)), pltpu.SemaphoreType.DMA((3,))],
        name=name, compiler_params=_cp(),
    )(halves)


def _sibling_halves(g_all, name):
    K, R, C = g_all.shape
    rh = R // 2

    def body(g_ref, o_ref, send_sem, recv_sem):
        x, y, c = _position()
        start = pl.multiple_of((1 - c) * rh, SUBLANES)
        cp = pltpu.make_async_remote_copy(
            src_ref=g_ref.at[:, pl.ds(start, rh), :], dst_ref=o_ref, send_sem=send_sem, recv_sem=recv_sem,
            device_id=(x, y, 1 - c), device_id_type=MESH)
        cp.start()
        cp.wait_recv()
        cp.wait_send()

    return pl.pallas_call(
        body, in_specs=[_hbm()], out_specs=_hbm(),
        out_shape=jax.ShapeDtypeStruct((K, rh, C), F32),
        scratch_shapes=[pltpu.SemaphoreType.DMA(()), pltpu.SemaphoreType.DMA(())],
        name=name, compiler_params=_cp(),
    )(g_all)


def _chip_exchange_copies(p_ref, o_ref, local_sem, send_sems, recv_sems, sem0, incoming):
    x, y, c = _position()
    me = 2 * x + y
    out = []
    for r, (dx, dy) in enumerate(_CHIP_RELATIONS):
        px, py = _flip(x, dx), _flip(y, dy)
        src_slot, dst_slot = (me, 2 * px + py) if incoming else (2 * px + py, me)
        out.append(pltpu.make_async_remote_copy(
            src_ref=p_ref.at[src_slot], dst_ref=o_ref.at[dst_slot], send_sem=send_sems.at[sem0 + r],
            recv_sem=recv_sems.at[sem0 + r], device_id=(px, py, c), device_id_type=MESH))
    return pltpu.make_async_copy(p_ref.at[me], o_ref.at[me], local_sem), out


def _chip_exchange(parts, name):
    def body(p_ref, o_ref, local_sem, send_sems, recv_sems):
        own, sent = _chip_exchange_copies(p_ref, o_ref, local_sem, send_sems, recv_sems, 0, False)
        own.start()
        for cp in sent:
            cp.start()
        for cp in _chip_exchange_copies(p_ref, o_ref, local_sem, send_sems, recv_sems, 0, True)[1]:
            cp.wait_recv()
        for cp in sent:
            cp.wait_send()
        own.wait()

    return pl.pallas_call(
        body, in_specs=[_hbm()], out_specs=_hbm(),
        out_shape=jax.ShapeDtypeStruct(parts.shape, parts.dtype),
        scratch_shapes=[pltpu.SemaphoreType.DMA(()), pltpu.SemaphoreType.DMA((3,)),
                        pltpu.SemaphoreType.DMA((3,))],
        name=name, compiler_params=_cp(),
    )(parts)


def _swap_with_sibling(half, name):
    rh, C = half.shape

    def body(h_ref, o_ref, send_sem, recv_sem):
        x, y, c = _position()
        cp = pltpu.make_async_remote_copy(
            src_ref=h_ref, dst_ref=o_ref, send_sem=send_sem, recv_sem=recv_sem,
            device_id=(x, y, 1 - c), device_id_type=MESH)
        cp.start()
        cp.wait_recv()
        cp.wait_send()

    return pl.pallas_call(
        body, in_specs=[_hbm()], out_specs=_hbm(),
        out_shape=jax.ShapeDtypeStruct((rh, C), F32),
        scratch_shapes=[pltpu.SemaphoreType.DMA(()), pltpu.SemaphoreType.DMA(())],
        name=name, compiler_params=_cp(),
    )(half)


_DEVICE_RELATIONS = tuple((b >> 2 & 1, b >> 1 & 1, b & 1) for b in range(1, 8))


def _gather_all_copies(p_ref, o_ref, local_sem, send_sems, recv_sems, sem0, incoming):
    x, y, c = _position()
    me = 4 * x + 2 * y + c
    out = []
    for r, (dx, dy, dc) in enumerate(_DEVICE_RELATIONS):
        px, py, pc = _flip(x, dx), _flip(y, dy), _flip(c, dc)
        slot = 4 * px + 2 * py + pc if incoming else me
        out.append(pltpu.make_async_remote_copy(
            src_ref=p_ref, dst_ref=o_ref.at[slot], send_sem=send_sems.at[sem0 + r], recv_sem=recv_sems.at[sem0 + r],
            device_id=(px, py, pc), device_id_type=MESH))
    return pltpu.make_async_copy(p_ref, o_ref.at[me], local_sem), out


def _gather_all(part, name):
    def body(p_ref, o_ref, local_sem, send_sems, recv_sems):
        own, sent = _gather_all_copies(p_ref, o_ref, local_sem, send_sems, recv_sems, 0, False)
        own.start()
        for cp in sent:
            cp.start()
        for cp in _gather_all_copies(p_ref, o_ref, local_sem, send_sems, recv_sems, 0, True)[1]:
            cp.wait_recv()
        for cp in sent:
            cp.wait_send()
        own.wait()

    return pl.pallas_call(
        body, in_specs=[_hbm()], out_specs=_hbm(),
        out_shape=jax.ShapeDtypeStruct((8,) + part.shape, part.dtype),
        scratch_shapes=[pltpu.SemaphoreType.DMA(()), pltpu.SemaphoreType.DMA((7,)),
                        pltpu.SemaphoreType.DMA((7,))],
        name=name, compiler_params=_cp(),
    )(part)


def _pack(arrs, row_mult, cols=PACK_COLS, lead=0):
    head = arrs[0].shape[:lead]
    pieces = []
    for a in arrs:
        flat = a.astype(F32).reshape(head + (-1,))
        fill = -flat.shape[-1] % cols
        if fill:
            flat = jnp.concatenate([flat, jnp.zeros(head + (fill,), F32)], axis=-1)
        pieces.append(flat.reshape(head + (-1, cols)))
    rows = sum(p.shape[lead] for p in pieces)
    fill = -rows % row_mult
    if fill:
        pieces.append(jnp.zeros(head + (fill, cols), F32))
    return jnp.concatenate(pieces, axis=lead) if len(pieces) > 1 else pieces[0]


def _unpack(buf, shapes):
    lead = buf.shape[:-2]
    cols = buf.shape[-1]
    out, off = [], 0
    for shp in shapes:
        n = math.prod(shp)
        rows = -(-n // cols)
        piece = buf[..., off:off + rows, :]
        if rows * cols != n:
            piece = piece.reshape(lead + (-1,))[..., :n]
        out.append(piece.reshape(lead + tuple(shp)))
        off += rows
    return out


def _cols_from_chips(g):
    k, L, A, n = g.shape
    return jnp.transpose(g, (1, 2, 0, 3)).reshape(L, A, k * n)


def _rows_from_chips(g):
    k, L, n, B = g.shape
    return jnp.transpose(g, (1, 0, 2, 3)).reshape(L, k * n, B)


def _cols_to_chips(full, k=4):
    L, A, N = full.shape
    return jnp.transpose(full.reshape(L, A, k, N // k), (2, 0, 1, 3))


def _rows_to_chips(full, k=4):
    L, N, B = full.shape
    return jnp.transpose(full.reshape(L, k, N // k, B), (1, 0, 2, 3))


def kernel(x, mixer_norm_w, attn_w_in, attn_b_f, attn_w_out, sgu_w_in, sgu_ln_g, sgu_ln_b, sgu_w_s, sgu_b_s, sgu_w_out, ffn_norm_w, ffn_w_in, ffn_w_out, final_norm_w, loss_target, m_mixer_norm_w, m_attn_w_in, m_attn_b_f, m_attn_w_out, m_sgu_w_in, m_sgu_ln_g, m_sgu_ln_b, m_sgu_w_s, m_sgu_b_s, m_sgu_w_out, m_ffn_norm_w, m_ffn_w_in, m_ffn_w_out, m_final_norm_w, v_mixer_norm_w, v_attn_w_in, v_attn_b_f, v_attn_w_out, v_sgu_w_in, v_sgu_ln_g, v_sgu_ln_b, v_sgu_w_s, v_sgu_b_s, v_sgu_w_out, v_ffn_norm_w, v_ffn_w_in, v_ffn_w_out, v_final_norm_w):
    T, D = x.shape[1], x.shape[2]
    depth = mixer_norm_w.shape[0]
    H = attn_b_f.shape[1]
    P = D // LANES
    assert D % LANES == 0 and D // H == 64 and 2 * P == H and 2 * P <= LANES
    G = sgu_w_s.shape[1]
    W = sgu_w_out.shape[1] * 4
    assert sgu_w_s.shape[2] == LANES and W == G * LANES
    scale = float(D // H) ** -0.5
    f_pad = LANES
    c_idx = lax.axis_index("c").astype(jnp.int32).reshape(1)

    groups = [
        ([attn_w_out, sgu_w_in, sgu_w_out, ffn_w_out, sgu_ln_g, sgu_ln_b],
         [m_attn_w_out, m_sgu_w_in, m_sgu_w_out, m_ffn_w_out, m_sgu_ln_g, m_sgu_ln_b],
         [v_attn_w_out, v_sgu_w_in, v_sgu_w_out, v_ffn_w_out, v_sgu_ln_g, v_sgu_ln_b]),
        ([ffn_w_in], [m_ffn_w_in], [v_ffn_w_in]),
        ([attn_w_in], [m_attn_w_in], [v_attn_w_in]),
    ]
    group_cols = [D, ffn_w_in.shape[2], attn_w_in.shape[2]]
    group_shapes = [[a.shape for a in g[0]] for g in groups]
    w_packs = [_pack(g[0], 512, cols) for g, cols in zip(groups, group_cols)]
    ln_pack = _pack([sgu_ln_g, sgu_ln_b], SUBLANES)
    my_chip = 2 * lax.axis_index("x") + lax.axis_index("y")
    w_packs_b = [w.astype(BF16) for w in w_packs]

    def finish_gather(t, gat):
        return _unpack(lax.dynamic_update_index_in_dim(gat, w_packs_b[t], my_chip, 0), group_shapes[t])

    gat, gat_ln = _gather_weights(w_packs_b[2], ln_pack, "gather_weights_2")
    (g_ai,) = finish_gather(2, gat)
    g_lng, g_lnb = _unpack(gat_ln, [sgu_ln_g.shape, sgu_ln_b.shape])
    w_ai = _cols_from_chips(g_ai)
    w_ai = jnp.pad(w_ai, ((0, 0), (0, 0), (0, 3 * D + f_pad - w_ai.shape[2])))
    b_f_pad = jnp.pad(attn_b_f, ((0, 0), (0, LANES - H)))
    xs = x.reshape(T, D)
    h, qkv, f = _norm_mm(xs, mixer_norm_w[0], w_ai[0], ((3 * D, BF16), (f_pad, F32)), "attn_qkv_0")
    cT, c_cols, c0T = _gate_fwd(f, b_f_pad[0:1], P, "gate_fwd_0")
    o, lseT, halves_0, halves_1 = _attn_fwd(qkv, cT, P, scale, "attn_fwd_0", riders=(w_packs_b[0], w_packs_b[1]))
    first_attention = dict(h=h, qkv=qkv, f=f, c0T=c0T, c_cols=c_cols, o=o, lseT=lseT)
    g_ao, g_si, g_so, g_fo, _, _ = finish_gather(0, _hand_to_sibling(halves_0, "gather_weights_0_d2d"))
    (g_fi,) = finish_gather(1, _hand_to_sibling(halves_1, "gather_weights_1_d2d"))
    w_ao = _rows_from_chips(g_ao)
    w_si = _cols_from_chips(g_si)
    w_so = _rows_from_chips(g_so)
    w_fo = _rows_from_chips(g_fo)
    w_fi5 = g_fi.reshape((2, 2) + g_fi.shape[1:])
    ln_g = jnp.transpose(g_lng, (1, 0, 2)).reshape(sgu_ln_g.shape[0], W)
    ln_b = jnp.transpose(g_lnb, (1, 0, 2)).reshape(sgu_ln_b.shape[0], W)
    w_tril = jnp.tril(sgu_w_s)
    w_tril_b = w_tril.astype(BF16)
    w_tril_tb = jnp.swapaxes(w_tril, 2, 3).astype(BF16)
    sgu_bias = jnp.repeat(jnp.swapaxes(sgu_b_s, 1, 2), LANES, axis=2)

    saved = []
    for i in range(depth):
        j = i // 2
        rec = {"x_in": xs}
        if i == 0:
            rec.update(first_attention)
            h, o = rec["h"], rec["o"]
            x_mid = _mm(o, w_ao[j], "nn", F32, f"attn_out_{i}", res=xs)
        elif i % 2 == 0:
            h, qkv, f = _norm_mm(xs, mixer_norm_w[i], w_ai[j], ((3 * D, BF16), (f_pad, F32)), f"attn_qkv_{i}")
            cT, c_cols, c0T = _gate_fwd(f, b_f_pad[j:j + 1], P, f"gate_fwd_{i}")
            o, lseT = _attn_fwd(qkv, cT, P, scale, f"attn_fwd_{i}")
            x_mid = _mm(o, w_ao[j], "nn", F32, f"attn_out_{i}", res=xs)
            rec.update(qkv=qkv, f=f, c0T=c0T, c_cols=c_cols, o=o, lseT=lseT)
        else:
            h, a = _norm_mm(xs, mixer_norm_w[i], w_si[j], ((2 * W, BF16),), f"sgu_in_{i}")
            gated = _sgu_fwd(a, ln_g[j], ln_b[j], w_tril_b[j], sgu_bias[j], f"sgu_fwd_{i}")
            x_mid = _mm(gated, w_so[j], "nn", F32, f"sgu_out_{i}", res=xs)
            rec.update(a=a, gated=gated)
        h2, fa, s = _ffn_in_act(x_mid, ffn_norm_w[i], w_fi5, i, f"ffn_in_{i}")
        xs = _mm(s, w_fo[i], "nn", F32, f"ffn_out_{i}", res=x_mid)
        rec.update(h=h, x_mid=x_mid, h2=h2, fa=fa, s=s)
        saved.append(rec)

    gx, loss_acc, dw_final = _loss_head(xs, final_norm_w, loss_target.reshape(T, D), "loss_head")
    loss = lax.psum(loss_acc[0, 0], ("x", "y", "c"))

    n_attn, n_sgu = attn_w_in.shape[0], sgu_w_in.shape[0]
    d_mixer_norm, d_ffn_norm = [None] * depth, [None] * depth
    d_ai, d_ao, d_bf = [None] * n_attn, [None] * n_attn, [None] * n_attn
    d_si, d_so, d_lng, d_lnb, d_ws, d_bs = ([None] * n_sgu for _ in range(6))
    d_fi, d_fo = [None] * depth, [None] * depth

    def pair_sums(t):
        if t == 0:
            grads = [_rows_to_chips(jnp.stack(d_ao)), _cols_to_chips(jnp.stack(d_si)), _rows_to_chips(jnp.stack(d_so)),
                     _rows_to_chips(jnp.stack(d_fo)),
                     jnp.transpose(jnp.stack(d_lng).reshape(n_sgu, 4, W // 4), (1, 0, 2)),
                     jnp.transpose(jnp.stack(d_lnb).reshape(n_sgu, 4, W // 4), (1, 0, 2))]
        elif t == 1:
            grads = [jnp.stack(d_fi, axis=1)]
        else:
            grads = [_cols_to_chips(jnp.stack(d_ai))]
        g_all = _pack(grads, 512, group_cols[t], lead=1)
        from_sibling = _sibling_halves(g_all, f"grad_sibling_halves_{t}")
        return _pair_sum(g_all, from_sibling, c_idx, f"grad_pair_sum_{t}")

    for i in reversed(range(depth)):
        j = i // 2
        rec = saved[i]
        d_fo[i] = _mm(rec["s"], gx, "tn", F32, f"ffn_out_wgrad_{i}")
        da = _ffn_out_bwd_act(gx, w_fo[i], rec["fa"], f"ffn_out_bwd_{i}")
        da = da.reshape((4,) + da.shape[2:])
        d_fi[i] = _mm_tn_shards(rec["h2"], da, f"ffn_in_wgrad_{i}")
        gx, dwn = _nt_norm_bwd(da, g_fi, i, rec["x_mid"], ffn_norm_w[i], gx, f"ffn_in_bwd_{i}")
        d_ffn_norm[i] = dwn[0]
        if i % 2 == 0:
            do = _mm(gx, w_ao[j], "nt", BF16, f"attn_out_bwd_{i}")
            d_ao[j] = _mm(rec["o"], gx, "tn", F32, f"attn_out_wgrad_{i}")
            dT = _attn_delta(do, rec["o"], P, f"attn_delta_{i}")
            riders, gather_riders = (), ()
            if i == 0:
                riders = tuple(pair_sums(t) for t in range(2))
                gather_riders = (_pack([jnp.stack(d_ws), jnp.stack(d_bs)], SUBLANES),)
            dq, dk, dv, dc_cols, drowT, *early = _attn_bwd(
                rec["qkv"], do, rec["lseT"], dT, rec["c0T"], rec["c_cols"], P, scale, f"attn_bwd_{i}",
                riders=riders, gather_riders=gather_riders)
            df, dbf = _gate_bwd(dc_cols, drowT, rec["f"], b_f_pad[j:j + 1], P, f"gate_bwd_{i}")
            d_bf[j] = dbf[0, :H]
            dproj = jnp.concatenate([dq, dk, dv, df.astype(BF16)], axis=1)
            d_ai[j] = _mm(rec["h"], dproj, "tn", F32, f"attn_in_wgrad_{i}")[:, :3 * D + H]
            dmix, w_mix = dproj, w_ai
        else:
            dgated = _mm(gx, w_so[j], "nt", F32, f"sgu_out_bwd_{i}")
            d_so[j] = _mm(rec["gated"], gx, "tn", F32, f"sgu_out_wgrad_{i}")
            da_s, dws, dbias, dlng, dlnb = _sgu_bwd(rec["a"], dgated, ln_g[j], ln_b[j], w_tril_b[j],
                                                    w_tril_tb[j], sgu_bias[j], f"sgu_bwd_{i}")
            d_ws[j] = jnp.tril(dws)
            d_bs[j] = jnp.sum(dbias.reshape(LANES, G, LANES), axis=2).T
            d_lng[j], d_lnb[j] = dlng[0], dlnb[0]
            d_si[j] = _mm(rec["h"], da_s, "tn", F32, f"sgu_in_wgrad_{i}")
            dmix, w_mix = da_s, w_si
        gx, dwn = _nt_norm_bwd(dmix[None], w_mix[None], j, rec["x_in"], mixer_norm_w[i], gx, f"mixer_in_bwd_{i}")
        d_mixer_norm[i] = dwn[0]
    grad_x = gx.reshape(x.shape)

    reduced = []
    for t, cols in enumerate(group_cols):
        from_chips = early[t] if t < 2 else _chip_exchange(pair_sums(t), f"grad_chip_exchange_{t}")
        my_half = _sum_parts(from_chips, f"grad_chip_sum_{t}")
        sibling_half = _swap_with_sibling(my_half, f"grad_swap_halves_{t}")
        packs = _adamw_halves(my_half, sibling_half, c_idx, w_packs[t], _pack(groups[t][1], 512, cols),
                              _pack(groups[t][2], 512, cols), f"adamw_sharded_{t}")
        reduced.append([_unpack(p, group_shapes[t]) for p in packs])

    def sharded_outputs(which):
        (ao, si, so, fo, lng, lnb), (fi,), (ai,) = (reduced[t][which] for t in range(3))
        return [ai, ao, si, so, fi, fo, lng, lnb]

    g_sh, d_sh, m_sh, v_sh = (sharded_outputs(w) for w in range(4))

    repl_groups = [
        ([sgu_w_s, sgu_b_s], [m_sgu_w_s, m_sgu_b_s], [v_sgu_w_s, v_sgu_b_s], early[2], "sgu"),
        ([mixer_norm_w, attn_b_f, ffn_norm_w, final_norm_w],
         [m_mixer_norm_w, m_attn_b_f, m_ffn_norm_w, m_final_norm_w],
         [v_mixer_norm_w, v_attn_b_f, v_ffn_norm_w, v_final_norm_w],
         _gather_all(_pack([jnp.stack(d_mixer_norm), jnp.stack(d_bf), jnp.stack(d_ffn_norm), dw_final[0]], SUBLANES),
                     "grad_gather_replicated"), "rest"),
    ]
    rep = []
    for ws, ms, vs, parts, tag in repl_groups:
        packs = _adamw_sum(parts, _pack(ws, SUBLANES), _pack(ms, SUBLANES), _pack(vs, SUBLANES),
                           f"adamw_replicated_{tag}")
        rep.append([_unpack(p, [a.shape for a in ws]) for p in packs])

    def replicated_outputs(which):
        (ws_, bs_), (mn, bf, fn, fin) = rep[0][which], rep[1][which]
        return [mn, bf, ws_, bs_, fn, fin]

    g_r, d_r, m_r, v_r = (replicated_outputs(w) for w in range(4))

    def ordered(sh, rp):
        ai, ao, si, so, fi, fo, lng, lnb = sh
        mn, bf, ws, bs, fn, fin = rp
        return [mn, ai, bf, ao, si, lng, lnb, ws, bs, so, fn, fi, fo, fin]

    return (loss, grad_x, *ordered(g_sh, g_r), *ordered(d_sh, d_r), *ordered(m_sh, m_r), *ordered(v_sh, v_r))
```

```python
import math

import jax
import jax.numpy as jnp
from jax import lax
from jax.experimental import pallas as pl
from jax.experimental.pallas import tpu as pltpu

F32 = jnp.float32
BF16 = jnp.bfloat16
NORM_EPS = 1e-6
LN_EPS = 1e-5
ADAM_LR = 0.001
ADAM_B1 = 0.9
ADAM_B2 = 0.999
ADAM_EPS = 1e-08
ADAM_WD = 0.01
ADAM_STEP = 10

LANES = 128
SUBLANES = 8
PACK_COLS = 1024
VMEM_LIMIT = 56 * 1024 * 1024
NEG_BIG = -1e30
LOG2E = 1.4426950408889634
MESH = pl.DeviceIdType.MESH


def _cp():
    return pltpu.CompilerParams(vmem_limit_bytes=VMEM_LIMIT)


def _tile(n, cap, mult):
    best = None
    d = mult
    while d <= min(n, cap):
        if n % d == 0:
            best = d
        d += mult
    return n if best is None else best


def _row_tile(rows, cols):
    cap = max(16, (512 * 1024 // cols) // 16 * 16)
    return _tile(rows, cap, 16)


def _hbm():
    return pl.BlockSpec(memory_space=pltpu.HBM)


def _nt_norm_bwd(a3, b4, layer, x, w, dres, name):
    S, T, Ks = a3.shape
    D = x.shape[1]
    tm = _tile(T, 256, 16)

    def body(a_ref, b_ref, x_ref, w_ref, dres_ref, dx_ref, dw_ref):
        @pl.when(pl.program_id(0) == 0)
        def _():
            dw_ref[...] = jnp.zeros_like(dw_ref)

        dh = _nt(a_ref[0].astype(BF16), b_ref[0, 0])
        for s in range(1, S):
            dh = dh + _nt(a_ref[s].astype(BF16), b_ref[s, 0])
        xf = x_ref[...]
        r = lax.rsqrt(jnp.mean(xf * xf, axis=-1, keepdims=True) + NORM_EPS)
        xhat = xf * r
        dxhat = dh * w_ref[...]
        dx_ref[...] = dres_ref[...] + r * (dxhat - xhat * jnp.mean(dxhat * xhat, axis=-1, keepdims=True))
        dw_ref[...] += jnp.sum(dh * xhat, axis=0, keepdims=True)

    row = pl.BlockSpec((tm, D), lambda i: (i, 0))
    return pl.pallas_call(
        body, grid=(T // tm,),
        in_specs=[pl.BlockSpec((S, tm, Ks), lambda i: (0, i, 0)),
                  pl.BlockSpec((S, 1, D, Ks), lambda i: (0, layer, 0, 0)),
                  row, pl.BlockSpec((1, D), lambda i: (0, 0)), row],
        out_specs=[row, pl.BlockSpec((SUBLANES, D), lambda i: (0, 0))],
        out_shape=[jax.ShapeDtypeStruct((T, D), F32), jax.ShapeDtypeStruct((SUBLANES, D), F32)],
        name=name, compiler_params=_cp(),
    )(a3, b4, x, w.reshape(1, D), dres)


def _mm(a, b, mode, out_dtype, name, res=None):
    if mode == "tn":
        kt, M = a.shape
        N = b.shape[1]
        tm = _tile(M, 1408, LANES)
        tn = _tile(N, 1408, LANES)
        tk = _tile(kt, 1024, 16)

        def body(a_ref, b_ref, o_ref):
            @pl.when(pl.program_id(2) == 0)
            def _():
                o_ref[...] = jnp.zeros_like(o_ref)

            o_ref[...] += lax.dot_general(
                a_ref[...].astype(BF16), b_ref[...].astype(BF16), (((0,), (0,)), ((), ())),
                preferred_element_type=F32)

        return pl.pallas_call(
            body, grid=(M // tm, N // tn, kt // tk),
            in_specs=[pl.BlockSpec((tk, tm), lambda i, j, k: (k, i)),
                      pl.BlockSpec((tk, tn), lambda i, j, k: (k, j))],
            out_specs=pl.BlockSpec((tm, tn), lambda i, j, k: (i, j)),
            out_shape=jax.ShapeDtypeStruct((M, N), F32), name=name, compiler_params=_cp(),
        )(a, b)

    M, K = a.shape
    N = b.shape[1] if mode == "nn" else b.shape[0]
    tm = _tile(M, 512, 16)
    cap = min(3072, (6 << 20) // (2 * K), (4 << 20) // (tm * jnp.dtype(out_dtype).itemsize))
    tn = _tile(N, max(LANES, cap // LANES * LANES), LANES)
    dims = (((1,), (0,)), ((), ())) if mode == "nn" else (((1,), (1,)), ((), ()))

    def body(*refs):
        if res is None:
            a_ref, b_ref, o_ref = refs
        else:
            a_ref, b_ref, r_ref, o_ref = refs
        acc = lax.dot_general(a_ref[...].astype(BF16), b_ref[...].astype(BF16), dims,
                              preferred_element_type=F32)
        if res is not None:
            acc = acc + r_ref[...]
        o_ref[...] = acc.astype(out_dtype)

    b_spec = (pl.BlockSpec((K, tn), lambda j, i: (0, j)) if mode == "nn"
              else pl.BlockSpec((tn, K), lambda j, i: (j, 0)))
    in_specs = [pl.BlockSpec((tm, K), lambda j, i: (i, 0)), b_spec]
    args = [a, b]
    if res is not None:
        in_specs.append(pl.BlockSpec((tm, tn), lambda j, i: (i, j)))
        args.append(res)
    return pl.pallas_call(
        body, grid=(N // tn, M // tm), in_specs=in_specs,
        out_specs=pl.BlockSpec((tm, tn), lambda j, i: (i, j)),
        out_shape=jax.ShapeDtypeStruct((M, N), out_dtype), name=name, compiler_params=_cp(),
    )(*args)


def _normed(x_ref, w_ref):
    xf = x_ref[...]
    r = lax.rsqrt(jnp.mean(xf * xf, axis=-1, keepdims=True) + NORM_EPS)
    return (xf * r * w_ref[...]).astype(BF16)


def _ffn_in_act(x, norm_w, w5, layer, name):
    T, D = x.shape
    n = w5.shape[-1]
    tm = _tile(T, 256, 16)

    def body(x_ref, nw_ref, w_ref, h_ref, a_ref, s_ref):
        hv = _normed(x_ref, nw_ref)
        h_ref[...] = hv
        for half in range(2):
            g = jnp.dot(hv, w_ref[0, half, 0], preferred_element_type=F32)
            u = jnp.dot(hv, w_ref[1, half, 0], preferred_element_type=F32)
            a_ref[0, half] = g.astype(BF16)
            a_ref[1, half] = u.astype(BF16)
            s_ref[:, half * n:(half + 1) * n] = (g * jax.nn.sigmoid(g) * u).astype(BF16)

    return pl.pallas_call(
        body, grid=(T // tm,),
        in_specs=[pl.BlockSpec((tm, D), lambda i: (i, 0)), pl.BlockSpec((1, D), lambda i: (0, 0)),
                  pl.BlockSpec((2, 2, 1, D, n), lambda i: (0, 0, layer, 0, 0))],
        out_specs=[pl.BlockSpec((tm, D), lambda i: (i, 0)),
                   pl.BlockSpec((2, 2, tm, n), lambda i: (0, 0, i, 0)), pl.BlockSpec((tm, 2 * n), lambda i: (i, 0))],
        out_shape=[jax.ShapeDtypeStruct((T, D), BF16), jax.ShapeDtypeStruct((2, 2, T, n), BF16),
                   jax.ShapeDtypeStruct((T, 2 * n), BF16)],
        name=name, compiler_params=_cp(),
    )(x, norm_w.reshape(1, D), w5)


def _norm_mm(x, norm_w, b, splits, name):
    T, D = x.shape
    N = b.shape[1]
    assert sum(wd for wd, _ in splits) == N
    tm = _tile(T, 256, 16)

    def body(x_ref, nw_ref, b_ref, h_ref, *outs):
        hv = _normed(x_ref, nw_ref)
        h_ref[...] = hv
        off = 0
        for o_ref, (wd, dt) in zip(outs, splits):
            o_ref[...] = jnp.dot(hv, b_ref[:, off:off + wd], preferred_element_type=F32).astype(dt)
            off += wd

    return pl.pallas_call(
        body, grid=(T // tm,),
        in_specs=[pl.BlockSpec((tm, D), lambda i: (i, 0)), pl.BlockSpec((1, D), lambda i: (0, 0)),
                  pl.BlockSpec((D, N), lambda i: (0, 0))],
        out_specs=[pl.BlockSpec((tm, D), lambda i: (i, 0))] + [pl.BlockSpec((tm, wd), lambda i: (i, 0)) for wd, _ in splits],
        out_shape=[jax.ShapeDtypeStruct((T, D), BF16)] + [jax.ShapeDtypeStruct((T, wd), dt) for wd, dt in splits],
        name=name, compiler_params=_cp(),
    )(x, norm_w.reshape(1, D), b)


def _ffn_out_bwd_act(gx, w_out, a4, name):
    T, D = gx.shape
    n = a4.shape[-1]
    tm = _tile(T, 256, 16)

    step = 3 * LANES if n % LANES == 0 and n > 3 * LANES else n
    pieces = [(c, min(step, n - c)) for c in range(0, n, step)]

    def body(gx_ref, w_ref, a_ref, da_ref):
        gxb = gx_ref[...].astype(BF16)
        for c, wd in pieces:
            ds = _nt(gxb, w_ref[c:c + wd, :])
            g = a_ref[0, 0, :, c:c + wd].astype(F32)
            u = a_ref[1, 0, :, c:c + wd].astype(F32)
            sg = jax.nn.sigmoid(g)
            da_ref[0, 0, :, c:c + wd] = (ds * u * (sg * (1.0 + g * (1.0 - sg)))).astype(BF16)
            da_ref[1, 0, :, c:c + wd] = (ds * (g * sg)).astype(BF16)

    blk = pl.BlockSpec((2, 1, tm, n), lambda j, i: (0, j, i, 0))
    return pl.pallas_call(
        body, grid=(2, T // tm),
        in_specs=[pl.BlockSpec((tm, D), lambda j, i: (i, 0)), pl.BlockSpec((n, D), lambda j, i: (j, 0)), blk],
        out_specs=blk,
        out_shape=jax.ShapeDtypeStruct((2, 2, T, n), BF16), name=name, compiler_params=_cp(),
    )(gx, w_out, a4)


def _mm_tn_shards(h, a4, name):
    K, T, n = a4.shape
    D = h.shape[1]
    tk = _tile(T, 1024, 16)

    def body(h_ref, a_ref, o_ref):
        @pl.when(pl.program_id(1) == 0)
        def _():
            o_ref[...] = jnp.zeros_like(o_ref)

        o_ref[0] += lax.dot_general(h_ref[...], a_ref[0], (((0,), (0,)), ((), ())), preferred_element_type=F32)

    return pl.pallas_call(
        body, grid=(K, T // tk),
        in_specs=[pl.BlockSpec((tk, D), lambda k, t: (t, 0)), pl.BlockSpec((1, tk, n), lambda k, t: (k, t, 0))],
        out_specs=pl.BlockSpec((1, D, n), lambda k, t: (k, 0, 0)),
        out_shape=jax.ShapeDtypeStruct((K, D, n), F32), name=name, compiler_params=_cp(),
    )(h, a4)


def _loss_head(x, w, tgt, name):
    T, D = x.shape
    tm = _tile(T, 512, SUBLANES)

    def body(x_ref, w_ref, t_ref, dx_ref, loss_ref, dw_ref):
        @pl.when(pl.program_id(0) == 0)
        def _():
            loss_ref[...] = jnp.zeros_like(loss_ref)
            dw_ref[...] = jnp.zeros_like(dw_ref)

        xf = x_ref[...]
        wv = w_ref[...]
        r = lax.rsqrt(jnp.mean(xf * xf, axis=-1, keepdims=True) + NORM_EPS)
        xhat = xf * r
        err = xhat * wv - t_ref[...]
        per_tok = jnp.mean(err * err, axis=-1, keepdims=True)
        loss_ref[...] += 0.5 * jnp.sum(per_tok, axis=0, keepdims=True)
        dy = err * (1.0 / D)
        dxhat = dy * wv
        dx_ref[...] = r * (dxhat - xhat * jnp.mean(dxhat * xhat, axis=-1, keepdims=True))
        dw_ref[...] += jnp.sum(dy * xhat, axis=0, keepdims=True)

    row = pl.BlockSpec((tm, D), lambda i: (i, 0))
    return pl.pallas_call(
        body, grid=(T // tm,),
        in_specs=[row, pl.BlockSpec((1, D), lambda i: (0, 0)), row],
        out_specs=[row, pl.BlockSpec((SUBLANES, LANES), lambda i: (0, 0)),
                   pl.BlockSpec((SUBLANES, D), lambda i: (0, 0))],
        out_shape=[jax.ShapeDtypeStruct((T, D), F32), jax.ShapeDtypeStruct((SUBLANES, LANES), F32),
                   jax.ShapeDtypeStruct((SUBLANES, D), F32)],
        name=name, compiler_params=_cp(),
    )(x, w.reshape(1, D), tgt)


def _split3(v):
    hi = v.astype(BF16)
    r1 = v - hi.astype(F32)
    mid = r1.astype(BF16)
    lo = (r1 - mid.astype(F32)).astype(BF16)
    return hi, mid, lo


def _tri_dot(tri, v):
    out = None
    for piece in _split3(v):
        t = jnp.dot(tri, piece, preferred_element_type=F32)
        out = t if out is None else out + t
    return out


def _q_block(T):
    return _tile(T, 256, LANES)


def _gate_fwd(f, b_f, P, name):
    T = f.shape[0]
    tb = _q_block(T)

    def body(f_ref, b_ref, ct_ref, cc_ref, c0_ref, carry):
        @pl.when(pl.program_id(0) == 0)
        def _():
            carry[...] = jnp.zeros_like(carry)

        z = f_ref[...] + b_ref[...]
        logf = jnp.minimum(z, 0.0) - jnp.log(1.0 + jnp.exp(-jnp.abs(z)))
        row = lax.broadcasted_iota(jnp.int32, (tb, tb), 0)
        col = lax.broadcasted_iota(jnp.int32, (tb, tb), 1)
        tri = (col <= row).astype(BF16)
        c = _tri_dot(tri, logf) + carry[0:1, :]
        carry[...] = jnp.broadcast_to(c[tb - 1:tb, :], carry.shape)
        first = jnp.broadcast_to(c[0:1, :], c.shape)
        for p in range(P):
            shifted = c if p == 0 else pltpu.roll(c, LANES - 2 * p, 1)
            cc_ref[p] = shifted
            ct_ref[p] = shifted.T[0:SUBLANES, :]
            c0_ref[p] = (first if p == 0 else pltpu.roll(first, LANES - 2 * p, 1)).T[0:SUBLANES, :]

    rows = pl.BlockSpec((P, SUBLANES, tb), lambda i: (0, 0, i))
    return pl.pallas_call(
        body, grid=(T // tb,),
        in_specs=[pl.BlockSpec((tb, LANES), lambda i: (i, 0)), pl.BlockSpec((1, LANES), lambda i: (0, 0))],
        out_specs=[rows, pl.BlockSpec((P, tb, LANES), lambda i: (0, i, 0)), rows],
        out_shape=[jax.ShapeDtypeStruct((P, SUBLANES, T), F32), jax.ShapeDtypeStruct((P, T, LANES), F32),
                   jax.ShapeDtypeStruct((P, SUBLANES, T), F32)],
        scratch_shapes=[pltpu.VMEM((SUBLANES, LANES), F32)],
        name=name, compiler_params=_cp(),
    )(f, b_f)


def _gate_bwd(dc_cols, drowT, f, b_f, P, name):
    T = f.shape[0]
    tb = _tile(T, 256, LANES)
    nb = T // tb

    def body(dc_ref, dr_ref, f_ref, b_ref, df_ref, db_ref, carry):
        @pl.when(pl.program_id(0) == 0)
        def _():
            carry[...] = jnp.zeros_like(carry)
            db_ref[...] = jnp.zeros_like(db_ref)

        lane = lax.broadcasted_iota(jnp.int32, (tb, LANES), 1)
        dc = jnp.zeros((tb, LANES), F32)
        for p in range(P):
            rows = jnp.concatenate([dr_ref[p], jnp.zeros((LANES - SUBLANES, tb), F32)], axis=0)
            part = jnp.where(lane < 2, dc_ref[p] + rows.T, 0.0)
            dc = dc + (part if p == 0 else pltpu.roll(part, 2 * p, 1))
        row = lax.broadcasted_iota(jnp.int32, (tb, tb), 0)
        col = lax.broadcasted_iota(jnp.int32, (tb, tb), 1)
        tri = (col >= row).astype(BF16)
        dlogf = _tri_dot(tri, dc) + carry[0:1, :]
        carry[...] = jnp.broadcast_to(dlogf[0:1, :], carry.shape)
        z = f_ref[...] + b_ref[...]
        df = jnp.where(lane < 2 * P, dlogf * jax.nn.sigmoid(-z), 0.0)
        df_ref[...] = df
        db_ref[...] += jnp.sum(df, axis=0, keepdims=True)

    return pl.pallas_call(
        body, grid=(nb,),
        in_specs=[pl.BlockSpec((P, tb, LANES), lambda i: (0, nb - 1 - i, 0)),
                  pl.BlockSpec((P, SUBLANES, tb), lambda i: (0, 0, nb - 1 - i)),
                  pl.BlockSpec((tb, LANES), lambda i: (nb - 1 - i, 0)),
                  pl.BlockSpec((1, LANES), lambda i: (0, 0))],
        out_specs=[pl.BlockSpec((tb, LANES), lambda i: (nb - 1 - i, 0)),
                   pl.BlockSpec((SUBLANES, LANES), lambda i: (0, 0))],
        out_shape=[jax.ShapeDtypeStruct((T, LANES), F32), jax.ShapeDtypeStruct((SUBLANES, LANES), F32)],
        scratch_shapes=[pltpu.VMEM((SUBLANES, LANES), F32)],
        name=name, compiler_params=_cp(),
    )(dc_cols, drowT, f, b_f)


def _nt(a, b):
    return lax.dot_general(a, b, (((1,), (1,)), ((), ())), preferred_element_type=F32)


def _attn_fwd(qkv, cT, P, scale, name, riders=()):
    T = qkv.shape[0]
    n_r = len(riders)
    tq = _q_block(T)
    tw = _tile(T, 8 * tq, 2 * tq)
    cw = tw // 2
    assert cw % tq == 0, "the sequence must split into chunks of whole query blocks"
    nq = T // tq

    def body(q_ref, k_ref, v_ref, c_ref, *rest):
        w_refs, (o_ref, lse_ref), ow_refs = rest[:n_r], rest[n_r:n_r + 2], rest[n_r + 2:2 * n_r + 2]
        s_scr = rest[2 * n_r + 2]
        i = pl.program_id(1)
        if n_r:
            send_sems, recv_sems, local_sems = rest[2 * n_r + 3:]
            first = (pl.program_id(0) == 0) & (i == 0)
            final = (pl.program_id(0) == P - 1) & (i == nq - 1)

            def own_slot(t):
                me = 2 * lax.axis_index("x") + lax.axis_index("y")
                return pltpu.make_async_copy(w_refs[t], ow_refs[t].at[me], local_sems.at[t])

            @pl.when(first)
            def _():
                for t in range(n_r):
                    own_slot(t).start()
                    for cp in _shard_half_copies(w_refs[t], ow_refs[t], send_sems, recv_sems, 3 * t, False):
                        cp.start()

            @pl.when(final)
            def _():
                for t in range(n_r):
                    for cp in _shard_half_copies(w_refs[t], ow_refs[t], send_sems, recv_sems, 3 * t, True):
                        cp.wait_recv()
                    for cp in _shard_half_copies(w_refs[t], ow_refs[t], send_sems, recv_sems, 3 * t, False):
                        cp.wait_send()
                    own_slot(t).wait()

        lane = lax.broadcasted_iota(jnp.int32, (1, LANES), 1)
        q = (q_ref[...].astype(F32) * (scale * LOG2E)).astype(BF16)
        q_heads = (jnp.where(lane < 64, q, jnp.zeros_like(q)), jnp.where(lane >= 64, q, jnp.zeros_like(q)))
        c0 = c_ref[0, :, pl.ds(pl.multiple_of(i * tq, tq), LANES)][:, 0:1]

        def scores(start, width, a):
            bias = (c0 - c_ref[0, :, pl.ds(start, width)]) * LOG2E
            return _nt(q_heads[a], k_ref[pl.ds(start, width), :]) + bias[a:a + 1, :]

        def softmax_pv(start, width, s_of, carry):
            v = v_ref[pl.ds(start, width), :]
            one = jnp.ones_like(v)
            v_heads = (jnp.where(lane < 64, v, one), jnp.where(lane >= 64, v, one))
            new = []
            for a in range(2):
                m, acc = carry[a]
                s = s_of(a)
                m_new = jnp.maximum(m, jnp.max(s, axis=1, keepdims=True))
                p = jnp.exp2(s - m_new)
                acc = jnp.exp2(m - m_new) * acc + jnp.dot(p.astype(BF16), v_heads[a], preferred_element_type=F32)
                new.append((m_new, acc))
            return tuple(new)

        def fill(start, buf):
            for a in range(2):
                s_scr[2 * buf + a] = scores(start, cw, a)

        def wide(j, carry):
            base = pl.multiple_of(j * tw, tw)
            fill(base + cw, 1)
            carry = softmax_pv(base, cw, lambda a: s_scr[a], carry)
            fill(base + tw, 0)
            return softmax_pv(base + cw, cw, lambda a: s_scr[2 + a], carry)

        init = tuple((jnp.full((tq, 1), NEG_BIG, F32), jnp.zeros((tq, LANES), F32)) for _ in range(2))
        n_wide = (i * tq) // tw
        fill(0, 0)
        carry = lax.fori_loop(0, n_wide, wide, init)

        base = pl.multiple_of(n_wide * tw, tw)
        ahead = i * tq - base
        col_minus_row = (lax.broadcasted_iota(jnp.int32, (tq, cw), 1)
                         - lax.broadcasted_iota(jnp.int32, (tq, cw), 0))

        def causal(buf, first_key):
            return lambda a: jnp.where(col_minus_row <= ahead - first_key, s_scr[2 * buf + a], NEG_BIG)

        def one_chunk(cr):
            return softmax_pv(base, cw, causal(0, 0), cr)

        def two_chunks(cr):
            fill(base + cw, 1)
            cr = softmax_pv(base, cw, causal(0, 0), cr)
            return softmax_pv(base + cw, cw, causal(1, cw), cr)

        (m0, a0), (m1, a1) = lax.cond(ahead >= cw, two_chunks, one_chunk, carry)
        sums = jnp.where(lane < 64, pltpu.roll(a0, 64, 1), pltpu.roll(a1, 64, 1))
        o_ref[...] = (jnp.where(lane < 64, a0, a1) / sums).astype(BF16)
        l0, l1 = a0[:, 64:65], a1[:, 0:1]
        lse = jnp.where(lane == 0, m0 + jnp.log2(l0), jnp.where(lane == 1, m1 + jnp.log2(l1), 0.0))
        lse_ref[0] = lse.T[0:SUBLANES, :]

    return pl.pallas_call(
        body, grid=(P, nq),
        in_specs=[pl.BlockSpec((tq, LANES), lambda p, i: (i, p)),
                  pl.BlockSpec((T, LANES), lambda p, i: (0, P + p)),
                  pl.BlockSpec((T, LANES), lambda p, i: (0, 2 * P + p)),
                  pl.BlockSpec((1, SUBLANES, T), lambda p, i: (p, 0, 0))] + [_hbm()] * n_r,
        out_specs=[pl.BlockSpec((tq, LANES), lambda p, i: (i, p)),
                   pl.BlockSpec((1, SUBLANES, tq), lambda p, i: (p, 0, i))] + [_hbm()] * n_r,
        out_shape=[jax.ShapeDtypeStruct((T, LANES * P), BF16), jax.ShapeDtypeStruct((P, SUBLANES, T), F32)]
        + [jax.ShapeDtypeStruct((4,) + w.shape, w.dtype) for w in riders],
        scratch_shapes=[pltpu.VMEM((4, tq, cw), F32)]
        + ([pltpu.SemaphoreType.DMA((3 * n_r,)), pltpu.SemaphoreType.DMA((3 * n_r,)),
            pltpu.SemaphoreType.DMA((n_r,))] if n_r else []),
        name=name, compiler_params=_cp(),
    )(qkv, qkv, qkv, cT, *riders)


def _attn_delta(do, o, P, name):
    T, D = o.shape
    tb = _tile(T, 256, LANES)

    def body(do_ref, o_ref, d_ref):
        lane = lax.broadcasted_iota(jnp.int32, (1, LANES), 1)
        for p in range(P):
            cols = slice(p * LANES, (p + 1) * LANES)
            prod = do_ref[:, cols].astype(F32) * o_ref[:, cols].astype(F32)
            d0 = jnp.sum(jnp.where(lane < 64, prod, 0.0), axis=1, keepdims=True)
            d1 = jnp.sum(jnp.where(lane >= 64, prod, 0.0), axis=1, keepdims=True)
            both = jnp.where(lane == 0, d0, jnp.where(lane == 1, d1, 0.0))
            d_ref[p] = both.T[0:SUBLANES, :]

    return pl.pallas_call(
        body, grid=(T // tb,),
        in_specs=[pl.BlockSpec((tb, D), lambda i: (i, 0)), pl.BlockSpec((tb, D), lambda i: (i, 0))],
        out_specs=pl.BlockSpec((P, SUBLANES, tb), lambda i: (0, 0, i)),
        out_shape=jax.ShapeDtypeStruct((P, SUBLANES, T), F32), name=name, compiler_params=_cp(),
    )(do, o)


def _attn_bwd(qkv, do, lseT, dT, c0T, c_cols, P, scale, name, riders=(), gather_riders=()):
    T = qkv.shape[0]
    n_x, n_g = len(riders), len(gather_riders)
    n_r = n_x + n_g
    tq = _q_block(T)
    tw = _tile(T, 4 * tq, 2 * tq)
    cw = tw // 2
    assert cw % tq == 0, "the sequence must split into chunks of whole query blocks"
    nq = T // tq

    def body(q_ref, do_ref, k_ref, v_ref, lse_ref, d_ref, c0_ref, cc_ref, *rest):
        p_refs, (dq_ref, dk_ref, dv_ref, dc_ref, drow_ref) = rest[:n_r], rest[n_r:n_r + 5]
        o_refs = rest[n_r + 5:2 * n_r + 5]
        dq_acc0, dq_acc1, s_scr = rest[2 * n_r + 5:2 * n_r + 8]
        j = pl.program_id(1)
        if n_r:
            local_sems, send_sems, recv_sems = rest[2 * n_r + 8:]

            def exchange(t, incoming):
                if t < n_x:
                    return _chip_exchange_copies(p_refs[t], o_refs[t], local_sems.at[t], send_sems, recv_sems,
                                                 3 * t, incoming)
                return _gather_all_copies(p_refs[t], o_refs[t], local_sems.at[t], send_sems, recv_sems,
                                          3 * n_x + 7 * (t - n_x), incoming)

            @pl.when((pl.program_id(0) == 0) & (j == 0))
            def _():
                for t in range(n_r):
                    own, sent = exchange(t, False)
                    own.start()
                    for cp in sent:
                        cp.start()

            @pl.when((pl.program_id(0) == P - 1) & (j == nq - 1))
            def _():
                for t in range(n_r):
                    for cp in exchange(t, True)[1]:
                        cp.wait_recv()
                    own, sent = exchange(t, False)
                    for cp in sent:
                        cp.wait_send()
                    own.wait()


        @pl.when(j == 0)
        def _():
            dq_acc0[...] = jnp.zeros_like(dq_acc0)
            dq_acc1[...] = jnp.zeros_like(dq_acc1)

        lane = lax.broadcasted_iota(jnp.int32, (1, LANES), 1)
        in_head = (lane < 64, lane >= 64)
        k = k_ref[...]
        v = v_ref[...]
        zero = jnp.zeros_like(k)
        one = jnp.ones_like(k)
        k_heads = tuple(jnp.where(h, k, zero) for h in in_head)
        v_heads = tuple(jnp.where(h, v, zero) for h in in_head)
        k_ones = tuple(jnp.where(h, k, one) for h in in_head)
        cc = cc_ref[0]
        c_first = (cc[0:1, 0:1], cc[0:1, 1:2])
        c_rel = ((cc[:, 0:1] - c_first[0]) * LOG2E, (cc[:, 1:2] - c_first[1]) * LOG2E)
        dq_accs = (dq_acc0, dq_acc1)

        def scaled_q(start, width, factor):
            return (q_ref[pl.ds(start, width), :].astype(F32) * factor).astype(BF16)

        def block(start, width, carry, first_query=None, scores=None):
            q = scaled_q(start, width, scale)
            q_one = jnp.ones_like(q)
            dov = do_ref[pl.ds(start, width), :]
            lse = lse_ref[0, :, pl.ds(start, width)]
            dlt = d_ref[0, :, pl.ds(start, width)]
            c0 = c0_ref[0, :, pl.ds(start, width)]
            new = []
            for a in range(2):
                dk_a, dv_a = carry[a]
                rowv = lse[a:a + 1, :] + (c_first[a] - c0[a:a + 1, :]) * LOG2E
                if scores is None:
                    st = _nt(k_heads[a], scaled_q(start, width, scale * LOG2E))
                else:
                    st = scores(a)
                pt = jnp.exp2((st - c_rel[a]) - rowv)
                if first_query is not None:
                    row = lax.broadcasted_iota(jnp.int32, (tq, width), 0)
                    col = lax.broadcasted_iota(jnp.int32, (tq, width), 1)
                    pt = jnp.where(col - row >= first_query, pt, 0.0)
                dpt = _nt(v_heads[a], dov)
                dst_b = (pt * (dpt - dlt[a:a + 1, :])).astype(BF16)
                dv_a = dv_a + jnp.dot(pt.astype(BF16), dov, preferred_element_type=F32)
                dk_a = dk_a + jnp.dot(dst_b, jnp.where(in_head[a], q, q_one), preferred_element_type=F32)
                dq_accs[a][pl.ds(start, width), :] += lax.dot_general(
                    dst_b, k_ones[a], (((0,), (0,)), ((), ())), preferred_element_type=F32)
                new.append((dk_a, dv_a))
            return tuple(new)

        first_key = j * tq
        first_wide = first_key // tw + 1
        last = T // tw - 1

        def fill(trip, buf):
            q = scaled_q(pl.multiple_of(jnp.minimum(trip, last) * tw, tw), tw, scale * LOG2E)
            for a in range(2):
                s_scr[2 * buf + a] = _nt(k_heads[a], q)

        def trip(i, buf, cr):
            return block(pl.multiple_of(i * tw, tw), tw, cr, scores=lambda a: s_scr[2 * buf + a])

        def two_trips(p, cr):
            i = first_wide + 2 * p
            fill(i + 1, 1)
            cr = trip(i, 0, cr)
            fill(i + 2, 0)
            return trip(i + 1, 1, cr)

        init = tuple((jnp.zeros((tq, LANES), F32), jnp.zeros((tq, LANES), F32)) for _ in range(2))
        fill(first_wide, 0)
        diag = pl.multiple_of((first_key // cw) * cw, cw)
        carry = block(diag, cw, init, first_key - diag)
        carry = lax.cond(
            diag + cw < first_wide * tw,
            lambda cr: block(pl.multiple_of(diag + cw, cw), cw, cr), lambda cr: cr, carry)
        n_trips = last + 1 - first_wide
        carry = lax.fori_loop(0, n_trips // 2, two_trips, carry)
        (dk0, dv0), (dk1, dv1) = lax.cond(n_trips % 2 == 1, lambda cr: trip(last, 0, cr), lambda cr: cr, carry)
        dk_ref[...] = jnp.where(lane < 64, dk0, dk1).astype(BF16)
        dv_ref[...] = jnp.where(lane < 64, dv0, dv1).astype(BF16)
        dc_ref[0] = jnp.where(lane == 0, -dk0[:, 64:65], jnp.where(lane == 1, -dk1[:, 0:1], 0.0))

        @pl.when(j == nq - 1)
        def _():
            def finish(i, _):
                rows = pl.ds(pl.multiple_of(i * tq, tq), tq)
                a0 = dq_acc0[rows, :]
                a1 = dq_acc1[rows, :]
                dq_ref[rows, :] = (jnp.where(lane < 64, a0, a1) * scale).astype(BF16)
                sums = jnp.where(lane == 0, a0[:, 64:65], jnp.where(lane == 1, a1[:, 0:1], 0.0))
                drow_ref[0, :, rows] = sums.T[0:SUBLANES, :]
                return 0

            lax.fori_loop(0, nq, finish, 0)

    full = lambda col: pl.BlockSpec((T, LANES), lambda p, j: (0, col(p)))
    blk = lambda col: pl.BlockSpec((tq, LANES), lambda p, j: (j, col(p)))
    rows = pl.BlockSpec((1, SUBLANES, T), lambda p, j: (p, 0, 0))
    cols = pl.BlockSpec((1, tq, LANES), lambda p, j: (p, j, 0))
    D = LANES * P
    return pl.pallas_call(
        body, grid=(P, nq),
        in_specs=[full(lambda p: p), full(lambda p: p), blk(lambda p: P + p), blk(lambda p: 2 * P + p),
                  rows, rows, rows, cols] + [_hbm()] * n_r,
        out_specs=[full(lambda p: p), blk(lambda p: p), blk(lambda p: p), cols, rows] + [_hbm()] * n_r,
        out_shape=[jax.ShapeDtypeStruct((T, D), BF16), jax.ShapeDtypeStruct((T, D), BF16),
                   jax.ShapeDtypeStruct((T, D), BF16), jax.ShapeDtypeStruct((P, T, LANES), F32),
                   jax.ShapeDtypeStruct((P, SUBLANES, T), F32)]
        + [jax.ShapeDtypeStruct(r.shape, r.dtype) for r in riders]
        + [jax.ShapeDtypeStruct((8,) + g.shape, g.dtype) for g in gather_riders],
        scratch_shapes=[pltpu.VMEM((T, LANES), F32), pltpu.VMEM((T, LANES), F32), pltpu.VMEM((4, tq, tw), F32)]
        + ([pltpu.SemaphoreType.DMA((n_r,)), pltpu.SemaphoreType.DMA((3 * n_x + 7 * n_g,)),
            pltpu.SemaphoreType.DMA((3 * n_x + 7 * n_g,))] if n_r else []),
        name=name, compiler_params=_cp(),
    )(qkv, do, qkv, qkv, lseT, dT, c0T, c_cols, *riders, *gather_riders)


_SQRT_HALF = 0.7071067811865476
_INV_SQRT_2PI = 0.3989422804014327


def _gelu(v):
    return 0.5 * v * (1.0 + lax.erf(v * _SQRT_HALF))


def _gelu_and_grad(v):
    cdf = 0.5 * (1.0 + lax.erf(v * _SQRT_HALF))
    return v * cdf, cdf + v * (_INV_SQRT_2PI * jnp.exp(-0.5 * v * v))


def _sgu_fwd(a, ln_g, ln_b, w_tril, bias, name):
    T, W2 = a.shape
    W = W2 // 2
    G = w_tril.shape[0]
    tb = _tile(T, 256, LANES)

    def body(a_ref, g_ref, b_ref, w_ref, bias_ref, out_ref):
        zu = _gelu(a_ref[:, :W].astype(F32))
        zv = _gelu(a_ref[:, W:].astype(F32))
        mu = jnp.mean(zv, axis=-1, keepdims=True)
        d = zv - mu
        rstd = lax.rsqrt(jnp.mean(d * d, axis=-1, keepdims=True) + LN_EPS)
        vn = (d * rstd * g_ref[...] + b_ref[...]).astype(BF16)
        for c in range(tb // LANES):
            rs = slice(c * LANES, (c + 1) * LANES)
            for g in range(G):
                cs = slice(g * LANES, (g + 1) * LANES)
                mixed = jnp.dot(w_ref[g], vn[rs, cs], preferred_element_type=F32) + bias_ref[:, cs]
                out_ref[rs, cs] = (zu[rs, cs] * mixed).astype(BF16)

    return pl.pallas_call(
        body, grid=(T // tb,),
        in_specs=[pl.BlockSpec((tb, W2), lambda i: (i, 0)), pl.BlockSpec((1, W), lambda i: (0, 0)),
                  pl.BlockSpec((1, W), lambda i: (0, 0)), pl.BlockSpec((G, LANES, LANES), lambda i: (0, 0, 0)),
                  pl.BlockSpec((LANES, W), lambda i: (0, 0))],
        out_specs=pl.BlockSpec((tb, W), lambda i: (i, 0)),
        out_shape=jax.ShapeDtypeStruct((T, W), BF16), name=name, compiler_params=_cp(),
    )(a, ln_g.reshape(1, W), ln_b.reshape(1, W), w_tril, bias)


def _sgu_bwd(a, dgated, ln_g, ln_b, w_tril, w_tril_t, bias, name):
    T, W2 = a.shape
    W = W2 // 2
    G = w_tril.shape[0]
    tb = _tile(T, 256, LANES)

    def body(a_ref, dg_ref, g_ref, b_ref, w_ref, wt_ref, bias_ref,
             da_ref, dws_ref, dbias_ref, dlng_ref, dlnb_ref, dvn_ref):
        @pl.when(pl.program_id(0) == 0)
        def _():
            dws_ref[...] = jnp.zeros_like(dws_ref)
            dbias_ref[...] = jnp.zeros_like(dbias_ref)
            dlng_ref[...] = jnp.zeros_like(dlng_ref)
            dlnb_ref[...] = jnp.zeros_like(dlnb_ref)

        up = a_ref[:, :W].astype(F32)
        vp = a_ref[:, W:].astype(F32)
        zu, gu = _gelu_and_grad(up)
        zv, gv = _gelu_and_grad(vp)
        mu = jnp.mean(zv, axis=-1, keepdims=True)
        d = zv - mu
        rstd = lax.rsqrt(jnp.mean(d * d, axis=-1, keepdims=True) + LN_EPS)
        vhat = d * rstd
        gam = g_ref[...]
        vn = (vhat * gam + b_ref[...]).astype(BF16)
        dgated = dg_ref[...]
        for c in range(tb // LANES):
            rs = slice(c * LANES, (c + 1) * LANES)
            for g in range(G):
                cs = slice(g * LANES, (g + 1) * LANES)
                vb = vn[rs, cs]
                mixed = jnp.dot(w_ref[g], vb, preferred_element_type=F32) + bias_ref[:, cs]
                dgt = dgated[rs, cs]
                da_ref[rs, cs] = (dgt * mixed * gu[rs, cs]).astype(BF16)
                dmx = dgt * zu[rs, cs]
                dbias_ref[:, cs] += dmx
                dmb = dmx.astype(BF16)
                dws_ref[g] += _nt(dmb, vb)
                dvn_ref[rs, cs] = jnp.dot(wt_ref[g], dmb, preferred_element_type=F32)
        dvn = dvn_ref[...]
        dlng_ref[...] += jnp.sum(dvn * vhat, axis=0, keepdims=True)
        dlnb_ref[...] += jnp.sum(dvn, axis=0, keepdims=True)
        dvh = dvn * gam
        dzv = rstd * (dvh - jnp.mean(dvh, axis=-1, keepdims=True)
                      - vhat * jnp.mean(dvh * vhat, axis=-1, keepdims=True))
        da_ref[:, W:] = (dzv * gv).astype(BF16)

    const2 = lambda shape: pl.BlockSpec(shape, lambda i: (0, 0))
    const3 = pl.BlockSpec((G, LANES, LANES), lambda i: (0, 0, 0))
    return pl.pallas_call(
        body, grid=(T // tb,),
        in_specs=[pl.BlockSpec((tb, W2), lambda i: (i, 0)), pl.BlockSpec((tb, W), lambda i: (i, 0)),
                  const2((1, W)), const2((1, W)), const3, const3, const2((LANES, W))],
        out_specs=[pl.BlockSpec((tb, W2), lambda i: (i, 0)), const3, const2((LANES, W)),
                   const2((SUBLANES, W)), const2((SUBLANES, W))],
        out_shape=[jax.ShapeDtypeStruct((T, W2), BF16), jax.ShapeDtypeStruct((G, LANES, LANES), F32),
                   jax.ShapeDtypeStruct((LANES, W), F32), jax.ShapeDtypeStruct((SUBLANES, W), F32),
                   jax.ShapeDtypeStruct((SUBLANES, W), F32)],
        scratch_shapes=[pltpu.VMEM((tb, W), F32)],
        name=name, compiler_params=_cp(),
    )(a, dgated, ln_g.reshape(1, W), ln_b.reshape(1, W), w_tril, w_tril_t, bias)


def _adam_math(w, g, m, v):
    m = ADAM_B1 * m + (1.0 - ADAM_B1) * g
    v = ADAM_B2 * v + (1.0 - ADAM_B2) * (g * g)
    m_hat = m / (1.0 - ADAM_B1 ** ADAM_STEP)
    v_hat = v / (1.0 - ADAM_B2 ** ADAM_STEP)
    delta = -ADAM_LR * (m_hat / (jnp.sqrt(v_hat) + ADAM_EPS) + ADAM_WD * w)
    return delta, m, v


def _adamw_halves(mine, theirs, c_idx, w, m, v, name):
    R, C = w.shape
    rh = R // 2
    tb = _row_tile(rh, C)
    nb = rh // tb

    def body(c_ref, a_ref, b_ref, w_ref, m_ref, v_ref, g_ref, d_ref, mo_ref, vo_ref):
        g = jnp.where(pl.program_id(0) == c_ref[0], a_ref[...], b_ref[...])
        d, mm, vv = _adam_math(w_ref[...], g, m_ref[...], v_ref[...])
        g_ref[...] = g
        d_ref[...] = d
        mo_ref[...] = mm
        vo_ref[...] = vv

    half = pl.BlockSpec((tb, C), lambda h, i, c: (i, 0))
    row = pl.BlockSpec((tb, C), lambda h, i, c: (h * nb + i, 0))
    sds = jax.ShapeDtypeStruct((R, C), F32)
    return pl.pallas_call(
        body,
        grid_spec=pltpu.PrefetchScalarGridSpec(
            num_scalar_prefetch=1, grid=(2, nb), in_specs=[half, half, row, row, row], out_specs=[row] * 4),
        out_shape=[sds] * 4, name=name, compiler_params=_cp())(c_idx, mine, theirs, w, m, v)


def _adamw_sum(parts, w, m, v, name):
    K, R, C = parts.shape
    tb = _tile(R, 128, SUBLANES)

    def body(p_ref, w_ref, m_ref, v_ref, g_ref, d_ref, mo_ref, vo_ref):
        g = p_ref[0]
        for k in range(1, K):
            g = g + p_ref[k]
        d, mm, vv = _adam_math(w_ref[...], g, m_ref[...], v_ref[...])
        g_ref[...] = g
        d_ref[...] = d
        mo_ref[...] = mm
        vo_ref[...] = vv

    row = pl.BlockSpec((tb, C), lambda i: (i, 0))
    sds = jax.ShapeDtypeStruct((R, C), F32)
    return pl.pallas_call(
        body, grid=(R // tb,),
        in_specs=[pl.BlockSpec((K, tb, C), lambda i: (0, i, 0)), row, row, row],
        out_specs=[row] * 4, out_shape=[sds] * 4, name=name, compiler_params=_cp())(parts, w, m, v)


def _pair_sum(g_all, recv, c_idx, name):
    K, R, C = g_all.shape
    rh = R // 2
    tb = _row_tile(rh, C)
    nb = rh // tb

    def body(c_ref, a_ref, b_ref, o_ref):
        o_ref[...] = (a_ref[...] + b_ref[...]).astype(BF16)

    return pl.pallas_call(
        body,
        grid_spec=pltpu.PrefetchScalarGridSpec(
            num_scalar_prefetch=1, grid=(K, nb),
            in_specs=[pl.BlockSpec((1, tb, C), lambda k, i, c: (k, c[0] * nb + i, 0)),
                      pl.BlockSpec((1, tb, C), lambda k, i, c: (k, i, 0))],
            out_specs=pl.BlockSpec((1, tb, C), lambda k, i, c: (k, i, 0))),
        out_shape=jax.ShapeDtypeStruct((K, rh, C), BF16), name=name, compiler_params=_cp(),
    )(c_idx, g_all, recv)


def _sum_parts(parts, name):
    K, R, C = parts.shape
    tb = _row_tile(R, C)

    def body(p_ref, o_ref):
        g = p_ref[0].astype(F32)
        for k in range(1, K):
            g = g + p_ref[k].astype(F32)
        o_ref[...] = g

    return pl.pallas_call(
        body, grid=(R // tb,), in_specs=[pl.BlockSpec((K, tb, C), lambda i: (0, i, 0))],
        out_specs=pl.BlockSpec((tb, C), lambda i: (i, 0)),
        out_shape=jax.ShapeDtypeStruct((R, C), F32), name=name, compiler_params=_cp())(parts)


_CHIP_RELATIONS = ((1, 0), (0, 1), (1, 1))


def _position():
    return lax.axis_index("x"), lax.axis_index("y"), lax.axis_index("c")


def _flip(v, bit):
    return 1 - v if bit else v


def _shard_half_copies(w_ref, ow_ref, send_sems, recv_sems, sem0, incoming):
    x, y, c = _position()
    rh = w_ref.shape[0] // 2
    rows = pl.ds(pl.multiple_of(c * rh, 16), rh)
    out = []
    for r, (dx, dy) in enumerate(_CHIP_RELATIONS):
        px, py = _flip(x, dx), _flip(y, dy)
        slot = 2 * px + py if incoming else 2 * x + y
        out.append(pltpu.make_async_remote_copy(
            src_ref=w_ref.at[rows, :], dst_ref=ow_ref.at[slot, rows, :], send_sem=send_sems.at[sem0 + r],
            recv_sem=recv_sems.at[sem0 + r], device_id=(px, py, c), device_id_type=MESH))
    return out


def _gather_weights(w_pack, side, name):
    n_side = 0 if side is None else 1

    def between_chips(*refs):
        if n_side:
            w_ref, s_ref, ow_ref, os_ref, local_sem, send_sems, recv_sems = refs
        else:
            w_ref, ow_ref, send_sems, recv_sems = refs
        x, y, c = _position()
        me = 2 * x + y
        if n_side:
            own_side = pltpu.make_async_copy(s_ref, os_ref.at[me], local_sem)
            own_side.start()

        def side_copies(incoming):
            out = []
            for r, (dx, dy) in enumerate(_CHIP_RELATIONS):
                px, py = _flip(x, dx), _flip(y, dy)
                out.append(pltpu.make_async_remote_copy(
                    src_ref=s_ref, dst_ref=os_ref.at[2 * px + py if incoming else me],
                    send_sem=send_sems.at[3 + r], recv_sem=recv_sems.at[3 + r],
                    device_id=(px, py, c), device_id_type=MESH))
            return out

        sent = _shard_half_copies(w_ref, ow_ref, send_sems, recv_sems, 0, False) + (side_copies(False) if n_side else [])
        for cp in sent:
            cp.start()
        for cp in _shard_half_copies(w_ref, ow_ref, send_sems, recv_sems, 0, True) + (side_copies(True) if n_side else []):
            cp.wait_recv()
        for cp in sent:
            cp.wait_send()
        if n_side:
            own_side.wait()

    sems = [pltpu.SemaphoreType.DMA((6,)), pltpu.SemaphoreType.DMA((6,))]
    gathered = jax.ShapeDtypeStruct((4,) + w_pack.shape, w_pack.dtype)
    if n_side:
        halves, sides = pl.pallas_call(
            between_chips, in_specs=[_hbm(), _hbm()], out_specs=[_hbm(), _hbm()],
            out_shape=[gathered, jax.ShapeDtypeStruct((4,) + side.shape, side.dtype)],
            scratch_shapes=[pltpu.SemaphoreType.DMA(())] + sems,
            name=name + "_ici", compiler_params=_cp(),
        )(w_pack, side)
    else:
        sides = None
        halves = pl.pallas_call(
            between_chips, in_specs=[_hbm()], out_specs=_hbm(), out_shape=gathered, scratch_shapes=sems,
            name=name + "_ici", compiler_params=_cp(),
        )(w_pack)
    return _hand_to_sibling(halves, name + "_d2d"), sides


def _hand_to_sibling(halves, name):
    _, R, C = halves.shape
    rh = R // 2

    def to_sibling(g_ref, o_ref, send_sems, recv_sems):
        x, y, c = _position()

        def copy(r, cc):
            dx, dy = _CHIP_RELATIONS[r]
            slot = 2 * _flip(x, dx) + _flip(y, dy)
            rows = pl.ds(pl.multiple_of(cc * rh, 16), rh)
            return pltpu.make_async_remote_copy(
                src_ref=g_ref.at[slot, rows, :], dst_ref=o_ref.at[slot, rows, :], send_sem=send_sems.at[r],
                recv_sem=recv_sems.at[r], device_id=(x, y, 1 - c), device_id_type=MESH)

        sent = [copy(r, c) for r in range(3)]
        for cp in sent:
            cp.start()
        for r in range(3):
            copy(r, 1 - c).wait_recv()
        for cp in sent:
            cp.wait_send()

    return pl.pallas_call(
        to_sibling, in_specs=[_hbm()], out_specs=_hbm(), input_output_aliases={0: 0},
        out_shape=jax.ShapeDtypeStruct(halves.shape, halves.dtype),
        scratch_shapes=[pltpu.SemaphoreType.DMA((3,)), pltpu.SemaphoreType.DMA((3,))],
        name=name, compiler_params=_cp(),
    )(halves)


def _sibling_halves(g_all, name):
    K, R, C = g_all.shape
    rh = R // 2

    def body(g_ref, o_ref, send_sem, recv_sem):
        x, y, c = _position()
        start = pl.multiple_of((1 - c) * rh, SUBLANES)
        cp = pltpu.make_async_remote_copy(
            src_ref=g_ref.at[:, pl.ds(start, rh), :], dst_ref=o_ref, send_sem=send_sem, recv_sem=recv_sem,
            device_id=(x, y, 1 - c), device_id_type=MESH)
        cp.start()
        cp.wait_recv()
        cp.wait_send()

    return pl.pallas_call(
        body, in_specs=[_hbm()], out_specs=_hbm(),
        out_shape=jax.ShapeDtypeStruct((K, rh, C), F32),
        scratch_shapes=[pltpu.SemaphoreType.DMA(()), pltpu.SemaphoreType.DMA(())],
        name=name, compiler_params=_cp(),
    )(g_all)


def _chip_exchange_copies(p_ref, o_ref, local_sem, send_sems, recv_sems, sem0, incoming):
    x, y, c = _position()
    me = 2 * x + y
    out = []
    for r, (dx, dy) in enumerate(_CHIP_RELATIONS):
        px, py = _flip(x, dx), _flip(y, dy)
        src_slot, dst_slot = (me, 2 * px + py) if incoming else (2 * px + py, me)
        out.append(pltpu.make_async_remote_copy(
            src_ref=p_ref.at[src_slot], dst_ref=o_ref.at[dst_slot], send_sem=send_sems.at[sem0 + r],
            recv_sem=recv_sems.at[sem0 + r], device_id=(px, py, c), device_id_type=MESH))
    return pltpu.make_async_copy(p_ref.at[me], o_ref.at[me], local_sem), out


def _chip_exchange(parts, name):
    def body(p_ref, o_ref, local_sem, send_sems, recv_sems):
        own, sent = _chip_exchange_copies(p_ref, o_ref, local_sem, send_sems, recv_sems, 0, False)
        own.start()
        for cp in sent:
            cp.start()
        for cp in _chip_exchange_copies(p_ref, o_ref, local_sem, send_sems, recv_sems, 0, True)[1]:
            cp.wait_recv()
        for cp in sent:
            cp.wait_send()
        own.wait()

    return pl.pallas_call(
        body, in_specs=[_hbm()], out_specs=_hbm(),
        out_shape=jax.ShapeDtypeStruct(parts.shape, parts.dtype),
        scratch_shapes=[pltpu.SemaphoreType.DMA(()), pltpu.SemaphoreType.DMA((3,)),
                        pltpu.SemaphoreType.DMA((3,))],
        name=name, compiler_params=_cp(),
    )(parts)


def _swap_with_sibling(half, name):
    rh, C = half.shape

    def body(h_ref, o_ref, send_sem, recv_sem):
        x, y, c = _position()
        cp = pltpu.make_async_remote_copy(
            src_ref=h_ref, dst_ref=o_ref, send_sem=send_sem, recv_sem=recv_sem,
            device_id=(x, y, 1 - c), device_id_type=MESH)
        cp.start()
        cp.wait_recv()
        cp.wait_send()

    return pl.pallas_call(
        body, in_specs=[_hbm()], out_specs=_hbm(),
        out_shape=jax.ShapeDtypeStruct((rh, C), F32),
        scratch_shapes=[pltpu.SemaphoreType.DMA(()), pltpu.SemaphoreType.DMA(())],
        name=name, compiler_params=_cp(),
    )(half)


_DEVICE_RELATIONS = tuple((b >> 2 & 1, b >> 1 & 1, b & 1) for b in range(1, 8))


def _gather_all_copies(p_ref, o_ref, local_sem, send_sems, recv_sems, sem0, incoming):
    x, y, c = _position()
    me = 4 * x + 2 * y + c
    out = []
    for r, (dx, dy, dc) in enumerate(_DEVICE_RELATIONS):
        px, py, pc = _flip(x, dx), _flip(y, dy), _flip(c, dc)
        slot = 4 * px + 2 * py + pc if incoming else me
        out.append(pltpu.make_async_remote_copy(
            src_ref=p_ref, dst_ref=o_ref.at[slot], send_sem=send_sems.at[sem0 + r], recv_sem=recv_sems.at[sem0 + r],
            device_id=(px, py, pc), device_id_type=MESH))
    return pltpu.make_async_copy(p_ref, o_ref.at[me], local_sem), out


def _gather_all(part, name):
    def body(p_ref, o_ref, local_sem, send_sems, recv_sems):
        own, sent = _gather_all_copies(p_ref, o_ref, local_sem, send_sems, recv_sems, 0, False)
        own.start()
        for cp in sent:
            cp.start()
        for cp in _gather_all_copies(p_ref, o_ref, local_sem, send_sems, recv_sems, 0, True)[1]:
            cp.wait_recv()
        for cp in sent:
            cp.wait_send()
        own.wait()

    return pl.pallas_call(
        body, in_specs=[_hbm()], out_specs=_hbm(),
        out_shape=jax.ShapeDtypeStruct((8,) + part.shape, part.dtype),
        scratch_shapes=[pltpu.SemaphoreType.DMA(()), pltpu.SemaphoreType.DMA((7,)),
                        pltpu.SemaphoreType.DMA((7,))],
        name=name, compiler_params=_cp(),
    )(part)


def _pack(arrs, row_mult, cols=PACK_COLS, lead=0):
    head = arrs[0].shape[:lead]
    pieces = []
    for a in arrs:
        flat = a.astype(F32).reshape(head + (-1,))
        fill = -flat.shape[-1] % cols
        if fill:
            flat = jnp.concatenate([flat, jnp.zeros(head + (fill,), F32)], axis=-1)
        pieces.append(flat.reshape(head + (-1, cols)))
    rows = sum(p.shape[lead] for p in pieces)
    fill = -rows % row_mult
    if fill:
        pieces.append(jnp.zeros(head + (fill, cols), F32))
    return jnp.concatenate(pieces, axis=lead) if len(pieces) > 1 else pieces[0]


def _unpack(buf, shapes):
    lead = buf.shape[:-2]
    cols = buf.shape[-1]
    out, off = [], 0
    for shp in shapes:
        n = math.prod(shp)
        rows = -(-n // cols)
        piece = buf[..., off:off + rows, :]
        if rows * cols != n:
            piece = piece.reshape(lead + (-1,))[..., :n]
        out.append(piece.reshape(lead + tuple(shp)))
        off += rows
    return out


def _cols_from_chips(g):
    k, L, A, n = g.shape
    return jnp.transpose(g, (1, 2, 0, 3)).reshape(L, A, k * n)


def _rows_from_chips(g):
    k, L, n, B = g.shape
    return jnp.transpose(g, (1, 0, 2, 3)).reshape(L, k * n, B)


def _cols_to_chips(full, k=4):
    L, A, N = full.shape
    return jnp.transpose(full.reshape(L, A, k, N // k), (2, 0, 1, 3))


def _rows_to_chips(full, k=4):
    L, N, B = full.shape
    return jnp.transpose(full.reshape(L, k, N // k, B), (1, 0, 2, 3))


def kernel(x, mixer_norm_w, attn_w_in, attn_b_f, attn_w_out, sgu_w_in, sgu_ln_g, sgu_ln_b, sgu_w_s, sgu_b_s, sgu_w_out, ffn_norm_w, ffn_w_in, ffn_w_out, final_norm_w, loss_target, m_mixer_norm_w, m_attn_w_in, m_attn_b_f, m_attn_w_out, m_sgu_w_in, m_sgu_ln_g, m_sgu_ln_b, m_sgu_w_s, m_sgu_b_s, m_sgu_w_out, m_ffn_norm_w, m_ffn_w_in, m_ffn_w_out, m_final_norm_w, v_mixer_norm_w, v_attn_w_in, v_attn_b_f, v_attn_w_out, v_sgu_w_in, v_sgu_ln_g, v_sgu_ln_b, v_sgu_w_s, v_sgu_b_s, v_sgu_w_out, v_ffn_norm_w, v_ffn_w_in, v_ffn_w_out, v_final_norm_w):
    T, D = x.shape[1], x.shape[2]
    depth = mixer_norm_w.shape[0]
    H = attn_b_f.shape[1]
    P = D // LANES
    assert D % LANES == 0 and D // H == 64 and 2 * P == H and 2 * P <= LANES
    G = sgu_w_s.shape[1]
    W = sgu_w_out.shape[1] * 4
    assert sgu_w_s.shape[2] == LANES and W == G * LANES
    scale = float(D // H) ** -0.5
    f_pad = LANES
    c_idx = lax.axis_index("c").astype(jnp.int32).reshape(1)

    groups = [
        ([attn_w_out, sgu_w_in, sgu_w_out, ffn_w_out, sgu_ln_g, sgu_ln_b],
         [m_attn_w_out, m_sgu_w_in, m_sgu_w_out, m_ffn_w_out, m_sgu_ln_g, m_sgu_ln_b],
         [v_attn_w_out, v_sgu_w_in, v_sgu_w_out, v_ffn_w_out, v_sgu_ln_g, v_sgu_ln_b]),
        ([ffn_w_in], [m_ffn_w_in], [v_ffn_w_in]),
        ([attn_w_in], [m_attn_w_in], [v_attn_w_in]),
    ]
    group_cols = [D, ffn_w_in.shape[2], attn_w_in.shape[2]]
    group_shapes = [[a.shape for a in g[0]] for g in groups]
    w_packs = [_pack(g[0], 512, cols) for g, cols in zip(groups, group_cols)]
    ln_pack = _pack([sgu_ln_g, sgu_ln_b], SUBLANES)
    my_chip = 2 * lax.axis_index("x") + lax.axis_index("y")
    w_packs_b = [w.astype(BF16) for w in w_packs]

    def finish_gather(t, gat, own_slot_filled=False):
        if not own_slot_filled:
            gat = lax.dynamic_update_index_in_dim(gat, w_packs_b[t], my_chip, 0)
        return _unpack(gat, group_shapes[t])

    gat, gat_ln = _gather_weights(w_packs_b[2], ln_pack, "gather_weights_2")
    (g_ai,) = finish_gather(2, gat)
    g_lng, g_lnb = _unpack(gat_ln, [sgu_ln_g.shape, sgu_ln_b.shape])
    w_ai = _cols_from_chips(g_ai)
    w_ai = jnp.pad(w_ai, ((0, 0), (0, 0), (0, 3 * D + f_pad - w_ai.shape[2])))
    b_f_pad = jnp.pad(attn_b_f, ((0, 0), (0, LANES - H)))
    xs = x.reshape(T, D)
    h, qkv, f = _norm_mm(xs, mixer_norm_w[0], w_ai[0], ((3 * D, BF16), (f_pad, F32)), "attn_qkv_0")
    cT, c_cols, c0T = _gate_fwd(f, b_f_pad[0:1], P, "gate_fwd_0")
    o, lseT, halves_0, halves_1 = _attn_fwd(qkv, cT, P, scale, "attn_fwd_0", riders=(w_packs_b[0], w_packs_b[1]))
    first_attention = dict(h=h, qkv=qkv, f=f, c0T=c0T, c_cols=c_cols, o=o, lseT=lseT)
    g_ao, g_si, g_so, g_fo, _, _ = finish_gather(0, _hand_to_sibling(halves_0, "gather_weights_0_d2d"), True)
    (g_fi,) = finish_gather(1, _hand_to_sibling(halves_1, "gather_weights_1_d2d"), True)
    w_ao = _rows_from_chips(g_ao)
    w_si = _cols_from_chips(g_si)
    w_so = _rows_from_chips(g_so)
    w_fo = _rows_from_chips(g_fo)
    w_fi5 = g_fi.reshape((2, 2) + g_fi.shape[1:])
    ln_g = jnp.transpose(g_lng, (1, 0, 2)).reshape(sgu_ln_g.shape[0], W)
    ln_b = jnp.transpose(g_lnb, (1, 0, 2)).reshape(sgu_ln_b.shape[0], W)
    w_tril = jnp.tril(sgu_w_s)
    w_tril_b = w_tril.astype(BF16)
    w_tril_tb = jnp.swapaxes(w_tril, 2, 3).astype(BF16)
    sgu_bias = jnp.repeat(jnp.swapaxes(sgu_b_s, 1, 2), LANES, axis=2)

    saved = []
    for i in range(depth):
        j = i // 2
        rec = {"x_in": xs}
        if i == 0:
            rec.update(first_attention)
            h, o = rec["h"], rec["o"]
            x_mid = _mm(o, w_ao[j], "nn", F32, f"attn_out_{i}", res=xs)
        elif i % 2 == 0:
            h, qkv, f = _norm_mm(xs, mixer_norm_w[i], w_ai[j], ((3 * D, BF16), (f_pad, F32)), f"attn_qkv_{i}")
            cT, c_cols, c0T = _gate_fwd(f, b_f_pad[j:j + 1], P, f"gate_fwd_{i}")
            o, lseT = _attn_fwd(qkv, cT, P, scale, f"attn_fwd_{i}")
            x_mid = _mm(o, w_ao[j], "nn", F32, f"attn_out_{i}", res=xs)
            rec.update(qkv=qkv, f=f, c0T=c0T, c_cols=c_cols, o=o, lseT=lseT)
        else:
            h, a = _norm_mm(xs, mixer_norm_w[i], w_si[j], ((2 * W, BF16),), f"sgu_in_{i}")
            gated = _sgu_fwd(a, ln_g[j], ln_b[j], w_tril_b[j], sgu_bias[j], f"sgu_fwd_{i}")
            x_mid = _mm(gated, w_so[j], "nn", F32, f"sgu_out_{i}", res=xs)
            rec.update(a=a, gated=gated)
        h2, fa, s = _ffn_in_act(x_mid, ffn_norm_w[i], w_fi5, i, f"ffn_in_{i}")
        xs = _mm(s, w_fo[i], "nn", F32, f"ffn_out_{i}", res=x_mid)
        rec.update(h=h, x_mid=x_mid, h2=h2, fa=fa, s=s)
        saved.append(rec)

    gx, loss_acc, dw_final = _loss_head(xs, final_norm_w, loss_target.reshape(T, D), "loss_head")
    loss = lax.psum(loss_acc[0, 0], ("x", "y", "c"))

    n_attn, n_sgu = attn_w_in.shape[0], sgu_w_in.shape[0]
    d_mixer_norm, d_ffn_norm = [None] * depth, [None] * depth
    d_ai, d_ao, d_bf = [None] * n_attn, [None] * n_attn, [None] * n_attn
    d_si, d_so, d_lng, d_lnb, d_ws, d_bs = ([None] * n_sgu for _ in range(6))
    d_fi, d_fo = [None] * depth, [None] * depth

    def pair_sums(t):
        if t == 0:
            grads = [_rows_to_chips(jnp.stack(d_ao)), _cols_to_chips(jnp.stack(d_si)), _rows_to_chips(jnp.stack(d_so)),
                     _rows_to_chips(jnp.stack(d_fo)),
                     jnp.transpose(jnp.stack(d_lng).reshape(n_sgu, 4, W // 4), (1, 0, 2)),
                     jnp.transpose(jnp.stack(d_lnb).reshape(n_sgu, 4, W // 4), (1, 0, 2))]
        elif t == 1:
            grads = [jnp.stack(d_fi, axis=1)]
        else:
            grads = [_cols_to_chips(jnp.stack(d_ai))]
        g_all = _pack(grads, 512, group_cols[t], lead=1)
        from_sibling = _sibling_halves(g_all, f"grad_sibling_halves_{t}")
        return _pair_sum(g_all, from_sibling, c_idx, f"grad_pair_sum_{t}")

    for i in reversed(range(depth)):
        j = i // 2
        rec = saved[i]
        d_fo[i] = _mm(rec["s"], gx, "tn", F32, f"ffn_out_wgrad_{i}")
        da = _ffn_out_bwd_act(gx, w_fo[i], rec["fa"], f"ffn_out_bwd_{i}")
        da = da.reshape((4,) + da.shape[2:])
        d_fi[i] = _mm_tn_shards(rec["h2"], da, f"ffn_in_wgrad_{i}")
        gx, dwn = _nt_norm_bwd(da, g_fi, i, rec["x_mid"], ffn_norm_w[i], gx, f"ffn_in_bwd_{i}")
        d_ffn_norm[i] = dwn[0]
        if i % 2 == 0:
            do = _mm(gx, w_ao[j], "nt", BF16, f"attn_out_bwd_{i}")
            d_ao[j] = _mm(rec["o"], gx, "tn", F32, f"attn_out_wgrad_{i}")
            dT = _attn_delta(do, rec["o"], P, f"attn_delta_{i}")
            riders, gather_riders = (), ()
            if i == 0:
                riders = tuple(pair_sums(t) for t in range(2))
                gather_riders = (_pack([jnp.stack(d_ws), jnp.stack(d_bs)], SUBLANES),)
            dq, dk, dv, dc_cols, drowT, *early = _attn_bwd(
                rec["qkv"], do, rec["lseT"], dT, rec["c0T"], rec["c_cols"], P, scale, f"attn_bwd_{i}",
                riders=riders, gather_riders=gather_riders)
            df, dbf = _gate_bwd(dc_cols, drowT, rec["f"], b_f_pad[j:j + 1], P, f"gate_bwd_{i}")
            d_bf[j] = dbf[0, :H]
            dproj = jnp.concatenate([dq, dk, dv, df.astype(BF16)], axis=1)
            d_ai[j] = _mm(rec["h"], dproj, "tn", F32, f"attn_in_wgrad_{i}")[:, :3 * D + H]
            dmix, w_mix = dproj, w_ai
        else:
            dgated = _mm(gx, w_so[j], "nt", F32, f"sgu_out_bwd_{i}")
            d_so[j] = _mm(rec["gated"], gx, "tn", F32, f"sgu_out_wgrad_{i}")
            da_s, dws, dbias, dlng, dlnb = _sgu_bwd(rec["a"], dgated, ln_g[j], ln_b[j], w_tril_b[j],
                                                    w_tril_tb[j], sgu_bias[j], f"sgu_bwd_{i}")
            d_ws[j] = jnp.tril(dws)
            d_bs[j] = jnp.sum(dbias.reshape(LANES, G, LANES), axis=2).T
            d_lng[j], d_lnb[j] = dlng[0], dlnb[0]
            d_si[j] = _mm(rec["h"], da_s, "tn", F32, f"sgu_in_wgrad_{i}")
            dmix, w_mix = da_s, w_si
        gx, dwn = _nt_norm_bwd(dmix[None], w_mix[None], j, rec["x_in"], mixer_norm_w[i], gx, f"mixer_in_bwd_{i}")
        d_mixer_norm[i] = dwn[0]
    grad_x = gx.reshape(x.shape)

    reduced = []
    for t, cols in enumerate(group_cols):
        from_chips = early[t] if t < 2 else _chip_exchange(pair_sums(t), f"grad_chip_exchange_{t}")
        my_half = _sum_parts(from_chips, f"grad_chip_sum_{t}")
        sibling_half = _swap_with_sibling(my_half, f"grad_swap_halves_{t}")
        packs = _adamw_halves(my_half, sibling_half, c_idx, w_packs[t], _pack(groups[t][1], 512, cols),
                              _pack(groups[t][2], 512, cols), f"adamw_sharded_{t}")
        reduced.append([_unpack(p, group_shapes[t]) for p in packs])

    def sharded_outputs(which):
        (ao, si, so, fo, lng, lnb), (fi,), (ai,) = (reduced[t][which] for t in range(3))
        return [ai, ao, si, so, fi, fo, lng, lnb]

    g_sh, d_sh, m_sh, v_sh = (sharded_outputs(w) for w in range(4))

    repl_groups = [
        ([sgu_w_s, sgu_b_s], [m_sgu_w_s, m_sgu_b_s], [v_sgu_w_s, v_sgu_b_s], early[2], "sgu"),
        ([mixer_norm_w, attn_b_f, ffn_norm_w, final_norm_w],
         [m_mixer_norm_w, m_attn_b_f, m_ffn_norm_w, m_final_norm_w],
         [v_mixer_norm_w, v_attn_b_f, v_ffn_norm_w, v_final_norm_w],
         _gather_all(_pack([jnp.stack(d_mixer_norm), jnp.stack(d_bf), jnp.stack(d_ffn_norm), dw_final[0]], SUBLANES),
                     "grad_gather_replicated"), "rest"),
    ]
    rep = []
    for ws, ms, vs, parts, tag in repl_groups:
        packs = _adamw_sum(parts, _pack(ws, SUBLANES), _pack(ms, SUBLANES), _pack(vs, SUBLANES),
                           f"adamw_replicated_{tag}")
        rep.append([_unpack(p, [a.shape for a in ws]) for p in packs])

    def replicated_outputs(which):
        (ws_, bs_), (mn, bf, fn, fin) = rep[0][which], rep[1][which]
        return [mn, bf, ws_, bs_, fn, fin]

    g_r, d_r, m_r, v_r = (replicated_outputs(w) for w in range(4))

    def ordered(sh, rp):
        ai, ao, si, so, fi, fo, lng, lnb = sh
        mn, bf, ws, bs, fn, fin = rp
        return [mn, ai, bf, ao, si, lng, lnb, ws, bs, so, fn, fi, fo, fin]

    return (loss, grad_x, *ordered(g_sh, g_r), *ordered(d_sh, d_r), *ordered(m_sh, m_r), *ordered(v_sh, v_r))
```
